```python
import jax, jax.numpy as jnp
from jax import lax
import numpy as np

D_MODEL = 1024
BATCH = 2
SEQ = 8192
DEPTH = 4

GRID_W = 64
CTX_LEN = 256
N_MIXERS = 3
N_A = (DEPTH + 2) // 3
N_B = (DEPTH + 1) // 3
N_C = DEPTH // 3
NORM_EPS = 1e-6

CHUNK_A = 128
A_INNER = D_MODEL
A_GROUPS = 8
A_GROUP_W = A_INNER // A_GROUPS

GLA_HEADS = 4
GLA_KEY = D_MODEL // 2
GLA_VAL = D_MODEL
GLA_HK = GLA_KEY // GLA_HEADS
GLA_HV = GLA_VAL // GLA_HEADS
GLA_RANK = 16
GLA_TAU = 16.0
GLA_CHUNK = 64
ROPE_BASE = 10000.0

NA_HEADS = 16
NA_HD = D_MODEL // NA_HEADS
NA_KH = 8
NA_KW = 16
NA_QCOLS = 16
NA_BAND = 32
NEG_INF = -1e30

N_EXPERTS = 32
TOP_K = 4
D_EXPERT = D_MODEL
SWIGLU_LIMIT = 7.0
SWIGLU_ALPHA = 1.702
MOE_BLOCK = 256

kernel_name = "hybrid_gmlp_gla_natten_moe_dit"

F32 = jnp.float32


def rmsnorm(x, g):
    xf = x.astype(F32)
    y = xf * lax.rsqrt(jnp.mean(xf * xf, axis=-1, keepdims=True) + NORM_EPS)
    return (y * g.astype(F32)).astype(x.dtype)


def modulate(x, g, shift, scale):
    return rmsnorm(x, g) * (1 + scale) + shift


def chunk_mlp(h, w_in, g_v, w_s, b_s, w_out):
    B, T, _ = h.shape
    z = jax.nn.gelu(h @ w_in, approximate=False)
    u, v = jnp.split(z, 2, axis=-1)
    v = rmsnorm(v, g_v).reshape(B, T // CHUNK_A, CHUNK_A, A_GROUPS, A_GROUP_W)
    s = jnp.einsum('gpq,bnqgc->bnpgc', w_s, v) + b_s.T[:, :, None]
    return (u * s.reshape(B, T, A_INNER)) @ w_out


def axial_rope(x, rows, cols):
    half = x.shape[-1] // 2
    nf = half // 2
    freqs = jnp.power(ROPE_BASE, -jnp.arange(nf, dtype=F32) / nf)

    def rot(xp, pos):
        ang = pos.astype(F32)[:, None] * freqs
        cos, sin = jnp.cos(ang)[None, :, None, :], jnp.sin(ang)[None, :, None, :]
        x1, x2 = xp[..., :nf].astype(F32), xp[..., nf:].astype(F32)
        return jnp.concatenate([x1 * cos - x2 * sin, x1 * sin + x2 * cos], axis=-1)

    return jnp.concatenate([rot(x[..., :half], rows), rot(x[..., half:], cols)], axis=-1)


def gla_project(h, w_in, w_a2, b_a):
    B, T, _ = h.shape
    p = h @ w_in
    q, k, v, g, r = jnp.split(p, [GLA_KEY, 2 * GLA_KEY, 2 * GLA_KEY + GLA_VAL, 2 * GLA_KEY + 2 * GLA_VAL], axis=-1)
    z = jnp.einsum('btdr,drk->btdk', r.reshape(B, T, 2, GLA_RANK), w_a2) + b_a
    log_a = (jax.nn.log_sigmoid(z.astype(F32)) / GLA_TAU).reshape(B, T, 2, GLA_HEADS, GLA_HK)
    q = q.reshape(B, T, GLA_HEADS, GLA_HK) * (GLA_HK ** -0.5)
    k = k.reshape(B, T, GLA_HEADS, GLA_HK)
    v = v.reshape(B, T, GLA_HEADS, GLA_HV)
    return q, k, v, g, log_a


def gla_scan(q, k, v, log_a, s0):
    B, T, H, K = q.shape
    V = v.shape[-1]
    n = T // GLA_CHUNK
    rs = lambda t: t.astype(F32).reshape(B, n, GLA_CHUNK, H, t.shape[-1])
    q, k, v, la = rs(q), rs(k), rs(v), rs(log_a)
    cum = jnp.cumsum(la, axis=2)
    last = cum[:, :, -1:]
    q_dec = q * jnp.exp(cum)
    k_inv = k * jnp.exp(-cum)
    k_end = k * jnp.exp(last - cum)
    mask = jnp.tril(jnp.ones((GLA_CHUNK, GLA_CHUNK), dtype=bool))
    att = jnp.where(mask, jnp.einsum('bnihk,bnjhk->bnhij', q_dec, k_inv), 0.0)
    o_intra = jnp.einsum('bnhij,bnjhv->bnihv', att, v)
    ds = jnp.einsum('bnjhk,bnjhv->nbhkv', k_end, v)
    dec = jnp.exp(jnp.moveaxis(last[:, :, 0], 1, 0))

    def step(s, inp):
        ds_n, dec_n = inp
        return s * dec_n[..., None] + ds_n, s

    s_fin, s_in = lax.scan(step, s0, (ds, dec))
    o_inter = jnp.einsum('bnihk,nbhkv->bnihv', q_dec, s_in)
    return (o_intra + o_inter).reshape(B, T, H, V), s_fin


def gla_bidir(q, k, v, log_a, s0_f, s0_b):
    flip = lambda t: jnp.flip(t, axis=1)
    o_f, s_f = gla_scan(q, k, v, log_a[:, :, 0], s0_f)
    o_b, s_b = gla_scan(flip(q), flip(k), flip(v), flip(log_a[:, :, 1]), s0_b)
    return o_f + flip(o_b), s_f, s_b


def gla_out(o, g, g_o, w_out):
    B, T = o.shape[:2]
    o = rmsnorm(o, g_o).reshape(B, T, GLA_VAL).astype(g.dtype)
    return (o * jax.nn.silu(g)) @ w_out


def gla_mixer(hl, hc, w_in, w_a2, b_a, g_o, w_out, rows, cols, ctx_out):
    ql, kl, vl, gl, al = gla_project(hl, w_in, w_a2, b_a)
    ql, kl = axial_rope(ql, rows, cols), axial_rope(kl, rows, cols)
    qc, kc, vc, gc, ac = gla_project(hc, w_in, w_a2, b_a)
    zero = jnp.zeros((hc.shape[0], GLA_HEADS, GLA_HK, GLA_HV), F32)
    oc, s_f, s_b = gla_bidir(qc, kc, vc, ac, zero, zero)
    ol, _, _ = gla_bidir(ql, kl, vl, al, s_f, s_b)
    yl = gla_out(ol, gl, g_o, w_out)
    yc = gla_out(oc, gc, g_o, w_out) if ctx_out else None
    return yl, yc


def neighbourhood_attention(q, k, v, kc, vc, rpb):
    B, S, H, Dh = q.shape
    rows = S // GRID_W
    kh = min(NA_KH, rows)
    grid = lambda t: t.reshape(B, rows, GRID_W, H, Dh)
    qg, kg, vg = grid(q * (Dh ** -0.5)), grid(k), grid(v)
    n_cb = GRID_W // NA_QCOLS
    qcol = np.arange(GRID_W).reshape(n_cb, NA_QCOLS)
    cstart = np.clip(qcol - NA_KW // 2, 0, GRID_W - NA_KW)
    bstart = np.minimum(cstart[:, 0], GRID_W - NA_BAND)
    band = bstart[:, None] + np.arange(NA_BAND)
    col_ok = (band[:, None, :] >= cstart[:, :, None]) & (band[:, None, :] < cstart[:, :, None] + NA_KW)
    col_idx = np.clip(band[:, None, :] - qcol[:, :, None] + NA_KW - 1, 0, 2 * NA_KW - 2)
    rpb_c = rpb[:, :, col_idx]

    def one_row(r):
        r0 = jnp.clip(r - kh // 2, 0, rows - kh)
        k_band = lax.dynamic_slice_in_dim(kg, r0, kh, axis=1)[:, :, band]
        v_band = lax.dynamic_slice_in_dim(vg, r0, kh, axis=1)[:, :, band]
        q_r = lax.dynamic_index_in_dim(qg, r, axis=1, keepdims=False).reshape(B, n_cb, NA_QCOLS, H, Dh)
        s_lat = jnp.einsum('bjqhd,bkjmhd->bhjqkm', q_r, k_band).astype(F32)
        row_idx = r0 + jnp.arange(kh) - r + NA_KH - 1
        bias = jnp.take(rpb_c, row_idx, axis=1).transpose(0, 2, 3, 1, 4).astype(F32)
        s_lat = jnp.where(col_ok[:, :, None, :], s_lat + bias, NEG_INF)
        s_ctx = jnp.einsum('bjqhd,blhd->bhjql', q_r, kc).astype(F32)
        s = jnp.concatenate([s_lat.reshape(B, H, n_cb, NA_QCOLS, kh * NA_BAND), s_ctx], axis=-1)
        p = jax.nn.softmax(s, axis=-1).astype(v.dtype)
        p_lat = p[..., :kh * NA_BAND].reshape(B, H, n_cb, NA_QCOLS, kh, NA_BAND)
        o = (jnp.einsum('bhjqkm,bkjmhd->bjqhd', p_lat, v_band)
             + jnp.einsum('bhjql,blhd->bjqhd', p[..., kh * NA_BAND:], vc))
        return o.reshape(B, GRID_W, H, Dh)

    out = lax.map(one_row, jnp.arange(rows))
    return jnp.moveaxis(out, 0, 1).reshape(B, S, H, Dh)


def dense_attention(q, k, v):
    s = jnp.einsum('bqhd,bkhd->bhqk', q, k).astype(F32) * (q.shape[-1] ** -0.5)
    p = jax.nn.softmax(s, axis=-1).astype(v.dtype)
    return jnp.einsum('bhqk,bkhd->bqhd', p, v)


def na_mixer(hl, hc, w_qkv, rpb, w_out, ctx_out):
    B, S, D = hl.shape
    L = hc.shape[1]
    ql, kl, vl = [t.reshape(B, S, NA_HEADS, NA_HD) for t in jnp.split(hl @ w_qkv, 3, axis=-1)]
    qc, kc, vc = [t.reshape(B, L, NA_HEADS, NA_HD) for t in jnp.split(hc @ w_qkv, 3, axis=-1)]
    yl = neighbourhood_attention(ql, kl, vl, kc, vc, rpb).reshape(B, S, D) @ w_out
    yc = dense_attention(qc, kc, vc).reshape(B, L, D) @ w_out if ctx_out else None
    return yl, yc


def moe(h, w_router, b_router, w_gu, b_gu, w_down, b_down):
    N, D = h.shape
    logits = (h @ w_router).astype(F32) + b_router.astype(F32)
    top_val, top_idx = lax.top_k(logits, TOP_K)
    gate = jax.nn.softmax(top_val, axis=-1)
    n_rows = N * TOP_K
    flat_e = top_idx.reshape(-1)
    order = jnp.argsort(flat_e)
    sorted_e = flat_e[order]
    sorted_tok = order // TOP_K
    counts = jnp.bincount(flat_e, length=N_EXPERTS)
    padded = (counts + MOE_BLOCK - 1) // MOE_BLOCK * MOE_BLOCK
    pad_end = jnp.cumsum(padded)
    start = jnp.cumsum(counts) - counts
    dest = (pad_end - padded)[sorted_e] + jnp.arange(n_rows) - start[sorted_e]
    n_blocks = -(-n_rows // MOE_BLOCK) + N_EXPERTS
    buf = jnp.zeros((n_blocks * MOE_BLOCK, D), h.dtype).at[dest].set(h[sorted_tok])
    block_e = jnp.minimum(jnp.searchsorted(pad_end, jnp.arange(n_blocks) * MOE_BLOCK, side='right'), N_EXPERTS - 1)

    def expert_block(args):
        xb, e = args
        glu, lin = jnp.split(xb @ w_gu[e] + b_gu[e], 2, axis=-1)
        glu = jnp.minimum(glu, SWIGLU_LIMIT)
        lin = jnp.clip(lin, -SWIGLU_LIMIT, SWIGLU_LIMIT)
        return (glu * jax.nn.sigmoid(SWIGLU_ALPHA * glu) * (lin + 1)) @ w_down[e] + b_down[e]

    out = lax.map(expert_block, (buf.reshape(n_blocks, MOE_BLOCK, D), block_e))
    rows_out = out.reshape(-1, D)[dest].astype(F32) * gate.reshape(-1)[order][:, None]
    return jax.ops.segment_sum(rows_out, sorted_tok, num_segments=N).astype(h.dtype)


def setup_inputs(seed: int = 0) -> dict:
    key = jax.random.key(seed)
    ks = iter(jax.random.split(key, 40))
    nrm = lambda shape, s: jax.random.normal(next(ks), shape, F32) * s
    D, E, F = D_MODEL, N_EXPERTS, D_EXPERT
    gla_in = 2 * GLA_KEY + 2 * GLA_VAL + 2 * GLA_RANK
    return {
        "x": nrm((BATCH, SEQ, D), 1.0),
        "c": nrm((BATCH, D), 1.0),
        "ctx": nrm((BATCH, CTX_LEN, D), 1.0),
        "c_ctx": nrm((D,), 1.0),
        "ada_w": nrm((DEPTH, D, 6 * D), 0.5 * D ** -0.5),
        "ada_b": nrm((DEPTH, 6 * D), 0.01),
        "norm_mix": 1.0 + nrm((DEPTH, D), 0.02),
        "norm_ffn": 1.0 + nrm((DEPTH, D), 0.02),
        "norm_out": 1.0 + nrm((D,), 0.02),
        "a_w_in": nrm((N_A, D, 2 * A_INNER), D ** -0.5),
        "a_g_v": 1.0 + nrm((N_A, A_INNER), 0.02),
        "a_w_s": nrm((N_A, A_GROUPS, CHUNK_A, CHUNK_A), CHUNK_A ** -0.5),
        "a_b_s": 1.0 + nrm((N_A, A_GROUPS, CHUNK_A), 0.01),
        "a_w_out": nrm((N_A, A_INNER, D), A_INNER ** -0.5),
        "b_w_in": nrm((N_B, D, gla_in), D ** -0.5),
        "b_w_a2": nrm((N_B, 2, GLA_RANK, GLA_KEY), GLA_RANK ** -0.5),
        "b_b_a": nrm((N_B, 2, GLA_KEY), 0.1),
        "b_g_o": 1.0 + nrm((N_B, GLA_HV), 0.02),
        "b_w_out": nrm((N_B, GLA_VAL, D), GLA_VAL ** -0.5),
        "c_w_qkv": nrm((N_C, D, 3 * D), D ** -0.5),
        "c_rpb": nrm((N_C, NA_HEADS, 2 * NA_KH - 1, 2 * NA_KW - 1), 0.1),
        "c_w_out": nrm((N_C, D, D), D ** -0.5),
        "moe_w_router": nrm((DEPTH, D, E), D ** -0.5),
        "moe_b_router": nrm((DEPTH, E), 0.01),
        "moe_w_gu": nrm((DEPTH, E, D, 2 * F), D ** -0.5),
        "moe_b_gu": nrm((DEPTH, E, 2 * F), 0.01),
        "moe_w_down": nrm((DEPTH, E, F, D), F ** -0.5),
        "moe_b_down": nrm((DEPTH, E, D), 0.01),
    }


def reference(x, c, ctx, c_ctx, ada_w, ada_b, norm_mix, norm_ffn, norm_out,
              a_w_in, a_g_v, a_w_s, a_b_s, a_w_out,
              b_w_in, b_w_a2, b_b_a, b_g_o, b_w_out,
              c_w_qkv, c_rpb, c_w_out,
              moe_w_router, moe_b_router, moe_w_gu, moe_b_gu, moe_w_down, moe_b_down):
    B, S, D = x.shape
    L = ctx.shape[1]
    t = jnp.arange(S)
    rows, cols = t // GRID_W, t % GRID_W
    s_lat = jax.nn.silu(c)
    s_ctx = jax.nn.silu(c_ctx)
    xc = ctx
    for i in range(DEPTH):
        kind, j = i % N_MIXERS, i // N_MIXERS
        ctx_later = any(k % N_MIXERS != 0 for k in range(i + 1, DEPTH))
        ctx_here = ctx_later or kind != 0
        ml = jnp.split((s_lat @ ada_w[i] + ada_b[i])[:, None, :], 6, axis=-1)
        mc = jnp.split(s_ctx @ ada_w[i] + ada_b[i], 6, axis=-1)
        hl = modulate(x, norm_mix[i], ml[0], ml[1])
        hc = modulate(xc, norm_mix[i], mc[0], mc[1]) if ctx_here else None
        if kind == 0:
            yl = chunk_mlp(hl, a_w_in[j], a_g_v[j], a_w_s[j], a_b_s[j], a_w_out[j])
            yc = chunk_mlp(hc, a_w_in[j], a_g_v[j], a_w_s[j], a_b_s[j], a_w_out[j]) if ctx_later else None
        elif kind == 1:
            yl, yc = gla_mixer(hl, hc, b_w_in[j], b_w_a2[j], b_b_a[j], b_g_o[j], b_w_out[j], rows, cols, ctx_later)
        else:
            yl, yc = na_mixer(hl, hc, c_w_qkv[j], c_rpb[j], c_w_out[j], ctx_later)
        x = x + ml[2] * yl
        hl2 = modulate(x, norm_ffn[i], ml[3], ml[4]).reshape(-1, D)
        moe_p = (moe_w_router[i], moe_b_router[i], moe_w_gu[i], moe_b_gu[i], moe_w_down[i], moe_b_down[i])
        if ctx_later:
            xc = xc + mc[2] * yc
            hc2 = modulate(xc, norm_ffn[i], mc[3], mc[4]).reshape(-1, D)
            f = moe(jnp.concatenate([hc2, hl2], axis=0), *moe_p)
            xc = xc + mc[5] * f[:B * L].reshape(B, L, D)
            f = f[B * L:]
        else:
            f = moe(hl2, *moe_p)
        x = x + ml[5] * f.reshape(B, S, D)
    return rmsnorm(x, norm_out)
```

```python
import functools

import numpy as np
import jax
import jax.numpy as jnp
from jax import lax
from jax.experimental import pallas as pl
from jax.experimental.pallas import tpu as pltpu

F32 = jnp.float32
BF16 = jnp.bfloat16
I32 = jnp.int32

NORM_EPS = 1e-6
GRID_W = 64
N_MIXERS = 3

CHUNK_A = 128
A_GROUPS = 8
GLA_HEADS = 4
GLA_RANK = 16
GLA_TAU = 16.0
GLA_CHUNK = 64
ROPE_BASE = 10000.0
NA_HEADS = 16
NA_KH = 8
NA_KW = 16
NEG_INF = -1e30
NA_QROWS = 8
NA_WROWS = 16
TOP_K = 4
SWIGLU_LIMIT = 7.0
SWIGLU_ALPHA = 1.702
MOE_BM = 256

TM = 256
VMEM_LIMIT = 56 * 1024 * 1024


def _cparams(sem):
    return pltpu.CompilerParams(dimension_semantics=sem, vmem_limit_bytes=VMEM_LIMIT)


def _dot(a, b):
    return jnp.dot(a, b, preferred_element_type=F32)


def _dot_nt(a, b):
    return lax.dot_general(a, b, (((1,), (1,)), ((), ())), preferred_element_type=F32)


def _dot_tn(a, b):
    return lax.dot_general(a, b, (((0,), (0,)), ((), ())), preferred_element_type=F32)


def _rms(x, g):
    return x * lax.rsqrt(jnp.mean(x * x, axis=-1, keepdims=True) + NORM_EPS) * g


def _modulate(x, g, shift, scale):
    return _rms(x, g) * (1.0 + scale) + shift


def _ada_kernel(s_ref, w_ref, b_ref, o_ref):
    s = s_ref[...]
    s = s * jax.nn.sigmoid(s)
    o_ref[0] = jnp.dot(s, w_ref[0], preferred_element_type=F32,
                       precision=lax.Precision.HIGHEST) + b_ref[0]


def _adaln(cond, ada_w, ada_b):
    depth, d, n6 = ada_w.shape
    bn = n6 // 4
    return pl.pallas_call(
        _ada_kernel,
        grid=(depth, n6 // bn),
        in_specs=[pl.BlockSpec((8, d), lambda i, j: (0, 0)),
                  pl.BlockSpec((1, d, bn), lambda i, j: (i, 0, j)),
                  pl.BlockSpec((1, 1, bn), lambda i, j: (i, 0, j))],
        out_specs=pl.BlockSpec((1, 8, bn), lambda i, j: (i, 0, j)),
        out_shape=jax.ShapeDtypeStruct((depth, 8, n6), F32),
        compiler_params=_cparams(("parallel", "parallel")),
        name="adaln",
    )(cond, ada_w, ada_b.reshape(depth, 1, n6))


def _gelu(z):
    return 0.5 * z * (1.0 + lax.erf(z * np.float32(np.sqrt(0.5))))


def _gmlp_kernel(x_ref, m_ref, g_ref, win_ref, gv_ref, ws_ref, bs_ref, wout_ref, o_ref):
    x = x_ref[...]
    m = m_ref[0]
    a = gv_ref.shape[1]
    h = _modulate(x, g_ref[...], m[0:1], m[1:2]).astype(BF16)
    z = _gelu(_dot(h, win_ref[...]))
    u = z[:, :a]
    v = _rms(z[:, a:], gv_ref[...]).astype(BF16)
    gw = a // A_GROUPS
    rows = []
    for c in range(x.shape[0] // CHUNK_A):
        cols = [_dot(ws_ref[g], v[c * CHUNK_A:(c + 1) * CHUNK_A, g * gw:(g + 1) * gw])
                for g in range(A_GROUPS)]
        rows.append(jnp.concatenate(cols, axis=1) + bs_ref[...])
    s = jnp.concatenate(rows, axis=0)
    y = _dot((u * s).astype(BF16), wout_ref[...])
    o_ref[...] = x + m[2:3] * y


def _gmlp_mixer(xs, mods, g, w_in, g_v, w_s, b_s, w_out, tile0, n_tiles, geo):
    d = xs.shape[1]
    a = g_v.shape[0]
    gw = a // A_GROUPS
    bias = jnp.repeat(b_s.T, gw, axis=1)
    grp = functools.partial(_tile_group, geo=geo, tile0=tile0)
    const2 = lambda t: (0, 0)
    return pl.pallas_call(
        _gmlp_kernel,
        grid=(n_tiles,),
        in_specs=[pl.BlockSpec((TM, d), lambda t: (t + tile0, 0)),
                  pl.BlockSpec((1, 6, d), lambda t: (grp(t), 0, 0)),
                  pl.BlockSpec((1, d), const2),
                  pl.BlockSpec((d, 2 * a), const2),
                  pl.BlockSpec((1, a), const2),
                  pl.BlockSpec((A_GROUPS, CHUNK_A, CHUNK_A), lambda t: (0, 0, 0)),
                  pl.BlockSpec((CHUNK_A, a), const2),
                  pl.BlockSpec((a, d), const2)],
        out_specs=pl.BlockSpec((TM, d), lambda t: (t, 0)),
        out_shape=jax.ShapeDtypeStruct((n_tiles * TM, d), F32),
        compiler_params=_cparams(("parallel",)),
        name="gmlp_mixer",
    )(xs, mods, g.reshape(1, d), w_in.astype(BF16), g_v.reshape(1, a), w_s.astype(BF16),
      bias, w_out.astype(BF16))


def _tile_group(t, geo, tile0=0):
    n_ctx_tiles, tiles_per_batch, nb = geo
    tt = t + tile0
    return jnp.where(tt < n_ctx_tiles, nb, (tt - n_ctx_tiles) // tiles_per_batch)


def _dot_f32(tri_bf16, x):
    hi = x.astype(BF16)
    r1 = x - hi.astype(F32)
    mid = r1.astype(BF16)
    lo = (r1 - mid.astype(F32)).astype(BF16)
    return _dot(tri_bf16, hi) + _dot(tri_bf16, mid) + _dot(tri_bf16, lo)


def _gla_proj_kernel(x_ref, m_ref, g_ref, wq_ref, wk_ref, wv_ref, wg_ref, wr_ref, wa_ref, ba_ref,
                     cos_ref, sin_ref, q_ref, k_ref, v_ref, go_ref, la_ref):
    x = x_ref[...]
    m = m_ref[0]
    h = _modulate(x, g_ref[...], m[0:1], m[1:2]).astype(BF16)
    kdim = wq_ref.shape[1]
    hk = kdim // GLA_HEADS
    q = _dot(h, wq_ref[...]) * np.float32(hk ** -0.5)
    k = _dot(h, wk_ref[...])
    v_ref[...] = _dot(h, wv_ref[...]).astype(BF16)
    go_ref[...] = _dot(h, wg_ref[...])
    r = _dot(h, wr_ref[...])
    z = _dot(r.astype(BF16), wa_ref[...]) + ba_ref[...]
    la_ref[...] = jax.nn.log_sigmoid(z) * np.float32(1.0 / GLA_TAU)

    cos = jnp.concatenate([cos_ref[...]] * GLA_HEADS, axis=1)
    sin = jnp.concatenate([sin_ref[...]] * GLA_HEADS, axis=1)
    nf = hk // 4
    lane = lax.broadcasted_iota(I32, q.shape, 1)
    first = (lane % (2 * nf)) < nf

    def rope(t):
        up = pltpu.roll(t, kdim - nf, 1)
        dn = pltpu.roll(t, nf, 1)
        return t * cos + jnp.where(first, up, dn) * sin

    q_ref[...] = rope(q)
    k_ref[...] = rope(k)


def _gla_scan_kernel(q_ref, k_ref, v_ref, la_ref, o_ref, st_ref, *, rev):
    s = pl.program_id(1)

    @pl.when(s == 0)
    def _():
        st_ref[...] = jnp.zeros_like(st_ref)

    tb, kdim = q_ref.shape
    hk = kdim // GLA_HEADS
    hv = v_ref.shape[1] // GLA_HEADS
    c = GLA_CHUNK
    ri = lax.broadcasted_iota(I32, (c, c), 0)
    ci = lax.broadcasted_iota(I32, (c, c), 1)
    keep = (ci >= ri) if rev else (ci <= ri)
    tri = keep.astype(BF16)
    order = range(tb // c)
    for n in (reversed(order) if rev else order):
        rows = slice(n * c, (n + 1) * c)
        cum = _dot_f32(tri, la_ref[rows, :])
        last = cum[0:1] if rev else cum[c - 1:c]
        q = q_ref[rows, :]
        k = k_ref[rows, :]
        q_dec = (q * jnp.exp(cum)).astype(BF16)
        k_inv = (k * jnp.exp(-cum)).astype(BF16)
        k_end = (k * jnp.exp(last - cum)).astype(BF16)
        dec = jnp.exp(last)
        for h in range(GLA_HEADS):
            ks = slice(h * hk, (h + 1) * hk)
            vs = slice(h * hv, (h + 1) * hv)
            att = jnp.where(keep, _dot_nt(q_dec[:, ks], k_inv[:, ks]), 0.0).astype(BF16)
            vh = v_ref[rows, vs]
            st = st_ref[h]
            o_ref[rows, vs] = _dot(att, vh) + _dot_nt(q_dec[:, ks], st.astype(BF16))
            st_ref[h] = st * dec[:, ks] + _dot_tn(vh, k_end[:, ks])


def _gla_out_kernel(x_ref, m_ref, of_ref, ob_ref, go_ref, gn_ref, wout_ref, o_ref):
    x = x_ref[...]
    m = m_ref[0]
    o = of_ref[...] + ob_ref[...]
    hv = gn_ref.shape[1]
    parts = [_rms(o[:, h * hv:(h + 1) * hv], gn_ref[...]) for h in range(GLA_HEADS)]
    o = jnp.concatenate(parts, axis=1)
    gate = go_ref[...]
    y = _dot((o * (gate * jax.nn.sigmoid(gate))).astype(BF16), wout_ref[...])
    o_ref[...] = x + m[2:3] * y


def _gla_mixer(xs, mods, g, w_in, w_a2, b_a, g_o, w_out, geo, dims):
    b, seq, lctx = dims
    nt, d = xs.shape
    n_tiles = nt // TM
    kdim = w_a2.shape[2]
    vdim = g_o.shape[0] * GLA_HEADS
    hk = kdim // GLA_HEADS
    nf = hk // 4
    wq = w_in[:, :kdim].astype(BF16)
    wk = w_in[:, kdim:2 * kdim].astype(BF16)
    wv = w_in[:, 2 * kdim:2 * kdim + vdim].astype(BF16)
    wg = w_in[:, 2 * kdim + vdim:2 * kdim + 2 * vdim].astype(BF16)
    wr = jnp.pad(w_in[:, 2 * kdim + 2 * vdim:], ((0, 0), (0, 128 - 2 * GLA_RANK))).astype(BF16)
    wa = jnp.zeros((128, 2 * kdim), F32)
    wa = wa.at[:GLA_RANK, :kdim].set(w_a2[0]).at[GLA_RANK:2 * GLA_RANK, kdim:].set(w_a2[1]).astype(BF16)
    ba = b_a.reshape(1, 2 * kdim)

    tpos = jnp.arange(seq)
    freqs = jnp.power(ROPE_BASE, -jnp.arange(nf, dtype=F32) / nf)
    ar = (tpos // GRID_W).astype(F32)[:, None] * freqs
    ac = (tpos % GRID_W).astype(F32)[:, None] * freqs
    cos = jnp.concatenate([jnp.cos(ar), jnp.cos(ar), jnp.cos(ac), jnp.cos(ac)], axis=1)
    sin = jnp.concatenate([-jnp.sin(ar), jnp.sin(ar), -jnp.sin(ac), jnp.sin(ac)], axis=1)
    cos = jnp.concatenate([jnp.ones((TM, hk), F32), cos], axis=0)
    sin = jnp.concatenate([jnp.zeros((TM, hk), F32), sin], axis=0)

    n_ctx_tiles, tiles_per_batch, _ = geo
    grp = functools.partial(_tile_group, geo=geo)

    def rope_blk(t):
        return (jnp.where(t < n_ctx_tiles, 0, 1 + (t - n_ctx_tiles) % tiles_per_batch), 0)

    const2 = lambda t: (0, 0)
    row = lambda t: (t, 0)
    q, k, v, go, la = pl.pallas_call(
        _gla_proj_kernel,
        grid=(n_tiles,),
        in_specs=[pl.BlockSpec((TM, d), row),
                  pl.BlockSpec((1, 6, d), lambda t: (grp(t), 0, 0)),
                  pl.BlockSpec((1, d), const2),
                  pl.BlockSpec((d, kdim), const2), pl.BlockSpec((d, kdim), const2),
                  pl.BlockSpec((d, vdim), const2), pl.BlockSpec((d, vdim), const2),
                  pl.BlockSpec((d, 128), const2), pl.BlockSpec((128, 2 * kdim), const2),
                  pl.BlockSpec((1, 2 * kdim), const2),
                  pl.BlockSpec((TM, hk), rope_blk), pl.BlockSpec((TM, hk), rope_blk)],
        out_specs=[pl.BlockSpec((TM, kdim), row), pl.BlockSpec((TM, kdim), row),
                   pl.BlockSpec((TM, vdim), row), pl.BlockSpec((TM, vdim), row),
                   pl.BlockSpec((TM, 2 * kdim), row)],
        out_shape=[jax.ShapeDtypeStruct((nt, kdim), F32), jax.ShapeDtypeStruct((nt, kdim), F32),
                   jax.ShapeDtypeStruct((nt, vdim), BF16), jax.ShapeDtypeStruct((nt, vdim), F32),
                   jax.ShapeDtypeStruct((nt, 2 * kdim), F32)],
        compiler_params=_cparams(("parallel",)),
        name="gla_proj",
    )(xs, mods, g.reshape(1, d), wq, wk, wv, wg, wr, wa, ba, cos, sin)

    tb = TM
    ctx_steps = lctx // tb
    lat_steps = seq // tb
    steps = ctx_steps + lat_steps

    def scan(rev):
        def blk(bi, s):
            if rev:
                cs, ls = ctx_steps - 1 - s, lat_steps - 1 - (s - ctx_steps)
            else:
                cs, ls = s, s - ctx_steps
            return jnp.where(s < ctx_steps, bi * ctx_steps + cs, b * ctx_steps + bi * lat_steps + ls)

        d_col = 1 if rev else 0
        return pl.pallas_call(
            functools.partial(_gla_scan_kernel, rev=rev),
            grid=(b, steps),
            in_specs=[pl.BlockSpec((tb, kdim), lambda bi, s: (blk(bi, s), 0)),
                      pl.BlockSpec((tb, kdim), lambda bi, s: (blk(bi, s), 0)),
                      pl.BlockSpec((tb, vdim), lambda bi, s: (blk(bi, s), 0)),
                      pl.BlockSpec((tb, kdim), lambda bi, s: (blk(bi, s), d_col))],
            out_specs=pl.BlockSpec((tb, vdim), lambda bi, s: (blk(bi, s), 0)),
            out_shape=jax.ShapeDtypeStruct((nt, vdim), F32),
            scratch_shapes=[pltpu.VMEM((GLA_HEADS, vdim // GLA_HEADS, hk), F32)],
            compiler_params=_cparams(("parallel", "arbitrary")),
            name="gla_scan_bwd" if rev else "gla_scan_fwd",
        )(q, k, v, la)

    o_f = scan(False)
    o_b = scan(True)

    return pl.pallas_call(
        _gla_out_kernel,
        grid=(n_tiles,),
        in_specs=[pl.BlockSpec((TM, d), row),
                  pl.BlockSpec((1, 6, d), lambda t: (grp(t), 0, 0)),
                  pl.BlockSpec((TM, vdim), row), pl.BlockSpec((TM, vdim), row),
                  pl.BlockSpec((TM, vdim), row),
                  pl.BlockSpec((1, vdim // GLA_HEADS), const2),
                  pl.BlockSpec((vdim, d), const2)],
        out_specs=pl.BlockSpec((TM, d), row),
        out_shape=jax.ShapeDtypeStruct((nt, d), F32),
        compiler_params=_cparams(("parallel",)),
        name="gla_out",
    )(xs, mods, o_f, o_b, go, g_o.reshape(1, -1), w_out.astype(BF16))


def _na_proj_kernel(x_ref, m_ref, g_ref, w_ref, q_ref, k_ref, v_ref):
    x = x_ref[...]
    m = m_ref[0]
    d = x.shape[1]
    h = _modulate(x, g_ref[...], m[0:1], m[1:2]).astype(BF16)
    qkv = _dot(h, w_ref[...])
    hd = d // NA_HEADS
    for p in range(q_ref.shape[0]):
        cs = slice(p * 128, (p + 1) * 128)
        q_ref[p] = (qkv[:, cs] * np.float32(hd ** -0.5)).astype(BF16)
        k_ref[p] = qkv[:, d + p * 128:d + (p + 1) * 128].astype(BF16)
        v_ref[p] = qkv[:, 2 * d + p * 128:2 * d + (p + 1) * 128].astype(BF16)


def _na_attn_kernel(tbl_ref, q_ref, k0_ref, k1_ref, k2_ref, k3_ref, v0_ref, v1_ref, v2_ref, v3_ref,
                    kc_ref, vc_ref, bias_ref, o_ref, s_ref, *, n_tiles):
    t = pl.program_id(2)
    typ = jnp.where(t == 0, 0, jnp.where(t == n_tiles - 1, 2, 1))
    q = q_ref[0]
    kw = jnp.concatenate([k0_ref[0], k1_ref[0], k2_ref[0], k3_ref[0]], axis=0)
    vw = jnp.concatenate([v0_ref[0], v1_ref[0], v2_ref[0], v3_ref[0]], axis=0)
    kc = kc_ref[0]
    vc = vc_ref[0]
    lane = lax.broadcasted_iota(I32, q.shape, 1)
    npair = NA_WROWS // 2
    outs = []
    for hh in range(2):
        sel = (lane < 64) if hh == 0 else (lane >= 64)
        qm = jnp.where(sel, q, jnp.zeros_like(q))
        s_ref[...] = _dot_nt(qm, kw)
        s_ctx = _dot_nt(qm, kc)
        for dr in range(NA_QROWS):
            for j in range(npair):
                e = tbl_ref[typ * (NA_QROWS * npair) + dr * npair + j]
                rs = slice(dr * GRID_W, (dr + 1) * GRID_W)
                cs = slice(j * 128, (j + 1) * 128)
                s_ref[rs, cs] = s_ref[rs, cs] + bias_ref[hh, e]
        s_lat = s_ref[...]
        mx = jnp.maximum(jnp.max(s_lat, axis=1, keepdims=True), jnp.max(s_ctx, axis=1, keepdims=True))
        e_lat = jnp.exp(s_lat - mx)
        e_ctx = jnp.exp(s_ctx - mx)
        inv = 1.0 / (jnp.sum(e_lat, axis=1, keepdims=True) + jnp.sum(e_ctx, axis=1, keepdims=True))
        outs.append(_dot((e_lat * inv).astype(BF16), vw) + _dot((e_ctx * inv).astype(BF16), vc))
    o_ref[...] = jnp.where(lane < 64, outs[0], outs[1]).astype(BF16)


def _na_out_kernel(x_ref, m_ref, a_ref, w_ref, o_ref):
    m = m_ref[0]
    o_ref[...] = x_ref[...] + m[2:3] * _dot(a_ref[...], w_ref[...])


def _na_tables(rows):
    npair = NA_WROWS // 2
    tbl = np.zeros((3, NA_QROWS, npair), np.int32)
    for typ, r_base in enumerate((0, NA_QROWS, rows - NA_QROWS)):
        w0 = int(np.clip(r_base - NA_KH // 2, 0, rows - NA_WROWS))
        for dr in range(NA_QROWS):
            r = r_base + dr
            r0 = int(np.clip(r - NA_KH // 2, 0, rows - NA_KH))
            for j in range(npair):
                kr = (w0 + 2 * j, w0 + 2 * j + 1)
                ok = [r0 <= x < r0 + NA_KH for x in kr]
                ri = [x - r + NA_KH - 1 for x in kr]
                if ok[0] and ok[1]:
                    e = ri[0]
                elif ok[0]:
                    e = 16 + ri[0]
                elif ok[1]:
                    e = 32 + ri[1]
                else:
                    e = 63
                tbl[typ, dr, j] = e
    return tbl.reshape(-1)


def _na_bias_table(rpb):
    nh = rpb.shape[0]
    qc = np.arange(GRID_W)
    cstart = np.clip(qc - NA_KW // 2, 0, GRID_W - NA_KW)
    kc = np.arange(GRID_W)
    ok = (kc[None, :] >= cstart[:, None]) & (kc[None, :] < cstart[:, None] + NA_KW)
    cidx = np.clip(kc[None, :] - qc[:, None] + NA_KW - 1, 0, 2 * NA_KW - 2)
    cb = jnp.where(ok[None, None], rpb[:, :, cidx], NEG_INF).astype(F32)
    neg = jnp.full((nh, 1, GRID_W, GRID_W), NEG_INF, F32)
    cb = jnp.concatenate([cb, neg], axis=1)
    negs = jnp.broadcast_to(neg, cb.shape)
    nxt = jnp.concatenate([cb[:, 1:], neg], axis=1)
    both = jnp.concatenate([cb, nxt], axis=-1)
    left = jnp.concatenate([cb, negs], axis=-1)
    right = jnp.concatenate([negs, cb], axis=-1)
    none = jnp.concatenate([negs, negs], axis=-1)
    return jnp.concatenate([both, left, right, none], axis=1)


def _na_mixer(xs, mods, g, w_qkv, rpb, w_out, geo, dims):
    b, seq, lctx = dims
    nt, d = xs.shape
    n_tiles_tok = nt // TM
    grp = functools.partial(_tile_group, geo=geo)
    npairs = d // 128
    const2 = lambda t: (0, 0)
    q, k, v = pl.pallas_call(
        _na_proj_kernel,
        grid=(n_tiles_tok,),
        in_specs=[pl.BlockSpec((TM, d), lambda t: (t, 0)),
                  pl.BlockSpec((1, 6, d), lambda t: (grp(t), 0, 0)),
                  pl.BlockSpec((1, d), const2),
                  pl.BlockSpec((d, 3 * d), const2)],
        out_specs=[pl.BlockSpec((npairs, TM, 128), lambda t: (0, t, 0))] * 3,
        out_shape=[jax.ShapeDtypeStruct((npairs, nt, 128), BF16)] * 3,
        compiler_params=_cparams(("parallel",)),
        name="na_proj",
    )(xs, mods, g.reshape(1, d), w_qkv.astype(BF16))

    rows = seq // GRID_W
    tq = NA_QROWS * GRID_W
    n_tiles = rows // NA_QROWS
    wb = 256
    nwin = NA_WROWS * GRID_W // wb
    lat0 = b * lctx
    tbl = jnp.asarray(_na_tables(rows))
    bias = _na_bias_table(rpb)

    def win(i):
        def f(p, bi, t, tbl_ref):
            w = jnp.clip(2 * t - 1, 0, seq // wb - nwin)
            return (p, (lat0 + bi * seq) // wb + w + i, 0)
        return f

    kv_specs = [pl.BlockSpec((1, wb, 128), win(i)) for i in range(nwin)]
    attn = pl.pallas_call(
        functools.partial(_na_attn_kernel, n_tiles=n_tiles),
        grid_spec=pltpu.PrefetchScalarGridSpec(
            num_scalar_prefetch=1,
            grid=(npairs, b, n_tiles),
            in_specs=[pl.BlockSpec((1, tq, 128), lambda p, bi, t, tr: (p, (lat0 + bi * seq) // tq + t, 0))]
                     + kv_specs + kv_specs
                     + [pl.BlockSpec((1, lctx, 128), lambda p, bi, t, tr: (p, bi, 0)),
                        pl.BlockSpec((1, lctx, 128), lambda p, bi, t, tr: (p, bi, 0)),
                        pl.BlockSpec((2, 64, GRID_W, 2 * GRID_W), lambda p, bi, t, tr: (p, 0, 0, 0))],
            out_specs=pl.BlockSpec((tq, 128), lambda p, bi, t, tr: (bi * n_tiles + t, p)),
            scratch_shapes=[pltpu.VMEM((tq, NA_WROWS * GRID_W), F32)]),
        out_shape=jax.ShapeDtypeStruct((b * seq, d), BF16),
        compiler_params=_cparams(("parallel", "parallel", "arbitrary")),
        name="na_attn",
    )(tbl, q, k, k, k, k, v, v, v, v, k, v, bias)

    n_lat_tiles = b * seq // TM
    tile0 = lat0 // TM
    grp_l = functools.partial(_tile_group, geo=geo, tile0=tile0)
    return pl.pallas_call(
        _na_out_kernel,
        grid=(n_lat_tiles,),
        in_specs=[pl.BlockSpec((TM, d), lambda t: (t + tile0, 0)),
                  pl.BlockSpec((1, 6, d), lambda t: (grp_l(t), 0, 0)),
                  pl.BlockSpec((TM, d), lambda t: (t, 0)),
                  pl.BlockSpec((d, d), const2)],
        out_specs=pl.BlockSpec((TM, d), lambda t: (t, 0)),
        out_shape=jax.ShapeDtypeStruct((b * seq, d), F32),
        compiler_params=_cparams(("parallel",)),
        name="na_out",
    )(xs, mods, attn, w_out.astype(BF16))


def _route_kernel(x_ref, m_ref, g_ref, wr_ref, br_ref, h_ref, idx_ref, gate_ref, rank_ref, cnt_ref, run_ref):
    t = pl.program_id(0)

    @pl.when(t == 0)
    def _():
        run_ref[...] = jnp.zeros_like(run_ref)

    x = x_ref[...]
    m = m_ref[0]
    h = _modulate(x, g_ref[...], m[3:4], m[4:5]).astype(BF16)
    h_ref[...] = h
    logits = _dot_nt(wr_ref[...], h) + br_ref[...]
    ne, tm = logits.shape
    e_iota = lax.broadcasted_iota(I32, (ne, tm), 0)
    vals, idxs = [], []
    l = logits
    for _ in range(TOP_K):
        mk = jnp.max(l, axis=0, keepdims=True)
        ik = jnp.min(jnp.where(l == mk, e_iota, ne), axis=0, keepdims=True)
        vals.append(mk)
        idxs.append(ik)
        l = jnp.where(e_iota == ik, -jnp.inf, l)
    top_val = jnp.concatenate(vals, axis=0)
    ex = jnp.exp(top_val - vals[0])
    gate_ref[...] = ex / jnp.sum(ex, axis=0, keepdims=True)
    idx_ref[...] = jnp.concatenate(idxs, axis=0)

    hits = [e_iota == ik for ik in idxs]
    cnt = hits[0].astype(F32)
    for hk in hits[1:]:
        cnt = cnt + hk.astype(F32)
    si = lax.broadcasted_iota(I32, (tm, tm), 0)
    ti = lax.broadcasted_iota(I32, (tm, tm), 1)
    before = (si < ti).astype(BF16)
    total = _dot(cnt.astype(BF16), before) + run_ref[...]
    ranks = [jnp.sum(jnp.where(hk, total, 0.0), axis=0, keepdims=True) for hk in hits]
    rank_ref[...] = jnp.concatenate(ranks, axis=0).astype(I32)
    run_ref[...] = run_ref[...] + jnp.sum(cnt, axis=1, keepdims=True)
    cnt_ref[...] = run_ref[...]


def _expert_kernel(be_ref, nu_ref, x_ref, wgu_ref, bgu_ref, wd_ref, bd_ref, o_ref, act_ref):
    blk = pl.program_id(0)
    f = wd_ref.shape[1]
    ch = 256

    @pl.when(blk < nu_ref[0])
    def _():
        x = x_ref[...]
        for j in range(f // ch):
            c0 = slice(j * ch, (j + 1) * ch)
            c1 = slice(f + j * ch, f + (j + 1) * ch)
            glu = _dot(x, wgu_ref[0, :, c0]) + bgu_ref[0, :, c0]
            lin = _dot(x, wgu_ref[0, :, c1]) + bgu_ref[0, :, c1]
            glu = jnp.minimum(glu, SWIGLU_LIMIT)
            lin = jnp.clip(lin, -SWIGLU_LIMIT, SWIGLU_LIMIT)
            act_ref[:, c0] = (glu * jax.nn.sigmoid(SWIGLU_ALPHA * glu) * (lin + 1.0)).astype(BF16)
        o_ref[...] = _dot(act_ref[...], wd_ref[0]) + bd_ref[0]

    @pl.when(blk >= nu_ref[0])
    def _():
        o_ref[...] = jnp.zeros_like(o_ref)


def _combine_kernel(x_ref, m_ref, y_ref, gate_ref, o_ref):
    m = m_ref[0]
    gate = gate_ref[...]
    f = gate[:, 0:1] * y_ref[0]
    for k in range(1, TOP_K):
        f = f + gate[:, k:k + 1] * y_ref[k]
    o_ref[...] = x_ref[...] + m[5:6] * f


def _moe_layer(xs, mods, g, w_router, b_router, w_gu, b_gu, w_down, b_down, tile0, geo):
    d = xs.shape[1]
    ne = w_router.shape[1]
    f = w_down.shape[1]
    n = xs.shape[0] - tile0 * TM
    n_tiles = n // TM
    grp = functools.partial(_tile_group, geo=geo, tile0=tile0)
    const2 = lambda t: (0, 0)
    h, idx, gate, rank, cnt = pl.pallas_call(
        _route_kernel,
        grid=(n_tiles,),
        in_specs=[pl.BlockSpec((TM, d), lambda t: (t + tile0, 0)),
                  pl.BlockSpec((1, 6, d), lambda t: (grp(t), 0, 0)),
                  pl.BlockSpec((1, d), const2),
                  pl.BlockSpec((ne, d), const2),
                  pl.BlockSpec((ne, 1), const2)],
        out_specs=[pl.BlockSpec((TM, d), lambda t: (t, 0)),
                   pl.BlockSpec((TOP_K, TM), lambda t: (0, t)),
                   pl.BlockSpec((TOP_K, TM), lambda t: (0, t)),
                   pl.BlockSpec((TOP_K, TM), lambda t: (0, t)),
                   pl.BlockSpec((ne, 1), const2)],
        out_shape=[jax.ShapeDtypeStruct((n, d), BF16),
                   jax.ShapeDtypeStruct((TOP_K, n), I32),
                   jax.ShapeDtypeStruct((TOP_K, n), F32),
                   jax.ShapeDtypeStruct((TOP_K, n), I32),
                   jax.ShapeDtypeStruct((ne, 1), F32)],
        scratch_shapes=[pltpu.VMEM((ne, 1), F32)],
        compiler_params=_cparams(("arbitrary",)),
        name="moe_route",
    )(xs, mods, g.reshape(1, d), w_router.T.astype(BF16), b_router.reshape(ne, 1))

    counts = cnt.reshape(ne).astype(I32)
    padded = (counts + MOE_BM - 1) // MOE_BM * MOE_BM
    pad_end = jnp.cumsum(padded)
    pad_start = pad_end - padded
    n_rows = n * TOP_K
    n_blocks = -(-n_rows // MOE_BM) + ne
    n_slots = n_blocks * MOE_BM
    dest = pad_start[idx] + rank
    block_e = jnp.minimum(jnp.searchsorted(pad_end, jnp.arange(n_blocks, dtype=I32) * MOE_BM, side='right'),
                          ne - 1).astype(I32)
    n_used = (pad_end[-1:] // MOE_BM).astype(I32)
    tok = jnp.broadcast_to(jnp.arange(n, dtype=I32)[None, :], (TOP_K, n))
    src_tok = jnp.zeros((n_slots,), I32).at[dest.reshape(-1)].set(tok.reshape(-1))
    buf = jnp.take(h, src_tok, axis=0)

    y = pl.pallas_call(
        _expert_kernel,
        grid_spec=pltpu.PrefetchScalarGridSpec(
            num_scalar_prefetch=2,
            grid=(n_blocks,),
            in_specs=[pl.BlockSpec((MOE_BM, d), lambda i, be, nu: (i, 0)),
                      pl.BlockSpec((1, d, 2 * f), lambda i, be, nu: (be[i], 0, 0)),
                      pl.BlockSpec((1, 1, 2 * f), lambda i, be, nu: (be[i], 0, 0)),
                      pl.BlockSpec((1, f, d), lambda i, be, nu: (be[i], 0, 0)),
                      pl.BlockSpec((1, 1, d), lambda i, be, nu: (be[i], 0, 0))],
            out_specs=pl.BlockSpec((MOE_BM, d), lambda i, be, nu: (i, 0)),
            scratch_shapes=[pltpu.VMEM((MOE_BM, f), BF16)]),
        out_shape=jax.ShapeDtypeStruct((n_slots, d), F32),
        compiler_params=_cparams(("arbitrary",)),
        name="moe_experts",
    )(block_e, n_used, buf, w_gu.astype(BF16), b_gu.reshape(ne, 1, 2 * f),
      w_down.astype(BF16), b_down.reshape(ne, 1, d))

    yk = jnp.take(y, dest.reshape(-1), axis=0).reshape(TOP_K, n, d)
    return pl.pallas_call(
        _combine_kernel,
        grid=(n_tiles,),
        in_specs=[pl.BlockSpec((TM, d), lambda t: (t + tile0, 0)),
                  pl.BlockSpec((1, 6, d), lambda t: (grp(t), 0, 0)),
                  pl.BlockSpec((TOP_K, TM, d), lambda t: (0, t, 0)),
                  pl.BlockSpec((TM, TOP_K), lambda t: (t, 0))],
        out_specs=pl.BlockSpec((TM, d), lambda t: (t, 0)),
        out_shape=jax.ShapeDtypeStruct((n, d), F32),
        compiler_params=_cparams(("parallel",)),
        name="moe_combine",
    )(xs, mods, yk, gate.T)


def _final_kernel(x_ref, g_ref, o_ref):
    o_ref[...] = _rms(x_ref[...], g_ref[...])


def _final_norm(xs, g):
    n, d = xs.shape
    return pl.pallas_call(
        _final_kernel,
        grid=(n // TM,),
        in_specs=[pl.BlockSpec((TM, d), lambda t: (t, 0)), pl.BlockSpec((1, d), lambda t: (0, 0))],
        out_specs=pl.BlockSpec((TM, d), lambda t: (t, 0)),
        out_shape=jax.ShapeDtypeStruct((n, d), F32),
        compiler_params=_cparams(("parallel",)),
        name="final_norm",
    )(xs, g.reshape(1, d))


def kernel(x, c, ctx, c_ctx, ada_w, ada_b, norm_mix, norm_ffn, norm_out, a_w_in, a_g_v, a_w_s, a_b_s, a_w_out, b_w_in, b_w_a2, b_b_a, b_g_o, b_w_out, c_w_qkv, c_rpb, c_w_out, moe_w_router, moe_b_router, moe_w_gu, moe_b_gu, moe_w_down, moe_b_down):
    b, seq, d = x.shape
    lctx = ctx.shape[1]
    depth = ada_w.shape[0]
    assert lctx % TM == 0 and seq % (NA_QROWS * GRID_W) == 0 and (b * lctx) % (NA_QROWS * GRID_W) == 0
    assert seq // GRID_W >= NA_WROWS + NA_QROWS and b + 1 <= 8
    geo = (b * lctx // TM, seq // TM, b)
    dims = (b, seq, lctx)

    cond = jnp.zeros((8, d), F32).at[:b].set(c).at[b].set(c_ctx)
    mods_all = _adaln(cond, ada_w, ada_b)[:, :b + 1].reshape(depth, b + 1, 6, d)

    xs = jnp.concatenate([ctx.reshape(b * lctx, d), x.reshape(b * seq, d)], axis=0)
    ctx_tiles = geo[0]
    has_ctx = True
    for i in range(depth):
        kind, j = i % N_MIXERS, i // N_MIXERS
        ctx_later = any(kk % N_MIXERS != 0 for kk in range(i + 1, depth))
        mods = mods_all[i]
        if kind == 0:
            keep_ctx = has_ctx and ctx_later
            skip = 0 if keep_ctx or not has_ctx else ctx_tiles
            geo_i = geo if has_ctx else (0, geo[1], b)
            xs = _gmlp_mixer(xs, mods, norm_mix[i], a_w_in[j], a_g_v[j], a_w_s[j], a_b_s[j], a_w_out[j],
                             skip, xs.shape[0] // TM - skip, geo_i)
            has_ctx = keep_ctx
        elif kind == 1:
            assert has_ctx
            xs = _gla_mixer(xs, mods, norm_mix[i], b_w_in[j], b_w_a2[j], b_b_a[j], b_g_o[j], b_w_out[j], geo, dims)
            if not ctx_later:
                xs = xs[b * lctx:]
                has_ctx = False
        else:
            assert has_ctx
            if ctx_later:
                raise NotImplementedError("context output of the neighbourhood mixer")
            xs = _na_mixer(xs, mods, norm_mix[i], c_w_qkv[j], c_rpb[j], c_w_out[j], geo, dims)
            has_ctx = False
        geo_i = geo if has_ctx else (0, geo[1], b)
        xs = _moe_layer(xs, mods, norm_ffn[i], moe_w_router[i], moe_b_router[i], moe_w_gu[i], moe_b_gu[i],
                        moe_w_down[i], moe_b_down[i], 0, geo_i)
    if has_ctx:
        xs = xs[b * lctx:]
    return _final_norm(xs, norm_out).reshape(b, seq, d)
```

```python
import functools

import numpy as np
import jax
import jax.numpy as jnp
from jax import lax
from jax.experimental import pallas as pl
from jax.experimental.pallas import tpu as pltpu
from jax.experimental.pallas import tpu_sc as plsc

F32 = jnp.float32
BF16 = jnp.bfloat16
I32 = jnp.int32

NORM_EPS = 1e-6
GRID_W = 64
N_MIXERS = 3

CHUNK_A = 128
A_GROUPS = 8
GLA_HEADS = 4
GLA_RANK = 16
GLA_TAU = 16.0
GLA_CHUNK = 64
ROPE_BASE = 10000.0
NA_HEADS = 16
NA_KH = 8
NA_KW = 16
NEG_INF = -1e30
NA_QROWS = 8
NA_WROWS = 16
TOP_K = 4
SWIGLU_LIMIT = 7.0
SWIGLU_ALPHA = 1.702
MOE_BM = 256

TM = 256
VMEM_LIMIT = 56 * 1024 * 1024


def _cparams(sem):
    return pltpu.CompilerParams(dimension_semantics=sem, vmem_limit_bytes=VMEM_LIMIT)


def _dot(a, b):
    return jnp.dot(a, b, preferred_element_type=F32)


def _dot_nt(a, b):
    return lax.dot_general(a, b, (((1,), (1,)), ((), ())), preferred_element_type=F32)


def _dot_tn(a, b):
    return lax.dot_general(a, b, (((0,), (0,)), ((), ())), preferred_element_type=F32)


def _rms(x, g):
    return x * lax.rsqrt(jnp.mean(x * x, axis=-1, keepdims=True) + NORM_EPS) * g


def _modulate(x, g, shift, scale):
    return _rms(x, g) * (1.0 + scale) + shift


def _ada_kernel(s_ref, w_ref, b_ref, o_ref):
    s = s_ref[...]
    s = s * jax.nn.sigmoid(s)
    o_ref[0] = jnp.dot(s, w_ref[0], preferred_element_type=F32,
                       precision=lax.Precision.HIGHEST) + b_ref[0]


def _adaln(cond, ada_w, ada_b):
    depth, d, n6 = ada_w.shape
    bn = n6 // 4
    return pl.pallas_call(
        _ada_kernel,
        grid=(depth, n6 // bn),
        in_specs=[pl.BlockSpec((8, d), lambda i, j: (0, 0)),
                  pl.BlockSpec((1, d, bn), lambda i, j: (i, 0, j)),
                  pl.BlockSpec((1, 1, bn), lambda i, j: (i, 0, j))],
        out_specs=pl.BlockSpec((1, 8, bn), lambda i, j: (i, 0, j)),
        out_shape=jax.ShapeDtypeStruct((depth, 8, n6), F32),
        compiler_params=_cparams(("parallel", "parallel")),
        name="adaln",
    )(cond, ada_w, ada_b.reshape(depth, 1, n6))


def _gelu(z):
    return 0.5 * z * (1.0 + lax.erf(z * np.float32(np.sqrt(0.5))))


def _gmlp_kernel(x_ref, m_ref, g_ref, win_ref, gv_ref, ws_ref, bs_ref, wout_ref, o_ref):
    x = x_ref[...]
    m = m_ref[0]
    a = gv_ref.shape[1]
    h = _modulate(x, g_ref[...], m[0:1], m[1:2]).astype(BF16)
    z = _gelu(_dot(h, win_ref[...]))
    u = z[:, :a]
    v = _rms(z[:, a:], gv_ref[...]).astype(BF16)
    gw = a // A_GROUPS
    rows = []
    for c in range(x.shape[0] // CHUNK_A):
        cols = [_dot(ws_ref[g], v[c * CHUNK_A:(c + 1) * CHUNK_A, g * gw:(g + 1) * gw])
                for g in range(A_GROUPS)]
        rows.append(jnp.concatenate(cols, axis=1) + bs_ref[...])
    s = jnp.concatenate(rows, axis=0)
    y = _dot((u * s).astype(BF16), wout_ref[...])
    o_ref[...] = x + m[2:3] * y


def _gmlp_mixer(xs, mods, g, w_in, g_v, w_s, b_s, w_out, tile0, n_tiles, geo):
    d = xs.shape[1]
    a = g_v.shape[0]
    gw = a // A_GROUPS
    bias = jnp.repeat(b_s.T, gw, axis=1)
    grp = functools.partial(_tile_group, geo=geo, tile0=tile0)
    const2 = lambda t: (0, 0)
    return pl.pallas_call(
        _gmlp_kernel,
        grid=(n_tiles,),
        in_specs=[pl.BlockSpec((TM, d), lambda t: (t + tile0, 0)),
                  pl.BlockSpec((1, 6, d), lambda t: (grp(t), 0, 0)),
                  pl.BlockSpec((1, d), const2),
                  pl.BlockSpec((d, 2 * a), const2),
                  pl.BlockSpec((1, a), const2),
                  pl.BlockSpec((A_GROUPS, CHUNK_A, CHUNK_A), lambda t: (0, 0, 0)),
                  pl.BlockSpec((CHUNK_A, a), const2),
                  pl.BlockSpec((a, d), const2)],
        out_specs=pl.BlockSpec((TM, d), lambda t: (t, 0)),
        out_shape=jax.ShapeDtypeStruct((n_tiles * TM, d), F32),
        compiler_params=_cparams(("parallel",)),
        name="gmlp_mixer",
    )(xs, mods, g.reshape(1, d), w_in.astype(BF16), g_v.reshape(1, a), w_s.astype(BF16),
      bias, w_out.astype(BF16))


def _tile_group(t, geo, tile0=0):
    n_ctx_tiles, tiles_per_batch, nb = geo
    tt = t + tile0
    return jnp.where(tt < n_ctx_tiles, nb, (tt - n_ctx_tiles) // tiles_per_batch)


def _dot_f32(tri_bf16, x):
    hi = x.astype(BF16)
    r1 = x - hi.astype(F32)
    mid = r1.astype(BF16)
    lo = (r1 - mid.astype(F32)).astype(BF16)
    return _dot(tri_bf16, hi) + _dot(tri_bf16, mid) + _dot(tri_bf16, lo)


def _gla_proj_kernel(x_ref, m_ref, g_ref, wq_ref, wk_ref, wv_ref, wg_ref, wr_ref, wa_ref, ba_ref,
                     cos_ref, sin_ref, q_ref, k_ref, v_ref, go_ref, la_ref):
    x = x_ref[...]
    m = m_ref[0]
    h = _modulate(x, g_ref[...], m[0:1], m[1:2]).astype(BF16)
    kdim = wq_ref.shape[1]
    hk = kdim // GLA_HEADS
    q = _dot(h, wq_ref[...]) * np.float32(hk ** -0.5)
    k = _dot(h, wk_ref[...])
    v_ref[...] = _dot(h, wv_ref[...]).astype(BF16)
    go_ref[...] = _dot(h, wg_ref[...])
    r = _dot(h, wr_ref[...])
    z = _dot(r.astype(BF16), wa_ref[...]) + ba_ref[...]
    la_ref[...] = jax.nn.log_sigmoid(z) * np.float32(1.0 / GLA_TAU)

    cos = jnp.concatenate([cos_ref[...]] * GLA_HEADS, axis=1)
    sin = jnp.concatenate([sin_ref[...]] * GLA_HEADS, axis=1)
    nf = hk // 4
    lane = lax.broadcasted_iota(I32, q.shape, 1)
    first = (lane % (2 * nf)) < nf

    def rope(t):
        up = pltpu.roll(t, kdim - nf, 1)
        dn = pltpu.roll(t, nf, 1)
        return t * cos + jnp.where(first, up, dn) * sin

    q_ref[...] = rope(q)
    k_ref[...] = rope(k)


def _gla_scan_kernel(q_ref, k_ref, v_ref, la_ref, o_ref, st_ref, *, rev):
    s = pl.program_id(1)

    @pl.when(s == 0)
    def _():
        st_ref[...] = jnp.zeros_like(st_ref)

    tb, kdim = q_ref.shape
    hk = kdim // GLA_HEADS
    hv = v_ref.shape[1] // GLA_HEADS
    c = GLA_CHUNK
    ri = lax.broadcasted_iota(I32, (c, c), 0)
    ci = lax.broadcasted_iota(I32, (c, c), 1)
    keep = (ci >= ri) if rev else (ci <= ri)
    tri = keep.astype(BF16)
    order = range(tb // c)
    for n in (reversed(order) if rev else order):
        rows = slice(n * c, (n + 1) * c)
        cum = _dot_f32(tri, la_ref[rows, :])
        last = cum[0:1] if rev else cum[c - 1:c]
        q = q_ref[rows, :]
        k = k_ref[rows, :]
        q_dec = (q * jnp.exp(cum)).astype(BF16)
        k_inv = (k * jnp.exp(-cum)).astype(BF16)
        k_end = (k * jnp.exp(last - cum)).astype(BF16)
        dec = jnp.exp(last)
        for h in range(GLA_HEADS):
            ks = slice(h * hk, (h + 1) * hk)
            vs = slice(h * hv, (h + 1) * hv)
            att = jnp.where(keep, _dot_nt(q_dec[:, ks], k_inv[:, ks]), 0.0).astype(BF16)
            vh = v_ref[rows, vs]
            st = st_ref[h]
            o_ref[rows, vs] = _dot(att, vh) + _dot_nt(q_dec[:, ks], st.astype(BF16))
            st_ref[h] = st * dec[:, ks] + _dot_tn(vh, k_end[:, ks])


def _gla_out_kernel(x_ref, m_ref, of_ref, ob_ref, go_ref, gn_ref, wout_ref, o_ref):
    x = x_ref[...]
    m = m_ref[0]
    o = of_ref[...] + ob_ref[...]
    hv = gn_ref.shape[1]
    parts = [_rms(o[:, h * hv:(h + 1) * hv], gn_ref[...]) for h in range(GLA_HEADS)]
    o = jnp.concatenate(parts, axis=1)
    gate = go_ref[...]
    y = _dot((o * (gate * jax.nn.sigmoid(gate))).astype(BF16), wout_ref[...])
    o_ref[...] = x + m[2:3] * y


def _gla_mixer(xs, mods, g, w_in, w_a2, b_a, g_o, w_out, geo, dims):
    b, seq, lctx = dims
    nt, d = xs.shape
    n_tiles = nt // TM
    kdim = w_a2.shape[2]
    vdim = g_o.shape[0] * GLA_HEADS
    hk = kdim // GLA_HEADS
    nf = hk // 4
    wq = w_in[:, :kdim].astype(BF16)
    wk = w_in[:, kdim:2 * kdim].astype(BF16)
    wv = w_in[:, 2 * kdim:2 * kdim + vdim].astype(BF16)
    wg = w_in[:, 2 * kdim + vdim:2 * kdim + 2 * vdim].astype(BF16)
    wr = jnp.pad(w_in[:, 2 * kdim + 2 * vdim:], ((0, 0), (0, 128 - 2 * GLA_RANK))).astype(BF16)
    wa = jnp.zeros((128, 2 * kdim), F32)
    wa = wa.at[:GLA_RANK, :kdim].set(w_a2[0]).at[GLA_RANK:2 * GLA_RANK, kdim:].set(w_a2[1]).astype(BF16)
    ba = b_a.reshape(1, 2 * kdim)

    tpos = jnp.arange(seq)
    freqs = jnp.power(ROPE_BASE, -jnp.arange(nf, dtype=F32) / nf)
    ar = (tpos // GRID_W).astype(F32)[:, None] * freqs
    ac = (tpos % GRID_W).astype(F32)[:, None] * freqs
    cos = jnp.concatenate([jnp.cos(ar), jnp.cos(ar), jnp.cos(ac), jnp.cos(ac)], axis=1)
    sin = jnp.concatenate([-jnp.sin(ar), jnp.sin(ar), -jnp.sin(ac), jnp.sin(ac)], axis=1)
    cos = jnp.concatenate([jnp.ones((TM, hk), F32), cos], axis=0)
    sin = jnp.concatenate([jnp.zeros((TM, hk), F32), sin], axis=0)

    n_ctx_tiles, tiles_per_batch, _ = geo
    grp = functools.partial(_tile_group, geo=geo)

    def rope_blk(t):
        return (jnp.where(t < n_ctx_tiles, 0, 1 + (t - n_ctx_tiles) % tiles_per_batch), 0)

    const2 = lambda t: (0, 0)
    row = lambda t: (t, 0)
    q, k, v, go, la = pl.pallas_call(
        _gla_proj_kernel,
        grid=(n_tiles,),
        in_specs=[pl.BlockSpec((TM, d), row),
                  pl.BlockSpec((1, 6, d), lambda t: (grp(t), 0, 0)),
                  pl.BlockSpec((1, d), const2),
                  pl.BlockSpec((d, kdim), const2), pl.BlockSpec((d, kdim), const2),
                  pl.BlockSpec((d, vdim), const2), pl.BlockSpec((d, vdim), const2),
                  pl.BlockSpec((d, 128), const2), pl.BlockSpec((128, 2 * kdim), const2),
                  pl.BlockSpec((1, 2 * kdim), const2),
                  pl.BlockSpec((TM, hk), rope_blk), pl.BlockSpec((TM, hk), rope_blk)],
        out_specs=[pl.BlockSpec((TM, kdim), row), pl.BlockSpec((TM, kdim), row),
                   pl.BlockSpec((TM, vdim), row), pl.BlockSpec((TM, vdim), row),
                   pl.BlockSpec((TM, 2 * kdim), row)],
        out_shape=[jax.ShapeDtypeStruct((nt, kdim), F32), jax.ShapeDtypeStruct((nt, kdim), F32),
                   jax.ShapeDtypeStruct((nt, vdim), BF16), jax.ShapeDtypeStruct((nt, vdim), F32),
                   jax.ShapeDtypeStruct((nt, 2 * kdim), F32)],
        compiler_params=_cparams(("parallel",)),
        name="gla_proj",
    )(xs, mods, g.reshape(1, d), wq, wk, wv, wg, wr, wa, ba, cos, sin)

    tb = TM
    ctx_steps = lctx // tb
    lat_steps = seq // tb
    steps = ctx_steps + lat_steps

    def scan(rev):
        def blk(bi, s):
            if rev:
                cs, ls = ctx_steps - 1 - s, lat_steps - 1 - (s - ctx_steps)
            else:
                cs, ls = s, s - ctx_steps
            return jnp.where(s < ctx_steps, bi * ctx_steps + cs, b * ctx_steps + bi * lat_steps + ls)

        d_col = 1 if rev else 0
        return pl.pallas_call(
            functools.partial(_gla_scan_kernel, rev=rev),
            grid=(b, steps),
            in_specs=[pl.BlockSpec((tb, kdim), lambda bi, s: (blk(bi, s), 0)),
                      pl.BlockSpec((tb, kdim), lambda bi, s: (blk(bi, s), 0)),
                      pl.BlockSpec((tb, vdim), lambda bi, s: (blk(bi, s), 0)),
                      pl.BlockSpec((tb, kdim), lambda bi, s: (blk(bi, s), d_col))],
            out_specs=pl.BlockSpec((tb, vdim), lambda bi, s: (blk(bi, s), 0)),
            out_shape=jax.ShapeDtypeStruct((nt, vdim), F32),
            scratch_shapes=[pltpu.VMEM((GLA_HEADS, vdim // GLA_HEADS, hk), F32)],
            compiler_params=_cparams(("parallel", "arbitrary")),
            name="gla_scan_bwd" if rev else "gla_scan_fwd",
        )(q, k, v, la)

    o_f = scan(False)
    o_b = scan(True)

    return pl.pallas_call(
        _gla_out_kernel,
        grid=(n_tiles,),
        in_specs=[pl.BlockSpec((TM, d), row),
                  pl.BlockSpec((1, 6, d), lambda t: (grp(t), 0, 0)),
                  pl.BlockSpec((TM, vdim), row), pl.BlockSpec((TM, vdim), row),
                  pl.BlockSpec((TM, vdim), row),
                  pl.BlockSpec((1, vdim // GLA_HEADS), const2),
                  pl.BlockSpec((vdim, d), const2)],
        out_specs=pl.BlockSpec((TM, d), row),
        out_shape=jax.ShapeDtypeStruct((nt, d), F32),
        compiler_params=_cparams(("parallel",)),
        name="gla_out",
    )(xs, mods, o_f, o_b, go, g_o.reshape(1, -1), w_out.astype(BF16))


def _na_proj_kernel(x_ref, m_ref, g_ref, w_ref, q_ref, k_ref, v_ref):
    x = x_ref[...]
    m = m_ref[0]
    d = x.shape[1]
    h = _modulate(x, g_ref[...], m[0:1], m[1:2]).astype(BF16)
    qkv = _dot(h, w_ref[...])
    hd = d // NA_HEADS
    for p in range(q_ref.shape[0]):
        cs = slice(p * 128, (p + 1) * 128)
        q_ref[p] = (qkv[:, cs] * np.float32(hd ** -0.5)).astype(BF16)
        k_ref[p] = qkv[:, d + p * 128:d + (p + 1) * 128].astype(BF16)
        v_ref[p] = qkv[:, 2 * d + p * 128:2 * d + (p + 1) * 128].astype(BF16)


def _na_attn_kernel(tbl_ref, q_ref, k0_ref, k1_ref, k2_ref, k3_ref, v0_ref, v1_ref, v2_ref, v3_ref,
                    kc_ref, vc_ref, bias_ref, o_ref, s_ref, *, n_tiles):
    t = pl.program_id(2)
    typ = jnp.where(t == 0, 0, jnp.where(t == n_tiles - 1, 2, 1))
    q = q_ref[0]
    kw = jnp.concatenate([k0_ref[0], k1_ref[0], k2_ref[0], k3_ref[0]], axis=0)
    vw = jnp.concatenate([v0_ref[0], v1_ref[0], v2_ref[0], v3_ref[0]], axis=0)
    kc = kc_ref[0]
    vc = vc_ref[0]
    lane = lax.broadcasted_iota(I32, q.shape, 1)
    npair = NA_WROWS // 2
    outs = []
    for hh in range(2):
        sel = (lane < 64) if hh == 0 else (lane >= 64)
        qm = jnp.where(sel, q, jnp.zeros_like(q))
        s_ref[...] = _dot_nt(qm, kw)
        s_ctx = _dot_nt(qm, kc)
        for dr in range(NA_QROWS):
            for j in range(npair):
                e = tbl_ref[typ * (NA_QROWS * npair) + dr * npair + j]
                rs = slice(dr * GRID_W, (dr + 1) * GRID_W)
                cs = slice(j * 128, (j + 1) * 128)
                s_ref[rs, cs] = s_ref[rs, cs] + bias_ref[hh, e]
        s_lat = s_ref[...]
        mx = jnp.maximum(jnp.max(s_lat, axis=1, keepdims=True), jnp.max(s_ctx, axis=1, keepdims=True))
        e_lat = jnp.exp(s_lat - mx)
        e_ctx = jnp.exp(s_ctx - mx)
        inv = 1.0 / (jnp.sum(e_lat, axis=1, keepdims=True) + jnp.sum(e_ctx, axis=1, keepdims=True))
        outs.append(_dot((e_lat * inv).astype(BF16), vw) + _dot((e_ctx * inv).astype(BF16), vc))
    o_ref[...] = jnp.where(lane < 64, outs[0], outs[1]).astype(BF16)


def _na_out_kernel(x_ref, m_ref, a_ref, w_ref, o_ref):
    m = m_ref[0]
    o_ref[...] = x_ref[...] + m[2:3] * _dot(a_ref[...], w_ref[...])


def _na_tables(rows):
    npair = NA_WROWS // 2
    tbl = np.zeros((3, NA_QROWS, npair), np.int32)
    for typ, r_base in enumerate((0, NA_QROWS, rows - NA_QROWS)):
        w0 = int(np.clip(r_base - NA_KH // 2, 0, rows - NA_WROWS))
        for dr in range(NA_QROWS):
            r = r_base + dr
            r0 = int(np.clip(r - NA_KH // 2, 0, rows - NA_KH))
            for j in range(npair):
                kr = (w0 + 2 * j, w0 + 2 * j + 1)
                ok = [r0 <= x < r0 + NA_KH for x in kr]
                ri = [x - r + NA_KH - 1 for x in kr]
                if ok[0] and ok[1]:
                    e = ri[0]
                elif ok[0]:
                    e = 16 + ri[0]
                elif ok[1]:
                    e = 32 + ri[1]
                else:
                    e = 63
                tbl[typ, dr, j] = e
    return tbl.reshape(-1)


def _na_bias_table(rpb):
    nh = rpb.shape[0]
    qc = np.arange(GRID_W)
    cstart = np.clip(qc - NA_KW // 2, 0, GRID_W - NA_KW)
    kc = np.arange(GRID_W)
    ok = (kc[None, :] >= cstart[:, None]) & (kc[None, :] < cstart[:, None] + NA_KW)
    cidx = np.clip(kc[None, :] - qc[:, None] + NA_KW - 1, 0, 2 * NA_KW - 2)
    cb = jnp.where(ok[None, None], rpb[:, :, cidx], NEG_INF).astype(F32)
    neg = jnp.full((nh, 1, GRID_W, GRID_W), NEG_INF, F32)
    cb = jnp.concatenate([cb, neg], axis=1)
    negs = jnp.broadcast_to(neg, cb.shape)
    nxt = jnp.concatenate([cb[:, 1:], neg], axis=1)
    both = jnp.concatenate([cb, nxt], axis=-1)
    left = jnp.concatenate([cb, negs], axis=-1)
    right = jnp.concatenate([negs, cb], axis=-1)
    none = jnp.concatenate([negs, negs], axis=-1)
    return jnp.concatenate([both, left, right, none], axis=1)


def _na_mixer(xs, mods, g, w_qkv, rpb, w_out, geo, dims):
    b, seq, lctx = dims
    nt, d = xs.shape
    n_tiles_tok = nt // TM
    grp = functools.partial(_tile_group, geo=geo)
    npairs = d // 128
    const2 = lambda t: (0, 0)
    q, k, v = pl.pallas_call(
        _na_proj_kernel,
        grid=(n_tiles_tok,),
        in_specs=[pl.BlockSpec((TM, d), lambda t: (t, 0)),
                  pl.BlockSpec((1, 6, d), lambda t: (grp(t), 0, 0)),
                  pl.BlockSpec((1, d), const2),
                  pl.BlockSpec((d, 3 * d), const2)],
        out_specs=[pl.BlockSpec((npairs, TM, 128), lambda t: (0, t, 0))] * 3,
        out_shape=[jax.ShapeDtypeStruct((npairs, nt, 128), BF16)] * 3,
        compiler_params=_cparams(("parallel",)),
        name="na_proj",
    )(xs, mods, g.reshape(1, d), w_qkv.astype(BF16))

    rows = seq // GRID_W
    tq = NA_QROWS * GRID_W
    n_tiles = rows // NA_QROWS
    wb = 256
    nwin = NA_WROWS * GRID_W // wb
    lat0 = b * lctx
    tbl = jnp.asarray(_na_tables(rows))
    bias = _na_bias_table(rpb)

    def win(i):
        def f(p, bi, t, tbl_ref):
            w = jnp.clip(2 * t - 1, 0, seq // wb - nwin)
            return (p, (lat0 + bi * seq) // wb + w + i, 0)
        return f

    kv_specs = [pl.BlockSpec((1, wb, 128), win(i)) for i in range(nwin)]
    attn = pl.pallas_call(
        functools.partial(_na_attn_kernel, n_tiles=n_tiles),
        grid_spec=pltpu.PrefetchScalarGridSpec(
            num_scalar_prefetch=1,
            grid=(npairs, b, n_tiles),
            in_specs=[pl.BlockSpec((1, tq, 128), lambda p, bi, t, tr: (p, (lat0 + bi * seq) // tq + t, 0))]
                     + kv_specs + kv_specs
                     + [pl.BlockSpec((1, lctx, 128), lambda p, bi, t, tr: (p, bi, 0)),
                        pl.BlockSpec((1, lctx, 128), lambda p, bi, t, tr: (p, bi, 0)),
                        pl.BlockSpec((2, 64, GRID_W, 2 * GRID_W), lambda p, bi, t, tr: (p, 0, 0, 0))],
            out_specs=pl.BlockSpec((tq, 128), lambda p, bi, t, tr: (bi * n_tiles + t, p)),
            scratch_shapes=[pltpu.VMEM((tq, NA_WROWS * GRID_W), F32)]),
        out_shape=jax.ShapeDtypeStruct((b * seq, d), BF16),
        compiler_params=_cparams(("parallel", "parallel", "arbitrary")),
        name="na_attn",
    )(tbl, q, k, k, k, k, v, v, v, v, k, v, bias)

    n_lat_tiles = b * seq // TM
    tile0 = lat0 // TM
    grp_l = functools.partial(_tile_group, geo=geo, tile0=tile0)
    return pl.pallas_call(
        _na_out_kernel,
        grid=(n_lat_tiles,),
        in_specs=[pl.BlockSpec((TM, d), lambda t: (t + tile0, 0)),
                  pl.BlockSpec((1, 6, d), lambda t: (grp_l(t), 0, 0)),
                  pl.BlockSpec((TM, d), lambda t: (t, 0)),
                  pl.BlockSpec((d, d), const2)],
        out_specs=pl.BlockSpec((TM, d), lambda t: (t, 0)),
        out_shape=jax.ShapeDtypeStruct((b * seq, d), F32),
        compiler_params=_cparams(("parallel",)),
        name="na_out",
    )(xs, mods, attn, w_out.astype(BF16))


SC_CORES = 2
SC_SUBCORES = 16
SC_WORKERS = SC_CORES * SC_SUBCORES
SLOT_T = 512


def _sc_mesh():
    return plsc.VectorSubcoreMesh(core_axis_name="c", subcore_axis_name="s")


def _sc_chunk(per_worker, max_chunk):
    return max(c for c in range(8, max_chunk + 1, 8) if per_worker % c == 0)


def _sc_gather_rows(table, idx):
    dd = table.shape[1]
    bsz = idx.shape[0]
    per_w = bsz // SC_WORKERS
    assert per_w * SC_WORKERS == bsz
    chunk = _sc_chunk(per_w, 64)
    n_chunks = per_w // chunk

    def body(table_hbm, idx_hbm, out_hbm, idx_v, rows_v, sem):
        wid = lax.axis_index("s") * SC_CORES + lax.axis_index("c")
        pltpu.sync_copy(idx_hbm.at[wid], idx_v)
        base = wid * per_w

        def step(j, carry):
            pltpu.async_copy(table_hbm.at[idx_v.at[j]], rows_v, sem).wait()
            pltpu.sync_copy(rows_v, out_hbm.at[pl.ds(pl.multiple_of(base + j * chunk, 8), chunk)])
            return carry

        lax.fori_loop(0, n_chunks, step, 0)

    return pl.kernel(
        body, out_type=jax.ShapeDtypeStruct((bsz, dd), table.dtype), mesh=_sc_mesh(),
        scratch_types=[pltpu.VMEM((n_chunks, chunk), I32), pltpu.VMEM((chunk, dd), table.dtype),
                       pltpu.SemaphoreType.DMA],
        name="sc_gather_rows",
    )(table, idx.reshape(SC_WORKERS, n_chunks, chunk))


def _sc_scatter_rows(rows, dest, n_out):
    n, dd = rows.shape
    kk = dest.shape[0]
    per_w = n // SC_WORKERS
    assert per_w * SC_WORKERS == n
    chunk = _sc_chunk(per_w, 64)
    n_chunks = per_w // chunk
    dest_w = dest.reshape(kk, SC_WORKERS, n_chunks, chunk).transpose(1, 2, 0, 3)
    dest_w = dest_w.reshape(SC_WORKERS, n_chunks * kk, chunk)

    def body(rows_hbm, dest_hbm, out_hbm, idx_v, rows_v):
        wid = lax.axis_index("s") * SC_CORES + lax.axis_index("c")
        pltpu.sync_copy(dest_hbm.at[wid], idx_v)
        base = wid * per_w

        def step(j, carry):
            pltpu.sync_copy(rows_hbm.at[pl.ds(pl.multiple_of(base + j * chunk, 8), chunk)], rows_v)
            for k in range(kk):
                pltpu.sync_copy(rows_v, out_hbm.at[idx_v.at[j * kk + k]])
            return carry

        lax.fori_loop(0, n_chunks, step, 0)

    return pl.kernel(
        body, out_type=jax.ShapeDtypeStruct((n_out, dd), rows.dtype), mesh=_sc_mesh(),
        scratch_types=[pltpu.VMEM((n_chunks * kk, chunk), I32), pltpu.VMEM((chunk, dd), rows.dtype)],
        name="sc_scatter_rows",
    )(rows, dest_w)


def _slot_kernel(idx_ref, rank_ref, cnt_ref, dest_ref, be_ref, nu_ref):
    cnt = cnt_ref[...]
    ne = cnt.shape[0]
    padded = jnp.floor((cnt + (MOE_BM - 1.0)) * (1.0 / MOE_BM)) * MOE_BM
    acc = jnp.broadcast_to(padded, (ne, 128))
    row = lax.broadcasted_iota(I32, (ne, 128), 0)
    s = 1
    while s < ne:
        acc = acc + jnp.where(row >= s, pltpu.roll(acc, s, 0), 0.0)
        s *= 2
    pad_end = acc[:, 0:1]
    pad_start = pad_end - padded
    idx = idx_ref[...]
    e_iota = lax.broadcasted_iota(I32, (ne, idx.shape[1]), 0)
    starts = [jnp.sum(jnp.where(e_iota == idx[k:k + 1], pad_start, 0.0), axis=0, keepdims=True)
              for k in range(idx.shape[0])]
    dest_ref[...] = jnp.concatenate(starts, axis=0).astype(I32) + rank_ref[...]
    nbp = be_ref.shape[1]
    bstart = lax.broadcasted_iota(I32, (ne, nbp), 1).astype(F32) * MOE_BM
    done = jnp.sum((pad_end <= bstart).astype(F32), axis=0, keepdims=True)
    be_ref[...] = jnp.minimum(done, ne - 1.0).astype(I32)
    nu_ref[...] = jnp.broadcast_to(pad_end[ne - 1:ne] * (1.0 / MOE_BM), nu_ref.shape).astype(I32)


def _route_kernel(x_ref, m_ref, g_ref, wr_ref, br_ref, h_ref, idx_ref, gate_ref, rank_ref, cnt_ref, run_ref):
    t = pl.program_id(0)

    @pl.when(t == 0)
    def _():
        run_ref[...] = jnp.zeros_like(run_ref)

    x = x_ref[...]
    m = m_ref[0]
    h = _modulate(x, g_ref[...], m[3:4], m[4:5]).astype(BF16)
    bits = lax.bitcast_convert_type(h.astype(F32), jnp.uint32)
    half = bits.shape[1] // 2
    h_ref[...] = (bits[:, half:] & jnp.uint32(0xFFFF0000)) | (bits[:, :half] >> 16)
    logits = _dot_nt(wr_ref[...], h) + br_ref[...]
    ne, tm = logits.shape
    e_iota = lax.broadcasted_iota(I32, (ne, tm), 0)
    vals, idxs = [], []
    l = logits
    for _ in range(TOP_K):
        mk = jnp.max(l, axis=0, keepdims=True)
        ik = jnp.min(jnp.where(l == mk, e_iota, ne), axis=0, keepdims=True)
        vals.append(mk)
        idxs.append(ik)
        l = jnp.where(e_iota == ik, -jnp.inf, l)
    top_val = jnp.concatenate(vals, axis=0)
    ex = jnp.exp(top_val - vals[0])
    gate_ref[...] = ex / jnp.sum(ex, axis=0, keepdims=True)
    idx_ref[...] = jnp.concatenate(idxs, axis=0)

    hits = [e_iota == ik for ik in idxs]
    cnt = hits[0].astype(F32)
    for hk in hits[1:]:
        cnt = cnt + hk.astype(F32)
    si = lax.broadcasted_iota(I32, (tm, tm), 0)
    ti = lax.broadcasted_iota(I32, (tm, tm), 1)
    before = (si < ti).astype(BF16)
    total = _dot(cnt.astype(BF16), before) + run_ref[...]
    ranks = [jnp.sum(jnp.where(hk, total, 0.0), axis=0, keepdims=True) for hk in hits]
    rank_ref[...] = jnp.concatenate(ranks, axis=0).astype(I32)
    run_ref[...] = run_ref[...] + jnp.sum(cnt, axis=1, keepdims=True)
    cnt_ref[...] = run_ref[...]


def _expert_kernel(be_ref, nu_ref, x_ref, wgu_ref, bgu_ref, wd_ref, bd_ref, o_ref, wgu_bf, wd_bf, act_ref):
    blk = pl.program_id(0)
    f = wd_bf.shape[0]
    ch = 256
    used = blk < nu_ref[0]
    fresh = jnp.logical_or(blk == 0, be_ref[blk] != be_ref[jnp.maximum(blk - 1, 0)])

    @pl.when(jnp.logical_and(used, fresh))
    def _():
        for j in range(2 * f // ch):
            cs = slice(j * ch, (j + 1) * ch)
            wgu_bf[:, cs] = wgu_ref[0, 0, :, cs].astype(BF16)
        for j in range(wd_bf.shape[1] // ch):
            cs = slice(j * ch, (j + 1) * ch)
            wd_bf[:, cs] = wd_ref[0, 0, :, cs].astype(BF16)

    @pl.when(used)
    def _():
        w = x_ref[...]
        lo = lax.bitcast_convert_type(w << 16, F32).astype(BF16)
        hi = lax.bitcast_convert_type(w & jnp.uint32(0xFFFF0000), F32).astype(BF16)
        x = jnp.concatenate([lo, hi], axis=1)
        for j in range(f // ch):
            c0 = slice(j * ch, (j + 1) * ch)
            c1 = slice(f + j * ch, f + (j + 1) * ch)
            glu = _dot(x, wgu_bf[:, c0]) + bgu_ref[0, 0, :, c0]
            lin = _dot(x, wgu_bf[:, c1]) + bgu_ref[0, 0, :, c1]
            glu = jnp.minimum(glu, SWIGLU_LIMIT)
            lin = jnp.clip(lin, -SWIGLU_LIMIT, SWIGLU_LIMIT)
            act_ref[:, c0] = (glu * jax.nn.sigmoid(SWIGLU_ALPHA * glu) * (lin + 1.0)).astype(BF16)
        o_ref[...] = _dot(act_ref[...], wd_bf[...]) + bd_ref[0, 0]

    @pl.when(jnp.logical_not(used))
    def _():
        o_ref[...] = jnp.zeros_like(o_ref)


def _combine_kernel(x_ref, m_ref, y_ref, gate_ref, o_ref):
    m = m_ref[0]
    gate = gate_ref[...]
    f = gate[:, 0:1] * y_ref[0]
    for k in range(1, TOP_K):
        f = f + gate[:, k:k + 1] * y_ref[k]
    o_ref[...] = x_ref[...] + m[5:6] * f


def _moe_layer(xs, mods, g, layer, w_router, b_router, w_gu, b_gu, w_down, b_down, geo):
    n, d = xs.shape
    ne = w_router.shape[1]
    f = w_down.shape[2]
    n_tiles = n // TM
    grp = functools.partial(_tile_group, geo=geo)
    const2 = lambda t: (0, 0)
    h, idx, gate, rank, cnt = pl.pallas_call(
        _route_kernel,
        grid=(n_tiles,),
        in_specs=[pl.BlockSpec((TM, d), lambda t: (t, 0)),
                  pl.BlockSpec((1, 6, d), lambda t: (grp(t), 0, 0)),
                  pl.BlockSpec((1, d), const2),
                  pl.BlockSpec((ne, d), const2),
                  pl.BlockSpec((ne, 1), const2)],
        out_specs=[pl.BlockSpec((TM, d // 2), lambda t: (t, 0)),
                   pl.BlockSpec((TOP_K, TM), lambda t: (0, t)),
                   pl.BlockSpec((TOP_K, TM), lambda t: (0, t)),
                   pl.BlockSpec((TOP_K, TM), lambda t: (0, t)),
                   pl.BlockSpec((ne, 1), const2)],
        out_shape=[jax.ShapeDtypeStruct((n, d // 2), jnp.uint32),
                   jax.ShapeDtypeStruct((TOP_K, n), I32),
                   jax.ShapeDtypeStruct((TOP_K, n), F32),
                   jax.ShapeDtypeStruct((TOP_K, n), I32),
                   jax.ShapeDtypeStruct((ne, 1), F32)],
        scratch_shapes=[pltpu.VMEM((ne, 1), F32)],
        compiler_params=_cparams(("arbitrary",)),
        name="moe_route",
    )(xs, mods, g.reshape(1, d), w_router.T.astype(BF16), b_router.reshape(ne, 1))

    n_rows = n * TOP_K
    n_blocks = -(-n_rows // MOE_BM) + ne
    n_slots = n_blocks * MOE_BM
    nbp = -(-n_blocks // 128) * 128
    dest, block_e, n_used = pl.pallas_call(
        _slot_kernel,
        grid=(n // SLOT_T,),
        in_specs=[pl.BlockSpec((TOP_K, SLOT_T), lambda t: (0, t)),
                  pl.BlockSpec((TOP_K, SLOT_T), lambda t: (0, t)),
                  pl.BlockSpec((ne, 1), const2)],
        out_specs=[pl.BlockSpec((TOP_K, SLOT_T), lambda t: (0, t)),
                   pl.BlockSpec((1, nbp), const2),
                   pl.BlockSpec((1, 128), const2)],
        out_shape=[jax.ShapeDtypeStruct((TOP_K, n), I32),
                   jax.ShapeDtypeStruct((1, nbp), I32),
                   jax.ShapeDtypeStruct((1, 128), I32)],
        compiler_params=_cparams(("arbitrary",)),
        name="moe_slots",
    )(idx, rank, cnt)

    buf = _sc_scatter_rows(h, dest, n_slots)

    lw = lambda i, be, nu: (layer, be[i], 0, 0)
    y = pl.pallas_call(
        _expert_kernel,
        grid_spec=pltpu.PrefetchScalarGridSpec(
            num_scalar_prefetch=2,
            grid=(n_blocks,),
            in_specs=[pl.BlockSpec((MOE_BM, d // 2), lambda i, be, nu: (i, 0)),
                      pl.BlockSpec((1, 1, d, 2 * f), lw),
                      pl.BlockSpec((1, 1, 1, 2 * f), lw),
                      pl.BlockSpec((1, 1, f, d), lw),
                      pl.BlockSpec((1, 1, 1, d), lw)],
            out_specs=pl.BlockSpec((MOE_BM, d), lambda i, be, nu: (i, 0)),
            scratch_shapes=[pltpu.VMEM((d, 2 * f), BF16), pltpu.VMEM((f, d), BF16),
                            pltpu.VMEM((MOE_BM, f), BF16)]),
        out_shape=jax.ShapeDtypeStruct((n_slots, d), F32),
        compiler_params=_cparams(("arbitrary",)),
        name="moe_experts",
    )(block_e[0, :n_blocks], n_used[0, :1], buf, w_gu, b_gu.reshape(b_gu.shape[0], ne, 1, 2 * f),
      w_down, b_down.reshape(b_down.shape[0], ne, 1, d))

    yk = _sc_gather_rows(y, dest.reshape(-1)).reshape(TOP_K, n, d)
    return pl.pallas_call(
        _combine_kernel,
        grid=(n_tiles,),
        in_specs=[pl.BlockSpec((TM, d), lambda t: (t, 0)),
                  pl.BlockSpec((1, 6, d), lambda t: (grp(t), 0, 0)),
                  pl.BlockSpec((TOP_K, TM, d), lambda t: (0, t, 0)),
                  pl.BlockSpec((TM, TOP_K), lambda t: (t, 0))],
        out_specs=pl.BlockSpec((TM, d), lambda t: (t, 0)),
        out_shape=jax.ShapeDtypeStruct((n, d), F32),
        compiler_params=_cparams(("parallel",)),
        name="moe_combine",
    )(xs, mods, yk, gate.T)


def _final_kernel(x_ref, g_ref, o_ref):
    o_ref[...] = _rms(x_ref[...], g_ref[...])


def _final_norm(xs, g):
    n, d = xs.shape
    return pl.pallas_call(
        _final_kernel,
        grid=(n // TM,),
        in_specs=[pl.BlockSpec((TM, d), lambda t: (t, 0)), pl.BlockSpec((1, d), lambda t: (0, 0))],
        out_specs=pl.BlockSpec((TM, d), lambda t: (t, 0)),
        out_shape=jax.ShapeDtypeStruct((n, d), F32),
        compiler_params=_cparams(("parallel",)),
        name="final_norm",
    )(xs, g.reshape(1, d))


def kernel(x, c, ctx, c_ctx, ada_w, ada_b, norm_mix, norm_ffn, norm_out, a_w_in, a_g_v, a_w_s, a_b_s, a_w_out, b_w_in, b_w_a2, b_b_a, b_g_o, b_w_out, c_w_qkv, c_rpb, c_w_out, moe_w_router, moe_b_router, moe_w_gu, moe_b_gu, moe_w_down, moe_b_down):
    b, seq, d = x.shape
    lctx = ctx.shape[1]
    depth = ada_w.shape[0]
    assert lctx % TM == 0 and seq % (NA_QROWS * GRID_W) == 0 and (b * lctx) % (NA_QROWS * GRID_W) == 0
    assert seq // GRID_W >= NA_WROWS + NA_QROWS and b + 1 <= 8
    geo = (b * lctx // TM, seq // TM, b)
    dims = (b, seq, lctx)

    cond = jnp.zeros((8, d), F32).at[:b].set(c).at[b].set(c_ctx)
    mods_all = _adaln(cond, ada_w, ada_b)[:, :b + 1].reshape(depth, b + 1, 6, d)

    xs = jnp.concatenate([ctx.reshape(b * lctx, d), x.reshape(b * seq, d)], axis=0)
    ctx_tiles = geo[0]
    has_ctx = True
    for i in range(depth):
        kind, j = i % N_MIXERS, i // N_MIXERS
        ctx_later = any(kk % N_MIXERS != 0 for kk in range(i + 1, depth))
        mods = mods_all[i]
        if kind == 0:
            keep_ctx = has_ctx and ctx_later
            skip = 0 if keep_ctx or not has_ctx else ctx_tiles
            geo_i = geo if has_ctx else (0, geo[1], b)
            xs = _gmlp_mixer(xs, mods, norm_mix[i], a_w_in[j], a_g_v[j], a_w_s[j], a_b_s[j], a_w_out[j],
                             skip, xs.shape[0] // TM - skip, geo_i)
            has_ctx = keep_ctx
        elif kind == 1:
            assert has_ctx
            xs = _gla_mixer(xs, mods, norm_mix[i], b_w_in[j], b_w_a2[j], b_b_a[j], b_g_o[j], b_w_out[j], geo, dims)
            if not ctx_later:
                xs = xs[b * lctx:]
                has_ctx = False
        else:
            assert has_ctx
            if ctx_later:
                raise NotImplementedError("context output of the neighbourhood mixer")
            xs = _na_mixer(xs, mods, norm_mix[i], c_w_qkv[j], c_rpb[j], c_w_out[j], geo, dims)
            has_ctx = False
        geo_i = geo if has_ctx else (0, geo[1], b)
        xs = _moe_layer(xs, mods, norm_ffn[i], i, moe_w_router[i], moe_b_router[i], moe_w_gu, moe_b_gu,
                        moe_w_down, moe_b_down, geo_i)
    if has_ctx:
        xs = xs[b * lctx:]
    return _final_norm(xs, norm_out).reshape(b, seq, d)
```

```python
import functools

import numpy as np
import jax
import jax.numpy as jnp
from jax import lax
from jax.experimental import pallas as pl
from jax.experimental.pallas import tpu as pltpu
from jax.experimental.pallas import tpu_sc as plsc

F32 = jnp.float32
BF16 = jnp.bfloat16
I32 = jnp.int32

NORM_EPS = 1e-6
GRID_W = 64
N_MIXERS = 3

CHUNK_A = 128
A_GROUPS = 8
GLA_HEADS = 4
GLA_RANK = 16
GLA_TAU = 16.0
GLA_CHUNK = 64
ROPE_BASE = 10000.0
NA_HEADS = 16
NA_KH = 8
NA_KW = 16
NEG_INF = -1e30
NA_QROWS = 8
NA_WROWS = 16
TOP_K = 4
SWIGLU_LIMIT = 7.0
SWIGLU_ALPHA = 1.702
MOE_BM = 512

TM = 256
VMEM_LIMIT = 56 * 1024 * 1024


def _cparams(sem):
    return pltpu.CompilerParams(dimension_semantics=sem, vmem_limit_bytes=VMEM_LIMIT)


def _dot(a, b):
    return jnp.dot(a, b, preferred_element_type=F32)


def _dot_nt(a, b):
    return lax.dot_general(a, b, (((1,), (1,)), ((), ())), preferred_element_type=F32)


def _dot_tn(a, b):
    return lax.dot_general(a, b, (((0,), (0,)), ((), ())), preferred_element_type=F32)


def _rms(x, g):
    return x * lax.rsqrt(jnp.mean(x * x, axis=-1, keepdims=True) + NORM_EPS) * g


def _modulate(x, g, shift, scale):
    return _rms(x, g) * (1.0 + scale) + shift


def _ada_kernel(s_ref, w_ref, b_ref, o_ref):
    s = s_ref[...]
    s = s * jax.nn.sigmoid(s)
    o_ref[0] = jnp.dot(s, w_ref[0], preferred_element_type=F32,
                       precision=lax.Precision.HIGHEST) + b_ref[0]


def _adaln(cond, ada_w, ada_b):
    depth, d, n6 = ada_w.shape
    bn = n6 // 4
    return pl.pallas_call(
        _ada_kernel,
        grid=(depth, n6 // bn),
        in_specs=[pl.BlockSpec((8, d), lambda i, j: (0, 0)),
                  pl.BlockSpec((1, d, bn), lambda i, j: (i, 0, j)),
                  pl.BlockSpec((1, 1, bn), lambda i, j: (i, 0, j))],
        out_specs=pl.BlockSpec((1, 8, bn), lambda i, j: (i, 0, j)),
        out_shape=jax.ShapeDtypeStruct((depth, 8, n6), F32),
        compiler_params=_cparams(("parallel", "parallel")),
        name="adaln",
    )(cond, ada_w, ada_b.reshape(depth, 1, n6))


def _gelu(z):
    return 0.5 * z * (1.0 + lax.erf(z * np.float32(np.sqrt(0.5))))


def _gmlp_kernel(x_ref, m_ref, g_ref, win_ref, gv_ref, ws_ref, bs_ref, wout_ref, o_ref):
    x = x_ref[...]
    m = m_ref[0]
    a = gv_ref.shape[1]
    h = _modulate(x, g_ref[...], m[0:1], m[1:2]).astype(BF16)
    z = _gelu(_dot(h, win_ref[...]))
    u = z[:, :a]
    v = _rms(z[:, a:], gv_ref[...]).astype(BF16)
    gw = a // A_GROUPS
    rows = []
    for c in range(x.shape[0] // CHUNK_A):
        cols = [_dot(ws_ref[g], v[c * CHUNK_A:(c + 1) * CHUNK_A, g * gw:(g + 1) * gw])
                for g in range(A_GROUPS)]
        rows.append(jnp.concatenate(cols, axis=1) + bs_ref[...])
    s = jnp.concatenate(rows, axis=0)
    y = _dot((u * s).astype(BF16), wout_ref[...])
    o_ref[...] = x + m[2:3] * y


def _gmlp_mixer(xs, mods, g, w_in, g_v, w_s, b_s, w_out, tile0, n_tiles, geo):
    d = xs.shape[1]
    a = g_v.shape[0]
    gw = a // A_GROUPS
    bias = jnp.repeat(b_s.T, gw, axis=1)
    grp = functools.partial(_tile_group, geo=geo, tile0=tile0)
    const2 = lambda t: (0, 0)
    return pl.pallas_call(
        _gmlp_kernel,
        grid=(n_tiles,),
        in_specs=[pl.BlockSpec((TM, d), lambda t: (t + tile0, 0)),
                  pl.BlockSpec((1, 6, d), lambda t: (grp(t), 0, 0)),
                  pl.BlockSpec((1, d), const2),
                  pl.BlockSpec((d, 2 * a), const2),
                  pl.BlockSpec((1, a), const2),
                  pl.BlockSpec((A_GROUPS, CHUNK_A, CHUNK_A), lambda t: (0, 0, 0)),
                  pl.BlockSpec((CHUNK_A, a), const2),
                  pl.BlockSpec((a, d), const2)],
        out_specs=pl.BlockSpec((TM, d), lambda t: (t, 0)),
        out_shape=jax.ShapeDtypeStruct((n_tiles * TM, d), F32),
        compiler_params=_cparams(("parallel",)),
        name="gmlp_mixer",
    )(xs, mods, g.reshape(1, d), w_in.astype(BF16), g_v.reshape(1, a), w_s.astype(BF16),
      bias, w_out.astype(BF16))


def _tile_group(t, geo, tile0=0):
    n_ctx_tiles, tiles_per_batch, nb = geo
    tt = t + tile0
    return jnp.where(tt < n_ctx_tiles, nb, (tt - n_ctx_tiles) // tiles_per_batch)


def _dot_f32(tri_bf16, x):
    hi = x.astype(BF16)
    r1 = x - hi.astype(F32)
    mid = r1.astype(BF16)
    lo = (r1 - mid.astype(F32)).astype(BF16)
    return _dot(tri_bf16, hi) + _dot(tri_bf16, mid) + _dot(tri_bf16, lo)


def _gla_proj_kernel(x_ref, m_ref, g_ref, wq_ref, wk_ref, wv_ref, wg_ref, wr_ref, wa_ref, ba_ref,
                     cos_ref, sin_ref, q_ref, k_ref, v_ref, go_ref, la_ref):
    x = x_ref[...]
    m = m_ref[0]
    h = _modulate(x, g_ref[...], m[0:1], m[1:2]).astype(BF16)
    kdim = wq_ref.shape[1]
    hk = kdim // GLA_HEADS
    q = _dot(h, wq_ref[...]) * np.float32(hk ** -0.5)
    k = _dot(h, wk_ref[...])
    v_ref[...] = _dot(h, wv_ref[...]).astype(BF16)
    go_ref[...] = _dot(h, wg_ref[...])
    r = _dot(h, wr_ref[...])
    z = _dot(r.astype(BF16), wa_ref[...]) + ba_ref[...]
    la_ref[...] = jax.nn.log_sigmoid(z) * np.float32(1.0 / GLA_TAU)

    cos = jnp.concatenate([cos_ref[...]] * GLA_HEADS, axis=1)
    sin = jnp.concatenate([sin_ref[...]] * GLA_HEADS, axis=1)
    nf = hk // 4
    lane = lax.broadcasted_iota(I32, q.shape, 1)
    first = (lane % (2 * nf)) < nf

    def rope(t):
        up = pltpu.roll(t, kdim - nf, 1)
        dn = pltpu.roll(t, nf, 1)
        return t * cos + jnp.where(first, up, dn) * sin

    q_ref[...] = rope(q)
    k_ref[...] = rope(k)


def _gla_scan_kernel(q_ref, k_ref, v_ref, la_ref, o_ref, st_ref, *, rev):
    s = pl.program_id(1)

    @pl.when(s == 0)
    def _():
        st_ref[...] = jnp.zeros_like(st_ref)

    tb, kdim = q_ref.shape
    hk = kdim // GLA_HEADS
    hv = v_ref.shape[1] // GLA_HEADS
    c = GLA_CHUNK
    ri = lax.broadcasted_iota(I32, (c, c), 0)
    ci = lax.broadcasted_iota(I32, (c, c), 1)
    keep = (ci >= ri) if rev else (ci <= ri)
    tri = keep.astype(BF16)
    order = range(tb // c)
    for n in (reversed(order) if rev else order):
        rows = slice(n * c, (n + 1) * c)
        cum = _dot_f32(tri, la_ref[rows, :])
        last = cum[0:1] if rev else cum[c - 1:c]
        q = q_ref[rows, :]
        k = k_ref[rows, :]
        q_dec = (q * jnp.exp(cum)).astype(BF16)
        k_inv = (k * jnp.exp(-cum)).astype(BF16)
        k_end = (k * jnp.exp(last - cum)).astype(BF16)
        dec = jnp.exp(last)
        for h in range(GLA_HEADS):
            ks = slice(h * hk, (h + 1) * hk)
            vs = slice(h * hv, (h + 1) * hv)
            att = jnp.where(keep, _dot_nt(q_dec[:, ks], k_inv[:, ks]), 0.0).astype(BF16)
            vh = v_ref[rows, vs]
            st = st_ref[h]
            o_ref[rows, vs] = _dot(att, vh) + _dot_nt(q_dec[:, ks], st.astype(BF16))
            st_ref[h] = st * dec[:, ks] + _dot_tn(vh, k_end[:, ks])


def _gla_out_kernel(x_ref, m_ref, of_ref, ob_ref, go_ref, gn_ref, wout_ref, o_ref):
    x = x_ref[...]
    m = m_ref[0]
    o = of_ref[...] + ob_ref[...]
    hv = gn_ref.shape[1]
    parts = [_rms(o[:, h * hv:(h + 1) * hv], gn_ref[...]) for h in range(GLA_HEADS)]
    o = jnp.concatenate(parts, axis=1)
    gate = go_ref[...]
    y = _dot((o * (gate * jax.nn.sigmoid(gate))).astype(BF16), wout_ref[...])
    o_ref[...] = x + m[2:3] * y


def _gla_mixer(xs, mods, g, w_in, w_a2, b_a, g_o, w_out, geo, dims):
    b, seq, lctx = dims
    nt, d = xs.shape
    n_tiles = nt // TM
    kdim = w_a2.shape[2]
    vdim = g_o.shape[0] * GLA_HEADS
    hk = kdim // GLA_HEADS
    nf = hk // 4
    wq = w_in[:, :kdim].astype(BF16)
    wk = w_in[:, kdim:2 * kdim].astype(BF16)
    wv = w_in[:, 2 * kdim:2 * kdim + vdim].astype(BF16)
    wg = w_in[:, 2 * kdim + vdim:2 * kdim + 2 * vdim].astype(BF16)
    wr = jnp.pad(w_in[:, 2 * kdim + 2 * vdim:], ((0, 0), (0, 128 - 2 * GLA_RANK))).astype(BF16)
    wa = jnp.zeros((128, 2 * kdim), F32)
    wa = wa.at[:GLA_RANK, :kdim].set(w_a2[0]).at[GLA_RANK:2 * GLA_RANK, kdim:].set(w_a2[1]).astype(BF16)
    ba = b_a.reshape(1, 2 * kdim)

    tpos = jnp.arange(seq)
    freqs = jnp.power(ROPE_BASE, -jnp.arange(nf, dtype=F32) / nf)
    ar = (tpos // GRID_W).astype(F32)[:, None] * freqs
    ac = (tpos % GRID_W).astype(F32)[:, None] * freqs
    cos = jnp.concatenate([jnp.cos(ar), jnp.cos(ar), jnp.cos(ac), jnp.cos(ac)], axis=1)
    sin = jnp.concatenate([-jnp.sin(ar), jnp.sin(ar), -jnp.sin(ac), jnp.sin(ac)], axis=1)
    cos = jnp.concatenate([jnp.ones((TM, hk), F32), cos], axis=0)
    sin = jnp.concatenate([jnp.zeros((TM, hk), F32), sin], axis=0)

    n_ctx_tiles, tiles_per_batch, _ = geo
    grp = functools.partial(_tile_group, geo=geo)

    def rope_blk(t):
        return (jnp.where(t < n_ctx_tiles, 0, 1 + (t - n_ctx_tiles) % tiles_per_batch), 0)

    const2 = lambda t: (0, 0)
    row = lambda t: (t, 0)
    q, k, v, go, la = pl.pallas_call(
        _gla_proj_kernel,
        grid=(n_tiles,),
        in_specs=[pl.BlockSpec((TM, d), row),
                  pl.BlockSpec((1, 6, d), lambda t: (grp(t), 0, 0)),
                  pl.BlockSpec((1, d), const2),
                  pl.BlockSpec((d, kdim), const2), pl.BlockSpec((d, kdim), const2),
                  pl.BlockSpec((d, vdim), const2), pl.BlockSpec((d, vdim), const2),
                  pl.BlockSpec((d, 128), const2), pl.BlockSpec((128, 2 * kdim), const2),
                  pl.BlockSpec((1, 2 * kdim), const2),
                  pl.BlockSpec((TM, hk), rope_blk), pl.BlockSpec((TM, hk), rope_blk)],
        out_specs=[pl.BlockSpec((TM, kdim), row), pl.BlockSpec((TM, kdim), row),
                   pl.BlockSpec((TM, vdim), row), pl.BlockSpec((TM, vdim), row),
                   pl.BlockSpec((TM, 2 * kdim), row)],
        out_shape=[jax.ShapeDtypeStruct((nt, kdim), F32), jax.ShapeDtypeStruct((nt, kdim), F32),
                   jax.ShapeDtypeStruct((nt, vdim), BF16), jax.ShapeDtypeStruct((nt, vdim), F32),
                   jax.ShapeDtypeStruct((nt, 2 * kdim), F32)],
        compiler_params=_cparams(("parallel",)),
        name="gla_proj",
    )(xs, mods, g.reshape(1, d), wq, wk, wv, wg, wr, wa, ba, cos, sin)

    tb = TM
    ctx_steps = lctx // tb
    lat_steps = seq // tb
    steps = ctx_steps + lat_steps

    def scan(rev):
        def blk(bi, s):
            if rev:
                cs, ls = ctx_steps - 1 - s, lat_steps - 1 - (s - ctx_steps)
            else:
                cs, ls = s, s - ctx_steps
            return jnp.where(s < ctx_steps, bi * ctx_steps + cs, b * ctx_steps + bi * lat_steps + ls)

        d_col = 1 if rev else 0
        return pl.pallas_call(
            functools.partial(_gla_scan_kernel, rev=rev),
            grid=(b, steps),
            in_specs=[pl.BlockSpec((tb, kdim), lambda bi, s: (blk(bi, s), 0)),
                      pl.BlockSpec((tb, kdim), lambda bi, s: (blk(bi, s), 0)),
                      pl.BlockSpec((tb, vdim), lambda bi, s: (blk(bi, s), 0)),
                      pl.BlockSpec((tb, kdim), lambda bi, s: (blk(bi, s), d_col))],
            out_specs=pl.BlockSpec((tb, vdim), lambda bi, s: (blk(bi, s), 0)),
            out_shape=jax.ShapeDtypeStruct((nt, vdim), F32),
            scratch_shapes=[pltpu.VMEM((GLA_HEADS, vdim // GLA_HEADS, hk), F32)],
            compiler_params=_cparams(("parallel", "arbitrary")),
            name="gla_scan_bwd" if rev else "gla_scan_fwd",
        )(q, k, v, la)

    o_f = scan(False)
    o_b = scan(True)

    return pl.pallas_call(
        _gla_out_kernel,
        grid=(n_tiles,),
        in_specs=[pl.BlockSpec((TM, d), row),
                  pl.BlockSpec((1, 6, d), lambda t: (grp(t), 0, 0)),
                  pl.BlockSpec((TM, vdim), row), pl.BlockSpec((TM, vdim), row),
                  pl.BlockSpec((TM, vdim), row),
                  pl.BlockSpec((1, vdim // GLA_HEADS), const2),
                  pl.BlockSpec((vdim, d), const2)],
        out_specs=pl.BlockSpec((TM, d), row),
        out_shape=jax.ShapeDtypeStruct((nt, d), F32),
        compiler_params=_cparams(("parallel",)),
        name="gla_out",
    )(xs, mods, o_f, o_b, go, g_o.reshape(1, -1), w_out.astype(BF16))


def _na_proj_kernel(x_ref, m_ref, g_ref, w_ref, q_ref, k_ref, v_ref):
    x = x_ref[...]
    m = m_ref[0]
    d = x.shape[1]
    h = _modulate(x, g_ref[...], m[0:1], m[1:2]).astype(BF16)
    qkv = _dot(h, w_ref[...])
    hd = d // NA_HEADS
    for p in range(q_ref.shape[0]):
        cs = slice(p * 128, (p + 1) * 128)
        q_ref[p] = (qkv[:, cs] * np.float32(hd ** -0.5)).astype(BF16)
        k_ref[p] = qkv[:, d + p * 128:d + (p + 1) * 128].astype(BF16)
        v_ref[p] = qkv[:, 2 * d + p * 128:2 * d + (p + 1) * 128].astype(BF16)


def _na_attn_kernel(tbl_ref, q_ref, k0_ref, k1_ref, k2_ref, k3_ref, v0_ref, v1_ref, v2_ref, v3_ref,
                    kc_ref, vc_ref, bias_ref, o_ref, s_ref, *, n_tiles):
    t = pl.program_id(2)
    typ = jnp.where(t == 0, 0, jnp.where(t == n_tiles - 1, 2, 1))
    q = q_ref[0]
    kw = jnp.concatenate([k0_ref[0], k1_ref[0], k2_ref[0], k3_ref[0]], axis=0)
    vw = jnp.concatenate([v0_ref[0], v1_ref[0], v2_ref[0], v3_ref[0]], axis=0)
    kc = kc_ref[0]
    vc = vc_ref[0]
    lane = lax.broadcasted_iota(I32, q.shape, 1)
    npair = NA_WROWS // 2
    outs = []
    for hh in range(2):
        sel = (lane < 64) if hh == 0 else (lane >= 64)
        qm = jnp.where(sel, q, jnp.zeros_like(q))
        s_ref[...] = _dot_nt(qm, kw)
        s_ctx = _dot_nt(qm, kc)
        for dr in range(NA_QROWS):
            for j in range(npair):
                e = tbl_ref[typ * (NA_QROWS * npair) + dr * npair + j]
                rs = slice(dr * GRID_W, (dr + 1) * GRID_W)
                cs = slice(j * 128, (j + 1) * 128)
                s_ref[rs, cs] = s_ref[rs, cs] + bias_ref[hh, e]
        s_lat = s_ref[...]
        mx = jnp.maximum(jnp.max(s_lat, axis=1, keepdims=True), jnp.max(s_ctx, axis=1, keepdims=True))
        e_lat = jnp.exp(s_lat - mx)
        e_ctx = jnp.exp(s_ctx - mx)
        inv = 1.0 / (jnp.sum(e_lat, axis=1, keepdims=True) + jnp.sum(e_ctx, axis=1, keepdims=True))
        outs.append(_dot((e_lat * inv).astype(BF16), vw) + _dot((e_ctx * inv).astype(BF16), vc))
    o_ref[...] = jnp.where(lane < 64, outs[0], outs[1]).astype(BF16)


def _na_out_kernel(x_ref, m_ref, a_ref, w_ref, o_ref):
    m = m_ref[0]
    o_ref[...] = x_ref[...] + m[2:3] * _dot(a_ref[...], w_ref[...])


def _na_tables(rows):
    npair = NA_WROWS // 2
    tbl = np.zeros((3, NA_QROWS, npair), np.int32)
    for typ, r_base in enumerate((0, NA_QROWS, rows - NA_QROWS)):
        w0 = int(np.clip(r_base - NA_KH // 2, 0, rows - NA_WROWS))
        for dr in range(NA_QROWS):
            r = r_base + dr
            r0 = int(np.clip(r - NA_KH // 2, 0, rows - NA_KH))
            for j in range(npair):
                kr = (w0 + 2 * j, w0 + 2 * j + 1)
                ok = [r0 <= x < r0 + NA_KH for x in kr]
                ri = [x - r + NA_KH - 1 for x in kr]
                if ok[0] and ok[1]:
                    e = ri[0]
                elif ok[0]:
                    e = 16 + ri[0]
                elif ok[1]:
                    e = 32 + ri[1]
                else:
                    e = 63
                tbl[typ, dr, j] = e
    return tbl.reshape(-1)


def _na_bias_table(rpb):
    nh = rpb.shape[0]
    qc = np.arange(GRID_W)
    cstart = np.clip(qc - NA_KW // 2, 0, GRID_W - NA_KW)
    kc = np.arange(GRID_W)
    ok = (kc[None, :] >= cstart[:, None]) & (kc[None, :] < cstart[:, None] + NA_KW)
    cidx = np.clip(kc[None, :] - qc[:, None] + NA_KW - 1, 0, 2 * NA_KW - 2)
    cb = jnp.where(ok[None, None], rpb[:, :, cidx], NEG_INF).astype(F32)
    neg = jnp.full((nh, 1, GRID_W, GRID_W), NEG_INF, F32)
    cb = jnp.concatenate([cb, neg], axis=1)
    negs = jnp.broadcast_to(neg, cb.shape)
    nxt = jnp.concatenate([cb[:, 1:], neg], axis=1)
    both = jnp.concatenate([cb, nxt], axis=-1)
    left = jnp.concatenate([cb, negs], axis=-1)
    right = jnp.concatenate([negs, cb], axis=-1)
    none = jnp.concatenate([negs, negs], axis=-1)
    return jnp.concatenate([both, left, right, none], axis=1)


def _na_mixer(xs, mods, g, w_qkv, rpb, w_out, geo, dims):
    b, seq, lctx = dims
    nt, d = xs.shape
    n_tiles_tok = nt // TM
    grp = functools.partial(_tile_group, geo=geo)
    npairs = d // 128
    const2 = lambda t: (0, 0)
    q, k, v = pl.pallas_call(
        _na_proj_kernel,
        grid=(n_tiles_tok,),
        in_specs=[pl.BlockSpec((TM, d), lambda t: (t, 0)),
                  pl.BlockSpec((1, 6, d), lambda t: (grp(t), 0, 0)),
                  pl.BlockSpec((1, d), const2),
                  pl.BlockSpec((d, 3 * d), const2)],
        out_specs=[pl.BlockSpec((npairs, TM, 128), lambda t: (0, t, 0))] * 3,
        out_shape=[jax.ShapeDtypeStruct((npairs, nt, 128), BF16)] * 3,
        compiler_params=_cparams(("parallel",)),
        name="na_proj",
    )(xs, mods, g.reshape(1, d), w_qkv.astype(BF16))

    rows = seq // GRID_W
    tq = NA_QROWS * GRID_W
    n_tiles = rows // NA_QROWS
    wb = 256
    nwin = NA_WROWS * GRID_W // wb
    lat0 = b * lctx
    tbl = jnp.asarray(_na_tables(rows))
    bias = _na_bias_table(rpb)

    def win(i):
        def f(p, bi, t, tbl_ref):
            w = jnp.clip(2 * t - 1, 0, seq // wb - nwin)
            return (p, (lat0 + bi * seq) // wb + w + i, 0)
        return f

    kv_specs = [pl.BlockSpec((1, wb, 128), win(i)) for i in range(nwin)]
    attn = pl.pallas_call(
        functools.partial(_na_attn_kernel, n_tiles=n_tiles),
        grid_spec=pltpu.PrefetchScalarGridSpec(
            num_scalar_prefetch=1,
            grid=(npairs, b, n_tiles),
            in_specs=[pl.BlockSpec((1, tq, 128), lambda p, bi, t, tr: (p, (lat0 + bi * seq) // tq + t, 0))]
                     + kv_specs + kv_specs
                     + [pl.BlockSpec((1, lctx, 128), lambda p, bi, t, tr: (p, bi, 0)),
                        pl.BlockSpec((1, lctx, 128), lambda p, bi, t, tr: (p, bi, 0)),
                        pl.BlockSpec((2, 64, GRID_W, 2 * GRID_W), lambda p, bi, t, tr: (p, 0, 0, 0))],
            out_specs=pl.BlockSpec((tq, 128), lambda p, bi, t, tr: (bi * n_tiles + t, p)),
            scratch_shapes=[pltpu.VMEM((tq, NA_WROWS * GRID_W), F32)]),
        out_shape=jax.ShapeDtypeStruct((b * seq, d), BF16),
        compiler_params=_cparams(("parallel", "parallel", "arbitrary")),
        name="na_attn",
    )(tbl, q, k, k, k, k, v, v, v, v, k, v, bias)

    n_lat_tiles = b * seq // TM
    tile0 = lat0 // TM
    grp_l = functools.partial(_tile_group, geo=geo, tile0=tile0)
    return pl.pallas_call(
        _na_out_kernel,
        grid=(n_lat_tiles,),
        in_specs=[pl.BlockSpec((TM, d), lambda t: (t + tile0, 0)),
                  pl.BlockSpec((1, 6, d), lambda t: (grp_l(t), 0, 0)),
                  pl.BlockSpec((TM, d), lambda t: (t, 0)),
                  pl.BlockSpec((d, d), const2)],
        out_specs=pl.BlockSpec((TM, d), lambda t: (t, 0)),
        out_shape=jax.ShapeDtypeStruct((b * seq, d), F32),
        compiler_params=_cparams(("parallel",)),
        name="na_out",
    )(xs, mods, attn, w_out.astype(BF16))


SC_CORES = 2
SC_SUBCORES = 16
SC_WORKERS = SC_CORES * SC_SUBCORES
SLOT_T = 512


def _sc_mesh():
    return plsc.VectorSubcoreMesh(core_axis_name="c", subcore_axis_name="s")


def _sc_chunk(per_worker, max_chunk):
    return max(c for c in range(8, max_chunk + 1, 8) if per_worker % c == 0)


def _sc_gather_rows(table, idx):
    dd = table.shape[1]
    bsz = idx.shape[0]
    per_w = bsz // SC_WORKERS
    assert per_w * SC_WORKERS == bsz
    chunk = _sc_chunk(per_w, 64)
    n_chunks = per_w // chunk

    def body(table_hbm, idx_hbm, out_hbm, idx_v, rows_v, sem):
        wid = lax.axis_index("s") * SC_CORES + lax.axis_index("c")
        pltpu.sync_copy(idx_hbm.at[wid], idx_v)
        base = wid * per_w

        def step(j, carry):
            pltpu.async_copy(table_hbm.at[idx_v.at[j]], rows_v, sem).wait()
            pltpu.sync_copy(rows_v, out_hbm.at[pl.ds(pl.multiple_of(base + j * chunk, 8), chunk)])
            return carry

        lax.fori_loop(0, n_chunks, step, 0)

    return pl.kernel(
        body, out_type=jax.ShapeDtypeStruct((bsz, dd), table.dtype), mesh=_sc_mesh(),
        scratch_types=[pltpu.VMEM((n_chunks, chunk), I32), pltpu.VMEM((chunk, dd), table.dtype),
                       pltpu.SemaphoreType.DMA],
        name="sc_gather_rows",
    )(table, idx.reshape(SC_WORKERS, n_chunks, chunk))


def _sc_scatter_rows(rows, dest, n_out):
    n, dd = rows.shape
    kk = dest.shape[0]
    per_w = n // SC_WORKERS
    assert per_w * SC_WORKERS == n
    chunk = _sc_chunk(per_w, 64)
    n_chunks = per_w // chunk
    dest_w = dest.reshape(kk, SC_WORKERS, n_chunks, chunk).transpose(1, 2, 0, 3)
    dest_w = dest_w.reshape(SC_WORKERS, n_chunks * kk, chunk)

    def body(rows_hbm, dest_hbm, out_hbm, idx_v, rows_v):
        wid = lax.axis_index("s") * SC_CORES + lax.axis_index("c")
        pltpu.sync_copy(dest_hbm.at[wid], idx_v)
        base = wid * per_w

        def step(j, carry):
            pltpu.sync_copy(rows_hbm.at[pl.ds(pl.multiple_of(base + j * chunk, 8), chunk)], rows_v)
            for k in range(kk):
                pltpu.sync_copy(rows_v, out_hbm.at[idx_v.at[j * kk + k]])
            return carry

        lax.fori_loop(0, n_chunks, step, 0)

    return pl.kernel(
        body, out_type=jax.ShapeDtypeStruct((n_out, dd), rows.dtype), mesh=_sc_mesh(),
        scratch_types=[pltpu.VMEM((n_chunks * kk, chunk), I32), pltpu.VMEM((chunk, dd), rows.dtype)],
        name="sc_scatter_rows",
    )(rows, dest_w)


def _slot_kernel(idx_ref, rank_ref, cnt_ref, dest_ref, be_ref, nv_ref, *, bm):
    cnt = cnt_ref[...]
    ne = cnt.shape[0]
    padded = jnp.floor((cnt + (bm - 1.0)) * (1.0 / bm)) * bm
    acc = jnp.broadcast_to(padded, (ne, 128))
    row = lax.broadcasted_iota(I32, (ne, 128), 0)
    s = 1
    while s < ne:
        acc = acc + jnp.where(row >= s, pltpu.roll(acc, s, 0), 0.0)
        s *= 2
    pad_end = acc[:, 0:1]
    pad_start = pad_end - padded
    idx = idx_ref[...]
    e_iota = lax.broadcasted_iota(I32, (ne, idx.shape[1]), 0)
    starts = [jnp.sum(jnp.where(e_iota == idx[k:k + 1], pad_start, 0.0), axis=0, keepdims=True)
              for k in range(idx.shape[0])]
    dest_ref[...] = jnp.concatenate(starts, axis=0).astype(I32) + rank_ref[...]
    nbp = be_ref.shape[1]
    bstart = lax.broadcasted_iota(I32, (ne, nbp), 1).astype(F32) * bm
    done = jnp.sum((pad_end <= bstart).astype(F32), axis=0, keepdims=True)
    be = jnp.minimum(done, ne - 1.0)
    be_ref[...] = be.astype(I32)
    mine = lax.broadcasted_iota(I32, (ne, nbp), 0).astype(F32) == be
    live_end = jnp.sum(jnp.where(mine, pad_start + cnt, 0.0), axis=0, keepdims=True)
    nv_ref[...] = jnp.clip(live_end - bstart[0:1], 0.0, bm).astype(I32)


def _route_kernel(x_ref, m_ref, g_ref, wr_ref, br_ref, h_ref, idx_ref, gate_ref, rank_ref, cnt_ref, run_ref):
    t = pl.program_id(0)

    @pl.when(t == 0)
    def _():
        run_ref[...] = jnp.zeros_like(run_ref)

    x = x_ref[...]
    m = m_ref[0]
    h = _modulate(x, g_ref[...], m[3:4], m[4:5]).astype(BF16)
    h_ref[...] = _pack_bf16_pairs(h)
    logits = _dot_nt(wr_ref[...], h) + br_ref[...]
    ne, tm = logits.shape
    e_iota = lax.broadcasted_iota(I32, (ne, tm), 0)
    vals, idxs = [], []
    l = logits
    for _ in range(TOP_K):
        mk = jnp.max(l, axis=0, keepdims=True)
        ik = jnp.min(jnp.where(l == mk, e_iota, ne), axis=0, keepdims=True)
        vals.append(mk)
        idxs.append(ik)
        l = jnp.where(e_iota == ik, -jnp.inf, l)
    top_val = jnp.concatenate(vals, axis=0)
    ex = jnp.exp(top_val - vals[0])
    gate_ref[...] = ex / jnp.sum(ex, axis=0, keepdims=True)
    idx_ref[...] = jnp.concatenate(idxs, axis=0)

    hits = [e_iota == ik for ik in idxs]
    cnt = hits[0].astype(F32)
    for hk in hits[1:]:
        cnt = cnt + hk.astype(F32)
    si = lax.broadcasted_iota(I32, (tm, tm), 0)
    ti = lax.broadcasted_iota(I32, (tm, tm), 1)
    before = (si < ti).astype(BF16)
    total = _dot(cnt.astype(BF16), before) + run_ref[...]
    ranks = [jnp.sum(jnp.where(hk, total, 0.0), axis=0, keepdims=True) for hk in hits]
    rank_ref[...] = jnp.concatenate(ranks, axis=0).astype(I32)
    run_ref[...] = run_ref[...] + jnp.sum(cnt, axis=1, keepdims=True)
    cnt_ref[...] = run_ref[...]


def _pack_bf16_pairs(v):
    bits = lax.bitcast_convert_type(v.astype(BF16).astype(F32), jnp.uint32)
    half = bits.shape[1] // 2
    return (bits[:, half:] & jnp.uint32(0xFFFF0000)) | (bits[:, :half] >> 16)


def _unpack_bf16_pairs(w):
    return (lax.bitcast_convert_type(w << 16, F32),
            lax.bitcast_convert_type(w & jnp.uint32(0xFFFF0000), F32))


def _expert_kernel(be_ref, nv_ref, x_ref, wgu_ref, bgu_ref, wd_ref, bd_ref, o_ref, wgu_bf, wd_bf, act_ref):
    blk = pl.program_id(0)
    ch = 256
    live = nv_ref[blk]
    used = live > 0
    fresh = jnp.logical_or(blk == 0, be_ref[blk] != be_ref[jnp.maximum(blk - 1, 0)])

    @pl.when(jnp.logical_and(used, fresh))
    def _():
        for j in range(wgu_bf.shape[1] // ch):
            cs = slice(j * ch, (j + 1) * ch)
            wgu_bf[:, cs] = wgu_ref[0, 0, :, cs].astype(BF16)
        for j in range(wd_bf.shape[1] // ch):
            cs = slice(j * ch, (j + 1) * ch)
            wd_bf[:, cs] = wd_ref[0, 0, :, cs].astype(BF16)

    f = act_ref.shape[1]

    @pl.when(used)
    def _():
        w = x_ref[...]
        row = lax.broadcasted_iota(I32, w.shape, 0)
        w = jnp.where(row < live, w, jnp.zeros_like(w))
        lo, hi = _unpack_bf16_pairs(w)
        x = jnp.concatenate([lo.astype(BF16), hi.astype(BF16)], axis=1)
        for j in range(f // ch):
            c0 = slice(j * ch, (j + 1) * ch)
            c1 = slice(f + j * ch, f + (j + 1) * ch)
            glu = _dot(x, wgu_bf[:, c0]) + bgu_ref[0, 0, :, c0]
            lin = _dot(x, wgu_bf[:, c1]) + bgu_ref[0, 0, :, c1]
            glu = jnp.minimum(glu, SWIGLU_LIMIT)
            lin = jnp.clip(lin, -SWIGLU_LIMIT, SWIGLU_LIMIT)
            act_ref[:, c0] = (glu * jax.nn.sigmoid(SWIGLU_ALPHA * glu) * (lin + 1.0)).astype(BF16)
        o_ref[...] = _pack_bf16_pairs(_dot(act_ref[...], wd_bf[...]) + bd_ref[0, 0])

    @pl.when(jnp.logical_not(used))
    def _():
        o_ref[...] = jnp.zeros_like(o_ref)


def _combine_kernel(x_ref, m_ref, y_ref, gate_ref, o_ref):
    m = m_ref[0]
    gate = gate_ref[...]
    half = y_ref.shape[2]
    f_lo, f_hi = None, None
    for k in range(TOP_K):
        lo, hi = _unpack_bf16_pairs(y_ref[k])
        gk = gate[:, k:k + 1]
        f_lo = gk * lo if f_lo is None else f_lo + gk * lo
        f_hi = gk * hi if f_hi is None else f_hi + gk * hi
    o_ref[:, :half] = x_ref[:, :half] + m[5:6, :half] * f_lo
    o_ref[:, half:] = x_ref[:, half:] + m[5:6, half:] * f_hi


def _moe_layer(xs, mods, g, layer, w_router, b_router, w_gu, b_gu, w_down, b_down, geo):
    n, d = xs.shape
    ne = w_router.shape[1]
    f = w_down.shape[2]
    n_tiles = n // TM
    grp = functools.partial(_tile_group, geo=geo)
    const2 = lambda t: (0, 0)
    h, idx, gate, rank, cnt = pl.pallas_call(
        _route_kernel,
        grid=(n_tiles,),
        in_specs=[pl.BlockSpec((TM, d), lambda t: (t, 0)),
                  pl.BlockSpec((1, 6, d), lambda t: (grp(t), 0, 0)),
                  pl.BlockSpec((1, d), const2),
                  pl.BlockSpec((ne, d), const2),
                  pl.BlockSpec((ne, 1), const2)],
        out_specs=[pl.BlockSpec((TM, d // 2), lambda t: (t, 0)),
                   pl.BlockSpec((TOP_K, TM), lambda t: (0, t)),
                   pl.BlockSpec((TOP_K, TM), lambda t: (0, t)),
                   pl.BlockSpec((TOP_K, TM), lambda t: (0, t)),
                   pl.BlockSpec((ne, 1), const2)],
        out_shape=[jax.ShapeDtypeStruct((n, d // 2), jnp.uint32),
                   jax.ShapeDtypeStruct((TOP_K, n), I32),
                   jax.ShapeDtypeStruct((TOP_K, n), F32),
                   jax.ShapeDtypeStruct((TOP_K, n), I32),
                   jax.ShapeDtypeStruct((ne, 1), F32)],
        scratch_shapes=[pltpu.VMEM((ne, 1), F32)],
        compiler_params=_cparams(("arbitrary",)),
        name="moe_route",
    )(xs, mods, g.reshape(1, d), w_router.T.astype(BF16), b_router.reshape(ne, 1))

    bm = MOE_BM
    n_rows = n * TOP_K
    n_blocks = -(-n_rows // bm) + ne
    n_slots = n_blocks * bm
    nbp = -(-n_blocks // 128) * 128
    dest, block_e, n_live = pl.pallas_call(
        functools.partial(_slot_kernel, bm=bm),
        grid=(n // SLOT_T,),
        in_specs=[pl.BlockSpec((TOP_K, SLOT_T), lambda t: (0, t)),
                  pl.BlockSpec((TOP_K, SLOT_T), lambda t: (0, t)),
                  pl.BlockSpec((ne, 1), const2)],
        out_specs=[pl.BlockSpec((TOP_K, SLOT_T), lambda t: (0, t)),
                   pl.BlockSpec((1, nbp), const2),
                   pl.BlockSpec((1, nbp), const2)],
        out_shape=[jax.ShapeDtypeStruct((TOP_K, n), I32),
                   jax.ShapeDtypeStruct((1, nbp), I32),
                   jax.ShapeDtypeStruct((1, nbp), I32)],
        compiler_params=_cparams(("arbitrary",)),
        name="moe_slots",
    )(idx, rank, cnt)

    buf = _sc_scatter_rows(h, dest, n_slots)

    scratch = [pltpu.VMEM((d, 2 * f), BF16), pltpu.VMEM((f, d), BF16), pltpu.VMEM((bm, f), BF16)]
    lw = lambda i, be, nv: (layer, be[i], 0, 0)
    lb = lw
    dy = d // 2
    y = pl.pallas_call(
        _expert_kernel,
        grid_spec=pltpu.PrefetchScalarGridSpec(
            num_scalar_prefetch=2,
            grid=(n_blocks,),
            in_specs=[pl.BlockSpec((bm, d // 2), lambda i, be, nv: (i, 0)),
                      pl.BlockSpec((1, 1, d, 2 * f), lw),
                      pl.BlockSpec((1, 1, 1, 2 * f), lb),
                      pl.BlockSpec((1, 1, f, d), lw),
                      pl.BlockSpec((1, 1, 1, d), lb)],
            out_specs=pl.BlockSpec((bm, dy), lambda i, be, nv: (i, 0)),
            scratch_shapes=scratch),
        out_shape=jax.ShapeDtypeStruct((n_slots, dy), jnp.uint32),
        compiler_params=_cparams(("arbitrary",)),
        name="moe_experts",
    )(block_e[0, :n_blocks], n_live[0, :n_blocks], buf, w_gu, b_gu.reshape(b_gu.shape[0], ne, 1, 2 * f),
      w_down, b_down.reshape(b_down.shape[0], ne, 1, d))

    yk = _sc_gather_rows(y, dest.reshape(-1)).reshape(TOP_K, n, dy)
    return pl.pallas_call(
        _combine_kernel,
        grid=(n_tiles,),
        in_specs=[pl.BlockSpec((TM, d), lambda t: (t, 0)),
                  pl.BlockSpec((1, 6, d), lambda t: (grp(t), 0, 0)),
                  pl.BlockSpec((TOP_K, TM, dy), lambda t: (0, t, 0)),
                  pl.BlockSpec((TM, TOP_K), lambda t: (t, 0))],
        out_specs=pl.BlockSpec((TM, d), lambda t: (t, 0)),
        out_shape=jax.ShapeDtypeStruct((n, d), F32),
        compiler_params=_cparams(("parallel",)),
        name="moe_combine",
    )(xs, mods, yk, gate.T)


def _final_kernel(x_ref, g_ref, o_ref):
    o_ref[...] = _rms(x_ref[...], g_ref[...])


def _final_norm(xs, g):
    n, d = xs.shape
    return pl.pallas_call(
        _final_kernel,
        grid=(n // TM,),
        in_specs=[pl.BlockSpec((TM, d), lambda t: (t, 0)), pl.BlockSpec((1, d), lambda t: (0, 0))],
        out_specs=pl.BlockSpec((TM, d), lambda t: (t, 0)),
        out_shape=jax.ShapeDtypeStruct((n, d), F32),
        compiler_params=_cparams(("parallel",)),
        name="final_norm",
    )(xs, g.reshape(1, d))


def kernel(x, c, ctx, c_ctx, ada_w, ada_b, norm_mix, norm_ffn, norm_out, a_w_in, a_g_v, a_w_s, a_b_s, a_w_out, b_w_in, b_w_a2, b_b_a, b_g_o, b_w_out, c_w_qkv, c_rpb, c_w_out, moe_w_router, moe_b_router, moe_w_gu, moe_b_gu, moe_w_down, moe_b_down):
    b, seq, d = x.shape
    lctx = ctx.shape[1]
    depth = ada_w.shape[0]
    assert lctx % TM == 0 and seq % (NA_QROWS * GRID_W) == 0 and (b * lctx) % (NA_QROWS * GRID_W) == 0
    assert seq // GRID_W >= NA_WROWS + NA_QROWS and b + 1 <= 8
    geo = (b * lctx // TM, seq // TM, b)
    dims = (b, seq, lctx)

    cond = jnp.zeros((8, d), F32).at[:b].set(c).at[b].set(c_ctx)
    mods_all = _adaln(cond, ada_w, ada_b)[:, :b + 1].reshape(depth, b + 1, 6, d)

    xs = jnp.concatenate([ctx.reshape(b * lctx, d), x.reshape(b * seq, d)], axis=0)
    ctx_tiles = geo[0]
    has_ctx = True
    for i in range(depth):
        kind, j = i % N_MIXERS, i // N_MIXERS
        ctx_later = any(kk % N_MIXERS != 0 for kk in range(i + 1, depth))
        mods = mods_all[i]
        if kind == 0:
            keep_ctx = has_ctx and ctx_later
            skip = 0 if keep_ctx or not has_ctx else ctx_tiles
            geo_i = geo if has_ctx else (0, geo[1], b)
            xs = _gmlp_mixer(xs, mods, norm_mix[i], a_w_in[j], a_g_v[j], a_w_s[j], a_b_s[j], a_w_out[j],
                             skip, xs.shape[0] // TM - skip, geo_i)
            has_ctx = keep_ctx
        elif kind == 1:
            assert has_ctx
            xs = _gla_mixer(xs, mods, norm_mix[i], b_w_in[j], b_w_a2[j], b_b_a[j], b_g_o[j], b_w_out[j], geo, dims)
            if not ctx_later:
                xs = xs[b * lctx:]
                has_ctx = False
        else:
            assert has_ctx
            if ctx_later:
                raise NotImplementedError("context output of the neighbourhood mixer")
            xs = _na_mixer(xs, mods, norm_mix[i], c_w_qkv[j], c_rpb[j], c_w_out[j], geo, dims)
            has_ctx = False
        geo_i = geo if has_ctx else (0, geo[1], b)
        xs = _moe_layer(xs, mods, norm_ffn[i], i, moe_w_router[i], moe_b_router[i], moe_w_gu, moe_b_gu,
                        moe_w_down, moe_b_down, geo_i)
    if has_ctx:
        xs = xs[b * lctx:]
    return _final_norm(xs, norm_out).reshape(b, seq, d)
```

```python
import functools

import numpy as np
import jax
import jax.numpy as jnp
from jax import lax
from jax.experimental import pallas as pl
from jax.experimental.pallas import tpu as pltpu
from jax.experimental.pallas import tpu_sc as plsc

F32 = jnp.float32
BF16 = jnp.bfloat16
I32 = jnp.int32

NORM_EPS = 1e-6
GRID_W = 64
N_MIXERS = 3

CHUNK_A = 128
A_GROUPS = 8
GLA_HEADS = 4
GLA_RANK = 16
GLA_TAU = 16.0
GLA_CHUNK = 64
ROPE_BASE = 10000.0
NA_HEADS = 16
NA_KH = 8
NA_KW = 16
NEG_INF = -1e30
NA_QROWS = 8
NA_WROWS = 16
TOP_K = 4
SWIGLU_LIMIT = 7.0
SWIGLU_ALPHA = 1.702
MOE_BM = 256

TM = 256
VMEM_LIMIT = 56 * 1024 * 1024


def _cparams(sem):
    return pltpu.CompilerParams(dimension_semantics=sem, vmem_limit_bytes=VMEM_LIMIT)


def _dot(a, b):
    return jnp.dot(a, b, preferred_element_type=F32)


def _dot_nt(a, b):
    return lax.dot_general(a, b, (((1,), (1,)), ((), ())), preferred_element_type=F32)


def _dot_tn(a, b):
    return lax.dot_general(a, b, (((0,), (0,)), ((), ())), preferred_element_type=F32)


def _rms(x, g):
    return x * lax.rsqrt(jnp.mean(x * x, axis=-1, keepdims=True) + NORM_EPS) * g


def _modulate(x, g, shift, scale):
    return _rms(x, g) * (1.0 + scale) + shift


def _ada_kernel(s_ref, w_ref, b_ref, o_ref):
    s = s_ref[...]
    s = s * jax.nn.sigmoid(s)
    o_ref[0] = jnp.dot(s, w_ref[0], preferred_element_type=F32,
                       precision=lax.Precision.HIGHEST) + b_ref[0]


def _adaln(cond, ada_w, ada_b):
    depth, d, n6 = ada_w.shape
    bn = n6 // 4
    return pl.pallas_call(
        _ada_kernel,
        grid=(depth, n6 // bn),
        in_specs=[pl.BlockSpec((8, d), lambda i, j: (0, 0)),
                  pl.BlockSpec((1, d, bn), lambda i, j: (i, 0, j)),
                  pl.BlockSpec((1, 1, bn), lambda i, j: (i, 0, j))],
        out_specs=pl.BlockSpec((1, 8, bn), lambda i, j: (i, 0, j)),
        out_shape=jax.ShapeDtypeStruct((depth, 8, n6), F32),
        compiler_params=_cparams(("parallel", "parallel")),
        name="adaln",
    )(cond, ada_w, ada_b.reshape(depth, 1, n6))


def _gelu(z):
    return 0.5 * z * (1.0 + lax.erf(z * np.float32(np.sqrt(0.5))))


def _gmlp_kernel(x_ref, m_ref, g_ref, win_ref, gv_ref, ws_ref, bs_ref, wout_ref, o_ref):
    x = x_ref[...]
    m = m_ref[0]
    a = gv_ref.shape[1]
    h = _modulate(x, g_ref[...], m[0:1], m[1:2]).astype(BF16)
    z = _gelu(_dot(h, win_ref[...]))
    u = z[:, :a]
    v = _rms(z[:, a:], gv_ref[...]).astype(BF16)
    gw = a // A_GROUPS
    rows = []
    for c in range(x.shape[0] // CHUNK_A):
        cols = [_dot(ws_ref[g], v[c * CHUNK_A:(c + 1) * CHUNK_A, g * gw:(g + 1) * gw])
                for g in range(A_GROUPS)]
        rows.append(jnp.concatenate(cols, axis=1) + bs_ref[...])
    s = jnp.concatenate(rows, axis=0)
    y = _dot((u * s).astype(BF16), wout_ref[...])
    o_ref[...] = x + m[2:3] * y


def _gmlp_mixer(xs, mods, g, w_in, g_v, w_s, b_s, w_out, tile0, n_tiles, geo):
    d = xs.shape[1]
    a = g_v.shape[0]
    gw = a // A_GROUPS
    bias = jnp.repeat(b_s.T, gw, axis=1)
    grp = functools.partial(_tile_group, geo=geo, tile0=tile0)
    const2 = lambda t: (0, 0)
    return pl.pallas_call(
        _gmlp_kernel,
        grid=(n_tiles,),
        in_specs=[pl.BlockSpec((TM, d), lambda t: (t + tile0, 0)),
                  pl.BlockSpec((1, 6, d), lambda t: (grp(t), 0, 0)),
                  pl.BlockSpec((1, d), const2),
                  pl.BlockSpec((d, 2 * a), const2),
                  pl.BlockSpec((1, a), const2),
                  pl.BlockSpec((A_GROUPS, CHUNK_A, CHUNK_A), lambda t: (0, 0, 0)),
                  pl.BlockSpec((CHUNK_A, a), const2),
                  pl.BlockSpec((a, d), const2)],
        out_specs=pl.BlockSpec((TM, d), lambda t: (t, 0)),
        out_shape=jax.ShapeDtypeStruct((n_tiles * TM, d), F32),
        compiler_params=_cparams(("parallel",)),
        name="gmlp_mixer",
    )(xs, mods, g.reshape(1, d), w_in.astype(BF16), g_v.reshape(1, a), w_s.astype(BF16),
      bias, w_out.astype(BF16))


def _tile_group(t, geo, tile0=0):
    n_ctx_tiles, tiles_per_batch, nb = geo
    tt = t + tile0
    return jnp.where(tt < n_ctx_tiles, nb, (tt - n_ctx_tiles) // tiles_per_batch)


def _dot_f32(tri_bf16, x):
    hi = x.astype(BF16)
    r1 = x - hi.astype(F32)
    mid = r1.astype(BF16)
    lo = (r1 - mid.astype(F32)).astype(BF16)
    return _dot(tri_bf16, hi) + _dot(tri_bf16, mid) + _dot(tri_bf16, lo)


def _gla_proj_kernel(x_ref, m_ref, g_ref, wq_ref, wk_ref, wv_ref, wg_ref, wr_ref, wa_ref, ba_ref,
                     cos_ref, sin_ref, q_ref, k_ref, v_ref, go_ref, la_ref):
    x = x_ref[...]
    m = m_ref[0]
    h = _modulate(x, g_ref[...], m[0:1], m[1:2]).astype(BF16)
    kdim = wq_ref.shape[1]
    hk = kdim // GLA_HEADS
    q = _dot(h, wq_ref[...]) * np.float32(hk ** -0.5)
    k = _dot(h, wk_ref[...])
    v_ref[...] = _dot(h, wv_ref[...]).astype(BF16)
    go_ref[...] = _dot(h, wg_ref[...])
    r = _dot(h, wr_ref[...])
    z = _dot(r.astype(BF16), wa_ref[...]) + ba_ref[...]
    la_ref[...] = jax.nn.log_sigmoid(z) * np.float32(1.0 / GLA_TAU)

    cos = jnp.concatenate([cos_ref[...]] * GLA_HEADS, axis=1)
    sin = jnp.concatenate([sin_ref[...]] * GLA_HEADS, axis=1)
    nf = hk // 4
    lane = lax.broadcasted_iota(I32, q.shape, 1)
    first = (lane % (2 * nf)) < nf

    def rope(t):
        up = pltpu.roll(t, kdim - nf, 1)
        dn = pltpu.roll(t, nf, 1)
        return t * cos + jnp.where(first, up, dn) * sin

    q_ref[...] = rope(q)
    k_ref[...] = rope(k)


def _gla_scan_kernel(q_ref, k_ref, v_ref, la_ref, o_ref, st_ref, *, rev):
    s = pl.program_id(1)

    @pl.when(s == 0)
    def _():
        st_ref[...] = jnp.zeros_like(st_ref)

    tb, kdim = q_ref.shape
    hk = kdim // GLA_HEADS
    hv = v_ref.shape[1] // GLA_HEADS
    c = GLA_CHUNK
    ri = lax.broadcasted_iota(I32, (c, c), 0)
    ci = lax.broadcasted_iota(I32, (c, c), 1)
    keep = (ci >= ri) if rev else (ci <= ri)
    tri = keep.astype(BF16)
    order = range(tb // c)
    for n in (reversed(order) if rev else order):
        rows = slice(n * c, (n + 1) * c)
        cum = _dot_f32(tri, la_ref[rows, :])
        last = cum[0:1] if rev else cum[c - 1:c]
        q = q_ref[rows, :]
        k = k_ref[rows, :]
        q_dec = (q * jnp.exp(cum)).astype(BF16)
        k_inv = (k * jnp.exp(-cum)).astype(BF16)
        k_end = (k * jnp.exp(last - cum)).astype(BF16)
        dec = jnp.exp(last)
        for h in range(GLA_HEADS):
            ks = slice(h * hk, (h + 1) * hk)
            vs = slice(h * hv, (h + 1) * hv)
            att = jnp.where(keep, _dot_nt(q_dec[:, ks], k_inv[:, ks]), 0.0).astype(BF16)
            vh = v_ref[rows, vs]
            st = st_ref[h]
            o_ref[rows, vs] = _dot(att, vh) + _dot_nt(q_dec[:, ks], st.astype(BF16))
            st_ref[h] = st * dec[:, ks] + _dot_tn(vh, k_end[:, ks])


def _gla_out_kernel(x_ref, m_ref, of_ref, ob_ref, go_ref, gn_ref, wout_ref, o_ref):
    x = x_ref[...]
    m = m_ref[0]
    o = of_ref[...] + ob_ref[...]
    hv = gn_ref.shape[1]
    parts = [_rms(o[:, h * hv:(h + 1) * hv], gn_ref[...]) for h in range(GLA_HEADS)]
    o = jnp.concatenate(parts, axis=1)
    gate = go_ref[...]
    y = _dot((o * (gate * jax.nn.sigmoid(gate))).astype(BF16), wout_ref[...])
    o_ref[...] = x + m[2:3] * y


def _gla_mixer(xs, mods, g, w_in, w_a2, b_a, g_o, w_out, geo, dims):
    b, seq, lctx = dims
    nt, d = xs.shape
    n_tiles = nt // TM
    kdim = w_a2.shape[2]
    vdim = g_o.shape[0] * GLA_HEADS
    hk = kdim // GLA_HEADS
    nf = hk // 4
    wq = w_in[:, :kdim].astype(BF16)
    wk = w_in[:, kdim:2 * kdim].astype(BF16)
    wv = w_in[:, 2 * kdim:2 * kdim + vdim].astype(BF16)
    wg = w_in[:, 2 * kdim + vdim:2 * kdim + 2 * vdim].astype(BF16)
    wr = jnp.pad(w_in[:, 2 * kdim + 2 * vdim:], ((0, 0), (0, 128 - 2 * GLA_RANK))).astype(BF16)
    wa = jnp.zeros((128, 2 * kdim), F32)
    wa = wa.at[:GLA_RANK, :kdim].set(w_a2[0]).at[GLA_RANK:2 * GLA_RANK, kdim:].set(w_a2[1]).astype(BF16)
    ba = b_a.reshape(1, 2 * kdim)

    tpos = jnp.arange(seq)
    freqs = jnp.power(ROPE_BASE, -jnp.arange(nf, dtype=F32) / nf)
    ar = (tpos // GRID_W).astype(F32)[:, None] * freqs
    ac = (tpos % GRID_W).astype(F32)[:, None] * freqs
    cos = jnp.concatenate([jnp.cos(ar), jnp.cos(ar), jnp.cos(ac), jnp.cos(ac)], axis=1)
    sin = jnp.concatenate([-jnp.sin(ar), jnp.sin(ar), -jnp.sin(ac), jnp.sin(ac)], axis=1)
    cos = jnp.concatenate([jnp.ones((TM, hk), F32), cos], axis=0)
    sin = jnp.concatenate([jnp.zeros((TM, hk), F32), sin], axis=0)

    n_ctx_tiles, tiles_per_batch, _ = geo
    grp = functools.partial(_tile_group, geo=geo)

    def rope_blk(t):
        return (jnp.where(t < n_ctx_tiles, 0, 1 + (t - n_ctx_tiles) % tiles_per_batch), 0)

    const2 = lambda t: (0, 0)
    row = lambda t: (t, 0)
    q, k, v, go, la = pl.pallas_call(
        _gla_proj_kernel,
        grid=(n_tiles,),
        in_specs=[pl.BlockSpec((TM, d), row),
                  pl.BlockSpec((1, 6, d), lambda t: (grp(t), 0, 0)),
                  pl.BlockSpec((1, d), const2),
                  pl.BlockSpec((d, kdim), const2), pl.BlockSpec((d, kdim), const2),
                  pl.BlockSpec((d, vdim), const2), pl.BlockSpec((d, vdim), const2),
                  pl.BlockSpec((d, 128), const2), pl.BlockSpec((128, 2 * kdim), const2),
                  pl.BlockSpec((1, 2 * kdim), const2),
                  pl.BlockSpec((TM, hk), rope_blk), pl.BlockSpec((TM, hk), rope_blk)],
        out_specs=[pl.BlockSpec((TM, kdim), row), pl.BlockSpec((TM, kdim), row),
                   pl.BlockSpec((TM, vdim), row), pl.BlockSpec((TM, vdim), row),
                   pl.BlockSpec((TM, 2 * kdim), row)],
        out_shape=[jax.ShapeDtypeStruct((nt, kdim), F32), jax.ShapeDtypeStruct((nt, kdim), F32),
                   jax.ShapeDtypeStruct((nt, vdim), BF16), jax.ShapeDtypeStruct((nt, vdim), F32),
                   jax.ShapeDtypeStruct((nt, 2 * kdim), F32)],
        compiler_params=_cparams(("parallel",)),
        name="gla_proj",
    )(xs, mods, g.reshape(1, d), wq, wk, wv, wg, wr, wa, ba, cos, sin)

    tb = TM
    ctx_steps = lctx // tb
    lat_steps = seq // tb
    steps = ctx_steps + lat_steps

    def scan(rev):
        def blk(bi, s):
            if rev:
                cs, ls = ctx_steps - 1 - s, lat_steps - 1 - (s - ctx_steps)
            else:
                cs, ls = s, s - ctx_steps
            return jnp.where(s < ctx_steps, bi * ctx_steps + cs, b * ctx_steps + bi * lat_steps + ls)

        d_col = 1 if rev else 0
        return pl.pallas_call(
            functools.partial(_gla_scan_kernel, rev=rev),
            grid=(b, steps),
            in_specs=[pl.BlockSpec((tb, kdim), lambda bi, s: (blk(bi, s), 0)),
                      pl.BlockSpec((tb, kdim), lambda bi, s: (blk(bi, s), 0)),
                      pl.BlockSpec((tb, vdim), lambda bi, s: (blk(bi, s), 0)),
                      pl.BlockSpec((tb, kdim), lambda bi, s: (blk(bi, s), d_col))],
            out_specs=pl.BlockSpec((tb, vdim), lambda bi, s: (blk(bi, s), 0)),
            out_shape=jax.ShapeDtypeStruct((nt, vdim), F32),
            scratch_shapes=[pltpu.VMEM((GLA_HEADS, vdim // GLA_HEADS, hk), F32)],
            compiler_params=_cparams(("parallel", "arbitrary")),
            name="gla_scan_bwd" if rev else "gla_scan_fwd",
        )(q, k, v, la)

    o_f = scan(False)
    o_b = scan(True)

    return pl.pallas_call(
        _gla_out_kernel,
        grid=(n_tiles,),
        in_specs=[pl.BlockSpec((TM, d), row),
                  pl.BlockSpec((1, 6, d), lambda t: (grp(t), 0, 0)),
                  pl.BlockSpec((TM, vdim), row), pl.BlockSpec((TM, vdim), row),
                  pl.BlockSpec((TM, vdim), row),
                  pl.BlockSpec((1, vdim // GLA_HEADS), const2),
                  pl.BlockSpec((vdim, d), const2)],
        out_specs=pl.BlockSpec((TM, d), row),
        out_shape=jax.ShapeDtypeStruct((nt, d), F32),
        compiler_params=_cparams(("parallel",)),
        name="gla_out",
    )(xs, mods, o_f, o_b, go, g_o.reshape(1, -1), w_out.astype(BF16))


def _na_proj_kernel(x_ref, m_ref, g_ref, w_ref, q_ref, k_ref, v_ref):
    x = x_ref[...]
    m = m_ref[0]
    d = x.shape[1]
    h = _modulate(x, g_ref[...], m[0:1], m[1:2]).astype(BF16)
    qkv = _dot(h, w_ref[...])
    hd = d // NA_HEADS
    for p in range(q_ref.shape[0]):
        cs = slice(p * 128, (p + 1) * 128)
        q_ref[p] = (qkv[:, cs] * np.float32(hd ** -0.5)).astype(BF16)
        k_ref[p] = qkv[:, d + p * 128:d + (p + 1) * 128].astype(BF16)
        v_ref[p] = qkv[:, 2 * d + p * 128:2 * d + (p + 1) * 128].astype(BF16)


def _na_attn_kernel(tbl_ref, q_ref, k0_ref, k1_ref, k2_ref, k3_ref, v0_ref, v1_ref, v2_ref, v3_ref,
                    kc_ref, vc_ref, bias_ref, o_ref, s_ref, *, n_tiles):
    t = pl.program_id(2)
    typ = jnp.where(t == 0, 0, jnp.where(t == n_tiles - 1, 2, 1))
    q = q_ref[0]
    kw = jnp.concatenate([k0_ref[0], k1_ref[0], k2_ref[0], k3_ref[0]], axis=0)
    vw = jnp.concatenate([v0_ref[0], v1_ref[0], v2_ref[0], v3_ref[0]], axis=0)
    kc = kc_ref[0]
    vc = vc_ref[0]
    lane = lax.broadcasted_iota(I32, q.shape, 1)
    npair = NA_WROWS // 2
    outs = []
    for hh in range(2):
        sel = (lane < 64) if hh == 0 else (lane >= 64)
        qm = jnp.where(sel, q, jnp.zeros_like(q))
        s_ref[...] = _dot_nt(qm, kw)
        s_ctx = _dot_nt(qm, kc)
        for dr in range(NA_QROWS):
            for j in range(npair):
                e = tbl_ref[typ * (NA_QROWS * npair) + dr * npair + j]
                rs = slice(dr * GRID_W, (dr + 1) * GRID_W)
                cs = slice(j * 128, (j + 1) * 128)
                s_ref[rs, cs] = s_ref[rs, cs] + bias_ref[hh, e]
        s_lat = s_ref[...]
        mx = jnp.maximum(jnp.max(s_lat, axis=1, keepdims=True), jnp.max(s_ctx, axis=1, keepdims=True))
        e_lat = jnp.exp(s_lat - mx)
        e_ctx = jnp.exp(s_ctx - mx)
        inv = 1.0 / (jnp.sum(e_lat, axis=1, keepdims=True) + jnp.sum(e_ctx, axis=1, keepdims=True))
        outs.append(_dot((e_lat * inv).astype(BF16), vw) + _dot((e_ctx * inv).astype(BF16), vc))
    o_ref[...] = jnp.where(lane < 64, outs[0], outs[1]).astype(BF16)


def _na_out_kernel(x_ref, m_ref, a_ref, w_ref, o_ref):
    m = m_ref[0]
    o_ref[...] = x_ref[...] + m[2:3] * _dot(a_ref[...], w_ref[...])


def _na_tables(rows):
    npair = NA_WROWS // 2
    tbl = np.zeros((3, NA_QROWS, npair), np.int32)
    for typ, r_base in enumerate((0, NA_QROWS, rows - NA_QROWS)):
        w0 = int(np.clip(r_base - NA_KH // 2, 0, rows - NA_WROWS))
        for dr in range(NA_QROWS):
            r = r_base + dr
            r0 = int(np.clip(r - NA_KH // 2, 0, rows - NA_KH))
            for j in range(npair):
                kr = (w0 + 2 * j, w0 + 2 * j + 1)
                ok = [r0 <= x < r0 + NA_KH for x in kr]
                ri = [x - r + NA_KH - 1 for x in kr]
                if ok[0] and ok[1]:
                    e = ri[0]
                elif ok[0]:
                    e = 16 + ri[0]
                elif ok[1]:
                    e = 32 + ri[1]
                else:
                    e = 63
                tbl[typ, dr, j] = e
    return tbl.reshape(-1)


def _na_bias_table(rpb):
    nh = rpb.shape[0]
    qc = np.arange(GRID_W)
    cstart = np.clip(qc - NA_KW // 2, 0, GRID_W - NA_KW)
    kc = np.arange(GRID_W)
    ok = (kc[None, :] >= cstart[:, None]) & (kc[None, :] < cstart[:, None] + NA_KW)
    cidx = np.clip(kc[None, :] - qc[:, None] + NA_KW - 1, 0, 2 * NA_KW - 2)
    cb = jnp.where(ok[None, None], rpb[:, :, cidx], NEG_INF).astype(F32)
    neg = jnp.full((nh, 1, GRID_W, GRID_W), NEG_INF, F32)
    cb = jnp.concatenate([cb, neg], axis=1)
    negs = jnp.broadcast_to(neg, cb.shape)
    nxt = jnp.concatenate([cb[:, 1:], neg], axis=1)
    both = jnp.concatenate([cb, nxt], axis=-1)
    left = jnp.concatenate([cb, negs], axis=-1)
    right = jnp.concatenate([negs, cb], axis=-1)
    none = jnp.concatenate([negs, negs], axis=-1)
    return jnp.concatenate([both, left, right, none], axis=1)


def _na_mixer(xs, mods, g, w_qkv, rpb, w_out, geo, dims):
    b, seq, lctx = dims
    nt, d = xs.shape
    n_tiles_tok = nt // TM
    grp = functools.partial(_tile_group, geo=geo)
    npairs = d // 128
    const2 = lambda t: (0, 0)
    q, k, v = pl.pallas_call(
        _na_proj_kernel,
        grid=(n_tiles_tok,),
        in_specs=[pl.BlockSpec((TM, d), lambda t: (t, 0)),
                  pl.BlockSpec((1, 6, d), lambda t: (grp(t), 0, 0)),
                  pl.BlockSpec((1, d), const2),
                  pl.BlockSpec((d, 3 * d), const2)],
        out_specs=[pl.BlockSpec((npairs, TM, 128), lambda t: (0, t, 0))] * 3,
        out_shape=[jax.ShapeDtypeStruct((npairs, nt, 128), BF16)] * 3,
        compiler_params=_cparams(("parallel",)),
        name="na_proj",
    )(xs, mods, g.reshape(1, d), w_qkv.astype(BF16))

    rows = seq // GRID_W
    tq = NA_QROWS * GRID_W
    n_tiles = rows // NA_QROWS
    wb = 256
    nwin = NA_WROWS * GRID_W // wb
    lat0 = b * lctx
    tbl = jnp.asarray(_na_tables(rows))
    bias = _na_bias_table(rpb)

    def win(i):
        def f(p, bi, t, tbl_ref):
            w = jnp.clip(2 * t - 1, 0, seq // wb - nwin)
            return (p, (lat0 + bi * seq) // wb + w + i, 0)
        return f

    kv_specs = [pl.BlockSpec((1, wb, 128), win(i)) for i in range(nwin)]
    attn = pl.pallas_call(
        functools.partial(_na_attn_kernel, n_tiles=n_tiles),
        grid_spec=pltpu.PrefetchScalarGridSpec(
            num_scalar_prefetch=1,
            grid=(npairs, b, n_tiles),
            in_specs=[pl.BlockSpec((1, tq, 128), lambda p, bi, t, tr: (p, (lat0 + bi * seq) // tq + t, 0))]
                     + kv_specs + kv_specs
                     + [pl.BlockSpec((1, lctx, 128), lambda p, bi, t, tr: (p, bi, 0)),
                        pl.BlockSpec((1, lctx, 128), lambda p, bi, t, tr: (p, bi, 0)),
                        pl.BlockSpec((2, 64, GRID_W, 2 * GRID_W), lambda p, bi, t, tr: (p, 0, 0, 0))],
            out_specs=pl.BlockSpec((tq, 128), lambda p, bi, t, tr: (bi * n_tiles + t, p)),
            scratch_shapes=[pltpu.VMEM((tq, NA_WROWS * GRID_W), F32)]),
        out_shape=jax.ShapeDtypeStruct((b * seq, d), BF16),
        compiler_params=_cparams(("parallel", "parallel", "arbitrary")),
        name="na_attn",
    )(tbl, q, k, k, k, k, v, v, v, v, k, v, bias)

    n_lat_tiles = b * seq // TM
    tile0 = lat0 // TM
    grp_l = functools.partial(_tile_group, geo=geo, tile0=tile0)
    return pl.pallas_call(
        _na_out_kernel,
        grid=(n_lat_tiles,),
        in_specs=[pl.BlockSpec((TM, d), lambda t: (t + tile0, 0)),
                  pl.BlockSpec((1, 6, d), lambda t: (grp_l(t), 0, 0)),
                  pl.BlockSpec((TM, d), lambda t: (t, 0)),
                  pl.BlockSpec((d, d), const2)],
        out_specs=pl.BlockSpec((TM, d), lambda t: (t, 0)),
        out_shape=jax.ShapeDtypeStruct((b * seq, d), F32),
        compiler_params=_cparams(("parallel",)),
        name="na_out",
    )(xs, mods, attn, w_out.astype(BF16))


SC_CORES = 2
SC_SUBCORES = 16
SC_WORKERS = SC_CORES * SC_SUBCORES
SLOT_T = 512


def _sc_mesh():
    return plsc.VectorSubcoreMesh(core_axis_name="c", subcore_axis_name="s")


def _sc_chunk(per_worker, max_chunk):
    return max(c for c in range(8, max_chunk + 1, 8) if per_worker % c == 0)


def _sc_gather_rows(table, idx):
    dd = table.shape[1]
    bsz = idx.shape[0]
    per_w = bsz // SC_WORKERS
    assert per_w * SC_WORKERS == bsz
    chunk = _sc_chunk(per_w, 64)
    n_chunks = per_w // chunk

    def body(table_hbm, idx_hbm, out_hbm, idx_v, rows_v, sem):
        wid = lax.axis_index("s") * SC_CORES + lax.axis_index("c")
        pltpu.sync_copy(idx_hbm.at[wid], idx_v)
        base = wid * per_w

        def step(j, carry):
            pltpu.async_copy(table_hbm.at[idx_v.at[j]], rows_v, sem).wait()
            pltpu.sync_copy(rows_v, out_hbm.at[pl.ds(pl.multiple_of(base + j * chunk, 8), chunk)])
            return carry

        lax.fori_loop(0, n_chunks, step, 0)

    return pl.kernel(
        body, out_type=jax.ShapeDtypeStruct((bsz, dd), table.dtype), mesh=_sc_mesh(),
        scratch_types=[pltpu.VMEM((n_chunks, chunk), I32), pltpu.VMEM((chunk, dd), table.dtype),
                       pltpu.SemaphoreType.DMA],
        name="sc_gather_rows",
    )(table, idx.reshape(SC_WORKERS, n_chunks, chunk))


def _sc_scatter_rows(rows, dest, n_out):
    n, dd = rows.shape
    kk = dest.shape[0]
    per_w = n // SC_WORKERS
    assert per_w * SC_WORKERS == n
    chunk = _sc_chunk(per_w, 64)
    n_chunks = per_w // chunk
    dest_w = dest.reshape(kk, SC_WORKERS, n_chunks, chunk).transpose(1, 2, 0, 3)
    dest_w = dest_w.reshape(SC_WORKERS, n_chunks * kk, chunk)

    def body(rows_hbm, dest_hbm, out_hbm, idx_v, rows_v):
        wid = lax.axis_index("s") * SC_CORES + lax.axis_index("c")
        pltpu.sync_copy(dest_hbm.at[wid], idx_v)
        base = wid * per_w

        def step(j, carry):
            pltpu.sync_copy(rows_hbm.at[pl.ds(pl.multiple_of(base + j * chunk, 8), chunk)], rows_v)
            for k in range(kk):
                pltpu.sync_copy(rows_v, out_hbm.at[idx_v.at[j * kk + k]])
            return carry

        lax.fori_loop(0, n_chunks, step, 0)

    return pl.kernel(
        body, out_type=jax.ShapeDtypeStruct((n_out, dd), rows.dtype), mesh=_sc_mesh(),
        scratch_types=[pltpu.VMEM((n_chunks * kk, chunk), I32), pltpu.VMEM((chunk, dd), rows.dtype)],
        name="sc_scatter_rows",
    )(rows, dest_w)


def _slot_kernel(idx_ref, rank_ref, cnt_ref, dest_ref, info_ref, *, bm):
    cnt = cnt_ref[...]
    ne = cnt.shape[0]
    padded = jnp.floor((cnt + (bm - 1.0)) * (1.0 / bm)) * bm
    acc = jnp.broadcast_to(padded, (ne, 128))
    row = lax.broadcasted_iota(I32, (ne, 128), 0)
    s = 1
    while s < ne:
        acc = acc + jnp.where(row >= s, pltpu.roll(acc, s, 0), 0.0)
        s *= 2
    pad_end = acc[:, 0:1]
    pad_start = pad_end - padded
    idx = idx_ref[...]
    e_iota = lax.broadcasted_iota(I32, (ne, idx.shape[1]), 0)
    starts = [jnp.sum(jnp.where(e_iota == idx[k:k + 1], pad_start, 0.0), axis=0, keepdims=True)
              for k in range(idx.shape[0])]
    dest_ref[...] = jnp.concatenate(starts, axis=0).astype(I32) + rank_ref[...]
    lane = lax.broadcasted_iota(I32, info_ref.shape, 1)
    info = jnp.where(lane == 0, pad_start, jnp.where(lane == 1, padded * (1.0 / bm), cnt))
    info_ref[...] = info.astype(I32)


def _route_kernel(x_ref, m_ref, g_ref, wr_ref, br_ref, h_ref, idx_ref, gate_ref, rank_ref, cnt_ref, run_ref):
    t = pl.program_id(0)

    @pl.when(t == 0)
    def _():
        run_ref[...] = jnp.zeros_like(run_ref)

    x = x_ref[...]
    m = m_ref[0]
    h = _modulate(x, g_ref[...], m[3:4], m[4:5]).astype(BF16)
    h_ref[...] = _pack_bf16_pairs(h)
    logits = _dot_nt(wr_ref[...], h) + br_ref[...]
    ne, tm = logits.shape
    e_iota = lax.broadcasted_iota(I32, (ne, tm), 0)
    vals, idxs = [], []
    l = logits
    for _ in range(TOP_K):
        mk = jnp.max(l, axis=0, keepdims=True)
        ik = jnp.min(jnp.where(l == mk, e_iota, ne), axis=0, keepdims=True)
        vals.append(mk)
        idxs.append(ik)
        l = jnp.where(e_iota == ik, -jnp.inf, l)
    top_val = jnp.concatenate(vals, axis=0)
    ex = jnp.exp(top_val - vals[0])
    gate_ref[...] = ex / jnp.sum(ex, axis=0, keepdims=True)
    idx_ref[...] = jnp.concatenate(idxs, axis=0)

    hits = [e_iota == ik for ik in idxs]
    cnt = hits[0].astype(F32)
    for hk in hits[1:]:
        cnt = cnt + hk.astype(F32)
    si = lax.broadcasted_iota(I32, (tm, tm), 0)
    ti = lax.broadcasted_iota(I32, (tm, tm), 1)
    before = (si < ti).astype(BF16)
    total = _dot(cnt.astype(BF16), before) + run_ref[...]
    ranks = [jnp.sum(jnp.where(hk, total, 0.0), axis=0, keepdims=True) for hk in hits]
    rank_ref[...] = jnp.concatenate(ranks, axis=0).astype(I32)
    run_ref[...] = run_ref[...] + jnp.sum(cnt, axis=1, keepdims=True)
    cnt_ref[...] = run_ref[...]


def _pack_bf16_pairs(v):
    bits = lax.bitcast_convert_type(v.astype(BF16).astype(F32), jnp.uint32)
    half = bits.shape[1] // 2
    return (bits[:, half:] & jnp.uint32(0xFFFF0000)) | (bits[:, :half] >> 16)


def _unpack_bf16_pairs(w):
    return (lax.bitcast_convert_type(w << 16, F32),
            lax.bitcast_convert_type(w & jnp.uint32(0xFFFF0000), F32))


def _expert_kernel(start_ref, nblk_ref, cnt_ref, x_hbm, wgu_ref, bgu_ref, wd_ref, bd_ref, y_hbm,
                   xbuf, ybuf, wgu_bf, wd_bf, act_ref, sem_in, sem_out, *, n_blocks):
    e = pl.program_id(0)
    bm = xbuf.shape[1]
    f = act_ref.shape[1]
    ch = 256
    nb = nblk_ref[e]
    start = start_ref[e]
    cnt = cnt_ref[e]

    def rows(j):
        return pl.ds(pl.multiple_of(start + j * bm, bm), bm)

    def fetch(j, slot):
        return pltpu.make_async_copy(x_hbm.at[rows(j)], xbuf.at[slot], sem_in.at[slot])

    def put(j, slot):
        return pltpu.make_async_copy(ybuf.at[slot], y_hbm.at[rows(j)], sem_out.at[slot])

    @pl.when(nb > 0)
    def _():
        fetch(0, 0).start()
        for j in range(wgu_bf.shape[1] // ch):
            cs = slice(j * ch, (j + 1) * ch)
            wgu_bf[:, cs] = wgu_ref[0, 0, :, cs].astype(BF16)
        for j in range(wd_bf.shape[1] // ch):
            cs = slice(j * ch, (j + 1) * ch)
            wd_bf[:, cs] = wd_ref[0, 0, :, cs].astype(BF16)

    def block(j, carry):
        slot = lax.rem(j, 2)

        @pl.when(j + 1 < nb)
        def _():
            fetch(j + 1, 1 - slot).start()

        fetch(j, slot).wait()

        @pl.when(j >= 2)
        def _():
            put(j - 2, slot).wait()

        w = xbuf[slot]
        row = lax.broadcasted_iota(I32, w.shape, 0)
        w = jnp.where(row < cnt - j * bm, w, jnp.zeros_like(w))
        lo, hi = _unpack_bf16_pairs(w)
        x = jnp.concatenate([lo.astype(BF16), hi.astype(BF16)], axis=1)
        for c in range(f // ch):
            c0 = slice(c * ch, (c + 1) * ch)
            c1 = slice(f + c * ch, f + (c + 1) * ch)
            glu = _dot(x, wgu_bf[:, c0]) + bgu_ref[0, 0, :, c0]
            lin = _dot(x, wgu_bf[:, c1]) + bgu_ref[0, 0, :, c1]
            glu = jnp.minimum(glu, SWIGLU_LIMIT)
            lin = jnp.clip(lin, -SWIGLU_LIMIT, SWIGLU_LIMIT)
            act_ref[:, c0] = (glu * jax.nn.sigmoid(SWIGLU_ALPHA * glu) * (lin + 1.0)).astype(BF16)
        ybuf[slot] = _pack_bf16_pairs(_dot(act_ref[...], wd_bf[...]) + bd_ref[0, 0])
        put(j, slot).start()
        return carry

    lax.fori_loop(0, nb, block, 0)

    @pl.when(nb >= 2)
    def _():
        put(nb - 2, lax.rem(nb, 2)).wait()

    @pl.when(nb >= 1)
    def _():
        put(nb - 1, lax.rem(nb + 1, 2)).wait()

    @pl.when(e == pl.num_programs(0) - 1)
    def _():
        ybuf[0] = jnp.zeros(ybuf.shape[1:], ybuf.dtype)

        def fill(j, carry):
            cp = pltpu.make_async_copy(ybuf.at[0], y_hbm.at[pl.ds(pl.multiple_of(j * bm, bm), bm)], sem_out.at[0])
            cp.start()
            cp.wait()
            return carry

        lax.fori_loop(start // bm + nb, n_blocks, fill, 0)


def _expert_call(start, nblk, cnt, buf, w_gu, b_gu, w_down, b_down, layer, bm):
    n_slots, dw = buf.shape
    _, ne, d, f2 = w_gu.shape
    f = f2 // 2
    lw = lambda e, *_: (layer, e, 0, 0)
    return pl.pallas_call(
        functools.partial(_expert_kernel, n_blocks=n_slots // bm),
        grid_spec=pltpu.PrefetchScalarGridSpec(
            num_scalar_prefetch=3,
            grid=(ne,),
            in_specs=[pl.BlockSpec(memory_space=pl.ANY),
                      pl.BlockSpec((1, 1, d, f2), lw),
                      pl.BlockSpec((1, 1, 1, f2), lw),
                      pl.BlockSpec((1, 1, f, d), lw),
                      pl.BlockSpec((1, 1, 1, d), lw)],
            out_specs=pl.BlockSpec(memory_space=pl.ANY),
            scratch_shapes=[pltpu.VMEM((2, bm, dw), jnp.uint32), pltpu.VMEM((2, bm, dw), jnp.uint32),
                            pltpu.VMEM((d, f2), BF16), pltpu.VMEM((f, d), BF16), pltpu.VMEM((bm, f), BF16),
                            pltpu.SemaphoreType.DMA((2,)), pltpu.SemaphoreType.DMA((2,))]),
        out_shape=jax.ShapeDtypeStruct((n_slots, dw), jnp.uint32),
        compiler_params=_cparams(("arbitrary",)),
        name="moe_experts",
    )(start, nblk, cnt, buf, w_gu, b_gu.reshape(b_gu.shape[0], ne, 1, f2),
      w_down, b_down.reshape(b_down.shape[0], ne, 1, d))


def _combine_kernel(x_ref, m_ref, y_ref, gate_ref, o_ref):
    m = m_ref[0]
    gate = gate_ref[...]
    half = y_ref.shape[2]
    f_lo, f_hi = None, None
    for k in range(TOP_K):
        lo, hi = _unpack_bf16_pairs(y_ref[k])
        gk = gate[:, k:k + 1]
        f_lo = gk * lo if f_lo is None else f_lo + gk * lo
        f_hi = gk * hi if f_hi is None else f_hi + gk * hi
    o_ref[:, :half] = x_ref[:, :half] + m[5:6, :half] * f_lo
    o_ref[:, half:] = x_ref[:, half:] + m[5:6, half:] * f_hi


def _moe_layer(xs, mods, g, layer, w_router, b_router, w_gu, b_gu, w_down, b_down, geo):
    n, d = xs.shape
    ne = w_router.shape[1]
    f = w_down.shape[2]
    n_tiles = n // TM
    grp = functools.partial(_tile_group, geo=geo)
    const2 = lambda t: (0, 0)
    h, idx, gate, rank, cnt = pl.pallas_call(
        _route_kernel,
        grid=(n_tiles,),
        in_specs=[pl.BlockSpec((TM, d), lambda t: (t, 0)),
                  pl.BlockSpec((1, 6, d), lambda t: (grp(t), 0, 0)),
                  pl.BlockSpec((1, d), const2),
                  pl.BlockSpec((ne, d), const2),
                  pl.BlockSpec((ne, 1), const2)],
        out_specs=[pl.BlockSpec((TM, d // 2), lambda t: (t, 0)),
                   pl.BlockSpec((TOP_K, TM), lambda t: (0, t)),
                   pl.BlockSpec((TOP_K, TM), lambda t: (0, t)),
                   pl.BlockSpec((TOP_K, TM), lambda t: (0, t)),
                   pl.BlockSpec((ne, 1), const2)],
        out_shape=[jax.ShapeDtypeStruct((n, d // 2), jnp.uint32),
                   jax.ShapeDtypeStruct((TOP_K, n), I32),
                   jax.ShapeDtypeStruct((TOP_K, n), F32),
                   jax.ShapeDtypeStruct((TOP_K, n), I32),
                   jax.ShapeDtypeStruct((ne, 1), F32)],
        scratch_shapes=[pltpu.VMEM((ne, 1), F32)],
        compiler_params=_cparams(("arbitrary",)),
        name="moe_route",
    )(xs, mods, g.reshape(1, d), w_router.T.astype(BF16), b_router.reshape(ne, 1))

    bm = MOE_BM
    n_rows = n * TOP_K
    n_blocks = -(-n_rows // bm) + ne
    n_slots = n_blocks * bm
    dest, info = pl.pallas_call(
        functools.partial(_slot_kernel, bm=bm),
        grid=(n // SLOT_T,),
        in_specs=[pl.BlockSpec((TOP_K, SLOT_T), lambda t: (0, t)),
                  pl.BlockSpec((TOP_K, SLOT_T), lambda t: (0, t)),
                  pl.BlockSpec((ne, 1), const2)],
        out_specs=[pl.BlockSpec((TOP_K, SLOT_T), lambda t: (0, t)),
                   pl.BlockSpec((ne, 128), const2)],
        out_shape=[jax.ShapeDtypeStruct((TOP_K, n), I32),
                   jax.ShapeDtypeStruct((ne, 128), I32)],
        compiler_params=_cparams(("arbitrary",)),
        name="moe_slots",
    )(idx, rank, cnt)

    buf = _sc_scatter_rows(h, dest, n_slots)
    y = _expert_call(info[:, 0], info[:, 1], info[:, 2], buf, w_gu, b_gu, w_down, b_down, layer, bm)
    dy = d // 2
    yk = _sc_gather_rows(y, dest.reshape(-1)).reshape(TOP_K, n, dy)
    return pl.pallas_call(
        _combine_kernel,
        grid=(n_tiles,),
        in_specs=[pl.BlockSpec((TM, d), lambda t: (t, 0)),
                  pl.BlockSpec((1, 6, d), lambda t: (grp(t), 0, 0)),
                  pl.BlockSpec((TOP_K, TM, dy), lambda t: (0, t, 0)),
                  pl.BlockSpec((TM, TOP_K), lambda t: (t, 0))],
        out_specs=pl.BlockSpec((TM, d), lambda t: (t, 0)),
        out_shape=jax.ShapeDtypeStruct((n, d), F32),
        compiler_params=_cparams(("parallel",)),
        name="moe_combine",
    )(xs, mods, yk, gate.T)


def _final_kernel(x_ref, g_ref, o_ref):
    o_ref[...] = _rms(x_ref[...], g_ref[...])


def _final_norm(xs, g):
    n, d = xs.shape
    return pl.pallas_call(
        _final_kernel,
        grid=(n // TM,),
        in_specs=[pl.BlockSpec((TM, d), lambda t: (t, 0)), pl.BlockSpec((1, d), lambda t: (0, 0))],
        out_specs=pl.BlockSpec((TM, d), lambda t: (t, 0)),
        out_shape=jax.ShapeDtypeStruct((n, d), F32),
        compiler_params=_cparams(("parallel",)),
        name="final_norm",
    )(xs, g.reshape(1, d))


def kernel(x, c, ctx, c_ctx, ada_w, ada_b, norm_mix, norm_ffn, norm_out, a_w_in, a_g_v, a_w_s, a_b_s, a_w_out, b_w_in, b_w_a2, b_b_a, b_g_o, b_w_out, c_w_qkv, c_rpb, c_w_out, moe_w_router, moe_b_router, moe_w_gu, moe_b_gu, moe_w_down, moe_b_down):
    b, seq, d = x.shape
    lctx = ctx.shape[1]
    depth = ada_w.shape[0]
    assert lctx % TM == 0 and seq % (NA_QROWS * GRID_W) == 0 and (b * lctx) % (NA_QROWS * GRID_W) == 0
    assert seq // GRID_W >= NA_WROWS + NA_QROWS and b + 1 <= 8
    geo = (b * lctx // TM, seq // TM, b)
    dims = (b, seq, lctx)

    cond = jnp.zeros((8, d), F32).at[:b].set(c).at[b].set(c_ctx)
    mods_all = _adaln(cond, ada_w, ada_b)[:, :b + 1].reshape(depth, b + 1, 6, d)

    xs = jnp.concatenate([ctx.reshape(b * lctx, d), x.reshape(b * seq, d)], axis=0)
    ctx_tiles = geo[0]
    has_ctx = True
    for i in range(depth):
        kind, j = i % N_MIXERS, i // N_MIXERS
        ctx_later = any(kk % N_MIXERS != 0 for kk in range(i + 1, depth))
        mods = mods_all[i]
        if kind == 0:
            keep_ctx = has_ctx and ctx_later
            skip = 0 if keep_ctx or not has_ctx else ctx_tiles
            geo_i = geo if has_ctx else (0, geo[1], b)
            xs = _gmlp_mixer(xs, mods, norm_mix[i], a_w_in[j], a_g_v[j], a_w_s[j], a_b_s[j], a_w_out[j],
                             skip, xs.shape[0] // TM - skip, geo_i)
            has_ctx = keep_ctx
        elif kind == 1:
            assert has_ctx
            xs = _gla_mixer(xs, mods, norm_mix[i], b_w_in[j], b_w_a2[j], b_b_a[j], b_g_o[j], b_w_out[j], geo, dims)
            if not ctx_later:
                xs = xs[b * lctx:]
                has_ctx = False
        else:
            assert has_ctx
            if ctx_later:
                raise NotImplementedError("context output of the neighbourhood mixer")
            xs = _na_mixer(xs, mods, norm_mix[i], c_w_qkv[j], c_rpb[j], c_w_out[j], geo, dims)
            has_ctx = False
        geo_i = geo if has_ctx else (0, geo[1], b)
        xs = _moe_layer(xs, mods, norm_ffn[i], i, moe_w_router[i], moe_b_router[i], moe_w_gu, moe_b_gu,
                        moe_w_down, moe_b_down, geo_i)
    if has_ctx:
        xs = xs[b * lctx:]
    return _final_norm(xs, norm_out).reshape(b, seq, d)
```

```python
import functools

import numpy as np
import jax
import jax.numpy as jnp
from jax import lax
from jax.experimental import pallas as pl
from jax.experimental.pallas import tpu as pltpu
from jax.experimental.pallas import tpu_sc as plsc

F32 = jnp.float32
BF16 = jnp.bfloat16
I32 = jnp.int32

NORM_EPS = 1e-6
GRID_W = 64
N_MIXERS = 3

CHUNK_A = 128
A_GROUPS = 8
GLA_HEADS = 4
GLA_RANK = 16
GLA_TAU = 16.0
GLA_CHUNK = 64
ROPE_BASE = 10000.0
NA_HEADS = 16
NA_KH = 8
NA_KW = 16
NEG_INF = -1e30
NA_QROWS = 8
NA_WROWS = 16
TOP_K = 4
SWIGLU_LIMIT = 7.0
SWIGLU_ALPHA = 1.702
MOE_BM = 512

TM = 256
VMEM_LIMIT = 56 * 1024 * 1024


def _cparams(sem):
    return pltpu.CompilerParams(dimension_semantics=sem, vmem_limit_bytes=VMEM_LIMIT)


def _dot(a, b):
    return jnp.dot(a, b, preferred_element_type=F32)


def _dot_nt(a, b):
    return lax.dot_general(a, b, (((1,), (1,)), ((), ())), preferred_element_type=F32)


def _dot_tn(a, b):
    return lax.dot_general(a, b, (((0,), (0,)), ((), ())), preferred_element_type=F32)


def _rms(x, g):
    return x * lax.rsqrt(jnp.mean(x * x, axis=-1, keepdims=True) + NORM_EPS) * g


def _modulate(x, g, shift, scale):
    return _rms(x, g) * (1.0 + scale) + shift


def _ada_kernel(s_ref, w_ref, b_ref, o_ref):
    s = s_ref[...]
    s = s * jax.nn.sigmoid(s)
    o_ref[0] = jnp.dot(s, w_ref[0], preferred_element_type=F32,
                       precision=lax.Precision.HIGHEST) + b_ref[0]


def _adaln(cond, ada_w, ada_b):
    depth, d, n6 = ada_w.shape
    bn = n6 // 4
    return pl.pallas_call(
        _ada_kernel,
        grid=(depth, n6 // bn),
        in_specs=[pl.BlockSpec((8, d), lambda i, j: (0, 0)),
                  pl.BlockSpec((1, d, bn), lambda i, j: (i, 0, j)),
                  pl.BlockSpec((1, 1, bn), lambda i, j: (i, 0, j))],
        out_specs=pl.BlockSpec((1, 8, bn), lambda i, j: (i, 0, j)),
        out_shape=jax.ShapeDtypeStruct((depth, 8, n6), F32),
        compiler_params=_cparams(("parallel", "parallel")),
        name="adaln",
    )(cond, ada_w, ada_b.reshape(depth, 1, n6))


def _gelu(z):
    return 0.5 * z * (1.0 + lax.erf(z * np.float32(np.sqrt(0.5))))


def _gmlp_kernel(x_ref, m_ref, g_ref, win_ref, gv_ref, ws_ref, bs_ref, wout_ref, o_ref):
    x = x_ref[...]
    m = m_ref[0]
    a = gv_ref.shape[1]
    h = _modulate(x, g_ref[...], m[0:1], m[1:2]).astype(BF16)
    z = _gelu(_dot(h, win_ref[...]))
    u = z[:, :a]
    v = _rms(z[:, a:], gv_ref[...]).astype(BF16)
    gw = a // A_GROUPS
    rows = []
    for c in range(x.shape[0] // CHUNK_A):
        cols = [_dot(ws_ref[g], v[c * CHUNK_A:(c + 1) * CHUNK_A, g * gw:(g + 1) * gw])
                for g in range(A_GROUPS)]
        rows.append(jnp.concatenate(cols, axis=1) + bs_ref[...])
    s = jnp.concatenate(rows, axis=0)
    y = _dot((u * s).astype(BF16), wout_ref[...])
    o_ref[...] = x + m[2:3] * y


def _gmlp_mixer(xs, mods, g, w_in, g_v, w_s, b_s, w_out, tile0, n_tiles, geo):
    d = xs.shape[1]
    a = g_v.shape[0]
    gw = a // A_GROUPS
    bias = jnp.repeat(b_s.T, gw, axis=1)
    grp = functools.partial(_tile_group, geo=geo, tile0=tile0)
    const2 = lambda t: (0, 0)
    return pl.pallas_call(
        _gmlp_kernel,
        grid=(n_tiles,),
        in_specs=[pl.BlockSpec((TM, d), lambda t: (t + tile0, 0)),
                  pl.BlockSpec((1, 6, d), lambda t: (grp(t), 0, 0)),
                  pl.BlockSpec((1, d), const2),
                  pl.BlockSpec((d, 2 * a), const2),
                  pl.BlockSpec((1, a), const2),
                  pl.BlockSpec((A_GROUPS, CHUNK_A, CHUNK_A), lambda t: (0, 0, 0)),
                  pl.BlockSpec((CHUNK_A, a), const2),
                  pl.BlockSpec((a, d), const2)],
        out_specs=pl.BlockSpec((TM, d), lambda t: (t, 0)),
        out_shape=jax.ShapeDtypeStruct((n_tiles * TM, d), F32),
        compiler_params=_cparams(("parallel",)),
        name="gmlp_mixer",
    )(xs, mods, g.reshape(1, d), w_in.astype(BF16), g_v.reshape(1, a), w_s.astype(BF16),
      bias, w_out.astype(BF16))


def _tile_group(t, geo, tile0=0):
    n_ctx_tiles, tiles_per_batch, nb = geo
    tt = t + tile0
    return jnp.where(tt < n_ctx_tiles, nb, (tt - n_ctx_tiles) // tiles_per_batch)


def _dot_f32(tri_bf16, x):
    hi = x.astype(BF16)
    r1 = x - hi.astype(F32)
    mid = r1.astype(BF16)
    lo = (r1 - mid.astype(F32)).astype(BF16)
    return _dot(tri_bf16, hi) + _dot(tri_bf16, mid) + _dot(tri_bf16, lo)


def _gla_proj_kernel(x_ref, m_ref, g_ref, wq_ref, wk_ref, wv_ref, wg_ref, wr_ref, wa_ref, ba_ref,
                     cos_ref, sin_ref, q_ref, k_ref, v_ref, go_ref, la_ref):
    x = x_ref[...]
    m = m_ref[0]
    h = _modulate(x, g_ref[...], m[0:1], m[1:2]).astype(BF16)
    kdim = wq_ref.shape[1]
    hk = kdim // GLA_HEADS
    q = _dot(h, wq_ref[...]) * np.float32(hk ** -0.5)
    k = _dot(h, wk_ref[...])
    v_ref[...] = _dot(h, wv_ref[...]).astype(BF16)
    go_ref[...] = _dot(h, wg_ref[...])
    r = _dot(h, wr_ref[...])
    z = _dot(r.astype(BF16), wa_ref[...]) + ba_ref[...]
    la_ref[...] = jax.nn.log_sigmoid(z) * np.float32(1.0 / GLA_TAU)

    cos = jnp.concatenate([cos_ref[...]] * GLA_HEADS, axis=1)
    sin = jnp.concatenate([sin_ref[...]] * GLA_HEADS, axis=1)
    nf = hk // 4
    lane = lax.broadcasted_iota(I32, q.shape, 1)
    first = (lane % (2 * nf)) < nf

    def rope(t):
        up = pltpu.roll(t, kdim - nf, 1)
        dn = pltpu.roll(t, nf, 1)
        return t * cos + jnp.where(first, up, dn) * sin

    q_ref[...] = rope(q)
    k_ref[...] = rope(k)


def _gla_scan_kernel(q_ref, k_ref, v_ref, la_ref, o_ref, st_ref, *, rev):
    s = pl.program_id(1)

    @pl.when(s == 0)
    def _():
        st_ref[...] = jnp.zeros_like(st_ref)

    tb, kdim = q_ref.shape
    hk = kdim // GLA_HEADS
    hv = v_ref.shape[1] // GLA_HEADS
    c = GLA_CHUNK
    ri = lax.broadcasted_iota(I32, (c, c), 0)
    ci = lax.broadcasted_iota(I32, (c, c), 1)
    keep = (ci >= ri) if rev else (ci <= ri)
    tri = keep.astype(BF16)
    order = range(tb // c)
    for n in (reversed(order) if rev else order):
        rows = slice(n * c, (n + 1) * c)
        cum = _dot_f32(tri, la_ref[rows, :])
        last = cum[0:1] if rev else cum[c - 1:c]
        q = q_ref[rows, :]
        k = k_ref[rows, :]
        q_dec = (q * jnp.exp(cum)).astype(BF16)
        k_inv = (k * jnp.exp(-cum)).astype(BF16)
        k_end = (k * jnp.exp(last - cum)).astype(BF16)
        dec = jnp.exp(last)
        for h in range(GLA_HEADS):
            ks = slice(h * hk, (h + 1) * hk)
            vs = slice(h * hv, (h + 1) * hv)
            att = jnp.where(keep, _dot_nt(q_dec[:, ks], k_inv[:, ks]), 0.0).astype(BF16)
            vh = v_ref[rows, vs]
            st = st_ref[h]
            o_ref[rows, vs] = _dot(att, vh) + _dot_nt(q_dec[:, ks], st.astype(BF16))
            st_ref[h] = st * dec[:, ks] + _dot_tn(vh, k_end[:, ks])


def _gla_out_kernel(x_ref, m_ref, of_ref, ob_ref, go_ref, gn_ref, wout_ref, o_ref):
    x = x_ref[...]
    m = m_ref[0]
    o = of_ref[...] + ob_ref[...]
    hv = gn_ref.shape[1]
    parts = [_rms(o[:, h * hv:(h + 1) * hv], gn_ref[...]) for h in range(GLA_HEADS)]
    o = jnp.concatenate(parts, axis=1)
    gate = go_ref[...]
    y = _dot((o * (gate * jax.nn.sigmoid(gate))).astype(BF16), wout_ref[...])
    o_ref[...] = x + m[2:3] * y


def _gla_mixer(xs, mods, g, w_in, w_a2, b_a, g_o, w_out, geo, dims):
    b, seq, lctx = dims
    nt, d = xs.shape
    n_tiles = nt // TM
    kdim = w_a2.shape[2]
    vdim = g_o.shape[0] * GLA_HEADS
    hk = kdim // GLA_HEADS
    nf = hk // 4
    wq = w_in[:, :kdim].astype(BF16)
    wk = w_in[:, kdim:2 * kdim].astype(BF16)
    wv = w_in[:, 2 * kdim:2 * kdim + vdim].astype(BF16)
    wg = w_in[:, 2 * kdim + vdim:2 * kdim + 2 * vdim].astype(BF16)
    wr = jnp.pad(w_in[:, 2 * kdim + 2 * vdim:], ((0, 0), (0, 128 - 2 * GLA_RANK))).astype(BF16)
    wa = jnp.zeros((128, 2 * kdim), F32)
    wa = wa.at[:GLA_RANK, :kdim].set(w_a2[0]).at[GLA_RANK:2 * GLA_RANK, kdim:].set(w_a2[1]).astype(BF16)
    ba = b_a.reshape(1, 2 * kdim)

    tpos = jnp.arange(seq)
    freqs = jnp.power(ROPE_BASE, -jnp.arange(nf, dtype=F32) / nf)
    ar = (tpos // GRID_W).astype(F32)[:, None] * freqs
    ac = (tpos % GRID_W).astype(F32)[:, None] * freqs
    cos = jnp.concatenate([jnp.cos(ar), jnp.cos(ar), jnp.cos(ac), jnp.cos(ac)], axis=1)
    sin = jnp.concatenate([-jnp.sin(ar), jnp.sin(ar), -jnp.sin(ac), jnp.sin(ac)], axis=1)
    cos = jnp.concatenate([jnp.ones((TM, hk), F32), cos], axis=0)
    sin = jnp.concatenate([jnp.zeros((TM, hk), F32), sin], axis=0)

    n_ctx_tiles, tiles_per_batch, _ = geo
    grp = functools.partial(_tile_group, geo=geo)

    def rope_blk(t):
        return (jnp.where(t < n_ctx_tiles, 0, 1 + (t - n_ctx_tiles) % tiles_per_batch), 0)

    const2 = lambda t: (0, 0)
    row = lambda t: (t, 0)
    q, k, v, go, la = pl.pallas_call(
        _gla_proj_kernel,
        grid=(n_tiles,),
        in_specs=[pl.BlockSpec((TM, d), row),
                  pl.BlockSpec((1, 6, d), lambda t: (grp(t), 0, 0)),
                  pl.BlockSpec((1, d), const2),
                  pl.BlockSpec((d, kdim), const2), pl.BlockSpec((d, kdim), const2),
                  pl.BlockSpec((d, vdim), const2), pl.BlockSpec((d, vdim), const2),
                  pl.BlockSpec((d, 128), const2), pl.BlockSpec((128, 2 * kdim), const2),
                  pl.BlockSpec((1, 2 * kdim), const2),
                  pl.BlockSpec((TM, hk), rope_blk), pl.BlockSpec((TM, hk), rope_blk)],
        out_specs=[pl.BlockSpec((TM, kdim), row), pl.BlockSpec((TM, kdim), row),
                   pl.BlockSpec((TM, vdim), row), pl.BlockSpec((TM, vdim), row),
                   pl.BlockSpec((TM, 2 * kdim), row)],
        out_shape=[jax.ShapeDtypeStruct((nt, kdim), F32), jax.ShapeDtypeStruct((nt, kdim), F32),
                   jax.ShapeDtypeStruct((nt, vdim), BF16), jax.ShapeDtypeStruct((nt, vdim), F32),
                   jax.ShapeDtypeStruct((nt, 2 * kdim), F32)],
        compiler_params=_cparams(("parallel",)),
        name="gla_proj",
    )(xs, mods, g.reshape(1, d), wq, wk, wv, wg, wr, wa, ba, cos, sin)

    tb = TM
    ctx_steps = lctx // tb
    lat_steps = seq // tb
    steps = ctx_steps + lat_steps

    def scan(rev):
        def blk(bi, s):
            if rev:
                cs, ls = ctx_steps - 1 - s, lat_steps - 1 - (s - ctx_steps)
            else:
                cs, ls = s, s - ctx_steps
            return jnp.where(s < ctx_steps, bi * ctx_steps + cs, b * ctx_steps + bi * lat_steps + ls)

        d_col = 1 if rev else 0
        return pl.pallas_call(
            functools.partial(_gla_scan_kernel, rev=rev),
            grid=(b, steps),
            in_specs=[pl.BlockSpec((tb, kdim), lambda bi, s: (blk(bi, s), 0)),
                      pl.BlockSpec((tb, kdim), lambda bi, s: (blk(bi, s), 0)),
                      pl.BlockSpec((tb, vdim), lambda bi, s: (blk(bi, s), 0)),
                      pl.BlockSpec((tb, kdim), lambda bi, s: (blk(bi, s), d_col))],
            out_specs=pl.BlockSpec((tb, vdim), lambda bi, s: (blk(bi, s), 0)),
            out_shape=jax.ShapeDtypeStruct((nt, vdim), F32),
            scratch_shapes=[pltpu.VMEM((GLA_HEADS, vdim // GLA_HEADS, hk), F32)],
            compiler_params=_cparams(("parallel", "arbitrary")),
            name="gla_scan_bwd" if rev else "gla_scan_fwd",
        )(q, k, v, la)

    o_f = scan(False)
    o_b = scan(True)

    return pl.pallas_call(
        _gla_out_kernel,
        grid=(n_tiles,),
        in_specs=[pl.BlockSpec((TM, d), row),
                  pl.BlockSpec((1, 6, d), lambda t: (grp(t), 0, 0)),
                  pl.BlockSpec((TM, vdim), row), pl.BlockSpec((TM, vdim), row),
                  pl.BlockSpec((TM, vdim), row),
                  pl.BlockSpec((1, vdim // GLA_HEADS), const2),
                  pl.BlockSpec((vdim, d), const2)],
        out_specs=pl.BlockSpec((TM, d), row),
        out_shape=jax.ShapeDtypeStruct((nt, d), F32),
        compiler_params=_cparams(("parallel",)),
        name="gla_out",
    )(xs, mods, o_f, o_b, go, g_o.reshape(1, -1), w_out.astype(BF16))


def _na_proj_kernel(x_ref, m_ref, g_ref, w_ref, q_ref, k_ref, v_ref):
    x = x_ref[...]
    m = m_ref[0]
    d = x.shape[1]
    h = _modulate(x, g_ref[...], m[0:1], m[1:2]).astype(BF16)
    qkv = _dot(h, w_ref[...])
    hd = d // NA_HEADS
    for p in range(q_ref.shape[0]):
        cs = slice(p * 128, (p + 1) * 128)
        q_ref[p] = (qkv[:, cs] * np.float32(hd ** -0.5)).astype(BF16)
        k_ref[p] = qkv[:, d + p * 128:d + (p + 1) * 128].astype(BF16)
        v_ref[p] = qkv[:, 2 * d + p * 128:2 * d + (p + 1) * 128].astype(BF16)


def _na_attn_kernel(tbl_ref, q_ref, k0_ref, k1_ref, k2_ref, k3_ref, v0_ref, v1_ref, v2_ref, v3_ref,
                    kc_ref, vc_ref, bias_ref, o_ref, s_ref, *, n_tiles):
    t = pl.program_id(2)
    typ = jnp.where(t == 0, 0, jnp.where(t == n_tiles - 1, 2, 1))
    q = q_ref[0]
    kw = jnp.concatenate([k0_ref[0], k1_ref[0], k2_ref[0], k3_ref[0]], axis=0)
    vw = jnp.concatenate([v0_ref[0], v1_ref[0], v2_ref[0], v3_ref[0]], axis=0)
    kc = kc_ref[0]
    vc = vc_ref[0]
    lane = lax.broadcasted_iota(I32, q.shape, 1)
    npair = NA_WROWS // 2
    outs = []
    for hh in range(2):
        sel = (lane < 64) if hh == 0 else (lane >= 64)
        qm = jnp.where(sel, q, jnp.zeros_like(q))
        s_ref[...] = _dot_nt(qm, kw)
        s_ctx = _dot_nt(qm, kc)
        for dr in range(NA_QROWS):
            for j in range(npair):
                e = tbl_ref[typ * (NA_QROWS * npair) + dr * npair + j]
                rs = slice(dr * GRID_W, (dr + 1) * GRID_W)
                cs = slice(j * 128, (j + 1) * 128)
                s_ref[rs, cs] = s_ref[rs, cs] + bias_ref[hh, e]
        s_lat = s_ref[...]
        mx = jnp.maximum(jnp.max(s_lat, axis=1, keepdims=True), jnp.max(s_ctx, axis=1, keepdims=True))
        e_lat = jnp.exp(s_lat - mx)
        e_ctx = jnp.exp(s_ctx - mx)
        inv = 1.0 / (jnp.sum(e_lat, axis=1, keepdims=True) + jnp.sum(e_ctx, axis=1, keepdims=True))
        outs.append(_dot((e_lat * inv).astype(BF16), vw) + _dot((e_ctx * inv).astype(BF16), vc))
    o_ref[...] = jnp.where(lane < 64, outs[0], outs[1]).astype(BF16)


def _na_out_kernel(x_ref, m_ref, a_ref, w_ref, o_ref):
    m = m_ref[0]
    o_ref[...] = x_ref[...] + m[2:3] * _dot(a_ref[...], w_ref[...])


def _na_tables(rows):
    npair = NA_WROWS // 2
    tbl = np.zeros((3, NA_QROWS, npair), np.int32)
    for typ, r_base in enumerate((0, NA_QROWS, rows - NA_QROWS)):
        w0 = int(np.clip(r_base - NA_KH // 2, 0, rows - NA_WROWS))
        for dr in range(NA_QROWS):
            r = r_base + dr
            r0 = int(np.clip(r - NA_KH // 2, 0, rows - NA_KH))
            for j in range(npair):
                kr = (w0 + 2 * j, w0 + 2 * j + 1)
                ok = [r0 <= x < r0 + NA_KH for x in kr]
                ri = [x - r + NA_KH - 1 for x in kr]
                if ok[0] and ok[1]:
                    e = ri[0]
                elif ok[0]:
                    e = 16 + ri[0]
                elif ok[1]:
                    e = 32 + ri[1]
                else:
                    e = 63
                tbl[typ, dr, j] = e
    return tbl.reshape(-1)


def _na_bias_table(rpb):
    nh = rpb.shape[0]
    qc = np.arange(GRID_W)
    cstart = np.clip(qc - NA_KW // 2, 0, GRID_W - NA_KW)
    kc = np.arange(GRID_W)
    ok = (kc[None, :] >= cstart[:, None]) & (kc[None, :] < cstart[:, None] + NA_KW)
    cidx = np.clip(kc[None, :] - qc[:, None] + NA_KW - 1, 0, 2 * NA_KW - 2)
    cb = jnp.where(ok[None, None], rpb[:, :, cidx], NEG_INF).astype(F32)
    neg = jnp.full((nh, 1, GRID_W, GRID_W), NEG_INF, F32)
    cb = jnp.concatenate([cb, neg], axis=1)
    negs = jnp.broadcast_to(neg, cb.shape)
    nxt = jnp.concatenate([cb[:, 1:], neg], axis=1)
    both = jnp.concatenate([cb, nxt], axis=-1)
    left = jnp.concatenate([cb, negs], axis=-1)
    right = jnp.concatenate([negs, cb], axis=-1)
    none = jnp.concatenate([negs, negs], axis=-1)
    return jnp.concatenate([both, left, right, none], axis=1)


def _na_mixer(xs, mods, g, w_qkv, rpb, w_out, geo, dims):
    b, seq, lctx = dims
    nt, d = xs.shape
    n_tiles_tok = nt // TM
    grp = functools.partial(_tile_group, geo=geo)
    npairs = d // 128
    const2 = lambda t: (0, 0)
    q, k, v = pl.pallas_call(
        _na_proj_kernel,
        grid=(n_tiles_tok,),
        in_specs=[pl.BlockSpec((TM, d), lambda t: (t, 0)),
                  pl.BlockSpec((1, 6, d), lambda t: (grp(t), 0, 0)),
                  pl.BlockSpec((1, d), const2),
                  pl.BlockSpec((d, 3 * d), const2)],
        out_specs=[pl.BlockSpec((npairs, TM, 128), lambda t: (0, t, 0))] * 3,
        out_shape=[jax.ShapeDtypeStruct((npairs, nt, 128), BF16)] * 3,
        compiler_params=_cparams(("parallel",)),
        name="na_proj",
    )(xs, mods, g.reshape(1, d), w_qkv.astype(BF16))

    rows = seq // GRID_W
    tq = NA_QROWS * GRID_W
    n_tiles = rows // NA_QROWS
    wb = 256
    nwin = NA_WROWS * GRID_W // wb
    lat0 = b * lctx
    tbl = jnp.asarray(_na_tables(rows))
    bias = _na_bias_table(rpb)

    def win(i):
        def f(p, bi, t, tbl_ref):
            w = jnp.clip(2 * t - 1, 0, seq // wb - nwin)
            return (p, (lat0 + bi * seq) // wb + w + i, 0)
        return f

    kv_specs = [pl.BlockSpec((1, wb, 128), win(i)) for i in range(nwin)]
    attn = pl.pallas_call(
        functools.partial(_na_attn_kernel, n_tiles=n_tiles),
        grid_spec=pltpu.PrefetchScalarGridSpec(
            num_scalar_prefetch=1,
            grid=(npairs, b, n_tiles),
            in_specs=[pl.BlockSpec((1, tq, 128), lambda p, bi, t, tr: (p, (lat0 + bi * seq) // tq + t, 0))]
                     + kv_specs + kv_specs
                     + [pl.BlockSpec((1, lctx, 128), lambda p, bi, t, tr: (p, bi, 0)),
                        pl.BlockSpec((1, lctx, 128), lambda p, bi, t, tr: (p, bi, 0)),
                        pl.BlockSpec((2, 64, GRID_W, 2 * GRID_W), lambda p, bi, t, tr: (p, 0, 0, 0))],
            out_specs=pl.BlockSpec((tq, 128), lambda p, bi, t, tr: (bi * n_tiles + t, p)),
            scratch_shapes=[pltpu.VMEM((tq, NA_WROWS * GRID_W), F32)]),
        out_shape=jax.ShapeDtypeStruct((b * seq, d), BF16),
        compiler_params=_cparams(("parallel", "parallel", "arbitrary")),
        name="na_attn",
    )(tbl, q, k, k, k, k, v, v, v, v, k, v, bias)

    n_lat_tiles = b * seq // TM
    tile0 = lat0 // TM
    grp_l = functools.partial(_tile_group, geo=geo, tile0=tile0)
    return pl.pallas_call(
        _na_out_kernel,
        grid=(n_lat_tiles,),
        in_specs=[pl.BlockSpec((TM, d), lambda t: (t + tile0, 0)),
                  pl.BlockSpec((1, 6, d), lambda t: (grp_l(t), 0, 0)),
                  pl.BlockSpec((TM, d), lambda t: (t, 0)),
                  pl.BlockSpec((d, d), const2)],
        out_specs=pl.BlockSpec((TM, d), lambda t: (t, 0)),
        out_shape=jax.ShapeDtypeStruct((b * seq, d), F32),
        compiler_params=_cparams(("parallel",)),
        name="na_out",
    )(xs, mods, attn, w_out.astype(BF16))


SC_CORES = 2
SC_SUBCORES = 16
SC_WORKERS = SC_CORES * SC_SUBCORES
SLOT_T = 512


def _sc_mesh():
    return plsc.VectorSubcoreMesh(core_axis_name="c", subcore_axis_name="s")


def _sc_chunk(per_worker, max_chunk):
    return max(c for c in range(8, max_chunk + 1, 8) if per_worker % (2 * c) == 0)


def _sc_gather_rows(table, idx):
    dd = table.shape[1]
    bsz = idx.shape[0]
    per_w = bsz // SC_WORKERS
    assert per_w * SC_WORKERS == bsz
    chunk = _sc_chunk(per_w, 64)
    n_chunks = per_w // chunk

    def body(table_hbm, idx_hbm, out_hbm, idx_v, rows0, rows1, g0, g1, w0, w1):
        wid = lax.axis_index("s") * SC_CORES + lax.axis_index("c")
        pltpu.sync_copy(idx_hbm.at[wid], idx_v)
        base = wid * per_w

        def out_rows(j):
            return out_hbm.at[pl.ds(pl.multiple_of(base + j * chunk, 8), chunk)]

        def step(i, carry):
            j0, j1 = 2 * i, 2 * i + 1
            ga = pltpu.async_copy(table_hbm.at[idx_v.at[j0]], rows0, g0)
            gb = pltpu.async_copy(table_hbm.at[idx_v.at[j1]], rows1, g1)
            ga.wait()
            wa = pltpu.async_copy(rows0, out_rows(j0), w0)
            gb.wait()
            wb = pltpu.async_copy(rows1, out_rows(j1), w1)
            wa.wait()
            wb.wait()
            return carry

        lax.fori_loop(0, n_chunks // 2, step, 0)

    return pl.kernel(
        body, out_type=jax.ShapeDtypeStruct((bsz, dd), table.dtype), mesh=_sc_mesh(),
        scratch_types=[pltpu.VMEM((n_chunks, chunk), I32),
                       pltpu.VMEM((chunk, dd), table.dtype), pltpu.VMEM((chunk, dd), table.dtype),
                       pltpu.SemaphoreType.DMA, pltpu.SemaphoreType.DMA,
                       pltpu.SemaphoreType.DMA, pltpu.SemaphoreType.DMA],
        name="sc_gather_rows",
    )(table, idx.reshape(SC_WORKERS, n_chunks, chunk))


def _sc_scatter_rows(rows, dest, n_out):
    n, dd = rows.shape
    kk = dest.shape[0]
    per_w = n // SC_WORKERS
    assert per_w * SC_WORKERS == n
    chunk = _sc_chunk(per_w, 64)
    n_chunks = per_w // chunk
    dest_w = dest.reshape(kk, SC_WORKERS, n_chunks, chunk).transpose(1, 2, 0, 3)
    dest_w = dest_w.reshape(SC_WORKERS, n_chunks * kk, chunk)

    def body(rows_hbm, dest_hbm, out_hbm, idx_v, rows0, rows1, r0, r1, s0, s1):
        wid = lax.axis_index("s") * SC_CORES + lax.axis_index("c")
        pltpu.sync_copy(dest_hbm.at[wid], idx_v)
        base = wid * per_w

        def in_rows(j):
            return rows_hbm.at[pl.ds(pl.multiple_of(base + j * chunk, 8), chunk)]

        def step(i, carry):
            j0, j1 = 2 * i, 2 * i + 1
            ra = pltpu.async_copy(in_rows(j0), rows0, r0)
            rb = pltpu.async_copy(in_rows(j1), rows1, r1)
            ra.wait()
            sa = [pltpu.async_copy(rows0, out_hbm.at[idx_v.at[j0 * kk + k]], s0) for k in range(kk)]
            rb.wait()
            sb = [pltpu.async_copy(rows1, out_hbm.at[idx_v.at[j1 * kk + k]], s1) for k in range(kk)]
            for cp in sa + sb:
                cp.wait()
            return carry

        lax.fori_loop(0, n_chunks // 2, step, 0)

    return pl.kernel(
        body, out_type=jax.ShapeDtypeStruct((n_out, dd), rows.dtype), mesh=_sc_mesh(),
        scratch_types=[pltpu.VMEM((n_chunks * kk, chunk), I32),
                       pltpu.VMEM((chunk, dd), rows.dtype), pltpu.VMEM((chunk, dd), rows.dtype),
                       pltpu.SemaphoreType.DMA, pltpu.SemaphoreType.DMA,
                       pltpu.SemaphoreType.DMA, pltpu.SemaphoreType.DMA],
        name="sc_scatter_rows",
    )(rows, dest_w)


def _slot_kernel(idx_ref, rank_ref, cnt_ref, dest_ref, info_ref, *, bm):
    cnt = cnt_ref[...]
    ne = cnt.shape[0]
    padded = jnp.floor((cnt + (bm - 1.0)) * (1.0 / bm)) * bm
    acc = jnp.broadcast_to(padded, (ne, 128))
    row = lax.broadcasted_iota(I32, (ne, 128), 0)
    s = 1
    while s < ne:
        acc = acc + jnp.where(row >= s, pltpu.roll(acc, s, 0), 0.0)
        s *= 2
    pad_end = acc[:, 0:1]
    pad_start = pad_end - padded
    idx = idx_ref[...]
    e_iota = lax.broadcasted_iota(I32, (ne, idx.shape[1]), 0)
    starts = [jnp.sum(jnp.where(e_iota == idx[k:k + 1], pad_start, 0.0), axis=0, keepdims=True)
              for k in range(idx.shape[0])]
    dest_ref[...] = jnp.concatenate(starts, axis=0).astype(I32) + rank_ref[...]
    lane = lax.broadcasted_iota(I32, info_ref.shape, 1)
    info = jnp.where(lane == 0, pad_start, jnp.where(lane == 1, padded * (1.0 / bm), cnt))
    info_ref[...] = info.astype(I32)


def _route_kernel(x_ref, m_ref, g_ref, wr_ref, br_ref, h_ref, idx_ref, gate_ref, rank_ref, cnt_ref, run_ref):
    t = pl.program_id(0)

    @pl.when(t == 0)
    def _():
        run_ref[...] = jnp.zeros_like(run_ref)

    x = x_ref[...]
    m = m_ref[0]
    h = _modulate(x, g_ref[...], m[3:4], m[4:5]).astype(BF16)
    h_ref[...] = _pack_bf16_pairs(h)
    logits = _dot_nt(wr_ref[...], h) + br_ref[...]
    ne, tm = logits.shape
    e_iota = lax.broadcasted_iota(I32, (ne, tm), 0)
    vals, idxs = [], []
    l = logits
    for _ in range(TOP_K):
        mk = jnp.max(l, axis=0, keepdims=True)
        ik = jnp.min(jnp.where(l == mk, e_iota, ne), axis=0, keepdims=True)
        vals.append(mk)
        idxs.append(ik)
        l = jnp.where(e_iota == ik, -jnp.inf, l)
    top_val = jnp.concatenate(vals, axis=0)
    ex = jnp.exp(top_val - vals[0])
    gate_ref[...] = ex / jnp.sum(ex, axis=0, keepdims=True)
    idx_ref[...] = jnp.concatenate(idxs, axis=0)

    hits = [e_iota == ik for ik in idxs]
    cnt = hits[0].astype(F32)
    for hk in hits[1:]:
        cnt = cnt + hk.astype(F32)
    si = lax.broadcasted_iota(I32, (tm, tm), 0)
    ti = lax.broadcasted_iota(I32, (tm, tm), 1)
    before = (si < ti).astype(BF16)
    total = _dot(cnt.astype(BF16), before) + run_ref[...]
    ranks = [jnp.sum(jnp.where(hk, total, 0.0), axis=0, keepdims=True) for hk in hits]
    rank_ref[...] = jnp.concatenate(ranks, axis=0).astype(I32)
    run_ref[...] = run_ref[...] + jnp.sum(cnt, axis=1, keepdims=True)
    cnt_ref[...] = run_ref[...]


def _pack_bf16_pairs(v):
    bits = lax.bitcast_convert_type(v.astype(BF16).astype(F32), jnp.uint32)
    half = bits.shape[1] // 2
    return (bits[:, half:] & jnp.uint32(0xFFFF0000)) | (bits[:, :half] >> 16)


def _unpack_bf16_pairs(w):
    return (lax.bitcast_convert_type(w << 16, F32),
            lax.bitcast_convert_type(w & jnp.uint32(0xFFFF0000), F32))


def _expert_kernel(be_ref, nv_ref, x_ref, wgu_ref, bgu_ref, wd_ref, bd_ref, o_ref, wgu_bf, wd_bf, act_ref):
    blk = pl.program_id(0)
    ch = wgu_bf.shape[2]
    f = act_ref.shape[1]
    nc = f // ch
    live = nv_ref[blk]
    used = live > 0
    fresh = jnp.logical_or(blk == 0, be_ref[blk] != be_ref[jnp.maximum(blk - 1, 0)])

    @pl.when(jnp.logical_and(used, fresh))
    def _():
        for j in range(wgu_bf.shape[0]):
            wgu_bf[j] = wgu_ref[0, 0, :, j * ch:(j + 1) * ch].astype(BF16)
        for j in range(wd_bf.shape[0]):
            wd_bf[j] = wd_ref[0, 0, :, j * ch:(j + 1) * ch].astype(BF16)

    @pl.when(used)
    def _():
        w = x_ref[...]
        row = lax.broadcasted_iota(I32, w.shape, 0)
        w = jnp.where(row < live, w, jnp.zeros_like(w))
        lo, hi = _unpack_bf16_pairs(w)
        x = jnp.concatenate([lo.astype(BF16), hi.astype(BF16)], axis=1)
        for c in range(nc):
            c0 = slice(c * ch, (c + 1) * ch)
            c1 = slice(f + c * ch, f + (c + 1) * ch)
            glu = _dot(x, wgu_bf[c]) + bgu_ref[0, 0, :, c0]
            lin = _dot(x, wgu_bf[nc + c]) + bgu_ref[0, 0, :, c1]
            glu = jnp.minimum(glu, SWIGLU_LIMIT)
            lin = jnp.clip(lin, -SWIGLU_LIMIT, SWIGLU_LIMIT)
            act_ref[:, c0] = (glu * jax.nn.sigmoid(SWIGLU_ALPHA * glu) * (lin + 1.0)).astype(BF16)
        a = act_ref[...]
        y = jnp.concatenate([_dot(a, wd_bf[n]) for n in range(wd_bf.shape[0])], axis=1)
        o_ref[...] = _pack_bf16_pairs(y + bd_ref[0, 0])

    @pl.when(jnp.logical_not(used))
    def _():
        o_ref[...] = jnp.zeros_like(o_ref)


def _expert_call(start, nblk, cnt, buf, w_gu, b_gu, w_down, b_down, layer, bm):
    n_slots, dw = buf.shape
    _, ne, d, f2 = w_gu.shape
    f = f2 // 2
    n_blocks = n_slots // bm
    b0 = jnp.arange(n_blocks, dtype=I32) * bm
    end = start + nblk * bm
    block_e = jnp.minimum(jnp.sum((end[None, :] <= b0[:, None]).astype(I32), axis=1), ne - 1)
    n_live = jnp.clip(start[block_e] + cnt[block_e] - b0, 0, bm)
    lw = lambda i, be, nv: (layer, be[i], 0, 0)
    cw = 256
    return pl.pallas_call(
        _expert_kernel,
        grid_spec=pltpu.PrefetchScalarGridSpec(
            num_scalar_prefetch=2,
            grid=(n_blocks,),
            in_specs=[pl.BlockSpec((bm, dw), lambda i, be, nv: (i, 0)),
                      pl.BlockSpec((1, 1, d, f2), lw),
                      pl.BlockSpec((1, 1, 1, f2), lw),
                      pl.BlockSpec((1, 1, f, d), lw),
                      pl.BlockSpec((1, 1, 1, d), lw)],
            out_specs=pl.BlockSpec((bm, dw), lambda i, be, nv: (i, 0)),
            scratch_shapes=[pltpu.VMEM((f2 // cw, d, cw), BF16), pltpu.VMEM((d // cw, f, cw), BF16),
                            pltpu.VMEM((bm, f), BF16)]),
        out_shape=jax.ShapeDtypeStruct((n_slots, dw), jnp.uint32),
        compiler_params=_cparams(("arbitrary",)),
        name="moe_experts",
    )(block_e, n_live, buf, w_gu, b_gu.reshape(b_gu.shape[0], ne, 1, f2),
      w_down, b_down.reshape(b_down.shape[0], ne, 1, d))


def _expert_kernel_ring(start_ref, nblk_ref, cnt_ref, x_hbm, wgu_ref, bgu_ref, wd_ref, bd_ref, y_hbm,
                        xbuf, ybuf, wgu_bf, wd_bf, act_ref, sem_in, sem_out, *, n_blocks, fused=False):
    e = pl.program_id(0)
    bm = xbuf.shape[1]
    f = act_ref.shape[1]
    ch = 256
    nb = nblk_ref[e]
    start = start_ref[e]
    cnt = cnt_ref[e]

    def rows(j):
        return pl.ds(pl.multiple_of(start + j * bm, bm), bm)

    def fetch(j, slot):
        return pltpu.make_async_copy(x_hbm.at[rows(j)], xbuf.at[slot], sem_in.at[slot])

    def put(j, slot):
        return pltpu.make_async_copy(ybuf.at[slot], y_hbm.at[rows(j)], sem_out.at[slot])

    @pl.when(nb > 0)
    def _():
        fetch(0, 0).start()
        if wgu_bf.ndim == 3:
            for j in range(wgu_bf.shape[0]):
                wgu_bf[j] = wgu_ref[0, 0, :, j * ch:(j + 1) * ch].astype(BF16)
            for j in range(wd_bf.shape[0]):
                wd_bf[j] = wd_ref[0, 0, :, j * ch:(j + 1) * ch].astype(BF16)
        else:
            for j in range(wgu_bf.shape[1] // ch):
                cs = slice(j * ch, (j + 1) * ch)
                wgu_bf[:, cs] = wgu_ref[0, 0, :, cs].astype(BF16)
            for j in range(wd_bf.shape[1] // ch):
                cs = slice(j * ch, (j + 1) * ch)
                wd_bf[:, cs] = wd_ref[0, 0, :, cs].astype(BF16)

    def block(j, carry):
        slot = lax.rem(j, 2)

        @pl.when(j + 1 < nb)
        def _():
            fetch(j + 1, 1 - slot).start()

        fetch(j, slot).wait()

        @pl.when(j >= 2)
        def _():
            put(j - 2, slot).wait()

        w = xbuf[slot]
        row = lax.broadcasted_iota(I32, w.shape, 0)
        w = jnp.where(row < cnt - j * bm, w, jnp.zeros_like(w))
        lo, hi = _unpack_bf16_pairs(w)
        x = jnp.concatenate([lo.astype(BF16), hi.astype(BF16)], axis=1)
        if fused:
            z = _dot(x, wgu_bf[...]) + bgu_ref[0, 0]
            glu = jnp.minimum(z[:, :f], SWIGLU_LIMIT)
            lin = jnp.clip(z[:, f:], -SWIGLU_LIMIT, SWIGLU_LIMIT)
            act = (glu * jax.nn.sigmoid(SWIGLU_ALPHA * glu) * (lin + 1.0)).astype(BF16)
            ybuf[slot] = _pack_bf16_pairs(_dot(act, wd_bf[...]) + bd_ref[0, 0])
            put(j, slot).start()
            return carry
        if wgu_bf.ndim == 3:
            nc = f // ch
            for c in range(nc):
                c0 = slice(c * ch, (c + 1) * ch)
                c1 = slice(f + c * ch, f + (c + 1) * ch)
                glu = _dot(x, wgu_bf[c]) + bgu_ref[0, 0, :, c0]
                lin = _dot(x, wgu_bf[nc + c]) + bgu_ref[0, 0, :, c1]
                glu = jnp.minimum(glu, SWIGLU_LIMIT)
                lin = jnp.clip(lin, -SWIGLU_LIMIT, SWIGLU_LIMIT)
                act_ref[:, c0] = (glu * jax.nn.sigmoid(SWIGLU_ALPHA * glu) * (lin + 1.0)).astype(BF16)
            a = act_ref[...]
            y = jnp.concatenate([_dot(a, wd_bf[n]) for n in range(wd_bf.shape[0])], axis=1)
            ybuf[slot] = _pack_bf16_pairs(y + bd_ref[0, 0])
            put(j, slot).start()
            return carry
        for c in range(f // ch):
            c0 = slice(c * ch, (c + 1) * ch)
            c1 = slice(f + c * ch, f + (c + 1) * ch)
            glu = _dot(x, wgu_bf[:, c0]) + bgu_ref[0, 0, :, c0]
            lin = _dot(x, wgu_bf[:, c1]) + bgu_ref[0, 0, :, c1]
            glu = jnp.minimum(glu, SWIGLU_LIMIT)
            lin = jnp.clip(lin, -SWIGLU_LIMIT, SWIGLU_LIMIT)
            act_ref[:, c0] = (glu * jax.nn.sigmoid(SWIGLU_ALPHA * glu) * (lin + 1.0)).astype(BF16)
        ybuf[slot] = _pack_bf16_pairs(_dot(act_ref[...], wd_bf[...]) + bd_ref[0, 0])
        put(j, slot).start()
        return carry

    lax.fori_loop(0, nb, block, 0)

    @pl.when(nb >= 2)
    def _():
        put(nb - 2, lax.rem(nb, 2)).wait()

    @pl.when(nb >= 1)
    def _():
        put(nb - 1, lax.rem(nb + 1, 2)).wait()

    @pl.when(e == pl.num_programs(0) - 1)
    def _():
        ybuf[0] = jnp.zeros(ybuf.shape[1:], ybuf.dtype)

        def fill(j, carry):
            cp = pltpu.make_async_copy(ybuf.at[0], y_hbm.at[pl.ds(pl.multiple_of(j * bm, bm), bm)], sem_out.at[0])
            cp.start()
            cp.wait()
            return carry

        lax.fori_loop(start // bm + nb, n_blocks, fill, 0)


def _expert_call_ring(start, nblk, cnt, buf, w_gu, b_gu, w_down, b_down, layer, bm, fused=False, chunked=False):
    n_slots, dw = buf.shape
    _, ne, d, f2 = w_gu.shape
    f = f2 // 2
    lw = lambda e, *_: (layer, e, 0, 0)
    return pl.pallas_call(
        functools.partial(_expert_kernel_ring, n_blocks=n_slots // bm, fused=fused),
        grid_spec=pltpu.PrefetchScalarGridSpec(
            num_scalar_prefetch=3,
            grid=(ne,),
            in_specs=[pl.BlockSpec(memory_space=pl.ANY),
                      pl.BlockSpec((1, 1, d, f2), lw),
                      pl.BlockSpec((1, 1, 1, f2), lw),
                      pl.BlockSpec((1, 1, f, d), lw),
                      pl.BlockSpec((1, 1, 1, d), lw)],
            out_specs=pl.BlockSpec(memory_space=pl.ANY),
            scratch_shapes=[pltpu.VMEM((2, bm, dw), jnp.uint32), pltpu.VMEM((2, bm, dw), jnp.uint32),
                            pltpu.VMEM((f2 // 256, d, 256) if chunked else (d, f2), BF16),
                            pltpu.VMEM((d // 256, f, 256) if chunked else (f, d), BF16), pltpu.VMEM((bm, f), BF16),
                            pltpu.SemaphoreType.DMA((2,)), pltpu.SemaphoreType.DMA((2,))]),
        out_shape=jax.ShapeDtypeStruct((n_slots, dw), jnp.uint32),
        compiler_params=_cparams(("arbitrary",)),
        name="moe_experts",
    )(start, nblk, cnt, buf, w_gu, b_gu.reshape(b_gu.shape[0], ne, 1, f2),
      w_down, b_down.reshape(b_down.shape[0], ne, 1, d))


def _combine_kernel(x_ref, m_ref, y_ref, gate_ref, gout_ref, o_ref, *, final):
    m = m_ref[0]
    gate = gate_ref[...]
    half = y_ref.shape[2]
    f_lo, f_hi = None, None
    for k in range(TOP_K):
        lo, hi = _unpack_bf16_pairs(y_ref[k])
        gk = gate[:, k:k + 1]
        f_lo = gk * lo if f_lo is None else f_lo + gk * lo
        f_hi = gk * hi if f_hi is None else f_hi + gk * hi
    x_lo = x_ref[:, :half] + m[5:6, :half] * f_lo
    x_hi = x_ref[:, half:] + m[5:6, half:] * f_hi
    if final:
        ms = (jnp.sum(x_lo * x_lo, axis=-1, keepdims=True) + jnp.sum(x_hi * x_hi, axis=-1, keepdims=True))
        r = lax.rsqrt(ms * (0.5 / half) + NORM_EPS)
        x_lo = x_lo * r * gout_ref[:, :half]
        x_hi = x_hi * r * gout_ref[:, half:]
    o_ref[:, :half] = x_lo
    o_ref[:, half:] = x_hi


def _moe_layer(xs, mods, g, layer, w_router, b_router, w_gu, b_gu, w_down, b_down, geo, g_out, final):
    n, d = xs.shape
    ne = w_router.shape[1]
    f = w_down.shape[2]
    n_tiles = n // TM
    grp = functools.partial(_tile_group, geo=geo)
    const2 = lambda t: (0, 0)
    h, idx, gate, rank, cnt = pl.pallas_call(
        _route_kernel,
        grid=(n_tiles,),
        in_specs=[pl.BlockSpec((TM, d), lambda t: (t, 0)),
                  pl.BlockSpec((1, 6, d), lambda t: (grp(t), 0, 0)),
                  pl.BlockSpec((1, d), const2),
                  pl.BlockSpec((ne, d), const2),
                  pl.BlockSpec((ne, 1), const2)],
        out_specs=[pl.BlockSpec((TM, d // 2), lambda t: (t, 0)),
                   pl.BlockSpec((TOP_K, TM), lambda t: (0, t)),
                   pl.BlockSpec((TOP_K, TM), lambda t: (0, t)),
                   pl.BlockSpec((TOP_K, TM), lambda t: (0, t)),
                   pl.BlockSpec((ne, 1), const2)],
        out_shape=[jax.ShapeDtypeStruct((n, d // 2), jnp.uint32),
                   jax.ShapeDtypeStruct((TOP_K, n), I32),
                   jax.ShapeDtypeStruct((TOP_K, n), F32),
                   jax.ShapeDtypeStruct((TOP_K, n), I32),
                   jax.ShapeDtypeStruct((ne, 1), F32)],
        scratch_shapes=[pltpu.VMEM((ne, 1), F32)],
        compiler_params=_cparams(("arbitrary",)),
        name="moe_route",
    )(xs, mods, g.reshape(1, d), w_router.T.astype(BF16), b_router.reshape(ne, 1))

    bm = MOE_BM
    n_rows = n * TOP_K
    n_blocks = -(-n_rows // bm) + ne
    n_slots = n_blocks * bm
    dest, info = pl.pallas_call(
        functools.partial(_slot_kernel, bm=bm),
        grid=(n // SLOT_T,),
        in_specs=[pl.BlockSpec((TOP_K, SLOT_T), lambda t: (0, t)),
                  pl.BlockSpec((TOP_K, SLOT_T), lambda t: (0, t)),
                  pl.BlockSpec((ne, 1), const2)],
        out_specs=[pl.BlockSpec((TOP_K, SLOT_T), lambda t: (0, t)),
                   pl.BlockSpec((ne, 128), const2)],
        out_shape=[jax.ShapeDtypeStruct((TOP_K, n), I32),
                   jax.ShapeDtypeStruct((ne, 128), I32)],
        compiler_params=_cparams(("arbitrary",)),
        name="moe_slots",
    )(idx, rank, cnt)

    buf = _sc_scatter_rows(h, dest, n_slots)
    y = _expert_call(info[:, 0], info[:, 1], info[:, 2], buf, w_gu, b_gu, w_down, b_down, layer, bm)
    dy = d // 2
    yk = _sc_gather_rows(y, dest.reshape(-1)).reshape(TOP_K, n, dy)
    return pl.pallas_call(
        functools.partial(_combine_kernel, final=final),
        grid=(n_tiles,),
        in_specs=[pl.BlockSpec((TM, d), lambda t: (t, 0)),
                  pl.BlockSpec((1, 6, d), lambda t: (grp(t), 0, 0)),
                  pl.BlockSpec((TOP_K, TM, dy), lambda t: (0, t, 0)),
                  pl.BlockSpec((TM, TOP_K), lambda t: (t, 0)),
                  pl.BlockSpec((1, d), const2)],
        out_specs=pl.BlockSpec((TM, d), lambda t: (t, 0)),
        out_shape=jax.ShapeDtypeStruct((n, d), F32),
        compiler_params=_cparams(("parallel",)),
        name="moe_combine",
    )(xs, mods, yk, gate.T, g_out.reshape(1, d))


def kernel(x, c, ctx, c_ctx, ada_w, ada_b, norm_mix, norm_ffn, norm_out, a_w_in, a_g_v, a_w_s, a_b_s, a_w_out, b_w_in, b_w_a2, b_b_a, b_g_o, b_w_out, c_w_qkv, c_rpb, c_w_out, moe_w_router, moe_b_router, moe_w_gu, moe_b_gu, moe_w_down, moe_b_down):
    b, seq, d = x.shape
    lctx = ctx.shape[1]
    depth = ada_w.shape[0]
    assert lctx % TM == 0 and seq % (NA_QROWS * GRID_W) == 0 and (b * lctx) % (NA_QROWS * GRID_W) == 0
    assert seq // GRID_W >= NA_WROWS + NA_QROWS and b + 1 <= 8
    geo = (b * lctx // TM, seq // TM, b)
    dims = (b, seq, lctx)

    cond = jnp.zeros((8, d), F32).at[:b].set(c).at[b].set(c_ctx)
    mods_all = _adaln(cond, ada_w, ada_b)[:, :b + 1].reshape(depth, b + 1, 6, d)

    xs = jnp.concatenate([ctx.reshape(b * lctx, d), x.reshape(b * seq, d)], axis=0)
    ctx_tiles = geo[0]
    has_ctx = True
    for i in range(depth):
        kind, j = i % N_MIXERS, i // N_MIXERS
        ctx_later = any(kk % N_MIXERS != 0 for kk in range(i + 1, depth))
        mods = mods_all[i]
        if kind == 0:
            keep_ctx = has_ctx and ctx_later
            skip = 0 if keep_ctx or not has_ctx else ctx_tiles
            geo_i = geo if has_ctx else (0, geo[1], b)
            xs = _gmlp_mixer(xs, mods, norm_mix[i], a_w_in[j], a_g_v[j], a_w_s[j], a_b_s[j], a_w_out[j],
                             skip, xs.shape[0] // TM - skip, geo_i)
            has_ctx = keep_ctx
        elif kind == 1:
            assert has_ctx
            xs = _gla_mixer(xs, mods, norm_mix[i], b_w_in[j], b_w_a2[j], b_b_a[j], b_g_o[j], b_w_out[j], geo, dims)
            if not ctx_later:
                xs = xs[b * lctx:]
                has_ctx = False
        else:
            assert has_ctx
            if ctx_later:
                raise NotImplementedError("context output of the neighbourhood mixer")
            xs = _na_mixer(xs, mods, norm_mix[i], c_w_qkv[j], c_rpb[j], c_w_out[j], geo, dims)
            has_ctx = False
        geo_i = geo if has_ctx else (0, geo[1], b)
        xs = _moe_layer(xs, mods, norm_ffn[i], i, moe_w_router[i], moe_b_router[i], moe_w_gu, moe_b_gu,
                        moe_w_down, moe_b_down, geo_i, norm_out, i == depth - 1)
    if has_ctx:
        xs = xs[b * lctx:]
    return xs.reshape(b, seq, d)
```

```python
import functools

import numpy as np
import jax
import jax.numpy as jnp
from jax import lax
from jax.experimental import pallas as pl
from jax.experimental.pallas import tpu as pltpu
from jax.experimental.pallas import tpu_sc as plsc

F32 = jnp.float32
BF16 = jnp.bfloat16
I32 = jnp.int32

NORM_EPS = 1e-6
GRID_W = 64
N_MIXERS = 3

CHUNK_A = 128
A_GROUPS = 8
GLA_HEADS = 4
GLA_RANK = 16
GLA_TAU = 16.0
GLA_CHUNK = 64
ROPE_BASE = 10000.0
NA_HEADS = 16
NA_KH = 8
NA_KW = 16
NEG_INF = -1e30
NA_QROWS = 4
NA_WROWS = 12
NA_PAIRS_PER_STEP = 2
TOP_K = 4
SWIGLU_LIMIT = 7.0
SWIGLU_ALPHA = 1.702
MOE_BM = 512

TM = 512
GLA_TB = 256
VMEM_LIMIT = 56 * 1024 * 1024


def _cparams(sem):
    return pltpu.CompilerParams(dimension_semantics=sem, vmem_limit_bytes=VMEM_LIMIT)


def _dot(a, b):
    return jnp.dot(a, b, preferred_element_type=F32)


def _dot_nt(a, b):
    return lax.dot_general(a, b, (((1,), (1,)), ((), ())), preferred_element_type=F32)


def _dot_tn(a, b):
    return lax.dot_general(a, b, (((0,), (0,)), ((), ())), preferred_element_type=F32)


def _rms(x, g):
    return x * lax.rsqrt(jnp.mean(x * x, axis=-1, keepdims=True) + NORM_EPS) * g


def _modulate(x, g, shift, scale):
    return _rms(x, g) * (1.0 + scale) + shift


def _ada_kernel(s_ref, w_ref, b_ref, o_ref):
    s = s_ref[...]
    s = s * jax.nn.sigmoid(s)
    o_ref[0] = jnp.dot(s, w_ref[0], preferred_element_type=F32,
                       precision=lax.Precision.HIGHEST) + b_ref[0]


def _adaln(cond, ada_w, ada_b):
    depth, d, n6 = ada_w.shape
    bn = n6 // 4
    return pl.pallas_call(
        _ada_kernel,
        grid=(depth, n6 // bn),
        in_specs=[pl.BlockSpec((8, d), lambda i, j: (0, 0)),
                  pl.BlockSpec((1, d, bn), lambda i, j: (i, 0, j)),
                  pl.BlockSpec((1, 1, bn), lambda i, j: (i, 0, j))],
        out_specs=pl.BlockSpec((1, 8, bn), lambda i, j: (i, 0, j)),
        out_shape=jax.ShapeDtypeStruct((depth, 8, n6), F32),
        compiler_params=_cparams(("parallel", "parallel")),
        name="adaln",
    )(cond, ada_w, ada_b.reshape(depth, 1, n6))


def _gelu(z):
    return 0.5 * z * (1.0 + lax.erf(z * np.float32(np.sqrt(0.5))))


def _gmlp_kernel(x_ref, m_ref, g_ref, win_ref, gv_ref, ws_ref, bs_ref, wout_ref, o_ref):
    x = x_ref[...]
    m = m_ref[0]
    a = gv_ref.shape[1]
    h = _modulate(x, g_ref[...], m[0:1], m[1:2]).astype(BF16)
    z = _gelu(_dot(h, win_ref[...]))
    u = z[:, :a]
    v = _rms(z[:, a:], gv_ref[...]).astype(BF16)
    gw = a // A_GROUPS
    rows = []
    for c in range(x.shape[0] // CHUNK_A):
        cols = [_dot(ws_ref[g], v[c * CHUNK_A:(c + 1) * CHUNK_A, g * gw:(g + 1) * gw])
                for g in range(A_GROUPS)]
        rows.append(jnp.concatenate(cols, axis=1) + bs_ref[...])
    s = jnp.concatenate(rows, axis=0)
    y = _dot((u * s).astype(BF16), wout_ref[...])
    o_ref[...] = x + m[2:3] * y


def _gmlp_mixer(xs, mods, g, w_in, g_v, w_s, b_s, w_out, tile0, n_tiles, geo):
    d = xs.shape[1]
    a = g_v.shape[0]
    gw = a // A_GROUPS
    bias = jnp.repeat(b_s.T, gw, axis=1)
    grp = functools.partial(_tile_group, geo=geo, tile0=tile0)
    const2 = lambda t: (0, 0)
    return pl.pallas_call(
        _gmlp_kernel,
        grid=(n_tiles,),
        in_specs=[pl.BlockSpec((TM, d), lambda t: (t + tile0, 0)),
                  pl.BlockSpec((1, 6, d), lambda t: (grp(t), 0, 0)),
                  pl.BlockSpec((1, d), const2),
                  pl.BlockSpec((d, 2 * a), const2),
                  pl.BlockSpec((1, a), const2),
                  pl.BlockSpec((A_GROUPS, CHUNK_A, CHUNK_A), lambda t: (0, 0, 0)),
                  pl.BlockSpec((CHUNK_A, a), const2),
                  pl.BlockSpec((a, d), const2)],
        out_specs=pl.BlockSpec((TM, d), lambda t: (t, 0)),
        out_shape=jax.ShapeDtypeStruct((n_tiles * TM, d), F32),
        compiler_params=_cparams(("parallel",)),
        name="gmlp_mixer",
    )(xs, mods, g.reshape(1, d), w_in.astype(BF16), g_v.reshape(1, a), w_s.astype(BF16),
      bias, w_out.astype(BF16))


def _tile_group(t, geo, tile0=0):
    n_ctx_tiles, tiles_per_batch, nb = geo
    tt = t + tile0
    return jnp.where(tt < n_ctx_tiles, nb, (tt - n_ctx_tiles) // tiles_per_batch)


def _dot_f32(tri_bf16, x):
    hi = x.astype(BF16)
    r1 = x - hi.astype(F32)
    mid = r1.astype(BF16)
    lo = (r1 - mid.astype(F32)).astype(BF16)
    return _dot(tri_bf16, hi) + _dot(tri_bf16, mid) + _dot(tri_bf16, lo)


def _gla_proj_kernel(x_ref, m_ref, g_ref, wq_ref, wk_ref, wv_ref, wg_ref, wr_ref, wa_ref, ba_ref,
                     cos_ref, sin_ref, q_ref, k_ref, v_ref, go_ref, la_ref):
    x = x_ref[...]
    m = m_ref[0]
    h = _modulate(x, g_ref[...], m[0:1], m[1:2]).astype(BF16)
    kdim = wq_ref.shape[1]
    hk = kdim // GLA_HEADS
    q = _dot(h, wq_ref[...]) * np.float32(hk ** -0.5)
    k = _dot(h, wk_ref[...])
    v_ref[...] = _dot(h, wv_ref[...]).astype(BF16)
    go_ref[...] = _dot(h, wg_ref[...])
    r = _dot(h, wr_ref[...])
    z = _dot(r.astype(BF16), wa_ref[...]) + ba_ref[...]
    la_ref[...] = jax.nn.log_sigmoid(z) * np.float32(1.0 / GLA_TAU)

    cos = jnp.concatenate([cos_ref[...]] * GLA_HEADS, axis=1)
    sin = jnp.concatenate([sin_ref[...]] * GLA_HEADS, axis=1)
    nf = hk // 4
    lane = lax.broadcasted_iota(I32, q.shape, 1)
    first = (lane % (2 * nf)) < nf

    def rope(t):
        up = pltpu.roll(t, kdim - nf, 1)
        dn = pltpu.roll(t, nf, 1)
        return t * cos + jnp.where(first, up, dn) * sin

    q_ref[...] = rope(q)
    k_ref[...] = rope(k)


def _gla_scan_kernel(q_ref, k_ref, v_ref, la_ref, o_ref, st_ref, *, rev):
    s = pl.program_id(1)

    @pl.when(s == 0)
    def _():
        st_ref[...] = jnp.zeros_like(st_ref)

    tb, kdim = q_ref.shape
    hk = kdim // GLA_HEADS
    hv = v_ref.shape[1] // GLA_HEADS
    c = GLA_CHUNK
    ri = lax.broadcasted_iota(I32, (c, c), 0)
    ci = lax.broadcasted_iota(I32, (c, c), 1)
    keep = (ci >= ri) if rev else (ci <= ri)
    tri = keep.astype(BF16)
    order = range(tb // c)
    for n in (reversed(order) if rev else order):
        rows = slice(n * c, (n + 1) * c)
        cum = _dot_f32(tri, la_ref[rows, :])
        last = cum[0:1] if rev else cum[c - 1:c]
        q = q_ref[rows, :]
        k = k_ref[rows, :]
        q_dec = (q * jnp.exp(cum)).astype(BF16)
        k_inv = (k * jnp.exp(-cum)).astype(BF16)
        k_end = (k * jnp.exp(last - cum)).astype(BF16)
        dec = jnp.exp(last)
        for h in range(GLA_HEADS):
            ks = slice(h * hk, (h + 1) * hk)
            vs = slice(h * hv, (h + 1) * hv)
            att = jnp.where(keep, _dot_nt(q_dec[:, ks], k_inv[:, ks]), 0.0).astype(BF16)
            vh = v_ref[rows, vs]
            st = st_ref[h]
            o_ref[rows, vs] = _dot(att, vh) + _dot_nt(q_dec[:, ks], st.astype(BF16))
            st_ref[h] = st * dec[:, ks] + _dot_tn(vh, k_end[:, ks])


def _gla_out_kernel(x_ref, m_ref, of_ref, ob_ref, go_ref, gn_ref, wout_ref, o_ref):
    x = x_ref[...]
    m = m_ref[0]
    o = of_ref[...] + ob_ref[...]
    hv = gn_ref.shape[1]
    parts = [_rms(o[:, h * hv:(h + 1) * hv], gn_ref[...]) for h in range(GLA_HEADS)]
    o = jnp.concatenate(parts, axis=1)
    gate = go_ref[...]
    y = _dot((o * (gate * jax.nn.sigmoid(gate))).astype(BF16), wout_ref[...])
    o_ref[...] = x + m[2:3] * y


def _gla_mixer(xs, mods, g, w_in, w_a2, b_a, g_o, w_out, geo, dims):
    b, seq, lctx = dims
    nt, d = xs.shape
    n_tiles = nt // TM
    kdim = w_a2.shape[2]
    vdim = g_o.shape[0] * GLA_HEADS
    hk = kdim // GLA_HEADS
    nf = hk // 4
    wq = w_in[:, :kdim].astype(BF16)
    wk = w_in[:, kdim:2 * kdim].astype(BF16)
    wv = w_in[:, 2 * kdim:2 * kdim + vdim].astype(BF16)
    wg = w_in[:, 2 * kdim + vdim:2 * kdim + 2 * vdim].astype(BF16)
    wr = jnp.pad(w_in[:, 2 * kdim + 2 * vdim:], ((0, 0), (0, 128 - 2 * GLA_RANK))).astype(BF16)
    wa = jnp.zeros((128, 2 * kdim), F32)
    wa = wa.at[:GLA_RANK, :kdim].set(w_a2[0]).at[GLA_RANK:2 * GLA_RANK, kdim:].set(w_a2[1]).astype(BF16)
    ba = b_a.reshape(1, 2 * kdim)

    tpos = jnp.arange(seq)
    freqs = jnp.power(ROPE_BASE, -jnp.arange(nf, dtype=F32) / nf)
    ar = (tpos // GRID_W).astype(F32)[:, None] * freqs
    ac = (tpos % GRID_W).astype(F32)[:, None] * freqs
    cos = jnp.concatenate([jnp.cos(ar), jnp.cos(ar), jnp.cos(ac), jnp.cos(ac)], axis=1)
    sin = jnp.concatenate([-jnp.sin(ar), jnp.sin(ar), -jnp.sin(ac), jnp.sin(ac)], axis=1)
    cos = jnp.concatenate([jnp.ones((TM, hk), F32), cos], axis=0)
    sin = jnp.concatenate([jnp.zeros((TM, hk), F32), sin], axis=0)

    n_ctx_tiles, tiles_per_batch, _ = geo
    grp = functools.partial(_tile_group, geo=geo)

    def rope_blk(t):
        return (jnp.where(t < n_ctx_tiles, 0, 1 + (t - n_ctx_tiles) % tiles_per_batch), 0)

    const2 = lambda t: (0, 0)
    row = lambda t: (t, 0)
    q, k, v, go, la = pl.pallas_call(
        _gla_proj_kernel,
        grid=(n_tiles,),
        in_specs=[pl.BlockSpec((TM, d), row),
                  pl.BlockSpec((1, 6, d), lambda t: (grp(t), 0, 0)),
                  pl.BlockSpec((1, d), const2),
                  pl.BlockSpec((d, kdim), const2), pl.BlockSpec((d, kdim), const2),
                  pl.BlockSpec((d, vdim), const2), pl.BlockSpec((d, vdim), const2),
                  pl.BlockSpec((d, 128), const2), pl.BlockSpec((128, 2 * kdim), const2),
                  pl.BlockSpec((1, 2 * kdim), const2),
                  pl.BlockSpec((TM, hk), rope_blk), pl.BlockSpec((TM, hk), rope_blk)],
        out_specs=[pl.BlockSpec((TM, kdim), row), pl.BlockSpec((TM, kdim), row),
                   pl.BlockSpec((TM, vdim), row), pl.BlockSpec((TM, vdim), row),
                   pl.BlockSpec((TM, 2 * kdim), row)],
        out_shape=[jax.ShapeDtypeStruct((nt, kdim), F32), jax.ShapeDtypeStruct((nt, kdim), F32),
                   jax.ShapeDtypeStruct((nt, vdim), BF16), jax.ShapeDtypeStruct((nt, vdim), F32),
                   jax.ShapeDtypeStruct((nt, 2 * kdim), F32)],
        compiler_params=_cparams(("parallel",)),
        name="gla_proj",
    )(xs, mods, g.reshape(1, d), wq, wk, wv, wg, wr, wa, ba, cos, sin)

    tb = GLA_TB
    ctx_steps = lctx // tb
    lat_steps = seq // tb
    steps = ctx_steps + lat_steps

    def scan(rev):
        def blk(bi, s):
            if rev:
                cs, ls = ctx_steps - 1 - s, lat_steps - 1 - (s - ctx_steps)
            else:
                cs, ls = s, s - ctx_steps
            return jnp.where(s < ctx_steps, bi * ctx_steps + cs, b * ctx_steps + bi * lat_steps + ls)

        d_col = 1 if rev else 0
        return pl.pallas_call(
            functools.partial(_gla_scan_kernel, rev=rev),
            grid=(b, steps),
            in_specs=[pl.BlockSpec((tb, kdim), lambda bi, s: (blk(bi, s), 0)),
                      pl.BlockSpec((tb, kdim), lambda bi, s: (blk(bi, s), 0)),
                      pl.BlockSpec((tb, vdim), lambda bi, s: (blk(bi, s), 0)),
                      pl.BlockSpec((tb, kdim), lambda bi, s: (blk(bi, s), d_col))],
            out_specs=pl.BlockSpec((tb, vdim), lambda bi, s: (blk(bi, s), 0)),
            out_shape=jax.ShapeDtypeStruct((nt, vdim), F32),
            scratch_shapes=[pltpu.VMEM((GLA_HEADS, vdim // GLA_HEADS, hk), F32)],
            compiler_params=_cparams(("parallel", "arbitrary")),
            name="gla_scan_bwd" if rev else "gla_scan_fwd",
        )(q, k, v, la)

    o_f = scan(False)
    o_b = scan(True)

    return pl.pallas_call(
        _gla_out_kernel,
        grid=(n_tiles,),
        in_specs=[pl.BlockSpec((TM, d), row),
                  pl.BlockSpec((1, 6, d), lambda t: (grp(t), 0, 0)),
                  pl.BlockSpec((TM, vdim), row), pl.BlockSpec((TM, vdim), row),
                  pl.BlockSpec((TM, vdim), row),
                  pl.BlockSpec((1, vdim // GLA_HEADS), const2),
                  pl.BlockSpec((vdim, d), const2)],
        out_specs=pl.BlockSpec((TM, d), row),
        out_shape=jax.ShapeDtypeStruct((nt, d), F32),
        compiler_params=_cparams(("parallel",)),
        name="gla_out",
    )(xs, mods, o_f, o_b, go, g_o.reshape(1, -1), w_out.astype(BF16))


def _na_proj_kernel(x_ref, m_ref, g_ref, w_ref, q_ref, k_ref, v_ref):
    x = x_ref[...]
    m = m_ref[0]
    d = x.shape[1]
    h = _modulate(x, g_ref[...], m[0:1], m[1:2]).astype(BF16)
    qkv = _dot(h, w_ref[...])
    hd = d // NA_HEADS
    for p in range(q_ref.shape[0]):
        cs = slice(p * 128, (p + 1) * 128)
        q_ref[p] = (qkv[:, cs] * np.float32(hd ** -0.5)).astype(BF16)
        k_ref[p] = qkv[:, d + p * 128:d + (p + 1) * 128].astype(BF16)
        v_ref[p] = qkv[:, 2 * d + p * 128:2 * d + (p + 1) * 128].astype(BF16)


def _na_attn_kernel(tbl_ref, q_ref, *refs, n_tiles, nwin):
    k_refs, v_refs = refs[:nwin], refs[nwin:2 * nwin]
    kc_ref, vc_ref, bias_ref, o_ref, s_ref = refs[2 * nwin:]
    t = pl.program_id(2)
    typ = jnp.where(t == 0, 0, jnp.where(t == n_tiles - 1, 2, 1))
    npair = NA_WROWS // 2
    lane = lax.broadcasted_iota(I32, q_ref.shape[1:], 1)
    outs = []
    for pp in range(q_ref.shape[0]):
        q = q_ref[pp]
        kw = jnp.concatenate([r[pp] for r in k_refs], axis=0)
        vw = jnp.concatenate([r[pp] for r in v_refs], axis=0)
        kc = kc_ref[pp]
        vc = vc_ref[pp]
        heads = []
        for hh in range(2):
            sel = (lane < 64) if hh == 0 else (lane >= 64)
            qm = jnp.where(sel, q, jnp.zeros_like(q))
            s = _dot_nt(qm, kw)
            s_ctx = _dot_nt(qm, kc)
            sl = s_ref.at[2 * pp + hh]
            for dr in range(NA_QROWS):
                for j in range(npair):
                    e = tbl_ref[typ * (NA_QROWS * npair) + dr * npair + j]
                    rs = slice(dr * GRID_W, (dr + 1) * GRID_W)
                    cs = slice(j * 128, (j + 1) * 128)
                    sl[rs, cs] = s[rs, cs] + bias_ref[2 * pp + hh, e]
            s_lat = sl[...]
            mx = jnp.maximum(jnp.max(s_lat, axis=1, keepdims=True), jnp.max(s_ctx, axis=1, keepdims=True))
            e_lat = jnp.exp(s_lat - mx)
            e_ctx = jnp.exp(s_ctx - mx)
            inv = 1.0 / (jnp.sum(e_lat, axis=1, keepdims=True) + jnp.sum(e_ctx, axis=1, keepdims=True))
            heads.append(_dot((e_lat * inv).astype(BF16), vw) + _dot((e_ctx * inv).astype(BF16), vc))
        outs.append(jnp.where(lane < 64, heads[0], heads[1]))
    o_ref[...] = jnp.concatenate(outs, axis=1).astype(BF16)


def _na_out_kernel(x_ref, m_ref, a_ref, w_ref, o_ref):
    m = m_ref[0]
    o_ref[...] = x_ref[...] + m[2:3] * _dot(a_ref[...], w_ref[...])


def _na_tables(rows):
    npair = NA_WROWS // 2
    tbl = np.zeros((3, NA_QROWS, npair), np.int32)
    for typ, r_base in enumerate((0, NA_QROWS, rows - NA_QROWS)):
        w0 = int(np.clip(r_base - NA_KH // 2, 0, rows - NA_WROWS))
        for dr in range(NA_QROWS):
            r = r_base + dr
            r0 = int(np.clip(r - NA_KH // 2, 0, rows - NA_KH))
            for j in range(npair):
                kr = (w0 + 2 * j, w0 + 2 * j + 1)
                ok = [r0 <= x < r0 + NA_KH for x in kr]
                ri = [x - r + NA_KH - 1 for x in kr]
                if ok[0] and ok[1]:
                    e = ri[0]
                elif ok[0]:
                    e = 16 + ri[0]
                elif ok[1]:
                    e = 32 + ri[1]
                else:
                    e = 63
                tbl[typ, dr, j] = e
    return tbl.reshape(-1)


def _na_bias_table(rpb):
    nh = rpb.shape[0]
    qc = np.arange(GRID_W)
    cstart = np.clip(qc - NA_KW // 2, 0, GRID_W - NA_KW)
    kc = np.arange(GRID_W)
    ok = (kc[None, :] >= cstart[:, None]) & (kc[None, :] < cstart[:, None] + NA_KW)
    cidx = np.clip(kc[None, :] - qc[:, None] + NA_KW - 1, 0, 2 * NA_KW - 2)
    cb = jnp.where(ok[None, None], rpb[:, :, cidx], NEG_INF).astype(F32)
    neg = jnp.full((nh, 1, GRID_W, GRID_W), NEG_INF, F32)
    cb = jnp.concatenate([cb, neg], axis=1)
    negs = jnp.broadcast_to(neg, cb.shape)
    nxt = jnp.concatenate([cb[:, 1:], neg], axis=1)
    both = jnp.concatenate([cb, nxt], axis=-1)
    left = jnp.concatenate([cb, negs], axis=-1)
    right = jnp.concatenate([negs, cb], axis=-1)
    none = jnp.concatenate([negs, negs], axis=-1)
    return jnp.concatenate([both, left, right, none], axis=1)


def _na_mixer(xs, mods, g, w_qkv, rpb, w_out, geo, dims):
    b, seq, lctx = dims
    nt, d = xs.shape
    n_tiles_tok = nt // TM
    grp = functools.partial(_tile_group, geo=geo)
    npairs = d // 128
    const2 = lambda t: (0, 0)
    q, k, v = pl.pallas_call(
        _na_proj_kernel,
        grid=(n_tiles_tok,),
        in_specs=[pl.BlockSpec((TM, d), lambda t: (t, 0)),
                  pl.BlockSpec((1, 6, d), lambda t: (grp(t), 0, 0)),
                  pl.BlockSpec((1, d), const2),
                  pl.BlockSpec((d, 3 * d), const2)],
        out_specs=[pl.BlockSpec((npairs, TM, 128), lambda t: (0, t, 0))] * 3,
        out_shape=[jax.ShapeDtypeStruct((npairs, nt, 128), BF16)] * 3,
        compiler_params=_cparams(("parallel",)),
        name="na_proj",
    )(xs, mods, g.reshape(1, d), w_qkv.astype(BF16))

    rows = seq // GRID_W
    tq = NA_QROWS * GRID_W
    n_tiles = rows // NA_QROWS
    wb = 256
    nwin = NA_WROWS * GRID_W // wb
    lat0 = b * lctx
    tbl = jnp.asarray(_na_tables(rows))
    bias = _na_bias_table(rpb)

    pp = NA_PAIRS_PER_STEP
    assert npairs % pp == 0 and (NA_QROWS * GRID_W) % wb == 0 and (NA_KH // 2 * GRID_W) % wb == 0

    def win(i):
        def f(p, bi, t, tbl_ref):
            w = jnp.clip(t * (NA_QROWS * GRID_W // wb) - NA_KH // 2 * GRID_W // wb, 0, seq // wb - nwin)
            return (p, (lat0 + bi * seq) // wb + w + i, 0)
        return f

    kv_specs = [pl.BlockSpec((pp, wb, 128), win(i)) for i in range(nwin)]
    attn = pl.pallas_call(
        functools.partial(_na_attn_kernel, n_tiles=n_tiles, nwin=nwin),
        grid_spec=pltpu.PrefetchScalarGridSpec(
            num_scalar_prefetch=1,
            grid=(npairs // pp, b, n_tiles),
            in_specs=[pl.BlockSpec((pp, tq, 128), lambda p, bi, t, tr: (p, (lat0 + bi * seq) // tq + t, 0))]
                     + kv_specs + kv_specs
                     + [pl.BlockSpec((pp, lctx, 128), lambda p, bi, t, tr: (p, bi, 0)),
                        pl.BlockSpec((pp, lctx, 128), lambda p, bi, t, tr: (p, bi, 0)),
                        pl.BlockSpec((2 * pp, 64, GRID_W, 2 * GRID_W), lambda p, bi, t, tr: (p, 0, 0, 0))],
            out_specs=pl.BlockSpec((tq, 128 * pp), lambda p, bi, t, tr: (bi * n_tiles + t, p)),
            scratch_shapes=[pltpu.VMEM((2 * pp, tq, NA_WROWS * GRID_W), F32)]),
        out_shape=jax.ShapeDtypeStruct((b * seq, d), BF16),
        compiler_params=_cparams(("parallel", "parallel", "arbitrary")),
        name="na_attn",
    )(tbl, q, *([k] * nwin), *([v] * nwin), k, v, bias)

    n_lat_tiles = b * seq // TM
    tile0 = lat0 // TM
    grp_l = functools.partial(_tile_group, geo=geo, tile0=tile0)
    return pl.pallas_call(
        _na_out_kernel,
        grid=(n_lat_tiles,),
        in_specs=[pl.BlockSpec((TM, d), lambda t: (t + tile0, 0)),
                  pl.BlockSpec((1, 6, d), lambda t: (grp_l(t), 0, 0)),
                  pl.BlockSpec((TM, d), lambda t: (t, 0)),
                  pl.BlockSpec((d, d), const2)],
        out_specs=pl.BlockSpec((TM, d), lambda t: (t, 0)),
        out_shape=jax.ShapeDtypeStruct((b * seq, d), F32),
        compiler_params=_cparams(("parallel",)),
        name="na_out",
    )(xs, mods, attn, w_out.astype(BF16))


SC_CORES = 2
SC_SUBCORES = 16
SC_WORKERS = SC_CORES * SC_SUBCORES
SLOT_T = 512


def _sc_mesh():
    return plsc.VectorSubcoreMesh(core_axis_name="c", subcore_axis_name="s")


def _sc_chunk(per_worker, max_chunk):
    return max(c for c in range(8, max_chunk + 1, 8) if per_worker % (2 * c) == 0)


def _sc_gather_rows(table, idx):
    dd = table.shape[1]
    bsz = idx.shape[0]
    per_w = bsz // SC_WORKERS
    assert per_w * SC_WORKERS == bsz
    chunk = _sc_chunk(per_w, 64)
    n_chunks = per_w // chunk

    def body(table_hbm, idx_hbm, out_hbm, idx_v, rows0, rows1, g0, g1, w0, w1):
        wid = lax.axis_index("s") * SC_CORES + lax.axis_index("c")
        pltpu.sync_copy(idx_hbm.at[wid], idx_v)
        base = wid * per_w

        def out_rows(j):
            return out_hbm.at[pl.ds(pl.multiple_of(base + j * chunk, 8), chunk)]

        def step(i, carry):
            j0, j1 = 2 * i, 2 * i + 1
            ga = pltpu.async_copy(table_hbm.at[idx_v.at[j0]], rows0, g0)
            gb = pltpu.async_copy(table_hbm.at[idx_v.at[j1]], rows1, g1)
            ga.wait()
            wa = pltpu.async_copy(rows0, out_rows(j0), w0)
            gb.wait()
            wb = pltpu.async_copy(rows1, out_rows(j1), w1)
            wa.wait()
            wb.wait()
            return carry

        lax.fori_loop(0, n_chunks // 2, step, 0)

    return pl.kernel(
        body, out_type=jax.ShapeDtypeStruct((bsz, dd), table.dtype), mesh=_sc_mesh(),
        scratch_types=[pltpu.VMEM((n_chunks, chunk), I32),
                       pltpu.VMEM((chunk, dd), table.dtype), pltpu.VMEM((chunk, dd), table.dtype),
                       pltpu.SemaphoreType.DMA, pltpu.SemaphoreType.DMA,
                       pltpu.SemaphoreType.DMA, pltpu.SemaphoreType.DMA],
        name="sc_gather_rows",
    )(table, idx.reshape(SC_WORKERS, n_chunks, chunk))


def _sc_scatter_rows(rows, dest, n_out):
    n, dd = rows.shape
    kk = dest.shape[0]
    per_w = n // SC_WORKERS
    assert per_w * SC_WORKERS == n
    chunk = _sc_chunk(per_w, 64)
    n_chunks = per_w // chunk
    dest_w = dest.reshape(kk, SC_WORKERS, n_chunks, chunk).transpose(1, 2, 0, 3)
    dest_w = dest_w.reshape(SC_WORKERS, n_chunks * kk, chunk)

    def body(rows_hbm, dest_hbm, out_hbm, idx_v, rows0, rows1, r0, r1, s0, s1):
        wid = lax.axis_index("s") * SC_CORES + lax.axis_index("c")
        pltpu.sync_copy(dest_hbm.at[wid], idx_v)
        base = wid * per_w

        def in_rows(j):
            return rows_hbm.at[pl.ds(pl.multiple_of(base + j * chunk, 8), chunk)]

        def step(i, carry):
            j0, j1 = 2 * i, 2 * i + 1
            ra = pltpu.async_copy(in_rows(j0), rows0, r0)
            rb = pltpu.async_copy(in_rows(j1), rows1, r1)
            ra.wait()
            sa = [pltpu.async_copy(rows0, out_hbm.at[idx_v.at[j0 * kk + k]], s0) for k in range(kk)]
            rb.wait()
            sb = [pltpu.async_copy(rows1, out_hbm.at[idx_v.at[j1 * kk + k]], s1) for k in range(kk)]
            for cp in sa + sb:
                cp.wait()
            return carry

        lax.fori_loop(0, n_chunks // 2, step, 0)

    return pl.kernel(
        body, out_type=jax.ShapeDtypeStruct((n_out, dd), rows.dtype), mesh=_sc_mesh(),
        scratch_types=[pltpu.VMEM((n_chunks * kk, chunk), I32),
                       pltpu.VMEM((chunk, dd), rows.dtype), pltpu.VMEM((chunk, dd), rows.dtype),
                       pltpu.SemaphoreType.DMA, pltpu.SemaphoreType.DMA,
                       pltpu.SemaphoreType.DMA, pltpu.SemaphoreType.DMA],
        name="sc_scatter_rows",
    )(rows, dest_w)


def _slot_kernel(idx_ref, rank_ref, cnt_ref, dest_ref, info_ref, *, bm):
    cnt = cnt_ref[...]
    ne = cnt.shape[0]
    padded = jnp.floor((cnt + (bm - 1.0)) * (1.0 / bm)) * bm
    acc = jnp.broadcast_to(padded, (ne, 128))
    row = lax.broadcasted_iota(I32, (ne, 128), 0)
    s = 1
    while s < ne:
        acc = acc + jnp.where(row >= s, pltpu.roll(acc, s, 0), 0.0)
        s *= 2
    pad_end = acc[:, 0:1]
    pad_start = pad_end - padded
    idx = idx_ref[...]
    e_iota = lax.broadcasted_iota(I32, (ne, idx.shape[1]), 0)
    starts = [jnp.sum(jnp.where(e_iota == idx[k:k + 1], pad_start, 0.0), axis=0, keepdims=True)
              for k in range(idx.shape[0])]
    dest_ref[...] = jnp.concatenate(starts, axis=0).astype(I32) + rank_ref[...]
    lane = lax.broadcasted_iota(I32, info_ref.shape, 1)
    info = jnp.where(lane == 0, pad_start, jnp.where(lane == 1, padded * (1.0 / bm), cnt))
    info_ref[...] = info.astype(I32)


def _route_kernel(x_ref, m_ref, g_ref, wr_ref, br_ref, h_ref, idx_ref, gate_ref, rank_ref, cnt_ref, run_ref):
    t = pl.program_id(0)

    @pl.when(t == 0)
    def _():
        run_ref[...] = jnp.zeros_like(run_ref)

    x = x_ref[...]
    m = m_ref[0]
    h = _modulate(x, g_ref[...], m[3:4], m[4:5]).astype(BF16)
    h_ref[...] = _pack_bf16_pairs(h)
    logits = _dot_nt(wr_ref[...], h) + br_ref[...]
    ne, tm = logits.shape
    e_iota = lax.broadcasted_iota(I32, (ne, tm), 0)
    vals, idxs = [], []
    l = logits
    for _ in range(TOP_K):
        mk = jnp.max(l, axis=0, keepdims=True)
        ik = jnp.min(jnp.where(l == mk, e_iota, ne), axis=0, keepdims=True)
        vals.append(mk)
        idxs.append(ik)
        l = jnp.where(e_iota == ik, -jnp.inf, l)
    top_val = jnp.concatenate(vals, axis=0)
    ex = jnp.exp(top_val - vals[0])
    gate_ref[...] = ex / jnp.sum(ex, axis=0, keepdims=True)
    idx_ref[...] = jnp.concatenate(idxs, axis=0)

    hits = [e_iota == ik for ik in idxs]
    cnt = hits[0].astype(F32)
    for hk in hits[1:]:
        cnt = cnt + hk.astype(F32)
    si = lax.broadcasted_iota(I32, (tm, tm), 0)
    ti = lax.broadcasted_iota(I32, (tm, tm), 1)
    before = (si < ti).astype(BF16)
    total = _dot(cnt.astype(BF16), before) + run_ref[...]
    ranks = [jnp.sum(jnp.where(hk, total, 0.0), axis=0, keepdims=True) for hk in hits]
    rank_ref[...] = jnp.concatenate(ranks, axis=0).astype(I32)
    run_ref[...] = run_ref[...] + jnp.sum(cnt, axis=1, keepdims=True)
    cnt_ref[...] = run_ref[...]


def _pack_bf16_pairs(v):
    bits = lax.bitcast_convert_type(v.astype(BF16).astype(F32), jnp.uint32)
    half = bits.shape[1] // 2
    return (bits[:, half:] & jnp.uint32(0xFFFF0000)) | (bits[:, :half] >> 16)


def _unpack_bf16_pairs(w):
    return (lax.bitcast_convert_type(w << 16, F32),
            lax.bitcast_convert_type(w & jnp.uint32(0xFFFF0000), F32))


def _expert_kernel(be_ref, nv_ref, x_ref, wgu_ref, bgu_ref, wd_ref, bd_ref, o_ref, wgu_bf, wd_bf, act_ref):
    blk = pl.program_id(0)
    ch = wgu_bf.shape[2]
    f = act_ref.shape[1]
    nc = f // ch
    live = nv_ref[blk]
    used = live > 0
    fresh = jnp.logical_or(blk == 0, be_ref[blk] != be_ref[jnp.maximum(blk - 1, 0)])

    @pl.when(jnp.logical_and(used, fresh))
    def _():
        for j in range(wgu_bf.shape[0]):
            wgu_bf[j] = wgu_ref[0, 0, :, j * ch:(j + 1) * ch].astype(BF16)
        for j in range(wd_bf.shape[0]):
            wd_bf[j] = wd_ref[0, 0, :, j * ch:(j + 1) * ch].astype(BF16)

    @pl.when(used)
    def _():
        w = x_ref[...]
        row = lax.broadcasted_iota(I32, w.shape, 0)
        w = jnp.where(row < live, w, jnp.zeros_like(w))
        lo, hi = _unpack_bf16_pairs(w)
        x = jnp.concatenate([lo.astype(BF16), hi.astype(BF16)], axis=1)
        for c in range(nc):
            c0 = slice(c * ch, (c + 1) * ch)
            c1 = slice(f + c * ch, f + (c + 1) * ch)
            glu = _dot(x, wgu_bf[c]) + bgu_ref[0, 0, :, c0]
            lin = _dot(x, wgu_bf[nc + c]) + bgu_ref[0, 0, :, c1]
            glu = jnp.minimum(glu, SWIGLU_LIMIT)
            lin = jnp.clip(lin, -SWIGLU_LIMIT, SWIGLU_LIMIT)
            act_ref[:, c0] = (glu * jax.nn.sigmoid(SWIGLU_ALPHA * glu) * (lin + 1.0)).astype(BF16)
        a = act_ref[...]
        y = jnp.concatenate([_dot(a, wd_bf[n]) for n in range(wd_bf.shape[0])], axis=1)
        o_ref[...] = _pack_bf16_pairs(y + bd_ref[0, 0])

    @pl.when(jnp.logical_not(used))
    def _():
        o_ref[...] = jnp.zeros_like(o_ref)


def _expert_call(start, nblk, cnt, buf, w_gu, b_gu, w_down, b_down, layer, bm):
    n_slots, dw = buf.shape
    _, ne, d, f2 = w_gu.shape
    f = f2 // 2
    n_blocks = n_slots // bm
    b0 = jnp.arange(n_blocks, dtype=I32) * bm
    end = start + nblk * bm
    block_e = jnp.minimum(jnp.sum((end[None, :] <= b0[:, None]).astype(I32), axis=1), ne - 1)
    n_live = jnp.clip(start[block_e] + cnt[block_e] - b0, 0, bm)
    lw = lambda i, be, nv: (layer, be[i], 0, 0)
    cw = 256
    return pl.pallas_call(
        _expert_kernel,
        grid_spec=pltpu.PrefetchScalarGridSpec(
            num_scalar_prefetch=2,
            grid=(n_blocks,),
            in_specs=[pl.BlockSpec((bm, dw), lambda i, be, nv: (i, 0)),
                      pl.BlockSpec((1, 1, d, f2), lw),
                      pl.BlockSpec((1, 1, 1, f2), lw),
                      pl.BlockSpec((1, 1, f, d), lw),
                      pl.BlockSpec((1, 1, 1, d), lw)],
            out_specs=pl.BlockSpec((bm, dw), lambda i, be, nv: (i, 0)),
            scratch_shapes=[pltpu.VMEM((f2 // cw, d, cw), BF16), pltpu.VMEM((d // cw, f, cw), BF16),
                            pltpu.VMEM((bm, f), BF16)]),
        out_shape=jax.ShapeDtypeStruct((n_slots, dw), jnp.uint32),
        compiler_params=_cparams(("arbitrary",)),
        name="moe_experts",
    )(block_e, n_live, buf, w_gu, b_gu.reshape(b_gu.shape[0], ne, 1, f2),
      w_down, b_down.reshape(b_down.shape[0], ne, 1, d))


def _expert_kernel_ring(start_ref, nblk_ref, cnt_ref, x_hbm, wgu_ref, bgu_ref, wd_ref, bd_ref, y_hbm,
                        xbuf, ybuf, wgu_bf, wd_bf, act_ref, sem_in, sem_out, *, n_blocks, fused=False):
    e = pl.program_id(0)
    bm = xbuf.shape[1]
    f = act_ref.shape[1]
    ch = 256
    nb = nblk_ref[e]
    start = start_ref[e]
    cnt = cnt_ref[e]

    def rows(j):
        return pl.ds(pl.multiple_of(start + j * bm, bm), bm)

    def fetch(j, slot):
        return pltpu.make_async_copy(x_hbm.at[rows(j)], xbuf.at[slot], sem_in.at[slot])

    def put(j, slot):
        return pltpu.make_async_copy(ybuf.at[slot], y_hbm.at[rows(j)], sem_out.at[slot])

    @pl.when(nb > 0)
    def _():
        fetch(0, 0).start()
        if wgu_bf.ndim == 3:
            for j in range(wgu_bf.shape[0]):
                wgu_bf[j] = wgu_ref[0, 0, :, j * ch:(j + 1) * ch].astype(BF16)
            for j in range(wd_bf.shape[0]):
                wd_bf[j] = wd_ref[0, 0, :, j * ch:(j + 1) * ch].astype(BF16)
        else:
            for j in range(wgu_bf.shape[1] // ch):
                cs = slice(j * ch, (j + 1) * ch)
                wgu_bf[:, cs] = wgu_ref[0, 0, :, cs].astype(BF16)
            for j in range(wd_bf.shape[1] // ch):
                cs = slice(j * ch, (j + 1) * ch)
                wd_bf[:, cs] = wd_ref[0, 0, :, cs].astype(BF16)

    def block(j, carry):
        slot = lax.rem(j, 2)

        @pl.when(j + 1 < nb)
        def _():
            fetch(j + 1, 1 - slot).start()

        fetch(j, slot).wait()

        @pl.when(j >= 2)
        def _():
            put(j - 2, slot).wait()

        w = xbuf[slot]
        row = lax.broadcasted_iota(I32, w.shape, 0)
        w = jnp.where(row < cnt - j * bm, w, jnp.zeros_like(w))
        lo, hi = _unpack_bf16_pairs(w)
        x = jnp.concatenate([lo.astype(BF16), hi.astype(BF16)], axis=1)
        if fused:
            z = _dot(x, wgu_bf[...]) + bgu_ref[0, 0]
            glu = jnp.minimum(z[:, :f], SWIGLU_LIMIT)
            lin = jnp.clip(z[:, f:], -SWIGLU_LIMIT, SWIGLU_LIMIT)
            act = (glu * jax.nn.sigmoid(SWIGLU_ALPHA * glu) * (lin + 1.0)).astype(BF16)
            ybuf[slot] = _pack_bf16_pairs(_dot(act, wd_bf[...]) + bd_ref[0, 0])
            put(j, slot).start()
            return carry
        if wgu_bf.ndim == 3:
            nc = f // ch
            for c in range(nc):
                c0 = slice(c * ch, (c + 1) * ch)
                c1 = slice(f + c * ch, f + (c + 1) * ch)
                glu = _dot(x, wgu_bf[c]) + bgu_ref[0, 0, :, c0]
                lin = _dot(x, wgu_bf[nc + c]) + bgu_ref[0, 0, :, c1]
                glu = jnp.minimum(glu, SWIGLU_LIMIT)
                lin = jnp.clip(lin, -SWIGLU_LIMIT, SWIGLU_LIMIT)
                act_ref[:, c0] = (glu * jax.nn.sigmoid(SWIGLU_ALPHA * glu) * (lin + 1.0)).astype(BF16)
            a = act_ref[...]
            y = jnp.concatenate([_dot(a, wd_bf[n]) for n in range(wd_bf.shape[0])], axis=1)
            ybuf[slot] = _pack_bf16_pairs(y + bd_ref[0, 0])
            put(j, slot).start()
            return carry
        for c in range(f // ch):
            c0 = slice(c * ch, (c + 1) * ch)
            c1 = slice(f + c * ch, f + (c + 1) * ch)
            glu = _dot(x, wgu_bf[:, c0]) + bgu_ref[0, 0, :, c0]
            lin = _dot(x, wgu_bf[:, c1]) + bgu_ref[0, 0, :, c1]
            glu = jnp.minimum(glu, SWIGLU_LIMIT)
            lin = jnp.clip(lin, -SWIGLU_LIMIT, SWIGLU_LIMIT)
            act_ref[:, c0] = (glu * jax.nn.sigmoid(SWIGLU_ALPHA * glu) * (lin + 1.0)).astype(BF16)
        ybuf[slot] = _pack_bf16_pairs(_dot(act_ref[...], wd_bf[...]) + bd_ref[0, 0])
        put(j, slot).start()
        return carry

    lax.fori_loop(0, nb, block, 0)

    @pl.when(nb >= 2)
    def _():
        put(nb - 2, lax.rem(nb, 2)).wait()

    @pl.when(nb >= 1)
    def _():
        put(nb - 1, lax.rem(nb + 1, 2)).wait()

    @pl.when(e == pl.num_programs(0) - 1)
    def _():
        ybuf[0] = jnp.zeros(ybuf.shape[1:], ybuf.dtype)

        def fill(j, carry):
            cp = pltpu.make_async_copy(ybuf.at[0], y_hbm.at[pl.ds(pl.multiple_of(j * bm, bm), bm)], sem_out.at[0])
            cp.start()
            cp.wait()
            return carry

        lax.fori_loop(start // bm + nb, n_blocks, fill, 0)


def _expert_call_ring(start, nblk, cnt, buf, w_gu, b_gu, w_down, b_down, layer, bm, fused=False, chunked=False):
    n_slots, dw = buf.shape
    _, ne, d, f2 = w_gu.shape
    f = f2 // 2
    lw = lambda e, *_: (layer, e, 0, 0)
    return pl.pallas_call(
        functools.partial(_expert_kernel_ring, n_blocks=n_slots // bm, fused=fused),
        grid_spec=pltpu.PrefetchScalarGridSpec(
            num_scalar_prefetch=3,
            grid=(ne,),
            in_specs=[pl.BlockSpec(memory_space=pl.ANY),
                      pl.BlockSpec((1, 1, d, f2), lw),
                      pl.BlockSpec((1, 1, 1, f2), lw),
                      pl.BlockSpec((1, 1, f, d), lw),
                      pl.BlockSpec((1, 1, 1, d), lw)],
            out_specs=pl.BlockSpec(memory_space=pl.ANY),
            scratch_shapes=[pltpu.VMEM((2, bm, dw), jnp.uint32), pltpu.VMEM((2, bm, dw), jnp.uint32),
                            pltpu.VMEM((f2 // 256, d, 256) if chunked else (d, f2), BF16),
                            pltpu.VMEM((d // 256, f, 256) if chunked else (f, d), BF16), pltpu.VMEM((bm, f), BF16),
                            pltpu.SemaphoreType.DMA((2,)), pltpu.SemaphoreType.DMA((2,))]),
        out_shape=jax.ShapeDtypeStruct((n_slots, dw), jnp.uint32),
        compiler_params=_cparams(("arbitrary",)),
        name="moe_experts",
    )(start, nblk, cnt, buf, w_gu, b_gu.reshape(b_gu.shape[0], ne, 1, f2),
      w_down, b_down.reshape(b_down.shape[0], ne, 1, d))


def _combine_kernel(x_ref, m_ref, y_ref, gate_ref, gout_ref, o_ref, *, final):
    m = m_ref[0]
    gate = gate_ref[...]
    half = y_ref.shape[2]
    f_lo, f_hi = None, None
    for k in range(TOP_K):
        lo, hi = _unpack_bf16_pairs(y_ref[k])
        gk = gate[:, k:k + 1]
        f_lo = gk * lo if f_lo is None else f_lo + gk * lo
        f_hi = gk * hi if f_hi is None else f_hi + gk * hi
    x_lo = x_ref[:, :half] + m[5:6, :half] * f_lo
    x_hi = x_ref[:, half:] + m[5:6, half:] * f_hi
    if final:
        ms = (jnp.sum(x_lo * x_lo, axis=-1, keepdims=True) + jnp.sum(x_hi * x_hi, axis=-1, keepdims=True))
        r = lax.rsqrt(ms * (0.5 / half) + NORM_EPS)
        x_lo = x_lo * r * gout_ref[:, :half]
        x_hi = x_hi * r * gout_ref[:, half:]
    o_ref[:, :half] = x_lo
    o_ref[:, half:] = x_hi


def _moe_layer(xs, mods, g, layer, w_router, b_router, w_gu, b_gu, w_down, b_down, geo, g_out, final):
    n, d = xs.shape
    ne = w_router.shape[1]
    f = w_down.shape[2]
    n_tiles = n // TM
    grp = functools.partial(_tile_group, geo=geo)
    const2 = lambda t: (0, 0)
    h, idx, gate, rank, cnt = pl.pallas_call(
        _route_kernel,
        grid=(n_tiles,),
        in_specs=[pl.BlockSpec((TM, d), lambda t: (t, 0)),
                  pl.BlockSpec((1, 6, d), lambda t: (grp(t), 0, 0)),
                  pl.BlockSpec((1, d), const2),
                  pl.BlockSpec((ne, d), const2),
                  pl.BlockSpec((ne, 1), const2)],
        out_specs=[pl.BlockSpec((TM, d // 2), lambda t: (t, 0)),
                   pl.BlockSpec((TOP_K, TM), lambda t: (0, t)),
                   pl.BlockSpec((TOP_K, TM), lambda t: (0, t)),
                   pl.BlockSpec((TOP_K, TM), lambda t: (0, t)),
                   pl.BlockSpec((ne, 1), const2)],
        out_shape=[jax.ShapeDtypeStruct((n, d // 2), jnp.uint32),
                   jax.ShapeDtypeStruct((TOP_K, n), I32),
                   jax.ShapeDtypeStruct((TOP_K, n), F32),
                   jax.ShapeDtypeStruct((TOP_K, n), I32),
                   jax.ShapeDtypeStruct((ne, 1), F32)],
        scratch_shapes=[pltpu.VMEM((ne, 1), F32)],
        compiler_params=_cparams(("arbitrary",)),
        name="moe_route",
    )(xs, mods, g.reshape(1, d), w_router.T.astype(BF16), b_router.reshape(ne, 1))

    bm = MOE_BM
    n_rows = n * TOP_K
    n_blocks = -(-n_rows // bm) + ne
    n_slots = n_blocks * bm
    dest, info = pl.pallas_call(
        functools.partial(_slot_kernel, bm=bm),
        grid=(n // SLOT_T,),
        in_specs=[pl.BlockSpec((TOP_K, SLOT_T), lambda t: (0, t)),
                  pl.BlockSpec((TOP_K, SLOT_T), lambda t: (0, t)),
                  pl.BlockSpec((ne, 1), const2)],
        out_specs=[pl.BlockSpec((TOP_K, SLOT_T), lambda t: (0, t)),
                   pl.BlockSpec((ne, 128), const2)],
        out_shape=[jax.ShapeDtypeStruct((TOP_K, n), I32),
                   jax.ShapeDtypeStruct((ne, 128), I32)],
        compiler_params=_cparams(("arbitrary",)),
        name="moe_slots",
    )(idx, rank, cnt)

    buf = _sc_scatter_rows(h, dest, n_slots)
    y = _expert_call(info[:, 0], info[:, 1], info[:, 2], buf, w_gu, b_gu, w_down, b_down, layer, bm)
    dy = d // 2
    yk = _sc_gather_rows(y, dest.reshape(-1)).reshape(TOP_K, n, dy)
    return pl.pallas_call(
        functools.partial(_combine_kernel, final=final),
        grid=(n_tiles,),
        in_specs=[pl.BlockSpec((TM, d), lambda t: (t, 0)),
                  pl.BlockSpec((1, 6, d), lambda t: (grp(t), 0, 0)),
                  pl.BlockSpec((TOP_K, TM, dy), lambda t: (0, t, 0)),
                  pl.BlockSpec((TM, TOP_K), lambda t: (t, 0)),
                  pl.BlockSpec((1, d), const2)],
        out_specs=pl.BlockSpec((TM, d), lambda t: (t, 0)),
        out_shape=jax.ShapeDtypeStruct((n, d), F32),
        compiler_params=_cparams(("parallel",)),
        name="moe_combine",
    )(xs, mods, yk, gate.T, g_out.reshape(1, d))


def kernel(x, c, ctx, c_ctx, ada_w, ada_b, norm_mix, norm_ffn, norm_out, a_w_in, a_g_v, a_w_s, a_b_s, a_w_out, b_w_in, b_w_a2, b_b_a, b_g_o, b_w_out, c_w_qkv, c_rpb, c_w_out, moe_w_router, moe_b_router, moe_w_gu, moe_b_gu, moe_w_down, moe_b_down):
    b, seq, d = x.shape
    lctx = ctx.shape[1]
    depth = ada_w.shape[0]
    assert (b * lctx) % TM == 0 and seq % TM == 0 and lctx % GLA_TB == 0 and seq % GLA_TB == 0
    assert seq % (NA_QROWS * GRID_W) == 0 and (b * lctx) % (NA_QROWS * GRID_W) == 0 and lctx % 256 == 0
    assert seq // GRID_W >= NA_WROWS + NA_QROWS and b + 1 <= 8
    geo = (b * lctx // TM, seq // TM, b)
    dims = (b, seq, lctx)

    cond = jnp.zeros((8, d), F32).at[:b].set(c).at[b].set(c_ctx)
    mods_all = _adaln(cond, ada_w, ada_b)[:, :b + 1].reshape(depth, b + 1, 6, d)

    xs = jnp.concatenate([ctx.reshape(b * lctx, d), x.reshape(b * seq, d)], axis=0)
    ctx_tiles = geo[0]
    has_ctx = True
    for i in range(depth):
        kind, j = i % N_MIXERS, i // N_MIXERS
        ctx_later = any(kk % N_MIXERS != 0 for kk in range(i + 1, depth))
        mods = mods_all[i]
        if kind == 0:
            keep_ctx = has_ctx and ctx_later
            skip = 0 if keep_ctx or not has_ctx else ctx_tiles
            geo_i = geo if has_ctx else (0, geo[1], b)
            xs = _gmlp_mixer(xs, mods, norm_mix[i], a_w_in[j], a_g_v[j], a_w_s[j], a_b_s[j], a_w_out[j],
                             skip, xs.shape[0] // TM - skip, geo_i)
            has_ctx = keep_ctx
        elif kind == 1:
            assert has_ctx
            xs = _gla_mixer(xs, mods, norm_mix[i], b_w_in[j], b_w_a2[j], b_b_a[j], b_g_o[j], b_w_out[j], geo, dims)
            if not ctx_later:
                xs = xs[b * lctx:]
                has_ctx = False
        else:
            assert has_ctx
            if ctx_later:
                raise NotImplementedError("context output of the neighbourhood mixer")
            xs = _na_mixer(xs, mods, norm_mix[i], c_w_qkv[j], c_rpb[j], c_w_out[j], geo, dims)
            has_ctx = False
        geo_i = geo if has_ctx else (0, geo[1], b)
        xs = _moe_layer(xs, mods, norm_ffn[i], i, moe_w_router[i], moe_b_router[i], moe_w_gu, moe_b_gu,
                        moe_w_down, moe_b_down, geo_i, norm_out, i == depth - 1)
    if has_ctx:
        xs = xs[b * lctx:]
    return xs.reshape(b, seq, d)
```

```python
import functools

import numpy as np
import jax
import jax.numpy as jnp
from jax import lax
from jax.experimental import pallas as pl
from jax.experimental.pallas import tpu as pltpu
from jax.experimental.pallas import tpu_sc as plsc

F32 = jnp.float32
BF16 = jnp.bfloat16
I32 = jnp.int32

NORM_EPS = 1e-6
GRID_W = 64
N_MIXERS = 3

CHUNK_A = 128
A_GROUPS = 8
GLA_HEADS = 4
GLA_RANK = 16
GLA_TAU = 16.0
GLA_CHUNK = 64
ROPE_BASE = 10000.0
NA_HEADS = 16
NA_KH = 8
NA_KW = 16
NEG_INF = -1e30
NA_QROWS = 4
NA_WROWS = 12
NA_PAIRS_PER_STEP = 2
TOP_K = 4
SWIGLU_LIMIT = 7.0
SWIGLU_ALPHA = 1.702
MOE_BM = 512

TM = 512
GLA_TB = 256
VMEM_LIMIT = 56 * 1024 * 1024


def _cparams(sem):
    return pltpu.CompilerParams(dimension_semantics=sem, vmem_limit_bytes=VMEM_LIMIT)


def _dot(a, b):
    return jnp.dot(a, b, preferred_element_type=F32)


def _dot_nt(a, b):
    return lax.dot_general(a, b, (((1,), (1,)), ((), ())), preferred_element_type=F32)


def _dot_tn(a, b):
    return lax.dot_general(a, b, (((0,), (0,)), ((), ())), preferred_element_type=F32)


def _rms(x, g):
    return x * lax.rsqrt(jnp.mean(x * x, axis=-1, keepdims=True) + NORM_EPS) * g


def _modulate(x, g, shift, scale):
    return _rms(x, g) * (1.0 + scale) + shift


def _ada_kernel(s_ref, w_ref, b_ref, o_ref):
    s = s_ref[...]
    s = s * jax.nn.sigmoid(s)
    o_ref[0] = jnp.dot(s, w_ref[0], preferred_element_type=F32,
                       precision=lax.Precision.HIGHEST) + b_ref[0]


def _adaln(cond, ada_w, ada_b):
    depth, d, n6 = ada_w.shape
    bn = n6 // 4
    return pl.pallas_call(
        _ada_kernel,
        grid=(depth, n6 // bn),
        in_specs=[pl.BlockSpec((8, d), lambda i, j: (0, 0)),
                  pl.BlockSpec((1, d, bn), lambda i, j: (i, 0, j)),
                  pl.BlockSpec((1, 1, bn), lambda i, j: (i, 0, j))],
        out_specs=pl.BlockSpec((1, 8, bn), lambda i, j: (i, 0, j)),
        out_shape=jax.ShapeDtypeStruct((depth, 8, n6), F32),
        compiler_params=_cparams(("parallel", "parallel")),
        name="adaln",
    )(cond, ada_w, ada_b.reshape(depth, 1, n6))


def _gelu(z):
    return 0.5 * z * (1.0 + lax.erf(z * np.float32(np.sqrt(0.5))))


def _gmlp_kernel(x_ref, m_ref, g_ref, win_ref, gv_ref, ws_ref, bs_ref, wout_ref, o_ref):
    x = x_ref[...]
    m = m_ref[0]
    a = gv_ref.shape[1]
    h = _modulate(x, g_ref[...], m[0:1], m[1:2]).astype(BF16)
    z = _gelu(_dot(h, win_ref[...]))
    u = z[:, :a]
    v = _rms(z[:, a:], gv_ref[...]).astype(BF16)
    gw = a // A_GROUPS
    rows = []
    for c in range(x.shape[0] // CHUNK_A):
        cols = [_dot(ws_ref[g], v[c * CHUNK_A:(c + 1) * CHUNK_A, g * gw:(g + 1) * gw])
                for g in range(A_GROUPS)]
        rows.append(jnp.concatenate(cols, axis=1) + bs_ref[...])
    s = jnp.concatenate(rows, axis=0)
    y = _dot((u * s).astype(BF16), wout_ref[...])
    o_ref[...] = x + m[2:3] * y


def _gmlp_mixer(xs, mods, g, w_in, g_v, w_s, b_s, w_out, tile0, n_tiles, geo):
    d = xs.shape[1]
    a = g_v.shape[0]
    gw = a // A_GROUPS
    bias = jnp.repeat(b_s.T, gw, axis=1)
    grp = functools.partial(_tile_group, geo=geo, tile0=tile0)
    const2 = lambda t: (0, 0)
    return pl.pallas_call(
        _gmlp_kernel,
        grid=(n_tiles,),
        in_specs=[pl.BlockSpec((TM, d), lambda t: (t + tile0, 0)),
                  pl.BlockSpec((1, 6, d), lambda t: (grp(t), 0, 0)),
                  pl.BlockSpec((1, d), const2),
                  pl.BlockSpec((d, 2 * a), const2),
                  pl.BlockSpec((1, a), const2),
                  pl.BlockSpec((A_GROUPS, CHUNK_A, CHUNK_A), lambda t: (0, 0, 0)),
                  pl.BlockSpec((CHUNK_A, a), const2),
                  pl.BlockSpec((a, d), const2)],
        out_specs=pl.BlockSpec((TM, d), lambda t: (t, 0)),
        out_shape=jax.ShapeDtypeStruct((n_tiles * TM, d), F32),
        compiler_params=_cparams(("parallel",)),
        name="gmlp_mixer",
    )(xs, mods, g.reshape(1, d), w_in.astype(BF16), g_v.reshape(1, a), w_s.astype(BF16),
      bias, w_out.astype(BF16))


def _tile_group(t, geo, tile0=0):
    n_ctx_tiles, tiles_per_batch, nb = geo
    tt = t + tile0
    return jnp.where(tt < n_ctx_tiles, nb, (tt - n_ctx_tiles) // tiles_per_batch)


def _dot_f32(tri_bf16, x):
    hi = x.astype(BF16)
    r1 = x - hi.astype(F32)
    mid = r1.astype(BF16)
    lo = (r1 - mid.astype(F32)).astype(BF16)
    return _dot(tri_bf16, hi) + _dot(tri_bf16, mid) + _dot(tri_bf16, lo)


def _gla_proj_kernel(x_ref, m_ref, g_ref, wq_ref, wk_ref, wv_ref, wg_ref, wr_ref, wa_ref, ba_ref,
                     cos_ref, sin_ref, q_ref, k_ref, v_ref, go_ref, la_ref):
    x = x_ref[...]
    m = m_ref[0]
    h = _modulate(x, g_ref[...], m[0:1], m[1:2]).astype(BF16)
    kdim = wq_ref.shape[1]
    hk = kdim // GLA_HEADS
    q = _dot(h, wq_ref[...]) * np.float32(hk ** -0.5)
    k = _dot(h, wk_ref[...])
    v_ref[...] = _dot(h, wv_ref[...]).astype(BF16)
    go_ref[...] = _dot(h, wg_ref[...])
    r = _dot(h, wr_ref[...])
    z = _dot(r.astype(BF16), wa_ref[...]) + ba_ref[...]
    la_ref[...] = jax.nn.log_sigmoid(z) * np.float32(1.0 / GLA_TAU)

    cos = jnp.concatenate([cos_ref[...]] * GLA_HEADS, axis=1)
    sin = jnp.concatenate([sin_ref[...]] * GLA_HEADS, axis=1)
    nf = hk // 4
    lane = lax.broadcasted_iota(I32, q.shape, 1)
    first = (lane % (2 * nf)) < nf

    def rope(t):
        up = pltpu.roll(t, kdim - nf, 1)
        dn = pltpu.roll(t, nf, 1)
        return t * cos + jnp.where(first, up, dn) * sin

    q_ref[...] = rope(q)
    k_ref[...] = rope(k)


def _gla_chunk(q_ref, k_ref, v_ref, la_ref, o_ref, st_ref, n, rev):
    kdim = q_ref.shape[1]
    hk = kdim // GLA_HEADS
    hv = v_ref.shape[1] // GLA_HEADS
    c = GLA_CHUNK
    ri = lax.broadcasted_iota(I32, (c, c), 0)
    ci = lax.broadcasted_iota(I32, (c, c), 1)
    keep = (ci >= ri) if rev else (ci <= ri)
    tri = keep.astype(BF16)
    rows = slice(n * c, (n + 1) * c)
    cum = _dot_f32(tri, la_ref[rows, :])
    last = cum[0:1] if rev else cum[c - 1:c]
    q = q_ref[rows, :]
    k = k_ref[rows, :]
    q_dec = (q * jnp.exp(cum)).astype(BF16)
    k_inv = (k * jnp.exp(-cum)).astype(BF16)
    k_end = (k * jnp.exp(last - cum)).astype(BF16)
    dec = jnp.exp(last)
    for h in range(GLA_HEADS):
        ks = slice(h * hk, (h + 1) * hk)
        vs = slice(h * hv, (h + 1) * hv)
        att = jnp.where(keep, _dot_nt(q_dec[:, ks], k_inv[:, ks]), 0.0).astype(BF16)
        vh = v_ref[rows, vs]
        st = st_ref[h]
        o_ref[rows, vs] = _dot(att, vh) + _dot_nt(q_dec[:, ks], st.astype(BF16))
        st_ref[h] = st * dec[:, ks] + _dot_tn(vh, k_end[:, ks])


def _gla_scan_kernel(qf_ref, kf_ref, vf_ref, laf_ref, qb_ref, kb_ref, vb_ref, lab_ref,
                     of_ref, ob_ref, stf_ref, stb_ref):
    s = pl.program_id(1)

    @pl.when(s == 0)
    def _():
        stf_ref[...] = jnp.zeros_like(stf_ref)
        stb_ref[...] = jnp.zeros_like(stb_ref)

    nch = qf_ref.shape[0] // GLA_CHUNK
    for n in range(nch):
        _gla_chunk(qf_ref, kf_ref, vf_ref, laf_ref, of_ref, stf_ref, n, False)
        _gla_chunk(qb_ref, kb_ref, vb_ref, lab_ref, ob_ref, stb_ref, nch - 1 - n, True)


def _gla_out_kernel(x_ref, m_ref, of_ref, ob_ref, go_ref, gn_ref, wout_ref, o_ref):
    x = x_ref[...]
    m = m_ref[0]
    o = of_ref[...] + ob_ref[...]
    hv = gn_ref.shape[1]
    parts = [_rms(o[:, h * hv:(h + 1) * hv], gn_ref[...]) for h in range(GLA_HEADS)]
    o = jnp.concatenate(parts, axis=1)
    gate = go_ref[...]
    y = _dot((o * (gate * jax.nn.sigmoid(gate))).astype(BF16), wout_ref[...])
    o_ref[...] = x + m[2:3] * y


def _gla_mixer(xs, mods, g, w_in, w_a2, b_a, g_o, w_out, geo, dims):
    b, seq, lctx = dims
    nt, d = xs.shape
    n_tiles = nt // TM
    kdim = w_a2.shape[2]
    vdim = g_o.shape[0] * GLA_HEADS
    hk = kdim // GLA_HEADS
    nf = hk // 4
    wq = w_in[:, :kdim].astype(BF16)
    wk = w_in[:, kdim:2 * kdim].astype(BF16)
    wv = w_in[:, 2 * kdim:2 * kdim + vdim].astype(BF16)
    wg = w_in[:, 2 * kdim + vdim:2 * kdim + 2 * vdim].astype(BF16)
    wr = jnp.pad(w_in[:, 2 * kdim + 2 * vdim:], ((0, 0), (0, 128 - 2 * GLA_RANK))).astype(BF16)
    wa = jnp.zeros((128, 2 * kdim), F32)
    wa = wa.at[:GLA_RANK, :kdim].set(w_a2[0]).at[GLA_RANK:2 * GLA_RANK, kdim:].set(w_a2[1]).astype(BF16)
    ba = b_a.reshape(1, 2 * kdim)

    tpos = jnp.arange(seq)
    freqs = jnp.power(ROPE_BASE, -jnp.arange(nf, dtype=F32) / nf)
    ar = (tpos // GRID_W).astype(F32)[:, None] * freqs
    ac = (tpos % GRID_W).astype(F32)[:, None] * freqs
    cos = jnp.concatenate([jnp.cos(ar), jnp.cos(ar), jnp.cos(ac), jnp.cos(ac)], axis=1)
    sin = jnp.concatenate([-jnp.sin(ar), jnp.sin(ar), -jnp.sin(ac), jnp.sin(ac)], axis=1)
    cos = jnp.concatenate([jnp.ones((TM, hk), F32), cos], axis=0)
    sin = jnp.concatenate([jnp.zeros((TM, hk), F32), sin], axis=0)

    n_ctx_tiles, tiles_per_batch, _ = geo
    grp = functools.partial(_tile_group, geo=geo)

    def rope_blk(t):
        return (jnp.where(t < n_ctx_tiles, 0, 1 + (t - n_ctx_tiles) % tiles_per_batch), 0)

    const2 = lambda t: (0, 0)
    row = lambda t: (t, 0)
    q, k, v, go, la = pl.pallas_call(
        _gla_proj_kernel,
        grid=(n_tiles,),
        in_specs=[pl.BlockSpec((TM, d), row),
                  pl.BlockSpec((1, 6, d), lambda t: (grp(t), 0, 0)),
                  pl.BlockSpec((1, d), const2),
                  pl.BlockSpec((d, kdim), const2), pl.BlockSpec((d, kdim), const2),
                  pl.BlockSpec((d, vdim), const2), pl.BlockSpec((d, vdim), const2),
                  pl.BlockSpec((d, 128), const2), pl.BlockSpec((128, 2 * kdim), const2),
                  pl.BlockSpec((1, 2 * kdim), const2),
                  pl.BlockSpec((TM, hk), rope_blk), pl.BlockSpec((TM, hk), rope_blk)],
        out_specs=[pl.BlockSpec((TM, kdim), row), pl.BlockSpec((TM, kdim), row),
                   pl.BlockSpec((TM, vdim), row), pl.BlockSpec((TM, vdim), row),
                   pl.BlockSpec((TM, 2 * kdim), row)],
        out_shape=[jax.ShapeDtypeStruct((nt, kdim), F32), jax.ShapeDtypeStruct((nt, kdim), F32),
                   jax.ShapeDtypeStruct((nt, vdim), BF16), jax.ShapeDtypeStruct((nt, vdim), F32),
                   jax.ShapeDtypeStruct((nt, 2 * kdim), F32)],
        compiler_params=_cparams(("parallel",)),
        name="gla_proj",
    )(xs, mods, g.reshape(1, d), wq, wk, wv, wg, wr, wa, ba, cos, sin)

    tb = GLA_TB
    ctx_steps = lctx // tb
    lat_steps = seq // tb
    steps = ctx_steps + lat_steps

    def blk(bi, s, rev):
        if rev:
            cs, ls = ctx_steps - 1 - s, lat_steps - 1 - (s - ctx_steps)
        else:
            cs, ls = s, s - ctx_steps
        return jnp.where(s < ctx_steps, bi * ctx_steps + cs, b * ctx_steps + bi * lat_steps + ls)

    def dir_specs(rev):
        row_blk = lambda bi, s: (blk(bi, s, rev), 0)
        return [pl.BlockSpec((tb, kdim), row_blk), pl.BlockSpec((tb, kdim), row_blk),
                pl.BlockSpec((tb, vdim), row_blk),
                pl.BlockSpec((tb, kdim), lambda bi, s: (blk(bi, s, rev), 1 if rev else 0))]

    state = pltpu.VMEM((GLA_HEADS, vdim // GLA_HEADS, hk), F32)
    o_f, o_b = pl.pallas_call(
        _gla_scan_kernel,
        grid=(b, steps),
        in_specs=dir_specs(False) + dir_specs(True),
        out_specs=[pl.BlockSpec((tb, vdim), lambda bi, s: (blk(bi, s, False), 0)),
                   pl.BlockSpec((tb, vdim), lambda bi, s: (blk(bi, s, True), 0))],
        out_shape=[jax.ShapeDtypeStruct((nt, vdim), F32)] * 2,
        scratch_shapes=[state, state],
        compiler_params=_cparams(("parallel", "arbitrary")),
        name="gla_scan",
    )(q, k, v, la, q, k, v, la)

    return pl.pallas_call(
        _gla_out_kernel,
        grid=(n_tiles,),
        in_specs=[pl.BlockSpec((TM, d), row),
                  pl.BlockSpec((1, 6, d), lambda t: (grp(t), 0, 0)),
                  pl.BlockSpec((TM, vdim), row), pl.BlockSpec((TM, vdim), row),
                  pl.BlockSpec((TM, vdim), row),
                  pl.BlockSpec((1, vdim // GLA_HEADS), const2),
                  pl.BlockSpec((vdim, d), const2)],
        out_specs=pl.BlockSpec((TM, d), row),
        out_shape=jax.ShapeDtypeStruct((nt, d), F32),
        compiler_params=_cparams(("parallel",)),
        name="gla_out",
    )(xs, mods, o_f, o_b, go, g_o.reshape(1, -1), w_out.astype(BF16))


def _na_proj_kernel(x_ref, m_ref, g_ref, w_ref, q_ref, k_ref, v_ref):
    x = x_ref[...]
    m = m_ref[0]
    d = x.shape[1]
    h = _modulate(x, g_ref[...], m[0:1], m[1:2]).astype(BF16)
    qkv = _dot(h, w_ref[...])
    hd = d // NA_HEADS
    for p in range(q_ref.shape[0]):
        cs = slice(p * 128, (p + 1) * 128)
        q_ref[p] = (qkv[:, cs] * np.float32(hd ** -0.5)).astype(BF16)
        k_ref[p] = qkv[:, d + p * 128:d + (p + 1) * 128].astype(BF16)
        v_ref[p] = qkv[:, 2 * d + p * 128:2 * d + (p + 1) * 128].astype(BF16)


def _na_attn_kernel(tbl_ref, q_ref, *refs, n_tiles, nwin):
    k_refs, v_refs = refs[:nwin], refs[nwin:2 * nwin]
    kc_ref, vc_ref, bias_ref, o_ref, s_ref = refs[2 * nwin:]
    t = pl.program_id(2)
    typ = jnp.where(t == 0, 0, jnp.where(t == n_tiles - 1, 2, 1))
    npair = NA_WROWS // 2
    lane = lax.broadcasted_iota(I32, q_ref.shape[1:], 1)
    outs = []
    for pp in range(q_ref.shape[0]):
        q = q_ref[pp]
        kw = jnp.concatenate([r[pp] for r in k_refs], axis=0)
        vw = jnp.concatenate([r[pp] for r in v_refs], axis=0)
        kc = kc_ref[pp]
        vc = vc_ref[pp]
        heads = []
        for hh in range(2):
            sel = (lane < 64) if hh == 0 else (lane >= 64)
            qm = jnp.where(sel, q, jnp.zeros_like(q))
            s = _dot_nt(qm, kw)
            s_ctx = _dot_nt(qm, kc)
            sl = s_ref.at[2 * pp + hh]
            for dr in range(NA_QROWS):
                for j in range(npair):
                    e = tbl_ref[typ * (NA_QROWS * npair) + dr * npair + j]
                    rs = slice(dr * GRID_W, (dr + 1) * GRID_W)
                    cs = slice(j * 128, (j + 1) * 128)
                    sl[rs, cs] = s[rs, cs] + bias_ref[2 * pp + hh, e]
            s_lat = sl[...]
            mx = jnp.maximum(jnp.max(s_lat, axis=1, keepdims=True), jnp.max(s_ctx, axis=1, keepdims=True))
            e_lat = jnp.exp(s_lat - mx)
            e_ctx = jnp.exp(s_ctx - mx)
            inv = 1.0 / (jnp.sum(e_lat, axis=1, keepdims=True) + jnp.sum(e_ctx, axis=1, keepdims=True))
            heads.append(_dot((e_lat * inv).astype(BF16), vw) + _dot((e_ctx * inv).astype(BF16), vc))
        outs.append(jnp.where(lane < 64, heads[0], heads[1]))
    o_ref[...] = jnp.concatenate(outs, axis=1).astype(BF16)


def _na_out_kernel(x_ref, m_ref, a_ref, w_ref, o_ref):
    m = m_ref[0]
    o_ref[...] = x_ref[...] + m[2:3] * _dot(a_ref[...], w_ref[...])


def _na_tables(rows):
    npair = NA_WROWS // 2
    tbl = np.zeros((3, NA_QROWS, npair), np.int32)
    for typ, r_base in enumerate((0, NA_QROWS, rows - NA_QROWS)):
        w0 = int(np.clip(r_base - NA_KH // 2, 0, rows - NA_WROWS))
        for dr in range(NA_QROWS):
            r = r_base + dr
            r0 = int(np.clip(r - NA_KH // 2, 0, rows - NA_KH))
            for j in range(npair):
                kr = (w0 + 2 * j, w0 + 2 * j + 1)
                ok = [r0 <= x < r0 + NA_KH for x in kr]
                ri = [x - r + NA_KH - 1 for x in kr]
                if ok[0] and ok[1]:
                    e = ri[0]
                elif ok[0]:
                    e = 16 + ri[0]
                elif ok[1]:
                    e = 32 + ri[1]
                else:
                    e = 63
                tbl[typ, dr, j] = e
    return tbl.reshape(-1)


def _na_bias_table(rpb):
    nh = rpb.shape[0]
    qc = np.arange(GRID_W)
    cstart = np.clip(qc - NA_KW // 2, 0, GRID_W - NA_KW)
    kc = np.arange(GRID_W)
    ok = (kc[None, :] >= cstart[:, None]) & (kc[None, :] < cstart[:, None] + NA_KW)
    nr, nw = rpb.shape[1], rpb.shape[2]
    w2 = 2 * GRID_W
    lead = GRID_W - NA_KW
    line = jnp.pad(rpb.astype(F32), ((0, 0), (0, 0), (lead, w2 - lead - nw)))
    rel = jnp.tile(line, (1, 1, GRID_W))[:, :, :GRID_W * (w2 - 1)].reshape(nh, nr, GRID_W, w2 - 1)
    rel = rel[..., GRID_W - 1:]
    cb = jnp.where(ok[None, None], rel, NEG_INF)
    neg = jnp.full((nh, 1, GRID_W, GRID_W), NEG_INF, F32)
    cb = jnp.concatenate([cb, neg], axis=1)
    negs = jnp.broadcast_to(neg, cb.shape)
    nxt = jnp.concatenate([cb[:, 1:], neg], axis=1)
    both = jnp.concatenate([cb, nxt], axis=-1)
    left = jnp.concatenate([cb, negs], axis=-1)
    right = jnp.concatenate([negs, cb], axis=-1)
    none = jnp.concatenate([negs, negs], axis=-1)
    return jnp.concatenate([both, left, right, none], axis=1)


def _na_mixer(xs, mods, g, w_qkv, rpb, w_out, geo, dims):
    b, seq, lctx = dims
    nt, d = xs.shape
    n_tiles_tok = nt // TM
    grp = functools.partial(_tile_group, geo=geo)
    npairs = d // 128
    const2 = lambda t: (0, 0)
    q, k, v = pl.pallas_call(
        _na_proj_kernel,
        grid=(n_tiles_tok,),
        in_specs=[pl.BlockSpec((TM, d), lambda t: (t, 0)),
                  pl.BlockSpec((1, 6, d), lambda t: (grp(t), 0, 0)),
                  pl.BlockSpec((1, d), const2),
                  pl.BlockSpec((d, 3 * d), const2)],
        out_specs=[pl.BlockSpec((npairs, TM, 128), lambda t: (0, t, 0))] * 3,
        out_shape=[jax.ShapeDtypeStruct((npairs, nt, 128), BF16)] * 3,
        compiler_params=_cparams(("parallel",)),
        name="na_proj",
    )(xs, mods, g.reshape(1, d), w_qkv.astype(BF16))

    rows = seq // GRID_W
    tq = NA_QROWS * GRID_W
    n_tiles = rows // NA_QROWS
    wb = 256
    nwin = NA_WROWS * GRID_W // wb
    lat0 = b * lctx
    tbl = jnp.asarray(_na_tables(rows))
    bias = _na_bias_table(rpb)

    pp = NA_PAIRS_PER_STEP
    assert npairs % pp == 0 and (NA_QROWS * GRID_W) % wb == 0 and (NA_KH // 2 * GRID_W) % wb == 0

    def win(i):
        def f(p, bi, t, tbl_ref):
            w = jnp.clip(t * (NA_QROWS * GRID_W // wb) - NA_KH // 2 * GRID_W // wb, 0, seq // wb - nwin)
            return (p, (lat0 + bi * seq) // wb + w + i, 0)
        return f

    kv_specs = [pl.BlockSpec((pp, wb, 128), win(i)) for i in range(nwin)]
    attn = pl.pallas_call(
        functools.partial(_na_attn_kernel, n_tiles=n_tiles, nwin=nwin),
        grid_spec=pltpu.PrefetchScalarGridSpec(
            num_scalar_prefetch=1,
            grid=(npairs // pp, b, n_tiles),
            in_specs=[pl.BlockSpec((pp, tq, 128), lambda p, bi, t, tr: (p, (lat0 + bi * seq) // tq + t, 0))]
                     + kv_specs + kv_specs
                     + [pl.BlockSpec((pp, lctx, 128), lambda p, bi, t, tr: (p, bi, 0)),
                        pl.BlockSpec((pp, lctx, 128), lambda p, bi, t, tr: (p, bi, 0)),
                        pl.BlockSpec((2 * pp, 64, GRID_W, 2 * GRID_W), lambda p, bi, t, tr: (p, 0, 0, 0))],
            out_specs=pl.BlockSpec((tq, 128 * pp), lambda p, bi, t, tr: (bi * n_tiles + t, p)),
            scratch_shapes=[pltpu.VMEM((2 * pp, tq, NA_WROWS * GRID_W), F32)]),
        out_shape=jax.ShapeDtypeStruct((b * seq, d), BF16),
        compiler_params=_cparams(("parallel", "parallel", "arbitrary")),
        name="na_attn",
    )(tbl, q, *([k] * nwin), *([v] * nwin), k, v, bias)

    n_lat_tiles = b * seq // TM
    tile0 = lat0 // TM
    grp_l = functools.partial(_tile_group, geo=geo, tile0=tile0)
    return pl.pallas_call(
        _na_out_kernel,
        grid=(n_lat_tiles,),
        in_specs=[pl.BlockSpec((TM, d), lambda t: (t + tile0, 0)),
                  pl.BlockSpec((1, 6, d), lambda t: (grp_l(t), 0, 0)),
                  pl.BlockSpec((TM, d), lambda t: (t, 0)),
                  pl.BlockSpec((d, d), const2)],
        out_specs=pl.BlockSpec((TM, d), lambda t: (t, 0)),
        out_shape=jax.ShapeDtypeStruct((b * seq, d), F32),
        compiler_params=_cparams(("parallel",)),
        name="na_out",
    )(xs, mods, attn, w_out.astype(BF16))


SC_CORES = 2
SC_SUBCORES = 16
SC_WORKERS = SC_CORES * SC_SUBCORES
SLOT_T = 512


def _sc_mesh():
    return plsc.VectorSubcoreMesh(core_axis_name="c", subcore_axis_name="s")


def _sc_chunk(per_worker, max_chunk):
    return max(c for c in range(8, max_chunk + 1, 8) if per_worker % (2 * c) == 0)


def _sc_gather_rows(table, idx):
    dd = table.shape[1]
    bsz = idx.shape[0]
    per_w = bsz // SC_WORKERS
    assert per_w * SC_WORKERS == bsz
    chunk = _sc_chunk(per_w, 64)
    n_chunks = per_w // chunk

    def body(table_hbm, idx_hbm, out_hbm, idx_v, rows0, rows1, g0, g1, w0, w1):
        wid = lax.axis_index("s") * SC_CORES + lax.axis_index("c")
        pltpu.sync_copy(idx_hbm.at[wid], idx_v)
        base = wid * per_w

        def out_rows(j):
            return out_hbm.at[pl.ds(pl.multiple_of(base + j * chunk, 8), chunk)]

        def step(i, carry):
            j0, j1 = 2 * i, 2 * i + 1
            ga = pltpu.async_copy(table_hbm.at[idx_v.at[j0]], rows0, g0)
            gb = pltpu.async_copy(table_hbm.at[idx_v.at[j1]], rows1, g1)
            ga.wait()
            wa = pltpu.async_copy(rows0, out_rows(j0), w0)
            gb.wait()
            wb = pltpu.async_copy(rows1, out_rows(j1), w1)
            wa.wait()
            wb.wait()
            return carry

        lax.fori_loop(0, n_chunks // 2, step, 0)

    return pl.kernel(
        body, out_type=jax.ShapeDtypeStruct((bsz, dd), table.dtype), mesh=_sc_mesh(),
        scratch_types=[pltpu.VMEM((n_chunks, chunk), I32),
                       pltpu.VMEM((chunk, dd), table.dtype), pltpu.VMEM((chunk, dd), table.dtype),
                       pltpu.SemaphoreType.DMA, pltpu.SemaphoreType.DMA,
                       pltpu.SemaphoreType.DMA, pltpu.SemaphoreType.DMA],
        name="sc_gather_rows",
    )(table, idx.reshape(SC_WORKERS, n_chunks, chunk))


def _sc_scatter_rows(rows, dest, n_out):
    n, dd = rows.shape
    kk = dest.shape[0]
    per_w = n // SC_WORKERS
    assert per_w * SC_WORKERS == n
    chunk = _sc_chunk(per_w, 64)
    n_chunks = per_w // chunk
    dest_w = dest.reshape(kk, SC_WORKERS, n_chunks, chunk).transpose(1, 2, 0, 3)
    dest_w = dest_w.reshape(SC_WORKERS, n_chunks * kk, chunk)

    def body(rows_hbm, dest_hbm, out_hbm, idx_v, rows0, rows1, r0, r1, s0, s1):
        wid = lax.axis_index("s") * SC_CORES + lax.axis_index("c")
        pltpu.sync_copy(dest_hbm.at[wid], idx_v)
        base = wid * per_w

        def in_rows(j):
            return rows_hbm.at[pl.ds(pl.multiple_of(base + j * chunk, 8), chunk)]

        def step(i, carry):
            j0, j1 = 2 * i, 2 * i + 1
            ra = pltpu.async_copy(in_rows(j0), rows0, r0)
            rb = pltpu.async_copy(in_rows(j1), rows1, r1)
            ra.wait()
            sa = [pltpu.async_copy(rows0, out_hbm.at[idx_v.at[j0 * kk + k]], s0) for k in range(kk)]
            rb.wait()
            sb = [pltpu.async_copy(rows1, out_hbm.at[idx_v.at[j1 * kk + k]], s1) for k in range(kk)]
            for cp in sa + sb:
                cp.wait()
            return carry

        lax.fori_loop(0, n_chunks // 2, step, 0)

    return pl.kernel(
        body, out_type=jax.ShapeDtypeStruct((n_out, dd), rows.dtype), mesh=_sc_mesh(),
        scratch_types=[pltpu.VMEM((n_chunks * kk, chunk), I32),
                       pltpu.VMEM((chunk, dd), rows.dtype), pltpu.VMEM((chunk, dd), rows.dtype),
                       pltpu.SemaphoreType.DMA, pltpu.SemaphoreType.DMA,
                       pltpu.SemaphoreType.DMA, pltpu.SemaphoreType.DMA],
        name="sc_scatter_rows",
    )(rows, dest_w)


def _slot_kernel(idx_ref, rank_ref, cnt_ref, dest_ref, info_ref, *, bm):
    cnt = cnt_ref[...]
    ne = cnt.shape[0]
    padded = jnp.floor((cnt + (bm - 1.0)) * (1.0 / bm)) * bm
    acc = jnp.broadcast_to(padded, (ne, 128))
    row = lax.broadcasted_iota(I32, (ne, 128), 0)
    s = 1
    while s < ne:
        acc = acc + jnp.where(row >= s, pltpu.roll(acc, s, 0), 0.0)
        s *= 2
    pad_end = acc[:, 0:1]
    pad_start = pad_end - padded
    idx = idx_ref[...]
    e_iota = lax.broadcasted_iota(I32, (ne, idx.shape[1]), 0)
    starts = [jnp.sum(jnp.where(e_iota == idx[k:k + 1], pad_start, 0.0), axis=0, keepdims=True)
              for k in range(idx.shape[0])]
    dest_ref[...] = jnp.concatenate(starts, axis=0).astype(I32) + rank_ref[...]
    lane = lax.broadcasted_iota(I32, info_ref.shape, 1)
    info = jnp.where(lane == 0, pad_start, jnp.where(lane == 1, padded * (1.0 / bm), cnt))
    info_ref[...] = info.astype(I32)


def _route_kernel(x_ref, m_ref, g_ref, wr_ref, br_ref, h_ref, idx_ref, gate_ref, rank_ref, cnt_ref, run_ref):
    t = pl.program_id(0)

    @pl.when(t == 0)
    def _():
        run_ref[...] = jnp.zeros_like(run_ref)

    x = x_ref[...]
    m = m_ref[0]
    h = _modulate(x, g_ref[...], m[3:4], m[4:5]).astype(BF16)
    h_ref[...] = _pack_bf16_pairs(h)
    logits = _dot_nt(wr_ref[...], h) + br_ref[...]
    ne, tm = logits.shape
    e_iota = lax.broadcasted_iota(I32, (ne, tm), 0)
    vals, idxs = [], []
    l = logits
    for _ in range(TOP_K):
        mk = jnp.max(l, axis=0, keepdims=True)
        ik = jnp.min(jnp.where(l == mk, e_iota, ne), axis=0, keepdims=True)
        vals.append(mk)
        idxs.append(ik)
        l = jnp.where(e_iota == ik, -jnp.inf, l)
    top_val = jnp.concatenate(vals, axis=0)
    ex = jnp.exp(top_val - vals[0])
    gate_ref[...] = ex / jnp.sum(ex, axis=0, keepdims=True)
    idx_ref[...] = jnp.concatenate(idxs, axis=0)

    hits = [e_iota == ik for ik in idxs]
    cnt = hits[0].astype(F32)
    for hk in hits[1:]:
        cnt = cnt + hk.astype(F32)
    si = lax.broadcasted_iota(I32, (tm, tm), 0)
    ti = lax.broadcasted_iota(I32, (tm, tm), 1)
    before = (si < ti).astype(BF16)
    total = _dot(cnt.astype(BF16), before) + run_ref[...]
    ranks = [jnp.sum(jnp.where(hk, total, 0.0), axis=0, keepdims=True) for hk in hits]
    rank_ref[...] = jnp.concatenate(ranks, axis=0).astype(I32)
    run_ref[...] = run_ref[...] + jnp.sum(cnt, axis=1, keepdims=True)
    cnt_ref[...] = run_ref[...]


def _pack_bf16_pairs(v):
    bits = lax.bitcast_convert_type(v.astype(BF16).astype(F32), jnp.uint32)
    half = bits.shape[1] // 2
    return (bits[:, half:] & jnp.uint32(0xFFFF0000)) | (bits[:, :half] >> 16)


def _unpack_bf16_pairs(w):
    return (lax.bitcast_convert_type(w << 16, F32),
            lax.bitcast_convert_type(w & jnp.uint32(0xFFFF0000), F32))


def _expert_kernel(be_ref, nv_ref, x_ref, wgu_ref, bgu_ref, wd_ref, bd_ref, o_ref, wgu_bf, wd_bf, act_ref):
    blk = pl.program_id(0)
    ch = wgu_bf.shape[2]
    f = act_ref.shape[1]
    nc = f // ch
    live = nv_ref[blk]
    used = live > 0
    fresh = jnp.logical_or(blk == 0, be_ref[blk] != be_ref[jnp.maximum(blk - 1, 0)])

    @pl.when(jnp.logical_and(used, fresh))
    def _():
        for j in range(wgu_bf.shape[0]):
            wgu_bf[j] = wgu_ref[0, 0, :, j * ch:(j + 1) * ch].astype(BF16)
        for j in range(wd_bf.shape[0]):
            wd_bf[j] = wd_ref[0, 0, :, j * ch:(j + 1) * ch].astype(BF16)

    bm = x_ref.shape[0]
    hb = bm // 2

    def run(nr):
        w = x_ref[:nr, :]
        row = lax.broadcasted_iota(I32, w.shape, 0)
        w = jnp.where(row < live, w, jnp.zeros_like(w))
        lo, hi = _unpack_bf16_pairs(w)
        x = jnp.concatenate([lo.astype(BF16), hi.astype(BF16)], axis=1)
        for c in range(nc):
            c0 = slice(c * ch, (c + 1) * ch)
            c1 = slice(f + c * ch, f + (c + 1) * ch)
            glu = _dot(x, wgu_bf[c]) + bgu_ref[0, 0, :, c0]
            lin = _dot(x, wgu_bf[nc + c]) + bgu_ref[0, 0, :, c1]
            glu = jnp.minimum(glu, SWIGLU_LIMIT)
            lin = jnp.clip(lin, -SWIGLU_LIMIT, SWIGLU_LIMIT)
            act_ref[:nr, c0] = (glu * jax.nn.sigmoid(SWIGLU_ALPHA * glu) * (lin + 1.0)).astype(BF16)
        a = act_ref[:nr, :]
        y = jnp.concatenate([_dot(a, wd_bf[n]) for n in range(wd_bf.shape[0])], axis=1)
        o_ref[:nr, :] = _pack_bf16_pairs(y + bd_ref[0, 0])

    @pl.when(live > hb)
    def _():
        run(bm)

    @pl.when(jnp.logical_and(used, live <= hb))
    def _():
        run(hb)
        o_ref[hb:, :] = jnp.zeros((bm - hb, o_ref.shape[1]), o_ref.dtype)

    @pl.when(jnp.logical_not(used))
    def _():
        o_ref[...] = jnp.zeros_like(o_ref)


def _expert_call(start, nblk, cnt, buf, w_gu, b_gu, w_down, b_down, layer, bm):
    n_slots, dw = buf.shape
    _, ne, d, f2 = w_gu.shape
    f = f2 // 2
    n_blocks = n_slots // bm
    b0 = jnp.arange(n_blocks, dtype=I32) * bm
    end = start + nblk * bm
    block_e = jnp.minimum(jnp.sum((end[None, :] <= b0[:, None]).astype(I32), axis=1), ne - 1)
    n_live = jnp.clip(start[block_e] + cnt[block_e] - b0, 0, bm)
    lw = lambda i, be, nv: (layer, be[i], 0, 0)
    cw = 256
    return pl.pallas_call(
        _expert_kernel,
        grid_spec=pltpu.PrefetchScalarGridSpec(
            num_scalar_prefetch=2,
            grid=(n_blocks,),
            in_specs=[pl.BlockSpec((bm, dw), lambda i, be, nv: (i, 0)),
                      pl.BlockSpec((1, 1, d, f2), lw),
                      pl.BlockSpec((1, 1, 1, f2), lw),
                      pl.BlockSpec((1, 1, f, d), lw),
                      pl.BlockSpec((1, 1, 1, d), lw)],
            out_specs=pl.BlockSpec((bm, dw), lambda i, be, nv: (i, 0)),
            scratch_shapes=[pltpu.VMEM((f2 // cw, d, cw), BF16), pltpu.VMEM((d // cw, f, cw), BF16),
                            pltpu.VMEM((bm, f), BF16)]),
        out_shape=jax.ShapeDtypeStruct((n_slots, dw), jnp.uint32),
        compiler_params=_cparams(("arbitrary",)),
        name="moe_experts",
    )(block_e, n_live, buf, w_gu, b_gu.reshape(b_gu.shape[0], ne, 1, f2),
      w_down, b_down.reshape(b_down.shape[0], ne, 1, d))


def _expert_kernel_ring(start_ref, nblk_ref, cnt_ref, x_hbm, wgu_ref, bgu_ref, wd_ref, bd_ref, y_hbm,
                        xbuf, ybuf, wgu_bf, wd_bf, act_ref, sem_in, sem_out, *, n_blocks, fused=False):
    e = pl.program_id(0)
    bm = xbuf.shape[1]
    f = act_ref.shape[1]
    ch = 256
    nb = nblk_ref[e]
    start = start_ref[e]
    cnt = cnt_ref[e]

    def rows(j):
        return pl.ds(pl.multiple_of(start + j * bm, bm), bm)

    def fetch(j, slot):
        return pltpu.make_async_copy(x_hbm.at[rows(j)], xbuf.at[slot], sem_in.at[slot])

    def put(j, slot):
        return pltpu.make_async_copy(ybuf.at[slot], y_hbm.at[rows(j)], sem_out.at[slot])

    @pl.when(nb > 0)
    def _():
        fetch(0, 0).start()
        if wgu_bf.ndim == 3:
            for j in range(wgu_bf.shape[0]):
                wgu_bf[j] = wgu_ref[0, 0, :, j * ch:(j + 1) * ch].astype(BF16)
            for j in range(wd_bf.shape[0]):
                wd_bf[j] = wd_ref[0, 0, :, j * ch:(j + 1) * ch].astype(BF16)
        else:
            for j in range(wgu_bf.shape[1] // ch):
                cs = slice(j * ch, (j + 1) * ch)
                wgu_bf[:, cs] = wgu_ref[0, 0, :, cs].astype(BF16)
            for j in range(wd_bf.shape[1] // ch):
                cs = slice(j * ch, (j + 1) * ch)
                wd_bf[:, cs] = wd_ref[0, 0, :, cs].astype(BF16)

    def block(j, carry):
        slot = lax.rem(j, 2)

        @pl.when(j + 1 < nb)
        def _():
            fetch(j + 1, 1 - slot).start()

        fetch(j, slot).wait()

        @pl.when(j >= 2)
        def _():
            put(j - 2, slot).wait()

        w = xbuf[slot]
        row = lax.broadcasted_iota(I32, w.shape, 0)
        w = jnp.where(row < cnt - j * bm, w, jnp.zeros_like(w))
        lo, hi = _unpack_bf16_pairs(w)
        x = jnp.concatenate([lo.astype(BF16), hi.astype(BF16)], axis=1)
        if fused:
            z = _dot(x, wgu_bf[...]) + bgu_ref[0, 0]
            glu = jnp.minimum(z[:, :f], SWIGLU_LIMIT)
            lin = jnp.clip(z[:, f:], -SWIGLU_LIMIT, SWIGLU_LIMIT)
            act = (glu * jax.nn.sigmoid(SWIGLU_ALPHA * glu) * (lin + 1.0)).astype(BF16)
            ybuf[slot] = _pack_bf16_pairs(_dot(act, wd_bf[...]) + bd_ref[0, 0])
            put(j, slot).start()
            return carry
        if wgu_bf.ndim == 3:
            nc = f // ch
            for c in range(nc):
                c0 = slice(c * ch, (c + 1) * ch)
                c1 = slice(f + c * ch, f + (c + 1) * ch)
                glu = _dot(x, wgu_bf[c]) + bgu_ref[0, 0, :, c0]
                lin = _dot(x, wgu_bf[nc + c]) + bgu_ref[0, 0, :, c1]
                glu = jnp.minimum(glu, SWIGLU_LIMIT)
                lin = jnp.clip(lin, -SWIGLU_LIMIT, SWIGLU_LIMIT)
                act_ref[:, c0] = (glu * jax.nn.sigmoid(SWIGLU_ALPHA * glu) * (lin + 1.0)).astype(BF16)
            a = act_ref[...]
            y = jnp.concatenate([_dot(a, wd_bf[n]) for n in range(wd_bf.shape[0])], axis=1)
            ybuf[slot] = _pack_bf16_pairs(y + bd_ref[0, 0])
            put(j, slot).start()
            return carry
        for c in range(f // ch):
            c0 = slice(c * ch, (c + 1) * ch)
            c1 = slice(f + c * ch, f + (c + 1) * ch)
            glu = _dot(x, wgu_bf[:, c0]) + bgu_ref[0, 0, :, c0]
            lin = _dot(x, wgu_bf[:, c1]) + bgu_ref[0, 0, :, c1]
            glu = jnp.minimum(glu, SWIGLU_LIMIT)
            lin = jnp.clip(lin, -SWIGLU_LIMIT, SWIGLU_LIMIT)
            act_ref[:, c0] = (glu * jax.nn.sigmoid(SWIGLU_ALPHA * glu) * (lin + 1.0)).astype(BF16)
        ybuf[slot] = _pack_bf16_pairs(_dot(act_ref[...], wd_bf[...]) + bd_ref[0, 0])
        put(j, slot).start()
        return carry

    lax.fori_loop(0, nb, block, 0)

    @pl.when(nb >= 2)
    def _():
        put(nb - 2, lax.rem(nb, 2)).wait()

    @pl.when(nb >= 1)
    def _():
        put(nb - 1, lax.rem(nb + 1, 2)).wait()

    @pl.when(e == pl.num_programs(0) - 1)
    def _():
        ybuf[0] = jnp.zeros(ybuf.shape[1:], ybuf.dtype)

        def fill(j, carry):
            cp = pltpu.make_async_copy(ybuf.at[0], y_hbm.at[pl.ds(pl.multiple_of(j * bm, bm), bm)], sem_out.at[0])
            cp.start()
            cp.wait()
            return carry

        lax.fori_loop(start // bm + nb, n_blocks, fill, 0)


def _expert_call_ring(start, nblk, cnt, buf, w_gu, b_gu, w_down, b_down, layer, bm, fused=False, chunked=False):
    n_slots, dw = buf.shape
    _, ne, d, f2 = w_gu.shape
    f = f2 // 2
    lw = lambda e, *_: (layer, e, 0, 0)
    return pl.pallas_call(
        functools.partial(_expert_kernel_ring, n_blocks=n_slots // bm, fused=fused),
        grid_spec=pltpu.PrefetchScalarGridSpec(
            num_scalar_prefetch=3,
            grid=(ne,),
            in_specs=[pl.BlockSpec(memory_space=pl.ANY),
                      pl.BlockSpec((1, 1, d, f2), lw),
                      pl.BlockSpec((1, 1, 1, f2), lw),
                      pl.BlockSpec((1, 1, f, d), lw),
                      pl.BlockSpec((1, 1, 1, d), lw)],
            out_specs=pl.BlockSpec(memory_space=pl.ANY),
            scratch_shapes=[pltpu.VMEM((2, bm, dw), jnp.uint32), pltpu.VMEM((2, bm, dw), jnp.uint32),
                            pltpu.VMEM((f2 // 256, d, 256) if chunked else (d, f2), BF16),
                            pltpu.VMEM((d // 256, f, 256) if chunked else (f, d), BF16), pltpu.VMEM((bm, f), BF16),
                            pltpu.SemaphoreType.DMA((2,)), pltpu.SemaphoreType.DMA((2,))]),
        out_shape=jax.ShapeDtypeStruct((n_slots, dw), jnp.uint32),
        compiler_params=_cparams(("arbitrary",)),
        name="moe_experts",
    )(start, nblk, cnt, buf, w_gu, b_gu.reshape(b_gu.shape[0], ne, 1, f2),
      w_down, b_down.reshape(b_down.shape[0], ne, 1, d))


def _combine_kernel(x_ref, m_ref, y_ref, gate_ref, gout_ref, o_ref, *, final):
    m = m_ref[0]
    gate = gate_ref[...]
    half = y_ref.shape[2]
    f_lo, f_hi = None, None
    for k in range(TOP_K):
        lo, hi = _unpack_bf16_pairs(y_ref[k])
        gk = gate[:, k:k + 1]
        f_lo = gk * lo if f_lo is None else f_lo + gk * lo
        f_hi = gk * hi if f_hi is None else f_hi + gk * hi
    x_lo = x_ref[:, :half] + m[5:6, :half] * f_lo
    x_hi = x_ref[:, half:] + m[5:6, half:] * f_hi
    if final:
        ms = (jnp.sum(x_lo * x_lo, axis=-1, keepdims=True) + jnp.sum(x_hi * x_hi, axis=-1, keepdims=True))
        r = lax.rsqrt(ms * (0.5 / half) + NORM_EPS)
        x_lo = x_lo * r * gout_ref[:, :half]
        x_hi = x_hi * r * gout_ref[:, half:]
    o_ref[:, :half] = x_lo
    o_ref[:, half:] = x_hi


def _moe_layer(xs, mods, g, layer, w_router, b_router, w_gu, b_gu, w_down, b_down, geo, g_out, final):
    n, d = xs.shape
    ne = w_router.shape[1]
    f = w_down.shape[2]
    n_tiles = n // TM
    grp = functools.partial(_tile_group, geo=geo)
    const2 = lambda t: (0, 0)
    h, idx, gate, rank, cnt = pl.pallas_call(
        _route_kernel,
        grid=(n_tiles,),
        in_specs=[pl.BlockSpec((TM, d), lambda t: (t, 0)),
                  pl.BlockSpec((1, 6, d), lambda t: (grp(t), 0, 0)),
                  pl.BlockSpec((1, d), const2),
                  pl.BlockSpec((ne, d), const2),
                  pl.BlockSpec((ne, 1), const2)],
        out_specs=[pl.BlockSpec((TM, d // 2), lambda t: (t, 0)),
                   pl.BlockSpec((TOP_K, TM), lambda t: (0, t)),
                   pl.BlockSpec((TOP_K, TM), lambda t: (0, t)),
                   pl.BlockSpec((TOP_K, TM), lambda t: (0, t)),
                   pl.BlockSpec((ne, 1), const2)],
        out_shape=[jax.ShapeDtypeStruct((n, d // 2), jnp.uint32),
                   jax.ShapeDtypeStruct((TOP_K, n), I32),
                   jax.ShapeDtypeStruct((TOP_K, n), F32),
                   jax.ShapeDtypeStruct((TOP_K, n), I32),
                   jax.ShapeDtypeStruct((ne, 1), F32)],
        scratch_shapes=[pltpu.VMEM((ne, 1), F32)],
        compiler_params=_cparams(("arbitrary",)),
        name="moe_route",
    )(xs, mods, g.reshape(1, d), w_router.T.astype(BF16), b_router.reshape(ne, 1))

    bm = MOE_BM
    n_rows = n * TOP_K
    n_blocks = -(-n_rows // bm) + ne
    n_slots = n_blocks * bm
    dest, info = pl.pallas_call(
        functools.partial(_slot_kernel, bm=bm),
        grid=(n // SLOT_T,),
        in_specs=[pl.BlockSpec((TOP_K, SLOT_T), lambda t: (0, t)),
                  pl.BlockSpec((TOP_K, SLOT_T), lambda t: (0, t)),
                  pl.BlockSpec((ne, 1), const2)],
        out_specs=[pl.BlockSpec((TOP_K, SLOT_T), lambda t: (0, t)),
                   pl.BlockSpec((ne, 128), const2)],
        out_shape=[jax.ShapeDtypeStruct((TOP_K, n), I32),
                   jax.ShapeDtypeStruct((ne, 128), I32)],
        compiler_params=_cparams(("arbitrary",)),
        name="moe_slots",
    )(idx, rank, cnt)

    buf = _sc_scatter_rows(h, dest, n_slots)
    y = _expert_call(info[:, 0], info[:, 1], info[:, 2], buf, w_gu, b_gu, w_down, b_down, layer, bm)
    dy = d // 2
    yk = _sc_gather_rows(y, dest.reshape(-1)).reshape(TOP_K, n, dy)
    return pl.pallas_call(
        functools.partial(_combine_kernel, final=final),
        grid=(n_tiles,),
        in_specs=[pl.BlockSpec((TM, d), lambda t: (t, 0)),
                  pl.BlockSpec((1, 6, d), lambda t: (grp(t), 0, 0)),
                  pl.BlockSpec((TOP_K, TM, dy), lambda t: (0, t, 0)),
                  pl.BlockSpec((TM, TOP_K), lambda t: (t, 0)),
                  pl.BlockSpec((1, d), const2)],
        out_specs=pl.BlockSpec((TM, d), lambda t: (t, 0)),
        out_shape=jax.ShapeDtypeStruct((n, d), F32),
        compiler_params=_cparams(("parallel",)),
        name="moe_combine",
    )(xs, mods, yk, gate.T, g_out.reshape(1, d))


def kernel(x, c, ctx, c_ctx, ada_w, ada_b, norm_mix, norm_ffn, norm_out, a_w_in, a_g_v, a_w_s, a_b_s, a_w_out, b_w_in, b_w_a2, b_b_a, b_g_o, b_w_out, c_w_qkv, c_rpb, c_w_out, moe_w_router, moe_b_router, moe_w_gu, moe_b_gu, moe_w_down, moe_b_down):
    b, seq, d = x.shape
    lctx = ctx.shape[1]
    depth = ada_w.shape[0]
    assert (b * lctx) % TM == 0 and seq % TM == 0 and lctx % GLA_TB == 0 and seq % GLA_TB == 0
    assert seq % (NA_QROWS * GRID_W) == 0 and (b * lctx) % (NA_QROWS * GRID_W) == 0 and lctx % 256 == 0
    assert seq // GRID_W >= NA_WROWS + NA_QROWS and b + 1 <= 8
    geo = (b * lctx // TM, seq // TM, b)
    dims = (b, seq, lctx)

    cond = jnp.zeros((8, d), F32).at[:b].set(c).at[b].set(c_ctx)
    mods_all = _adaln(cond, ada_w, ada_b)[:, :b + 1].reshape(depth, b + 1, 6, d)

    xs = jnp.concatenate([ctx.reshape(b * lctx, d), x.reshape(b * seq, d)], axis=0)
    ctx_tiles = geo[0]
    has_ctx = True
    for i in range(depth):
        kind, j = i % N_MIXERS, i // N_MIXERS
        ctx_later = any(kk % N_MIXERS != 0 for kk in range(i + 1, depth))
        mods = mods_all[i]
        if kind == 0:
            keep_ctx = has_ctx and ctx_later
            skip = 0 if keep_ctx or not has_ctx else ctx_tiles
            geo_i = geo if has_ctx else (0, geo[1], b)
            xs = _gmlp_mixer(xs, mods, norm_mix[i], a_w_in[j], a_g_v[j], a_w_s[j], a_b_s[j], a_w_out[j],
                             skip, xs.shape[0] // TM - skip, geo_i)
            has_ctx = keep_ctx
        elif kind == 1:
            assert has_ctx
            xs = _gla_mixer(xs, mods, norm_mix[i], b_w_in[j], b_w_a2[j], b_b_a[j], b_g_o[j], b_w_out[j], geo, dims)
            if not ctx_later:
                xs = xs[b * lctx:]
                has_ctx = False
        else:
            assert has_ctx
            if ctx_later:
                raise NotImplementedError("context output of the neighbourhood mixer")
            xs = _na_mixer(xs, mods, norm_mix[i], c_w_qkv[j], c_rpb[j], c_w_out[j], geo, dims)
            has_ctx = False
        geo_i = geo if has_ctx else (0, geo[1], b)
        xs = _moe_layer(xs, mods, norm_ffn[i], i, moe_w_router[i], moe_b_router[i], moe_w_gu, moe_b_gu,
                        moe_w_down, moe_b_down, geo_i, norm_out, i == depth - 1)
    if has_ctx:
        xs = xs[b * lctx:]
    return xs.reshape(b, seq, d)
```

```python
import functools

import numpy as np
import jax
import jax.numpy as jnp
from jax import lax
from jax.experimental import pallas as pl
from jax.experimental.pallas import tpu as pltpu
from jax.experimental.pallas import tpu_sc as plsc

F32 = jnp.float32
BF16 = jnp.bfloat16
I32 = jnp.int32

NORM_EPS = 1e-6
GRID_W = 64
N_MIXERS = 3

CHUNK_A = 128
A_GROUPS = 8
GLA_HEADS = 4
GLA_RANK = 16
GLA_TAU = 16.0
GLA_CHUNK = 64
ROPE_BASE = 10000.0
NA_HEADS = 16
NA_KH = 8
NA_KW = 16
NEG_INF = -1e30
NA_QROWS = 4
NA_WROWS = 12
NA_PAIRS_PER_STEP = 2
TOP_K = 4
SWIGLU_LIMIT = 7.0
SWIGLU_ALPHA = 1.702
MOE_BM = 512

TM = 512
GLA_TB = 256
VMEM_LIMIT = 56 * 1024 * 1024


def _cparams(sem):
    return pltpu.CompilerParams(dimension_semantics=sem, vmem_limit_bytes=VMEM_LIMIT)


def _dot(a, b):
    return jnp.dot(a, b, preferred_element_type=F32)


def _dot_nt(a, b):
    return lax.dot_general(a, b, (((1,), (1,)), ((), ())), preferred_element_type=F32)


def _dot_tn(a, b):
    return lax.dot_general(a, b, (((0,), (0,)), ((), ())), preferred_element_type=F32)


def _rms(x, g):
    return x * lax.rsqrt(jnp.mean(x * x, axis=-1, keepdims=True) + NORM_EPS) * g


def _modulate(x, g, shift, scale):
    return _rms(x, g) * (1.0 + scale) + shift


def _ada_kernel(s_ref, w_ref, b_ref, o_ref):
    s = s_ref[...]
    s = s * jax.nn.sigmoid(s)
    o_ref[0] = jnp.dot(s, w_ref[0], preferred_element_type=F32,
                       precision=lax.Precision.HIGHEST) + b_ref[0]


def _adaln(cond, ada_w, ada_b):
    depth, d, n6 = ada_w.shape
    bn = n6 // 4
    return pl.pallas_call(
        _ada_kernel,
        grid=(depth, n6 // bn),
        in_specs=[pl.BlockSpec((8, d), lambda i, j: (0, 0)),
                  pl.BlockSpec((1, d, bn), lambda i, j: (i, 0, j)),
                  pl.BlockSpec((1, 1, bn), lambda i, j: (i, 0, j))],
        out_specs=pl.BlockSpec((1, 8, bn), lambda i, j: (i, 0, j)),
        out_shape=jax.ShapeDtypeStruct((depth, 8, n6), F32),
        compiler_params=_cparams(("parallel", "parallel")),
        name="adaln",
    )(cond, ada_w, ada_b.reshape(depth, 1, n6))


def _gelu(z):
    return 0.5 * z * (1.0 + lax.erf(z * np.float32(np.sqrt(0.5))))


def _gmlp_kernel(x_ref, m_ref, g_ref, win_ref, gv_ref, ws_ref, bs_ref, wout_ref, o_ref):
    x = x_ref[...]
    m = m_ref[0]
    a = gv_ref.shape[1]
    h = _modulate(x, g_ref[...], m[0:1], m[1:2]).astype(BF16)
    z = _gelu(_dot(h, win_ref[...]))
    u = z[:, :a]
    v = _rms(z[:, a:], gv_ref[...]).astype(BF16)
    gw = a // A_GROUPS
    rows = []
    for c in range(x.shape[0] // CHUNK_A):
        cols = [_dot(ws_ref[g], v[c * CHUNK_A:(c + 1) * CHUNK_A, g * gw:(g + 1) * gw])
                for g in range(A_GROUPS)]
        rows.append(jnp.concatenate(cols, axis=1) + bs_ref[...])
    s = jnp.concatenate(rows, axis=0)
    y = _dot((u * s).astype(BF16), wout_ref[...])
    o_ref[...] = x + m[2:3] * y


def _gmlp_mixer(xs, mods, g, w_in, g_v, w_s, b_s, w_out, tile0, n_tiles, geo):
    d = xs.shape[1]
    a = g_v.shape[0]
    gw = a // A_GROUPS
    bias = jnp.repeat(b_s.T, gw, axis=1)
    grp = functools.partial(_tile_group, geo=geo, tile0=tile0)
    const2 = lambda t: (0, 0)
    return pl.pallas_call(
        _gmlp_kernel,
        grid=(n_tiles,),
        in_specs=[pl.BlockSpec((TM, d), lambda t: (t + tile0, 0)),
                  pl.BlockSpec((1, 6, d), lambda t: (grp(t), 0, 0)),
                  pl.BlockSpec((1, d), const2),
                  pl.BlockSpec((d, 2 * a), const2),
                  pl.BlockSpec((1, a), const2),
                  pl.BlockSpec((A_GROUPS, CHUNK_A, CHUNK_A), lambda t: (0, 0, 0)),
                  pl.BlockSpec((CHUNK_A, a), const2),
                  pl.BlockSpec((a, d), const2)],
        out_specs=pl.BlockSpec((TM, d), lambda t: (t, 0)),
        out_shape=jax.ShapeDtypeStruct((n_tiles * TM, d), F32),
        compiler_params=_cparams(("parallel",)),
        name="gmlp_mixer",
    )(xs, mods, g.reshape(1, d), w_in.astype(BF16), g_v.reshape(1, a), w_s.astype(BF16),
      bias, w_out.astype(BF16))


def _tile_group(t, geo, tile0=0):
    n_ctx_tiles, tiles_per_batch, nb = geo
    tt = t + tile0
    return jnp.where(tt < n_ctx_tiles, nb, (tt - n_ctx_tiles) // tiles_per_batch)


def _dot_f32(tri_bf16, x):
    hi = x.astype(BF16)
    r1 = x - hi.astype(F32)
    mid = r1.astype(BF16)
    lo = (r1 - mid.astype(F32)).astype(BF16)
    return _dot(tri_bf16, hi) + _dot(tri_bf16, mid) + _dot(tri_bf16, lo)


def _gla_proj_kernel(x_ref, m_ref, g_ref, wq_ref, wk_ref, wv_ref, wg_ref, wr_ref, wa_ref, ba_ref,
                     cos_ref, sin_ref, q_ref, k_ref, v_ref, go_ref, la_ref):
    x = x_ref[...]
    m = m_ref[0]
    h = _modulate(x, g_ref[...], m[0:1], m[1:2]).astype(BF16)
    kdim = wq_ref.shape[1]
    hk = kdim // GLA_HEADS
    q = _dot(h, wq_ref[...]) * np.float32(hk ** -0.5)
    k = _dot(h, wk_ref[...])
    v_ref[...] = _dot(h, wv_ref[...]).astype(BF16)
    go_ref[...] = _dot(h, wg_ref[...])
    r = _dot(h, wr_ref[...])
    z = _dot(r.astype(BF16), wa_ref[...]) + ba_ref[...]
    la_ref[...] = jax.nn.log_sigmoid(z) * np.float32(1.0 / GLA_TAU)

    cos = jnp.concatenate([cos_ref[...]] * GLA_HEADS, axis=1)
    sin = jnp.concatenate([sin_ref[...]] * GLA_HEADS, axis=1)
    nf = hk // 4
    lane = lax.broadcasted_iota(I32, q.shape, 1)
    first = (lane % (2 * nf)) < nf

    def rope(t):
        up = pltpu.roll(t, kdim - nf, 1)
        dn = pltpu.roll(t, nf, 1)
        return t * cos + jnp.where(first, up, dn) * sin

    q_ref[...] = rope(q)
    k_ref[...] = rope(k)


def _gla_chunk(q_ref, k_ref, v_ref, la_ref, o_ref, st_ref, n, rev):
    kdim = q_ref.shape[1]
    hk = kdim // GLA_HEADS
    hv = v_ref.shape[1] // GLA_HEADS
    c = GLA_CHUNK
    ri = lax.broadcasted_iota(I32, (c, c), 0)
    ci = lax.broadcasted_iota(I32, (c, c), 1)
    keep = (ci >= ri) if rev else (ci <= ri)
    tri = keep.astype(BF16)
    rows = slice(n * c, (n + 1) * c)
    cum = _dot_f32(tri, la_ref[rows, :])
    last = cum[0:1] if rev else cum[c - 1:c]
    q = q_ref[rows, :]
    k = k_ref[rows, :]
    q_dec = (q * jnp.exp(cum)).astype(BF16)
    k_inv = (k * jnp.exp(-cum)).astype(BF16)
    k_end = (k * jnp.exp(last - cum)).astype(BF16)
    dec = jnp.exp(last)
    for h in range(GLA_HEADS):
        ks = slice(h * hk, (h + 1) * hk)
        vs = slice(h * hv, (h + 1) * hv)
        att = jnp.where(keep, _dot_nt(q_dec[:, ks], k_inv[:, ks]), 0.0).astype(BF16)
        vh = v_ref[rows, vs]
        st = st_ref[h]
        o_ref[rows, vs] = _dot(att, vh) + _dot_nt(q_dec[:, ks], st.astype(BF16))
        st_ref[h] = st * dec[:, ks] + _dot_tn(vh, k_end[:, ks])


def _gla_scan_kernel(qf_ref, kf_ref, vf_ref, laf_ref, qb_ref, kb_ref, vb_ref, lab_ref,
                     of_ref, ob_ref, stf_ref, stb_ref):
    s = pl.program_id(1)

    @pl.when(s == 0)
    def _():
        stf_ref[...] = jnp.zeros_like(stf_ref)
        stb_ref[...] = jnp.zeros_like(stb_ref)

    nch = qf_ref.shape[0] // GLA_CHUNK
    for n in range(nch):
        _gla_chunk(qf_ref, kf_ref, vf_ref, laf_ref, of_ref, stf_ref, n, False)
        _gla_chunk(qb_ref, kb_ref, vb_ref, lab_ref, ob_ref, stb_ref, nch - 1 - n, True)


def _gla_out_kernel(x_ref, m_ref, of_ref, ob_ref, go_ref, gn_ref, wout_ref, o_ref):
    x = x_ref[...]
    m = m_ref[0]
    o = of_ref[...] + ob_ref[...]
    hv = gn_ref.shape[1]
    parts = [_rms(o[:, h * hv:(h + 1) * hv], gn_ref[...]) for h in range(GLA_HEADS)]
    o = jnp.concatenate(parts, axis=1)
    gate = go_ref[...]
    y = _dot((o * (gate * jax.nn.sigmoid(gate))).astype(BF16), wout_ref[...])
    o_ref[...] = x + m[2:3] * y


def _gla_mixer(xs, mods, g, w_in, w_a2, b_a, g_o, w_out, geo, dims):
    b, seq, lctx = dims
    nt, d = xs.shape
    n_tiles = nt // TM
    kdim = w_a2.shape[2]
    vdim = g_o.shape[0] * GLA_HEADS
    hk = kdim // GLA_HEADS
    nf = hk // 4
    wq = w_in[:, :kdim].astype(BF16)
    wk = w_in[:, kdim:2 * kdim].astype(BF16)
    wv = w_in[:, 2 * kdim:2 * kdim + vdim].astype(BF16)
    wg = w_in[:, 2 * kdim + vdim:2 * kdim + 2 * vdim].astype(BF16)
    wr = jnp.pad(w_in[:, 2 * kdim + 2 * vdim:], ((0, 0), (0, 128 - 2 * GLA_RANK))).astype(BF16)
    wa = jnp.zeros((128, 2 * kdim), F32)
    wa = wa.at[:GLA_RANK, :kdim].set(w_a2[0]).at[GLA_RANK:2 * GLA_RANK, kdim:].set(w_a2[1]).astype(BF16)
    ba = b_a.reshape(1, 2 * kdim)

    tpos = jnp.arange(seq)
    freqs = jnp.power(ROPE_BASE, -jnp.arange(nf, dtype=F32) / nf)
    ar = (tpos // GRID_W).astype(F32)[:, None] * freqs
    ac = (tpos % GRID_W).astype(F32)[:, None] * freqs
    cos = jnp.concatenate([jnp.cos(ar), jnp.cos(ar), jnp.cos(ac), jnp.cos(ac)], axis=1)
    sin = jnp.concatenate([-jnp.sin(ar), jnp.sin(ar), -jnp.sin(ac), jnp.sin(ac)], axis=1)
    cos = jnp.concatenate([jnp.ones((TM, hk), F32), cos], axis=0)
    sin = jnp.concatenate([jnp.zeros((TM, hk), F32), sin], axis=0)

    n_ctx_tiles, tiles_per_batch, _ = geo
    grp = functools.partial(_tile_group, geo=geo)

    def rope_blk(t):
        return (jnp.where(t < n_ctx_tiles, 0, 1 + (t - n_ctx_tiles) % tiles_per_batch), 0)

    const2 = lambda t: (0, 0)
    row = lambda t: (t, 0)
    q, k, v, go, la = pl.pallas_call(
        _gla_proj_kernel,
        grid=(n_tiles,),
        in_specs=[pl.BlockSpec((TM, d), row),
                  pl.BlockSpec((1, 6, d), lambda t: (grp(t), 0, 0)),
                  pl.BlockSpec((1, d), const2),
                  pl.BlockSpec((d, kdim), const2), pl.BlockSpec((d, kdim), const2),
                  pl.BlockSpec((d, vdim), const2), pl.BlockSpec((d, vdim), const2),
                  pl.BlockSpec((d, 128), const2), pl.BlockSpec((128, 2 * kdim), const2),
                  pl.BlockSpec((1, 2 * kdim), const2),
                  pl.BlockSpec((TM, hk), rope_blk), pl.BlockSpec((TM, hk), rope_blk)],
        out_specs=[pl.BlockSpec((TM, kdim), row), pl.BlockSpec((TM, kdim), row),
                   pl.BlockSpec((TM, vdim), row), pl.BlockSpec((TM, vdim), row),
                   pl.BlockSpec((TM, 2 * kdim), row)],
        out_shape=[jax.ShapeDtypeStruct((nt, kdim), F32), jax.ShapeDtypeStruct((nt, kdim), F32),
                   jax.ShapeDtypeStruct((nt, vdim), BF16), jax.ShapeDtypeStruct((nt, vdim), F32),
                   jax.ShapeDtypeStruct((nt, 2 * kdim), F32)],
        compiler_params=_cparams(("parallel",)),
        name="gla_proj",
    )(xs, mods, g.reshape(1, d), wq, wk, wv, wg, wr, wa, ba, cos, sin)

    tb = GLA_TB
    ctx_steps = lctx // tb
    lat_steps = seq // tb
    steps = ctx_steps + lat_steps

    def blk(bi, s, rev):
        if rev:
            cs, ls = ctx_steps - 1 - s, lat_steps - 1 - (s - ctx_steps)
        else:
            cs, ls = s, s - ctx_steps
        return jnp.where(s < ctx_steps, bi * ctx_steps + cs, b * ctx_steps + bi * lat_steps + ls)

    def dir_specs(rev):
        row_blk = lambda bi, s: (blk(bi, s, rev), 0)
        return [pl.BlockSpec((tb, kdim), row_blk), pl.BlockSpec((tb, kdim), row_blk),
                pl.BlockSpec((tb, vdim), row_blk),
                pl.BlockSpec((tb, kdim), lambda bi, s: (blk(bi, s, rev), 1 if rev else 0))]

    state = pltpu.VMEM((GLA_HEADS, vdim // GLA_HEADS, hk), F32)
    o_f, o_b = pl.pallas_call(
        _gla_scan_kernel,
        grid=(b, steps),
        in_specs=dir_specs(False) + dir_specs(True),
        out_specs=[pl.BlockSpec((tb, vdim), lambda bi, s: (blk(bi, s, False), 0)),
                   pl.BlockSpec((tb, vdim), lambda bi, s: (blk(bi, s, True), 0))],
        out_shape=[jax.ShapeDtypeStruct((nt, vdim), F32)] * 2,
        scratch_shapes=[state, state],
        compiler_params=_cparams(("parallel", "arbitrary")),
        name="gla_scan",
    )(q, k, v, la, q, k, v, la)

    return pl.pallas_call(
        _gla_out_kernel,
        grid=(n_tiles,),
        in_specs=[pl.BlockSpec((TM, d), row),
                  pl.BlockSpec((1, 6, d), lambda t: (grp(t), 0, 0)),
                  pl.BlockSpec((TM, vdim), row), pl.BlockSpec((TM, vdim), row),
                  pl.BlockSpec((TM, vdim), row),
                  pl.BlockSpec((1, vdim // GLA_HEADS), const2),
                  pl.BlockSpec((vdim, d), const2)],
        out_specs=pl.BlockSpec((TM, d), row),
        out_shape=jax.ShapeDtypeStruct((nt, d), F32),
        compiler_params=_cparams(("parallel",)),
        name="gla_out",
    )(xs, mods, o_f, o_b, go, g_o.reshape(1, -1), w_out.astype(BF16))


def _na_proj_kernel(x_ref, m_ref, g_ref, w_ref, q_ref, k_ref, v_ref):
    x = x_ref[...]
    m = m_ref[0]
    d = x.shape[1]
    h = _modulate(x, g_ref[...], m[0:1], m[1:2]).astype(BF16)
    qkv = _dot(h, w_ref[...])
    hd = d // NA_HEADS
    for p in range(q_ref.shape[0]):
        cs = slice(p * 128, (p + 1) * 128)
        q_ref[p] = (qkv[:, cs] * np.float32(hd ** -0.5)).astype(BF16)
        k_ref[p] = qkv[:, d + p * 128:d + (p + 1) * 128].astype(BF16)
        v_ref[p] = qkv[:, 2 * d + p * 128:2 * d + (p + 1) * 128].astype(BF16)


def _na_attn_kernel(tbl_ref, q_ref, *refs, n_tiles, nwin):
    k_refs, v_refs = refs[:nwin], refs[nwin:2 * nwin]
    kc_ref, vc_ref, bias_ref, o_ref, s_ref = refs[2 * nwin:]
    t = pl.program_id(2)
    typ = jnp.where(t == 0, 0, jnp.where(t == n_tiles - 1, 2, 1))
    npair = NA_WROWS // 2
    lane = lax.broadcasted_iota(I32, q_ref.shape[1:], 1)
    outs = []
    for pp in range(q_ref.shape[0]):
        q = q_ref[pp]
        kw = jnp.concatenate([r[pp] for r in k_refs], axis=0)
        vw = jnp.concatenate([r[pp] for r in v_refs], axis=0)
        kc = kc_ref[pp]
        vc = vc_ref[pp]
        heads = []
        for hh in range(2):
            sel = (lane < 64) if hh == 0 else (lane >= 64)
            qm = jnp.where(sel, q, jnp.zeros_like(q))
            s = _dot_nt(qm, kw)
            s_ctx = _dot_nt(qm, kc)
            sl = s_ref.at[2 * pp + hh]
            for dr in range(NA_QROWS):
                for j in range(npair):
                    e = tbl_ref[typ * (NA_QROWS * npair) + dr * npair + j]
                    rs = slice(dr * GRID_W, (dr + 1) * GRID_W)
                    cs = slice(j * 128, (j + 1) * 128)
                    sl[rs, cs] = s[rs, cs] + bias_ref[2 * pp + hh, e]
            s_lat = sl[...]
            mx = jnp.maximum(jnp.max(s_lat, axis=1, keepdims=True), jnp.max(s_ctx, axis=1, keepdims=True))
            e_lat = jnp.exp(s_lat - mx)
            e_ctx = jnp.exp(s_ctx - mx)
            inv = 1.0 / (jnp.sum(e_lat, axis=1, keepdims=True) + jnp.sum(e_ctx, axis=1, keepdims=True))
            heads.append(_dot((e_lat * inv).astype(BF16), vw) + _dot((e_ctx * inv).astype(BF16), vc))
        outs.append(jnp.where(lane < 64, heads[0], heads[1]))
    o_ref[...] = jnp.concatenate(outs, axis=1).astype(BF16)


def _na_out_kernel(x_ref, m_ref, a_ref, w_ref, o_ref):
    m = m_ref[0]
    o_ref[...] = x_ref[...] + m[2:3] * _dot(a_ref[...], w_ref[...])


def _na_tables(rows):
    npair = NA_WROWS // 2
    tbl = np.zeros((3, NA_QROWS, npair), np.int32)
    for typ, r_base in enumerate((0, NA_QROWS, rows - NA_QROWS)):
        w0 = int(np.clip(r_base - NA_KH // 2, 0, rows - NA_WROWS))
        for dr in range(NA_QROWS):
            r = r_base + dr
            r0 = int(np.clip(r - NA_KH // 2, 0, rows - NA_KH))
            for j in range(npair):
                kr = (w0 + 2 * j, w0 + 2 * j + 1)
                ok = [r0 <= x < r0 + NA_KH for x in kr]
                ri = [x - r + NA_KH - 1 for x in kr]
                if ok[0] and ok[1]:
                    e = ri[0]
                elif ok[0]:
                    e = 16 + ri[0]
                elif ok[1]:
                    e = 32 + ri[1]
                else:
                    e = 63
                tbl[typ, dr, j] = e
    return tbl.reshape(-1)


def _na_bias_table(rpb):
    nh = rpb.shape[0]
    qc = np.arange(GRID_W)
    cstart = np.clip(qc - NA_KW // 2, 0, GRID_W - NA_KW)
    kc = np.arange(GRID_W)
    ok = (kc[None, :] >= cstart[:, None]) & (kc[None, :] < cstart[:, None] + NA_KW)
    nr, nw = rpb.shape[1], rpb.shape[2]
    w2 = 2 * GRID_W
    lead = GRID_W - NA_KW
    line = jnp.pad(rpb.astype(F32), ((0, 0), (0, 0), (lead, w2 - lead - nw)))
    rel = jnp.tile(line, (1, 1, GRID_W))[:, :, :GRID_W * (w2 - 1)].reshape(nh, nr, GRID_W, w2 - 1)
    rel = rel[..., GRID_W - 1:]
    cb = jnp.where(ok[None, None], rel, NEG_INF)
    neg = jnp.full((nh, 1, GRID_W, GRID_W), NEG_INF, F32)
    cb = jnp.concatenate([cb, neg], axis=1)
    negs = jnp.broadcast_to(neg, cb.shape)
    nxt = jnp.concatenate([cb[:, 1:], neg], axis=1)
    both = jnp.concatenate([cb, nxt], axis=-1)
    left = jnp.concatenate([cb, negs], axis=-1)
    right = jnp.concatenate([negs, cb], axis=-1)
    none = jnp.concatenate([negs, negs], axis=-1)
    return jnp.concatenate([both, left, right, none], axis=1)


def _na_mixer(xs, mods, g, w_qkv, rpb, w_out, geo, dims):
    b, seq, lctx = dims
    nt, d = xs.shape
    n_tiles_tok = nt // TM
    grp = functools.partial(_tile_group, geo=geo)
    npairs = d // 128
    const2 = lambda t: (0, 0)
    q, k, v = pl.pallas_call(
        _na_proj_kernel,
        grid=(n_tiles_tok,),
        in_specs=[pl.BlockSpec((TM, d), lambda t: (t, 0)),
                  pl.BlockSpec((1, 6, d), lambda t: (grp(t), 0, 0)),
                  pl.BlockSpec((1, d), const2),
                  pl.BlockSpec((d, 3 * d), const2)],
        out_specs=[pl.BlockSpec((npairs, TM, 128), lambda t: (0, t, 0))] * 3,
        out_shape=[jax.ShapeDtypeStruct((npairs, nt, 128), BF16)] * 3,
        compiler_params=_cparams(("parallel",)),
        name="na_proj",
    )(xs, mods, g.reshape(1, d), w_qkv.astype(BF16))

    rows = seq // GRID_W
    tq = NA_QROWS * GRID_W
    n_tiles = rows // NA_QROWS
    wb = 256
    nwin = NA_WROWS * GRID_W // wb
    lat0 = b * lctx
    tbl = jnp.asarray(_na_tables(rows))
    bias = _na_bias_table(rpb)

    pp = NA_PAIRS_PER_STEP
    assert npairs % pp == 0 and (NA_QROWS * GRID_W) % wb == 0 and (NA_KH // 2 * GRID_W) % wb == 0

    def win(i):
        def f(p, bi, t, tbl_ref):
            w = jnp.clip(t * (NA_QROWS * GRID_W // wb) - NA_KH // 2 * GRID_W // wb, 0, seq // wb - nwin)
            return (p, (lat0 + bi * seq) // wb + w + i, 0)
        return f

    kv_specs = [pl.BlockSpec((pp, wb, 128), win(i)) for i in range(nwin)]
    attn = pl.pallas_call(
        functools.partial(_na_attn_kernel, n_tiles=n_tiles, nwin=nwin),
        grid_spec=pltpu.PrefetchScalarGridSpec(
            num_scalar_prefetch=1,
            grid=(npairs // pp, b, n_tiles),
            in_specs=[pl.BlockSpec((pp, tq, 128), lambda p, bi, t, tr: (p, (lat0 + bi * seq) // tq + t, 0))]
                     + kv_specs + kv_specs
                     + [pl.BlockSpec((pp, lctx, 128), lambda p, bi, t, tr: (p, bi, 0)),
                        pl.BlockSpec((pp, lctx, 128), lambda p, bi, t, tr: (p, bi, 0)),
                        pl.BlockSpec((2 * pp, 64, GRID_W, 2 * GRID_W), lambda p, bi, t, tr: (p, 0, 0, 0))],
            out_specs=pl.BlockSpec((tq, 128 * pp), lambda p, bi, t, tr: (bi * n_tiles + t, p)),
            scratch_shapes=[pltpu.VMEM((2 * pp, tq, NA_WROWS * GRID_W), F32)]),
        out_shape=jax.ShapeDtypeStruct((b * seq, d), BF16),
        compiler_params=_cparams(("parallel", "parallel", "arbitrary")),
        name="na_attn",
    )(tbl, q, *([k] * nwin), *([v] * nwin), k, v, bias)

    n_lat_tiles = b * seq // TM
    tile0 = lat0 // TM
    grp_l = functools.partial(_tile_group, geo=geo, tile0=tile0)
    return pl.pallas_call(
        _na_out_kernel,
        grid=(n_lat_tiles,),
        in_specs=[pl.BlockSpec((TM, d), lambda t: (t + tile0, 0)),
                  pl.BlockSpec((1, 6, d), lambda t: (grp_l(t), 0, 0)),
                  pl.BlockSpec((TM, d), lambda t: (t, 0)),
                  pl.BlockSpec((d, d), const2)],
        out_specs=pl.BlockSpec((TM, d), lambda t: (t, 0)),
        out_shape=jax.ShapeDtypeStruct((b * seq, d), F32),
        compiler_params=_cparams(("parallel",)),
        name="na_out",
    )(xs, mods, attn, w_out.astype(BF16))


SC_CORES = 2
SC_SUBCORES = 16
SC_WORKERS = SC_CORES * SC_SUBCORES
SLOT_T = 512


def _sc_mesh():
    return plsc.VectorSubcoreMesh(core_axis_name="c", subcore_axis_name="s")


def _sc_chunk(per_worker, max_chunk):
    return max(c for c in range(8, max_chunk + 1, 8) if per_worker % (2 * c) == 0)


def _sc_gather_rows(table, idx):
    dd = table.shape[1]
    bsz = idx.shape[0]
    per_w = bsz // SC_WORKERS
    assert per_w * SC_WORKERS == bsz
    chunk = _sc_chunk(per_w, 64)
    n_chunks = per_w // chunk

    def body(table_hbm, idx_hbm, out_hbm, idx_v, rows0, rows1, g0, g1, w0, w1):
        wid = lax.axis_index("s") * SC_CORES + lax.axis_index("c")
        pltpu.sync_copy(idx_hbm.at[wid], idx_v)
        base = wid * per_w

        def out_rows(j):
            return out_hbm.at[pl.ds(pl.multiple_of(base + j * chunk, 8), chunk)]

        def step(i, carry):
            j0, j1 = 2 * i, 2 * i + 1
            ga = pltpu.async_copy(table_hbm.at[idx_v.at[j0]], rows0, g0)
            gb = pltpu.async_copy(table_hbm.at[idx_v.at[j1]], rows1, g1)
            ga.wait()
            wa = pltpu.async_copy(rows0, out_rows(j0), w0)
            gb.wait()
            wb = pltpu.async_copy(rows1, out_rows(j1), w1)
            wa.wait()
            wb.wait()
            return carry

        lax.fori_loop(0, n_chunks // 2, step, 0)

    return pl.kernel(
        body, out_type=jax.ShapeDtypeStruct((bsz, dd), table.dtype), mesh=_sc_mesh(),
        scratch_types=[pltpu.VMEM((n_chunks, chunk), I32),
                       pltpu.VMEM((chunk, dd), table.dtype), pltpu.VMEM((chunk, dd), table.dtype),
                       pltpu.SemaphoreType.DMA, pltpu.SemaphoreType.DMA,
                       pltpu.SemaphoreType.DMA, pltpu.SemaphoreType.DMA],
        name="sc_gather_rows",
    )(table, idx.reshape(SC_WORKERS, n_chunks, chunk))


def _sc_scatter_rows(rows, dest, n_out):
    n, dd = rows.shape
    kk = dest.shape[0]
    per_w = n // SC_WORKERS
    assert per_w * SC_WORKERS == n
    chunk = _sc_chunk(per_w, 64)
    n_chunks = per_w // chunk
    dest_w = dest.reshape(kk, SC_WORKERS, n_chunks, chunk).transpose(1, 2, 0, 3)
    dest_w = dest_w.reshape(SC_WORKERS, n_chunks * kk, chunk)

    def body(rows_hbm, dest_hbm, out_hbm, idx_v, rows0, rows1, r0, r1, s0, s1):
        wid = lax.axis_index("s") * SC_CORES + lax.axis_index("c")
        pltpu.sync_copy(dest_hbm.at[wid], idx_v)
        base = wid * per_w

        def in_rows(j):
            return rows_hbm.at[pl.ds(pl.multiple_of(base + j * chunk, 8), chunk)]

        def step(i, carry):
            j0, j1 = 2 * i, 2 * i + 1
            ra = pltpu.async_copy(in_rows(j0), rows0, r0)
            rb = pltpu.async_copy(in_rows(j1), rows1, r1)
            ra.wait()
            sa = [pltpu.async_copy(rows0, out_hbm.at[idx_v.at[j0 * kk + k]], s0) for k in range(kk)]
            rb.wait()
            sb = [pltpu.async_copy(rows1, out_hbm.at[idx_v.at[j1 * kk + k]], s1) for k in range(kk)]
            for cp in sa + sb:
                cp.wait()
            return carry

        lax.fori_loop(0, n_chunks // 2, step, 0)

    return pl.kernel(
        body, out_type=jax.ShapeDtypeStruct((n_out, dd), rows.dtype), mesh=_sc_mesh(),
        scratch_types=[pltpu.VMEM((n_chunks * kk, chunk), I32),
                       pltpu.VMEM((chunk, dd), rows.dtype), pltpu.VMEM((chunk, dd), rows.dtype),
                       pltpu.SemaphoreType.DMA, pltpu.SemaphoreType.DMA,
                       pltpu.SemaphoreType.DMA, pltpu.SemaphoreType.DMA],
        name="sc_scatter_rows",
    )(rows, dest_w)


def _slot_kernel(idx_ref, rank_ref, cnt_ref, dest_ref, info_ref, *, bm):
    cnt = cnt_ref[...]
    ne = cnt.shape[0]
    padded = jnp.floor((cnt + (bm - 1.0)) * (1.0 / bm)) * bm
    acc = jnp.broadcast_to(padded, (ne, 128))
    row = lax.broadcasted_iota(I32, (ne, 128), 0)
    s = 1
    while s < ne:
        acc = acc + jnp.where(row >= s, pltpu.roll(acc, s, 0), 0.0)
        s *= 2
    pad_end = acc[:, 0:1]
    pad_start = pad_end - padded
    idx = idx_ref[...]
    e_iota = lax.broadcasted_iota(I32, (ne, idx.shape[1]), 0)
    starts = [jnp.sum(jnp.where(e_iota == idx[k:k + 1], pad_start, 0.0), axis=0, keepdims=True)
              for k in range(idx.shape[0])]
    dest_ref[...] = jnp.concatenate(starts, axis=0).astype(I32) + rank_ref[...]
    lane = lax.broadcasted_iota(I32, info_ref.shape, 1)
    info = jnp.where(lane == 0, pad_start, jnp.where(lane == 1, padded * (1.0 / bm), cnt))
    info_ref[...] = info.astype(I32)


def _route_kernel(x_ref, m_ref, g_ref, wr_ref, br_ref, h_ref, idx_ref, gate_ref, rank_ref, cnt_ref, run_ref):
    t = pl.program_id(0)

    @pl.when(t == 0)
    def _():
        run_ref[...] = jnp.zeros_like(run_ref)

    x = x_ref[...]
    m = m_ref[0]
    h = _modulate(x, g_ref[...], m[3:4], m[4:5]).astype(BF16)
    h_ref[...] = _pack_bf16_pairs(h)
    logits = _dot_nt(wr_ref[...], h) + br_ref[...]
    ne, tm = logits.shape
    e_iota = lax.broadcasted_iota(I32, (ne, tm), 0)
    vals, idxs = [], []
    l = logits
    for _ in range(TOP_K):
        mk = jnp.max(l, axis=0, keepdims=True)
        ik = jnp.min(jnp.where(l == mk, e_iota, ne), axis=0, keepdims=True)
        vals.append(mk)
        idxs.append(ik)
        l = jnp.where(e_iota == ik, -jnp.inf, l)
    top_val = jnp.concatenate(vals, axis=0)
    ex = jnp.exp(top_val - vals[0])
    gate_ref[...] = ex / jnp.sum(ex, axis=0, keepdims=True)
    idx_ref[...] = jnp.concatenate(idxs, axis=0)

    hits = [e_iota == ik for ik in idxs]
    cnt = hits[0].astype(F32)
    for hk in hits[1:]:
        cnt = cnt + hk.astype(F32)
    si = lax.broadcasted_iota(I32, (tm, tm), 0)
    ti = lax.broadcasted_iota(I32, (tm, tm), 1)
    before = (si < ti).astype(BF16)
    total = _dot(cnt.astype(BF16), before) + run_ref[...]
    ranks = [jnp.sum(jnp.where(hk, total, 0.0), axis=0, keepdims=True) for hk in hits]
    rank_ref[...] = jnp.concatenate(ranks, axis=0).astype(I32)
    run_ref[...] = run_ref[...] + jnp.sum(cnt, axis=1, keepdims=True)
    cnt_ref[...] = run_ref[...]


def _pack_bf16_pairs(v):
    bits = lax.bitcast_convert_type(v.astype(BF16).astype(F32), jnp.uint32)
    half = bits.shape[1] // 2
    return (bits[:, half:] & jnp.uint32(0xFFFF0000)) | (bits[:, :half] >> 16)


def _unpack_bf16_pairs(w):
    return (lax.bitcast_convert_type(w << 16, F32),
            lax.bitcast_convert_type(w & jnp.uint32(0xFFFF0000), F32))


def _expert_kernel(be_ref, nv_ref, nx_ref, sl_ref, x_ref, wgu_hbm, bgu_ref, wd_hbm, bd_ref, o_ref,
                   wgu_in, wd_in, wgu_bf, wd_bf, act_ref, sem_gu, sem_d, *, layer):
    blk = pl.program_id(0)
    ch = wgu_bf.shape[2]
    f = act_ref.shape[1]
    nc = f // ch
    live = nv_ref[blk]
    used = live > 0
    fresh = jnp.logical_or(blk == 0, be_ref[blk] != be_ref[jnp.maximum(blk - 1, 0)])

    def fetch(e, slot):
        return (pltpu.make_async_copy(wgu_hbm.at[layer, e], wgu_in.at[slot], sem_gu.at[slot]),
                pltpu.make_async_copy(wd_hbm.at[layer, e], wd_in.at[slot], sem_d.at[slot]))

    @pl.when(jnp.logical_and(used, fresh))
    def _():
        slot = sl_ref[blk]
        mine = fetch(be_ref[blk], slot)

        @pl.when(blk == 0)
        def _():
            for cp in mine:
                cp.start()

        @pl.when(nx_ref[blk] >= 0)
        def _():
            for cp in fetch(nx_ref[blk], 1 - slot):
                cp.start()

        for cp in mine:
            cp.wait()
        for j in range(wgu_bf.shape[0]):
            wgu_bf[j] = wgu_in[slot, :, j * ch:(j + 1) * ch].astype(BF16)
        for j in range(wd_bf.shape[0]):
            wd_bf[j] = wd_in[slot, :, j * ch:(j + 1) * ch].astype(BF16)

    bm = x_ref.shape[0]
    hb = bm // 2

    def run(nr):
        w = x_ref[:nr, :]
        row = lax.broadcasted_iota(I32, w.shape, 0)
        w = jnp.where(row < live, w, jnp.zeros_like(w))
        lo, hi = _unpack_bf16_pairs(w)
        x = jnp.concatenate([lo.astype(BF16), hi.astype(BF16)], axis=1)
        for c in range(nc):
            c0 = slice(c * ch, (c + 1) * ch)
            c1 = slice(f + c * ch, f + (c + 1) * ch)
            glu = _dot(x, wgu_bf[c]) + bgu_ref[0, 0, :, c0]
            lin = _dot(x, wgu_bf[nc + c]) + bgu_ref[0, 0, :, c1]
            glu = jnp.minimum(glu, SWIGLU_LIMIT)
            lin = jnp.clip(lin, -SWIGLU_LIMIT, SWIGLU_LIMIT)
            act_ref[:nr, c0] = (glu * jax.nn.sigmoid(SWIGLU_ALPHA * glu) * (lin + 1.0)).astype(BF16)
        a = act_ref[:nr, :]
        y = jnp.concatenate([_dot(a, wd_bf[n]) for n in range(wd_bf.shape[0])], axis=1)
        o_ref[:nr, :] = _pack_bf16_pairs(y + bd_ref[0, 0])

    @pl.when(live > hb)
    def _():
        run(bm)

    @pl.when(jnp.logical_and(used, live <= hb))
    def _():
        run(hb)
        o_ref[hb:, :] = jnp.zeros((bm - hb, o_ref.shape[1]), o_ref.dtype)

    @pl.when(jnp.logical_not(used))
    def _():
        o_ref[...] = jnp.zeros_like(o_ref)


def _expert_call(start, nblk, cnt, buf, w_gu, b_gu, w_down, b_down, layer, bm):
    n_slots, dw = buf.shape
    _, ne, d, f2 = w_gu.shape
    f = f2 // 2
    n_blocks = n_slots // bm
    b0 = jnp.arange(n_blocks, dtype=I32) * bm
    end = start + nblk * bm
    block_e = jnp.minimum(jnp.sum((end[None, :] <= b0[:, None]).astype(I32), axis=1), ne - 1)
    n_live = jnp.clip(start[block_e] + cnt[block_e] - b0, 0, bm)
    eid = jnp.arange(ne, dtype=I32)
    has = cnt > 0
    later = jnp.logical_and(has[None, :], eid[None, :] > eid[:, None])
    nxt_e = jnp.min(jnp.where(later, eid[None, :], ne), axis=1)
    nxt_e = jnp.where(nxt_e < ne, nxt_e, -1).astype(I32)
    slot_e = ((jnp.cumsum(has.astype(I32)) - 1) % 2).astype(I32)
    lw = lambda i, be, nv, nx, sl: (layer, be[i], 0, 0)
    row = lambda i, be, nv, nx, sl: (i, 0)
    cw = 256
    return pl.pallas_call(
        functools.partial(_expert_kernel, layer=layer),
        grid_spec=pltpu.PrefetchScalarGridSpec(
            num_scalar_prefetch=4,
            grid=(n_blocks,),
            in_specs=[pl.BlockSpec((bm, dw), row),
                      pl.BlockSpec(memory_space=pl.ANY),
                      pl.BlockSpec((1, 1, 1, f2), lw),
                      pl.BlockSpec(memory_space=pl.ANY),
                      pl.BlockSpec((1, 1, 1, d), lw)],
            out_specs=pl.BlockSpec((bm, dw), row),
            scratch_shapes=[pltpu.VMEM((2, d, f2), F32), pltpu.VMEM((2, f, d), F32),
                            pltpu.VMEM((f2 // cw, d, cw), BF16), pltpu.VMEM((d // cw, f, cw), BF16),
                            pltpu.VMEM((bm, f), BF16),
                            pltpu.SemaphoreType.DMA((2,)), pltpu.SemaphoreType.DMA((2,))]),
        out_shape=jax.ShapeDtypeStruct((n_slots, dw), jnp.uint32),
        compiler_params=_cparams(("arbitrary",)),
        name="moe_experts",
    )(block_e, n_live, nxt_e[block_e], slot_e[block_e], buf, w_gu, b_gu.reshape(b_gu.shape[0], ne, 1, f2),
      w_down, b_down.reshape(b_down.shape[0], ne, 1, d))


def _expert_kernel_ring(start_ref, nblk_ref, cnt_ref, x_hbm, wgu_ref, bgu_ref, wd_ref, bd_ref, y_hbm,
                        xbuf, ybuf, wgu_bf, wd_bf, act_ref, sem_in, sem_out, *, n_blocks, fused=False):
    e = pl.program_id(0)
    bm = xbuf.shape[1]
    f = act_ref.shape[1]
    ch = 256
    nb = nblk_ref[e]
    start = start_ref[e]
    cnt = cnt_ref[e]

    def rows(j):
        return pl.ds(pl.multiple_of(start + j * bm, bm), bm)

    def fetch(j, slot):
        return pltpu.make_async_copy(x_hbm.at[rows(j)], xbuf.at[slot], sem_in.at[slot])

    def put(j, slot):
        return pltpu.make_async_copy(ybuf.at[slot], y_hbm.at[rows(j)], sem_out.at[slot])

    @pl.when(nb > 0)
    def _():
        fetch(0, 0).start()
        if wgu_bf.ndim == 3:
            for j in range(wgu_bf.shape[0]):
                wgu_bf[j] = wgu_ref[0, 0, :, j * ch:(j + 1) * ch].astype(BF16)
            for j in range(wd_bf.shape[0]):
                wd_bf[j] = wd_ref[0, 0, :, j * ch:(j + 1) * ch].astype(BF16)
        else:
            for j in range(wgu_bf.shape[1] // ch):
                cs = slice(j * ch, (j + 1) * ch)
                wgu_bf[:, cs] = wgu_ref[0, 0, :, cs].astype(BF16)
            for j in range(wd_bf.shape[1] // ch):
                cs = slice(j * ch, (j + 1) * ch)
                wd_bf[:, cs] = wd_ref[0, 0, :, cs].astype(BF16)

    def block(j, carry):
        slot = lax.rem(j, 2)

        @pl.when(j + 1 < nb)
        def _():
            fetch(j + 1, 1 - slot).start()

        fetch(j, slot).wait()

        @pl.when(j >= 2)
        def _():
            put(j - 2, slot).wait()

        w = xbuf[slot]
        row = lax.broadcasted_iota(I32, w.shape, 0)
        w = jnp.where(row < cnt - j * bm, w, jnp.zeros_like(w))
        lo, hi = _unpack_bf16_pairs(w)
        x = jnp.concatenate([lo.astype(BF16), hi.astype(BF16)], axis=1)
        if fused:
            z = _dot(x, wgu_bf[...]) + bgu_ref[0, 0]
            glu = jnp.minimum(z[:, :f], SWIGLU_LIMIT)
            lin = jnp.clip(z[:, f:], -SWIGLU_LIMIT, SWIGLU_LIMIT)
            act = (glu * jax.nn.sigmoid(SWIGLU_ALPHA * glu) * (lin + 1.0)).astype(BF16)
            ybuf[slot] = _pack_bf16_pairs(_dot(act, wd_bf[...]) + bd_ref[0, 0])
            put(j, slot).start()
            return carry
        if wgu_bf.ndim == 3:
            nc = f // ch
            for c in range(nc):
                c0 = slice(c * ch, (c + 1) * ch)
                c1 = slice(f + c * ch, f + (c + 1) * ch)
                glu = _dot(x, wgu_bf[c]) + bgu_ref[0, 0, :, c0]
                lin = _dot(x, wgu_bf[nc + c]) + bgu_ref[0, 0, :, c1]
                glu = jnp.minimum(glu, SWIGLU_LIMIT)
                lin = jnp.clip(lin, -SWIGLU_LIMIT, SWIGLU_LIMIT)
                act_ref[:, c0] = (glu * jax.nn.sigmoid(SWIGLU_ALPHA * glu) * (lin + 1.0)).astype(BF16)
            a = act_ref[...]
            y = jnp.concatenate([_dot(a, wd_bf[n]) for n in range(wd_bf.shape[0])], axis=1)
            ybuf[slot] = _pack_bf16_pairs(y + bd_ref[0, 0])
            put(j, slot).start()
            return carry
        for c in range(f // ch):
            c0 = slice(c * ch, (c + 1) * ch)
            c1 = slice(f + c * ch, f + (c + 1) * ch)
            glu = _dot(x, wgu_bf[:, c0]) + bgu_ref[0, 0, :, c0]
            lin = _dot(x, wgu_bf[:, c1]) + bgu_ref[0, 0, :, c1]
            glu = jnp.minimum(glu, SWIGLU_LIMIT)
            lin = jnp.clip(lin, -SWIGLU_LIMIT, SWIGLU_LIMIT)
            act_ref[:, c0] = (glu * jax.nn.sigmoid(SWIGLU_ALPHA * glu) * (lin + 1.0)).astype(BF16)
        ybuf[slot] = _pack_bf16_pairs(_dot(act_ref[...], wd_bf[...]) + bd_ref[0, 0])
        put(j, slot).start()
        return carry

    lax.fori_loop(0, nb, block, 0)

    @pl.when(nb >= 2)
    def _():
        put(nb - 2, lax.rem(nb, 2)).wait()

    @pl.when(nb >= 1)
    def _():
        put(nb - 1, lax.rem(nb + 1, 2)).wait()

    @pl.when(e == pl.num_programs(0) - 1)
    def _():
        ybuf[0] = jnp.zeros(ybuf.shape[1:], ybuf.dtype)

        def fill(j, carry):
            cp = pltpu.make_async_copy(ybuf.at[0], y_hbm.at[pl.ds(pl.multiple_of(j * bm, bm), bm)], sem_out.at[0])
            cp.start()
            cp.wait()
            return carry

        lax.fori_loop(start // bm + nb, n_blocks, fill, 0)


def _expert_call_ring(start, nblk, cnt, buf, w_gu, b_gu, w_down, b_down, layer, bm, fused=False, chunked=False):
    n_slots, dw = buf.shape
    _, ne, d, f2 = w_gu.shape
    f = f2 // 2
    lw = lambda e, *_: (layer, e, 0, 0)
    return pl.pallas_call(
        functools.partial(_expert_kernel_ring, n_blocks=n_slots // bm, fused=fused),
        grid_spec=pltpu.PrefetchScalarGridSpec(
            num_scalar_prefetch=3,
            grid=(ne,),
            in_specs=[pl.BlockSpec(memory_space=pl.ANY),
                      pl.BlockSpec((1, 1, d, f2), lw),
                      pl.BlockSpec((1, 1, 1, f2), lw),
                      pl.BlockSpec((1, 1, f, d), lw),
                      pl.BlockSpec((1, 1, 1, d), lw)],
            out_specs=pl.BlockSpec(memory_space=pl.ANY),
            scratch_shapes=[pltpu.VMEM((2, bm, dw), jnp.uint32), pltpu.VMEM((2, bm, dw), jnp.uint32),
                            pltpu.VMEM((f2 // 256, d, 256) if chunked else (d, f2), BF16),
                            pltpu.VMEM((d // 256, f, 256) if chunked else (f, d), BF16), pltpu.VMEM((bm, f), BF16),
                            pltpu.SemaphoreType.DMA((2,)), pltpu.SemaphoreType.DMA((2,))]),
        out_shape=jax.ShapeDtypeStruct((n_slots, dw), jnp.uint32),
        compiler_params=_cparams(("arbitrary",)),
        name="moe_experts",
    )(start, nblk, cnt, buf, w_gu, b_gu.reshape(b_gu.shape[0], ne, 1, f2),
      w_down, b_down.reshape(b_down.shape[0], ne, 1, d))


def _combine_kernel(x_ref, m_ref, y_ref, gate_ref, gout_ref, o_ref, *, final):
    m = m_ref[0]
    gate = gate_ref[...]
    half = y_ref.shape[2]
    f_lo, f_hi = None, None
    for k in range(TOP_K):
        lo, hi = _unpack_bf16_pairs(y_ref[k])
        gk = gate[:, k:k + 1]
        f_lo = gk * lo if f_lo is None else f_lo + gk * lo
        f_hi = gk * hi if f_hi is None else f_hi + gk * hi
    x_lo = x_ref[:, :half] + m[5:6, :half] * f_lo
    x_hi = x_ref[:, half:] + m[5:6, half:] * f_hi
    if final:
        ms = (jnp.sum(x_lo * x_lo, axis=-1, keepdims=True) + jnp.sum(x_hi * x_hi, axis=-1, keepdims=True))
        r = lax.rsqrt(ms * (0.5 / half) + NORM_EPS)
        x_lo = x_lo * r * gout_ref[:, :half]
        x_hi = x_hi * r * gout_ref[:, half:]
    o_ref[:, :half] = x_lo
    o_ref[:, half:] = x_hi


def _moe_layer(xs, mods, g, layer, w_router, b_router, w_gu, b_gu, w_down, b_down, geo, g_out, final):
    n, d = xs.shape
    ne = w_router.shape[1]
    f = w_down.shape[2]
    n_tiles = n // TM
    grp = functools.partial(_tile_group, geo=geo)
    const2 = lambda t: (0, 0)
    h, idx, gate, rank, cnt = pl.pallas_call(
        _route_kernel,
        grid=(n_tiles,),
        in_specs=[pl.BlockSpec((TM, d), lambda t: (t, 0)),
                  pl.BlockSpec((1, 6, d), lambda t: (grp(t), 0, 0)),
                  pl.BlockSpec((1, d), const2),
                  pl.BlockSpec((ne, d), const2),
                  pl.BlockSpec((ne, 1), const2)],
        out_specs=[pl.BlockSpec((TM, d // 2), lambda t: (t, 0)),
                   pl.BlockSpec((TOP_K, TM), lambda t: (0, t)),
                   pl.BlockSpec((TOP_K, TM), lambda t: (0, t)),
                   pl.BlockSpec((TOP_K, TM), lambda t: (0, t)),
                   pl.BlockSpec((ne, 1), const2)],
        out_shape=[jax.ShapeDtypeStruct((n, d // 2), jnp.uint32),
                   jax.ShapeDtypeStruct((TOP_K, n), I32),
                   jax.ShapeDtypeStruct((TOP_K, n), F32),
                   jax.ShapeDtypeStruct((TOP_K, n), I32),
                   jax.ShapeDtypeStruct((ne, 1), F32)],
        scratch_shapes=[pltpu.VMEM((ne, 1), F32)],
        compiler_params=_cparams(("arbitrary",)),
        name="moe_route",
    )(xs, mods, g.reshape(1, d), w_router.T.astype(BF16), b_router.reshape(ne, 1))

    bm = MOE_BM
    n_rows = n * TOP_K
    n_blocks = -(-n_rows // bm) + ne
    n_slots = n_blocks * bm
    dest, info = pl.pallas_call(
        functools.partial(_slot_kernel, bm=bm),
        grid=(n // SLOT_T,),
        in_specs=[pl.BlockSpec((TOP_K, SLOT_T), lambda t: (0, t)),
                  pl.BlockSpec((TOP_K, SLOT_T), lambda t: (0, t)),
                  pl.BlockSpec((ne, 1), const2)],
        out_specs=[pl.BlockSpec((TOP_K, SLOT_T), lambda t: (0, t)),
                   pl.BlockSpec((ne, 128), const2)],
        out_shape=[jax.ShapeDtypeStruct((TOP_K, n), I32),
                   jax.ShapeDtypeStruct((ne, 128), I32)],
        compiler_params=_cparams(("arbitrary",)),
        name="moe_slots",
    )(idx, rank, cnt)

    buf = _sc_scatter_rows(h, dest, n_slots)
    y = _expert_call(info[:, 0], info[:, 1], info[:, 2], buf, w_gu, b_gu, w_down, b_down, layer, bm)
    dy = d // 2
    yk = _sc_gather_rows(y, dest.reshape(-1)).reshape(TOP_K, n, dy)
    return pl.pallas_call(
        functools.partial(_combine_kernel, final=final),
        grid=(n_tiles,),
        in_specs=[pl.BlockSpec((TM, d), lambda t: (t, 0)),
                  pl.BlockSpec((1, 6, d), lambda t: (grp(t), 0, 0)),
                  pl.BlockSpec((TOP_K, TM, dy), lambda t: (0, t, 0)),
                  pl.BlockSpec((TM, TOP_K), lambda t: (t, 0)),
                  pl.BlockSpec((1, d), const2)],
        out_specs=pl.BlockSpec((TM, d), lambda t: (t, 0)),
        out_shape=jax.ShapeDtypeStruct((n, d), F32),
        compiler_params=_cparams(("parallel",)),
        name="moe_combine",
    )(xs, mods, yk, gate.T, g_out.reshape(1, d))


def kernel(x, c, ctx, c_ctx, ada_w, ada_b, norm_mix, norm_ffn, norm_out, a_w_in, a_g_v, a_w_s, a_b_s, a_w_out, b_w_in, b_w_a2, b_b_a, b_g_o, b_w_out, c_w_qkv, c_rpb, c_w_out, moe_w_router, moe_b_router, moe_w_gu, moe_b_gu, moe_w_down, moe_b_down):
    b, seq, d = x.shape
    lctx = ctx.shape[1]
    depth = ada_w.shape[0]
    assert (b * lctx) % TM == 0 and seq % TM == 0 and lctx % GLA_TB == 0 and seq % GLA_TB == 0
    assert seq % (NA_QROWS * GRID_W) == 0 and (b * lctx) % (NA_QROWS * GRID_W) == 0 and lctx % 256 == 0
    assert seq // GRID_W >= NA_WROWS + NA_QROWS and b + 1 <= 8
    geo = (b * lctx // TM, seq // TM, b)
    dims = (b, seq, lctx)

    cond = jnp.zeros((8, d), F32).at[:b].set(c).at[b].set(c_ctx)
    mods_all = _adaln(cond, ada_w, ada_b)[:, :b + 1].reshape(depth, b + 1, 6, d)

    xs = jnp.concatenate([ctx.reshape(b * lctx, d), x.reshape(b * seq, d)], axis=0)
    ctx_tiles = geo[0]
    has_ctx = True
    for i in range(depth):
        kind, j = i % N_MIXERS, i // N_MIXERS
        ctx_later = any(kk % N_MIXERS != 0 for kk in range(i + 1, depth))
        mods = mods_all[i]
        if kind == 0:
            keep_ctx = has_ctx and ctx_later
            skip = 0 if keep_ctx or not has_ctx else ctx_tiles
            geo_i = geo if has_ctx else (0, geo[1], b)
            xs = _gmlp_mixer(xs, mods, norm_mix[i], a_w_in[j], a_g_v[j], a_w_s[j], a_b_s[j], a_w_out[j],
                             skip, xs.shape[0] // TM - skip, geo_i)
            has_ctx = keep_ctx
        elif kind == 1:
            assert has_ctx
            xs = _gla_mixer(xs, mods, norm_mix[i], b_w_in[j], b_w_a2[j], b_b_a[j], b_g_o[j], b_w_out[j], geo, dims)
            if not ctx_later:
                xs = xs[b * lctx:]
                has_ctx = False
        else:
            assert has_ctx
            if ctx_later:
                raise NotImplementedError("context output of the neighbourhood mixer")
            xs = _na_mixer(xs, mods, norm_mix[i], c_w_qkv[j], c_rpb[j], c_w_out[j], geo, dims)
            has_ctx = False
        geo_i = geo if has_ctx else (0, geo[1], b)
        xs = _moe_layer(xs, mods, norm_ffn[i], i, moe_w_router[i], moe_b_router[i], moe_w_gu, moe_b_gu,
                        moe_w_down, moe_b_down, geo_i, norm_out, i == depth - 1)
    if has_ctx:
        xs = xs[b * lctx:]
    return xs.reshape(b, seq, d)
```

```python
import functools

import numpy as np
import jax
import jax.numpy as jnp
from jax import lax
from jax.experimental import pallas as pl
from jax.experimental.pallas import tpu as pltpu
from jax.experimental.pallas import tpu_sc as plsc

F32 = jnp.float32
BF16 = jnp.bfloat16
I32 = jnp.int32

NORM_EPS = 1e-6
GRID_W = 64
N_MIXERS = 3

CHUNK_A = 128
A_GROUPS = 8
GLA_HEADS = 4
GLA_RANK = 16
GLA_TAU = 16.0
GLA_CHUNK = 64
ROPE_BASE = 10000.0
NA_HEADS = 16
NA_KH = 8
NA_KW = 16
NEG_INF = -1e30
NA_QROWS = 4
NA_WROWS = 12
NA_PAIRS_PER_STEP = 2
TOP_K = 4
SWIGLU_LIMIT = 7.0
SWIGLU_ALPHA = 1.702
MOE_BM = 512

TM = 512
GLA_TB = 256
VMEM_LIMIT = 56 * 1024 * 1024


def _cparams(sem):
    return pltpu.CompilerParams(dimension_semantics=sem, vmem_limit_bytes=VMEM_LIMIT)


def _dot(a, b):
    return jnp.dot(a, b, preferred_element_type=F32)


def _dot_nt(a, b):
    return lax.dot_general(a, b, (((1,), (1,)), ((), ())), preferred_element_type=F32)


def _dot_tn(a, b):
    return lax.dot_general(a, b, (((0,), (0,)), ((), ())), preferred_element_type=F32)


def _rms(x, g):
    return x * lax.rsqrt(jnp.mean(x * x, axis=-1, keepdims=True) + NORM_EPS) * g


def _modulate(x, g, shift, scale):
    return _rms(x, g) * (1.0 + scale) + shift


def _ada_kernel(s_ref, w_ref, b_ref, o_ref):
    s = s_ref[...]
    s = s * jax.nn.sigmoid(s)
    o_ref[0] = jnp.dot(s, w_ref[0], preferred_element_type=F32,
                       precision=lax.Precision.HIGHEST) + b_ref[0]


def _adaln(cond, ada_w, ada_b):
    depth, d, n6 = ada_w.shape
    bn = n6 // 4
    return pl.pallas_call(
        _ada_kernel,
        grid=(depth, n6 // bn),
        in_specs=[pl.BlockSpec((8, d), lambda i, j: (0, 0)),
                  pl.BlockSpec((1, d, bn), lambda i, j: (i, 0, j)),
                  pl.BlockSpec((1, 1, bn), lambda i, j: (i, 0, j))],
        out_specs=pl.BlockSpec((1, 8, bn), lambda i, j: (i, 0, j)),
        out_shape=jax.ShapeDtypeStruct((depth, 8, n6), F32),
        compiler_params=_cparams(("parallel", "parallel")),
        name="adaln",
    )(cond, ada_w, ada_b.reshape(depth, 1, n6))


def _gelu(z):
    return 0.5 * z * (1.0 + lax.erf(z * np.float32(np.sqrt(0.5))))


def _gmlp_kernel(x_ref, m_ref, g_ref, win_ref, gv_ref, ws_ref, bs_ref, wout_ref, o_ref):
    x = x_ref[...]
    m = m_ref[0]
    a = gv_ref.shape[1]
    h = _modulate(x, g_ref[...], m[0:1], m[1:2]).astype(BF16)
    z = _gelu(_dot(h, win_ref[...]))
    u = z[:, :a]
    v = _rms(z[:, a:], gv_ref[...]).astype(BF16)
    gw = a // A_GROUPS
    rows = []
    for c in range(x.shape[0] // CHUNK_A):
        cols = [_dot(ws_ref[g], v[c * CHUNK_A:(c + 1) * CHUNK_A, g * gw:(g + 1) * gw])
                for g in range(A_GROUPS)]
        rows.append(jnp.concatenate(cols, axis=1) + bs_ref[...])
    s = jnp.concatenate(rows, axis=0)
    y = _dot((u * s).astype(BF16), wout_ref[...])
    o_ref[...] = x + m[2:3] * y


def _gmlp_mixer(xs, mods, g, w_in, g_v, w_s, b_s, w_out, tile0, n_tiles, geo):
    d = xs.shape[1]
    a = g_v.shape[0]
    gw = a // A_GROUPS
    bias = jnp.repeat(b_s.T, gw, axis=1)
    grp = functools.partial(_tile_group, geo=geo, tile0=tile0)
    const2 = lambda t: (0, 0)
    return pl.pallas_call(
        _gmlp_kernel,
        grid=(n_tiles,),
        in_specs=[pl.BlockSpec((TM, d), lambda t: (t + tile0, 0)),
                  pl.BlockSpec((1, 6, d), lambda t: (grp(t), 0, 0)),
                  pl.BlockSpec((1, d), const2),
                  pl.BlockSpec((d, 2 * a), const2),
                  pl.BlockSpec((1, a), const2),
                  pl.BlockSpec((A_GROUPS, CHUNK_A, CHUNK_A), lambda t: (0, 0, 0)),
                  pl.BlockSpec((CHUNK_A, a), const2),
                  pl.BlockSpec((a, d), const2)],
        out_specs=pl.BlockSpec((TM, d), lambda t: (t, 0)),
        out_shape=jax.ShapeDtypeStruct((n_tiles * TM, d), F32),
        compiler_params=_cparams(("parallel",)),
        name="gmlp_mixer",
    )(xs, mods, g.reshape(1, d), w_in.astype(BF16), g_v.reshape(1, a), w_s.astype(BF16),
      bias, w_out.astype(BF16))


def _tile_group(t, geo, tile0=0):
    n_ctx_tiles, tiles_per_batch, nb = geo
    tt = t + tile0
    return jnp.where(tt < n_ctx_tiles, nb, (tt - n_ctx_tiles) // tiles_per_batch)


def _dot_f32(tri_bf16, x):
    hi = x.astype(BF16)
    r1 = x - hi.astype(F32)
    mid = r1.astype(BF16)
    lo = (r1 - mid.astype(F32)).astype(BF16)
    return _dot(tri_bf16, hi) + _dot(tri_bf16, mid) + _dot(tri_bf16, lo)


def _gla_proj_kernel(x_ref, m_ref, g_ref, wq_ref, wk_ref, wv_ref, wg_ref, wr_ref, wa_ref, ba_ref,
                     cos_ref, sin_ref, q_ref, k_ref, v_ref, go_ref, la_ref):
    x = x_ref[...]
    m = m_ref[0]
    h = _modulate(x, g_ref[...], m[0:1], m[1:2]).astype(BF16)
    kdim = wq_ref.shape[1]
    hk = kdim // GLA_HEADS
    q = _dot(h, wq_ref[...]) * np.float32(hk ** -0.5)
    k = _dot(h, wk_ref[...])
    v_ref[...] = _dot(h, wv_ref[...]).astype(BF16)
    go_ref[...] = _dot(h, wg_ref[...])
    r = _dot(h, wr_ref[...])
    z = _dot(r.astype(BF16), wa_ref[...]) + ba_ref[...]
    la_ref[...] = jax.nn.log_sigmoid(z) * np.float32(1.0 / GLA_TAU)

    cos = jnp.concatenate([cos_ref[...]] * GLA_HEADS, axis=1)
    sin = jnp.concatenate([sin_ref[...]] * GLA_HEADS, axis=1)
    nf = hk // 4
    lane = lax.broadcasted_iota(I32, q.shape, 1)
    first = (lane % (2 * nf)) < nf

    def rope(t):
        up = pltpu.roll(t, kdim - nf, 1)
        dn = pltpu.roll(t, nf, 1)
        return t * cos + jnp.where(first, up, dn) * sin

    q_ref[...] = rope(q)
    k_ref[...] = rope(k)


def _gla_chunk(q_ref, k_ref, v_ref, la_ref, o_ref, st_ref, n, rev):
    kdim = q_ref.shape[1]
    hk = kdim // GLA_HEADS
    hv = v_ref.shape[1] // GLA_HEADS
    c = GLA_CHUNK
    ri = lax.broadcasted_iota(I32, (c, c), 0)
    ci = lax.broadcasted_iota(I32, (c, c), 1)
    keep = (ci >= ri) if rev else (ci <= ri)
    tri = keep.astype(BF16)
    rows = slice(n * c, (n + 1) * c)
    cum = _dot_f32(tri, la_ref[rows, :])
    last = cum[0:1] if rev else cum[c - 1:c]
    q = q_ref[rows, :]
    k = k_ref[rows, :]
    q_dec = (q * jnp.exp(cum)).astype(BF16)
    k_inv = (k * jnp.exp(-cum)).astype(BF16)
    k_end = (k * jnp.exp(last - cum)).astype(BF16)
    dec = jnp.exp(last)
    for h in range(GLA_HEADS):
        ks = slice(h * hk, (h + 1) * hk)
        vs = slice(h * hv, (h + 1) * hv)
        att = jnp.where(keep, _dot_nt(q_dec[:, ks], k_inv[:, ks]), 0.0).astype(BF16)
        vh = v_ref[rows, vs]
        st = st_ref[h]
        o_ref[rows, vs] = _dot(att, vh) + _dot_nt(q_dec[:, ks], st.astype(BF16))
        st_ref[h] = st * dec[:, ks] + _dot_tn(vh, k_end[:, ks])


def _gla_scan_kernel(qf_ref, kf_ref, vf_ref, laf_ref, qb_ref, kb_ref, vb_ref, lab_ref,
                     of_ref, ob_ref, stf_ref, stb_ref):
    s = pl.program_id(1)

    @pl.when(s == 0)
    def _():
        stf_ref[...] = jnp.zeros_like(stf_ref)
        stb_ref[...] = jnp.zeros_like(stb_ref)

    nch = qf_ref.shape[0] // GLA_CHUNK
    for n in range(nch):
        _gla_chunk(qf_ref, kf_ref, vf_ref, laf_ref, of_ref, stf_ref, n, False)
        _gla_chunk(qb_ref, kb_ref, vb_ref, lab_ref, ob_ref, stb_ref, nch - 1 - n, True)


def _gla_out_kernel(x_ref, m_ref, of_ref, ob_ref, go_ref, gn_ref, wout_ref, o_ref):
    x = x_ref[...]
    m = m_ref[0]
    o = of_ref[...] + ob_ref[...]
    hv = gn_ref.shape[1]
    parts = [_rms(o[:, h * hv:(h + 1) * hv], gn_ref[...]) for h in range(GLA_HEADS)]
    o = jnp.concatenate(parts, axis=1)
    gate = go_ref[...]
    y = _dot((o * (gate * jax.nn.sigmoid(gate))).astype(BF16), wout_ref[...])
    o_ref[...] = x + m[2:3] * y


def _gla_mixer(xs, mods, g, w_in, w_a2, b_a, g_o, w_out, geo, dims):
    b, seq, lctx = dims
    nt, d = xs.shape
    n_tiles = nt // TM
    kdim = w_a2.shape[2]
    vdim = g_o.shape[0] * GLA_HEADS
    hk = kdim // GLA_HEADS
    nf = hk // 4
    wq = w_in[:, :kdim].astype(BF16)
    wk = w_in[:, kdim:2 * kdim].astype(BF16)
    wv = w_in[:, 2 * kdim:2 * kdim + vdim].astype(BF16)
    wg = w_in[:, 2 * kdim + vdim:2 * kdim + 2 * vdim].astype(BF16)
    wr = jnp.pad(w_in[:, 2 * kdim + 2 * vdim:], ((0, 0), (0, 128 - 2 * GLA_RANK))).astype(BF16)
    wa = jnp.zeros((128, 2 * kdim), F32)
    wa = wa.at[:GLA_RANK, :kdim].set(w_a2[0]).at[GLA_RANK:2 * GLA_RANK, kdim:].set(w_a2[1]).astype(BF16)
    ba = b_a.reshape(1, 2 * kdim)

    tpos = jnp.arange(seq)
    freqs = jnp.power(ROPE_BASE, -jnp.arange(nf, dtype=F32) / nf)
    ar = (tpos // GRID_W).astype(F32)[:, None] * freqs
    ac = (tpos % GRID_W).astype(F32)[:, None] * freqs
    cos = jnp.concatenate([jnp.cos(ar), jnp.cos(ar), jnp.cos(ac), jnp.cos(ac)], axis=1)
    sin = jnp.concatenate([-jnp.sin(ar), jnp.sin(ar), -jnp.sin(ac), jnp.sin(ac)], axis=1)
    cos = jnp.concatenate([jnp.ones((TM, hk), F32), cos], axis=0)
    sin = jnp.concatenate([jnp.zeros((TM, hk), F32), sin], axis=0)

    n_ctx_tiles, tiles_per_batch, _ = geo
    grp = functools.partial(_tile_group, geo=geo)

    def rope_blk(t):
        return (jnp.where(t < n_ctx_tiles, 0, 1 + (t - n_ctx_tiles) % tiles_per_batch), 0)

    const2 = lambda t: (0, 0)
    row = lambda t: (t, 0)
    q, k, v, go, la = pl.pallas_call(
        _gla_proj_kernel,
        grid=(n_tiles,),
        in_specs=[pl.BlockSpec((TM, d), row),
                  pl.BlockSpec((1, 6, d), lambda t: (grp(t), 0, 0)),
                  pl.BlockSpec((1, d), const2),
                  pl.BlockSpec((d, kdim), const2), pl.BlockSpec((d, kdim), const2),
                  pl.BlockSpec((d, vdim), const2), pl.BlockSpec((d, vdim), const2),
                  pl.BlockSpec((d, 128), const2), pl.BlockSpec((128, 2 * kdim), const2),
                  pl.BlockSpec((1, 2 * kdim), const2),
                  pl.BlockSpec((TM, hk), rope_blk), pl.BlockSpec((TM, hk), rope_blk)],
        out_specs=[pl.BlockSpec((TM, kdim), row), pl.BlockSpec((TM, kdim), row),
                   pl.BlockSpec((TM, vdim), row), pl.BlockSpec((TM, vdim), row),
                   pl.BlockSpec((TM, 2 * kdim), row)],
        out_shape=[jax.ShapeDtypeStruct((nt, kdim), F32), jax.ShapeDtypeStruct((nt, kdim), F32),
                   jax.ShapeDtypeStruct((nt, vdim), BF16), jax.ShapeDtypeStruct((nt, vdim), F32),
                   jax.ShapeDtypeStruct((nt, 2 * kdim), F32)],
        compiler_params=_cparams(("parallel",)),
        name="gla_proj",
    )(xs, mods, g.reshape(1, d), wq, wk, wv, wg, wr, wa, ba, cos, sin)

    tb = GLA_TB
    ctx_steps = lctx // tb
    lat_steps = seq // tb
    steps = ctx_steps + lat_steps

    def blk(bi, s, rev):
        if rev:
            cs, ls = ctx_steps - 1 - s, lat_steps - 1 - (s - ctx_steps)
        else:
            cs, ls = s, s - ctx_steps
        return jnp.where(s < ctx_steps, bi * ctx_steps + cs, b * ctx_steps + bi * lat_steps + ls)

    def dir_specs(rev):
        row_blk = lambda bi, s: (blk(bi, s, rev), 0)
        return [pl.BlockSpec((tb, kdim), row_blk), pl.BlockSpec((tb, kdim), row_blk),
                pl.BlockSpec((tb, vdim), row_blk),
                pl.BlockSpec((tb, kdim), lambda bi, s: (blk(bi, s, rev), 1 if rev else 0))]

    state = pltpu.VMEM((GLA_HEADS, vdim // GLA_HEADS, hk), F32)
    o_f, o_b = pl.pallas_call(
        _gla_scan_kernel,
        grid=(b, steps),
        in_specs=dir_specs(False) + dir_specs(True),
        out_specs=[pl.BlockSpec((tb, vdim), lambda bi, s: (blk(bi, s, False), 0)),
                   pl.BlockSpec((tb, vdim), lambda bi, s: (blk(bi, s, True), 0))],
        out_shape=[jax.ShapeDtypeStruct((nt, vdim), F32)] * 2,
        scratch_shapes=[state, state],
        compiler_params=_cparams(("parallel", "arbitrary")),
        name="gla_scan",
    )(q, k, v, la, q, k, v, la)

    return pl.pallas_call(
        _gla_out_kernel,
        grid=(n_tiles,),
        in_specs=[pl.BlockSpec((TM, d), row),
                  pl.BlockSpec((1, 6, d), lambda t: (grp(t), 0, 0)),
                  pl.BlockSpec((TM, vdim), row), pl.BlockSpec((TM, vdim), row),
                  pl.BlockSpec((TM, vdim), row),
                  pl.BlockSpec((1, vdim // GLA_HEADS), const2),
                  pl.BlockSpec((vdim, d), const2)],
        out_specs=pl.BlockSpec((TM, d), row),
        out_shape=jax.ShapeDtypeStruct((nt, d), F32),
        compiler_params=_cparams(("parallel",)),
        name="gla_out",
    )(xs, mods, o_f, o_b, go, g_o.reshape(1, -1), w_out.astype(BF16))


def _na_proj_kernel(x_ref, m_ref, g_ref, w_ref, q_ref, k_ref, v_ref):
    x = x_ref[...]
    m = m_ref[0]
    d = x.shape[1]
    h = _modulate(x, g_ref[...], m[0:1], m[1:2]).astype(BF16)
    qkv = _dot(h, w_ref[...])
    hd = d // NA_HEADS
    for p in range(q_ref.shape[0]):
        cs = slice(p * 128, (p + 1) * 128)
        q_ref[p] = (qkv[:, cs] * np.float32(hd ** -0.5)).astype(BF16)
        k_ref[p] = qkv[:, d + p * 128:d + (p + 1) * 128].astype(BF16)
        v_ref[p] = qkv[:, 2 * d + p * 128:2 * d + (p + 1) * 128].astype(BF16)


def _na_attn_kernel(tbl_ref, q_ref, *refs, n_tiles, nwin):
    k_refs, v_refs = refs[:nwin], refs[nwin:2 * nwin]
    kc_ref, vc_ref, bias_ref, o_ref, s_ref = refs[2 * nwin:]
    t = pl.program_id(2)
    typ = jnp.where(t == 0, 0, jnp.where(t == n_tiles - 1, 2, 1))
    npair = NA_WROWS // 2
    lane = lax.broadcasted_iota(I32, q_ref.shape[1:], 1)
    outs = []
    for pp in range(q_ref.shape[0]):
        q = q_ref[pp]
        kw = jnp.concatenate([r[pp] for r in k_refs], axis=0)
        vw = jnp.concatenate([r[pp] for r in v_refs], axis=0)
        kc = kc_ref[pp]
        vc = vc_ref[pp]
        heads = []
        for hh in range(2):
            sel = (lane < 64) if hh == 0 else (lane >= 64)
            qm = jnp.where(sel, q, jnp.zeros_like(q))
            s = _dot_nt(qm, kw)
            s_ctx = _dot_nt(qm, kc)
            sl = s_ref.at[2 * pp + hh]
            for dr in range(NA_QROWS):
                for j in range(npair):
                    e = tbl_ref[typ * (NA_QROWS * npair) + dr * npair + j]
                    rs = slice(dr * GRID_W, (dr + 1) * GRID_W)
                    cs = slice(j * 128, (j + 1) * 128)
                    sl[rs, cs] = s[rs, cs] + bias_ref[2 * pp + hh, e]
            s_lat = sl[...]
            mx = jnp.maximum(jnp.max(s_lat, axis=1, keepdims=True), jnp.max(s_ctx, axis=1, keepdims=True))
            e_lat = jnp.exp(s_lat - mx)
            e_ctx = jnp.exp(s_ctx - mx)
            inv = 1.0 / (jnp.sum(e_lat, axis=1, keepdims=True) + jnp.sum(e_ctx, axis=1, keepdims=True))
            heads.append(_dot((e_lat * inv).astype(BF16), vw) + _dot((e_ctx * inv).astype(BF16), vc))
        outs.append(jnp.where(lane < 64, heads[0], heads[1]))
    o_ref[...] = jnp.concatenate(outs, axis=1).astype(BF16)


def _na_out_kernel(x_ref, m_ref, a_ref, w_ref, o_ref):
    m = m_ref[0]
    o_ref[...] = x_ref[...] + m[2:3] * _dot(a_ref[...], w_ref[...])


def _na_tables(rows):
    npair = NA_WROWS // 2
    tbl = np.zeros((3, NA_QROWS, npair), np.int32)
    for typ, r_base in enumerate((0, NA_QROWS, rows - NA_QROWS)):
        w0 = int(np.clip(r_base - NA_KH // 2, 0, rows - NA_WROWS))
        for dr in range(NA_QROWS):
            r = r_base + dr
            r0 = int(np.clip(r - NA_KH // 2, 0, rows - NA_KH))
            for j in range(npair):
                kr = (w0 + 2 * j, w0 + 2 * j + 1)
                ok = [r0 <= x < r0 + NA_KH for x in kr]
                ri = [x - r + NA_KH - 1 for x in kr]
                if ok[0] and ok[1]:
                    e = ri[0]
                elif ok[0]:
                    e = 16 + ri[0]
                elif ok[1]:
                    e = 32 + ri[1]
                else:
                    e = 63
                tbl[typ, dr, j] = e
    return tbl.reshape(-1)


def _na_bias_kernel(rpb_ref, o_ref, *, nr, nw):
    h = pl.program_id(0)
    c = lax.broadcasted_iota(I32, (GRID_W, GRID_W), 0)
    k = lax.broadcasted_iota(I32, (GRID_W, GRID_W), 1)
    cstart = jnp.clip(c - NA_KW // 2, 0, GRID_W - NA_KW)
    ok = jnp.logical_and(k >= cstart, k < cstart + NA_KW)
    rel = k - c + NA_KW - 1
    neg = jnp.full((GRID_W, GRID_W), NEG_INF, F32)
    cbs = []
    for r in range(nr):
        acc = neg
        for w in range(nw):
            acc = jnp.where(rel == w, rpb_ref[(h * nr + r) * nw + w], acc)
        cbs.append(jnp.where(ok, acc, neg))
    cbs += [neg] * (17 - nr)
    for r in range(16):
        o_ref[0, r] = jnp.concatenate([cbs[r], cbs[r + 1]], axis=1)
        o_ref[0, 16 + r] = jnp.concatenate([cbs[r], neg], axis=1)
        o_ref[0, 32 + r] = jnp.concatenate([neg, cbs[r]], axis=1)
        o_ref[0, 48 + r] = jnp.concatenate([neg, neg], axis=1)


def _na_bias_table(rpb):
    nh, nr, nw = rpb.shape
    return pl.pallas_call(
        functools.partial(_na_bias_kernel, nr=nr, nw=nw),
        grid=(nh,),
        in_specs=[pl.BlockSpec(memory_space=pltpu.SMEM)],
        out_specs=pl.BlockSpec((1, 64, GRID_W, 2 * GRID_W), lambda h: (h, 0, 0, 0)),
        out_shape=jax.ShapeDtypeStruct((nh, 64, GRID_W, 2 * GRID_W), F32),
        compiler_params=_cparams(("parallel",)),
        name="na_bias_table",
    )(rpb.reshape(-1).astype(F32))


def _na_mixer(xs, mods, g, w_qkv, rpb, w_out, geo, dims):
    b, seq, lctx = dims
    nt, d = xs.shape
    n_tiles_tok = nt // TM
    grp = functools.partial(_tile_group, geo=geo)
    npairs = d // 128
    const2 = lambda t: (0, 0)
    q, k, v = pl.pallas_call(
        _na_proj_kernel,
        grid=(n_tiles_tok,),
        in_specs=[pl.BlockSpec((TM, d), lambda t: (t, 0)),
                  pl.BlockSpec((1, 6, d), lambda t: (grp(t), 0, 0)),
                  pl.BlockSpec((1, d), const2),
                  pl.BlockSpec((d, 3 * d), const2)],
        out_specs=[pl.BlockSpec((npairs, TM, 128), lambda t: (0, t, 0))] * 3,
        out_shape=[jax.ShapeDtypeStruct((npairs, nt, 128), BF16)] * 3,
        compiler_params=_cparams(("parallel",)),
        name="na_proj",
    )(xs, mods, g.reshape(1, d), w_qkv.astype(BF16))

    rows = seq // GRID_W
    tq = NA_QROWS * GRID_W
    n_tiles = rows // NA_QROWS
    wb = 256
    nwin = NA_WROWS * GRID_W // wb
    lat0 = b * lctx
    tbl = jnp.asarray(_na_tables(rows))
    bias = _na_bias_table(rpb)

    pp = NA_PAIRS_PER_STEP
    assert npairs % pp == 0 and (NA_QROWS * GRID_W) % wb == 0 and (NA_KH // 2 * GRID_W) % wb == 0

    def win(i):
        def f(p, bi, t, tbl_ref):
            w = jnp.clip(t * (NA_QROWS * GRID_W // wb) - NA_KH // 2 * GRID_W // wb, 0, seq // wb - nwin)
            return (p, (lat0 + bi * seq) // wb + w + i, 0)
        return f

    kv_specs = [pl.BlockSpec((pp, wb, 128), win(i)) for i in range(nwin)]
    attn = pl.pallas_call(
        functools.partial(_na_attn_kernel, n_tiles=n_tiles, nwin=nwin),
        grid_spec=pltpu.PrefetchScalarGridSpec(
            num_scalar_prefetch=1,
            grid=(npairs // pp, b, n_tiles),
            in_specs=[pl.BlockSpec((pp, tq, 128), lambda p, bi, t, tr: (p, (lat0 + bi * seq) // tq + t, 0))]
                     + kv_specs + kv_specs
                     + [pl.BlockSpec((pp, lctx, 128), lambda p, bi, t, tr: (p, bi, 0)),
                        pl.BlockSpec((pp, lctx, 128), lambda p, bi, t, tr: (p, bi, 0)),
                        pl.BlockSpec((2 * pp, 64, GRID_W, 2 * GRID_W), lambda p, bi, t, tr: (p, 0, 0, 0))],
            out_specs=pl.BlockSpec((tq, 128 * pp), lambda p, bi, t, tr: (bi * n_tiles + t, p)),
            scratch_shapes=[pltpu.VMEM((2 * pp, tq, NA_WROWS * GRID_W), F32)]),
        out_shape=jax.ShapeDtypeStruct((b * seq, d), BF16),
        compiler_params=_cparams(("parallel", "parallel", "arbitrary")),
        name="na_attn",
    )(tbl, q, *([k] * nwin), *([v] * nwin), k, v, bias)

    n_lat_tiles = b * seq // TM
    tile0 = lat0 // TM
    grp_l = functools.partial(_tile_group, geo=geo, tile0=tile0)
    return pl.pallas_call(
        _na_out_kernel,
        grid=(n_lat_tiles,),
        in_specs=[pl.BlockSpec((TM, d), lambda t: (t + tile0, 0)),
                  pl.BlockSpec((1, 6, d), lambda t: (grp_l(t), 0, 0)),
                  pl.BlockSpec((TM, d), lambda t: (t, 0)),
                  pl.BlockSpec((d, d), const2)],
        out_specs=pl.BlockSpec((TM, d), lambda t: (t, 0)),
        out_shape=jax.ShapeDtypeStruct((b * seq, d), F32),
        compiler_params=_cparams(("parallel",)),
        name="na_out",
    )(xs, mods, attn, w_out.astype(BF16))


SC_CORES = 2
SC_SUBCORES = 16
SC_WORKERS = SC_CORES * SC_SUBCORES
SLOT_T = 512


def _sc_mesh():
    return plsc.VectorSubcoreMesh(core_axis_name="c", subcore_axis_name="s")


def _sc_chunk(per_worker, max_chunk):
    return max(c for c in range(8, max_chunk + 1, 8) if per_worker % (2 * c) == 0)


def _sc_gather_rows(table, idx):
    dd = table.shape[1]
    bsz = idx.shape[0]
    per_w = bsz // SC_WORKERS
    assert per_w * SC_WORKERS == bsz
    chunk = _sc_chunk(per_w, 64)
    n_chunks = per_w // chunk

    def body(table_hbm, idx_hbm, out_hbm, idx_v, rows0, rows1, g0, g1, w0, w1):
        wid = lax.axis_index("s") * SC_CORES + lax.axis_index("c")
        pltpu.sync_copy(idx_hbm.at[wid], idx_v)
        base = wid * per_w

        def out_rows(j):
            return out_hbm.at[pl.ds(pl.multiple_of(base + j * chunk, 8), chunk)]

        def step(i, carry):
            j0, j1 = 2 * i, 2 * i + 1
            ga = pltpu.async_copy(table_hbm.at[idx_v.at[j0]], rows0, g0)
            gb = pltpu.async_copy(table_hbm.at[idx_v.at[j1]], rows1, g1)
            ga.wait()
            wa = pltpu.async_copy(rows0, out_rows(j0), w0)
            gb.wait()
            wb = pltpu.async_copy(rows1, out_rows(j1), w1)
            wa.wait()
            wb.wait()
            return carry

        lax.fori_loop(0, n_chunks // 2, step, 0)

    return pl.kernel(
        body, out_type=jax.ShapeDtypeStruct((bsz, dd), table.dtype), mesh=_sc_mesh(),
        scratch_types=[pltpu.VMEM((n_chunks, chunk), I32),
                       pltpu.VMEM((chunk, dd), table.dtype), pltpu.VMEM((chunk, dd), table.dtype),
                       pltpu.SemaphoreType.DMA, pltpu.SemaphoreType.DMA,
                       pltpu.SemaphoreType.DMA, pltpu.SemaphoreType.DMA],
        name="sc_gather_rows",
    )(table, idx.reshape(SC_WORKERS, n_chunks, chunk))


def _sc_scatter_rows(rows, dest, n_out):
    n, dd = rows.shape
    kk = dest.shape[0]
    per_w = n // SC_WORKERS
    assert per_w * SC_WORKERS == n
    chunk = _sc_chunk(per_w, 64)
    n_chunks = per_w // chunk
    dest_w = dest.reshape(kk, SC_WORKERS, n_chunks, chunk).transpose(1, 2, 0, 3)
    dest_w = dest_w.reshape(SC_WORKERS, n_chunks * kk, chunk)

    def body(rows_hbm, dest_hbm, out_hbm, idx_v, rows0, rows1, r0, r1, s0, s1):
        wid = lax.axis_index("s") * SC_CORES + lax.axis_index("c")
        pltpu.sync_copy(dest_hbm.at[wid], idx_v)
        base = wid * per_w

        def in_rows(j):
            return rows_hbm.at[pl.ds(pl.multiple_of(base + j * chunk, 8), chunk)]

        def step(i, carry):
            j0, j1 = 2 * i, 2 * i + 1
            ra = pltpu.async_copy(in_rows(j0), rows0, r0)
            rb = pltpu.async_copy(in_rows(j1), rows1, r1)
            ra.wait()
            sa = [pltpu.async_copy(rows0, out_hbm.at[idx_v.at[j0 * kk + k]], s0) for k in range(kk)]
            rb.wait()
            sb = [pltpu.async_copy(rows1, out_hbm.at[idx_v.at[j1 * kk + k]], s1) for k in range(kk)]
            for cp in sa + sb:
                cp.wait()
            return carry

        lax.fori_loop(0, n_chunks // 2, step, 0)

    return pl.kernel(
        body, out_type=jax.ShapeDtypeStruct((n_out, dd), rows.dtype), mesh=_sc_mesh(),
        scratch_types=[pltpu.VMEM((n_chunks * kk, chunk), I32),
                       pltpu.VMEM((chunk, dd), rows.dtype), pltpu.VMEM((chunk, dd), rows.dtype),
                       pltpu.SemaphoreType.DMA, pltpu.SemaphoreType.DMA,
                       pltpu.SemaphoreType.DMA, pltpu.SemaphoreType.DMA],
        name="sc_scatter_rows",
    )(rows, dest_w)


def _prefix_sum_rows(col):
    nr = col.shape[0]
    acc = jnp.broadcast_to(col, (nr, 128))
    row = lax.broadcasted_iota(I32, (nr, 128), 0)
    s = 1
    while s < nr:
        acc = acc + jnp.where(row >= s, pltpu.roll(acc, s, 0), 0.0)
        s *= 2
    return acc[:, 0:1]


def _slot_kernel(idx_ref, rank_ref, cnt_ref, dest_ref, btab_ref, *, bm):
    cnt = cnt_ref[...]
    ne = cnt.shape[0]
    padded = jnp.floor((cnt + (bm - 1.0)) * (1.0 / bm)) * bm
    pad_end = _prefix_sum_rows(padded)
    pad_start = pad_end - padded
    idx = idx_ref[...]
    e_iota = lax.broadcasted_iota(I32, (ne, idx.shape[1]), 0)
    starts = [jnp.sum(jnp.where(e_iota == idx[k:k + 1], pad_start, 0.0), axis=0, keepdims=True)
              for k in range(idx.shape[0])]
    dest_ref[...] = jnp.concatenate(starts, axis=0).astype(I32) + rank_ref[...]

    nbp = btab_ref.shape[1]
    bstart = lax.broadcasted_iota(I32, (ne, nbp), 1).astype(F32) * bm
    be = jnp.minimum(jnp.sum((pad_end <= bstart).astype(F32), axis=0, keepdims=True), ne - 1.0)
    mine = lax.broadcasted_iota(I32, (ne, nbp), 0).astype(F32) == be
    pick = lambda col: jnp.sum(jnp.where(mine, col, 0.0), axis=0, keepdims=True)
    live = jnp.clip(pick(pad_start + cnt) - bstart[0:1], 0.0, bm)
    r = lax.broadcasted_iota(I32, (ne, ne), 0)
    c = lax.broadcasted_iota(I32, (ne, ne), 1)
    ends_on_lanes = _dot_f32(jnp.ones((ne, ne), BF16), jnp.where(r == c, pad_end, 0.0))
    nxt_e = jnp.sum((ends_on_lanes <= pad_end).astype(F32), axis=1, keepdims=True)
    nxt_e = jnp.where(nxt_e < ne, nxt_e, -1.0)
    ordinal = _prefix_sum_rows((cnt > 0).astype(F32)) - 1.0
    slot_e = ordinal - 2.0 * jnp.floor(ordinal * 0.5)
    rowi = lax.broadcasted_iota(I32, btab_ref.shape, 0)
    tab = jnp.where(rowi == 0, be, jnp.where(rowi == 1, live, jnp.where(rowi == 2, pick(nxt_e), pick(slot_e))))
    btab_ref[...] = tab.astype(I32)


def _route_kernel(x_ref, m_ref, g_ref, wr_ref, br_ref, h_ref, idx_ref, gate_ref, rank_ref, cnt_ref, run_ref):
    t = pl.program_id(0)

    @pl.when(t == 0)
    def _():
        run_ref[...] = jnp.zeros_like(run_ref)

    x = x_ref[...]
    m = m_ref[0]
    h = _modulate(x, g_ref[...], m[3:4], m[4:5]).astype(BF16)
    h_ref[...] = _pack_bf16_pairs(h)
    logits = _dot_nt(wr_ref[...], h) + br_ref[...]
    ne, tm = logits.shape
    e_iota = lax.broadcasted_iota(I32, (ne, tm), 0)
    vals, idxs = [], []
    l = logits
    for _ in range(TOP_K):
        mk = jnp.max(l, axis=0, keepdims=True)
        ik = jnp.min(jnp.where(l == mk, e_iota, ne), axis=0, keepdims=True)
        vals.append(mk)
        idxs.append(ik)
        l = jnp.where(e_iota == ik, -jnp.inf, l)
    top_val = jnp.concatenate(vals, axis=0)
    ex = jnp.exp(top_val - vals[0])
    gate_ref[...] = ex / jnp.sum(ex, axis=0, keepdims=True)
    idx_ref[...] = jnp.concatenate(idxs, axis=0)

    hits = [e_iota == ik for ik in idxs]
    cnt = hits[0].astype(F32)
    for hk in hits[1:]:
        cnt = cnt + hk.astype(F32)
    si = lax.broadcasted_iota(I32, (tm, tm), 0)
    ti = lax.broadcasted_iota(I32, (tm, tm), 1)
    before = (si < ti).astype(BF16)
    total = _dot(cnt.astype(BF16), before) + run_ref[...]
    ranks = [jnp.sum(jnp.where(hk, total, 0.0), axis=0, keepdims=True) for hk in hits]
    rank_ref[...] = jnp.concatenate(ranks, axis=0).astype(I32)
    run_ref[...] = run_ref[...] + jnp.sum(cnt, axis=1, keepdims=True)
    cnt_ref[...] = run_ref[...]


def _pack_bf16_pairs(v):
    bits = lax.bitcast_convert_type(v.astype(BF16).astype(F32), jnp.uint32)
    half = bits.shape[1] // 2
    return (bits[:, half:] & jnp.uint32(0xFFFF0000)) | (bits[:, :half] >> 16)


def _unpack_bf16_pairs(w):
    return (lax.bitcast_convert_type(w << 16, F32),
            lax.bitcast_convert_type(w & jnp.uint32(0xFFFF0000), F32))


def _expert_kernel(tab_ref, x_ref, wgu_hbm, bgu_ref, wd_hbm, bd_ref, o_ref,
                   wgu_in, wd_in, wgu_bf, wd_bf, act_ref, sem_gu, sem_d, *, layer):
    blk = pl.program_id(0)
    ch = wgu_bf.shape[2]
    f = act_ref.shape[1]
    nc = f // ch
    expert, live, nxt, wslot = (tab_ref[i, blk] for i in range(4))
    used = live > 0
    fresh = jnp.logical_or(blk == 0, expert != tab_ref[0, jnp.maximum(blk - 1, 0)])

    def fetch(e, slot):
        return (pltpu.make_async_copy(wgu_hbm.at[layer, e], wgu_in.at[slot], sem_gu.at[slot]),
                pltpu.make_async_copy(wd_hbm.at[layer, e], wd_in.at[slot], sem_d.at[slot]))

    @pl.when(jnp.logical_and(used, fresh))
    def _():
        slot = wslot
        mine = fetch(expert, slot)

        @pl.when(blk == 0)
        def _():
            for cp in mine:
                cp.start()

        @pl.when(nxt >= 0)
        def _():
            for cp in fetch(nxt, 1 - slot):
                cp.start()

        for cp in mine:
            cp.wait()
        for j in range(wgu_bf.shape[0]):
            wgu_bf[j] = wgu_in[slot, :, j * ch:(j + 1) * ch].astype(BF16)
        for j in range(wd_bf.shape[0]):
            wd_bf[j] = wd_in[slot, :, j * ch:(j + 1) * ch].astype(BF16)

    bm = x_ref.shape[0]
    hb = bm // 2

    def run(nr):
        w = x_ref[:nr, :]
        row = lax.broadcasted_iota(I32, w.shape, 0)
        w = jnp.where(row < live, w, jnp.zeros_like(w))
        lo, hi = _unpack_bf16_pairs(w)
        x = jnp.concatenate([lo.astype(BF16), hi.astype(BF16)], axis=1)
        for c in range(nc):
            c0 = slice(c * ch, (c + 1) * ch)
            c1 = slice(f + c * ch, f + (c + 1) * ch)
            glu = _dot(x, wgu_bf[c]) + bgu_ref[0, 0, :, c0]
            lin = _dot(x, wgu_bf[nc + c]) + bgu_ref[0, 0, :, c1]
            glu = jnp.minimum(glu, SWIGLU_LIMIT)
            lin = jnp.clip(lin, -SWIGLU_LIMIT, SWIGLU_LIMIT)
            act_ref[:nr, c0] = (glu * jax.nn.sigmoid(SWIGLU_ALPHA * glu) * (lin + 1.0)).astype(BF16)
        a = act_ref[:nr, :]
        y = jnp.concatenate([_dot(a, wd_bf[n]) for n in range(wd_bf.shape[0])], axis=1)
        o_ref[:nr, :] = _pack_bf16_pairs(y + bd_ref[0, 0])

    @pl.when(live > hb)
    def _():
        run(bm)

    @pl.when(jnp.logical_and(used, live <= hb))
    def _():
        run(hb)
        o_ref[hb:, :] = jnp.zeros((bm - hb, o_ref.shape[1]), o_ref.dtype)

    @pl.when(jnp.logical_not(used))
    def _():
        o_ref[...] = jnp.zeros_like(o_ref)


def _expert_call(tab, buf, w_gu, b_gu, w_down, b_down, layer, bm):
    n_slots, dw = buf.shape
    _, ne, d, f2 = w_gu.shape
    f = f2 // 2
    n_blocks = n_slots // bm
    lw = lambda i, tb: (layer, tb[0, i], 0, 0)
    row = lambda i, tb: (i, 0)
    cw = 256
    return pl.pallas_call(
        functools.partial(_expert_kernel, layer=layer),
        grid_spec=pltpu.PrefetchScalarGridSpec(
            num_scalar_prefetch=1,
            grid=(n_blocks,),
            in_specs=[pl.BlockSpec((bm, dw), row),
                      pl.BlockSpec(memory_space=pl.ANY),
                      pl.BlockSpec((1, 1, 1, f2), lw),
                      pl.BlockSpec(memory_space=pl.ANY),
                      pl.BlockSpec((1, 1, 1, d), lw)],
            out_specs=pl.BlockSpec((bm, dw), row),
            scratch_shapes=[pltpu.VMEM((2, d, f2), F32), pltpu.VMEM((2, f, d), F32),
                            pltpu.VMEM((f2 // cw, d, cw), BF16), pltpu.VMEM((d // cw, f, cw), BF16),
                            pltpu.VMEM((bm, f), BF16),
                            pltpu.SemaphoreType.DMA((2,)), pltpu.SemaphoreType.DMA((2,))]),
        out_shape=jax.ShapeDtypeStruct((n_slots, dw), jnp.uint32),
        compiler_params=_cparams(("arbitrary",)),
        name="moe_experts",
    )(tab, buf, w_gu, b_gu.reshape(b_gu.shape[0], ne, 1, f2), w_down, b_down.reshape(b_down.shape[0], ne, 1, d))


def _expert_kernel_ring(start_ref, nblk_ref, cnt_ref, x_hbm, wgu_ref, bgu_ref, wd_ref, bd_ref, y_hbm,
                        xbuf, ybuf, wgu_bf, wd_bf, act_ref, sem_in, sem_out, *, n_blocks, fused=False):
    e = pl.program_id(0)
    bm = xbuf.shape[1]
    f = act_ref.shape[1]
    ch = 256
    nb = nblk_ref[e]
    start = start_ref[e]
    cnt = cnt_ref[e]

    def rows(j):
        return pl.ds(pl.multiple_of(start + j * bm, bm), bm)

    def fetch(j, slot):
        return pltpu.make_async_copy(x_hbm.at[rows(j)], xbuf.at[slot], sem_in.at[slot])

    def put(j, slot):
        return pltpu.make_async_copy(ybuf.at[slot], y_hbm.at[rows(j)], sem_out.at[slot])

    @pl.when(nb > 0)
    def _():
        fetch(0, 0).start()
        if wgu_bf.ndim == 3:
            for j in range(wgu_bf.shape[0]):
                wgu_bf[j] = wgu_ref[0, 0, :, j * ch:(j + 1) * ch].astype(BF16)
            for j in range(wd_bf.shape[0]):
                wd_bf[j] = wd_ref[0, 0, :, j * ch:(j + 1) * ch].astype(BF16)
        else:
            for j in range(wgu_bf.shape[1] // ch):
                cs = slice(j * ch, (j + 1) * ch)
                wgu_bf[:, cs] = wgu_ref[0, 0, :, cs].astype(BF16)
            for j in range(wd_bf.shape[1] // ch):
                cs = slice(j * ch, (j + 1) * ch)
                wd_bf[:, cs] = wd_ref[0, 0, :, cs].astype(BF16)

    def block(j, carry):
        slot = lax.rem(j, 2)

        @pl.when(j + 1 < nb)
        def _():
            fetch(j + 1, 1 - slot).start()

        fetch(j, slot).wait()

        @pl.when(j >= 2)
        def _():
            put(j - 2, slot).wait()

        w = xbuf[slot]
        row = lax.broadcasted_iota(I32, w.shape, 0)
        w = jnp.where(row < cnt - j * bm, w, jnp.zeros_like(w))
        lo, hi = _unpack_bf16_pairs(w)
        x = jnp.concatenate([lo.astype(BF16), hi.astype(BF16)], axis=1)
        if fused:
            z = _dot(x, wgu_bf[...]) + bgu_ref[0, 0]
            glu = jnp.minimum(z[:, :f], SWIGLU_LIMIT)
            lin = jnp.clip(z[:, f:], -SWIGLU_LIMIT, SWIGLU_LIMIT)
            act = (glu * jax.nn.sigmoid(SWIGLU_ALPHA * glu) * (lin + 1.0)).astype(BF16)
            ybuf[slot] = _pack_bf16_pairs(_dot(act, wd_bf[...]) + bd_ref[0, 0])
            put(j, slot).start()
            return carry
        if wgu_bf.ndim == 3:
            nc = f // ch
            for c in range(nc):
                c0 = slice(c * ch, (c + 1) * ch)
                c1 = slice(f + c * ch, f + (c + 1) * ch)
                glu = _dot(x, wgu_bf[c]) + bgu_ref[0, 0, :, c0]
                lin = _dot(x, wgu_bf[nc + c]) + bgu_ref[0, 0, :, c1]
                glu = jnp.minimum(glu, SWIGLU_LIMIT)
                lin = jnp.clip(lin, -SWIGLU_LIMIT, SWIGLU_LIMIT)
                act_ref[:, c0] = (glu * jax.nn.sigmoid(SWIGLU_ALPHA * glu) * (lin + 1.0)).astype(BF16)
            a = act_ref[...]
            y = jnp.concatenate([_dot(a, wd_bf[n]) for n in range(wd_bf.shape[0])], axis=1)
            ybuf[slot] = _pack_bf16_pairs(y + bd_ref[0, 0])
            put(j, slot).start()
            return carry
        for c in range(f // ch):
            c0 = slice(c * ch, (c + 1) * ch)
            c1 = slice(f + c * ch, f + (c + 1) * ch)
            glu = _dot(x, wgu_bf[:, c0]) + bgu_ref[0, 0, :, c0]
            lin = _dot(x, wgu_bf[:, c1]) + bgu_ref[0, 0, :, c1]
            glu = jnp.minimum(glu, SWIGLU_LIMIT)
            lin = jnp.clip(lin, -SWIGLU_LIMIT, SWIGLU_LIMIT)
            act_ref[:, c0] = (glu * jax.nn.sigmoid(SWIGLU_ALPHA * glu) * (lin + 1.0)).astype(BF16)
        ybuf[slot] = _pack_bf16_pairs(_dot(act_ref[...], wd_bf[...]) + bd_ref[0, 0])
        put(j, slot).start()
        return carry

    lax.fori_loop(0, nb, block, 0)

    @pl.when(nb >= 2)
    def _():
        put(nb - 2, lax.rem(nb, 2)).wait()

    @pl.when(nb >= 1)
    def _():
        put(nb - 1, lax.rem(nb + 1, 2)).wait()

    @pl.when(e == pl.num_programs(0) - 1)
    def _():
        ybuf[0] = jnp.zeros(ybuf.shape[1:], ybuf.dtype)

        def fill(j, carry):
            cp = pltpu.make_async_copy(ybuf.at[0], y_hbm.at[pl.ds(pl.multiple_of(j * bm, bm), bm)], sem_out.at[0])
            cp.start()
            cp.wait()
            return carry

        lax.fori_loop(start // bm + nb, n_blocks, fill, 0)


def _expert_call_ring(start, nblk, cnt, buf, w_gu, b_gu, w_down, b_down, layer, bm, fused=False, chunked=False):
    n_slots, dw = buf.shape
    _, ne, d, f2 = w_gu.shape
    f = f2 // 2
    lw = lambda e, *_: (layer, e, 0, 0)
    return pl.pallas_call(
        functools.partial(_expert_kernel_ring, n_blocks=n_slots // bm, fused=fused),
        grid_spec=pltpu.PrefetchScalarGridSpec(
            num_scalar_prefetch=3,
            grid=(ne,),
            in_specs=[pl.BlockSpec(memory_space=pl.ANY),
                      pl.BlockSpec((1, 1, d, f2), lw),
                      pl.BlockSpec((1, 1, 1, f2), lw),
                      pl.BlockSpec((1, 1, f, d), lw),
                      pl.BlockSpec((1, 1, 1, d), lw)],
            out_specs=pl.BlockSpec(memory_space=pl.ANY),
            scratch_shapes=[pltpu.VMEM((2, bm, dw), jnp.uint32), pltpu.VMEM((2, bm, dw), jnp.uint32),
                            pltpu.VMEM((f2 // 256, d, 256) if chunked else (d, f2), BF16),
                            pltpu.VMEM((d // 256, f, 256) if chunked else (f, d), BF16), pltpu.VMEM((bm, f), BF16),
                            pltpu.SemaphoreType.DMA((2,)), pltpu.SemaphoreType.DMA((2,))]),
        out_shape=jax.ShapeDtypeStruct((n_slots, dw), jnp.uint32),
        compiler_params=_cparams(("arbitrary",)),
        name="moe_experts",
    )(start, nblk, cnt, buf, w_gu, b_gu.reshape(b_gu.shape[0], ne, 1, f2),
      w_down, b_down.reshape(b_down.shape[0], ne, 1, d))


def _combine_kernel(x_ref, m_ref, y_ref, gate_ref, gout_ref, o_ref, *, final):
    m = m_ref[0]
    gate = gate_ref[...]
    half = y_ref.shape[2]
    f_lo, f_hi = None, None
    for k in range(TOP_K):
        lo, hi = _unpack_bf16_pairs(y_ref[k])
        gk = gate[:, k:k + 1]
        f_lo = gk * lo if f_lo is None else f_lo + gk * lo
        f_hi = gk * hi if f_hi is None else f_hi + gk * hi
    x_lo = x_ref[:, :half] + m[5:6, :half] * f_lo
    x_hi = x_ref[:, half:] + m[5:6, half:] * f_hi
    if final:
        ms = (jnp.sum(x_lo * x_lo, axis=-1, keepdims=True) + jnp.sum(x_hi * x_hi, axis=-1, keepdims=True))
        r = lax.rsqrt(ms * (0.5 / half) + NORM_EPS)
        x_lo = x_lo * r * gout_ref[:, :half]
        x_hi = x_hi * r * gout_ref[:, half:]
    o_ref[:, :half] = x_lo
    o_ref[:, half:] = x_hi


def _moe_layer(xs, mods, g, layer, w_router, b_router, w_gu, b_gu, w_down, b_down, geo, g_out, final):
    n, d = xs.shape
    ne = w_router.shape[1]
    f = w_down.shape[2]
    n_tiles = n // TM
    grp = functools.partial(_tile_group, geo=geo)
    const2 = lambda t: (0, 0)
    h, idx, gate, rank, cnt = pl.pallas_call(
        _route_kernel,
        grid=(n_tiles,),
        in_specs=[pl.BlockSpec((TM, d), lambda t: (t, 0)),
                  pl.BlockSpec((1, 6, d), lambda t: (grp(t), 0, 0)),
                  pl.BlockSpec((1, d), const2),
                  pl.BlockSpec((ne, d), const2),
                  pl.BlockSpec((ne, 1), const2)],
        out_specs=[pl.BlockSpec((TM, d // 2), lambda t: (t, 0)),
                   pl.BlockSpec((TOP_K, TM), lambda t: (0, t)),
                   pl.BlockSpec((TOP_K, TM), lambda t: (0, t)),
                   pl.BlockSpec((TOP_K, TM), lambda t: (0, t)),
                   pl.BlockSpec((ne, 1), const2)],
        out_shape=[jax.ShapeDtypeStruct((n, d // 2), jnp.uint32),
                   jax.ShapeDtypeStruct((TOP_K, n), I32),
                   jax.ShapeDtypeStruct((TOP_K, n), F32),
                   jax.ShapeDtypeStruct((TOP_K, n), I32),
                   jax.ShapeDtypeStruct((ne, 1), F32)],
        scratch_shapes=[pltpu.VMEM((ne, 1), F32)],
        compiler_params=_cparams(("arbitrary",)),
        name="moe_route",
    )(xs, mods, g.reshape(1, d), w_router.T.astype(BF16), b_router.reshape(ne, 1))

    bm = MOE_BM
    n_rows = n * TOP_K
    n_blocks = -(-n_rows // bm) + ne
    n_slots = n_blocks * bm
    nbp = -(-n_blocks // 128) * 128
    dest, btab = pl.pallas_call(
        functools.partial(_slot_kernel, bm=bm),
        grid=(n // SLOT_T,),
        in_specs=[pl.BlockSpec((TOP_K, SLOT_T), lambda t: (0, t)),
                  pl.BlockSpec((TOP_K, SLOT_T), lambda t: (0, t)),
                  pl.BlockSpec((ne, 1), const2)],
        out_specs=[pl.BlockSpec((TOP_K, SLOT_T), lambda t: (0, t)),
                   pl.BlockSpec((8, nbp), const2)],
        out_shape=[jax.ShapeDtypeStruct((TOP_K, n), I32),
                   jax.ShapeDtypeStruct((8, nbp), I32)],
        compiler_params=_cparams(("arbitrary",)),
        name="moe_slots",
    )(idx, rank, cnt)

    buf = _sc_scatter_rows(h, dest, n_slots)
    y = _expert_call(btab, buf, w_gu, b_gu, w_down, b_down, layer, bm)
    dy = d // 2
    yk = _sc_gather_rows(y, dest.reshape(-1)).reshape(TOP_K, n, dy)
    return pl.pallas_call(
        functools.partial(_combine_kernel, final=final),
        grid=(n_tiles,),
        in_specs=[pl.BlockSpec((TM, d), lambda t: (t, 0)),
                  pl.BlockSpec((1, 6, d), lambda t: (grp(t), 0, 0)),
                  pl.BlockSpec((TOP_K, TM, dy), lambda t: (0, t, 0)),
                  pl.BlockSpec((TM, TOP_K), lambda t: (t, 0)),
                  pl.BlockSpec((1, d), const2)],
        out_specs=pl.BlockSpec((TM, d), lambda t: (t, 0)),
        out_shape=jax.ShapeDtypeStruct((n, d), F32),
        compiler_params=_cparams(("parallel",)),
        name="moe_combine",
    )(xs, mods, yk, gate.T, g_out.reshape(1, d))


def kernel(x, c, ctx, c_ctx, ada_w, ada_b, norm_mix, norm_ffn, norm_out, a_w_in, a_g_v, a_w_s, a_b_s, a_w_out, b_w_in, b_w_a2, b_b_a, b_g_o, b_w_out, c_w_qkv, c_rpb, c_w_out, moe_w_router, moe_b_router, moe_w_gu, moe_b_gu, moe_w_down, moe_b_down):
    b, seq, d = x.shape
    lctx = ctx.shape[1]
    depth = ada_w.shape[0]
    assert (b * lctx) % TM == 0 and seq % TM == 0 and lctx % GLA_TB == 0 and seq % GLA_TB == 0
    assert seq % (NA_QROWS * GRID_W) == 0 and (b * lctx) % (NA_QROWS * GRID_W) == 0 and lctx % 256 == 0
    assert seq // GRID_W >= NA_WROWS + NA_QROWS and b + 1 <= 8
    geo = (b * lctx // TM, seq // TM, b)
    dims = (b, seq, lctx)

    cond = jnp.zeros((8, d), F32).at[:b].set(c).at[b].set(c_ctx)
    mods_all = _adaln(cond, ada_w, ada_b)[:, :b + 1].reshape(depth, b + 1, 6, d)

    xs = jnp.concatenate([ctx.reshape(b * lctx, d), x.reshape(b * seq, d)], axis=0)
    ctx_tiles = geo[0]
    has_ctx = True
    for i in range(depth):
        kind, j = i % N_MIXERS, i // N_MIXERS
        ctx_later = any(kk % N_MIXERS != 0 for kk in range(i + 1, depth))
        mods = mods_all[i]
        if kind == 0:
            keep_ctx = has_ctx and ctx_later
            skip = 0 if keep_ctx or not has_ctx else ctx_tiles
            geo_i = geo if has_ctx else (0, geo[1], b)
            xs = _gmlp_mixer(xs, mods, norm_mix[i], a_w_in[j], a_g_v[j], a_w_s[j], a_b_s[j], a_w_out[j],
                             skip, xs.shape[0] // TM - skip, geo_i)
            has_ctx = keep_ctx
        elif kind == 1:
            assert has_ctx
            xs = _gla_mixer(xs, mods, norm_mix[i], b_w_in[j], b_w_a2[j], b_b_a[j], b_g_o[j], b_w_out[j], geo, dims)
            if not ctx_later:
                xs = xs[b * lctx:]
                has_ctx = False
        else:
            assert has_ctx
            if ctx_later:
                raise NotImplementedError("context output of the neighbourhood mixer")
            xs = _na_mixer(xs, mods, norm_mix[i], c_w_qkv[j], c_rpb[j], c_w_out[j], geo, dims)
            has_ctx = False
        geo_i = geo if has_ctx else (0, geo[1], b)
        xs = _moe_layer(xs, mods, norm_ffn[i], i, moe_w_router[i], moe_b_router[i], moe_w_gu, moe_b_gu,
                        moe_w_down, moe_b_down, geo_i, norm_out, i == depth - 1)
    if has_ctx:
        xs = xs[b * lctx:]
    return xs.reshape(b, seq, d)
```

```python
import functools

import numpy as np
import jax
import jax.numpy as jnp
from jax import lax
from jax.experimental import pallas as pl
from jax.experimental.pallas import tpu as pltpu
from jax.experimental.pallas import tpu_sc as plsc

F32 = jnp.float32
BF16 = jnp.bfloat16
I32 = jnp.int32

NORM_EPS = 1e-6
GRID_W = 64
N_MIXERS = 3

CHUNK_A = 128
A_GROUPS = 8
GLA_HEADS = 4
GLA_RANK = 16
GLA_TAU = 16.0
GLA_CHUNK = 128
ROPE_BASE = 10000.0
NA_HEADS = 16
NA_KH = 8
NA_KW = 16
NEG_INF = -1e30
NA_QROWS = 4
NA_WROWS = 12
NA_PAIRS_PER_STEP = 4
TOP_K = 4
SWIGLU_LIMIT = 7.0
SWIGLU_ALPHA = 1.702
MOE_BM = 512

TM = 512
GLA_TB = 256
VMEM_LIMIT = 56 * 1024 * 1024


def _cparams(sem):
    return pltpu.CompilerParams(dimension_semantics=sem, vmem_limit_bytes=VMEM_LIMIT)


def _dot(a, b):
    return jnp.dot(a, b, preferred_element_type=F32)


def _dot_nt(a, b):
    return lax.dot_general(a, b, (((1,), (1,)), ((), ())), preferred_element_type=F32)


def _dot_tn(a, b):
    return lax.dot_general(a, b, (((0,), (0,)), ((), ())), preferred_element_type=F32)


def _rms(x, g):
    return x * lax.rsqrt(jnp.mean(x * x, axis=-1, keepdims=True) + NORM_EPS) * g


def _modulate(x, g, shift, scale):
    return _rms(x, g) * (1.0 + scale) + shift


def _ada_kernel(s_ref, w_ref, b_ref, o_ref):
    s = s_ref[...]
    s = s * jax.nn.sigmoid(s)
    o_ref[0] = jnp.dot(s, w_ref[0], preferred_element_type=F32,
                       precision=lax.Precision.HIGHEST) + b_ref[0]


def _adaln(cond, ada_w, ada_b):
    depth, d, n6 = ada_w.shape
    bn = n6 // 4
    return pl.pallas_call(
        _ada_kernel,
        grid=(depth, n6 // bn),
        in_specs=[pl.BlockSpec((8, d), lambda i, j: (0, 0)),
                  pl.BlockSpec((1, d, bn), lambda i, j: (i, 0, j)),
                  pl.BlockSpec((1, 1, bn), lambda i, j: (i, 0, j))],
        out_specs=pl.BlockSpec((1, 8, bn), lambda i, j: (i, 0, j)),
        out_shape=jax.ShapeDtypeStruct((depth, 8, n6), F32),
        compiler_params=_cparams(("parallel", "parallel")),
        name="adaln",
    )(cond, ada_w, ada_b.reshape(depth, 1, n6))


def _gelu(z):
    return 0.5 * z * (1.0 + lax.erf(z * np.float32(np.sqrt(0.5))))


def _gmlp_kernel(x_ref, m_ref, g_ref, win_ref, gv_ref, ws_ref, bs_ref, wout_ref, o_ref):
    x = x_ref[...]
    m = m_ref[0]
    a = gv_ref.shape[1]
    h = _modulate(x, g_ref[...], m[0:1], m[1:2]).astype(BF16)
    z = _gelu(_dot(h, win_ref[...]))
    u = z[:, :a]
    v = _rms(z[:, a:], gv_ref[...]).astype(BF16)
    gw = a // A_GROUPS
    rows = []
    for c in range(x.shape[0] // CHUNK_A):
        cols = [_dot(ws_ref[g], v[c * CHUNK_A:(c + 1) * CHUNK_A, g * gw:(g + 1) * gw])
                for g in range(A_GROUPS)]
        rows.append(jnp.concatenate(cols, axis=1) + bs_ref[...])
    s = jnp.concatenate(rows, axis=0)
    y = _dot((u * s).astype(BF16), wout_ref[...])
    o_ref[...] = x + m[2:3] * y


def _gmlp_mixer(xs, mods, g, w_in, g_v, w_s, b_s, w_out, tile0, n_tiles, geo):
    d = xs.shape[1]
    a = g_v.shape[0]
    gw = a // A_GROUPS
    bias = jnp.repeat(b_s.T, gw, axis=1)
    grp = functools.partial(_tile_group, geo=geo, tile0=tile0)
    const2 = lambda t: (0, 0)
    return pl.pallas_call(
        _gmlp_kernel,
        grid=(n_tiles,),
        in_specs=[pl.BlockSpec((TM, d), lambda t: (t + tile0, 0)),
                  pl.BlockSpec((1, 6, d), lambda t: (grp(t), 0, 0)),
                  pl.BlockSpec((1, d), const2),
                  pl.BlockSpec((d, 2 * a), const2),
                  pl.BlockSpec((1, a), const2),
                  pl.BlockSpec((A_GROUPS, CHUNK_A, CHUNK_A), lambda t: (0, 0, 0)),
                  pl.BlockSpec((CHUNK_A, a), const2),
                  pl.BlockSpec((a, d), const2)],
        out_specs=pl.BlockSpec((TM, d), lambda t: (t, 0)),
        out_shape=jax.ShapeDtypeStruct((n_tiles * TM, d), F32),
        compiler_params=_cparams(("parallel",)),
        name="gmlp_mixer",
    )(xs, mods, g.reshape(1, d), w_in.astype(BF16), g_v.reshape(1, a), w_s.astype(BF16),
      bias, w_out.astype(BF16))


def _tile_group(t, geo, tile0=0):
    n_ctx_tiles, tiles_per_batch, nb = geo
    tt = t + tile0
    return jnp.where(tt < n_ctx_tiles, nb, (tt - n_ctx_tiles) // tiles_per_batch)


def _dot_f32(tri_bf16, x):
    hi = x.astype(BF16)
    r1 = x - hi.astype(F32)
    mid = r1.astype(BF16)
    lo = (r1 - mid.astype(F32)).astype(BF16)
    return _dot(tri_bf16, hi) + _dot(tri_bf16, mid) + _dot(tri_bf16, lo)


def _gla_proj_kernel(x_ref, m_ref, g_ref, wq_ref, wk_ref, wv_ref, wg_ref, wr_ref, wa_ref, ba_ref,
                     cos_ref, sin_ref, q_ref, k_ref, v_ref, go_ref, la_ref):
    x = x_ref[...]
    m = m_ref[0]
    h = _modulate(x, g_ref[...], m[0:1], m[1:2]).astype(BF16)
    kdim = wq_ref.shape[1]
    hk = kdim // GLA_HEADS
    q = _dot(h, wq_ref[...]) * np.float32(hk ** -0.5)
    k = _dot(h, wk_ref[...])
    v_ref[...] = _dot(h, wv_ref[...]).astype(BF16)
    go_ref[...] = _dot(h, wg_ref[...])
    r = _dot(h, wr_ref[...])
    z = _dot(r.astype(BF16), wa_ref[...]) + ba_ref[...]
    la_ref[...] = jax.nn.log_sigmoid(z) * np.float32(1.0 / GLA_TAU)

    cos = jnp.concatenate([cos_ref[...]] * GLA_HEADS, axis=1)
    sin = jnp.concatenate([sin_ref[...]] * GLA_HEADS, axis=1)
    nf = hk // 4
    lane = lax.broadcasted_iota(I32, q.shape, 1)
    first = (lane % (2 * nf)) < nf

    def rope(t):
        up = pltpu.roll(t, kdim - nf, 1)
        dn = pltpu.roll(t, nf, 1)
        return t * cos + jnp.where(first, up, dn) * sin

    q_ref[...] = rope(q)
    k_ref[...] = rope(k)


def _gla_chunk(q_ref, k_ref, v_ref, la_ref, o_ref, st_ref, n, rev):
    kdim = q_ref.shape[1]
    hk = kdim // GLA_HEADS
    hv = v_ref.shape[1] // GLA_HEADS
    c = GLA_CHUNK
    ri = lax.broadcasted_iota(I32, (c, c), 0)
    ci = lax.broadcasted_iota(I32, (c, c), 1)
    keep = (ci >= ri) if rev else (ci <= ri)
    tri = keep.astype(BF16)
    rows = slice(n * c, (n + 1) * c)
    cum = _dot_f32(tri, la_ref[rows, :])
    last = cum[0:1] if rev else cum[c - 1:c]
    q = q_ref[rows, :]
    k = k_ref[rows, :]
    q_dec = (q * jnp.exp(cum)).astype(BF16)
    mid = cum[c // 2:c // 2 + 1]
    q_att = (q * jnp.exp(cum - mid)).astype(BF16)
    k_att = (k * jnp.exp(mid - cum)).astype(BF16)
    k_end = (k * jnp.exp(last - cum)).astype(BF16)
    dec = jnp.exp(last)
    for h in range(GLA_HEADS):
        ks = slice(h * hk, (h + 1) * hk)
        vs = slice(h * hv, (h + 1) * hv)
        att = jnp.where(keep, _dot_nt(q_att[:, ks], k_att[:, ks]), 0.0).astype(BF16)
        vh = v_ref[rows, vs]
        st = st_ref[h]
        o_ref[rows, vs] = _dot(att, vh) + _dot_nt(q_dec[:, ks], st.astype(BF16))
        st_ref[h] = st * dec[:, ks] + _dot_tn(vh, k_end[:, ks])


def _gla_scan_kernel(qf_ref, kf_ref, vf_ref, laf_ref, qb_ref, kb_ref, vb_ref, lab_ref,
                     of_ref, ob_ref, stf_ref, stb_ref):
    s = pl.program_id(1)

    @pl.when(s == 0)
    def _():
        stf_ref[...] = jnp.zeros_like(stf_ref)
        stb_ref[...] = jnp.zeros_like(stb_ref)

    nch = qf_ref.shape[0] // GLA_CHUNK
    for n in range(nch):
        _gla_chunk(qf_ref, kf_ref, vf_ref, laf_ref, of_ref, stf_ref, n, False)
        _gla_chunk(qb_ref, kb_ref, vb_ref, lab_ref, ob_ref, stb_ref, nch - 1 - n, True)


def _gla_out_kernel(x_ref, m_ref, of_ref, ob_ref, go_ref, gn_ref, wout_ref, o_ref):
    x = x_ref[...]
    m = m_ref[0]
    o = of_ref[...] + ob_ref[...]
    hv = gn_ref.shape[1]
    parts = [_rms(o[:, h * hv:(h + 1) * hv], gn_ref[...]) for h in range(GLA_HEADS)]
    o = jnp.concatenate(parts, axis=1)
    gate = go_ref[...]
    y = _dot((o * (gate * jax.nn.sigmoid(gate))).astype(BF16), wout_ref[...])
    o_ref[...] = x + m[2:3] * y


def _gla_mixer(xs, mods, g, w_in, w_a2, b_a, g_o, w_out, geo, dims):
    b, seq, lctx = dims
    nt, d = xs.shape
    n_tiles = nt // TM
    kdim = w_a2.shape[2]
    vdim = g_o.shape[0] * GLA_HEADS
    hk = kdim // GLA_HEADS
    nf = hk // 4
    wq = w_in[:, :kdim].astype(BF16)
    wk = w_in[:, kdim:2 * kdim].astype(BF16)
    wv = w_in[:, 2 * kdim:2 * kdim + vdim].astype(BF16)
    wg = w_in[:, 2 * kdim + vdim:2 * kdim + 2 * vdim].astype(BF16)
    wr = jnp.pad(w_in[:, 2 * kdim + 2 * vdim:], ((0, 0), (0, 128 - 2 * GLA_RANK))).astype(BF16)
    wa = jnp.zeros((128, 2 * kdim), F32)
    wa = wa.at[:GLA_RANK, :kdim].set(w_a2[0]).at[GLA_RANK:2 * GLA_RANK, kdim:].set(w_a2[1]).astype(BF16)
    ba = b_a.reshape(1, 2 * kdim)

    tpos = jnp.arange(seq)
    freqs = jnp.power(ROPE_BASE, -jnp.arange(nf, dtype=F32) / nf)
    ar = (tpos // GRID_W).astype(F32)[:, None] * freqs
    ac = (tpos % GRID_W).astype(F32)[:, None] * freqs
    cos = jnp.concatenate([jnp.cos(ar), jnp.cos(ar), jnp.cos(ac), jnp.cos(ac)], axis=1)
    sin = jnp.concatenate([-jnp.sin(ar), jnp.sin(ar), -jnp.sin(ac), jnp.sin(ac)], axis=1)
    cos = jnp.concatenate([jnp.ones((TM, hk), F32), cos], axis=0)
    sin = jnp.concatenate([jnp.zeros((TM, hk), F32), sin], axis=0)

    n_ctx_tiles, tiles_per_batch, _ = geo
    grp = functools.partial(_tile_group, geo=geo)

    def rope_blk(t):
        return (jnp.where(t < n_ctx_tiles, 0, 1 + (t - n_ctx_tiles) % tiles_per_batch), 0)

    const2 = lambda t: (0, 0)
    row = lambda t: (t, 0)
    q, k, v, go, la = pl.pallas_call(
        _gla_proj_kernel,
        grid=(n_tiles,),
        in_specs=[pl.BlockSpec((TM, d), row),
                  pl.BlockSpec((1, 6, d), lambda t: (grp(t), 0, 0)),
                  pl.BlockSpec((1, d), const2),
                  pl.BlockSpec((d, kdim), const2), pl.BlockSpec((d, kdim), const2),
                  pl.BlockSpec((d, vdim), const2), pl.BlockSpec((d, vdim), const2),
                  pl.BlockSpec((d, 128), const2), pl.BlockSpec((128, 2 * kdim), const2),
                  pl.BlockSpec((1, 2 * kdim), const2),
                  pl.BlockSpec((TM, hk), rope_blk), pl.BlockSpec((TM, hk), rope_blk)],
        out_specs=[pl.BlockSpec((TM, kdim), row), pl.BlockSpec((TM, kdim), row),
                   pl.BlockSpec((TM, vdim), row), pl.BlockSpec((TM, vdim), row),
                   pl.BlockSpec((TM, 2 * kdim), row)],
        out_shape=[jax.ShapeDtypeStruct((nt, kdim), F32), jax.ShapeDtypeStruct((nt, kdim), F32),
                   jax.ShapeDtypeStruct((nt, vdim), BF16), jax.ShapeDtypeStruct((nt, vdim), F32),
                   jax.ShapeDtypeStruct((nt, 2 * kdim), F32)],
        compiler_params=_cparams(("parallel",)),
        name="gla_proj",
    )(xs, mods, g.reshape(1, d), wq, wk, wv, wg, wr, wa, ba, cos, sin)

    tb = GLA_TB
    ctx_steps = lctx // tb
    lat_steps = seq // tb
    steps = ctx_steps + lat_steps

    def blk(bi, s, rev):
        if rev:
            cs, ls = ctx_steps - 1 - s, lat_steps - 1 - (s - ctx_steps)
        else:
            cs, ls = s, s - ctx_steps
        return jnp.where(s < ctx_steps, bi * ctx_steps + cs, b * ctx_steps + bi * lat_steps + ls)

    def dir_specs(rev):
        row_blk = lambda bi, s: (blk(bi, s, rev), 0)
        return [pl.BlockSpec((tb, kdim), row_blk), pl.BlockSpec((tb, kdim), row_blk),
                pl.BlockSpec((tb, vdim), row_blk),
                pl.BlockSpec((tb, kdim), lambda bi, s: (blk(bi, s, rev), 1 if rev else 0))]

    state = pltpu.VMEM((GLA_HEADS, vdim // GLA_HEADS, hk), F32)
    o_f, o_b = pl.pallas_call(
        _gla_scan_kernel,
        grid=(b, steps),
        in_specs=dir_specs(False) + dir_specs(True),
        out_specs=[pl.BlockSpec((tb, vdim), lambda bi, s: (blk(bi, s, False), 0)),
                   pl.BlockSpec((tb, vdim), lambda bi, s: (blk(bi, s, True), 0))],
        out_shape=[jax.ShapeDtypeStruct((nt, vdim), F32)] * 2,
        scratch_shapes=[state, state],
        compiler_params=_cparams(("parallel", "arbitrary")),
        name="gla_scan",
    )(q, k, v, la, q, k, v, la)

    return pl.pallas_call(
        _gla_out_kernel,
        grid=(n_tiles,),
        in_specs=[pl.BlockSpec((TM, d), row),
                  pl.BlockSpec((1, 6, d), lambda t: (grp(t), 0, 0)),
                  pl.BlockSpec((TM, vdim), row), pl.BlockSpec((TM, vdim), row),
                  pl.BlockSpec((TM, vdim), row),
                  pl.BlockSpec((1, vdim // GLA_HEADS), const2),
                  pl.BlockSpec((vdim, d), const2)],
        out_specs=pl.BlockSpec((TM, d), row),
        out_shape=jax.ShapeDtypeStruct((nt, d), F32),
        compiler_params=_cparams(("parallel",)),
        name="gla_out",
    )(xs, mods, o_f, o_b, go, g_o.reshape(1, -1), w_out.astype(BF16))


def _na_proj_kernel(x_ref, m_ref, g_ref, w_ref, q_ref, k_ref, v_ref):
    x = x_ref[...]
    m = m_ref[0]
    d = x.shape[1]
    h = _modulate(x, g_ref[...], m[0:1], m[1:2]).astype(BF16)
    qkv = _dot(h, w_ref[...])
    hd = d // NA_HEADS
    for p in range(q_ref.shape[0]):
        cs = slice(p * 128, (p + 1) * 128)
        q_ref[p] = (qkv[:, cs] * np.float32(hd ** -0.5)).astype(BF16)
        k_ref[p] = qkv[:, d + p * 128:d + (p + 1) * 128].astype(BF16)
        v_ref[p] = qkv[:, 2 * d + p * 128:2 * d + (p + 1) * 128].astype(BF16)


def _na_attn_kernel(tbl_ref, q_ref, *refs, n_tiles, nwin):
    k_refs, v_refs = refs[:nwin], refs[nwin:2 * nwin]
    kc_ref, vc_ref, bias_ref, o_ref, s_ref, p_ref = refs[2 * nwin:]
    t = pl.program_id(2)
    typ = jnp.where(t == 0, 0, jnp.where(t == n_tiles - 1, 2, 1))
    npair = NA_WROWS // 2
    nlat = NA_WROWS * GRID_W
    lane = lax.broadcasted_iota(I32, q_ref.shape[1:], 1)
    npp, tq = q_ref.shape[0], q_ref.shape[1]
    for pp in range(npp):
        q = q_ref[pp]
        k_all = jnp.concatenate([r[pp] for r in k_refs] + [kc_ref[pp]], axis=0)
        zero = jnp.zeros_like(q)
        q2 = jnp.concatenate([jnp.where(lane < 64, q, zero), jnp.where(lane >= 64, q, zero)], axis=0)
        s_ref[pp] = _dot_nt(q2, k_all)
    for pp in range(npp):
        for hh in range(2):
            for dr in range(NA_QROWS):
                rs = slice(hh * tq + dr * GRID_W, hh * tq + (dr + 1) * GRID_W)
                lat = [s_ref[pp, rs, j * 128:(j + 1) * 128]
                       + bias_ref[2 * pp + hh, tbl_ref[typ * (NA_QROWS * npair) + dr * npair + j]]
                       for j in range(npair)]
                sb = jnp.concatenate(lat + [s_ref[pp, rs, nlat:]], axis=1)
                e = jnp.exp(sb - jnp.max(sb, axis=1, keepdims=True))
                p_ref[pp, rs, :] = (e * (1.0 / jnp.sum(e, axis=1, keepdims=True))).astype(BF16)
    outs = []
    for pp in range(npp):
        v_all = jnp.concatenate([r[pp] for r in v_refs] + [vc_ref[pp]], axis=0)
        o2 = _dot(p_ref[pp], v_all)
        outs.append(jnp.where(lane < 64, o2[:tq], o2[tq:]))
    o_ref[...] = jnp.concatenate(outs, axis=1).astype(BF16)


def _na_out_kernel(x_ref, m_ref, a_ref, w_ref, o_ref):
    m = m_ref[0]
    o_ref[...] = x_ref[...] + m[2:3] * _dot(a_ref[...], w_ref[...])


def _na_tables(rows):
    npair = NA_WROWS // 2
    tbl = np.zeros((3, NA_QROWS, npair), np.int32)
    for typ, r_base in enumerate((0, NA_QROWS, rows - NA_QROWS)):
        w0 = int(np.clip(r_base - NA_KH // 2, 0, rows - NA_WROWS))
        for dr in range(NA_QROWS):
            r = r_base + dr
            r0 = int(np.clip(r - NA_KH // 2, 0, rows - NA_KH))
            for j in range(npair):
                kr = (w0 + 2 * j, w0 + 2 * j + 1)
                ok = [r0 <= x < r0 + NA_KH for x in kr]
                ri = [x - r + NA_KH - 1 for x in kr]
                if ok[0] and ok[1]:
                    e = ri[0]
                elif ok[0]:
                    e = 16 + ri[0]
                elif ok[1]:
                    e = 32 + ri[1]
                else:
                    e = 63
                tbl[typ, dr, j] = e
    return tbl.reshape(-1)


def _na_bias_kernel(rpb_ref, o_ref, *, nr, nw):
    h = pl.program_id(0)
    c = lax.broadcasted_iota(I32, (GRID_W, GRID_W), 0)
    k = lax.broadcasted_iota(I32, (GRID_W, GRID_W), 1)
    cstart = jnp.clip(c - NA_KW // 2, 0, GRID_W - NA_KW)
    ok = jnp.logical_and(k >= cstart, k < cstart + NA_KW)
    rel = k - c + NA_KW - 1
    neg = jnp.full((GRID_W, GRID_W), NEG_INF, F32)
    cbs = []
    for r in range(nr):
        acc = neg
        for w in range(nw):
            acc = jnp.where(rel == w, rpb_ref[(h * nr + r) * nw + w], acc)
        cbs.append(jnp.where(ok, acc, neg))
    cbs += [neg] * (17 - nr)
    for r in range(16):
        o_ref[0, r] = jnp.concatenate([cbs[r], cbs[r + 1]], axis=1)
        o_ref[0, 16 + r] = jnp.concatenate([cbs[r], neg], axis=1)
        o_ref[0, 32 + r] = jnp.concatenate([neg, cbs[r]], axis=1)
        o_ref[0, 48 + r] = jnp.concatenate([neg, neg], axis=1)


def _na_bias_table(rpb):
    nh, nr, nw = rpb.shape
    return pl.pallas_call(
        functools.partial(_na_bias_kernel, nr=nr, nw=nw),
        grid=(nh,),
        in_specs=[pl.BlockSpec(memory_space=pltpu.SMEM)],
        out_specs=pl.BlockSpec((1, 64, GRID_W, 2 * GRID_W), lambda h: (h, 0, 0, 0)),
        out_shape=jax.ShapeDtypeStruct((nh, 64, GRID_W, 2 * GRID_W), F32),
        compiler_params=_cparams(("parallel",)),
        name="na_bias_table",
    )(rpb.reshape(-1).astype(F32))


def _na_mixer(xs, mods, g, w_qkv, rpb, w_out, geo, dims):
    b, seq, lctx = dims
    nt, d = xs.shape
    n_tiles_tok = nt // TM
    grp = functools.partial(_tile_group, geo=geo)
    npairs = d // 128
    const2 = lambda t: (0, 0)
    q, k, v = pl.pallas_call(
        _na_proj_kernel,
        grid=(n_tiles_tok,),
        in_specs=[pl.BlockSpec((TM, d), lambda t: (t, 0)),
                  pl.BlockSpec((1, 6, d), lambda t: (grp(t), 0, 0)),
                  pl.BlockSpec((1, d), const2),
                  pl.BlockSpec((d, 3 * d), const2)],
        out_specs=[pl.BlockSpec((npairs, TM, 128), lambda t: (0, t, 0))] * 3,
        out_shape=[jax.ShapeDtypeStruct((npairs, nt, 128), BF16)] * 3,
        compiler_params=_cparams(("parallel",)),
        name="na_proj",
    )(xs, mods, g.reshape(1, d), w_qkv.astype(BF16))

    rows = seq // GRID_W
    tq = NA_QROWS * GRID_W
    n_tiles = rows // NA_QROWS
    wb = 256
    nwin = NA_WROWS * GRID_W // wb
    lat0 = b * lctx
    tbl = jnp.asarray(_na_tables(rows))
    bias = _na_bias_table(rpb)

    pp = NA_PAIRS_PER_STEP
    assert npairs % pp == 0 and (NA_QROWS * GRID_W) % wb == 0 and (NA_KH // 2 * GRID_W) % wb == 0

    def win(i):
        def f(p, bi, t, tbl_ref):
            w = jnp.clip(t * (NA_QROWS * GRID_W // wb) - NA_KH // 2 * GRID_W // wb, 0, seq // wb - nwin)
            return (p, (lat0 + bi * seq) // wb + w + i, 0)
        return f

    kv_specs = [pl.BlockSpec((pp, wb, 128), win(i)) for i in range(nwin)]
    attn = pl.pallas_call(
        functools.partial(_na_attn_kernel, n_tiles=n_tiles, nwin=nwin),
        grid_spec=pltpu.PrefetchScalarGridSpec(
            num_scalar_prefetch=1,
            grid=(npairs // pp, b, n_tiles),
            in_specs=[pl.BlockSpec((pp, tq, 128), lambda p, bi, t, tr: (p, (lat0 + bi * seq) // tq + t, 0))]
                     + kv_specs + kv_specs
                     + [pl.BlockSpec((pp, lctx, 128), lambda p, bi, t, tr: (p, bi, 0)),
                        pl.BlockSpec((pp, lctx, 128), lambda p, bi, t, tr: (p, bi, 0)),
                        pl.BlockSpec((2 * pp, 64, GRID_W, 2 * GRID_W), lambda p, bi, t, tr: (p, 0, 0, 0),
                                     pipeline_mode=pl.Buffered(1))],
            out_specs=pl.BlockSpec((tq, 128 * pp), lambda p, bi, t, tr: (bi * n_tiles + t, p)),
            scratch_shapes=[pltpu.VMEM((pp, 2 * tq, NA_WROWS * GRID_W + lctx), F32),
                            pltpu.VMEM((pp, 2 * tq, NA_WROWS * GRID_W + lctx), BF16)]),
        out_shape=jax.ShapeDtypeStruct((b * seq, d), BF16),
        compiler_params=_cparams(("parallel", "parallel", "arbitrary")),
        name="na_attn",
    )(tbl, q, *([k] * nwin), *([v] * nwin), k, v, bias)

    n_lat_tiles = b * seq // TM
    tile0 = lat0 // TM
    grp_l = functools.partial(_tile_group, geo=geo, tile0=tile0)
    return pl.pallas_call(
        _na_out_kernel,
        grid=(n_lat_tiles,),
        in_specs=[pl.BlockSpec((TM, d), lambda t: (t + tile0, 0)),
                  pl.BlockSpec((1, 6, d), lambda t: (grp_l(t), 0, 0)),
                  pl.BlockSpec((TM, d), lambda t: (t, 0)),
                  pl.BlockSpec((d, d), const2)],
        out_specs=pl.BlockSpec((TM, d), lambda t: (t, 0)),
        out_shape=jax.ShapeDtypeStruct((b * seq, d), F32),
        compiler_params=_cparams(("parallel",)),
        name="na_out",
    )(xs, mods, attn, w_out.astype(BF16))


SC_CORES = 2
SC_SUBCORES = 16
SC_WORKERS = SC_CORES * SC_SUBCORES
SLOT_T = 512


def _sc_mesh():
    return plsc.VectorSubcoreMesh(core_axis_name="c", subcore_axis_name="s")


def _sc_chunk(per_worker, max_chunk):
    return max(c for c in range(8, max_chunk + 1, 8) if per_worker % (2 * c) == 0)


def _sc_gather_rows(table, idx):
    dd = table.shape[1]
    bsz = idx.shape[0]
    per_w = bsz // SC_WORKERS
    assert per_w * SC_WORKERS == bsz
    chunk = _sc_chunk(per_w, 64)
    n_chunks = per_w // chunk

    def body(table_hbm, idx_hbm, out_hbm, idx_v, rows0, rows1, g0, g1, w0, w1):
        wid = lax.axis_index("s") * SC_CORES + lax.axis_index("c")
        pltpu.sync_copy(idx_hbm.at[wid], idx_v)
        base = wid * per_w

        def out_rows(j):
            return out_hbm.at[pl.ds(pl.multiple_of(base + j * chunk, 8), chunk)]

        def step(i, carry):
            j0, j1 = 2 * i, 2 * i + 1
            ga = pltpu.async_copy(table_hbm.at[idx_v.at[j0]], rows0, g0)
            gb = pltpu.async_copy(table_hbm.at[idx_v.at[j1]], rows1, g1)
            ga.wait()
            wa = pltpu.async_copy(rows0, out_rows(j0), w0)
            gb.wait()
            wb = pltpu.async_copy(rows1, out_rows(j1), w1)
            wa.wait()
            wb.wait()
            return carry

        lax.fori_loop(0, n_chunks // 2, step, 0)

    return pl.kernel(
        body, out_type=jax.ShapeDtypeStruct((bsz, dd), table.dtype), mesh=_sc_mesh(),
        scratch_types=[pltpu.VMEM((n_chunks, chunk), I32),
                       pltpu.VMEM((chunk, dd), table.dtype), pltpu.VMEM((chunk, dd), table.dtype),
                       pltpu.SemaphoreType.DMA, pltpu.SemaphoreType.DMA,
                       pltpu.SemaphoreType.DMA, pltpu.SemaphoreType.DMA],
        name="sc_gather_rows",
    )(table, idx.reshape(SC_WORKERS, n_chunks, chunk))


def _sc_scatter_rows(rows, dest, n_out):
    n, dd = rows.shape
    kk = dest.shape[0]
    per_w = n // SC_WORKERS
    assert per_w * SC_WORKERS == n
    chunk = _sc_chunk(per_w, 64)
    n_chunks = per_w // chunk
    dest_w = dest.reshape(kk, SC_WORKERS, n_chunks, chunk).transpose(1, 2, 0, 3)
    dest_w = dest_w.reshape(SC_WORKERS, n_chunks * kk, chunk)

    def body(rows_hbm, dest_hbm, out_hbm, idx_v, rows0, rows1, r0, r1, s0, s1):
        wid = lax.axis_index("s") * SC_CORES + lax.axis_index("c")
        pltpu.sync_copy(dest_hbm.at[wid], idx_v)
        base = wid * per_w

        def in_rows(j):
            return rows_hbm.at[pl.ds(pl.multiple_of(base + j * chunk, 8), chunk)]

        def step(i, carry):
            j0, j1 = 2 * i, 2 * i + 1
            ra = pltpu.async_copy(in_rows(j0), rows0, r0)
            rb = pltpu.async_copy(in_rows(j1), rows1, r1)
            ra.wait()
            sa = [pltpu.async_copy(rows0, out_hbm.at[idx_v.at[j0 * kk + k]], s0) for k in range(kk)]
            rb.wait()
            sb = [pltpu.async_copy(rows1, out_hbm.at[idx_v.at[j1 * kk + k]], s1) for k in range(kk)]
            for cp in sa + sb:
                cp.wait()
            return carry

        lax.fori_loop(0, n_chunks // 2, step, 0)

    return pl.kernel(
        body, out_type=jax.ShapeDtypeStruct((n_out, dd), rows.dtype), mesh=_sc_mesh(),
        scratch_types=[pltpu.VMEM((n_chunks * kk, chunk), I32),
                       pltpu.VMEM((chunk, dd), rows.dtype), pltpu.VMEM((chunk, dd), rows.dtype),
                       pltpu.SemaphoreType.DMA, pltpu.SemaphoreType.DMA,
                       pltpu.SemaphoreType.DMA, pltpu.SemaphoreType.DMA],
        name="sc_scatter_rows",
    )(rows, dest_w)


def _prefix_sum_rows(col):
    nr = col.shape[0]
    acc = jnp.broadcast_to(col, (nr, 128))
    row = lax.broadcasted_iota(I32, (nr, 128), 0)
    s = 1
    while s < nr:
        acc = acc + jnp.where(row >= s, pltpu.roll(acc, s, 0), 0.0)
        s *= 2
    return acc[:, 0:1]


def _slot_kernel(idx_ref, rank_ref, cnt_ref, dest_ref, btab_ref, *, bm):
    cnt = cnt_ref[...]
    ne = cnt.shape[0]
    padded = jnp.floor((cnt + (bm - 1.0)) * (1.0 / bm)) * bm
    pad_end = _prefix_sum_rows(padded)
    pad_start = pad_end - padded
    idx = idx_ref[...]
    e_iota = lax.broadcasted_iota(I32, (ne, idx.shape[1]), 0)
    starts = [jnp.sum(jnp.where(e_iota == idx[k:k + 1], pad_start, 0.0), axis=0, keepdims=True)
              for k in range(idx.shape[0])]
    dest_ref[...] = jnp.concatenate(starts, axis=0).astype(I32) + rank_ref[...]

    nbp = btab_ref.shape[1]
    bstart = lax.broadcasted_iota(I32, (ne, nbp), 1).astype(F32) * bm
    be = jnp.minimum(jnp.sum((pad_end <= bstart).astype(F32), axis=0, keepdims=True), ne - 1.0)
    mine = lax.broadcasted_iota(I32, (ne, nbp), 0).astype(F32) == be
    pick = lambda col: jnp.sum(jnp.where(mine, col, 0.0), axis=0, keepdims=True)
    live = jnp.clip(pick(pad_start + cnt) - bstart[0:1], 0.0, bm)
    r = lax.broadcasted_iota(I32, (ne, ne), 0)
    c = lax.broadcasted_iota(I32, (ne, ne), 1)
    ends_on_lanes = _dot_f32(jnp.ones((ne, ne), BF16), jnp.where(r == c, pad_end, 0.0))
    nxt_e = jnp.sum((ends_on_lanes <= pad_end).astype(F32), axis=1, keepdims=True)
    nxt_e = jnp.where(nxt_e < ne, nxt_e, -1.0)
    ordinal = _prefix_sum_rows((cnt > 0).astype(F32)) - 1.0
    slot_e = ordinal - 2.0 * jnp.floor(ordinal * 0.5)
    rowi = lax.broadcasted_iota(I32, btab_ref.shape, 0)
    tab = jnp.where(rowi == 0, be, jnp.where(rowi == 1, live, jnp.where(rowi == 2, pick(nxt_e), pick(slot_e))))
    btab_ref[...] = tab.astype(I32)


def _route_kernel(x_ref, m_ref, g_ref, wr_ref, br_ref, h_ref, idx_ref, gate_ref, rank_ref, cnt_ref, run_ref):
    t = pl.program_id(0)

    @pl.when(t == 0)
    def _():
        run_ref[...] = jnp.zeros_like(run_ref)

    x = x_ref[...]
    m = m_ref[0]
    h = _modulate(x, g_ref[...], m[3:4], m[4:5]).astype(BF16)
    h_ref[...] = _pack_bf16_pairs(h)
    logits = _dot_nt(wr_ref[...], h) + br_ref[...]
    ne, tm = logits.shape
    e_iota = lax.broadcasted_iota(I32, (ne, tm), 0)
    vals, idxs = [], []
    l = logits
    for _ in range(TOP_K):
        mk = jnp.max(l, axis=0, keepdims=True)
        ik = jnp.min(jnp.where(l == mk, e_iota, ne), axis=0, keepdims=True)
        vals.append(mk)
        idxs.append(ik)
        l = jnp.where(e_iota == ik, -jnp.inf, l)
    top_val = jnp.concatenate(vals, axis=0)
    ex = jnp.exp(top_val - vals[0])
    gate_ref[...] = ex / jnp.sum(ex, axis=0, keepdims=True)
    idx_ref[...] = jnp.concatenate(idxs, axis=0)

    hits = [e_iota == ik for ik in idxs]
    cnt = hits[0].astype(F32)
    for hk in hits[1:]:
        cnt = cnt + hk.astype(F32)
    si = lax.broadcasted_iota(I32, (tm, tm), 0)
    ti = lax.broadcasted_iota(I32, (tm, tm), 1)
    before = (si < ti).astype(BF16)
    total = _dot(cnt.astype(BF16), before) + run_ref[...]
    ranks = [jnp.sum(jnp.where(hk, total, 0.0), axis=0, keepdims=True) for hk in hits]
    rank_ref[...] = jnp.concatenate(ranks, axis=0).astype(I32)
    run_ref[...] = run_ref[...] + jnp.sum(cnt, axis=1, keepdims=True)
    cnt_ref[...] = run_ref[...]


def _pack_bf16_pairs(v):
    bits = lax.bitcast_convert_type(v.astype(BF16).astype(F32), jnp.uint32)
    half = bits.shape[1] // 2
    return (bits[:, half:] & jnp.uint32(0xFFFF0000)) | (bits[:, :half] >> 16)


def _unpack_bf16_pairs(w):
    return (lax.bitcast_convert_type(w << 16, F32),
            lax.bitcast_convert_type(w & jnp.uint32(0xFFFF0000), F32))


def _expert_kernel(tab_ref, x_ref, wgu_hbm, bgu_ref, wd_hbm, bd_ref, o_ref,
                   wgu_in, wd_in, wgu_bf, wd_bf, act_ref, sem_gu, sem_d, *, layer):
    blk = pl.program_id(0)
    ch = wgu_bf.shape[2]
    f = act_ref.shape[1]
    nc = f // ch
    expert, live, nxt, wslot = (tab_ref[i, blk] for i in range(4))
    used = live > 0
    fresh = jnp.logical_or(blk == 0, expert != tab_ref[0, jnp.maximum(blk - 1, 0)])

    def fetch(e, slot):
        return (pltpu.make_async_copy(wgu_hbm.at[layer, e], wgu_in.at[slot], sem_gu.at[slot]),
                pltpu.make_async_copy(wd_hbm.at[layer, e], wd_in.at[slot], sem_d.at[slot]))

    @pl.when(jnp.logical_and(used, fresh))
    def _():
        slot = wslot
        mine = fetch(expert, slot)

        @pl.when(blk == 0)
        def _():
            for cp in mine:
                cp.start()

        @pl.when(nxt >= 0)
        def _():
            for cp in fetch(nxt, 1 - slot):
                cp.start()

        for cp in mine:
            cp.wait()
        for j in range(wgu_bf.shape[0]):
            wgu_bf[j] = wgu_in[slot, :, j * ch:(j + 1) * ch].astype(BF16)
        for j in range(wd_bf.shape[0]):
            wd_bf[j] = wd_in[slot, :, j * ch:(j + 1) * ch].astype(BF16)

    bm = x_ref.shape[0]
    hb = bm // 2

    def run(nr):
        w = x_ref[:nr, :]
        row = lax.broadcasted_iota(I32, w.shape, 0)
        w = jnp.where(row < live, w, jnp.zeros_like(w))
        lo, hi = _unpack_bf16_pairs(w)
        x = jnp.concatenate([lo.astype(BF16), hi.astype(BF16)], axis=1)
        for c in range(nc):
            c0 = slice(c * ch, (c + 1) * ch)
            c1 = slice(f + c * ch, f + (c + 1) * ch)
            glu = _dot(x, wgu_bf[c]) + bgu_ref[0, 0, :, c0]
            lin = _dot(x, wgu_bf[nc + c]) + bgu_ref[0, 0, :, c1]
            glu = jnp.minimum(glu, SWIGLU_LIMIT)
            lin = jnp.clip(lin, -SWIGLU_LIMIT, SWIGLU_LIMIT)
            act_ref[:nr, c0] = (glu * jax.nn.sigmoid(SWIGLU_ALPHA * glu) * (lin + 1.0)).astype(BF16)
        a = act_ref[:nr, :]
        y = jnp.concatenate([_dot(a, wd_bf[n]) for n in range(wd_bf.shape[0])], axis=1)
        o_ref[:nr, :] = _pack_bf16_pairs(y + bd_ref[0, 0])

    @pl.when(live > hb)
    def _():
        run(bm)

    @pl.when(jnp.logical_and(used, live <= hb))
    def _():
        run(hb)
        o_ref[hb:, :] = jnp.zeros((bm - hb, o_ref.shape[1]), o_ref.dtype)

    @pl.when(jnp.logical_not(used))
    def _():
        o_ref[...] = jnp.zeros_like(o_ref)


def _expert_call(tab, buf, w_gu, b_gu, w_down, b_down, layer, bm):
    n_slots, dw = buf.shape
    _, ne, d, f2 = w_gu.shape
    f = f2 // 2
    n_blocks = n_slots // bm
    lw = lambda i, tb: (layer, tb[0, i], 0, 0)
    row = lambda i, tb: (i, 0)
    cw = 256
    return pl.pallas_call(
        functools.partial(_expert_kernel, layer=layer),
        grid_spec=pltpu.PrefetchScalarGridSpec(
            num_scalar_prefetch=1,
            grid=(n_blocks,),
            in_specs=[pl.BlockSpec((bm, dw), row),
                      pl.BlockSpec(memory_space=pl.ANY),
                      pl.BlockSpec((1, 1, 1, f2), lw),
                      pl.BlockSpec(memory_space=pl.ANY),
                      pl.BlockSpec((1, 1, 1, d), lw)],
            out_specs=pl.BlockSpec((bm, dw), row),
            scratch_shapes=[pltpu.VMEM((2, d, f2), F32), pltpu.VMEM((2, f, d), F32),
                            pltpu.VMEM((f2 // cw, d, cw), BF16), pltpu.VMEM((d // cw, f, cw), BF16),
                            pltpu.VMEM((bm, f), BF16),
                            pltpu.SemaphoreType.DMA((2,)), pltpu.SemaphoreType.DMA((2,))]),
        out_shape=jax.ShapeDtypeStruct((n_slots, dw), jnp.uint32),
        compiler_params=_cparams(("arbitrary",)),
        name="moe_experts",
    )(tab, buf, w_gu, b_gu.reshape(b_gu.shape[0], ne, 1, f2), w_down, b_down.reshape(b_down.shape[0], ne, 1, d))


def _expert_kernel_ring(start_ref, nblk_ref, cnt_ref, x_hbm, wgu_ref, bgu_ref, wd_ref, bd_ref, y_hbm,
                        xbuf, ybuf, wgu_bf, wd_bf, act_ref, sem_in, sem_out, *, n_blocks, fused=False):
    e = pl.program_id(0)
    bm = xbuf.shape[1]
    f = act_ref.shape[1]
    ch = 256
    nb = nblk_ref[e]
    start = start_ref[e]
    cnt = cnt_ref[e]

    def rows(j):
        return pl.ds(pl.multiple_of(start + j * bm, bm), bm)

    def fetch(j, slot):
        return pltpu.make_async_copy(x_hbm.at[rows(j)], xbuf.at[slot], sem_in.at[slot])

    def put(j, slot):
        return pltpu.make_async_copy(ybuf.at[slot], y_hbm.at[rows(j)], sem_out.at[slot])

    @pl.when(nb > 0)
    def _():
        fetch(0, 0).start()
        if wgu_bf.ndim == 3:
            for j in range(wgu_bf.shape[0]):
                wgu_bf[j] = wgu_ref[0, 0, :, j * ch:(j + 1) * ch].astype(BF16)
            for j in range(wd_bf.shape[0]):
                wd_bf[j] = wd_ref[0, 0, :, j * ch:(j + 1) * ch].astype(BF16)
        else:
            for j in range(wgu_bf.shape[1] // ch):
                cs = slice(j * ch, (j + 1) * ch)
                wgu_bf[:, cs] = wgu_ref[0, 0, :, cs].astype(BF16)
            for j in range(wd_bf.shape[1] // ch):
                cs = slice(j * ch, (j + 1) * ch)
                wd_bf[:, cs] = wd_ref[0, 0, :, cs].astype(BF16)

    def block(j, carry):
        slot = lax.rem(j, 2)

        @pl.when(j + 1 < nb)
        def _():
            fetch(j + 1, 1 - slot).start()

        fetch(j, slot).wait()

        @pl.when(j >= 2)
        def _():
            put(j - 2, slot).wait()

        w = xbuf[slot]
        row = lax.broadcasted_iota(I32, w.shape, 0)
        w = jnp.where(row < cnt - j * bm, w, jnp.zeros_like(w))
        lo, hi = _unpack_bf16_pairs(w)
        x = jnp.concatenate([lo.astype(BF16), hi.astype(BF16)], axis=1)
        if fused:
            z = _dot(x, wgu_bf[...]) + bgu_ref[0, 0]
            glu = jnp.minimum(z[:, :f], SWIGLU_LIMIT)
            lin = jnp.clip(z[:, f:], -SWIGLU_LIMIT, SWIGLU_LIMIT)
            act = (glu * jax.nn.sigmoid(SWIGLU_ALPHA * glu) * (lin + 1.0)).astype(BF16)
            ybuf[slot] = _pack_bf16_pairs(_dot(act, wd_bf[...]) + bd_ref[0, 0])
            put(j, slot).start()
            return carry
        if wgu_bf.ndim == 3:
            nc = f // ch
            for c in range(nc):
                c0 = slice(c * ch, (c + 1) * ch)
                c1 = slice(f + c * ch, f + (c + 1) * ch)
                glu = _dot(x, wgu_bf[c]) + bgu_ref[0, 0, :, c0]
                lin = _dot(x, wgu_bf[nc + c]) + bgu_ref[0, 0, :, c1]
                glu = jnp.minimum(glu, SWIGLU_LIMIT)
                lin = jnp.clip(lin, -SWIGLU_LIMIT, SWIGLU_LIMIT)
                act_ref[:, c0] = (glu * jax.nn.sigmoid(SWIGLU_ALPHA * glu) * (lin + 1.0)).astype(BF16)
            a = act_ref[...]
            y = jnp.concatenate([_dot(a, wd_bf[n]) for n in range(wd_bf.shape[0])], axis=1)
            ybuf[slot] = _pack_bf16_pairs(y + bd_ref[0, 0])
            put(j, slot).start()
            return carry
        for c in range(f // ch):
            c0 = slice(c * ch, (c + 1) * ch)
            c1 = slice(f + c * ch, f + (c + 1) * ch)
            glu = _dot(x, wgu_bf[:, c0]) + bgu_ref[0, 0, :, c0]
            lin = _dot(x, wgu_bf[:, c1]) + bgu_ref[0, 0, :, c1]
            glu = jnp.minimum(glu, SWIGLU_LIMIT)
            lin = jnp.clip(lin, -SWIGLU_LIMIT, SWIGLU_LIMIT)
            act_ref[:, c0] = (glu * jax.nn.sigmoid(SWIGLU_ALPHA * glu) * (lin + 1.0)).astype(BF16)
        ybuf[slot] = _pack_bf16_pairs(_dot(act_ref[...], wd_bf[...]) + bd_ref[0, 0])
        put(j, slot).start()
        return carry

    lax.fori_loop(0, nb, block, 0)

    @pl.when(nb >= 2)
    def _():
        put(nb - 2, lax.rem(nb, 2)).wait()

    @pl.when(nb >= 1)
    def _():
        put(nb - 1, lax.rem(nb + 1, 2)).wait()

    @pl.when(e == pl.num_programs(0) - 1)
    def _():
        ybuf[0] = jnp.zeros(ybuf.shape[1:], ybuf.dtype)

        def fill(j, carry):
            cp = pltpu.make_async_copy(ybuf.at[0], y_hbm.at[pl.ds(pl.multiple_of(j * bm, bm), bm)], sem_out.at[0])
            cp.start()
            cp.wait()
            return carry

        lax.fori_loop(start // bm + nb, n_blocks, fill, 0)


def _expert_call_ring(start, nblk, cnt, buf, w_gu, b_gu, w_down, b_down, layer, bm, fused=False, chunked=False):
    n_slots, dw = buf.shape
    _, ne, d, f2 = w_gu.shape
    f = f2 // 2
    lw = lambda e, *_: (layer, e, 0, 0)
    return pl.pallas_call(
        functools.partial(_expert_kernel_ring, n_blocks=n_slots // bm, fused=fused),
        grid_spec=pltpu.PrefetchScalarGridSpec(
            num_scalar_prefetch=3,
            grid=(ne,),
            in_specs=[pl.BlockSpec(memory_space=pl.ANY),
                      pl.BlockSpec((1, 1, d, f2), lw),
                      pl.BlockSpec((1, 1, 1, f2), lw),
                      pl.BlockSpec((1, 1, f, d), lw),
                      pl.BlockSpec((1, 1, 1, d), lw)],
            out_specs=pl.BlockSpec(memory_space=pl.ANY),
            scratch_shapes=[pltpu.VMEM((2, bm, dw), jnp.uint32), pltpu.VMEM((2, bm, dw), jnp.uint32),
                            pltpu.VMEM((f2 // 256, d, 256) if chunked else (d, f2), BF16),
                            pltpu.VMEM((d // 256, f, 256) if chunked else (f, d), BF16), pltpu.VMEM((bm, f), BF16),
                            pltpu.SemaphoreType.DMA((2,)), pltpu.SemaphoreType.DMA((2,))]),
        out_shape=jax.ShapeDtypeStruct((n_slots, dw), jnp.uint32),
        compiler_params=_cparams(("arbitrary",)),
        name="moe_experts",
    )(start, nblk, cnt, buf, w_gu, b_gu.reshape(b_gu.shape[0], ne, 1, f2),
      w_down, b_down.reshape(b_down.shape[0], ne, 1, d))


def _combine_kernel(x_ref, m_ref, y_ref, gate_ref, gout_ref, o_ref, *, final):
    m = m_ref[0]
    gate = gate_ref[...]
    half = y_ref.shape[2]
    f_lo, f_hi = None, None
    for k in range(TOP_K):
        lo, hi = _unpack_bf16_pairs(y_ref[k])
        gk = gate[:, k:k + 1]
        f_lo = gk * lo if f_lo is None else f_lo + gk * lo
        f_hi = gk * hi if f_hi is None else f_hi + gk * hi
    x_lo = x_ref[:, :half] + m[5:6, :half] * f_lo
    x_hi = x_ref[:, half:] + m[5:6, half:] * f_hi
    if final:
        ms = (jnp.sum(x_lo * x_lo, axis=-1, keepdims=True) + jnp.sum(x_hi * x_hi, axis=-1, keepdims=True))
        r = lax.rsqrt(ms * (0.5 / half) + NORM_EPS)
        x_lo = x_lo * r * gout_ref[:, :half]
        x_hi = x_hi * r * gout_ref[:, half:]
    o_ref[:, :half] = x_lo
    o_ref[:, half:] = x_hi


def _moe_layer(xs, mods, g, layer, w_router, b_router, w_gu, b_gu, w_down, b_down, geo, g_out, final):
    n, d = xs.shape
    ne = w_router.shape[1]
    f = w_down.shape[2]
    n_tiles = n // TM
    grp = functools.partial(_tile_group, geo=geo)
    const2 = lambda t: (0, 0)
    h, idx, gate, rank, cnt = pl.pallas_call(
        _route_kernel,
        grid=(n_tiles,),
        in_specs=[pl.BlockSpec((TM, d), lambda t: (t, 0)),
                  pl.BlockSpec((1, 6, d), lambda t: (grp(t), 0, 0)),
                  pl.BlockSpec((1, d), const2),
                  pl.BlockSpec((ne, d), const2),
                  pl.BlockSpec((ne, 1), const2)],
        out_specs=[pl.BlockSpec((TM, d // 2), lambda t: (t, 0)),
                   pl.BlockSpec((TOP_K, TM), lambda t: (0, t)),
                   pl.BlockSpec((TOP_K, TM), lambda t: (0, t)),
                   pl.BlockSpec((TOP_K, TM), lambda t: (0, t)),
                   pl.BlockSpec((ne, 1), const2)],
        out_shape=[jax.ShapeDtypeStruct((n, d // 2), jnp.uint32),
                   jax.ShapeDtypeStruct((TOP_K, n), I32),
                   jax.ShapeDtypeStruct((TOP_K, n), F32),
                   jax.ShapeDtypeStruct((TOP_K, n), I32),
                   jax.ShapeDtypeStruct((ne, 1), F32)],
        scratch_shapes=[pltpu.VMEM((ne, 1), F32)],
        compiler_params=_cparams(("arbitrary",)),
        name="moe_route",
    )(xs, mods, g.reshape(1, d), w_router.T.astype(BF16), b_router.reshape(ne, 1))

    bm = MOE_BM
    n_rows = n * TOP_K
    n_blocks = -(-n_rows // bm) + ne
    n_slots = n_blocks * bm
    nbp = -(-n_blocks // 128) * 128
    dest, btab = pl.pallas_call(
        functools.partial(_slot_kernel, bm=bm),
        grid=(n // SLOT_T,),
        in_specs=[pl.BlockSpec((TOP_K, SLOT_T), lambda t: (0, t)),
                  pl.BlockSpec((TOP_K, SLOT_T), lambda t: (0, t)),
                  pl.BlockSpec((ne, 1), const2)],
        out_specs=[pl.BlockSpec((TOP_K, SLOT_T), lambda t: (0, t)),
                   pl.BlockSpec((8, nbp), const2)],
        out_shape=[jax.ShapeDtypeStruct((TOP_K, n), I32),
                   jax.ShapeDtypeStruct((8, nbp), I32)],
        compiler_params=_cparams(("arbitrary",)),
        name="moe_slots",
    )(idx, rank, cnt)

    buf = _sc_scatter_rows(h, dest, n_slots)
    y = _expert_call(btab, buf, w_gu, b_gu, w_down, b_down, layer, bm)
    dy = d // 2
    yk = _sc_gather_rows(y, dest.reshape(-1)).reshape(TOP_K, n, dy)
    return pl.pallas_call(
        functools.partial(_combine_kernel, final=final),
        grid=(n_tiles,),
        in_specs=[pl.BlockSpec((TM, d), lambda t: (t, 0)),
                  pl.BlockSpec((1, 6, d), lambda t: (grp(t), 0, 0)),
                  pl.BlockSpec((TOP_K, TM, dy), lambda t: (0, t, 0)),
                  pl.BlockSpec((TM, TOP_K), lambda t: (t, 0)),
                  pl.BlockSpec((1, d), const2)],
        out_specs=pl.BlockSpec((TM, d), lambda t: (t, 0)),
        out_shape=jax.ShapeDtypeStruct((n, d), F32),
        compiler_params=_cparams(("parallel",)),
        name="moe_combine",
    )(xs, mods, yk, gate.T, g_out.reshape(1, d))


def kernel(x, c, ctx, c_ctx, ada_w, ada_b, norm_mix, norm_ffn, norm_out, a_w_in, a_g_v, a_w_s, a_b_s, a_w_out, b_w_in, b_w_a2, b_b_a, b_g_o, b_w_out, c_w_qkv, c_rpb, c_w_out, moe_w_router, moe_b_router, moe_w_gu, moe_b_gu, moe_w_down, moe_b_down):
    b, seq, d = x.shape
    lctx = ctx.shape[1]
    depth = ada_w.shape[0]
    assert (b * lctx) % TM == 0 and seq % TM == 0 and lctx % GLA_TB == 0 and seq % GLA_TB == 0
    assert seq % (NA_QROWS * GRID_W) == 0 and (b * lctx) % (NA_QROWS * GRID_W) == 0 and lctx % 256 == 0
    assert seq // GRID_W >= NA_WROWS + NA_QROWS and b + 1 <= 8
    geo = (b * lctx // TM, seq // TM, b)
    dims = (b, seq, lctx)

    cond = jnp.zeros((8, d), F32).at[:b].set(c).at[b].set(c_ctx)
    mods_all = _adaln(cond, ada_w, ada_b)[:, :b + 1].reshape(depth, b + 1, 6, d)

    xs = jnp.concatenate([ctx.reshape(b * lctx, d), x.reshape(b * seq, d)], axis=0)
    ctx_tiles = geo[0]
    has_ctx = True
    for i in range(depth):
        kind, j = i % N_MIXERS, i // N_MIXERS
        ctx_later = any(kk % N_MIXERS != 0 for kk in range(i + 1, depth))
        mods = mods_all[i]
        if kind == 0:
            keep_ctx = has_ctx and ctx_later
            skip = 0 if keep_ctx or not has_ctx else ctx_tiles
            geo_i = geo if has_ctx else (0, geo[1], b)
            xs = _gmlp_mixer(xs, mods, norm_mix[i], a_w_in[j], a_g_v[j], a_w_s[j], a_b_s[j], a_w_out[j],
                             skip, xs.shape[0] // TM - skip, geo_i)
            has_ctx = keep_ctx
        elif kind == 1:
            assert has_ctx
            xs = _gla_mixer(xs, mods, norm_mix[i], b_w_in[j], b_w_a2[j], b_b_a[j], b_g_o[j], b_w_out[j], geo, dims)
            if not ctx_later:
                xs = xs[b * lctx:]
                has_ctx = False
        else:
            assert has_ctx
            if ctx_later:
                raise NotImplementedError("context output of the neighbourhood mixer")
            xs = _na_mixer(xs, mods, norm_mix[i], c_w_qkv[j], c_rpb[j], c_w_out[j], geo, dims)
            has_ctx = False
        geo_i = geo if has_ctx else (0, geo[1], b)
        xs = _moe_layer(xs, mods, norm_ffn[i], i, moe_w_router[i], moe_b_router[i], moe_w_gu, moe_b_gu,
                        moe_w_down, moe_b_down, geo_i, norm_out, i == depth - 1)
    if has_ctx:
        xs = xs[b * lctx:]
    return xs.reshape(b, seq, d)
```

```python
import functools

import numpy as np
import jax
import jax.numpy as jnp
from jax import lax
from jax.experimental import pallas as pl
from jax.experimental.pallas import tpu as pltpu
from jax.experimental.pallas import tpu_sc as plsc

F32 = jnp.float32
BF16 = jnp.bfloat16
I32 = jnp.int32

NORM_EPS = 1e-6
GRID_W = 64
N_MIXERS = 3

CHUNK_A = 128
A_GROUPS = 8
GMLP_PART = 256
GLA_HEADS = 4
GLA_RANK = 16
GLA_TAU = 16.0
GLA_CHUNK = 128
ROPE_BASE = 10000.0
NA_HEADS = 16
NA_KH = 8
NA_KW = 16
NEG_INF = -1e30
NA_QROWS = 4
NA_WROWS = 12
NA_PAIRS_PER_STEP = 4
TOP_K = 4
SWIGLU_LIMIT = 7.0
SWIGLU_ALPHA = 1.702
MOE_BM = 1024
MOE_SUB = 256

TM = 512
GLA_TB = 256
VMEM_LIMIT = 56 * 1024 * 1024


def _cparams(sem):
    return pltpu.CompilerParams(dimension_semantics=sem, vmem_limit_bytes=VMEM_LIMIT)


def _dot(a, b):
    return jnp.dot(a, b, preferred_element_type=F32)


def _dot_nt(a, b):
    return lax.dot_general(a, b, (((1,), (1,)), ((), ())), preferred_element_type=F32)


def _dot_tn(a, b):
    return lax.dot_general(a, b, (((0,), (0,)), ((), ())), preferred_element_type=F32)


def _rms(x, g):
    return x * lax.rsqrt(jnp.mean(x * x, axis=-1, keepdims=True) + NORM_EPS) * g


def _modulate(x, g, shift, scale):
    return _rms(x, g) * (1.0 + scale) + shift


def _ada_kernel(s_ref, w_ref, b_ref, o_ref):
    s = s_ref[...]
    s = s * jax.nn.sigmoid(s)
    o_ref[0] = jnp.dot(s, w_ref[0], preferred_element_type=F32,
                       precision=lax.Precision.HIGHEST) + b_ref[0]


def _adaln(cond, ada_w, ada_b):
    depth, d, n6 = ada_w.shape
    bn = n6 // 4
    return pl.pallas_call(
        _ada_kernel,
        grid=(depth, n6 // bn),
        in_specs=[pl.BlockSpec((8, d), lambda i, j: (0, 0)),
                  pl.BlockSpec((1, d, bn), lambda i, j: (i, 0, j)),
                  pl.BlockSpec((1, 1, bn), lambda i, j: (i, 0, j))],
        out_specs=pl.BlockSpec((1, 8, bn), lambda i, j: (i, 0, j)),
        out_shape=jax.ShapeDtypeStruct((depth, 8, n6), F32),
        compiler_params=_cparams(("parallel", "parallel")),
        name="adaln",
    )(cond, ada_w, ada_b.reshape(depth, 1, n6))


def _gelu(z):
    return 0.5 * z * (1.0 + lax.erf(z * np.float32(np.sqrt(0.5))))


def _gmlp_kernel(x_ref, m_ref, g_ref, win_ref, gv_ref, ws_ref, bs_ref, wout_ref, o_ref):
    m = m_ref[0]
    a = gv_ref.shape[1]
    gw = a // A_GROUPS
    npart = x_ref.shape[0] // GMLP_PART
    parts = [slice(p * GMLP_PART, (p + 1) * GMLP_PART) for p in range(npart)]
    z = [_dot(_modulate(x_ref[rs, :], g_ref[...], m[0:1], m[1:2]).astype(BF16), win_ref[...]) for rs in parts]
    z = [_gelu(zp) for zp in z]
    for rs, zp in zip(parts, z):
        u = zp[:, :a]
        v = _rms(zp[:, a:], gv_ref[...]).astype(BF16)
        rows = []
        for c in range(GMLP_PART // CHUNK_A):
            cols = [_dot(ws_ref[g], v[c * CHUNK_A:(c + 1) * CHUNK_A, g * gw:(g + 1) * gw])
                    for g in range(A_GROUPS)]
            rows.append(jnp.concatenate(cols, axis=1) + bs_ref[...])
        s = jnp.concatenate(rows, axis=0)
        y = _dot((u * s).astype(BF16), wout_ref[...])
        o_ref[rs, :] = x_ref[rs, :] + m[2:3] * y


def _gmlp_mixer(xs, mods, g, w_in, g_v, w_s, b_s, w_out, tile0, n_tiles, geo):
    d = xs.shape[1]
    a = g_v.shape[0]
    gw = a // A_GROUPS
    bias = jnp.repeat(b_s.T, gw, axis=1)
    grp = functools.partial(_tile_group, geo=geo, tile0=tile0)
    const2 = lambda t: (0, 0)
    return pl.pallas_call(
        _gmlp_kernel,
        grid=(n_tiles,),
        in_specs=[pl.BlockSpec((TM, d), lambda t: (t + tile0, 0)),
                  pl.BlockSpec((1, 6, d), lambda t: (grp(t), 0, 0)),
                  pl.BlockSpec((1, d), const2),
                  pl.BlockSpec((d, 2 * a), const2),
                  pl.BlockSpec((1, a), const2),
                  pl.BlockSpec((A_GROUPS, CHUNK_A, CHUNK_A), lambda t: (0, 0, 0)),
                  pl.BlockSpec((CHUNK_A, a), const2),
                  pl.BlockSpec((a, d), const2)],
        out_specs=pl.BlockSpec((TM, d), lambda t: (t, 0)),
        out_shape=jax.ShapeDtypeStruct((n_tiles * TM, d), F32),
        compiler_params=_cparams(("parallel",)),
        name="gmlp_mixer",
    )(xs, mods, g.reshape(1, d), w_in.astype(BF16), g_v.reshape(1, a), w_s.astype(BF16),
      bias, w_out.astype(BF16))


def _tile_group(t, geo, tile0=0):
    n_ctx_tiles, tiles_per_batch, nb = geo
    tt = t + tile0
    return jnp.where(tt < n_ctx_tiles, nb, (tt - n_ctx_tiles) // tiles_per_batch)


def _dot_f32(tri_bf16, x):
    hi = x.astype(BF16)
    r1 = x - hi.astype(F32)
    mid = r1.astype(BF16)
    lo = (r1 - mid.astype(F32)).astype(BF16)
    return _dot(tri_bf16, hi) + _dot(tri_bf16, mid) + _dot(tri_bf16, lo)


def _gla_proj_kernel(x_ref, m_ref, g_ref, wq_ref, wk_ref, wv_ref, wg_ref, wr_ref, wa_ref, ba_ref,
                     cos_ref, sin_ref, q_ref, k_ref, v_ref, go_ref, la_ref):
    x = x_ref[...]
    m = m_ref[0]
    h = _modulate(x, g_ref[...], m[0:1], m[1:2]).astype(BF16)
    kdim = wq_ref.shape[1]
    hk = kdim // GLA_HEADS
    r = _dot(h, wr_ref[...])
    z = _dot(r.astype(BF16), wa_ref[...]) + ba_ref[...]
    q = _dot(h, wq_ref[...]) * np.float32(hk ** -0.5)
    k = _dot(h, wk_ref[...])
    la_ref[...] = jax.nn.log_sigmoid(z) * np.float32(1.0 / GLA_TAU)
    v_ref[...] = _dot(h, wv_ref[...]).astype(BF16)
    go_ref[...] = _dot(h, wg_ref[...])

    cos = jnp.concatenate([cos_ref[...]] * GLA_HEADS, axis=1)
    sin = jnp.concatenate([sin_ref[...]] * GLA_HEADS, axis=1)
    nf = hk // 4
    lane = lax.broadcasted_iota(I32, q.shape, 1)
    first = (lane % (2 * nf)) < nf

    def rope(t):
        up = pltpu.roll(t, kdim - nf, 1)
        dn = pltpu.roll(t, nf, 1)
        return t * cos + jnp.where(first, up, dn) * sin

    q_ref[...] = rope(q)
    k_ref[...] = rope(k)


def _gla_decays(q_ref, k_ref, la_ref, n, rev):
    c = GLA_CHUNK
    ri = lax.broadcasted_iota(I32, (c, c), 0)
    ci = lax.broadcasted_iota(I32, (c, c), 1)
    keep = (ci >= ri) if rev else (ci <= ri)
    rows = slice(n * c, (n + 1) * c)
    cum = _dot_f32(keep.astype(BF16), la_ref[rows, :])
    last = cum[0:1] if rev else cum[c - 1:c]
    q = q_ref[rows, :]
    k = k_ref[rows, :]
    mid = cum[c // 2:c // 2 + 1]
    return dict(rows=rows, keep=keep, dec=jnp.exp(last),
                q_dec=(q * jnp.exp(cum)).astype(BF16),
                q_att=(q * jnp.exp(cum - mid)).astype(BF16),
                k_att=(k * jnp.exp(mid - cum)).astype(BF16),
                k_end=(k * jnp.exp(last - cum)).astype(BF16))


def _gla_scan_kernel(qf_ref, kf_ref, vf_ref, laf_ref, qb_ref, kb_ref, vb_ref, lab_ref,
                     of_ref, ob_ref, stf_ref, stb_ref):
    s = pl.program_id(1)

    @pl.when(s == 0)
    def _():
        stf_ref[...] = jnp.zeros_like(stf_ref)
        stb_ref[...] = jnp.zeros_like(stb_ref)

    nch = qf_ref.shape[0] // GLA_CHUNK
    hk = qf_ref.shape[1] // GLA_HEADS
    hv = vf_ref.shape[1] // GLA_HEADS
    dirs = ((qf_ref, kf_ref, vf_ref, laf_ref, of_ref, stf_ref, False),
            (qb_ref, kb_ref, vb_ref, lab_ref, ob_ref, stb_ref, True))
    heads = [(slice(h * hk, (h + 1) * hk), slice(h * hv, (h + 1) * hv)) for h in range(GLA_HEADS)]
    for n in range(nch):
        w = [_gla_decays(q_ref, k_ref, la_ref, nch - 1 - n if rev else n, rev)
             for q_ref, k_ref, _, la_ref, _, _, rev in dirs]
        att = [[jnp.where(w[d]["keep"], _dot_nt(w[d]["q_att"][:, ks], w[d]["k_att"][:, ks]), 0.0).astype(BF16)
                for ks, _ in heads] for d in range(2)]
        for d, (_, _, v_ref, _, o_ref, st_ref, _) in enumerate(dirs):
            for h, (ks, vs) in enumerate(heads):
                o_ref[w[d]["rows"], vs] = (_dot(att[d][h], v_ref[w[d]["rows"], vs])
                                           + _dot_nt(w[d]["q_dec"][:, ks], st_ref[h].astype(BF16)))
        for d, (_, _, v_ref, _, _, st_ref, _) in enumerate(dirs):
            for h, (ks, vs) in enumerate(heads):
                st_ref[h] = (st_ref[h] * w[d]["dec"][:, ks]
                             + _dot_tn(v_ref[w[d]["rows"], vs], w[d]["k_end"][:, ks]))


def _gla_out_kernel(x_ref, m_ref, of_ref, ob_ref, go_ref, gn_ref, wout_ref, o_ref):
    x = x_ref[...]
    m = m_ref[0]
    o = of_ref[...] + ob_ref[...]
    hv = gn_ref.shape[1]
    parts = [_rms(o[:, h * hv:(h + 1) * hv], gn_ref[...]) for h in range(GLA_HEADS)]
    o = jnp.concatenate(parts, axis=1)
    gate = go_ref[...]
    y = _dot((o * (gate * jax.nn.sigmoid(gate))).astype(BF16), wout_ref[...])
    o_ref[...] = x + m[2:3] * y


def _gla_mixer(xs, mods, g, w_in, w_a2, b_a, g_o, w_out, geo, dims):
    b, seq, lctx = dims
    nt, d = xs.shape
    n_tiles = nt // TM
    kdim = w_a2.shape[2]
    vdim = g_o.shape[0] * GLA_HEADS
    hk = kdim // GLA_HEADS
    nf = hk // 4
    wq = w_in[:, :kdim].astype(BF16)
    wk = w_in[:, kdim:2 * kdim].astype(BF16)
    wv = w_in[:, 2 * kdim:2 * kdim + vdim].astype(BF16)
    wg = w_in[:, 2 * kdim + vdim:2 * kdim + 2 * vdim].astype(BF16)
    wr = jnp.pad(w_in[:, 2 * kdim + 2 * vdim:], ((0, 0), (0, 128 - 2 * GLA_RANK))).astype(BF16)
    wa = jnp.zeros((128, 2 * kdim), F32)
    wa = wa.at[:GLA_RANK, :kdim].set(w_a2[0]).at[GLA_RANK:2 * GLA_RANK, kdim:].set(w_a2[1]).astype(BF16)
    ba = b_a.reshape(1, 2 * kdim)

    tpos = jnp.arange(seq)
    freqs = jnp.power(ROPE_BASE, -jnp.arange(nf, dtype=F32) / nf)
    ar = (tpos // GRID_W).astype(F32)[:, None] * freqs
    ac = (tpos % GRID_W).astype(F32)[:, None] * freqs
    cos = jnp.concatenate([jnp.cos(ar), jnp.cos(ar), jnp.cos(ac), jnp.cos(ac)], axis=1)
    sin = jnp.concatenate([-jnp.sin(ar), jnp.sin(ar), -jnp.sin(ac), jnp.sin(ac)], axis=1)
    cos = jnp.concatenate([jnp.ones((TM, hk), F32), cos], axis=0)
    sin = jnp.concatenate([jnp.zeros((TM, hk), F32), sin], axis=0)

    n_ctx_tiles, tiles_per_batch, _ = geo
    grp = functools.partial(_tile_group, geo=geo)

    def rope_blk(t):
        return (jnp.where(t < n_ctx_tiles, 0, 1 + (t - n_ctx_tiles) % tiles_per_batch), 0)

    const2 = lambda t: (0, 0)
    row = lambda t: (t, 0)
    q, k, v, go, la = pl.pallas_call(
        _gla_proj_kernel,
        grid=(n_tiles,),
        in_specs=[pl.BlockSpec((TM, d), row),
                  pl.BlockSpec((1, 6, d), lambda t: (grp(t), 0, 0)),
                  pl.BlockSpec((1, d), const2),
                  pl.BlockSpec((d, kdim), const2), pl.BlockSpec((d, kdim), const2),
                  pl.BlockSpec((d, vdim), const2), pl.BlockSpec((d, vdim), const2),
                  pl.BlockSpec((d, 128), const2), pl.BlockSpec((128, 2 * kdim), const2),
                  pl.BlockSpec((1, 2 * kdim), const2),
                  pl.BlockSpec((TM, hk), rope_blk), pl.BlockSpec((TM, hk), rope_blk)],
        out_specs=[pl.BlockSpec((TM, kdim), row), pl.BlockSpec((TM, kdim), row),
                   pl.BlockSpec((TM, vdim), row), pl.BlockSpec((TM, vdim), row),
                   pl.BlockSpec((TM, 2 * kdim), row)],
        out_shape=[jax.ShapeDtypeStruct((nt, kdim), F32), jax.ShapeDtypeStruct((nt, kdim), F32),
                   jax.ShapeDtypeStruct((nt, vdim), BF16), jax.ShapeDtypeStruct((nt, vdim), F32),
                   jax.ShapeDtypeStruct((nt, 2 * kdim), F32)],
        compiler_params=_cparams(("parallel",)),
        name="gla_proj",
    )(xs, mods, g.reshape(1, d), wq, wk, wv, wg, wr, wa, ba, cos, sin)

    tb = GLA_TB
    ctx_steps = lctx // tb
    lat_steps = seq // tb
    steps = ctx_steps + lat_steps

    def blk(bi, s, rev):
        if rev:
            cs, ls = ctx_steps - 1 - s, lat_steps - 1 - (s - ctx_steps)
        else:
            cs, ls = s, s - ctx_steps
        return jnp.where(s < ctx_steps, bi * ctx_steps + cs, b * ctx_steps + bi * lat_steps + ls)

    def dir_specs(rev):
        row_blk = lambda bi, s: (blk(bi, s, rev), 0)
        return [pl.BlockSpec((tb, kdim), row_blk), pl.BlockSpec((tb, kdim), row_blk),
                pl.BlockSpec((tb, vdim), row_blk),
                pl.BlockSpec((tb, kdim), lambda bi, s: (blk(bi, s, rev), 1 if rev else 0))]

    state = pltpu.VMEM((GLA_HEADS, vdim // GLA_HEADS, hk), F32)
    o_f, o_b = pl.pallas_call(
        _gla_scan_kernel,
        grid=(b, steps),
        in_specs=dir_specs(False) + dir_specs(True),
        out_specs=[pl.BlockSpec((tb, vdim), lambda bi, s: (blk(bi, s, False), 0)),
                   pl.BlockSpec((tb, vdim), lambda bi, s: (blk(bi, s, True), 0))],
        out_shape=[jax.ShapeDtypeStruct((nt, vdim), F32)] * 2,
        scratch_shapes=[state, state],
        compiler_params=_cparams(("parallel", "arbitrary")),
        name="gla_scan",
    )(q, k, v, la, q, k, v, la)

    return pl.pallas_call(
        _gla_out_kernel,
        grid=(n_tiles,),
        in_specs=[pl.BlockSpec((TM, d), row),
                  pl.BlockSpec((1, 6, d), lambda t: (grp(t), 0, 0)),
                  pl.BlockSpec((TM, vdim), row), pl.BlockSpec((TM, vdim), row),
                  pl.BlockSpec((TM, vdim), row),
                  pl.BlockSpec((1, vdim // GLA_HEADS), const2),
                  pl.BlockSpec((vdim, d), const2)],
        out_specs=pl.BlockSpec((TM, d), row),
        out_shape=jax.ShapeDtypeStruct((nt, d), F32),
        compiler_params=_cparams(("parallel",)),
        name="gla_out",
    )(xs, mods, o_f, o_b, go, g_o.reshape(1, -1), w_out.astype(BF16))


def _na_proj_kernel(x_ref, m_ref, g_ref, w_ref, q_ref, k_ref, v_ref):
    x = x_ref[...]
    m = m_ref[0]
    d = x.shape[1]
    h = _modulate(x, g_ref[...], m[0:1], m[1:2]).astype(BF16)
    qkv = _dot(h, w_ref[...])
    hd = d // NA_HEADS
    for p in range(q_ref.shape[0]):
        cs = slice(p * 128, (p + 1) * 128)
        q_ref[p] = (qkv[:, cs] * np.float32(hd ** -0.5)).astype(BF16)
        k_ref[p] = qkv[:, d + p * 128:d + (p + 1) * 128].astype(BF16)
        v_ref[p] = qkv[:, 2 * d + p * 128:2 * d + (p + 1) * 128].astype(BF16)


def _na_attn_kernel(tbl_ref, q_ref, *refs, n_tiles, nwin):
    k_refs, v_refs = refs[:nwin], refs[nwin:2 * nwin]
    kc_ref, vc_ref, bias_ref, o_ref, s_ref, p_ref = refs[2 * nwin:]
    t = pl.program_id(2)
    typ = jnp.where(t == 0, 0, jnp.where(t == n_tiles - 1, 2, 1))
    npair = NA_WROWS // 2
    nlat = NA_WROWS * GRID_W
    lane = lax.broadcasted_iota(I32, q_ref.shape[1:], 1)
    npp, tq = q_ref.shape[0], q_ref.shape[1]
    for pp in range(npp):
        q = q_ref[pp]
        k_all = jnp.concatenate([r[pp] for r in k_refs] + [kc_ref[pp]], axis=0)
        zero = jnp.zeros_like(q)
        q2 = jnp.concatenate([jnp.where(lane < 64, q, zero), jnp.where(lane >= 64, q, zero)], axis=0)
        s_ref[pp] = _dot_nt(q2, k_all)
    for pp in range(npp):
        for hh in range(2):
            for dr in range(NA_QROWS):
                rs = slice(hh * tq + dr * GRID_W, hh * tq + (dr + 1) * GRID_W)
                lat = [s_ref[pp, rs, j * 128:(j + 1) * 128]
                       + bias_ref[2 * pp + hh, tbl_ref[typ * (NA_QROWS * npair) + dr * npair + j]]
                       for j in range(npair)]
                sb = jnp.concatenate(lat + [s_ref[pp, rs, nlat:]], axis=1)
                e = jnp.exp(sb - jnp.max(sb, axis=1, keepdims=True))
                p_ref[pp, rs, :] = (e * (1.0 / jnp.sum(e, axis=1, keepdims=True))).astype(BF16)
    outs = []
    for pp in range(npp):
        v_all = jnp.concatenate([r[pp] for r in v_refs] + [vc_ref[pp]], axis=0)
        o2 = _dot(p_ref[pp], v_all)
        outs.append(jnp.where(lane < 64, o2[:tq], o2[tq:]))
    o_ref[...] = jnp.concatenate(outs, axis=1).astype(BF16)


def _na_out_kernel(x_ref, m_ref, a_ref, w_ref, o_ref):
    m = m_ref[0]
    o_ref[...] = x_ref[...] + m[2:3] * _dot(a_ref[...], w_ref[...])


def _na_tables(rows):
    npair = NA_WROWS // 2
    tbl = np.zeros((3, NA_QROWS, npair), np.int32)
    for typ, r_base in enumerate((0, NA_QROWS, rows - NA_QROWS)):
        w0 = int(np.clip(r_base - NA_KH // 2, 0, rows - NA_WROWS))
        for dr in range(NA_QROWS):
            r = r_base + dr
            r0 = int(np.clip(r - NA_KH // 2, 0, rows - NA_KH))
            for j in range(npair):
                kr = (w0 + 2 * j, w0 + 2 * j + 1)
                ok = [r0 <= x < r0 + NA_KH for x in kr]
                ri = [x - r + NA_KH - 1 for x in kr]
                if ok[0] and ok[1]:
                    e = ri[0]
                elif ok[0]:
                    e = 16 + ri[0]
                elif ok[1]:
                    e = 32 + ri[1]
                else:
                    e = 63
                tbl[typ, dr, j] = e
    return tbl.reshape(-1)


def _na_bias_kernel(rpb_ref, o_ref, *, nr, nw):
    h = pl.program_id(0)
    c = lax.broadcasted_iota(I32, (GRID_W, GRID_W), 0)
    k = lax.broadcasted_iota(I32, (GRID_W, GRID_W), 1)
    cstart = jnp.clip(c - NA_KW // 2, 0, GRID_W - NA_KW)
    ok = jnp.logical_and(k >= cstart, k < cstart + NA_KW)
    rel = k - c + NA_KW - 1
    neg = jnp.full((GRID_W, GRID_W), NEG_INF, F32)
    cbs = []
    for r in range(nr):
        acc = neg
        for w in range(nw):
            acc = jnp.where(rel == w, rpb_ref[(h * nr + r) * nw + w], acc)
        cbs.append(jnp.where(ok, acc, neg))
    cbs += [neg] * (17 - nr)
    for r in range(16):
        o_ref[0, r] = jnp.concatenate([cbs[r], cbs[r + 1]], axis=1)
        o_ref[0, 16 + r] = jnp.concatenate([cbs[r], neg], axis=1)
        o_ref[0, 32 + r] = jnp.concatenate([neg, cbs[r]], axis=1)
        o_ref[0, 48 + r] = jnp.concatenate([neg, neg], axis=1)


def _na_bias_table(rpb):
    nh, nr, nw = rpb.shape
    return pl.pallas_call(
        functools.partial(_na_bias_kernel, nr=nr, nw=nw),
        grid=(nh,),
        in_specs=[pl.BlockSpec(memory_space=pltpu.SMEM)],
        out_specs=pl.BlockSpec((1, 64, GRID_W, 2 * GRID_W), lambda h: (h, 0, 0, 0)),
        out_shape=jax.ShapeDtypeStruct((nh, 64, GRID_W, 2 * GRID_W), F32),
        compiler_params=_cparams(("parallel",)),
        name="na_bias_table",
    )(rpb.reshape(-1).astype(F32))


def _na_mixer(xs, mods, g, w_qkv, rpb, w_out, geo, dims):
    b, seq, lctx = dims
    nt, d = xs.shape
    n_tiles_tok = nt // TM
    grp = functools.partial(_tile_group, geo=geo)
    npairs = d // 128
    const2 = lambda t: (0, 0)
    q, k, v = pl.pallas_call(
        _na_proj_kernel,
        grid=(n_tiles_tok,),
        in_specs=[pl.BlockSpec((TM, d), lambda t: (t, 0)),
                  pl.BlockSpec((1, 6, d), lambda t: (grp(t), 0, 0)),
                  pl.BlockSpec((1, d), const2),
                  pl.BlockSpec((d, 3 * d), const2)],
        out_specs=[pl.BlockSpec((npairs, TM, 128), lambda t: (0, t, 0))] * 3,
        out_shape=[jax.ShapeDtypeStruct((npairs, nt, 128), BF16)] * 3,
        compiler_params=_cparams(("parallel",)),
        name="na_proj",
    )(xs, mods, g.reshape(1, d), w_qkv.astype(BF16))

    rows = seq // GRID_W
    tq = NA_QROWS * GRID_W
    n_tiles = rows // NA_QROWS
    wb = 256
    nwin = NA_WROWS * GRID_W // wb
    lat0 = b * lctx
    tbl = jnp.asarray(_na_tables(rows))
    bias = _na_bias_table(rpb)

    pp = NA_PAIRS_PER_STEP
    assert npairs % pp == 0 and (NA_QROWS * GRID_W) % wb == 0 and (NA_KH // 2 * GRID_W) % wb == 0

    def win(i):
        def f(p, bi, t, tbl_ref):
            w = jnp.clip(t * (NA_QROWS * GRID_W // wb) - NA_KH // 2 * GRID_W // wb, 0, seq // wb - nwin)
            return (p, (lat0 + bi * seq) // wb + w + i, 0)
        return f

    kv_specs = [pl.BlockSpec((pp, wb, 128), win(i)) for i in range(nwin)]
    attn = pl.pallas_call(
        functools.partial(_na_attn_kernel, n_tiles=n_tiles, nwin=nwin),
        grid_spec=pltpu.PrefetchScalarGridSpec(
            num_scalar_prefetch=1,
            grid=(npairs // pp, b, n_tiles),
            in_specs=[pl.BlockSpec((pp, tq, 128), lambda p, bi, t, tr: (p, (lat0 + bi * seq) // tq + t, 0))]
                     + kv_specs + kv_specs
                     + [pl.BlockSpec((pp, lctx, 128), lambda p, bi, t, tr: (p, bi, 0)),
                        pl.BlockSpec((pp, lctx, 128), lambda p, bi, t, tr: (p, bi, 0)),
                        pl.BlockSpec((2 * pp, 64, GRID_W, 2 * GRID_W), lambda p, bi, t, tr: (p, 0, 0, 0),
                                     pipeline_mode=pl.Buffered(1))],
            out_specs=pl.BlockSpec((tq, 128 * pp), lambda p, bi, t, tr: (bi * n_tiles + t, p)),
            scratch_shapes=[pltpu.VMEM((pp, 2 * tq, NA_WROWS * GRID_W + lctx), F32),
                            pltpu.VMEM((pp, 2 * tq, NA_WROWS * GRID_W + lctx), BF16)]),
        out_shape=jax.ShapeDtypeStruct((b * seq, d), BF16),
        compiler_params=_cparams(("parallel", "parallel", "arbitrary")),
        name="na_attn",
    )(tbl, q, *([k] * nwin), *([v] * nwin), k, v, bias)

    n_lat_tiles = b * seq // TM
    tile0 = lat0 // TM
    grp_l = functools.partial(_tile_group, geo=geo, tile0=tile0)
    return pl.pallas_call(
        _na_out_kernel,
        grid=(n_lat_tiles,),
        in_specs=[pl.BlockSpec((TM, d), lambda t: (t + tile0, 0)),
                  pl.BlockSpec((1, 6, d), lambda t: (grp_l(t), 0, 0)),
                  pl.BlockSpec((TM, d), lambda t: (t, 0)),
                  pl.BlockSpec((d, d), const2)],
        out_specs=pl.BlockSpec((TM, d), lambda t: (t, 0)),
        out_shape=jax.ShapeDtypeStruct((b * seq, d), F32),
        compiler_params=_cparams(("parallel",)),
        name="na_out",
    )(xs, mods, attn, w_out.astype(BF16))


SC_CORES = 2
SC_SUBCORES = 16
SC_WORKERS = SC_CORES * SC_SUBCORES
SLOT_T = 512


def _sc_mesh():
    return plsc.VectorSubcoreMesh(core_axis_name="c", subcore_axis_name="s")


def _sc_chunk(per_worker, max_chunk):
    return max(c for c in range(8, max_chunk + 1, 8) if per_worker % (2 * c) == 0)


def _sc_gather_rows(table, idx):
    dd = table.shape[1]
    bsz = idx.shape[0]
    per_w = bsz // SC_WORKERS
    assert per_w * SC_WORKERS == bsz
    chunk = _sc_chunk(per_w, 64)
    n_chunks = per_w // chunk

    def body(table_hbm, idx_hbm, out_hbm, idx_v, rows0, rows1, g0, g1, w0, w1):
        wid = lax.axis_index("s") * SC_CORES + lax.axis_index("c")
        pltpu.sync_copy(idx_hbm.at[wid], idx_v)
        base = wid * per_w

        def out_rows(j):
            return out_hbm.at[pl.ds(pl.multiple_of(base + j * chunk, 8), chunk)]

        def step(i, carry):
            j0, j1 = 2 * i, 2 * i + 1
            ga = pltpu.async_copy(table_hbm.at[idx_v.at[j0]], rows0, g0)
            gb = pltpu.async_copy(table_hbm.at[idx_v.at[j1]], rows1, g1)
            ga.wait()
            wa = pltpu.async_copy(rows0, out_rows(j0), w0)
            gb.wait()
            wb = pltpu.async_copy(rows1, out_rows(j1), w1)
            wa.wait()
            wb.wait()
            return carry

        lax.fori_loop(0, n_chunks // 2, step, 0)

    return pl.kernel(
        body, out_type=jax.ShapeDtypeStruct((bsz, dd), table.dtype), mesh=_sc_mesh(),
        scratch_types=[pltpu.VMEM((n_chunks, chunk), I32),
                       pltpu.VMEM((chunk, dd), table.dtype), pltpu.VMEM((chunk, dd), table.dtype),
                       pltpu.SemaphoreType.DMA, pltpu.SemaphoreType.DMA,
                       pltpu.SemaphoreType.DMA, pltpu.SemaphoreType.DMA],
        name="sc_gather_rows",
    )(table, idx.reshape(SC_WORKERS, n_chunks, chunk))


def _sc_scatter_rows(rows, dest, n_out):
    n, dd = rows.shape
    kk = dest.shape[0]
    per_w = n // SC_WORKERS
    assert per_w * SC_WORKERS == n
    chunk = _sc_chunk(per_w, 64)
    n_chunks = per_w // chunk
    dest_w = dest.reshape(kk, SC_WORKERS, n_chunks, chunk).transpose(1, 2, 0, 3)
    dest_w = dest_w.reshape(SC_WORKERS, n_chunks * kk, chunk)

    def body(rows_hbm, dest_hbm, out_hbm, idx_v, rows0, rows1, r0, r1, s0, s1):
        wid = lax.axis_index("s") * SC_CORES + lax.axis_index("c")
        pltpu.sync_copy(dest_hbm.at[wid], idx_v)
        base = wid * per_w

        def in_rows(j):
            return rows_hbm.at[pl.ds(pl.multiple_of(base + j * chunk, 8), chunk)]

        def step(i, carry):
            j0, j1 = 2 * i, 2 * i + 1
            ra = pltpu.async_copy(in_rows(j0), rows0, r0)
            rb = pltpu.async_copy(in_rows(j1), rows1, r1)
            ra.wait()
            sa = [pltpu.async_copy(rows0, out_hbm.at[idx_v.at[j0 * kk + k]], s0) for k in range(kk)]
            rb.wait()
            sb = [pltpu.async_copy(rows1, out_hbm.at[idx_v.at[j1 * kk + k]], s1) for k in range(kk)]
            for cp in sa + sb:
                cp.wait()
            return carry

        lax.fori_loop(0, n_chunks // 2, step, 0)

    return pl.kernel(
        body, out_type=jax.ShapeDtypeStruct((n_out, dd), rows.dtype), mesh=_sc_mesh(),
        scratch_types=[pltpu.VMEM((n_chunks * kk, chunk), I32),
                       pltpu.VMEM((chunk, dd), rows.dtype), pltpu.VMEM((chunk, dd), rows.dtype),
                       pltpu.SemaphoreType.DMA, pltpu.SemaphoreType.DMA,
                       pltpu.SemaphoreType.DMA, pltpu.SemaphoreType.DMA],
        name="sc_scatter_rows",
    )(rows, dest_w)


def _prefix_sum_rows(col):
    nr = col.shape[0]
    acc = jnp.broadcast_to(col, (nr, 128))
    row = lax.broadcasted_iota(I32, (nr, 128), 0)
    s = 1
    while s < nr:
        acc = acc + jnp.where(row >= s, pltpu.roll(acc, s, 0), 0.0)
        s *= 2
    return acc[:, 0:1]


def _slot_kernel(idx_ref, rank_ref, cnt_ref, dest_ref, btab_ref, *, bm):
    cnt = cnt_ref[...]
    ne = cnt.shape[0]
    padded = jnp.floor((cnt + (bm - 1.0)) * (1.0 / bm)) * bm
    pad_end = _prefix_sum_rows(padded)
    pad_start = pad_end - padded
    idx = idx_ref[...]
    e_iota = lax.broadcasted_iota(I32, (ne, idx.shape[1]), 0)
    starts = [jnp.sum(jnp.where(e_iota == idx[k:k + 1], pad_start, 0.0), axis=0, keepdims=True)
              for k in range(idx.shape[0])]
    dest_ref[...] = jnp.concatenate(starts, axis=0).astype(I32) + rank_ref[...]

    nbp = btab_ref.shape[1]
    bstart = lax.broadcasted_iota(I32, (ne, nbp), 1).astype(F32) * bm
    be = jnp.minimum(jnp.sum((pad_end <= bstart).astype(F32), axis=0, keepdims=True), ne - 1.0)
    mine = lax.broadcasted_iota(I32, (ne, nbp), 0).astype(F32) == be
    pick = lambda col: jnp.sum(jnp.where(mine, col, 0.0), axis=0, keepdims=True)
    live = jnp.clip(pick(pad_start + cnt) - bstart[0:1], 0.0, bm)
    r = lax.broadcasted_iota(I32, (ne, ne), 0)
    c = lax.broadcasted_iota(I32, (ne, ne), 1)
    ends_on_lanes = _dot_f32(jnp.ones((ne, ne), BF16), jnp.where(r == c, pad_end, 0.0))
    nxt_e = jnp.sum((ends_on_lanes <= pad_end).astype(F32), axis=1, keepdims=True)
    nxt_e = jnp.where(nxt_e < ne, nxt_e, -1.0)
    ordinal = _prefix_sum_rows((cnt > 0).astype(F32)) - 1.0
    slot_e = ordinal - 2.0 * jnp.floor(ordinal * 0.5)
    rowi = lax.broadcasted_iota(I32, btab_ref.shape, 0)
    tab = jnp.where(rowi == 0, be, jnp.where(rowi == 1, live, jnp.where(rowi == 2, pick(nxt_e), pick(slot_e))))
    btab_ref[...] = tab.astype(I32)


def _route_kernel(x_ref, m_ref, g_ref, wr_ref, br_ref, h_ref, idx_ref, gate_ref, rank_ref, cnt_ref, run_ref):
    t = pl.program_id(0)

    @pl.when(t == 0)
    def _():
        run_ref[...] = jnp.zeros_like(run_ref)

    x = x_ref[...]
    m = m_ref[0]
    h = _modulate(x, g_ref[...], m[3:4], m[4:5]).astype(BF16)
    h_ref[...] = _pack_bf16_pairs(h)
    logits = _dot_nt(wr_ref[...], h) + br_ref[...]
    ne, tm = logits.shape
    e_iota = lax.broadcasted_iota(I32, (ne, tm), 0)
    vals, idxs = [], []
    l = logits
    for _ in range(TOP_K):
        mk = jnp.max(l, axis=0, keepdims=True)
        ik = jnp.min(jnp.where(l == mk, e_iota, ne), axis=0, keepdims=True)
        vals.append(mk)
        idxs.append(ik)
        l = jnp.where(e_iota == ik, -jnp.inf, l)
    top_val = jnp.concatenate(vals, axis=0)
    ex = jnp.exp(top_val - vals[0])
    gate_ref[...] = ex / jnp.sum(ex, axis=0, keepdims=True)
    idx_ref[...] = jnp.concatenate(idxs, axis=0)

    hits = [e_iota == ik for ik in idxs]
    cnt = hits[0].astype(F32)
    for hk in hits[1:]:
        cnt = cnt + hk.astype(F32)
    si = lax.broadcasted_iota(I32, (tm, tm), 0)
    ti = lax.broadcasted_iota(I32, (tm, tm), 1)
    before = (si < ti).astype(BF16)
    total = _dot(cnt.astype(BF16), before) + run_ref[...]
    ranks = [jnp.sum(jnp.where(hk, total, 0.0), axis=0, keepdims=True) for hk in hits]
    rank_ref[...] = jnp.concatenate(ranks, axis=0).astype(I32)
    run_ref[...] = run_ref[...] + jnp.sum(cnt, axis=1, keepdims=True)
    cnt_ref[...] = run_ref[...]


def _pack_bf16_pairs(v):
    bits = lax.bitcast_convert_type(v.astype(BF16).astype(F32), jnp.uint32)
    half = bits.shape[1] // 2
    return (bits[:, half:] & jnp.uint32(0xFFFF0000)) | (bits[:, :half] >> 16)


def _unpack_bf16_pairs(w):
    return (lax.bitcast_convert_type(w << 16, F32),
            lax.bitcast_convert_type(w & jnp.uint32(0xFFFF0000), F32))


def _expert_kernel(tab_ref, x_ref, wgu_hbm, bgu_ref, wd_hbm, bd_ref, o_ref,
                   wgu_in, wd_in, wgu_bf, wd_bf, act_ref, sem_gu, sem_d, *, layer):
    blk = pl.program_id(0)
    ch = wgu_bf.shape[2]
    f = act_ref.shape[1]
    nc = f // ch
    expert, live, nxt, wslot = (tab_ref[i, blk] for i in range(4))
    used = live > 0
    fresh = jnp.logical_or(blk == 0, expert != tab_ref[0, jnp.maximum(blk - 1, 0)])

    def fetch(e, slot):
        return (pltpu.make_async_copy(wgu_hbm.at[layer, e], wgu_in.at[slot], sem_gu.at[slot]),
                pltpu.make_async_copy(wd_hbm.at[layer, e], wd_in.at[slot], sem_d.at[slot]))

    @pl.when(jnp.logical_and(used, fresh))
    def _():
        slot = wslot
        mine = fetch(expert, slot)

        @pl.when(blk == 0)
        def _():
            for cp in mine:
                cp.start()

        @pl.when(nxt >= 0)
        def _():
            for cp in fetch(nxt, 1 - slot):
                cp.start()

        for cp in mine:
            cp.wait()
        for j in range(wgu_bf.shape[0]):
            wgu_bf[j] = wgu_in[slot, :, j * ch:(j + 1) * ch].astype(BF16)
        for j in range(wd_bf.shape[0]):
            wd_bf[j] = wd_in[slot, :, j * ch:(j + 1) * ch].astype(BF16)

    bm = x_ref.shape[0]

    def run(nr):
        w = x_ref[:nr, :]
        row = lax.broadcasted_iota(I32, w.shape, 0)
        w = jnp.where(row < live, w, jnp.zeros_like(w))
        lo, hi = _unpack_bf16_pairs(w)
        x = jnp.concatenate([lo.astype(BF16), hi.astype(BF16)], axis=1)
        for c in range(nc):
            c0 = slice(c * ch, (c + 1) * ch)
            c1 = slice(f + c * ch, f + (c + 1) * ch)
            glu = _dot(x, wgu_bf[c]) + bgu_ref[0, 0, :, c0]
            lin = _dot(x, wgu_bf[nc + c]) + bgu_ref[0, 0, :, c1]
            glu = jnp.minimum(glu, SWIGLU_LIMIT)
            lin = jnp.clip(lin, -SWIGLU_LIMIT, SWIGLU_LIMIT)
            act_ref[:nr, c0] = (glu * jax.nn.sigmoid(SWIGLU_ALPHA * glu) * (lin + 1.0)).astype(BF16)
        a = act_ref[:nr, :]
        y = jnp.concatenate([_dot(a, wd_bf[n]) for n in range(wd_bf.shape[0])], axis=1)
        o_ref[:nr, :] = _pack_bf16_pairs(y + bd_ref[0, 0])

    for nr in range(MOE_SUB, bm + 1, MOE_SUB):
        @pl.when(jnp.logical_and(live > nr - MOE_SUB, live <= nr))
        def _(nr=nr):
            run(nr)
            if nr < bm:
                o_ref[nr:, :] = jnp.zeros((bm - nr, o_ref.shape[1]), o_ref.dtype)

    @pl.when(jnp.logical_not(used))
    def _():
        o_ref[...] = jnp.zeros_like(o_ref)


def _expert_call(tab, buf, w_gu, b_gu, w_down, b_down, layer, bm):
    n_slots, dw = buf.shape
    _, ne, d, f2 = w_gu.shape
    f = f2 // 2
    n_blocks = n_slots // bm
    lw = lambda i, tb: (layer, tb[0, i], 0, 0)
    row = lambda i, tb: (i, 0)
    cw = 256
    return pl.pallas_call(
        functools.partial(_expert_kernel, layer=layer),
        grid_spec=pltpu.PrefetchScalarGridSpec(
            num_scalar_prefetch=1,
            grid=(n_blocks,),
            in_specs=[pl.BlockSpec((bm, dw), row),
                      pl.BlockSpec(memory_space=pl.ANY),
                      pl.BlockSpec((1, 1, 1, f2), lw),
                      pl.BlockSpec(memory_space=pl.ANY),
                      pl.BlockSpec((1, 1, 1, d), lw)],
            out_specs=pl.BlockSpec((bm, dw), row),
            scratch_shapes=[pltpu.VMEM((2, d, f2), F32), pltpu.VMEM((2, f, d), F32),
                            pltpu.VMEM((f2 // cw, d, cw), BF16), pltpu.VMEM((d // cw, f, cw), BF16),
                            pltpu.VMEM((bm, f), BF16),
                            pltpu.SemaphoreType.DMA((2,)), pltpu.SemaphoreType.DMA((2,))]),
        out_shape=jax.ShapeDtypeStruct((n_slots, dw), jnp.uint32),
        compiler_params=_cparams(("arbitrary",)),
        name="moe_experts",
    )(tab, buf, w_gu, b_gu.reshape(b_gu.shape[0], ne, 1, f2), w_down, b_down.reshape(b_down.shape[0], ne, 1, d))


def _expert_kernel_ring(start_ref, nblk_ref, cnt_ref, x_hbm, wgu_ref, bgu_ref, wd_ref, bd_ref, y_hbm,
                        xbuf, ybuf, wgu_bf, wd_bf, act_ref, sem_in, sem_out, *, n_blocks, fused=False):
    e = pl.program_id(0)
    bm = xbuf.shape[1]
    f = act_ref.shape[1]
    ch = 256
    nb = nblk_ref[e]
    start = start_ref[e]
    cnt = cnt_ref[e]

    def rows(j):
        return pl.ds(pl.multiple_of(start + j * bm, bm), bm)

    def fetch(j, slot):
        return pltpu.make_async_copy(x_hbm.at[rows(j)], xbuf.at[slot], sem_in.at[slot])

    def put(j, slot):
        return pltpu.make_async_copy(ybuf.at[slot], y_hbm.at[rows(j)], sem_out.at[slot])

    @pl.when(nb > 0)
    def _():
        fetch(0, 0).start()
        if wgu_bf.ndim == 3:
            for j in range(wgu_bf.shape[0]):
                wgu_bf[j] = wgu_ref[0, 0, :, j * ch:(j + 1) * ch].astype(BF16)
            for j in range(wd_bf.shape[0]):
                wd_bf[j] = wd_ref[0, 0, :, j * ch:(j + 1) * ch].astype(BF16)
        else:
            for j in range(wgu_bf.shape[1] // ch):
                cs = slice(j * ch, (j + 1) * ch)
                wgu_bf[:, cs] = wgu_ref[0, 0, :, cs].astype(BF16)
            for j in range(wd_bf.shape[1] // ch):
                cs = slice(j * ch, (j + 1) * ch)
                wd_bf[:, cs] = wd_ref[0, 0, :, cs].astype(BF16)

    def block(j, carry):
        slot = lax.rem(j, 2)

        @pl.when(j + 1 < nb)
        def _():
            fetch(j + 1, 1 - slot).start()

        fetch(j, slot).wait()

        @pl.when(j >= 2)
        def _():
            put(j - 2, slot).wait()

        w = xbuf[slot]
        row = lax.broadcasted_iota(I32, w.shape, 0)
        w = jnp.where(row < cnt - j * bm, w, jnp.zeros_like(w))
        lo, hi = _unpack_bf16_pairs(w)
        x = jnp.concatenate([lo.astype(BF16), hi.astype(BF16)], axis=1)
        if fused:
            z = _dot(x, wgu_bf[...]) + bgu_ref[0, 0]
            glu = jnp.minimum(z[:, :f], SWIGLU_LIMIT)
            lin = jnp.clip(z[:, f:], -SWIGLU_LIMIT, SWIGLU_LIMIT)
            act = (glu * jax.nn.sigmoid(SWIGLU_ALPHA * glu) * (lin + 1.0)).astype(BF16)
            ybuf[slot] = _pack_bf16_pairs(_dot(act, wd_bf[...]) + bd_ref[0, 0])
            put(j, slot).start()
            return carry
        if wgu_bf.ndim == 3:
            nc = f // ch
            for c in range(nc):
                c0 = slice(c * ch, (c + 1) * ch)
                c1 = slice(f + c * ch, f + (c + 1) * ch)
                glu = _dot(x, wgu_bf[c]) + bgu_ref[0, 0, :, c0]
                lin = _dot(x, wgu_bf[nc + c]) + bgu_ref[0, 0, :, c1]
                glu = jnp.minimum(glu, SWIGLU_LIMIT)
                lin = jnp.clip(lin, -SWIGLU_LIMIT, SWIGLU_LIMIT)
                act_ref[:, c0] = (glu * jax.nn.sigmoid(SWIGLU_ALPHA * glu) * (lin + 1.0)).astype(BF16)
            a = act_ref[...]
            y = jnp.concatenate([_dot(a, wd_bf[n]) for n in range(wd_bf.shape[0])], axis=1)
            ybuf[slot] = _pack_bf16_pairs(y + bd_ref[0, 0])
            put(j, slot).start()
            return carry
        for c in range(f // ch):
            c0 = slice(c * ch, (c + 1) * ch)
            c1 = slice(f + c * ch, f + (c + 1) * ch)
            glu = _dot(x, wgu_bf[:, c0]) + bgu_ref[0, 0, :, c0]
            lin = _dot(x, wgu_bf[:, c1]) + bgu_ref[0, 0, :, c1]
            glu = jnp.minimum(glu, SWIGLU_LIMIT)
            lin = jnp.clip(lin, -SWIGLU_LIMIT, SWIGLU_LIMIT)
            act_ref[:, c0] = (glu * jax.nn.sigmoid(SWIGLU_ALPHA * glu) * (lin + 1.0)).astype(BF16)
        ybuf[slot] = _pack_bf16_pairs(_dot(act_ref[...], wd_bf[...]) + bd_ref[0, 0])
        put(j, slot).start()
        return carry

    lax.fori_loop(0, nb, block, 0)

    @pl.when(nb >= 2)
    def _():
        put(nb - 2, lax.rem(nb, 2)).wait()

    @pl.when(nb >= 1)
    def _():
        put(nb - 1, lax.rem(nb + 1, 2)).wait()

    @pl.when(e == pl.num_programs(0) - 1)
    def _():
        ybuf[0] = jnp.zeros(ybuf.shape[1:], ybuf.dtype)

        def fill(j, carry):
            cp = pltpu.make_async_copy(ybuf.at[0], y_hbm.at[pl.ds(pl.multiple_of(j * bm, bm), bm)], sem_out.at[0])
            cp.start()
            cp.wait()
            return carry

        lax.fori_loop(start // bm + nb, n_blocks, fill, 0)


def _expert_call_ring(start, nblk, cnt, buf, w_gu, b_gu, w_down, b_down, layer, bm, fused=False, chunked=False):
    n_slots, dw = buf.shape
    _, ne, d, f2 = w_gu.shape
    f = f2 // 2
    lw = lambda e, *_: (layer, e, 0, 0)
    return pl.pallas_call(
        functools.partial(_expert_kernel_ring, n_blocks=n_slots // bm, fused=fused),
        grid_spec=pltpu.PrefetchScalarGridSpec(
            num_scalar_prefetch=3,
            grid=(ne,),
            in_specs=[pl.BlockSpec(memory_space=pl.ANY),
                      pl.BlockSpec((1, 1, d, f2), lw),
                      pl.BlockSpec((1, 1, 1, f2), lw),
                      pl.BlockSpec((1, 1, f, d), lw),
                      pl.BlockSpec((1, 1, 1, d), lw)],
            out_specs=pl.BlockSpec(memory_space=pl.ANY),
            scratch_shapes=[pltpu.VMEM((2, bm, dw), jnp.uint32), pltpu.VMEM((2, bm, dw), jnp.uint32),
                            pltpu.VMEM((f2 // 256, d, 256) if chunked else (d, f2), BF16),
                            pltpu.VMEM((d // 256, f, 256) if chunked else (f, d), BF16), pltpu.VMEM((bm, f), BF16),
                            pltpu.SemaphoreType.DMA((2,)), pltpu.SemaphoreType.DMA((2,))]),
        out_shape=jax.ShapeDtypeStruct((n_slots, dw), jnp.uint32),
        compiler_params=_cparams(("arbitrary",)),
        name="moe_experts",
    )(start, nblk, cnt, buf, w_gu, b_gu.reshape(b_gu.shape[0], ne, 1, f2),
      w_down, b_down.reshape(b_down.shape[0], ne, 1, d))


def _combine_kernel(x_ref, m_ref, y_ref, gate_ref, gout_ref, o_ref, *, final):
    m = m_ref[0]
    gate = gate_ref[...]
    half = y_ref.shape[2]
    f_lo, f_hi = None, None
    for k in range(TOP_K):
        lo, hi = _unpack_bf16_pairs(y_ref[k])
        gk = gate[:, k:k + 1]
        f_lo = gk * lo if f_lo is None else f_lo + gk * lo
        f_hi = gk * hi if f_hi is None else f_hi + gk * hi
    x_lo = x_ref[:, :half] + m[5:6, :half] * f_lo
    x_hi = x_ref[:, half:] + m[5:6, half:] * f_hi
    if final:
        ms = (jnp.sum(x_lo * x_lo, axis=-1, keepdims=True) + jnp.sum(x_hi * x_hi, axis=-1, keepdims=True))
        r = lax.rsqrt(ms * (0.5 / half) + NORM_EPS)
        x_lo = x_lo * r * gout_ref[:, :half]
        x_hi = x_hi * r * gout_ref[:, half:]
    o_ref[:, :half] = x_lo
    o_ref[:, half:] = x_hi


def _moe_layer(xs, mods, g, layer, w_router, b_router, w_gu, b_gu, w_down, b_down, geo, g_out, final):
    n, d = xs.shape
    ne = w_router.shape[1]
    f = w_down.shape[2]
    n_tiles = n // TM
    grp = functools.partial(_tile_group, geo=geo)
    const2 = lambda t: (0, 0)
    h, idx, gate, rank, cnt = pl.pallas_call(
        _route_kernel,
        grid=(n_tiles,),
        in_specs=[pl.BlockSpec((TM, d), lambda t: (t, 0)),
                  pl.BlockSpec((1, 6, d), lambda t: (grp(t), 0, 0)),
                  pl.BlockSpec((1, d), const2),
                  pl.BlockSpec((ne, d), const2),
                  pl.BlockSpec((ne, 1), const2)],
        out_specs=[pl.BlockSpec((TM, d // 2), lambda t: (t, 0)),
                   pl.BlockSpec((TOP_K, TM), lambda t: (0, t)),
                   pl.BlockSpec((TOP_K, TM), lambda t: (0, t)),
                   pl.BlockSpec((TOP_K, TM), lambda t: (0, t)),
                   pl.BlockSpec((ne, 1), const2)],
        out_shape=[jax.ShapeDtypeStruct((n, d // 2), jnp.uint32),
                   jax.ShapeDtypeStruct((TOP_K, n), I32),
                   jax.ShapeDtypeStruct((TOP_K, n), F32),
                   jax.ShapeDtypeStruct((TOP_K, n), I32),
                   jax.ShapeDtypeStruct((ne, 1), F32)],
        scratch_shapes=[pltpu.VMEM((ne, 1), F32)],
        compiler_params=_cparams(("arbitrary",)),
        name="moe_route",
    )(xs, mods, g.reshape(1, d), w_router.T.astype(BF16), b_router.reshape(ne, 1))

    bm = MOE_BM
    n_rows = n * TOP_K
    n_blocks = -(-n_rows // bm) + ne
    n_slots = n_blocks * bm
    nbp = -(-n_blocks // 128) * 128
    dest, btab = pl.pallas_call(
        functools.partial(_slot_kernel, bm=bm),
        grid=(n // SLOT_T,),
        in_specs=[pl.BlockSpec((TOP_K, SLOT_T), lambda t: (0, t)),
                  pl.BlockSpec((TOP_K, SLOT_T), lambda t: (0, t)),
                  pl.BlockSpec((ne, 1), const2)],
        out_specs=[pl.BlockSpec((TOP_K, SLOT_T), lambda t: (0, t)),
                   pl.BlockSpec((8, nbp), const2)],
        out_shape=[jax.ShapeDtypeStruct((TOP_K, n), I32),
                   jax.ShapeDtypeStruct((8, nbp), I32)],
        compiler_params=_cparams(("arbitrary",)),
        name="moe_slots",
    )(idx, rank, cnt)

    buf = _sc_scatter_rows(h, dest, n_slots)
    y = _expert_call(btab, buf, w_gu, b_gu, w_down, b_down, layer, bm)
    dy = d // 2
    yk = _sc_gather_rows(y, dest.reshape(-1)).reshape(TOP_K, n, dy)
    return pl.pallas_call(
        functools.partial(_combine_kernel, final=final),
        grid=(n_tiles,),
        in_specs=[pl.BlockSpec((TM, d), lambda t: (t, 0)),
                  pl.BlockSpec((1, 6, d), lambda t: (grp(t), 0, 0)),
                  pl.BlockSpec((TOP_K, TM, dy), lambda t: (0, t, 0)),
                  pl.BlockSpec((TM, TOP_K), lambda t: (t, 0)),
                  pl.BlockSpec((1, d), const2)],
        out_specs=pl.BlockSpec((TM, d), lambda t: (t, 0)),
        out_shape=jax.ShapeDtypeStruct((n, d), F32),
        compiler_params=_cparams(("parallel",)),
        name="moe_combine",
    )(xs, mods, yk, gate.T, g_out.reshape(1, d))


def kernel(x, c, ctx, c_ctx, ada_w, ada_b, norm_mix, norm_ffn, norm_out, a_w_in, a_g_v, a_w_s, a_b_s, a_w_out, b_w_in, b_w_a2, b_b_a, b_g_o, b_w_out, c_w_qkv, c_rpb, c_w_out, moe_w_router, moe_b_router, moe_w_gu, moe_b_gu, moe_w_down, moe_b_down):
    b, seq, d = x.shape
    lctx = ctx.shape[1]
    depth = ada_w.shape[0]
    assert (b * lctx) % TM == 0 and seq % TM == 0 and lctx % GLA_TB == 0 and seq % GLA_TB == 0
    assert seq % (NA_QROWS * GRID_W) == 0 and (b * lctx) % (NA_QROWS * GRID_W) == 0 and lctx % 256 == 0
    assert seq // GRID_W >= NA_WROWS + NA_QROWS and b + 1 <= 8
    geo = (b * lctx // TM, seq // TM, b)
    dims = (b, seq, lctx)

    cond = jnp.zeros((8, d), F32).at[:b].set(c).at[b].set(c_ctx)
    mods_all = _adaln(cond, ada_w, ada_b)[:, :b + 1].reshape(depth, b + 1, 6, d)

    xs = jnp.concatenate([ctx.reshape(b * lctx, d), x.reshape(b * seq, d)], axis=0)
    ctx_tiles = geo[0]
    has_ctx = True
    for i in range(depth):
        kind, j = i % N_MIXERS, i // N_MIXERS
        ctx_later = any(kk % N_MIXERS != 0 for kk in range(i + 1, depth))
        mods = mods_all[i]
        if kind == 0:
            keep_ctx = has_ctx and ctx_later
            skip = 0 if keep_ctx or not has_ctx else ctx_tiles
            geo_i = geo if has_ctx else (0, geo[1], b)
            xs = _gmlp_mixer(xs, mods, norm_mix[i], a_w_in[j], a_g_v[j], a_w_s[j], a_b_s[j], a_w_out[j],
                             skip, xs.shape[0] // TM - skip, geo_i)
            has_ctx = keep_ctx
        elif kind == 1:
            assert has_ctx
            xs = _gla_mixer(xs, mods, norm_mix[i], b_w_in[j], b_w_a2[j], b_b_a[j], b_g_o[j], b_w_out[j], geo, dims)
            if not ctx_later:
                xs = xs[b * lctx:]
                has_ctx = False
        else:
            assert has_ctx
            if ctx_later:
                raise NotImplementedError("context output of the neighbourhood mixer")
            xs = _na_mixer(xs, mods, norm_mix[i], c_w_qkv[j], c_rpb[j], c_w_out[j], geo, dims)
            has_ctx = False
        geo_i = geo if has_ctx else (0, geo[1], b)
        xs = _moe_layer(xs, mods, norm_ffn[i], i, moe_w_router[i], moe_b_router[i], moe_w_gu, moe_b_gu,
                        moe_w_down, moe_b_down, geo_i, norm_out, i == depth - 1)
    if has_ctx:
        xs = xs[b * lctx:]
    return xs.reshape(b, seq, d)
```

```python
import functools

import numpy as np
import jax
import jax.numpy as jnp
from jax import lax
from jax.experimental import pallas as pl
from jax.experimental.pallas import tpu as pltpu
from jax.experimental.pallas import tpu_sc as plsc

F32 = jnp.float32
BF16 = jnp.bfloat16
I32 = jnp.int32

NORM_EPS = 1e-6
GRID_W = 64
N_MIXERS = 3

CHUNK_A = 128
A_GROUPS = 8
GMLP_PART = 256
GLA_HEADS = 4
GLA_RANK = 16
GLA_TAU = 16.0
GLA_CHUNK = 128
ROPE_BASE = 10000.0
NA_HEADS = 16
NA_KH = 8
NA_KW = 16
NEG_INF = -1e30
NA_QROWS = 4
NA_WROWS = 12
NA_PAIRS_PER_STEP = 4
TOP_K = 4
SWIGLU_LIMIT = 7.0
SWIGLU_ALPHA = 1.702
MOE_BM = 1024
MOE_SUB = 256

TM = 512
GLA_TB = 256
VMEM_LIMIT = 56 * 1024 * 1024


def _cparams(sem):
    return pltpu.CompilerParams(dimension_semantics=sem, vmem_limit_bytes=VMEM_LIMIT)


def _dot(a, b):
    return jnp.dot(a, b, preferred_element_type=F32)


def _dot_nt(a, b):
    return lax.dot_general(a, b, (((1,), (1,)), ((), ())), preferred_element_type=F32)


def _dot_tn(a, b):
    return lax.dot_general(a, b, (((0,), (0,)), ((), ())), preferred_element_type=F32)


def _rms(x, g):
    return x * lax.rsqrt(jnp.mean(x * x, axis=-1, keepdims=True) + NORM_EPS) * g


def _modulate(x, g, shift, scale):
    return _rms(x, g) * (1.0 + scale) + shift


def _ada_kernel(s_ref, w_ref, b_ref, o_ref):
    s = s_ref[...]
    s = s * jax.nn.sigmoid(s)
    o_ref[0] = jnp.dot(s, w_ref[0], preferred_element_type=F32,
                       precision=lax.Precision.HIGHEST) + b_ref[0]


def _adaln(cond, ada_w, ada_b):
    depth, d, n6 = ada_w.shape
    bn = n6 // 4
    return pl.pallas_call(
        _ada_kernel,
        grid=(depth, n6 // bn),
        in_specs=[pl.BlockSpec((8, d), lambda i, j: (0, 0)),
                  pl.BlockSpec((1, d, bn), lambda i, j: (i, 0, j)),
                  pl.BlockSpec((1, 1, bn), lambda i, j: (i, 0, j))],
        out_specs=pl.BlockSpec((1, 8, bn), lambda i, j: (i, 0, j)),
        out_shape=jax.ShapeDtypeStruct((depth, 8, n6), F32),
        compiler_params=_cparams(("parallel", "parallel")),
        name="adaln",
    )(cond, ada_w, ada_b.reshape(depth, 1, n6))


def _gelu(z):
    return 0.5 * z * (1.0 + lax.erf(z * np.float32(np.sqrt(0.5))))


def _gmlp_kernel(x_ref, m_ref, g_ref, win_ref, gv_ref, ws_ref, bs_ref, wout_ref, *route_refs):
    route_in, o_ref, route_out = route_refs[:3], route_refs[3], route_refs[4:]
    m = m_ref[0]
    a = gv_ref.shape[1]
    gw = a // A_GROUPS
    npart = x_ref.shape[0] // GMLP_PART
    parts = [slice(p * GMLP_PART, (p + 1) * GMLP_PART) for p in range(npart)]
    z = [_dot(_modulate(x_ref[rs, :], g_ref[...], m[0:1], m[1:2]).astype(BF16), win_ref[...]) for rs in parts]
    z = [_gelu(zp) for zp in z]
    for rs, zp in zip(parts, z):
        u = zp[:, :a]
        v = _rms(zp[:, a:], gv_ref[...]).astype(BF16)
        rows = []
        for c in range(GMLP_PART // CHUNK_A):
            cols = [_dot(ws_ref[g], v[c * CHUNK_A:(c + 1) * CHUNK_A, g * gw:(g + 1) * gw])
                    for g in range(A_GROUPS)]
            rows.append(jnp.concatenate(cols, axis=1) + bs_ref[...])
        s = jnp.concatenate(rows, axis=0)
        y = _dot((u * s).astype(BF16), wout_ref[...])
        o_ref[rs, :] = x_ref[rs, :] + m[2:3] * y
    _route_tail(o_ref[...], m, *route_in, *route_out)


def _gmlp_mixer(xs, mods, g, w_in, g_v, w_s, b_s, w_out, tile0, n_tiles, geo, route_w):
    d = xs.shape[1]
    a = g_v.shape[0]
    gw = a // A_GROUPS
    bias = jnp.repeat(b_s.T, gw, axis=1)
    grp = functools.partial(_tile_group, geo=geo, tile0=tile0)
    const2 = lambda t: (0, 0)
    r_in, r_out, r_shape, r_scratch = _route_specs(n_tiles * TM, d, route_w[1].shape[1])
    out = pl.pallas_call(
        _gmlp_kernel,
        grid=(n_tiles,),
        in_specs=[pl.BlockSpec((TM, d), lambda t: (t + tile0, 0)),
                  pl.BlockSpec((1, 6, d), lambda t: (grp(t), 0, 0)),
                  pl.BlockSpec((1, d), const2),
                  pl.BlockSpec((d, 2 * a), const2),
                  pl.BlockSpec((1, a), const2),
                  pl.BlockSpec((A_GROUPS, CHUNK_A, CHUNK_A), lambda t: (0, 0, 0)),
                  pl.BlockSpec((CHUNK_A, a), const2),
                  pl.BlockSpec((a, d), const2)] + r_in,
        out_specs=[pl.BlockSpec((TM, d), lambda t: (t, 0))] + r_out,
        out_shape=[jax.ShapeDtypeStruct((n_tiles * TM, d), F32)] + r_shape,
        scratch_shapes=r_scratch,
        compiler_params=_cparams(("arbitrary",)),
        name="gmlp_mixer",
    )(xs, mods, g.reshape(1, d), w_in.astype(BF16), g_v.reshape(1, a), w_s.astype(BF16),
      bias, w_out.astype(BF16), *_route_args(route_w, d))
    return out[0], out[1:]


def _tile_group(t, geo, tile0=0):
    n_ctx_tiles, tiles_per_batch, nb = geo
    tt = t + tile0
    return jnp.where(tt < n_ctx_tiles, nb, (tt - n_ctx_tiles) // tiles_per_batch)


def _dot_f32(tri_bf16, x):
    hi = x.astype(BF16)
    r1 = x - hi.astype(F32)
    mid = r1.astype(BF16)
    lo = (r1 - mid.astype(F32)).astype(BF16)
    return _dot(tri_bf16, hi) + _dot(tri_bf16, mid) + _dot(tri_bf16, lo)


def _gla_proj_kernel(x_ref, m_ref, g_ref, wq_ref, wk_ref, wv_ref, wg_ref, wr_ref, wa_ref, ba_ref,
                     cos_ref, sin_ref, q_ref, k_ref, v_ref, go_ref, la_ref):
    x = x_ref[...]
    m = m_ref[0]
    h = _modulate(x, g_ref[...], m[0:1], m[1:2]).astype(BF16)
    kdim = wq_ref.shape[1]
    hk = kdim // GLA_HEADS
    r = _dot(h, wr_ref[...])
    z = _dot(r.astype(BF16), wa_ref[...]) + ba_ref[...]
    q = _dot(h, wq_ref[...]) * np.float32(hk ** -0.5)
    k = _dot(h, wk_ref[...])
    la_ref[...] = jax.nn.log_sigmoid(z) * np.float32(1.0 / GLA_TAU)
    v_ref[...] = _dot(h, wv_ref[...]).astype(BF16)
    go_ref[...] = _dot(h, wg_ref[...])

    cos = jnp.concatenate([cos_ref[...]] * GLA_HEADS, axis=1)
    sin = jnp.concatenate([sin_ref[...]] * GLA_HEADS, axis=1)
    nf = hk // 4
    lane = lax.broadcasted_iota(I32, q.shape, 1)
    first = (lane % (2 * nf)) < nf

    def rope(t):
        up = pltpu.roll(t, kdim - nf, 1)
        dn = pltpu.roll(t, nf, 1)
        return t * cos + jnp.where(first, up, dn) * sin

    q_ref[...] = rope(q)
    k_ref[...] = rope(k)


def _gla_decays(q_ref, k_ref, la_ref, n, rev):
    c = GLA_CHUNK
    ri = lax.broadcasted_iota(I32, (c, c), 0)
    ci = lax.broadcasted_iota(I32, (c, c), 1)
    keep = (ci >= ri) if rev else (ci <= ri)
    rows = slice(n * c, (n + 1) * c)
    cum = _dot_f32(keep.astype(BF16), la_ref[rows, :])
    last = cum[0:1] if rev else cum[c - 1:c]
    q = q_ref[rows, :]
    k = k_ref[rows, :]
    mid = cum[c // 2:c // 2 + 1]
    return dict(rows=rows, keep=keep, dec=jnp.exp(last),
                q_dec=(q * jnp.exp(cum)).astype(BF16),
                q_att=(q * jnp.exp(cum - mid)).astype(BF16),
                k_att=(k * jnp.exp(mid - cum)).astype(BF16),
                k_end=(k * jnp.exp(last - cum)).astype(BF16))


def _gla_scan_kernel(qf_ref, kf_ref, vf_ref, laf_ref, qb_ref, kb_ref, vb_ref, lab_ref,
                     of_ref, ob_ref, stf_ref, stb_ref):
    s = pl.program_id(1)

    @pl.when(s == 0)
    def _():
        stf_ref[...] = jnp.zeros_like(stf_ref)
        stb_ref[...] = jnp.zeros_like(stb_ref)

    nch = qf_ref.shape[0] // GLA_CHUNK
    hk = qf_ref.shape[1] // GLA_HEADS
    hv = vf_ref.shape[1] // GLA_HEADS
    dirs = ((qf_ref, kf_ref, vf_ref, laf_ref, of_ref, stf_ref, False),
            (qb_ref, kb_ref, vb_ref, lab_ref, ob_ref, stb_ref, True))
    heads = [(slice(h * hk, (h + 1) * hk), slice(h * hv, (h + 1) * hv)) for h in range(GLA_HEADS)]
    for n in range(nch):
        w = [_gla_decays(q_ref, k_ref, la_ref, nch - 1 - n if rev else n, rev)
             for q_ref, k_ref, _, la_ref, _, _, rev in dirs]
        att = [[jnp.where(w[d]["keep"], _dot_nt(w[d]["q_att"][:, ks], w[d]["k_att"][:, ks]), 0.0).astype(BF16)
                for ks, _ in heads] for d in range(2)]
        for d, (_, _, v_ref, _, o_ref, st_ref, _) in enumerate(dirs):
            for h, (ks, vs) in enumerate(heads):
                o_ref[w[d]["rows"], vs] = (_dot(att[d][h], v_ref[w[d]["rows"], vs])
                                           + _dot_nt(w[d]["q_dec"][:, ks], st_ref[h].astype(BF16)))
        for d, (_, _, v_ref, _, _, st_ref, _) in enumerate(dirs):
            for h, (ks, vs) in enumerate(heads):
                st_ref[h] = (st_ref[h] * w[d]["dec"][:, ks]
                             + _dot_tn(v_ref[w[d]["rows"], vs], w[d]["k_end"][:, ks]))


def _gla_out_kernel(x_ref, m_ref, of_ref, ob_ref, go_ref, gn_ref, wout_ref, *route_refs):
    route_in, o_ref, route_out = route_refs[:3], route_refs[3], route_refs[4:]
    x = x_ref[...]
    m = m_ref[0]
    o = of_ref[...] + ob_ref[...]
    hv = gn_ref.shape[1]
    parts = [_rms(o[:, h * hv:(h + 1) * hv], gn_ref[...]) for h in range(GLA_HEADS)]
    o = jnp.concatenate(parts, axis=1)
    gate = go_ref[...]
    y = _dot((o * (gate * jax.nn.sigmoid(gate))).astype(BF16), wout_ref[...])
    x_new = x + m[2:3] * y
    o_ref[...] = x_new
    _route_tail(x_new, m, *route_in, *route_out)


def _gla_mixer(xs, mods, g, w_in, w_a2, b_a, g_o, w_out, geo, dims, route_w):
    b, seq, lctx = dims
    nt, d = xs.shape
    n_tiles = nt // TM
    kdim = w_a2.shape[2]
    vdim = g_o.shape[0] * GLA_HEADS
    hk = kdim // GLA_HEADS
    nf = hk // 4
    wq = w_in[:, :kdim].astype(BF16)
    wk = w_in[:, kdim:2 * kdim].astype(BF16)
    wv = w_in[:, 2 * kdim:2 * kdim + vdim].astype(BF16)
    wg = w_in[:, 2 * kdim + vdim:2 * kdim + 2 * vdim].astype(BF16)
    wr = jnp.pad(w_in[:, 2 * kdim + 2 * vdim:], ((0, 0), (0, 128 - 2 * GLA_RANK))).astype(BF16)
    wa = jnp.zeros((128, 2 * kdim), F32)
    wa = wa.at[:GLA_RANK, :kdim].set(w_a2[0]).at[GLA_RANK:2 * GLA_RANK, kdim:].set(w_a2[1]).astype(BF16)
    ba = b_a.reshape(1, 2 * kdim)

    tpos = jnp.arange(seq)
    freqs = jnp.power(ROPE_BASE, -jnp.arange(nf, dtype=F32) / nf)
    ar = (tpos // GRID_W).astype(F32)[:, None] * freqs
    ac = (tpos % GRID_W).astype(F32)[:, None] * freqs
    cos = jnp.concatenate([jnp.cos(ar), jnp.cos(ar), jnp.cos(ac), jnp.cos(ac)], axis=1)
    sin = jnp.concatenate([-jnp.sin(ar), jnp.sin(ar), -jnp.sin(ac), jnp.sin(ac)], axis=1)
    cos = jnp.concatenate([jnp.ones((TM, hk), F32), cos], axis=0)
    sin = jnp.concatenate([jnp.zeros((TM, hk), F32), sin], axis=0)

    n_ctx_tiles, tiles_per_batch, _ = geo
    grp = functools.partial(_tile_group, geo=geo)

    def rope_blk(t):
        return (jnp.where(t < n_ctx_tiles, 0, 1 + (t - n_ctx_tiles) % tiles_per_batch), 0)

    const2 = lambda t: (0, 0)
    row = lambda t: (t, 0)
    q, k, v, go, la = pl.pallas_call(
        _gla_proj_kernel,
        grid=(n_tiles,),
        in_specs=[pl.BlockSpec((TM, d), row),
                  pl.BlockSpec((1, 6, d), lambda t: (grp(t), 0, 0)),
                  pl.BlockSpec((1, d), const2),
                  pl.BlockSpec((d, kdim), const2), pl.BlockSpec((d, kdim), const2),
                  pl.BlockSpec((d, vdim), const2), pl.BlockSpec((d, vdim), const2),
                  pl.BlockSpec((d, 128), const2), pl.BlockSpec((128, 2 * kdim), const2),
                  pl.BlockSpec((1, 2 * kdim), const2),
                  pl.BlockSpec((TM, hk), rope_blk), pl.BlockSpec((TM, hk), rope_blk)],
        out_specs=[pl.BlockSpec((TM, kdim), row), pl.BlockSpec((TM, kdim), row),
                   pl.BlockSpec((TM, vdim), row), pl.BlockSpec((TM, vdim), row),
                   pl.BlockSpec((TM, 2 * kdim), row)],
        out_shape=[jax.ShapeDtypeStruct((nt, kdim), F32), jax.ShapeDtypeStruct((nt, kdim), F32),
                   jax.ShapeDtypeStruct((nt, vdim), BF16), jax.ShapeDtypeStruct((nt, vdim), F32),
                   jax.ShapeDtypeStruct((nt, 2 * kdim), F32)],
        compiler_params=_cparams(("parallel",)),
        name="gla_proj",
    )(xs, mods, g.reshape(1, d), wq, wk, wv, wg, wr, wa, ba, cos, sin)

    tb = GLA_TB
    ctx_steps = lctx // tb
    lat_steps = seq // tb
    steps = ctx_steps + lat_steps

    def blk(bi, s, rev):
        if rev:
            cs, ls = ctx_steps - 1 - s, lat_steps - 1 - (s - ctx_steps)
        else:
            cs, ls = s, s - ctx_steps
        return jnp.where(s < ctx_steps, bi * ctx_steps + cs, b * ctx_steps + bi * lat_steps + ls)

    def dir_specs(rev):
        row_blk = lambda bi, s: (blk(bi, s, rev), 0)
        return [pl.BlockSpec((tb, kdim), row_blk), pl.BlockSpec((tb, kdim), row_blk),
                pl.BlockSpec((tb, vdim), row_blk),
                pl.BlockSpec((tb, kdim), lambda bi, s: (blk(bi, s, rev), 1 if rev else 0))]

    state = pltpu.VMEM((GLA_HEADS, vdim // GLA_HEADS, hk), F32)
    o_f, o_b = pl.pallas_call(
        _gla_scan_kernel,
        grid=(b, steps),
        in_specs=dir_specs(False) + dir_specs(True),
        out_specs=[pl.BlockSpec((tb, vdim), lambda bi, s: (blk(bi, s, False), 0)),
                   pl.BlockSpec((tb, vdim), lambda bi, s: (blk(bi, s, True), 0))],
        out_shape=[jax.ShapeDtypeStruct((nt, vdim), F32)] * 2,
        scratch_shapes=[state, state],
        compiler_params=_cparams(("parallel", "arbitrary")),
        name="gla_scan",
    )(q, k, v, la, q, k, v, la)

    r_in, r_out, r_shape, r_scratch = _route_specs(nt, d, route_w[1].shape[1])
    out = pl.pallas_call(
        _gla_out_kernel,
        grid=(n_tiles,),
        in_specs=[pl.BlockSpec((TM, d), row),
                  pl.BlockSpec((1, 6, d), lambda t: (grp(t), 0, 0)),
                  pl.BlockSpec((TM, vdim), row), pl.BlockSpec((TM, vdim), row),
                  pl.BlockSpec((TM, vdim), row),
                  pl.BlockSpec((1, vdim // GLA_HEADS), const2),
                  pl.BlockSpec((vdim, d), const2)] + r_in,
        out_specs=[pl.BlockSpec((TM, d), row)] + r_out,
        out_shape=[jax.ShapeDtypeStruct((nt, d), F32)] + r_shape,
        scratch_shapes=r_scratch,
        compiler_params=_cparams(("arbitrary",)),
        name="gla_out",
    )(xs, mods, o_f, o_b, go, g_o.reshape(1, -1), w_out.astype(BF16), *_route_args(route_w, d))
    return out[0], out[1:]


def _na_proj_kernel(x_ref, m_ref, g_ref, w_ref, q_ref, k_ref, v_ref):
    x = x_ref[...]
    m = m_ref[0]
    d = x.shape[1]
    h = _modulate(x, g_ref[...], m[0:1], m[1:2]).astype(BF16)
    qkv = _dot(h, w_ref[...])
    hd = d // NA_HEADS
    for p in range(q_ref.shape[0]):
        cs = slice(p * 128, (p + 1) * 128)
        q_ref[p] = (qkv[:, cs] * np.float32(hd ** -0.5)).astype(BF16)
        k_ref[p] = qkv[:, d + p * 128:d + (p + 1) * 128].astype(BF16)
        v_ref[p] = qkv[:, 2 * d + p * 128:2 * d + (p + 1) * 128].astype(BF16)


def _na_attn_kernel(tbl_ref, q_ref, *refs, n_tiles, nwin):
    k_refs, v_refs = refs[:nwin], refs[nwin:2 * nwin]
    kc_ref, vc_ref, bias_ref, o_ref, s_ref, p_ref = refs[2 * nwin:]
    t = pl.program_id(2)
    typ = jnp.where(t == 0, 0, jnp.where(t == n_tiles - 1, 2, 1))
    npair = NA_WROWS // 2
    nlat = NA_WROWS * GRID_W
    lane = lax.broadcasted_iota(I32, q_ref.shape[1:], 1)
    npp, tq = q_ref.shape[0], q_ref.shape[1]
    for pp in range(npp):
        q = q_ref[pp]
        k_all = jnp.concatenate([r[pp] for r in k_refs] + [kc_ref[pp]], axis=0)
        zero = jnp.zeros_like(q)
        q2 = jnp.concatenate([jnp.where(lane < 64, q, zero), jnp.where(lane >= 64, q, zero)], axis=0)
        s_ref[pp] = _dot_nt(q2, k_all)
    for pp in range(npp):
        for hh in range(2):
            for dr in range(NA_QROWS):
                rs = slice(hh * tq + dr * GRID_W, hh * tq + (dr + 1) * GRID_W)
                lat = [s_ref[pp, rs, j * 128:(j + 1) * 128]
                       + bias_ref[2 * pp + hh, tbl_ref[typ * (NA_QROWS * npair) + dr * npair + j]]
                       for j in range(npair)]
                sb = jnp.concatenate(lat + [s_ref[pp, rs, nlat:]], axis=1)
                e = jnp.exp(sb - jnp.max(sb, axis=1, keepdims=True))
                p_ref[pp, rs, :] = (e * (1.0 / jnp.sum(e, axis=1, keepdims=True))).astype(BF16)
    outs = []
    for pp in range(npp):
        v_all = jnp.concatenate([r[pp] for r in v_refs] + [vc_ref[pp]], axis=0)
        o2 = _dot(p_ref[pp], v_all)
        outs.append(jnp.where(lane < 64, o2[:tq], o2[tq:]))
    o_ref[...] = jnp.concatenate(outs, axis=1).astype(BF16)


def _na_out_kernel(x_ref, m_ref, a_ref, w_ref, *route_refs):
    route_in, o_ref, route_out = route_refs[:3], route_refs[3], route_refs[4:]
    m = m_ref[0]
    x_new = x_ref[...] + m[2:3] * _dot(a_ref[...], w_ref[...])
    o_ref[...] = x_new
    _route_tail(x_new, m, *route_in, *route_out)


def _na_tables(rows):
    npair = NA_WROWS // 2
    tbl = np.zeros((3, NA_QROWS, npair), np.int32)
    for typ, r_base in enumerate((0, NA_QROWS, rows - NA_QROWS)):
        w0 = int(np.clip(r_base - NA_KH // 2, 0, rows - NA_WROWS))
        for dr in range(NA_QROWS):
            r = r_base + dr
            r0 = int(np.clip(r - NA_KH // 2, 0, rows - NA_KH))
            for j in range(npair):
                kr = (w0 + 2 * j, w0 + 2 * j + 1)
                ok = [r0 <= x < r0 + NA_KH for x in kr]
                ri = [x - r + NA_KH - 1 for x in kr]
                if ok[0] and ok[1]:
                    e = ri[0]
                elif ok[0]:
                    e = 16 + ri[0]
                elif ok[1]:
                    e = 32 + ri[1]
                else:
                    e = 63
                tbl[typ, dr, j] = e
    return tbl.reshape(-1)


def _na_bias_kernel(rpb_ref, o_ref, *, nr, nw):
    h = pl.program_id(0)
    c = lax.broadcasted_iota(I32, (GRID_W, GRID_W), 0)
    k = lax.broadcasted_iota(I32, (GRID_W, GRID_W), 1)
    cstart = jnp.clip(c - NA_KW // 2, 0, GRID_W - NA_KW)
    ok = jnp.logical_and(k >= cstart, k < cstart + NA_KW)
    rel = k - c + NA_KW - 1
    neg = jnp.full((GRID_W, GRID_W), NEG_INF, F32)
    cbs = []
    for r in range(nr):
        acc = neg
        for w in range(nw):
            acc = jnp.where(rel == w, rpb_ref[(h * nr + r) * nw + w], acc)
        cbs.append(jnp.where(ok, acc, neg))
    cbs += [neg] * (17 - nr)
    for r in range(16):
        o_ref[0, r] = jnp.concatenate([cbs[r], cbs[r + 1]], axis=1)
        o_ref[0, 16 + r] = jnp.concatenate([cbs[r], neg], axis=1)
        o_ref[0, 32 + r] = jnp.concatenate([neg, cbs[r]], axis=1)
        o_ref[0, 48 + r] = jnp.concatenate([neg, neg], axis=1)


def _na_bias_table(rpb):
    nh, nr, nw = rpb.shape
    return pl.pallas_call(
        functools.partial(_na_bias_kernel, nr=nr, nw=nw),
        grid=(nh,),
        in_specs=[pl.BlockSpec(memory_space=pltpu.SMEM)],
        out_specs=pl.BlockSpec((1, 64, GRID_W, 2 * GRID_W), lambda h: (h, 0, 0, 0)),
        out_shape=jax.ShapeDtypeStruct((nh, 64, GRID_W, 2 * GRID_W), F32),
        compiler_params=_cparams(("parallel",)),
        name="na_bias_table",
    )(rpb.reshape(-1).astype(F32))


def _na_mixer(xs, mods, g, w_qkv, rpb, w_out, geo, dims, route_w):
    b, seq, lctx = dims
    nt, d = xs.shape
    n_tiles_tok = nt // TM
    grp = functools.partial(_tile_group, geo=geo)
    npairs = d // 128
    const2 = lambda t: (0, 0)
    q, k, v = pl.pallas_call(
        _na_proj_kernel,
        grid=(n_tiles_tok,),
        in_specs=[pl.BlockSpec((TM, d), lambda t: (t, 0)),
                  pl.BlockSpec((1, 6, d), lambda t: (grp(t), 0, 0)),
                  pl.BlockSpec((1, d), const2),
                  pl.BlockSpec((d, 3 * d), const2)],
        out_specs=[pl.BlockSpec((npairs, TM, 128), lambda t: (0, t, 0))] * 3,
        out_shape=[jax.ShapeDtypeStruct((npairs, nt, 128), BF16)] * 3,
        compiler_params=_cparams(("parallel",)),
        name="na_proj",
    )(xs, mods, g.reshape(1, d), w_qkv.astype(BF16))

    rows = seq // GRID_W
    tq = NA_QROWS * GRID_W
    n_tiles = rows // NA_QROWS
    wb = 256
    nwin = NA_WROWS * GRID_W // wb
    lat0 = b * lctx
    tbl = jnp.asarray(_na_tables(rows))
    bias = _na_bias_table(rpb)

    pp = NA_PAIRS_PER_STEP
    assert npairs % pp == 0 and (NA_QROWS * GRID_W) % wb == 0 and (NA_KH // 2 * GRID_W) % wb == 0

    def win(i):
        def f(p, bi, t, tbl_ref):
            w = jnp.clip(t * (NA_QROWS * GRID_W // wb) - NA_KH // 2 * GRID_W // wb, 0, seq // wb - nwin)
            return (p, (lat0 + bi * seq) // wb + w + i, 0)
        return f

    kv_specs = [pl.BlockSpec((pp, wb, 128), win(i)) for i in range(nwin)]
    attn = pl.pallas_call(
        functools.partial(_na_attn_kernel, n_tiles=n_tiles, nwin=nwin),
        grid_spec=pltpu.PrefetchScalarGridSpec(
            num_scalar_prefetch=1,
            grid=(npairs // pp, b, n_tiles),
            in_specs=[pl.BlockSpec((pp, tq, 128), lambda p, bi, t, tr: (p, (lat0 + bi * seq) // tq + t, 0))]
                     + kv_specs + kv_specs
                     + [pl.BlockSpec((pp, lctx, 128), lambda p, bi, t, tr: (p, bi, 0)),
                        pl.BlockSpec((pp, lctx, 128), lambda p, bi, t, tr: (p, bi, 0)),
                        pl.BlockSpec((2 * pp, 64, GRID_W, 2 * GRID_W), lambda p, bi, t, tr: (p, 0, 0, 0),
                                     pipeline_mode=pl.Buffered(1))],
            out_specs=pl.BlockSpec((tq, 128 * pp), lambda p, bi, t, tr: (bi * n_tiles + t, p)),
            scratch_shapes=[pltpu.VMEM((pp, 2 * tq, NA_WROWS * GRID_W + lctx), F32),
                            pltpu.VMEM((pp, 2 * tq, NA_WROWS * GRID_W + lctx), BF16)]),
        out_shape=jax.ShapeDtypeStruct((b * seq, d), BF16),
        compiler_params=_cparams(("parallel", "parallel", "arbitrary")),
        name="na_attn",
    )(tbl, q, *([k] * nwin), *([v] * nwin), k, v, bias)

    n_lat_tiles = b * seq // TM
    tile0 = lat0 // TM
    grp_l = functools.partial(_tile_group, geo=geo, tile0=tile0)
    r_in, r_out, r_shape, r_scratch = _route_specs(b * seq, d, route_w[1].shape[1])
    out = pl.pallas_call(
        _na_out_kernel,
        grid=(n_lat_tiles,),
        in_specs=[pl.BlockSpec((TM, d), lambda t: (t + tile0, 0)),
                  pl.BlockSpec((1, 6, d), lambda t: (grp_l(t), 0, 0)),
                  pl.BlockSpec((TM, d), lambda t: (t, 0)),
                  pl.BlockSpec((d, d), const2)] + r_in,
        out_specs=[pl.BlockSpec((TM, d), lambda t: (t, 0))] + r_out,
        out_shape=[jax.ShapeDtypeStruct((b * seq, d), F32)] + r_shape,
        scratch_shapes=r_scratch,
        compiler_params=_cparams(("arbitrary",)),
        name="na_out",
    )(xs, mods, attn, w_out.astype(BF16), *_route_args(route_w, d))
    return out[0], out[1:]


SC_CORES = 2
SC_SUBCORES = 16
SC_WORKERS = SC_CORES * SC_SUBCORES
SLOT_T = 512


def _sc_mesh():
    return plsc.VectorSubcoreMesh(core_axis_name="c", subcore_axis_name="s")


def _sc_chunk(per_worker, max_chunk):
    return max(c for c in range(8, max_chunk + 1, 8) if per_worker % (2 * c) == 0)


def _sc_gather_rows(table, idx):
    dd = table.shape[1]
    bsz = idx.shape[0]
    per_w = bsz // SC_WORKERS
    assert per_w * SC_WORKERS == bsz
    chunk = _sc_chunk(per_w, 64)
    n_chunks = per_w // chunk

    def body(table_hbm, idx_hbm, out_hbm, idx_v, rows0, rows1, g0, g1, w0, w1):
        wid = lax.axis_index("s") * SC_CORES + lax.axis_index("c")
        pltpu.sync_copy(idx_hbm.at[wid], idx_v)
        base = wid * per_w

        def out_rows(j):
            return out_hbm.at[pl.ds(pl.multiple_of(base + j * chunk, 8), chunk)]

        def step(i, carry):
            j0, j1 = 2 * i, 2 * i + 1
            ga = pltpu.async_copy(table_hbm.at[idx_v.at[j0]], rows0, g0)
            gb = pltpu.async_copy(table_hbm.at[idx_v.at[j1]], rows1, g1)
            ga.wait()
            wa = pltpu.async_copy(rows0, out_rows(j0), w0)
            gb.wait()
            wb = pltpu.async_copy(rows1, out_rows(j1), w1)
            wa.wait()
            wb.wait()
            return carry

        lax.fori_loop(0, n_chunks // 2, step, 0)

    return pl.kernel(
        body, out_type=jax.ShapeDtypeStruct((bsz, dd), table.dtype), mesh=_sc_mesh(),
        scratch_types=[pltpu.VMEM((n_chunks, chunk), I32),
                       pltpu.VMEM((chunk, dd), table.dtype), pltpu.VMEM((chunk, dd), table.dtype),
                       pltpu.SemaphoreType.DMA, pltpu.SemaphoreType.DMA,
                       pltpu.SemaphoreType.DMA, pltpu.SemaphoreType.DMA],
        name="sc_gather_rows",
    )(table, idx.reshape(SC_WORKERS, n_chunks, chunk))


def _sc_scatter_rows(rows, dest, n_out):
    n, dd = rows.shape
    kk = dest.shape[0]
    per_w = n // SC_WORKERS
    assert per_w * SC_WORKERS == n
    chunk = _sc_chunk(per_w, 64)
    n_chunks = per_w // chunk
    dest_w = dest.reshape(kk, SC_WORKERS, n_chunks, chunk).transpose(1, 2, 0, 3)
    dest_w = dest_w.reshape(SC_WORKERS, n_chunks * kk, chunk)

    def body(rows_hbm, dest_hbm, out_hbm, idx_v, rows0, rows1, r0, r1, s0, s1):
        wid = lax.axis_index("s") * SC_CORES + lax.axis_index("c")
        pltpu.sync_copy(dest_hbm.at[wid], idx_v)
        base = wid * per_w

        def in_rows(j):
            return rows_hbm.at[pl.ds(pl.multiple_of(base + j * chunk, 8), chunk)]

        def step(i, carry):
            j0, j1 = 2 * i, 2 * i + 1
            ra = pltpu.async_copy(in_rows(j0), rows0, r0)
            rb = pltpu.async_copy(in_rows(j1), rows1, r1)
            ra.wait()
            sa = [pltpu.async_copy(rows0, out_hbm.at[idx_v.at[j0 * kk + k]], s0) for k in range(kk)]
            rb.wait()
            sb = [pltpu.async_copy(rows1, out_hbm.at[idx_v.at[j1 * kk + k]], s1) for k in range(kk)]
            for cp in sa + sb:
                cp.wait()
            return carry

        lax.fori_loop(0, n_chunks // 2, step, 0)

    return pl.kernel(
        body, out_type=jax.ShapeDtypeStruct((n_out, dd), rows.dtype), mesh=_sc_mesh(),
        scratch_types=[pltpu.VMEM((n_chunks * kk, chunk), I32),
                       pltpu.VMEM((chunk, dd), rows.dtype), pltpu.VMEM((chunk, dd), rows.dtype),
                       pltpu.SemaphoreType.DMA, pltpu.SemaphoreType.DMA,
                       pltpu.SemaphoreType.DMA, pltpu.SemaphoreType.DMA],
        name="sc_scatter_rows",
    )(rows, dest_w)


def _prefix_sum_rows(col):
    nr = col.shape[0]
    acc = jnp.broadcast_to(col, (nr, 128))
    row = lax.broadcasted_iota(I32, (nr, 128), 0)
    s = 1
    while s < nr:
        acc = acc + jnp.where(row >= s, pltpu.roll(acc, s, 0), 0.0)
        s *= 2
    return acc[:, 0:1]


def _slot_kernel(idx_ref, rank_ref, cnt_ref, dest_ref, btab_ref, *, bm):
    cnt = cnt_ref[...]
    ne = cnt.shape[0]
    padded = jnp.floor((cnt + (bm - 1.0)) * (1.0 / bm)) * bm
    pad_end = _prefix_sum_rows(padded)
    pad_start = pad_end - padded
    idx = idx_ref[...]
    e_iota = lax.broadcasted_iota(I32, (ne, idx.shape[1]), 0)
    starts = [jnp.sum(jnp.where(e_iota == idx[k:k + 1], pad_start, 0.0), axis=0, keepdims=True)
              for k in range(idx.shape[0])]
    dest_ref[...] = jnp.concatenate(starts, axis=0).astype(I32) + rank_ref[...]

    nbp = btab_ref.shape[1]
    bstart = lax.broadcasted_iota(I32, (ne, nbp), 1).astype(F32) * bm
    be = jnp.minimum(jnp.sum((pad_end <= bstart).astype(F32), axis=0, keepdims=True), ne - 1.0)
    mine = lax.broadcasted_iota(I32, (ne, nbp), 0).astype(F32) == be
    pick = lambda col: jnp.sum(jnp.where(mine, col, 0.0), axis=0, keepdims=True)
    live = jnp.clip(pick(pad_start + cnt) - bstart[0:1], 0.0, bm)
    r = lax.broadcasted_iota(I32, (ne, ne), 0)
    c = lax.broadcasted_iota(I32, (ne, ne), 1)
    ends_on_lanes = _dot_f32(jnp.ones((ne, ne), BF16), jnp.where(r == c, pad_end, 0.0))
    nxt_e = jnp.sum((ends_on_lanes <= pad_end).astype(F32), axis=1, keepdims=True)
    nxt_e = jnp.where(nxt_e < ne, nxt_e, -1.0)
    ordinal = _prefix_sum_rows((cnt > 0).astype(F32)) - 1.0
    slot_e = ordinal - 2.0 * jnp.floor(ordinal * 0.5)
    rowi = lax.broadcasted_iota(I32, btab_ref.shape, 0)
    tab = jnp.where(rowi == 0, be, jnp.where(rowi == 1, live, jnp.where(rowi == 2, pick(nxt_e), pick(slot_e))))
    btab_ref[...] = tab.astype(I32)


def _route_specs(n, d, ne):
    const2 = lambda t: (0, 0)
    in_specs = [pl.BlockSpec((1, d), const2), pl.BlockSpec((ne, d), const2), pl.BlockSpec((ne, 1), const2)]
    out_specs = [pl.BlockSpec((TM, d // 2), lambda t: (t, 0)),
                 pl.BlockSpec((TOP_K, TM), lambda t: (0, t)),
                 pl.BlockSpec((TOP_K, TM), lambda t: (0, t)),
                 pl.BlockSpec((TOP_K, TM), lambda t: (0, t)),
                 pl.BlockSpec((ne, 1), const2)]
    out_shape = [jax.ShapeDtypeStruct((n, d // 2), jnp.uint32),
                 jax.ShapeDtypeStruct((TOP_K, n), I32),
                 jax.ShapeDtypeStruct((TOP_K, n), F32),
                 jax.ShapeDtypeStruct((TOP_K, n), I32),
                 jax.ShapeDtypeStruct((ne, 1), F32)]
    return in_specs, out_specs, out_shape, [pltpu.VMEM((ne, 1), F32)]


def _route_args(route_w, d):
    g_ffn, w_router, b_router = route_w
    ne = w_router.shape[1]
    return g_ffn.reshape(1, d), w_router.T.astype(BF16), b_router.reshape(ne, 1)


def _route_tail(x, m, g_ref, wr_ref, br_ref, h_ref, idx_ref, gate_ref, rank_ref, cnt_ref, run_ref):
    t = pl.program_id(0)

    @pl.when(t == 0)
    def _():
        run_ref[...] = jnp.zeros_like(run_ref)

    h = _modulate(x, g_ref[...], m[3:4], m[4:5]).astype(BF16)
    h_ref[...] = _pack_bf16_pairs(h)
    logits = _dot_nt(wr_ref[...], h) + br_ref[...]
    ne, tm = logits.shape
    e_iota = lax.broadcasted_iota(I32, (ne, tm), 0)
    vals, idxs = [], []
    l = logits
    for _ in range(TOP_K):
        mk = jnp.max(l, axis=0, keepdims=True)
        ik = jnp.min(jnp.where(l == mk, e_iota, ne), axis=0, keepdims=True)
        vals.append(mk)
        idxs.append(ik)
        l = jnp.where(e_iota == ik, -jnp.inf, l)
    top_val = jnp.concatenate(vals, axis=0)
    ex = jnp.exp(top_val - vals[0])
    gate_ref[...] = ex / jnp.sum(ex, axis=0, keepdims=True)
    idx_ref[...] = jnp.concatenate(idxs, axis=0)

    hits = [e_iota == ik for ik in idxs]
    cnt = hits[0].astype(F32)
    for hk in hits[1:]:
        cnt = cnt + hk.astype(F32)
    si = lax.broadcasted_iota(I32, (tm, tm), 0)
    ti = lax.broadcasted_iota(I32, (tm, tm), 1)
    before = (si < ti).astype(BF16)
    total = _dot(cnt.astype(BF16), before) + run_ref[...]
    ranks = [jnp.sum(jnp.where(hk, total, 0.0), axis=0, keepdims=True) for hk in hits]
    rank_ref[...] = jnp.concatenate(ranks, axis=0).astype(I32)
    run_ref[...] = run_ref[...] + jnp.sum(cnt, axis=1, keepdims=True)
    cnt_ref[...] = run_ref[...]


def _pack_bf16_pairs(v):
    bits = lax.bitcast_convert_type(v.astype(BF16).astype(F32), jnp.uint32)
    half = bits.shape[1] // 2
    return (bits[:, half:] & jnp.uint32(0xFFFF0000)) | (bits[:, :half] >> 16)


def _unpack_bf16_pairs(w):
    return (lax.bitcast_convert_type(w << 16, F32),
            lax.bitcast_convert_type(w & jnp.uint32(0xFFFF0000), F32))


def _expert_kernel(tab_ref, x_ref, wgu_hbm, bgu_ref, wd_hbm, bd_ref, o_ref,
                   wgu_in, wd_in, wgu_bf, wd_bf, act_ref, sem_gu, sem_d, *, layer):
    blk = pl.program_id(0)
    ch = wgu_bf.shape[2]
    f = act_ref.shape[1]
    nc = f // ch
    expert, live, nxt, wslot = (tab_ref[i, blk] for i in range(4))
    used = live > 0
    fresh = jnp.logical_or(blk == 0, expert != tab_ref[0, jnp.maximum(blk - 1, 0)])

    def fetch(e, slot):
        return (pltpu.make_async_copy(wgu_hbm.at[layer, e], wgu_in.at[slot], sem_gu.at[slot]),
                pltpu.make_async_copy(wd_hbm.at[layer, e], wd_in.at[slot], sem_d.at[slot]))

    @pl.when(jnp.logical_and(used, fresh))
    def _():
        slot = wslot
        mine = fetch(expert, slot)

        @pl.when(blk == 0)
        def _():
            for cp in mine:
                cp.start()

        @pl.when(nxt >= 0)
        def _():
            for cp in fetch(nxt, 1 - slot):
                cp.start()

        for cp in mine:
            cp.wait()
        for j in range(wgu_bf.shape[0]):
            wgu_bf[j] = wgu_in[slot, :, j * ch:(j + 1) * ch].astype(BF16)
        for j in range(wd_bf.shape[0]):
            wd_bf[j] = wd_in[slot, :, j * ch:(j + 1) * ch].astype(BF16)

    bm = x_ref.shape[0]

    def run(nr):
        w = x_ref[:nr, :]
        row = lax.broadcasted_iota(I32, w.shape, 0)
        w = jnp.where(row < live, w, jnp.zeros_like(w))
        lo, hi = _unpack_bf16_pairs(w)
        x = jnp.concatenate([lo.astype(BF16), hi.astype(BF16)], axis=1)
        for c in range(nc):
            c0 = slice(c * ch, (c + 1) * ch)
            c1 = slice(f + c * ch, f + (c + 1) * ch)
            glu = _dot(x, wgu_bf[c]) + bgu_ref[0, 0, :, c0]
            lin = _dot(x, wgu_bf[nc + c]) + bgu_ref[0, 0, :, c1]
            glu = jnp.minimum(glu, SWIGLU_LIMIT)
            lin = jnp.clip(lin, -SWIGLU_LIMIT, SWIGLU_LIMIT)
            act_ref[:nr, c0] = (glu * jax.nn.sigmoid(SWIGLU_ALPHA * glu) * (lin + 1.0)).astype(BF16)
        a = act_ref[:nr, :]
        y = jnp.concatenate([_dot(a, wd_bf[n]) for n in range(wd_bf.shape[0])], axis=1)
        o_ref[:nr, :] = _pack_bf16_pairs(y + bd_ref[0, 0])

    for nr in range(MOE_SUB, bm + 1, MOE_SUB):
        @pl.when(jnp.logical_and(live > nr - MOE_SUB, live <= nr))
        def _(nr=nr):
            run(nr)
            if nr < bm:
                o_ref[nr:, :] = jnp.zeros((bm - nr, o_ref.shape[1]), o_ref.dtype)

    @pl.when(jnp.logical_not(used))
    def _():
        o_ref[...] = jnp.zeros_like(o_ref)


def _expert_call(tab, buf, w_gu, b_gu, w_down, b_down, layer, bm):
    n_slots, dw = buf.shape
    _, ne, d, f2 = w_gu.shape
    f = f2 // 2
    n_blocks = n_slots // bm
    lw = lambda i, tb: (layer, tb[0, i], 0, 0)
    row = lambda i, tb: (i, 0)
    cw = 256
    return pl.pallas_call(
        functools.partial(_expert_kernel, layer=layer),
        grid_spec=pltpu.PrefetchScalarGridSpec(
            num_scalar_prefetch=1,
            grid=(n_blocks,),
            in_specs=[pl.BlockSpec((bm, dw), row),
                      pl.BlockSpec(memory_space=pl.ANY),
                      pl.BlockSpec((1, 1, 1, f2), lw),
                      pl.BlockSpec(memory_space=pl.ANY),
                      pl.BlockSpec((1, 1, 1, d), lw)],
            out_specs=pl.BlockSpec((bm, dw), row),
            scratch_shapes=[pltpu.VMEM((2, d, f2), F32), pltpu.VMEM((2, f, d), F32),
                            pltpu.VMEM((f2 // cw, d, cw), BF16), pltpu.VMEM((d // cw, f, cw), BF16),
                            pltpu.VMEM((bm, f), BF16),
                            pltpu.SemaphoreType.DMA((2,)), pltpu.SemaphoreType.DMA((2,))]),
        out_shape=jax.ShapeDtypeStruct((n_slots, dw), jnp.uint32),
        compiler_params=_cparams(("arbitrary",)),
        name="moe_experts",
    )(tab, buf, w_gu, b_gu.reshape(b_gu.shape[0], ne, 1, f2), w_down, b_down.reshape(b_down.shape[0], ne, 1, d))


def _combine_kernel(x_ref, m_ref, y_ref, gate_ref, gout_ref, o_ref, *, final):
    m = m_ref[0]
    gate = gate_ref[...]
    half = y_ref.shape[2]
    f_lo, f_hi = None, None
    for k in range(TOP_K):
        lo, hi = _unpack_bf16_pairs(y_ref[k])
        gk = gate[:, k:k + 1]
        f_lo = gk * lo if f_lo is None else f_lo + gk * lo
        f_hi = gk * hi if f_hi is None else f_hi + gk * hi
    x_lo = x_ref[:, :half] + m[5:6, :half] * f_lo
    x_hi = x_ref[:, half:] + m[5:6, half:] * f_hi
    if final:
        ms = (jnp.sum(x_lo * x_lo, axis=-1, keepdims=True) + jnp.sum(x_hi * x_hi, axis=-1, keepdims=True))
        r = lax.rsqrt(ms * (0.5 / half) + NORM_EPS)
        x_lo = x_lo * r * gout_ref[:, :half]
        x_hi = x_hi * r * gout_ref[:, half:]
    o_ref[:, :half] = x_lo
    o_ref[:, half:] = x_hi


def _moe_layer(xs, routed, mods, layer, w_gu, b_gu, w_down, b_down, geo, g_out, final):
    n, d = xs.shape
    h, idx, gate, rank, cnt = routed
    assert h.shape[0] == n
    ne = cnt.shape[0]
    n_tiles = n // TM
    grp = functools.partial(_tile_group, geo=geo)
    const2 = lambda t: (0, 0)

    bm = MOE_BM
    n_rows = n * TOP_K
    n_blocks = -(-n_rows // bm) + ne
    n_slots = n_blocks * bm
    nbp = -(-n_blocks // 128) * 128
    dest, btab = pl.pallas_call(
        functools.partial(_slot_kernel, bm=bm),
        grid=(n // SLOT_T,),
        in_specs=[pl.BlockSpec((TOP_K, SLOT_T), lambda t: (0, t)),
                  pl.BlockSpec((TOP_K, SLOT_T), lambda t: (0, t)),
                  pl.BlockSpec((ne, 1), const2)],
        out_specs=[pl.BlockSpec((TOP_K, SLOT_T), lambda t: (0, t)),
                   pl.BlockSpec((8, nbp), const2)],
        out_shape=[jax.ShapeDtypeStruct((TOP_K, n), I32),
                   jax.ShapeDtypeStruct((8, nbp), I32)],
        compiler_params=_cparams(("arbitrary",)),
        name="moe_slots",
    )(idx, rank, cnt)

    buf = _sc_scatter_rows(h, dest, n_slots)
    y = _expert_call(btab, buf, w_gu, b_gu, w_down, b_down, layer, bm)
    dy = d // 2
    yk = _sc_gather_rows(y, dest.reshape(-1)).reshape(TOP_K, n, dy)
    return pl.pallas_call(
        functools.partial(_combine_kernel, final=final),
        grid=(n_tiles,),
        in_specs=[pl.BlockSpec((TM, d), lambda t: (t, 0)),
                  pl.BlockSpec((1, 6, d), lambda t: (grp(t), 0, 0)),
                  pl.BlockSpec((TOP_K, TM, dy), lambda t: (0, t, 0)),
                  pl.BlockSpec((TM, TOP_K), lambda t: (t, 0)),
                  pl.BlockSpec((1, d), const2)],
        out_specs=pl.BlockSpec((TM, d), lambda t: (t, 0)),
        out_shape=jax.ShapeDtypeStruct((n, d), F32),
        compiler_params=_cparams(("parallel",)),
        name="moe_combine",
    )(xs, mods, yk, gate.T, g_out.reshape(1, d))


def kernel(x, c, ctx, c_ctx, ada_w, ada_b, norm_mix, norm_ffn, norm_out, a_w_in, a_g_v, a_w_s, a_b_s, a_w_out, b_w_in, b_w_a2, b_b_a, b_g_o, b_w_out, c_w_qkv, c_rpb, c_w_out, moe_w_router, moe_b_router, moe_w_gu, moe_b_gu, moe_w_down, moe_b_down):
    b, seq, d = x.shape
    lctx = ctx.shape[1]
    depth = ada_w.shape[0]
    assert (b * lctx) % TM == 0 and seq % TM == 0 and lctx % GLA_TB == 0 and seq % GLA_TB == 0
    assert seq % (NA_QROWS * GRID_W) == 0 and (b * lctx) % (NA_QROWS * GRID_W) == 0 and lctx % 256 == 0
    assert seq // GRID_W >= NA_WROWS + NA_QROWS and b + 1 <= 8
    geo = (b * lctx // TM, seq // TM, b)
    dims = (b, seq, lctx)

    cond = jnp.zeros((8, d), F32).at[:b].set(c).at[b].set(c_ctx)
    mods_all = _adaln(cond, ada_w, ada_b)[:, :b + 1].reshape(depth, b + 1, 6, d)

    xs = jnp.concatenate([ctx.reshape(b * lctx, d), x.reshape(b * seq, d)], axis=0)
    ctx_tiles = geo[0]
    has_ctx = True
    for i in range(depth):
        kind, j = i % N_MIXERS, i // N_MIXERS
        ctx_later = any(kk % N_MIXERS != 0 for kk in range(i + 1, depth))
        mods = mods_all[i]
        route_w = (norm_ffn[i], moe_w_router[i], moe_b_router[i])
        if kind == 0:
            keep_ctx = has_ctx and ctx_later
            skip = 0 if keep_ctx or not has_ctx else ctx_tiles
            geo_i = geo if has_ctx else (0, geo[1], b)
            xs, routed = _gmlp_mixer(xs, mods, norm_mix[i], a_w_in[j], a_g_v[j], a_w_s[j], a_b_s[j], a_w_out[j],
                                     skip, xs.shape[0] // TM - skip, geo_i, route_w)
            has_ctx = keep_ctx
        elif kind == 1:
            assert has_ctx
            if not ctx_later:
                raise NotImplementedError("GLA mixer whose context rows are dropped before the FFN")
            xs, routed = _gla_mixer(xs, mods, norm_mix[i], b_w_in[j], b_w_a2[j], b_b_a[j], b_g_o[j], b_w_out[j],
                                    geo, dims, route_w)
        else:
            assert has_ctx
            if ctx_later:
                raise NotImplementedError("context output of the neighbourhood mixer")
            xs, routed = _na_mixer(xs, mods, norm_mix[i], c_w_qkv[j], c_rpb[j], c_w_out[j], geo, dims, route_w)
            has_ctx = False
        geo_i = geo if has_ctx else (0, geo[1], b)
        xs = _moe_layer(xs, routed, mods, i, moe_w_gu, moe_b_gu, moe_w_down, moe_b_down, geo_i, norm_out,
                        i == depth - 1)
    if has_ctx:
        xs = xs[b * lctx:]
    return xs.reshape(b, seq, d)
```

```python
import functools

import numpy as np
import jax
import jax.numpy as jnp
from jax import lax
from jax.experimental import pallas as pl
from jax.experimental.pallas import tpu as pltpu
from jax.experimental.pallas import tpu_sc as plsc

F32 = jnp.float32
BF16 = jnp.bfloat16
I32 = jnp.int32

NORM_EPS = 1e-6
GRID_W = 64
N_MIXERS = 3

CHUNK_A = 128
A_GROUPS = 8
GMLP_PART = 256
GLA_HEADS = 4
GLA_RANK = 16
GLA_TAU = 16.0
GLA_CHUNK = 128
ROPE_BASE = 10000.0
NA_HEADS = 16
NA_KH = 8
NA_KW = 16
NEG_INF = -1e30
NA_QROWS = 4
NA_WROWS = 12
NA_PAIRS_PER_STEP = 4
TOP_K = 4
SWIGLU_LIMIT = 7.0
SWIGLU_ALPHA = 1.702
MOE_BM = 1024
MOE_SUB = 256

TM = 512
GLA_TB = 256
VMEM_LIMIT = 56 * 1024 * 1024


def _cparams(sem):
    return pltpu.CompilerParams(dimension_semantics=sem, vmem_limit_bytes=VMEM_LIMIT)


def _dot(a, b):
    return jnp.dot(a, b, preferred_element_type=F32)


def _dot_nt(a, b):
    return lax.dot_general(a, b, (((1,), (1,)), ((), ())), preferred_element_type=F32)


def _dot_tn(a, b):
    return lax.dot_general(a, b, (((0,), (0,)), ((), ())), preferred_element_type=F32)


def _rms(x, g):
    return x * lax.rsqrt(jnp.mean(x * x, axis=-1, keepdims=True) + NORM_EPS) * g


def _modulate(x, g, shift, scale):
    return _rms(x, g) * (1.0 + scale) + shift


def _ada_kernel(s_ref, w_ref, b_ref, o_ref):
    s = s_ref[...]
    s = s * jax.nn.sigmoid(s)
    o_ref[0] = jnp.dot(s, w_ref[0], preferred_element_type=F32,
                       precision=lax.Precision.HIGHEST) + b_ref[0]


def _adaln(cond, ada_w, ada_b):
    depth, d, n6 = ada_w.shape
    bn = n6 // 4
    return pl.pallas_call(
        _ada_kernel,
        grid=(depth, n6 // bn),
        in_specs=[pl.BlockSpec((8, d), lambda i, j: (0, 0)),
                  pl.BlockSpec((1, d, bn), lambda i, j: (i, 0, j)),
                  pl.BlockSpec((1, 1, bn), lambda i, j: (i, 0, j))],
        out_specs=pl.BlockSpec((1, 8, bn), lambda i, j: (i, 0, j)),
        out_shape=jax.ShapeDtypeStruct((depth, 8, n6), F32),
        compiler_params=_cparams(("parallel", "parallel")),
        name="adaln",
    )(cond, ada_w, ada_b.reshape(depth, 1, n6))


def _gelu(z):
    return 0.5 * z * (1.0 + lax.erf(z * np.float32(np.sqrt(0.5))))


def _gmlp_kernel(x_ref, m_ref, g_ref, win_ref, gv_ref, ws_ref, bs_ref, wout_ref, *route_refs):
    route_in, o_ref, route_out = route_refs[:3], route_refs[3], route_refs[4:]
    m = m_ref[0]
    a = gv_ref.shape[1]
    gw = a // A_GROUPS
    npart = x_ref.shape[0] // GMLP_PART
    parts = [slice(p * GMLP_PART, (p + 1) * GMLP_PART) for p in range(npart)]
    z = [_dot(_modulate(x_ref[rs, :], g_ref[...], m[0:1], m[1:2]).astype(BF16), win_ref[...]) for rs in parts]
    z = [_gelu(zp) for zp in z]
    for rs, zp in zip(parts, z):
        u = zp[:, :a]
        v = _rms(zp[:, a:], gv_ref[...]).astype(BF16)
        rows = []
        for c in range(GMLP_PART // CHUNK_A):
            cols = [_dot(ws_ref[g], v[c * CHUNK_A:(c + 1) * CHUNK_A, g * gw:(g + 1) * gw])
                    for g in range(A_GROUPS)]
            rows.append(jnp.concatenate(cols, axis=1) + bs_ref[...])
        s = jnp.concatenate(rows, axis=0)
        y = _dot((u * s).astype(BF16), wout_ref[...])
        o_ref[rs, :] = x_ref[rs, :] + m[2:3] * y
    _route_tail(o_ref[...], m, *route_in, *route_out)


def _gmlp_mixer(xs, mods, g, w_in, g_v, w_s, b_s, w_out, tile0, n_tiles, geo, route_w):
    d = xs.shape[1]
    a = g_v.shape[0]
    gw = a // A_GROUPS
    bias = jnp.repeat(b_s.T, gw, axis=1)
    grp = functools.partial(_tile_group, geo=geo, tile0=tile0)
    const2 = lambda t: (0, 0)
    r_in, r_out, r_shape, r_scratch = _route_specs(n_tiles * TM, d, route_w[1].shape[1])
    out = pl.pallas_call(
        _gmlp_kernel,
        grid=(n_tiles,),
        in_specs=[pl.BlockSpec((TM, d), lambda t: (t + tile0, 0)),
                  pl.BlockSpec((1, 6, d), lambda t: (grp(t), 0, 0)),
                  pl.BlockSpec((1, d), const2),
                  pl.BlockSpec((d, 2 * a), const2),
                  pl.BlockSpec((1, a), const2),
                  pl.BlockSpec((A_GROUPS, CHUNK_A, CHUNK_A), lambda t: (0, 0, 0)),
                  pl.BlockSpec((CHUNK_A, a), const2),
                  pl.BlockSpec((a, d), const2)] + r_in,
        out_specs=[pl.BlockSpec((TM, d), lambda t: (t, 0))] + r_out,
        out_shape=[jax.ShapeDtypeStruct((n_tiles * TM, d), F32)] + r_shape,
        scratch_shapes=r_scratch,
        compiler_params=_cparams(("arbitrary",)),
        name="gmlp_mixer",
    )(xs, mods, g.reshape(1, d), w_in.astype(BF16), g_v.reshape(1, a), w_s.astype(BF16),
      bias, w_out.astype(BF16), *_route_args(route_w, d))
    return out[0], out[1:]


def _tile_group(t, geo, tile0=0):
    n_ctx_tiles, tiles_per_batch, nb = geo
    tt = t + tile0
    return jnp.where(tt < n_ctx_tiles, nb, (tt - n_ctx_tiles) // tiles_per_batch)


def _dot_f32(tri_bf16, x):
    hi = x.astype(BF16)
    r1 = x - hi.astype(F32)
    mid = r1.astype(BF16)
    lo = (r1 - mid.astype(F32)).astype(BF16)
    return _dot(tri_bf16, hi) + _dot(tri_bf16, mid) + _dot(tri_bf16, lo)


def _gla_proj_kernel(x_ref, m_ref, g_ref, wq_ref, wk_ref, wv_ref, wg_ref, wr_ref, wa_ref, ba_ref,
                     cos_ref, sin_ref, q_ref, k_ref, v_ref, go_ref, la_ref):
    x = x_ref[...]
    m = m_ref[0]
    h = _modulate(x, g_ref[...], m[0:1], m[1:2]).astype(BF16)
    kdim = wq_ref.shape[1]
    hk = kdim // GLA_HEADS
    r = _dot(h, wr_ref[...])
    z = _dot(r.astype(BF16), wa_ref[...]) + ba_ref[...]
    q = _dot(h, wq_ref[...]) * np.float32(hk ** -0.5)
    k = _dot(h, wk_ref[...])
    la_ref[...] = jax.nn.log_sigmoid(z) * np.float32(1.0 / GLA_TAU)
    v_ref[...] = _dot(h, wv_ref[...]).astype(BF16)
    go_ref[...] = _dot(h, wg_ref[...])

    cos = jnp.concatenate([cos_ref[...]] * GLA_HEADS, axis=1)
    sin = jnp.concatenate([sin_ref[...]] * GLA_HEADS, axis=1)
    nf = hk // 4
    lane = lax.broadcasted_iota(I32, q.shape, 1)
    first = (lane % (2 * nf)) < nf

    def rope(t):
        up = pltpu.roll(t, kdim - nf, 1)
        dn = pltpu.roll(t, nf, 1)
        return t * cos + jnp.where(first, up, dn) * sin

    q_ref[...] = rope(q)
    k_ref[...] = rope(k)


def _gla_decays(q_ref, k_ref, la_ref, n, rev):
    c = GLA_CHUNK
    ri = lax.broadcasted_iota(I32, (c, c), 0)
    ci = lax.broadcasted_iota(I32, (c, c), 1)
    keep = (ci >= ri) if rev else (ci <= ri)
    rows = slice(n * c, (n + 1) * c)
    cum = _dot_f32(keep.astype(BF16), la_ref[rows, :])
    last = cum[0:1] if rev else cum[c - 1:c]
    q = q_ref[rows, :]
    k = k_ref[rows, :]
    mid = cum[c // 2:c // 2 + 1]
    return dict(rows=rows, keep=keep, dec=jnp.exp(last),
                q_dec=(q * jnp.exp(cum)).astype(BF16),
                q_att=(q * jnp.exp(cum - mid)).astype(BF16),
                k_att=(k * jnp.exp(mid - cum)).astype(BF16),
                k_end=(k * jnp.exp(last - cum)).astype(BF16))


def _gla_scan_kernel(qf_ref, kf_ref, vf_ref, laf_ref, qb_ref, kb_ref, vb_ref, lab_ref,
                     of_ref, ob_ref, stf_ref, stb_ref):
    s = pl.program_id(1)

    @pl.when(s == 0)
    def _():
        stf_ref[...] = jnp.zeros_like(stf_ref)
        stb_ref[...] = jnp.zeros_like(stb_ref)

    nch = qf_ref.shape[0] // GLA_CHUNK
    hk = qf_ref.shape[1] // GLA_HEADS
    hv = vf_ref.shape[1] // GLA_HEADS
    dirs = ((qf_ref, kf_ref, vf_ref, laf_ref, of_ref, stf_ref, False),
            (qb_ref, kb_ref, vb_ref, lab_ref, ob_ref, stb_ref, True))
    heads = [(slice(h * hk, (h + 1) * hk), slice(h * hv, (h + 1) * hv)) for h in range(GLA_HEADS)]
    for n in range(nch):
        w = [_gla_decays(q_ref, k_ref, la_ref, nch - 1 - n if rev else n, rev)
             for q_ref, k_ref, _, la_ref, _, _, rev in dirs]
        att = [[jnp.where(w[d]["keep"], _dot_nt(w[d]["q_att"][:, ks], w[d]["k_att"][:, ks]), 0.0).astype(BF16)
                for ks, _ in heads] for d in range(2)]
        for d, (_, _, v_ref, _, o_ref, st_ref, _) in enumerate(dirs):
            for h, (ks, vs) in enumerate(heads):
                o_ref[w[d]["rows"], vs] = (_dot(att[d][h], v_ref[w[d]["rows"], vs])
                                           + _dot_nt(w[d]["q_dec"][:, ks], st_ref[h].astype(BF16)))
        for d, (_, _, v_ref, _, _, st_ref, _) in enumerate(dirs):
            for h, (ks, vs) in enumerate(heads):
                st_ref[h] = (st_ref[h] * w[d]["dec"][:, ks]
                             + _dot_tn(v_ref[w[d]["rows"], vs], w[d]["k_end"][:, ks]))


def _gla_out_kernel(x_ref, m_ref, of_ref, ob_ref, go_ref, gn_ref, wout_ref, *route_refs):
    route_in, o_ref, route_out = route_refs[:3], route_refs[3], route_refs[4:]
    x = x_ref[...]
    m = m_ref[0]
    o = of_ref[...] + ob_ref[...]
    hv = gn_ref.shape[1]
    parts = [_rms(o[:, h * hv:(h + 1) * hv], gn_ref[...]) for h in range(GLA_HEADS)]
    o = jnp.concatenate(parts, axis=1)
    gate = go_ref[...]
    y = _dot((o * (gate * jax.nn.sigmoid(gate))).astype(BF16), wout_ref[...])
    x_new = x + m[2:3] * y
    o_ref[...] = x_new
    _route_tail(x_new, m, *route_in, *route_out)


def _gla_mixer(xs, mods, g, w_in, w_a2, b_a, g_o, w_out, geo, dims, route_w):
    b, seq, lctx = dims
    nt, d = xs.shape
    n_tiles = nt // TM
    kdim = w_a2.shape[2]
    vdim = g_o.shape[0] * GLA_HEADS
    hk = kdim // GLA_HEADS
    nf = hk // 4
    wq = w_in[:, :kdim].astype(BF16)
    wk = w_in[:, kdim:2 * kdim].astype(BF16)
    wv = w_in[:, 2 * kdim:2 * kdim + vdim].astype(BF16)
    wg = w_in[:, 2 * kdim + vdim:2 * kdim + 2 * vdim].astype(BF16)
    wr = jnp.pad(w_in[:, 2 * kdim + 2 * vdim:], ((0, 0), (0, 128 - 2 * GLA_RANK))).astype(BF16)
    wa = jnp.zeros((128, 2 * kdim), F32)
    wa = wa.at[:GLA_RANK, :kdim].set(w_a2[0]).at[GLA_RANK:2 * GLA_RANK, kdim:].set(w_a2[1]).astype(BF16)
    ba = b_a.reshape(1, 2 * kdim)

    tpos = jnp.arange(seq)
    freqs = jnp.power(ROPE_BASE, -jnp.arange(nf, dtype=F32) / nf)
    ar = (tpos // GRID_W).astype(F32)[:, None] * freqs
    ac = (tpos % GRID_W).astype(F32)[:, None] * freqs
    cos = jnp.concatenate([jnp.cos(ar), jnp.cos(ar), jnp.cos(ac), jnp.cos(ac)], axis=1)
    sin = jnp.concatenate([-jnp.sin(ar), jnp.sin(ar), -jnp.sin(ac), jnp.sin(ac)], axis=1)
    cos = jnp.concatenate([jnp.ones((TM, hk), F32), cos], axis=0)
    sin = jnp.concatenate([jnp.zeros((TM, hk), F32), sin], axis=0)

    n_ctx_tiles, tiles_per_batch, _ = geo
    grp = functools.partial(_tile_group, geo=geo)

    def rope_blk(t):
        return (jnp.where(t < n_ctx_tiles, 0, 1 + (t - n_ctx_tiles) % tiles_per_batch), 0)

    const2 = lambda t: (0, 0)
    row = lambda t: (t, 0)
    q, k, v, go, la = pl.pallas_call(
        _gla_proj_kernel,
        grid=(n_tiles,),
        in_specs=[pl.BlockSpec((TM, d), row),
                  pl.BlockSpec((1, 6, d), lambda t: (grp(t), 0, 0)),
                  pl.BlockSpec((1, d), const2),
                  pl.BlockSpec((d, kdim), const2), pl.BlockSpec((d, kdim), const2),
                  pl.BlockSpec((d, vdim), const2), pl.BlockSpec((d, vdim), const2),
                  pl.BlockSpec((d, 128), const2), pl.BlockSpec((128, 2 * kdim), const2),
                  pl.BlockSpec((1, 2 * kdim), const2),
                  pl.BlockSpec((TM, hk), rope_blk), pl.BlockSpec((TM, hk), rope_blk)],
        out_specs=[pl.BlockSpec((TM, kdim), row), pl.BlockSpec((TM, kdim), row),
                   pl.BlockSpec((TM, vdim), row), pl.BlockSpec((TM, vdim), row),
                   pl.BlockSpec((TM, 2 * kdim), row)],
        out_shape=[jax.ShapeDtypeStruct((nt, kdim), F32), jax.ShapeDtypeStruct((nt, kdim), F32),
                   jax.ShapeDtypeStruct((nt, vdim), BF16), jax.ShapeDtypeStruct((nt, vdim), F32),
                   jax.ShapeDtypeStruct((nt, 2 * kdim), F32)],
        compiler_params=_cparams(("parallel",)),
        name="gla_proj",
    )(xs, mods, g.reshape(1, d), wq, wk, wv, wg, wr, wa, ba, cos, sin)

    tb = GLA_TB
    ctx_steps = lctx // tb
    lat_steps = seq // tb
    steps = ctx_steps + lat_steps

    def blk(bi, s, rev):
        if rev:
            cs, ls = ctx_steps - 1 - s, lat_steps - 1 - (s - ctx_steps)
        else:
            cs, ls = s, s - ctx_steps
        return jnp.where(s < ctx_steps, bi * ctx_steps + cs, b * ctx_steps + bi * lat_steps + ls)

    def dir_specs(rev):
        row_blk = lambda bi, s: (blk(bi, s, rev), 0)
        return [pl.BlockSpec((tb, kdim), row_blk), pl.BlockSpec((tb, kdim), row_blk),
                pl.BlockSpec((tb, vdim), row_blk),
                pl.BlockSpec((tb, kdim), lambda bi, s: (blk(bi, s, rev), 1 if rev else 0))]

    state = pltpu.VMEM((GLA_HEADS, vdim // GLA_HEADS, hk), F32)
    o_f, o_b = pl.pallas_call(
        _gla_scan_kernel,
        grid=(b, steps),
        in_specs=dir_specs(False) + dir_specs(True),
        out_specs=[pl.BlockSpec((tb, vdim), lambda bi, s: (blk(bi, s, False), 0)),
                   pl.BlockSpec((tb, vdim), lambda bi, s: (blk(bi, s, True), 0))],
        out_shape=[jax.ShapeDtypeStruct((nt, vdim), F32)] * 2,
        scratch_shapes=[state, state],
        compiler_params=_cparams(("parallel", "arbitrary")),
        name="gla_scan",
    )(q, k, v, la, q, k, v, la)

    r_in, r_out, r_shape, r_scratch = _route_specs(nt, d, route_w[1].shape[1])
    out = pl.pallas_call(
        _gla_out_kernel,
        grid=(n_tiles,),
        in_specs=[pl.BlockSpec((TM, d), row),
                  pl.BlockSpec((1, 6, d), lambda t: (grp(t), 0, 0)),
                  pl.BlockSpec((TM, vdim), row), pl.BlockSpec((TM, vdim), row),
                  pl.BlockSpec((TM, vdim), row),
                  pl.BlockSpec((1, vdim // GLA_HEADS), const2),
                  pl.BlockSpec((vdim, d), const2)] + r_in,
        out_specs=[pl.BlockSpec((TM, d), row)] + r_out,
        out_shape=[jax.ShapeDtypeStruct((nt, d), F32)] + r_shape,
        scratch_shapes=r_scratch,
        compiler_params=_cparams(("arbitrary",)),
        name="gla_out",
    )(xs, mods, o_f, o_b, go, g_o.reshape(1, -1), w_out.astype(BF16), *_route_args(route_w, d))
    return out[0], out[1:]


def _na_proj_kernel(x_ref, m_ref, g_ref, w_ref, q_ref, k_ref, v_ref):
    x = x_ref[...]
    m = m_ref[0]
    d = x.shape[1]
    h = _modulate(x, g_ref[...], m[0:1], m[1:2]).astype(BF16)
    qkv = _dot(h, w_ref[...])
    hd = d // NA_HEADS
    for p in range(q_ref.shape[0]):
        cs = slice(p * 128, (p + 1) * 128)
        q_ref[p] = (qkv[:, cs] * np.float32(hd ** -0.5)).astype(BF16)
        k_ref[p] = qkv[:, d + p * 128:d + (p + 1) * 128].astype(BF16)
        v_ref[p] = qkv[:, 2 * d + p * 128:2 * d + (p + 1) * 128].astype(BF16)


def _na_attn_kernel(tbl_ref, q_ref, *refs, n_tiles, nwin):
    k_refs, v_refs = refs[:nwin], refs[nwin:2 * nwin]
    kc_ref, vc_ref, bias_ref, o_ref, s_ref, p_ref = refs[2 * nwin:]
    t = pl.program_id(2)
    typ = jnp.where(t == 0, 0, jnp.where(t == n_tiles - 1, 2, 1))
    npair = NA_WROWS // 2
    nlat = NA_WROWS * GRID_W
    lane = lax.broadcasted_iota(I32, q_ref.shape[1:], 1)
    npp, tq = q_ref.shape[0], q_ref.shape[1]
    for pp in range(npp):
        q = q_ref[pp]
        k_all = jnp.concatenate([r[pp] for r in k_refs] + [kc_ref[pp]], axis=0)
        zero = jnp.zeros_like(q)
        q2 = jnp.concatenate([jnp.where(lane < 64, q, zero), jnp.where(lane >= 64, q, zero)], axis=0)
        s_ref[pp] = _dot_nt(q2, k_all)
    for pp in range(npp):
        for hh in range(2):
            for dr in range(NA_QROWS):
                rs = slice(hh * tq + dr * GRID_W, hh * tq + (dr + 1) * GRID_W)
                lat = [s_ref[pp, rs, j * 128:(j + 1) * 128]
                       + bias_ref[2 * pp + hh, tbl_ref[typ * (NA_QROWS * npair) + dr * npair + j]]
                       for j in range(npair)]
                sb = jnp.concatenate(lat + [s_ref[pp, rs, nlat:]], axis=1)
                e = jnp.exp(sb - jnp.max(sb, axis=1, keepdims=True))
                p_ref[pp, rs, :] = (e * (1.0 / jnp.sum(e, axis=1, keepdims=True))).astype(BF16)
    outs = []
    for pp in range(npp):
        v_all = jnp.concatenate([r[pp] for r in v_refs] + [vc_ref[pp]], axis=0)
        o2 = _dot(p_ref[pp], v_all)
        outs.append(jnp.where(lane < 64, o2[:tq], o2[tq:]))
    o_ref[...] = jnp.concatenate(outs, axis=1).astype(BF16)


def _na_out_kernel(x_ref, m_ref, a_ref, w_ref, *route_refs):
    route_in, o_ref, route_out = route_refs[:3], route_refs[3], route_refs[4:]
    m = m_ref[0]
    x_new = x_ref[...] + m[2:3] * _dot(a_ref[...], w_ref[...])
    o_ref[...] = x_new
    _route_tail(x_new, m, *route_in, *route_out)


def _na_tables(rows):
    npair = NA_WROWS // 2
    tbl = np.zeros((3, NA_QROWS, npair), np.int32)
    for typ, r_base in enumerate((0, NA_QROWS, rows - NA_QROWS)):
        w0 = int(np.clip(r_base - NA_KH // 2, 0, rows - NA_WROWS))
        for dr in range(NA_QROWS):
            r = r_base + dr
            r0 = int(np.clip(r - NA_KH // 2, 0, rows - NA_KH))
            for j in range(npair):
                kr = (w0 + 2 * j, w0 + 2 * j + 1)
                ok = [r0 <= x < r0 + NA_KH for x in kr]
                ri = [x - r + NA_KH - 1 for x in kr]
                if ok[0] and ok[1]:
                    e = ri[0]
                elif ok[0]:
                    e = 16 + ri[0]
                elif ok[1]:
                    e = 32 + ri[1]
                else:
                    e = 63
                tbl[typ, dr, j] = e
    return tbl.reshape(-1)


def _na_bias_kernel(rpb_ref, o_ref, *, nr, nw):
    h = pl.program_id(0)
    c = lax.broadcasted_iota(I32, (GRID_W, GRID_W), 0)
    k = lax.broadcasted_iota(I32, (GRID_W, GRID_W), 1)
    cstart = jnp.clip(c - NA_KW // 2, 0, GRID_W - NA_KW)
    ok = jnp.logical_and(k >= cstart, k < cstart + NA_KW)
    rel = k - c + NA_KW - 1
    neg = jnp.full((GRID_W, GRID_W), NEG_INF, F32)
    cbs = []
    for r in range(nr):
        acc = neg
        for w in range(nw):
            acc = jnp.where(rel == w, rpb_ref[(h * nr + r) * nw + w], acc)
        cbs.append(jnp.where(ok, acc, neg))
    cbs += [neg] * (17 - nr)
    for r in range(16):
        o_ref[0, r] = jnp.concatenate([cbs[r], cbs[r + 1]], axis=1)
        o_ref[0, 16 + r] = jnp.concatenate([cbs[r], neg], axis=1)
        o_ref[0, 32 + r] = jnp.concatenate([neg, cbs[r]], axis=1)
        o_ref[0, 48 + r] = jnp.concatenate([neg, neg], axis=1)


def _na_bias_table(rpb):
    nh, nr, nw = rpb.shape
    return pl.pallas_call(
        functools.partial(_na_bias_kernel, nr=nr, nw=nw),
        grid=(nh,),
        in_specs=[pl.BlockSpec(memory_space=pltpu.SMEM)],
        out_specs=pl.BlockSpec((1, 64, GRID_W, 2 * GRID_W), lambda h: (h, 0, 0, 0)),
        out_shape=jax.ShapeDtypeStruct((nh, 64, GRID_W, 2 * GRID_W), F32),
        compiler_params=_cparams(("parallel",)),
        name="na_bias_table",
    )(rpb.reshape(-1).astype(F32))


def _na_mixer(xs, mods, g, w_qkv, rpb, w_out, geo, dims, route_w):
    b, seq, lctx = dims
    nt, d = xs.shape
    n_tiles_tok = nt // TM
    grp = functools.partial(_tile_group, geo=geo)
    npairs = d // 128
    const2 = lambda t: (0, 0)
    q, k, v = pl.pallas_call(
        _na_proj_kernel,
        grid=(n_tiles_tok,),
        in_specs=[pl.BlockSpec((TM, d), lambda t: (t, 0)),
                  pl.BlockSpec((1, 6, d), lambda t: (grp(t), 0, 0)),
                  pl.BlockSpec((1, d), const2),
                  pl.BlockSpec((d, 3 * d), const2)],
        out_specs=[pl.BlockSpec((npairs, TM, 128), lambda t: (0, t, 0))] * 3,
        out_shape=[jax.ShapeDtypeStruct((npairs, nt, 128), BF16)] * 3,
        compiler_params=_cparams(("parallel",)),
        name="na_proj",
    )(xs, mods, g.reshape(1, d), w_qkv.astype(BF16))

    rows = seq // GRID_W
    tq = NA_QROWS * GRID_W
    n_tiles = rows // NA_QROWS
    wb = 256
    nwin = NA_WROWS * GRID_W // wb
    lat0 = b * lctx
    tbl = jnp.asarray(_na_tables(rows))
    bias = _na_bias_table(rpb)

    pp = NA_PAIRS_PER_STEP
    assert npairs % pp == 0 and (NA_QROWS * GRID_W) % wb == 0 and (NA_KH // 2 * GRID_W) % wb == 0

    def win(i):
        def f(p, bi, t, tbl_ref):
            w = jnp.clip(t * (NA_QROWS * GRID_W // wb) - NA_KH // 2 * GRID_W // wb, 0, seq // wb - nwin)
            return (p, (lat0 + bi * seq) // wb + w + i, 0)
        return f

    kv_specs = [pl.BlockSpec((pp, wb, 128), win(i)) for i in range(nwin)]
    attn = pl.pallas_call(
        functools.partial(_na_attn_kernel, n_tiles=n_tiles, nwin=nwin),
        grid_spec=pltpu.PrefetchScalarGridSpec(
            num_scalar_prefetch=1,
            grid=(npairs // pp, b, n_tiles),
            in_specs=[pl.BlockSpec((pp, tq, 128), lambda p, bi, t, tr: (p, (lat0 + bi * seq) // tq + t, 0))]
                     + kv_specs + kv_specs
                     + [pl.BlockSpec((pp, lctx, 128), lambda p, bi, t, tr: (p, bi, 0)),
                        pl.BlockSpec((pp, lctx, 128), lambda p, bi, t, tr: (p, bi, 0)),
                        pl.BlockSpec((2 * pp, 64, GRID_W, 2 * GRID_W), lambda p, bi, t, tr: (p, 0, 0, 0),
                                     pipeline_mode=pl.Buffered(1))],
            out_specs=pl.BlockSpec((tq, 128 * pp), lambda p, bi, t, tr: (bi * n_tiles + t, p)),
            scratch_shapes=[pltpu.VMEM((pp, 2 * tq, NA_WROWS * GRID_W + lctx), F32),
                            pltpu.VMEM((pp, 2 * tq, NA_WROWS * GRID_W + lctx), BF16)]),
        out_shape=jax.ShapeDtypeStruct((b * seq, d), BF16),
        compiler_params=_cparams(("parallel", "parallel", "arbitrary")),
        name="na_attn",
    )(tbl, q, *([k] * nwin), *([v] * nwin), k, v, bias)

    n_lat_tiles = b * seq // TM
    tile0 = lat0 // TM
    grp_l = functools.partial(_tile_group, geo=geo, tile0=tile0)
    r_in, r_out, r_shape, r_scratch = _route_specs(b * seq, d, route_w[1].shape[1])
    out = pl.pallas_call(
        _na_out_kernel,
        grid=(n_lat_tiles,),
        in_specs=[pl.BlockSpec((TM, d), lambda t: (t + tile0, 0)),
                  pl.BlockSpec((1, 6, d), lambda t: (grp_l(t), 0, 0)),
                  pl.BlockSpec((TM, d), lambda t: (t, 0)),
                  pl.BlockSpec((d, d), const2)] + r_in,
        out_specs=[pl.BlockSpec((TM, d), lambda t: (t, 0))] + r_out,
        out_shape=[jax.ShapeDtypeStruct((b * seq, d), F32)] + r_shape,
        scratch_shapes=r_scratch,
        compiler_params=_cparams(("arbitrary",)),
        name="na_out",
    )(xs, mods, attn, w_out.astype(BF16), *_route_args(route_w, d))
    return out[0], out[1:]


SC_CORES = 2
SC_SUBCORES = 16
SC_WORKERS = SC_CORES * SC_SUBCORES
SLOT_T = 512
COMBINE_PARTS = 2


def _sc_mesh():
    return plsc.VectorSubcoreMesh(core_axis_name="c", subcore_axis_name="s")


def _sc_chunk(per_worker, max_chunk):
    return max(c for c in range(8, max_chunk + 1, 8) if per_worker % (2 * c) == 0)


def _sc_gather_rows(table, idx):
    dd = table.shape[1]
    bsz = idx.shape[0]
    per_w = bsz // SC_WORKERS
    assert per_w * SC_WORKERS == bsz
    chunk = _sc_chunk(per_w, 64)
    n_chunks = per_w // chunk

    def body(table_hbm, idx_hbm, out_hbm, idx_v, rows0, rows1, g0, g1, w0, w1):
        wid = lax.axis_index("s") * SC_CORES + lax.axis_index("c")
        pltpu.sync_copy(idx_hbm.at[wid], idx_v)
        base = wid * per_w

        def out_rows(j):
            return out_hbm.at[pl.ds(pl.multiple_of(base + j * chunk, 8), chunk)]

        def step(i, carry):
            j0, j1 = 2 * i, 2 * i + 1
            ga = pltpu.async_copy(table_hbm.at[idx_v.at[j0]], rows0, g0)
            gb = pltpu.async_copy(table_hbm.at[idx_v.at[j1]], rows1, g1)
            ga.wait()
            wa = pltpu.async_copy(rows0, out_rows(j0), w0)
            gb.wait()
            wb = pltpu.async_copy(rows1, out_rows(j1), w1)
            wa.wait()
            wb.wait()
            return carry

        lax.fori_loop(0, n_chunks // 2, step, 0)

    return pl.kernel(
        body, out_type=jax.ShapeDtypeStruct((bsz, dd), table.dtype), mesh=_sc_mesh(),
        scratch_types=[pltpu.VMEM((n_chunks, chunk), I32),
                       pltpu.VMEM((chunk, dd), table.dtype), pltpu.VMEM((chunk, dd), table.dtype),
                       pltpu.SemaphoreType.DMA, pltpu.SemaphoreType.DMA,
                       pltpu.SemaphoreType.DMA, pltpu.SemaphoreType.DMA],
        name="sc_gather_rows",
    )(table, idx.reshape(SC_WORKERS, n_chunks, chunk))


def _sc_scatter_rows(rows, dest, n_out):
    n, dd = rows.shape
    kk = dest.shape[0]
    per_w = n // SC_WORKERS
    assert per_w * SC_WORKERS == n
    chunk = _sc_chunk(per_w, 64)
    n_chunks = per_w // chunk
    dest_w = dest.reshape(kk, SC_WORKERS, n_chunks, chunk).transpose(1, 2, 0, 3)
    dest_w = dest_w.reshape(SC_WORKERS, n_chunks * kk, chunk)

    def body(rows_hbm, dest_hbm, out_hbm, idx_v, rows0, rows1, r0, r1, s0, s1):
        wid = lax.axis_index("s") * SC_CORES + lax.axis_index("c")
        pltpu.sync_copy(dest_hbm.at[wid], idx_v)
        base = wid * per_w

        def in_rows(j):
            return rows_hbm.at[pl.ds(pl.multiple_of(base + j * chunk, 8), chunk)]

        def step(i, carry):
            j0, j1 = 2 * i, 2 * i + 1
            ra = pltpu.async_copy(in_rows(j0), rows0, r0)
            rb = pltpu.async_copy(in_rows(j1), rows1, r1)
            ra.wait()
            sa = [pltpu.async_copy(rows0, out_hbm.at[idx_v.at[j0 * kk + k]], s0) for k in range(kk)]
            rb.wait()
            sb = [pltpu.async_copy(rows1, out_hbm.at[idx_v.at[j1 * kk + k]], s1) for k in range(kk)]
            for cp in sa + sb:
                cp.wait()
            return carry

        lax.fori_loop(0, n_chunks // 2, step, 0)

    return pl.kernel(
        body, out_type=jax.ShapeDtypeStruct((n_out, dd), rows.dtype), mesh=_sc_mesh(),
        scratch_types=[pltpu.VMEM((n_chunks * kk, chunk), I32),
                       pltpu.VMEM((chunk, dd), rows.dtype), pltpu.VMEM((chunk, dd), rows.dtype),
                       pltpu.SemaphoreType.DMA, pltpu.SemaphoreType.DMA,
                       pltpu.SemaphoreType.DMA, pltpu.SemaphoreType.DMA],
        name="sc_scatter_rows",
    )(rows, dest_w)


def _prefix_sum_rows(col):
    nr = col.shape[0]
    acc = jnp.broadcast_to(col, (nr, 128))
    row = lax.broadcasted_iota(I32, (nr, 128), 0)
    s = 1
    while s < nr:
        acc = acc + jnp.where(row >= s, pltpu.roll(acc, s, 0), 0.0)
        s *= 2
    return acc[:, 0:1]


def _slot_kernel(idx_ref, rank_ref, cnt_ref, dest_ref, btab_ref, *, bm):
    cnt = cnt_ref[...]
    ne = cnt.shape[0]
    padded = jnp.floor((cnt + (bm - 1.0)) * (1.0 / bm)) * bm
    pad_end = _prefix_sum_rows(padded)
    pad_start = pad_end - padded
    idx = idx_ref[...]
    e_iota = lax.broadcasted_iota(I32, (ne, idx.shape[1]), 0)
    starts = [jnp.sum(jnp.where(e_iota == idx[k:k + 1], pad_start, 0.0), axis=0, keepdims=True)
              for k in range(idx.shape[0])]
    dest_ref[...] = jnp.concatenate(starts, axis=0).astype(I32) + rank_ref[...]

    nbp = btab_ref.shape[1]
    bstart = lax.broadcasted_iota(I32, (ne, nbp), 1).astype(F32) * bm
    be = jnp.minimum(jnp.sum((pad_end <= bstart).astype(F32), axis=0, keepdims=True), ne - 1.0)
    mine = lax.broadcasted_iota(I32, (ne, nbp), 0).astype(F32) == be
    pick = lambda col: jnp.sum(jnp.where(mine, col, 0.0), axis=0, keepdims=True)
    live = jnp.clip(pick(pad_start + cnt) - bstart[0:1], 0.0, bm)
    r = lax.broadcasted_iota(I32, (ne, ne), 0)
    c = lax.broadcasted_iota(I32, (ne, ne), 1)
    ends_on_lanes = _dot_f32(jnp.ones((ne, ne), BF16), jnp.where(r == c, pad_end, 0.0))
    nxt_e = jnp.sum((ends_on_lanes <= pad_end).astype(F32), axis=1, keepdims=True)
    nxt_e = jnp.where(nxt_e < ne, nxt_e, -1.0)
    ordinal = _prefix_sum_rows((cnt > 0).astype(F32)) - 1.0
    slot_e = ordinal - 2.0 * jnp.floor(ordinal * 0.5)
    rowi = lax.broadcasted_iota(I32, btab_ref.shape, 0)
    tab = jnp.where(rowi == 0, be, jnp.where(rowi == 1, live, jnp.where(rowi == 2, pick(nxt_e), pick(slot_e))))
    btab_ref[...] = tab.astype(I32)


def _route_specs(n, d, ne):
    const2 = lambda t: (0, 0)
    in_specs = [pl.BlockSpec((1, d), const2), pl.BlockSpec((ne, d), const2), pl.BlockSpec((ne, 1), const2)]
    out_specs = [pl.BlockSpec((TM, d // 2), lambda t: (t, 0)),
                 pl.BlockSpec((TOP_K, TM), lambda t: (0, t)),
                 pl.BlockSpec((TOP_K, TM), lambda t: (0, t)),
                 pl.BlockSpec((TOP_K, TM), lambda t: (0, t)),
                 pl.BlockSpec((ne, 1), const2)]
    out_shape = [jax.ShapeDtypeStruct((n, d // 2), jnp.uint32),
                 jax.ShapeDtypeStruct((TOP_K, n), I32),
                 jax.ShapeDtypeStruct((TOP_K, n), F32),
                 jax.ShapeDtypeStruct((TOP_K, n), I32),
                 jax.ShapeDtypeStruct((ne, 1), F32)]
    return in_specs, out_specs, out_shape, [pltpu.VMEM((ne, 1), F32)]


def _route_args(route_w, d):
    g_ffn, w_router, b_router = route_w
    ne = w_router.shape[1]
    return g_ffn.reshape(1, d), w_router.T.astype(BF16), b_router.reshape(ne, 1)


def _route_tail(x, m, g_ref, wr_ref, br_ref, h_ref, idx_ref, gate_ref, rank_ref, cnt_ref, run_ref):
    t = pl.program_id(0)

    @pl.when(t == 0)
    def _():
        run_ref[...] = jnp.zeros_like(run_ref)

    h = _modulate(x, g_ref[...], m[3:4], m[4:5]).astype(BF16)
    h_ref[...] = _pack_bf16_pairs(h)
    logits = _dot_nt(wr_ref[...], h) + br_ref[...]
    ne, tm = logits.shape
    e_iota = lax.broadcasted_iota(I32, (ne, tm), 0)
    vals, idxs = [], []
    l = logits
    for _ in range(TOP_K):
        mk = jnp.max(l, axis=0, keepdims=True)
        ik = jnp.min(jnp.where(l == mk, e_iota, ne), axis=0, keepdims=True)
        vals.append(mk)
        idxs.append(ik)
        l = jnp.where(e_iota == ik, -jnp.inf, l)
    top_val = jnp.concatenate(vals, axis=0)
    ex = jnp.exp(top_val - vals[0])
    gate_ref[...] = ex / jnp.sum(ex, axis=0, keepdims=True)
    idx_ref[...] = jnp.concatenate(idxs, axis=0)

    hits = [e_iota == ik for ik in idxs]
    cnt = hits[0].astype(F32)
    for hk in hits[1:]:
        cnt = cnt + hk.astype(F32)
    si = lax.broadcasted_iota(I32, (tm, tm), 0)
    ti = lax.broadcasted_iota(I32, (tm, tm), 1)
    before = (si < ti).astype(BF16)
    total = _dot(cnt.astype(BF16), before) + run_ref[...]
    ranks = [jnp.sum(jnp.where(hk, total, 0.0), axis=0, keepdims=True) for hk in hits]
    rank_ref[...] = jnp.concatenate(ranks, axis=0).astype(I32)
    run_ref[...] = run_ref[...] + jnp.sum(cnt, axis=1, keepdims=True)
    cnt_ref[...] = run_ref[...]


def _pack_bf16_pairs(v):
    bits = lax.bitcast_convert_type(v.astype(BF16).astype(F32), jnp.uint32)
    half = bits.shape[1] // 2
    return (bits[:, half:] & jnp.uint32(0xFFFF0000)) | (bits[:, :half] >> 16)


def _unpack_bf16_pairs(w):
    return (lax.bitcast_convert_type(w << 16, F32),
            lax.bitcast_convert_type(w & jnp.uint32(0xFFFF0000), F32))


def _expert_kernel(tab_ref, x_ref, wgu_hbm, bgu_ref, wd_hbm, bd_ref, o_ref,
                   wgu_in, wd_in, wgu_bf, wd_bf, act_ref, sem_gu, sem_d, *, layer):
    blk = pl.program_id(0)
    ch = wgu_bf.shape[2]
    f = act_ref.shape[1]
    nc = f // ch
    expert, live, nxt, wslot = (tab_ref[i, blk] for i in range(4))
    used = live > 0
    fresh = jnp.logical_or(blk == 0, expert != tab_ref[0, jnp.maximum(blk - 1, 0)])

    def fetch(e, slot):
        return (pltpu.make_async_copy(wgu_hbm.at[layer, e], wgu_in.at[slot], sem_gu.at[slot]),
                pltpu.make_async_copy(wd_hbm.at[layer, e], wd_in.at[slot], sem_d.at[slot]))

    @pl.when(jnp.logical_and(used, fresh))
    def _():
        slot = wslot
        mine = fetch(expert, slot)

        @pl.when(blk == 0)
        def _():
            for cp in mine:
                cp.start()

        @pl.when(nxt >= 0)
        def _():
            for cp in fetch(nxt, 1 - slot):
                cp.start()

        for cp in mine:
            cp.wait()
        for j in range(wgu_bf.shape[0]):
            wgu_bf[j] = wgu_in[slot, :, j * ch:(j + 1) * ch].astype(BF16)
        for j in range(wd_bf.shape[0]):
            wd_bf[j] = wd_in[slot, :, j * ch:(j + 1) * ch].astype(BF16)

    bm = x_ref.shape[0]

    def run(nr):
        w = x_ref[:nr, :]
        row = lax.broadcasted_iota(I32, w.shape, 0)
        w = jnp.where(row < live, w, jnp.zeros_like(w))
        lo, hi = _unpack_bf16_pairs(w)
        x = jnp.concatenate([lo.astype(BF16), hi.astype(BF16)], axis=1)
        for c in range(nc):
            c0 = slice(c * ch, (c + 1) * ch)
            c1 = slice(f + c * ch, f + (c + 1) * ch)
            glu = _dot(x, wgu_bf[c]) + bgu_ref[0, 0, :, c0]
            lin = _dot(x, wgu_bf[nc + c]) + bgu_ref[0, 0, :, c1]
            glu = jnp.minimum(glu, SWIGLU_LIMIT)
            lin = jnp.clip(lin, -SWIGLU_LIMIT, SWIGLU_LIMIT)
            act_ref[:nr, c0] = (glu * jax.nn.sigmoid(SWIGLU_ALPHA * glu) * (lin + 1.0)).astype(BF16)
        a = act_ref[:nr, :]
        y = jnp.concatenate([_dot(a, wd_bf[n]) for n in range(wd_bf.shape[0])], axis=1)
        o_ref[:nr, :] = _pack_bf16_pairs(y + bd_ref[0, 0])

    for nr in range(MOE_SUB, bm + 1, MOE_SUB):
        @pl.when(jnp.logical_and(live > nr - MOE_SUB, live <= nr))
        def _(nr=nr):
            run(nr)
            if nr < bm:
                o_ref[nr:, :] = jnp.zeros((bm - nr, o_ref.shape[1]), o_ref.dtype)

    @pl.when(jnp.logical_not(used))
    def _():
        o_ref[...] = jnp.zeros_like(o_ref)


def _expert_call(tab, buf, w_gu, b_gu, w_down, b_down, layer, bm):
    n_slots, dw = buf.shape
    _, ne, d, f2 = w_gu.shape
    f = f2 // 2
    n_blocks = n_slots // bm
    lw = lambda i, tb: (layer, tb[0, i], 0, 0)
    row = lambda i, tb: (i, 0)
    cw = 256
    return pl.pallas_call(
        functools.partial(_expert_kernel, layer=layer),
        grid_spec=pltpu.PrefetchScalarGridSpec(
            num_scalar_prefetch=1,
            grid=(n_blocks,),
            in_specs=[pl.BlockSpec((bm, dw), row),
                      pl.BlockSpec(memory_space=pl.ANY),
                      pl.BlockSpec((1, 1, 1, f2), lw),
                      pl.BlockSpec(memory_space=pl.ANY),
                      pl.BlockSpec((1, 1, 1, d), lw)],
            out_specs=pl.BlockSpec((bm, dw), row),
            scratch_shapes=[pltpu.VMEM((2, d, f2), F32), pltpu.VMEM((2, f, d), F32),
                            pltpu.VMEM((f2 // cw, d, cw), BF16), pltpu.VMEM((d // cw, f, cw), BF16),
                            pltpu.VMEM((bm, f), BF16),
                            pltpu.SemaphoreType.DMA((2,)), pltpu.SemaphoreType.DMA((2,))]),
        out_shape=jax.ShapeDtypeStruct((n_slots, dw), jnp.uint32),
        compiler_params=_cparams(("arbitrary",)),
        name="moe_experts",
    )(tab, buf, w_gu, b_gu.reshape(b_gu.shape[0], ne, 1, f2), w_down, b_down.reshape(b_down.shape[0], ne, 1, d))


def _combine_kernel(x_ref, m_ref, y_ref, gate_ref, gout_ref, *rest, final):
    o_ref = rest[-1]
    m = m_ref[0]
    gate = gate_ref[...]
    half = y_ref.shape[2]
    f_lo, f_hi = None, None
    for k in range(TOP_K):
        lo, hi = _unpack_bf16_pairs(y_ref[k])
        gk = gate[:, k:k + 1]
        f_lo = gk * lo if f_lo is None else f_lo + gk * lo
        f_hi = gk * hi if f_hi is None else f_hi + gk * hi
    x_lo = x_ref[:, :half] + m[5:6, :half] * f_lo
    x_hi = x_ref[:, half:] + m[5:6, half:] * f_hi
    if final:
        ms = (jnp.sum(x_lo * x_lo, axis=-1, keepdims=True) + jnp.sum(x_hi * x_hi, axis=-1, keepdims=True))
        r = lax.rsqrt(ms * (0.5 / half) + NORM_EPS)
        x_lo = x_lo * r * gout_ref[:, :half]
        x_hi = x_hi * r * gout_ref[:, half:]
    o_ref[:, :half] = x_lo
    o_ref[:, half:] = x_hi


def _moe_layer(xs, routed, mods, layer, w_gu, b_gu, w_down, b_down, geo, g_out, final):
    n, d = xs.shape
    h, idx, gate, rank, cnt = routed
    assert h.shape[0] == n
    ne = cnt.shape[0]
    n_tiles = n // TM
    grp = functools.partial(_tile_group, geo=geo)
    const2 = lambda t: (0, 0)

    bm = MOE_BM
    n_rows = n * TOP_K
    n_blocks = -(-n_rows // bm) + ne
    n_slots = n_blocks * bm
    nbp = -(-n_blocks // 128) * 128
    dest, btab = pl.pallas_call(
        functools.partial(_slot_kernel, bm=bm),
        grid=(n // SLOT_T,),
        in_specs=[pl.BlockSpec((TOP_K, SLOT_T), lambda t: (0, t)),
                  pl.BlockSpec((TOP_K, SLOT_T), lambda t: (0, t)),
                  pl.BlockSpec((ne, 1), const2)],
        out_specs=[pl.BlockSpec((TOP_K, SLOT_T), lambda t: (0, t)),
                   pl.BlockSpec((8, nbp), const2)],
        out_shape=[jax.ShapeDtypeStruct((TOP_K, n), I32),
                   jax.ShapeDtypeStruct((8, nbp), I32)],
        compiler_params=_cparams(("arbitrary",)),
        name="moe_slots",
    )(idx, rank, cnt)

    buf = _sc_scatter_rows(h, dest, n_slots)
    y = _expert_call(btab, buf, w_gu, b_gu, w_down, b_down, layer, bm)
    dy = d // 2
    gate_t = gate.T
    bounds = [n_tiles * p // COMBINE_PARTS for p in range(COMBINE_PARTS + 1)]
    out = None
    for t0, t1 in zip(bounds[:-1], bounds[1:]):
        rows = slice(t0 * TM, t1 * TM)
        ykp = _sc_gather_rows(y, dest[:, rows].reshape(-1)).reshape(TOP_K, (t1 - t0) * TM, dy)
        in_specs = [pl.BlockSpec((TM, d), lambda t, t0=t0: (t + t0, 0)),
                    pl.BlockSpec((1, 6, d), lambda t, t0=t0: (grp(t + t0), 0, 0)),
                    pl.BlockSpec((TOP_K, TM, dy), lambda t: (0, t, 0)),
                    pl.BlockSpec((TM, TOP_K), lambda t, t0=t0: (t + t0, 0)),
                    pl.BlockSpec((1, d), const2)]
        args = [xs, mods, ykp, gate_t, g_out.reshape(1, d)]
        if out is not None:
            in_specs.append(pl.BlockSpec(memory_space=pl.ANY))
            args.append(out)
        out = pl.pallas_call(
            functools.partial(_combine_kernel, final=final),
            grid=(t1 - t0,),
            in_specs=in_specs,
            out_specs=pl.BlockSpec((TM, d), lambda t, t0=t0: (t + t0, 0)),
            out_shape=jax.ShapeDtypeStruct((n, d), F32),
            input_output_aliases={} if len(args) == 5 else {5: 0},
            compiler_params=_cparams(("parallel",)),
            name="moe_combine",
        )(*args)
    return out


def kernel(x, c, ctx, c_ctx, ada_w, ada_b, norm_mix, norm_ffn, norm_out, a_w_in, a_g_v, a_w_s, a_b_s, a_w_out, b_w_in, b_w_a2, b_b_a, b_g_o, b_w_out, c_w_qkv, c_rpb, c_w_out, moe_w_router, moe_b_router, moe_w_gu, moe_b_gu, moe_w_down, moe_b_down):
    b, seq, d = x.shape
    lctx = ctx.shape[1]
    depth = ada_w.shape[0]
    assert (b * lctx) % TM == 0 and seq % TM == 0 and lctx % GLA_TB == 0 and seq % GLA_TB == 0
    assert seq % (NA_QROWS * GRID_W) == 0 and (b * lctx) % (NA_QROWS * GRID_W) == 0 and lctx % 256 == 0
    assert seq // GRID_W >= NA_WROWS + NA_QROWS and b + 1 <= 8
    geo = (b * lctx // TM, seq // TM, b)
    dims = (b, seq, lctx)

    cond = jnp.zeros((8, d), F32).at[:b].set(c).at[b].set(c_ctx)
    mods_all = _adaln(cond, ada_w, ada_b)[:, :b + 1].reshape(depth, b + 1, 6, d)

    xs = jnp.concatenate([ctx.reshape(b * lctx, d), x.reshape(b * seq, d)], axis=0)
    ctx_tiles = geo[0]
    has_ctx = True
    for i in range(depth):
        kind, j = i % N_MIXERS, i // N_MIXERS
        ctx_later = any(kk % N_MIXERS != 0 for kk in range(i + 1, depth))
        mods = mods_all[i]
        route_w = (norm_ffn[i], moe_w_router[i], moe_b_router[i])
        if kind == 0:
            keep_ctx = has_ctx and ctx_later
            skip = 0 if keep_ctx or not has_ctx else ctx_tiles
            geo_i = geo if has_ctx else (0, geo[1], b)
            xs, routed = _gmlp_mixer(xs, mods, norm_mix[i], a_w_in[j], a_g_v[j], a_w_s[j], a_b_s[j], a_w_out[j],
                                     skip, xs.shape[0] // TM - skip, geo_i, route_w)
            has_ctx = keep_ctx
        elif kind == 1:
            assert has_ctx
            if not ctx_later:
                raise NotImplementedError("GLA mixer whose context rows are dropped before the FFN")
            xs, routed = _gla_mixer(xs, mods, norm_mix[i], b_w_in[j], b_w_a2[j], b_b_a[j], b_g_o[j], b_w_out[j],
                                    geo, dims, route_w)
        else:
            assert has_ctx
            if ctx_later:
                raise NotImplementedError("context output of the neighbourhood mixer")
            xs, routed = _na_mixer(xs, mods, norm_mix[i], c_w_qkv[j], c_rpb[j], c_w_out[j], geo, dims, route_w)
            has_ctx = False
        geo_i = geo if has_ctx else (0, geo[1], b)
        xs = _moe_layer(xs, routed, mods, i, moe_w_gu, moe_b_gu, moe_w_down, moe_b_down, geo_i, norm_out,
                        i == depth - 1)
    if has_ctx:
        xs = xs[b * lctx:]
    return xs.reshape(b, seq, d)
```

```python
import functools

import numpy as np
import jax
import jax.numpy as jnp
from jax import lax
from jax.experimental import pallas as pl
from jax.experimental.pallas import tpu as pltpu
from jax.experimental.pallas import tpu_sc as plsc

F32 = jnp.float32
BF16 = jnp.bfloat16
I32 = jnp.int32

NORM_EPS = 1e-6
GRID_W = 64
N_MIXERS = 3

CHUNK_A = 128
A_GROUPS = 8
GMLP_PART = 256
GLA_HEADS = 4
GLA_RANK = 16
GLA_TAU = 16.0
GLA_CHUNK = 128
ROPE_BASE = 10000.0
NA_HEADS = 16
NA_KH = 8
NA_KW = 16
NEG_INF = -1e30
NA_QROWS = 4
NA_WROWS = 12
NA_PAIRS_PER_STEP = 4
TOP_K = 4
SWIGLU_LIMIT = 7.0
SWIGLU_ALPHA = 1.702
MOE_BM = 1024
MOE_SUB = 256

TM = 512
GLA_TB = 256
GLA_PROJ_PART = 256
VMEM_LIMIT = 56 * 1024 * 1024


def _cparams(sem):
    return pltpu.CompilerParams(dimension_semantics=sem, vmem_limit_bytes=VMEM_LIMIT)


def _dot(a, b):
    return jnp.dot(a, b, preferred_element_type=F32)


def _dot_nt(a, b):
    return lax.dot_general(a, b, (((1,), (1,)), ((), ())), preferred_element_type=F32)


def _dot_tn(a, b):
    return lax.dot_general(a, b, (((0,), (0,)), ((), ())), preferred_element_type=F32)


def _rms(x, g):
    return x * lax.rsqrt(jnp.mean(x * x, axis=-1, keepdims=True) + NORM_EPS) * g


def _modulate(x, g, shift, scale):
    return _rms(x, g) * (1.0 + scale) + shift


def _ada_kernel(s_ref, w_ref, b_ref, o_ref):
    s = s_ref[...]
    s = s * jax.nn.sigmoid(s)
    o_ref[0] = jnp.dot(s, w_ref[0], preferred_element_type=F32,
                       precision=lax.Precision.HIGHEST) + b_ref[0]


def _adaln(cond, ada_w, ada_b):
    depth, d, n6 = ada_w.shape
    bn = n6 // 4
    return pl.pallas_call(
        _ada_kernel,
        grid=(depth, n6 // bn),
        in_specs=[pl.BlockSpec((8, d), lambda i, j: (0, 0)),
                  pl.BlockSpec((1, d, bn), lambda i, j: (i, 0, j)),
                  pl.BlockSpec((1, 1, bn), lambda i, j: (i, 0, j))],
        out_specs=pl.BlockSpec((1, 8, bn), lambda i, j: (i, 0, j)),
        out_shape=jax.ShapeDtypeStruct((depth, 8, n6), F32),
        compiler_params=_cparams(("parallel", "parallel")),
        name="adaln",
    )(cond, ada_w, ada_b.reshape(depth, 1, n6))


def _gelu(z):
    return 0.5 * z * (1.0 + lax.erf(z * np.float32(np.sqrt(0.5))))


def _gmlp_kernel(x_ref, m_ref, g_ref, win_ref, gv_ref, ws_ref, bs_ref, wout_ref, *route_refs):
    route_in, o_ref, route_out = route_refs[:3], route_refs[3], route_refs[4:]
    m = m_ref[0]
    a = gv_ref.shape[1]
    gw = a // A_GROUPS
    npart = x_ref.shape[0] // GMLP_PART
    parts = [slice(p * GMLP_PART, (p + 1) * GMLP_PART) for p in range(npart)]
    z = [_dot(_modulate(x_ref[rs, :], g_ref[...], m[0:1], m[1:2]).astype(BF16), win_ref[...]) for rs in parts]
    for rs, zp in zip(parts, z):
        zp = _gelu(zp)
        u = zp[:, :a]
        v = _rms(zp[:, a:], gv_ref[...]).astype(BF16)
        rows = []
        for c in range(GMLP_PART // CHUNK_A):
            cols = [_dot(ws_ref[g], v[c * CHUNK_A:(c + 1) * CHUNK_A, g * gw:(g + 1) * gw])
                    for g in range(A_GROUPS)]
            rows.append(jnp.concatenate(cols, axis=1) + bs_ref[...])
        s = jnp.concatenate(rows, axis=0)
        y = _dot((u * s).astype(BF16), wout_ref[...])
        o_ref[rs, :] = x_ref[rs, :] + m[2:3] * y
    _route_tail(o_ref[...], m, *route_in, *route_out)


def _gmlp_mixer(xs, mods, g, w_in, g_v, w_s, b_s, w_out, tile0, n_tiles, geo, route_w):
    d = xs.shape[1]
    a = g_v.shape[0]
    gw = a // A_GROUPS
    bias = jnp.repeat(b_s.T, gw, axis=1)
    grp = functools.partial(_tile_group, geo=geo, tile0=tile0)
    const2 = lambda t: (0, 0)
    r_in, r_out, r_shape, r_scratch = _route_specs(n_tiles * TM, d, route_w[1].shape[1])
    out = pl.pallas_call(
        _gmlp_kernel,
        grid=(n_tiles,),
        in_specs=[pl.BlockSpec((TM, d), lambda t: (t + tile0, 0)),
                  pl.BlockSpec((1, 6, d), lambda t: (grp(t), 0, 0)),
                  pl.BlockSpec((1, d), const2),
                  pl.BlockSpec((d, 2 * a), const2),
                  pl.BlockSpec((1, a), const2),
                  pl.BlockSpec((A_GROUPS, CHUNK_A, CHUNK_A), lambda t: (0, 0, 0)),
                  pl.BlockSpec((CHUNK_A, a), const2),
                  pl.BlockSpec((a, d), const2)] + r_in,
        out_specs=[pl.BlockSpec((TM, d), lambda t: (t, 0))] + r_out,
        out_shape=[jax.ShapeDtypeStruct((n_tiles * TM, d), F32)] + r_shape,
        scratch_shapes=r_scratch,
        compiler_params=_cparams(("arbitrary",)),
        name="gmlp_mixer",
    )(xs, mods, g.reshape(1, d), w_in.astype(BF16), g_v.reshape(1, a), w_s.astype(BF16),
      bias, w_out.astype(BF16), *_route_args(route_w, d))
    return out[0], out[1:]


def _tile_group(t, geo, tile0=0):
    n_ctx_tiles, tiles_per_batch, nb = geo
    tt = t + tile0
    return jnp.where(tt < n_ctx_tiles, nb, (tt - n_ctx_tiles) // tiles_per_batch)


def _dot_f32(tri_bf16, x):
    hi = x.astype(BF16)
    r1 = x - hi.astype(F32)
    mid = r1.astype(BF16)
    lo = (r1 - mid.astype(F32)).astype(BF16)
    return _dot(tri_bf16, hi) + _dot(tri_bf16, mid) + _dot(tri_bf16, lo)


def _gla_proj_kernel(x_ref, m_ref, g_ref, wq_ref, wk_ref, wv_ref, wg_ref, wr_ref, wa_ref, ba_ref,
                     cos_ref, sin_ref, q_ref, k_ref, v_ref, go_ref, la_ref):
    m = m_ref[0]
    kdim = wq_ref.shape[1]
    hk = kdim // GLA_HEADS
    nf = hk // 4
    npart = x_ref.shape[0] // GLA_PROJ_PART
    parts = [slice(p * GLA_PROJ_PART, (p + 1) * GLA_PROJ_PART) for p in range(npart)]
    zs, qs, ks = [], [], []
    for rs in parts:
        h = _modulate(x_ref[rs, :], g_ref[...], m[0:1], m[1:2]).astype(BF16)
        zs.append(_dot(_dot(h, wr_ref[...]).astype(BF16), wa_ref[...]) + ba_ref[...])
        qs.append(_dot(h, wq_ref[...]) * np.float32(hk ** -0.5))
        ks.append(_dot(h, wk_ref[...]))
        v_ref[rs, :] = _dot(h, wv_ref[...]).astype(BF16)
        go_ref[rs, :] = _dot(h, wg_ref[...])
    for rs, z in zip(parts, zs):
        la_ref[rs, :] = jax.nn.log_sigmoid(z) * np.float32(1.0 / GLA_TAU)

    lane = lax.broadcasted_iota(I32, (GLA_PROJ_PART, kdim), 1)
    first = (lane % (2 * nf)) < nf
    for rs, q, k in zip(parts, qs, ks):
        cos = jnp.concatenate([cos_ref[rs, :]] * GLA_HEADS, axis=1)
        sin = jnp.concatenate([sin_ref[rs, :]] * GLA_HEADS, axis=1)

        def rope(t):
            up = pltpu.roll(t, kdim - nf, 1)
            dn = pltpu.roll(t, nf, 1)
            return t * cos + jnp.where(first, up, dn) * sin

        q_ref[rs, :] = rope(q)
        k_ref[rs, :] = rope(k)


def _gla_decays(q_ref, k_ref, la_ref, n, rev):
    c = GLA_CHUNK
    ri = lax.broadcasted_iota(I32, (c, c), 0)
    ci = lax.broadcasted_iota(I32, (c, c), 1)
    keep = (ci >= ri) if rev else (ci <= ri)
    rows = slice(n * c, (n + 1) * c)
    cum = _dot_f32(keep.astype(BF16), la_ref[rows, :])
    last = cum[0:1] if rev else cum[c - 1:c]
    q = q_ref[rows, :]
    k = k_ref[rows, :]
    mid = cum[c // 2:c // 2 + 1]
    return dict(rows=rows, keep=keep, dec=jnp.exp(last),
                q_dec=(q * jnp.exp(cum)).astype(BF16),
                q_att=(q * jnp.exp(cum - mid)).astype(BF16),
                k_att=(k * jnp.exp(mid - cum)).astype(BF16),
                k_end=(k * jnp.exp(last - cum)).astype(BF16))


def _gla_scan_kernel(qf_ref, kf_ref, vf_ref, laf_ref, qb_ref, kb_ref, vb_ref, lab_ref,
                     of_ref, ob_ref, stf_ref, stb_ref):
    s = pl.program_id(1)

    @pl.when(s == 0)
    def _():
        stf_ref[...] = jnp.zeros_like(stf_ref)
        stb_ref[...] = jnp.zeros_like(stb_ref)

    nch = qf_ref.shape[0] // GLA_CHUNK
    hk = qf_ref.shape[1] // GLA_HEADS
    hv = vf_ref.shape[1] // GLA_HEADS
    dirs = ((qf_ref, kf_ref, vf_ref, laf_ref, of_ref, stf_ref, False),
            (qb_ref, kb_ref, vb_ref, lab_ref, ob_ref, stb_ref, True))
    heads = [(slice(h * hk, (h + 1) * hk), slice(h * hv, (h + 1) * hv)) for h in range(GLA_HEADS)]
    for n in range(nch):
        w = [_gla_decays(q_ref, k_ref, la_ref, nch - 1 - n if rev else n, rev)
             for q_ref, k_ref, _, la_ref, _, _, rev in dirs]
        att = [[jnp.where(w[d]["keep"], _dot_nt(w[d]["q_att"][:, ks], w[d]["k_att"][:, ks]), 0.0).astype(BF16)
                for ks, _ in heads] for d in range(2)]
        for d, (_, _, v_ref, _, o_ref, st_ref, _) in enumerate(dirs):
            for h, (ks, vs) in enumerate(heads):
                o_ref[w[d]["rows"], vs] = (_dot(att[d][h], v_ref[w[d]["rows"], vs])
                                           + _dot_nt(w[d]["q_dec"][:, ks], st_ref[h].astype(BF16)))
        for d, (_, _, v_ref, _, _, st_ref, _) in enumerate(dirs):
            for h, (ks, vs) in enumerate(heads):
                st_ref[h] = (st_ref[h] * w[d]["dec"][:, ks]
                             + _dot_tn(v_ref[w[d]["rows"], vs], w[d]["k_end"][:, ks]))


def _gla_out_kernel(x_ref, m_ref, of_ref, ob_ref, go_ref, gn_ref, wout_ref, *route_refs):
    route_in, o_ref, route_out = route_refs[:3], route_refs[3], route_refs[4:]
    x = x_ref[...]
    m = m_ref[0]
    o = of_ref[...] + ob_ref[...]
    hv = gn_ref.shape[1]
    parts = [_rms(o[:, h * hv:(h + 1) * hv], gn_ref[...]) for h in range(GLA_HEADS)]
    o = jnp.concatenate(parts, axis=1)
    gate = go_ref[...]
    y = _dot((o * (gate * jax.nn.sigmoid(gate))).astype(BF16), wout_ref[...])
    x_new = x + m[2:3] * y
    o_ref[...] = x_new
    _route_tail(x_new, m, *route_in, *route_out)


def _gla_mixer(xs, mods, g, w_in, w_a2, b_a, g_o, w_out, geo, dims, route_w):
    b, seq, lctx = dims
    nt, d = xs.shape
    n_tiles = nt // TM
    kdim = w_a2.shape[2]
    vdim = g_o.shape[0] * GLA_HEADS
    hk = kdim // GLA_HEADS
    nf = hk // 4
    wq = w_in[:, :kdim].astype(BF16)
    wk = w_in[:, kdim:2 * kdim].astype(BF16)
    wv = w_in[:, 2 * kdim:2 * kdim + vdim].astype(BF16)
    wg = w_in[:, 2 * kdim + vdim:2 * kdim + 2 * vdim].astype(BF16)
    wr = jnp.pad(w_in[:, 2 * kdim + 2 * vdim:], ((0, 0), (0, 128 - 2 * GLA_RANK))).astype(BF16)
    wa = jnp.zeros((128, 2 * kdim), F32)
    wa = wa.at[:GLA_RANK, :kdim].set(w_a2[0]).at[GLA_RANK:2 * GLA_RANK, kdim:].set(w_a2[1]).astype(BF16)
    ba = b_a.reshape(1, 2 * kdim)

    tpos = jnp.arange(seq)
    freqs = jnp.power(ROPE_BASE, -jnp.arange(nf, dtype=F32) / nf)
    ar = (tpos // GRID_W).astype(F32)[:, None] * freqs
    ac = (tpos % GRID_W).astype(F32)[:, None] * freqs
    cos = jnp.concatenate([jnp.cos(ar), jnp.cos(ar), jnp.cos(ac), jnp.cos(ac)], axis=1)
    sin = jnp.concatenate([-jnp.sin(ar), jnp.sin(ar), -jnp.sin(ac), jnp.sin(ac)], axis=1)
    cos = jnp.concatenate([jnp.ones((TM, hk), F32), cos], axis=0)
    sin = jnp.concatenate([jnp.zeros((TM, hk), F32), sin], axis=0)

    n_ctx_tiles, tiles_per_batch, _ = geo
    grp = functools.partial(_tile_group, geo=geo)

    def rope_blk(t):
        return (jnp.where(t < n_ctx_tiles, 0, 1 + (t - n_ctx_tiles) % tiles_per_batch), 0)

    const2 = lambda t: (0, 0)
    row = lambda t: (t, 0)
    q, k, v, go, la = pl.pallas_call(
        _gla_proj_kernel,
        grid=(n_tiles,),
        in_specs=[pl.BlockSpec((TM, d), row),
                  pl.BlockSpec((1, 6, d), lambda t: (grp(t), 0, 0)),
                  pl.BlockSpec((1, d), const2),
                  pl.BlockSpec((d, kdim), const2), pl.BlockSpec((d, kdim), const2),
                  pl.BlockSpec((d, vdim), const2), pl.BlockSpec((d, vdim), const2),
                  pl.BlockSpec((d, 128), const2), pl.BlockSpec((128, 2 * kdim), const2),
                  pl.BlockSpec((1, 2 * kdim), const2),
                  pl.BlockSpec((TM, hk), rope_blk), pl.BlockSpec((TM, hk), rope_blk)],
        out_specs=[pl.BlockSpec((TM, kdim), row), pl.BlockSpec((TM, kdim), row),
                   pl.BlockSpec((TM, vdim), row), pl.BlockSpec((TM, vdim), row),
                   pl.BlockSpec((TM, 2 * kdim), row)],
        out_shape=[jax.ShapeDtypeStruct((nt, kdim), F32), jax.ShapeDtypeStruct((nt, kdim), F32),
                   jax.ShapeDtypeStruct((nt, vdim), BF16), jax.ShapeDtypeStruct((nt, vdim), F32),
                   jax.ShapeDtypeStruct((nt, 2 * kdim), F32)],
        compiler_params=_cparams(("parallel",)),
        name="gla_proj",
    )(xs, mods, g.reshape(1, d), wq, wk, wv, wg, wr, wa, ba, cos, sin)

    tb = GLA_TB
    ctx_steps = lctx // tb
    lat_steps = seq // tb
    steps = ctx_steps + lat_steps

    def blk(bi, s, rev):
        if rev:
            cs, ls = ctx_steps - 1 - s, lat_steps - 1 - (s - ctx_steps)
        else:
            cs, ls = s, s - ctx_steps
        return jnp.where(s < ctx_steps, bi * ctx_steps + cs, b * ctx_steps + bi * lat_steps + ls)

    def dir_specs(rev):
        row_blk = lambda bi, s: (blk(bi, s, rev), 0)
        return [pl.BlockSpec((tb, kdim), row_blk), pl.BlockSpec((tb, kdim), row_blk),
                pl.BlockSpec((tb, vdim), row_blk),
                pl.BlockSpec((tb, kdim), lambda bi, s: (blk(bi, s, rev), 1 if rev else 0))]

    state = pltpu.VMEM((GLA_HEADS, vdim // GLA_HEADS, hk), F32)
    o_f, o_b = pl.pallas_call(
        _gla_scan_kernel,
        grid=(b, steps),
        in_specs=dir_specs(False) + dir_specs(True),
        out_specs=[pl.BlockSpec((tb, vdim), lambda bi, s: (blk(bi, s, False), 0)),
                   pl.BlockSpec((tb, vdim), lambda bi, s: (blk(bi, s, True), 0))],
        out_shape=[jax.ShapeDtypeStruct((nt, vdim), F32)] * 2,
        scratch_shapes=[state, state],
        compiler_params=_cparams(("parallel", "arbitrary")),
        name="gla_scan",
    )(q, k, v, la, q, k, v, la)

    r_in, r_out, r_shape, r_scratch = _route_specs(nt, d, route_w[1].shape[1])
    out = pl.pallas_call(
        _gla_out_kernel,
        grid=(n_tiles,),
        in_specs=[pl.BlockSpec((TM, d), row),
                  pl.BlockSpec((1, 6, d), lambda t: (grp(t), 0, 0)),
                  pl.BlockSpec((TM, vdim), row), pl.BlockSpec((TM, vdim), row),
                  pl.BlockSpec((TM, vdim), row),
                  pl.BlockSpec((1, vdim // GLA_HEADS), const2),
                  pl.BlockSpec((vdim, d), const2)] + r_in,
        out_specs=[pl.BlockSpec((TM, d), row)] + r_out,
        out_shape=[jax.ShapeDtypeStruct((nt, d), F32)] + r_shape,
        scratch_shapes=r_scratch,
        compiler_params=_cparams(("arbitrary",)),
        name="gla_out",
    )(xs, mods, o_f, o_b, go, g_o.reshape(1, -1), w_out.astype(BF16), *_route_args(route_w, d))
    return out[0], out[1:]


def _na_proj_kernel(x_ref, m_ref, g_ref, w_ref, q_ref, k_ref, v_ref):
    x = x_ref[...]
    m = m_ref[0]
    d = x.shape[1]
    h = _modulate(x, g_ref[...], m[0:1], m[1:2]).astype(BF16)
    qkv = _dot(h, w_ref[...])
    hd = d // NA_HEADS
    for p in range(q_ref.shape[0]):
        cs = slice(p * 128, (p + 1) * 128)
        q_ref[p] = (qkv[:, cs] * np.float32(hd ** -0.5)).astype(BF16)
        k_ref[p] = qkv[:, d + p * 128:d + (p + 1) * 128].astype(BF16)
        v_ref[p] = qkv[:, 2 * d + p * 128:2 * d + (p + 1) * 128].astype(BF16)


def _na_attn_kernel(tbl_ref, q_ref, *refs, n_tiles, nwin):
    k_refs, v_refs = refs[:nwin], refs[nwin:2 * nwin]
    kc_ref, vc_ref, bias_ref, o_ref, s_ref, p_ref = refs[2 * nwin:]
    t = pl.program_id(2)
    typ = jnp.where(t == 0, 0, jnp.where(t == n_tiles - 1, 2, 1))
    npair = NA_WROWS // 2
    nlat = NA_WROWS * GRID_W
    lane = lax.broadcasted_iota(I32, q_ref.shape[1:], 1)
    npp, tq = q_ref.shape[0], q_ref.shape[1]
    for pp in range(npp):
        q = q_ref[pp]
        k_all = jnp.concatenate([r[pp] for r in k_refs] + [kc_ref[pp]], axis=0)
        zero = jnp.zeros_like(q)
        q2 = jnp.concatenate([jnp.where(lane < 64, q, zero), jnp.where(lane >= 64, q, zero)], axis=0)
        s_ref[pp] = _dot_nt(q2, k_all)
    for pp in range(npp):
        for hh in range(2):
            for dr in range(NA_QROWS):
                rs = slice(hh * tq + dr * GRID_W, hh * tq + (dr + 1) * GRID_W)
                lat = [s_ref[pp, rs, j * 128:(j + 1) * 128]
                       + bias_ref[2 * pp + hh, tbl_ref[typ * (NA_QROWS * npair) + dr * npair + j]]
                       for j in range(npair)]
                sb = jnp.concatenate(lat + [s_ref[pp, rs, nlat:]], axis=1)
                e = jnp.exp(sb - jnp.max(sb, axis=1, keepdims=True))
                p_ref[pp, rs, :] = (e * (1.0 / jnp.sum(e, axis=1, keepdims=True))).astype(BF16)
    outs = []
    for pp in range(npp):
        v_all = jnp.concatenate([r[pp] for r in v_refs] + [vc_ref[pp]], axis=0)
        o2 = _dot(p_ref[pp], v_all)
        outs.append(jnp.where(lane < 64, o2[:tq], o2[tq:]))
    o_ref[...] = jnp.concatenate(outs, axis=1).astype(BF16)


def _na_out_kernel(x_ref, m_ref, a_ref, w_ref, *route_refs):
    route_in, o_ref, route_out = route_refs[:3], route_refs[3], route_refs[4:]
    m = m_ref[0]
    x_new = x_ref[...] + m[2:3] * _dot(a_ref[...], w_ref[...])
    o_ref[...] = x_new
    _route_tail(x_new, m, *route_in, *route_out)


def _na_tables(rows):
    npair = NA_WROWS // 2
    tbl = np.zeros((3, NA_QROWS, npair), np.int32)
    for typ, r_base in enumerate((0, NA_QROWS, rows - NA_QROWS)):
        w0 = int(np.clip(r_base - NA_KH // 2, 0, rows - NA_WROWS))
        for dr in range(NA_QROWS):
            r = r_base + dr
            r0 = int(np.clip(r - NA_KH // 2, 0, rows - NA_KH))
            for j in range(npair):
                kr = (w0 + 2 * j, w0 + 2 * j + 1)
                ok = [r0 <= x < r0 + NA_KH for x in kr]
                ri = [x - r + NA_KH - 1 for x in kr]
                if ok[0] and ok[1]:
                    e = ri[0]
                elif ok[0]:
                    e = 16 + ri[0]
                elif ok[1]:
                    e = 32 + ri[1]
                else:
                    e = 63
                tbl[typ, dr, j] = e
    return tbl.reshape(-1)


def _na_bias_kernel(rpb_ref, o_ref, *, nr, nw):
    h = pl.program_id(0)
    c = lax.broadcasted_iota(I32, (GRID_W, GRID_W), 0)
    k = lax.broadcasted_iota(I32, (GRID_W, GRID_W), 1)
    cstart = jnp.clip(c - NA_KW // 2, 0, GRID_W - NA_KW)
    ok = jnp.logical_and(k >= cstart, k < cstart + NA_KW)
    rel = k - c + NA_KW - 1
    neg = jnp.full((GRID_W, GRID_W), NEG_INF, F32)
    cbs = []
    for r in range(nr):
        acc = neg
        for w in range(nw):
            acc = jnp.where(rel == w, rpb_ref[(h * nr + r) * nw + w], acc)
        cbs.append(jnp.where(ok, acc, neg))
    cbs += [neg] * (17 - nr)
    for r in range(16):
        o_ref[0, r] = jnp.concatenate([cbs[r], cbs[r + 1]], axis=1)
        o_ref[0, 16 + r] = jnp.concatenate([cbs[r], neg], axis=1)
        o_ref[0, 32 + r] = jnp.concatenate([neg, cbs[r]], axis=1)
        o_ref[0, 48 + r] = jnp.concatenate([neg, neg], axis=1)


def _na_bias_table(rpb):
    nh, nr, nw = rpb.shape
    return pl.pallas_call(
        functools.partial(_na_bias_kernel, nr=nr, nw=nw),
        grid=(nh,),
        in_specs=[pl.BlockSpec(memory_space=pltpu.SMEM)],
        out_specs=pl.BlockSpec((1, 64, GRID_W, 2 * GRID_W), lambda h: (h, 0, 0, 0)),
        out_shape=jax.ShapeDtypeStruct((nh, 64, GRID_W, 2 * GRID_W), F32),
        compiler_params=_cparams(("parallel",)),
        name="na_bias_table",
    )(rpb.reshape(-1).astype(F32))


def _na_mixer(xs, mods, g, w_qkv, rpb, w_out, geo, dims, route_w):
    b, seq, lctx = dims
    nt, d = xs.shape
    n_tiles_tok = nt // TM
    grp = functools.partial(_tile_group, geo=geo)
    npairs = d // 128
    const2 = lambda t: (0, 0)
    q, k, v = pl.pallas_call(
        _na_proj_kernel,
        grid=(n_tiles_tok,),
        in_specs=[pl.BlockSpec((TM, d), lambda t: (t, 0)),
                  pl.BlockSpec((1, 6, d), lambda t: (grp(t), 0, 0)),
                  pl.BlockSpec((1, d), const2),
                  pl.BlockSpec((d, 3 * d), const2)],
        out_specs=[pl.BlockSpec((npairs, TM, 128), lambda t: (0, t, 0))] * 3,
        out_shape=[jax.ShapeDtypeStruct((npairs, nt, 128), BF16)] * 3,
        compiler_params=_cparams(("parallel",)),
        name="na_proj",
    )(xs, mods, g.reshape(1, d), w_qkv.astype(BF16))

    rows = seq // GRID_W
    tq = NA_QROWS * GRID_W
    n_tiles = rows // NA_QROWS
    wb = 256
    nwin = NA_WROWS * GRID_W // wb
    lat0 = b * lctx
    tbl = jnp.asarray(_na_tables(rows))
    bias = _na_bias_table(rpb)

    pp = NA_PAIRS_PER_STEP
    assert npairs % pp == 0 and (NA_QROWS * GRID_W) % wb == 0 and (NA_KH // 2 * GRID_W) % wb == 0

    def win(i):
        def f(p, bi, t, tbl_ref):
            w = jnp.clip(t * (NA_QROWS * GRID_W // wb) - NA_KH // 2 * GRID_W // wb, 0, seq // wb - nwin)
            return (p, (lat0 + bi * seq) // wb + w + i, 0)
        return f

    kv_specs = [pl.BlockSpec((pp, wb, 128), win(i)) for i in range(nwin)]
    attn = pl.pallas_call(
        functools.partial(_na_attn_kernel, n_tiles=n_tiles, nwin=nwin),
        grid_spec=pltpu.PrefetchScalarGridSpec(
            num_scalar_prefetch=1,
            grid=(npairs // pp, b, n_tiles),
            in_specs=[pl.BlockSpec((pp, tq, 128), lambda p, bi, t, tr: (p, (lat0 + bi * seq) // tq + t, 0))]
                     + kv_specs + kv_specs
                     + [pl.BlockSpec((pp, lctx, 128), lambda p, bi, t, tr: (p, bi, 0)),
                        pl.BlockSpec((pp, lctx, 128), lambda p, bi, t, tr: (p, bi, 0)),
                        pl.BlockSpec((2 * pp, 64, GRID_W, 2 * GRID_W), lambda p, bi, t, tr: (p, 0, 0, 0),
                                     pipeline_mode=pl.Buffered(1))],
            out_specs=pl.BlockSpec((tq, 128 * pp), lambda p, bi, t, tr: (bi * n_tiles + t, p)),
            scratch_shapes=[pltpu.VMEM((pp, 2 * tq, NA_WROWS * GRID_W + lctx), F32),
                            pltpu.VMEM((pp, 2 * tq, NA_WROWS * GRID_W + lctx), BF16)]),
        out_shape=jax.ShapeDtypeStruct((b * seq, d), BF16),
        compiler_params=_cparams(("parallel", "parallel", "arbitrary")),
        name="na_attn",
    )(tbl, q, *([k] * nwin), *([v] * nwin), k, v, bias)

    n_lat_tiles = b * seq // TM
    tile0 = lat0 // TM
    grp_l = functools.partial(_tile_group, geo=geo, tile0=tile0)
    r_in, r_out, r_shape, r_scratch = _route_specs(b * seq, d, route_w[1].shape[1])
    out = pl.pallas_call(
        _na_out_kernel,
        grid=(n_lat_tiles,),
        in_specs=[pl.BlockSpec((TM, d), lambda t: (t + tile0, 0)),
                  pl.BlockSpec((1, 6, d), lambda t: (grp_l(t), 0, 0)),
                  pl.BlockSpec((TM, d), lambda t: (t, 0)),
                  pl.BlockSpec((d, d), const2)] + r_in,
        out_specs=[pl.BlockSpec((TM, d), lambda t: (t, 0))] + r_out,
        out_shape=[jax.ShapeDtypeStruct((b * seq, d), F32)] + r_shape,
        scratch_shapes=r_scratch,
        compiler_params=_cparams(("arbitrary",)),
        name="na_out",
    )(xs, mods, attn, w_out.astype(BF16), *_route_args(route_w, d))
    return out[0], out[1:]


SC_CORES = 2
SC_SUBCORES = 16
SC_WORKERS = SC_CORES * SC_SUBCORES
SLOT_T = 512


def _sc_mesh():
    return plsc.VectorSubcoreMesh(core_axis_name="c", subcore_axis_name="s")


def _sc_chunk(per_worker, max_chunk):
    return max(c for c in range(8, max_chunk + 1, 8) if per_worker % (2 * c) == 0)


def _sc_gather_rows(table, idx):
    dd = table.shape[1]
    bsz = idx.shape[0]
    per_w = bsz // SC_WORKERS
    assert per_w * SC_WORKERS == bsz
    chunk = _sc_chunk(per_w, 64)
    n_chunks = per_w // chunk

    def body(table_hbm, idx_hbm, out_hbm, idx_v, rows0, rows1, g0, g1, w0, w1):
        wid = lax.axis_index("s") * SC_CORES + lax.axis_index("c")
        pltpu.sync_copy(idx_hbm.at[wid], idx_v)
        base = wid * per_w

        def out_rows(j):
            return out_hbm.at[pl.ds(pl.multiple_of(base + j * chunk, 8), chunk)]

        def step(i, carry):
            j0, j1 = 2 * i, 2 * i + 1
            ga = pltpu.async_copy(table_hbm.at[idx_v.at[j0]], rows0, g0)
            gb = pltpu.async_copy(table_hbm.at[idx_v.at[j1]], rows1, g1)
            ga.wait()
            wa = pltpu.async_copy(rows0, out_rows(j0), w0)
            gb.wait()
            wb = pltpu.async_copy(rows1, out_rows(j1), w1)
            wa.wait()
            wb.wait()
            return carry

        lax.fori_loop(0, n_chunks // 2, step, 0)

    return pl.kernel(
        body, out_type=jax.ShapeDtypeStruct((bsz, dd), table.dtype), mesh=_sc_mesh(),
        scratch_types=[pltpu.VMEM((n_chunks, chunk), I32),
                       pltpu.VMEM((chunk, dd), table.dtype), pltpu.VMEM((chunk, dd), table.dtype),
                       pltpu.SemaphoreType.DMA, pltpu.SemaphoreType.DMA,
                       pltpu.SemaphoreType.DMA, pltpu.SemaphoreType.DMA],
        name="sc_gather_rows",
    )(table, idx.reshape(SC_WORKERS, n_chunks, chunk))


def _sc_scatter_rows(rows, dest, n_out):
    n, dd = rows.shape
    kk = dest.shape[0]
    per_w = n // SC_WORKERS
    assert per_w * SC_WORKERS == n
    chunk = _sc_chunk(per_w, 64)
    n_chunks = per_w // chunk
    dest_w = dest.reshape(kk, SC_WORKERS, n_chunks, chunk).transpose(1, 2, 0, 3)
    dest_w = dest_w.reshape(SC_WORKERS, n_chunks * kk, chunk)

    def body(rows_hbm, dest_hbm, out_hbm, idx_v, rows0, rows1, r0, r1, s0, s1):
        wid = lax.axis_index("s") * SC_CORES + lax.axis_index("c")
        pltpu.sync_copy(dest_hbm.at[wid], idx_v)
        base = wid * per_w

        def in_rows(j):
            return rows_hbm.at[pl.ds(pl.multiple_of(base + j * chunk, 8), chunk)]

        def step(i, carry):
            j0, j1 = 2 * i, 2 * i + 1
            ra = pltpu.async_copy(in_rows(j0), rows0, r0)
            rb = pltpu.async_copy(in_rows(j1), rows1, r1)
            ra.wait()
            sa = [pltpu.async_copy(rows0, out_hbm.at[idx_v.at[j0 * kk + k]], s0) for k in range(kk)]
            rb.wait()
            sb = [pltpu.async_copy(rows1, out_hbm.at[idx_v.at[j1 * kk + k]], s1) for k in range(kk)]
            for cp in sa + sb:
                cp.wait()
            return carry

        lax.fori_loop(0, n_chunks // 2, step, 0)

    return pl.kernel(
        body, out_type=jax.ShapeDtypeStruct((n_out, dd), rows.dtype), mesh=_sc_mesh(),
        scratch_types=[pltpu.VMEM((n_chunks * kk, chunk), I32),
                       pltpu.VMEM((chunk, dd), rows.dtype), pltpu.VMEM((chunk, dd), rows.dtype),
                       pltpu.SemaphoreType.DMA, pltpu.SemaphoreType.DMA,
                       pltpu.SemaphoreType.DMA, pltpu.SemaphoreType.DMA],
        name="sc_scatter_rows",
    )(rows, dest_w)


def _prefix_sum_rows(col):
    nr = col.shape[0]
    acc = jnp.broadcast_to(col, (nr, 128))
    row = lax.broadcasted_iota(I32, (nr, 128), 0)
    s = 1
    while s < nr:
        acc = acc + jnp.where(row >= s, pltpu.roll(acc, s, 0), 0.0)
        s *= 2
    return acc[:, 0:1]


def _slot_kernel(idx_ref, rank_ref, cnt_ref, dest_ref, btab_ref, *, bm):
    cnt = cnt_ref[...]
    ne = cnt.shape[0]
    padded = jnp.floor((cnt + (bm - 1.0)) * (1.0 / bm)) * bm
    pad_end = _prefix_sum_rows(padded)
    pad_start = pad_end - padded
    idx = idx_ref[...]
    e_iota = lax.broadcasted_iota(I32, (ne, idx.shape[1]), 0)
    starts = [jnp.sum(jnp.where(e_iota == idx[k:k + 1], pad_start, 0.0), axis=0, keepdims=True)
              for k in range(idx.shape[0])]
    dest_ref[...] = jnp.concatenate(starts, axis=0).astype(I32) + rank_ref[...]

    nbp = btab_ref.shape[1]
    bstart = lax.broadcasted_iota(I32, (ne, nbp), 1).astype(F32) * bm
    be = jnp.minimum(jnp.sum((pad_end <= bstart).astype(F32), axis=0, keepdims=True), ne - 1.0)
    mine = lax.broadcasted_iota(I32, (ne, nbp), 0).astype(F32) == be
    pick = lambda col: jnp.sum(jnp.where(mine, col, 0.0), axis=0, keepdims=True)
    live = jnp.clip(pick(pad_start + cnt) - bstart[0:1], 0.0, bm)
    r = lax.broadcasted_iota(I32, (ne, ne), 0)
    c = lax.broadcasted_iota(I32, (ne, ne), 1)
    ends_on_lanes = _dot_f32(jnp.ones((ne, ne), BF16), jnp.where(r == c, pad_end, 0.0))
    nxt_e = jnp.sum((ends_on_lanes <= pad_end).astype(F32), axis=1, keepdims=True)
    nxt_e = jnp.where(nxt_e < ne, nxt_e, -1.0)
    ordinal = _prefix_sum_rows((cnt > 0).astype(F32)) - 1.0
    slot_e = ordinal - 2.0 * jnp.floor(ordinal * 0.5)
    rowi = lax.broadcasted_iota(I32, btab_ref.shape, 0)
    tab = jnp.where(rowi == 0, be, jnp.where(rowi == 1, live, jnp.where(rowi == 2, pick(nxt_e), pick(slot_e))))
    btab_ref[...] = tab.astype(I32)


def _route_specs(n, d, ne):
    const2 = lambda t: (0, 0)
    in_specs = [pl.BlockSpec((1, d), const2), pl.BlockSpec((ne, d), const2), pl.BlockSpec((ne, 1), const2)]
    out_specs = [pl.BlockSpec((TM, d // 2), lambda t: (t, 0)),
                 pl.BlockSpec((TOP_K, TM), lambda t: (0, t)),
                 pl.BlockSpec((TOP_K, TM), lambda t: (0, t)),
                 pl.BlockSpec((TOP_K, TM), lambda t: (0, t)),
                 pl.BlockSpec((ne, 1), const2)]
    out_shape = [jax.ShapeDtypeStruct((n, d // 2), jnp.uint32),
                 jax.ShapeDtypeStruct((TOP_K, n), I32),
                 jax.ShapeDtypeStruct((TOP_K, n), F32),
                 jax.ShapeDtypeStruct((TOP_K, n), I32),
                 jax.ShapeDtypeStruct((ne, 1), F32)]
    return in_specs, out_specs, out_shape, [pltpu.VMEM((ne, 1), F32)]


def _route_args(route_w, d):
    g_ffn, w_router, b_router = route_w
    ne = w_router.shape[1]
    return g_ffn.reshape(1, d), w_router.T.astype(BF16), b_router.reshape(ne, 1)


def _route_tail(x, m, g_ref, wr_ref, br_ref, h_ref, idx_ref, gate_ref, rank_ref, cnt_ref, run_ref):
    t = pl.program_id(0)

    @pl.when(t == 0)
    def _():
        run_ref[...] = jnp.zeros_like(run_ref)

    h = _modulate(x, g_ref[...], m[3:4], m[4:5]).astype(BF16)
    h_ref[...] = _pack_bf16_pairs(h)
    logits = _dot_nt(wr_ref[...], h) + br_ref[...]
    ne, tm = logits.shape
    e_iota = lax.broadcasted_iota(I32, (ne, tm), 0)
    vals, idxs = [], []
    l = logits
    for _ in range(TOP_K):
        mk = jnp.max(l, axis=0, keepdims=True)
        ik = jnp.min(jnp.where(l == mk, e_iota, ne), axis=0, keepdims=True)
        vals.append(mk)
        idxs.append(ik)
        l = jnp.where(e_iota == ik, -jnp.inf, l)
    top_val = jnp.concatenate(vals, axis=0)
    ex = jnp.exp(top_val - vals[0])
    gate_ref[...] = ex / jnp.sum(ex, axis=0, keepdims=True)
    idx_ref[...] = jnp.concatenate(idxs, axis=0)

    hits = [e_iota == ik for ik in idxs]
    cnt = hits[0].astype(F32)
    for hk in hits[1:]:
        cnt = cnt + hk.astype(F32)
    si = lax.broadcasted_iota(I32, (tm, tm), 0)
    ti = lax.broadcasted_iota(I32, (tm, tm), 1)
    before = (si < ti).astype(BF16)
    total = _dot(cnt.astype(BF16), before) + run_ref[...]
    ranks = [jnp.sum(jnp.where(hk, total, 0.0), axis=0, keepdims=True) for hk in hits]
    rank_ref[...] = jnp.concatenate(ranks, axis=0).astype(I32)
    run_ref[...] = run_ref[...] + jnp.sum(cnt, axis=1, keepdims=True)
    cnt_ref[...] = run_ref[...]


def _pack_bf16_pairs(v):
    bits = lax.bitcast_convert_type(v.astype(BF16).astype(F32), jnp.uint32)
    half = bits.shape[1] // 2
    return (bits[:, half:] & jnp.uint32(0xFFFF0000)) | (bits[:, :half] >> 16)


def _unpack_bf16_pairs(w):
    return (lax.bitcast_convert_type(w << 16, F32),
            lax.bitcast_convert_type(w & jnp.uint32(0xFFFF0000), F32))


def _expert_kernel(tab_ref, x_ref, wgu_hbm, bgu_ref, wd_hbm, bd_ref, o_ref,
                   wgu_in, wd_in, wgu_bf, wd_bf, act_ref, sem_gu, sem_d, *, layer):
    blk = pl.program_id(0)
    ch = wgu_bf.shape[2]
    f = act_ref.shape[1]
    nc = f // ch
    expert, live, nxt, wslot = (tab_ref[i, blk] for i in range(4))
    used = live > 0
    fresh = jnp.logical_or(blk == 0, expert != tab_ref[0, jnp.maximum(blk - 1, 0)])

    def fetch(e, slot):
        return (pltpu.make_async_copy(wgu_hbm.at[layer, e], wgu_in.at[slot], sem_gu.at[slot]),
                pltpu.make_async_copy(wd_hbm.at[layer, e], wd_in.at[slot], sem_d.at[slot]))

    @pl.when(jnp.logical_and(used, fresh))
    def _():
        slot = wslot
        mine = fetch(expert, slot)

        @pl.when(blk == 0)
        def _():
            for cp in mine:
                cp.start()

        @pl.when(nxt >= 0)
        def _():
            for cp in fetch(nxt, 1 - slot):
                cp.start()

        for cp in mine:
            cp.wait()

    bm = x_ref.shape[0]
    fresh_full = jnp.logical_and(fresh, live > bm - MOE_SUB)

    def cast_gu(j):
        wgu_bf[j] = wgu_in[wslot, :, j * ch:(j + 1) * ch].astype(BF16)

    def cast_d(j):
        wd_bf[j] = wd_in[wslot, :, j * ch:(j + 1) * ch].astype(BF16)

    @pl.when(jnp.logical_and(jnp.logical_and(used, fresh), jnp.logical_not(fresh_full)))
    def _():
        for j in range(wgu_bf.shape[0]):
            cast_gu(j)
        for j in range(wd_bf.shape[0]):
            cast_d(j)

    def run(nr, cast):
        w = x_ref[:nr, :]
        row = lax.broadcasted_iota(I32, w.shape, 0)
        w = jnp.where(row < live, w, jnp.zeros_like(w))
        lo, hi = _unpack_bf16_pairs(w)
        x = jnp.concatenate([lo.astype(BF16), hi.astype(BF16)], axis=1)
        for c in range(nc):
            c0 = slice(c * ch, (c + 1) * ch)
            c1 = slice(f + c * ch, f + (c + 1) * ch)
            if cast:
                cast_gu(c)
                cast_gu(nc + c)
            glu = _dot(x, wgu_bf[c]) + bgu_ref[0, 0, :, c0]
            lin = _dot(x, wgu_bf[nc + c]) + bgu_ref[0, 0, :, c1]
            glu = jnp.minimum(glu, SWIGLU_LIMIT)
            lin = jnp.clip(lin, -SWIGLU_LIMIT, SWIGLU_LIMIT)
            act_ref[:nr, c0] = (glu * jax.nn.sigmoid(SWIGLU_ALPHA * glu) * (lin + 1.0)).astype(BF16)
        a = act_ref[:nr, :]
        ys = []
        for n in range(wd_bf.shape[0]):
            if cast:
                cast_d(n)
            ys.append(_dot(a, wd_bf[n]))
        o_ref[:nr, :] = _pack_bf16_pairs(jnp.concatenate(ys, axis=1) + bd_ref[0, 0])

    for nr in range(MOE_SUB, bm + 1, MOE_SUB):
        @pl.when(jnp.logical_and(jnp.logical_and(live > nr - MOE_SUB, live <= nr), jnp.logical_not(fresh_full)))
        def _(nr=nr):
            run(nr, False)
            if nr < bm:
                o_ref[nr:, :] = jnp.zeros((bm - nr, o_ref.shape[1]), o_ref.dtype)

    @pl.when(fresh_full)
    def _():
        run(bm, True)

    @pl.when(jnp.logical_not(used))
    def _():
        o_ref[...] = jnp.zeros_like(o_ref)


def _expert_call(tab, buf, w_gu, b_gu, w_down, b_down, layer, bm):
    n_slots, dw = buf.shape
    _, ne, d, f2 = w_gu.shape
    f = f2 // 2
    n_blocks = n_slots // bm
    lw = lambda i, tb: (layer, tb[0, i], 0, 0)
    row = lambda i, tb: (i, 0)
    cw = 256
    return pl.pallas_call(
        functools.partial(_expert_kernel, layer=layer),
        grid_spec=pltpu.PrefetchScalarGridSpec(
            num_scalar_prefetch=1,
            grid=(n_blocks,),
            in_specs=[pl.BlockSpec((bm, dw), row),
                      pl.BlockSpec(memory_space=pl.ANY),
                      pl.BlockSpec((1, 1, 1, f2), lw),
                      pl.BlockSpec(memory_space=pl.ANY),
                      pl.BlockSpec((1, 1, 1, d), lw)],
            out_specs=pl.BlockSpec((bm, dw), row),
            scratch_shapes=[pltpu.VMEM((2, d, f2), F32), pltpu.VMEM((2, f, d), F32),
                            pltpu.VMEM((f2 // cw, d, cw), BF16), pltpu.VMEM((d // cw, f, cw), BF16),
                            pltpu.VMEM((bm, f), BF16),
                            pltpu.SemaphoreType.DMA((2,)), pltpu.SemaphoreType.DMA((2,))]),
        out_shape=jax.ShapeDtypeStruct((n_slots, dw), jnp.uint32),
        compiler_params=_cparams(("arbitrary",)),
        name="moe_experts",
    )(tab, buf, w_gu, b_gu.reshape(b_gu.shape[0], ne, 1, f2), w_down, b_down.reshape(b_down.shape[0], ne, 1, d))


def _combine_kernel(x_ref, m_ref, y_ref, gate_ref, gout_ref, o_ref, *, final):
    m = m_ref[0]
    gate = gate_ref[...]
    half = y_ref.shape[2]
    f_lo, f_hi = None, None
    for k in range(TOP_K):
        lo, hi = _unpack_bf16_pairs(y_ref[k])
        gk = gate[:, k:k + 1]
        f_lo = gk * lo if f_lo is None else f_lo + gk * lo
        f_hi = gk * hi if f_hi is None else f_hi + gk * hi
    x_lo = x_ref[:, :half] + m[5:6, :half] * f_lo
    x_hi = x_ref[:, half:] + m[5:6, half:] * f_hi
    if final:
        ms = (jnp.sum(x_lo * x_lo, axis=-1, keepdims=True) + jnp.sum(x_hi * x_hi, axis=-1, keepdims=True))
        r = lax.rsqrt(ms * (0.5 / half) + NORM_EPS)
        x_lo = x_lo * r * gout_ref[:, :half]
        x_hi = x_hi * r * gout_ref[:, half:]
    o_ref[:, :half] = x_lo
    o_ref[:, half:] = x_hi


def _moe_layer(xs, routed, mods, layer, w_gu, b_gu, w_down, b_down, geo, g_out, final):
    n, d = xs.shape
    h, idx, gate, rank, cnt = routed
    assert h.shape[0] == n
    ne = cnt.shape[0]
    n_tiles = n // TM
    grp = functools.partial(_tile_group, geo=geo)
    const2 = lambda t: (0, 0)

    bm = MOE_BM
    n_rows = n * TOP_K
    n_blocks = -(-n_rows // bm) + ne
    n_slots = n_blocks * bm
    nbp = -(-n_blocks // 128) * 128
    dest, btab = pl.pallas_call(
        functools.partial(_slot_kernel, bm=bm),
        grid=(n // SLOT_T,),
        in_specs=[pl.BlockSpec((TOP_K, SLOT_T), lambda t: (0, t)),
                  pl.BlockSpec((TOP_K, SLOT_T), lambda t: (0, t)),
                  pl.BlockSpec((ne, 1), const2)],
        out_specs=[pl.BlockSpec((TOP_K, SLOT_T), lambda t: (0, t)),
                   pl.BlockSpec((8, nbp), const2)],
        out_shape=[jax.ShapeDtypeStruct((TOP_K, n), I32),
                   jax.ShapeDtypeStruct((8, nbp), I32)],
        compiler_params=_cparams(("arbitrary",)),
        name="moe_slots",
    )(idx, rank, cnt)

    buf = _sc_scatter_rows(h, dest, n_slots)
    y = _expert_call(btab, buf, w_gu, b_gu, w_down, b_down, layer, bm)
    dy = d // 2
    yk = _sc_gather_rows(y, dest.reshape(-1)).reshape(TOP_K, n, dy)
    return pl.pallas_call(
        functools.partial(_combine_kernel, final=final),
        grid=(n_tiles,),
        in_specs=[pl.BlockSpec((TM, d), lambda t: (t, 0)),
                  pl.BlockSpec((1, 6, d), lambda t: (grp(t), 0, 0)),
                  pl.BlockSpec((TOP_K, TM, dy), lambda t: (0, t, 0)),
                  pl.BlockSpec((TM, TOP_K), lambda t: (t, 0)),
                  pl.BlockSpec((1, d), const2)],
        out_specs=pl.BlockSpec((TM, d), lambda t: (t, 0)),
        out_shape=jax.ShapeDtypeStruct((n, d), F32),
        compiler_params=_cparams(("parallel",)),
        name="moe_combine",
    )(xs, mods, yk, gate.T, g_out.reshape(1, d))


def kernel(x, c, ctx, c_ctx, ada_w, ada_b, norm_mix, norm_ffn, norm_out, a_w_in, a_g_v, a_w_s, a_b_s, a_w_out, b_w_in, b_w_a2, b_b_a, b_g_o, b_w_out, c_w_qkv, c_rpb, c_w_out, moe_w_router, moe_b_router, moe_w_gu, moe_b_gu, moe_w_down, moe_b_down):
    b, seq, d = x.shape
    lctx = ctx.shape[1]
    depth = ada_w.shape[0]
    assert (b * lctx) % TM == 0 and seq % TM == 0 and lctx % GLA_TB == 0 and seq % GLA_TB == 0
    assert seq % (NA_QROWS * GRID_W) == 0 and (b * lctx) % (NA_QROWS * GRID_W) == 0 and lctx % 256 == 0
    assert seq // GRID_W >= NA_WROWS + NA_QROWS and b + 1 <= 8
    geo = (b * lctx // TM, seq // TM, b)
    dims = (b, seq, lctx)

    cond = jnp.zeros((8, d), F32).at[:b].set(c).at[b].set(c_ctx)
    mods_all = _adaln(cond, ada_w, ada_b)[:, :b + 1].reshape(depth, b + 1, 6, d)

    xs = jnp.concatenate([ctx.reshape(b * lctx, d), x.reshape(b * seq, d)], axis=0)
    ctx_tiles = geo[0]
    has_ctx = True
    for i in range(depth):
        kind, j = i % N_MIXERS, i // N_MIXERS
        ctx_later = any(kk % N_MIXERS != 0 for kk in range(i + 1, depth))
        mods = mods_all[i]
        route_w = (norm_ffn[i], moe_w_router[i], moe_b_router[i])
        if kind == 0:
            keep_ctx = has_ctx and ctx_later
            skip = 0 if keep_ctx or not has_ctx else ctx_tiles
            geo_i = geo if has_ctx else (0, geo[1], b)
            xs, routed = _gmlp_mixer(xs, mods, norm_mix[i], a_w_in[j], a_g_v[j], a_w_s[j], a_b_s[j], a_w_out[j],
                                     skip, xs.shape[0] // TM - skip, geo_i, route_w)
            has_ctx = keep_ctx
        elif kind == 1:
            assert has_ctx
            if not ctx_later:
                raise NotImplementedError("GLA mixer whose context rows are dropped before the FFN")
            xs, routed = _gla_mixer(xs, mods, norm_mix[i], b_w_in[j], b_w_a2[j], b_b_a[j], b_g_o[j], b_w_out[j],
                                    geo, dims, route_w)
        else:
            assert has_ctx
            if ctx_later:
                raise NotImplementedError("context output of the neighbourhood mixer")
            xs, routed = _na_mixer(xs, mods, norm_mix[i], c_w_qkv[j], c_rpb[j], c_w_out[j], geo, dims, route_w)
            has_ctx = False
        geo_i = geo if has_ctx else (0, geo[1], b)
        xs = _moe_layer(xs, routed, mods, i, moe_w_gu, moe_b_gu, moe_w_down, moe_b_down, geo_i, norm_out,
                        i == depth - 1)
    if has_ctx:
        xs = xs[b * lctx:]
    return xs.reshape(b, seq, d)
```

```python
import functools

import numpy as np
import jax
import jax.numpy as jnp
from jax import lax
from jax.experimental import pallas as pl
from jax.experimental.pallas import tpu as pltpu
from jax.experimental.pallas import tpu_sc as plsc

F32 = jnp.float32
BF16 = jnp.bfloat16
I32 = jnp.int32

NORM_EPS = 1e-6
GRID_W = 64
N_MIXERS = 3

CHUNK_A = 128
A_GROUPS = 8
GMLP_PART = 256
GLA_HEADS = 4
GLA_RANK = 16
GLA_TAU = 16.0
GLA_CHUNK = 128
ROPE_BASE = 10000.0
NA_HEADS = 16
NA_KH = 8
NA_KW = 16
NEG_INF = -1e30
NA_QROWS = 4
NA_WROWS = 12
NA_PAIRS_PER_STEP = 4
TOP_K = 4
SWIGLU_LIMIT = 7.0
SWIGLU_ALPHA = 1.702
MOE_BM = 1024
MOE_SUB = 256

TM = 512
GLA_TB = 256
GLA_PROJ_PART = 256
VMEM_LIMIT = 56 * 1024 * 1024


def _cparams(sem):
    return pltpu.CompilerParams(dimension_semantics=sem, vmem_limit_bytes=VMEM_LIMIT)


def _dot(a, b):
    return jnp.dot(a, b, preferred_element_type=F32)


def _dot_nt(a, b):
    return lax.dot_general(a, b, (((1,), (1,)), ((), ())), preferred_element_type=F32)


def _dot_tn(a, b):
    return lax.dot_general(a, b, (((0,), (0,)), ((), ())), preferred_element_type=F32)


def _rms(x, g):
    return x * lax.rsqrt(jnp.mean(x * x, axis=-1, keepdims=True) + NORM_EPS) * g


def _modulate(x, g, shift, scale):
    return _rms(x, g) * (1.0 + scale) + shift


def _ada_kernel(s_ref, w_ref, b_ref, o_ref):
    s = s_ref[...]
    s = s * jax.nn.sigmoid(s)
    o_ref[0] = jnp.dot(s, w_ref[0], preferred_element_type=F32,
                       precision=lax.Precision.HIGHEST) + b_ref[0]


def _adaln(cond, ada_w, ada_b):
    depth, d, n6 = ada_w.shape
    bn = n6 // 4
    return pl.pallas_call(
        _ada_kernel,
        grid=(depth, n6 // bn),
        in_specs=[pl.BlockSpec((8, d), lambda i, j: (0, 0)),
                  pl.BlockSpec((1, d, bn), lambda i, j: (i, 0, j)),
                  pl.BlockSpec((1, 1, bn), lambda i, j: (i, 0, j))],
        out_specs=pl.BlockSpec((1, 8, bn), lambda i, j: (i, 0, j)),
        out_shape=jax.ShapeDtypeStruct((depth, 8, n6), F32),
        compiler_params=_cparams(("parallel", "parallel")),
        name="adaln",
    )(cond, ada_w, ada_b.reshape(depth, 1, n6))


def _gelu(z):
    return 0.5 * z * (1.0 + lax.erf(z * np.float32(np.sqrt(0.5))))


def _gmlp_kernel(x_ref, m_ref, g_ref, win_ref, gv_ref, ws_ref, bs_ref, wout_ref, *route_refs):
    route_in, o_ref, route_out = route_refs[:3], route_refs[3], route_refs[4:]
    m = m_ref[0]
    a = gv_ref.shape[1]
    gw = a // A_GROUPS
    npart = x_ref.shape[0] // GMLP_PART
    parts = [slice(p * GMLP_PART, (p + 1) * GMLP_PART) for p in range(npart)]
    z = [_dot(_modulate(x_ref[rs, :], g_ref[...], m[0:1], m[1:2]).astype(BF16), win_ref[...]) for rs in parts]
    for rs, zp in zip(parts, z):
        zp = _gelu(zp)
        u = zp[:, :a]
        v = _rms(zp[:, a:], gv_ref[...]).astype(BF16)
        rows = []
        for c in range(GMLP_PART // CHUNK_A):
            cols = [_dot(ws_ref[g], v[c * CHUNK_A:(c + 1) * CHUNK_A, g * gw:(g + 1) * gw])
                    for g in range(A_GROUPS)]
            rows.append(jnp.concatenate(cols, axis=1) + bs_ref[...])
        s = jnp.concatenate(rows, axis=0)
        y = _dot((u * s).astype(BF16), wout_ref[...])
        o_ref[rs, :] = x_ref[rs, :] + m[2:3] * y
    _route_tail(o_ref[...], m, *route_in, *route_out)


def _gmlp_mixer(xs, mods, g, w_in, g_v, w_s, b_s, w_out, tile0, n_tiles, geo, route_w):
    d = xs.shape[1]
    a = g_v.shape[0]
    gw = a // A_GROUPS
    bias = jnp.repeat(b_s.T, gw, axis=1)
    grp = functools.partial(_tile_group, geo=geo, tile0=tile0)
    const2 = lambda t: (0, 0)
    r_in, r_out, r_shape, r_scratch = _route_specs(n_tiles * TM, d, route_w[1].shape[1])
    out = pl.pallas_call(
        _gmlp_kernel,
        grid=(n_tiles,),
        in_specs=[pl.BlockSpec((TM, d), lambda t: (t + tile0, 0)),
                  pl.BlockSpec((1, 6, d), lambda t: (grp(t), 0, 0)),
                  pl.BlockSpec((1, d), const2),
                  pl.BlockSpec((d, 2 * a), const2),
                  pl.BlockSpec((1, a), const2),
                  pl.BlockSpec((A_GROUPS, CHUNK_A, CHUNK_A), lambda t: (0, 0, 0)),
                  pl.BlockSpec((CHUNK_A, a), const2),
                  pl.BlockSpec((a, d), const2)] + r_in,
        out_specs=[pl.BlockSpec((TM, d), lambda t: (t, 0))] + r_out,
        out_shape=[jax.ShapeDtypeStruct((n_tiles * TM, d), F32)] + r_shape,
        scratch_shapes=r_scratch,
        compiler_params=_cparams(("arbitrary",)),
        name="gmlp_mixer",
    )(xs, mods, g.reshape(1, d), w_in.astype(BF16), g_v.reshape(1, a), w_s.astype(BF16),
      bias, w_out.astype(BF16), *_route_args(route_w, d))
    return out[0], out[1:]


def _tile_group(t, geo, tile0=0):
    n_ctx_tiles, tiles_per_batch, nb = geo
    tt = t + tile0
    return jnp.where(tt < n_ctx_tiles, nb, (tt - n_ctx_tiles) // tiles_per_batch)


def _dot_f32(tri_bf16, x):
    hi = x.astype(BF16)
    r1 = x - hi.astype(F32)
    mid = r1.astype(BF16)
    lo = (r1 - mid.astype(F32)).astype(BF16)
    return _dot(tri_bf16, hi) + _dot(tri_bf16, mid) + _dot(tri_bf16, lo)


def _gla_proj_kernel(x_ref, m_ref, g_ref, wq_ref, wk_ref, wv_ref, wg_ref, wr_ref, wa_ref, ba_ref,
                     cos_ref, sin_ref, q_ref, k_ref, v_ref, go_ref, la_ref):
    m = m_ref[0]
    kdim = wq_ref.shape[1]
    hk = kdim // GLA_HEADS
    nf = hk // 4
    npart = x_ref.shape[0] // GLA_PROJ_PART
    parts = [slice(p * GLA_PROJ_PART, (p + 1) * GLA_PROJ_PART) for p in range(npart)]
    zs, qs, ks = [], [], []
    for rs in parts:
        h = _modulate(x_ref[rs, :], g_ref[...], m[0:1], m[1:2]).astype(BF16)
        zs.append(_dot(_dot(h, wr_ref[...]).astype(BF16), wa_ref[...]) + ba_ref[...])
        qs.append(_dot(h, wq_ref[...]) * np.float32(hk ** -0.5))
        ks.append(_dot(h, wk_ref[...]))
        v_ref[rs, :] = _dot(h, wv_ref[...]).astype(BF16)
        go_ref[rs, :] = _dot(h, wg_ref[...])
    for rs, z in zip(parts, zs):
        la_ref[rs, :] = jax.nn.log_sigmoid(z) * np.float32(1.0 / GLA_TAU)

    lane = lax.broadcasted_iota(I32, (GLA_PROJ_PART, kdim), 1)
    first = (lane % (2 * nf)) < nf
    for rs, q, k in zip(parts, qs, ks):
        cos = jnp.concatenate([cos_ref[rs, :]] * GLA_HEADS, axis=1)
        sin = jnp.concatenate([sin_ref[rs, :]] * GLA_HEADS, axis=1)

        def rope(t):
            up = pltpu.roll(t, kdim - nf, 1)
            dn = pltpu.roll(t, nf, 1)
            return t * cos + jnp.where(first, up, dn) * sin

        q_ref[rs, :] = rope(q)
        k_ref[rs, :] = rope(k)


def _gla_decays(q_ref, k_ref, la_ref, n, rev):
    c = GLA_CHUNK
    ri = lax.broadcasted_iota(I32, (c, c), 0)
    ci = lax.broadcasted_iota(I32, (c, c), 1)
    keep = (ci >= ri) if rev else (ci <= ri)
    rows = slice(n * c, (n + 1) * c)
    cum = _dot_f32(keep.astype(BF16), la_ref[rows, :])
    last = cum[0:1] if rev else cum[c - 1:c]
    q = q_ref[rows, :]
    k = k_ref[rows, :]
    mid = cum[c // 2:c // 2 + 1]
    return dict(rows=rows, keep=keep, dec=jnp.exp(last),
                q_dec=(q * jnp.exp(cum)).astype(BF16),
                q_att=(q * jnp.exp(cum - mid)).astype(BF16),
                k_att=(k * jnp.exp(mid - cum)).astype(BF16),
                k_end=(k * jnp.exp(last - cum)).astype(BF16))


def _gla_scan_kernel(qf_ref, kf_ref, vf_ref, laf_ref, qb_ref, kb_ref, vb_ref, lab_ref,
                     of_ref, ob_ref, stf_ref, stb_ref):
    s = pl.program_id(1)

    @pl.when(s == 0)
    def _():
        stf_ref[...] = jnp.zeros_like(stf_ref)
        stb_ref[...] = jnp.zeros_like(stb_ref)

    nch = qf_ref.shape[0] // GLA_CHUNK
    hk = qf_ref.shape[1] // GLA_HEADS
    hv = vf_ref.shape[1] // GLA_HEADS
    dirs = ((qf_ref, kf_ref, vf_ref, laf_ref, of_ref, stf_ref, False),
            (qb_ref, kb_ref, vb_ref, lab_ref, ob_ref, stb_ref, True))
    heads = [(slice(h * hk, (h + 1) * hk), slice(h * hv, (h + 1) * hv)) for h in range(GLA_HEADS)]
    for n in range(nch):
        w = [_gla_decays(q_ref, k_ref, la_ref, nch - 1 - n if rev else n, rev)
             for q_ref, k_ref, _, la_ref, _, _, rev in dirs]
        att = [[jnp.where(w[d]["keep"], _dot_nt(w[d]["q_att"][:, ks], w[d]["k_att"][:, ks]), 0.0).astype(BF16)
                for ks, _ in heads] for d in range(2)]
        for d, (_, _, v_ref, _, o_ref, st_ref, _) in enumerate(dirs):
            for h, (ks, vs) in enumerate(heads):
                o_ref[w[d]["rows"], vs] = (_dot(att[d][h], v_ref[w[d]["rows"], vs])
                                           + _dot_nt(w[d]["q_dec"][:, ks], st_ref[h].astype(BF16)))
        for d, (_, _, v_ref, _, _, st_ref, _) in enumerate(dirs):
            for h, (ks, vs) in enumerate(heads):
                st_ref[h] = (st_ref[h] * w[d]["dec"][:, ks]
                             + _dot_tn(v_ref[w[d]["rows"], vs], w[d]["k_end"][:, ks]))


def _gla_out_kernel(x_ref, m_ref, of_ref, ob_ref, go_ref, gn_ref, wout_ref, *route_refs):
    route_in, o_ref, route_out = route_refs[:3], route_refs[3], route_refs[4:]
    x = x_ref[...]
    m = m_ref[0]
    o = of_ref[...] + ob_ref[...]
    hv = gn_ref.shape[1]
    parts = [_rms(o[:, h * hv:(h + 1) * hv], gn_ref[...]) for h in range(GLA_HEADS)]
    o = jnp.concatenate(parts, axis=1)
    gate = go_ref[...]
    y = _dot((o * (gate * jax.nn.sigmoid(gate))).astype(BF16), wout_ref[...])
    x_new = x + m[2:3] * y
    o_ref[...] = x_new
    _route_tail(x_new, m, *route_in, *route_out)


def _gla_mixer(xs, mods, g, w_in, w_a2, b_a, g_o, w_out, geo, dims, route_w):
    b, seq, lctx = dims
    nt, d = xs.shape
    n_tiles = nt // TM
    kdim = w_a2.shape[2]
    vdim = g_o.shape[0] * GLA_HEADS
    hk = kdim // GLA_HEADS
    nf = hk // 4
    wq = w_in[:, :kdim].astype(BF16)
    wk = w_in[:, kdim:2 * kdim].astype(BF16)
    wv = w_in[:, 2 * kdim:2 * kdim + vdim].astype(BF16)
    wg = w_in[:, 2 * kdim + vdim:2 * kdim + 2 * vdim].astype(BF16)
    wr = jnp.pad(w_in[:, 2 * kdim + 2 * vdim:], ((0, 0), (0, 128 - 2 * GLA_RANK))).astype(BF16)
    wa = jnp.zeros((128, 2 * kdim), F32)
    wa = wa.at[:GLA_RANK, :kdim].set(w_a2[0]).at[GLA_RANK:2 * GLA_RANK, kdim:].set(w_a2[1]).astype(BF16)
    ba = b_a.reshape(1, 2 * kdim)

    tpos = np.arange(seq)
    freqs = np.power(np.float32(ROPE_BASE), -np.arange(nf, dtype=np.float32) / np.float32(nf)).astype(np.float32)
    ar = (tpos // GRID_W).astype(np.float32)[:, None] * freqs
    ac = (tpos % GRID_W).astype(np.float32)[:, None] * freqs
    cos = np.concatenate([np.cos(ar), np.cos(ar), np.cos(ac), np.cos(ac)], axis=1)
    sin = np.concatenate([-np.sin(ar), np.sin(ar), -np.sin(ac), np.sin(ac)], axis=1)
    cos = jnp.asarray(np.concatenate([np.ones((TM, hk), np.float32), cos], axis=0).astype(np.float32))
    sin = jnp.asarray(np.concatenate([np.zeros((TM, hk), np.float32), sin], axis=0).astype(np.float32))

    n_ctx_tiles, tiles_per_batch, _ = geo
    grp = functools.partial(_tile_group, geo=geo)

    def rope_blk(t):
        return (jnp.where(t < n_ctx_tiles, 0, 1 + (t - n_ctx_tiles) % tiles_per_batch), 0)

    const2 = lambda t: (0, 0)
    row = lambda t: (t, 0)
    q, k, v, go, la = pl.pallas_call(
        _gla_proj_kernel,
        grid=(n_tiles,),
        in_specs=[pl.BlockSpec((TM, d), row),
                  pl.BlockSpec((1, 6, d), lambda t: (grp(t), 0, 0)),
                  pl.BlockSpec((1, d), const2),
                  pl.BlockSpec((d, kdim), const2), pl.BlockSpec((d, kdim), const2),
                  pl.BlockSpec((d, vdim), const2), pl.BlockSpec((d, vdim), const2),
                  pl.BlockSpec((d, 128), const2), pl.BlockSpec((128, 2 * kdim), const2),
                  pl.BlockSpec((1, 2 * kdim), const2),
                  pl.BlockSpec((TM, hk), rope_blk), pl.BlockSpec((TM, hk), rope_blk)],
        out_specs=[pl.BlockSpec((TM, kdim), row), pl.BlockSpec((TM, kdim), row),
                   pl.BlockSpec((TM, vdim), row), pl.BlockSpec((TM, vdim), row),
                   pl.BlockSpec((TM, 2 * kdim), row)],
        out_shape=[jax.ShapeDtypeStruct((nt, kdim), F32), jax.ShapeDtypeStruct((nt, kdim), F32),
                   jax.ShapeDtypeStruct((nt, vdim), BF16), jax.ShapeDtypeStruct((nt, vdim), F32),
                   jax.ShapeDtypeStruct((nt, 2 * kdim), F32)],
        compiler_params=_cparams(("parallel",)),
        name="gla_proj",
    )(xs, mods, g.reshape(1, d), wq, wk, wv, wg, wr, wa, ba, cos, sin)

    tb = GLA_TB
    ctx_steps = lctx // tb
    lat_steps = seq // tb
    steps = ctx_steps + lat_steps

    def blk(bi, s, rev):
        if rev:
            cs, ls = ctx_steps - 1 - s, lat_steps - 1 - (s - ctx_steps)
        else:
            cs, ls = s, s - ctx_steps
        return jnp.where(s < ctx_steps, bi * ctx_steps + cs, b * ctx_steps + bi * lat_steps + ls)

    def dir_specs(rev):
        row_blk = lambda bi, s: (blk(bi, s, rev), 0)
        return [pl.BlockSpec((tb, kdim), row_blk), pl.BlockSpec((tb, kdim), row_blk),
                pl.BlockSpec((tb, vdim), row_blk),
                pl.BlockSpec((tb, kdim), lambda bi, s: (blk(bi, s, rev), 1 if rev else 0))]

    state = pltpu.VMEM((GLA_HEADS, vdim // GLA_HEADS, hk), F32)
    o_f, o_b = pl.pallas_call(
        _gla_scan_kernel,
        grid=(b, steps),
        in_specs=dir_specs(False) + dir_specs(True),
        out_specs=[pl.BlockSpec((tb, vdim), lambda bi, s: (blk(bi, s, False), 0)),
                   pl.BlockSpec((tb, vdim), lambda bi, s: (blk(bi, s, True), 0))],
        out_shape=[jax.ShapeDtypeStruct((nt, vdim), F32)] * 2,
        scratch_shapes=[state, state],
        compiler_params=_cparams(("parallel", "arbitrary")),
        name="gla_scan",
    )(q, k, v, la, q, k, v, la)

    r_in, r_out, r_shape, r_scratch = _route_specs(nt, d, route_w[1].shape[1])
    out = pl.pallas_call(
        _gla_out_kernel,
        grid=(n_tiles,),
        in_specs=[pl.BlockSpec((TM, d), row),
                  pl.BlockSpec((1, 6, d), lambda t: (grp(t), 0, 0)),
                  pl.BlockSpec((TM, vdim), row), pl.BlockSpec((TM, vdim), row),
                  pl.BlockSpec((TM, vdim), row),
                  pl.BlockSpec((1, vdim // GLA_HEADS), const2),
                  pl.BlockSpec((vdim, d), const2)] + r_in,
        out_specs=[pl.BlockSpec((TM, d), row)] + r_out,
        out_shape=[jax.ShapeDtypeStruct((nt, d), F32)] + r_shape,
        scratch_shapes=r_scratch,
        compiler_params=_cparams(("arbitrary",)),
        name="gla_out",
    )(xs, mods, o_f, o_b, go, g_o.reshape(1, -1), w_out.astype(BF16), *_route_args(route_w, d))
    return out[0], out[1:]


def _na_proj_kernel(x_ref, m_ref, g_ref, w_ref, q_ref, k_ref, v_ref):
    x = x_ref[...]
    m = m_ref[0]
    d = x.shape[1]
    h = _modulate(x, g_ref[...], m[0:1], m[1:2]).astype(BF16)
    qkv = _dot(h, w_ref[...])
    hd = d // NA_HEADS
    for p in range(q_ref.shape[0]):
        cs = slice(p * 128, (p + 1) * 128)
        q_ref[p] = (qkv[:, cs] * np.float32(hd ** -0.5)).astype(BF16)
        k_ref[p] = qkv[:, d + p * 128:d + (p + 1) * 128].astype(BF16)
        v_ref[p] = qkv[:, 2 * d + p * 128:2 * d + (p + 1) * 128].astype(BF16)


def _na_attn_kernel(tbl_ref, q_ref, *refs, n_tiles, nwin):
    k_refs, v_refs = refs[:nwin], refs[nwin:2 * nwin]
    kc_ref, vc_ref, bias_ref, o_ref, s_ref, p_ref = refs[2 * nwin:]
    t = pl.program_id(2)
    typ = jnp.where(t == 0, 0, jnp.where(t == n_tiles - 1, 2, 1))
    npair = NA_WROWS // 2
    nlat = NA_WROWS * GRID_W
    lane = lax.broadcasted_iota(I32, q_ref.shape[1:], 1)
    npp, tq = q_ref.shape[0], q_ref.shape[1]
    for pp in range(npp):
        q = q_ref[pp]
        k_all = jnp.concatenate([r[pp] for r in k_refs] + [kc_ref[pp]], axis=0)
        zero = jnp.zeros_like(q)
        q2 = jnp.concatenate([jnp.where(lane < 64, q, zero), jnp.where(lane >= 64, q, zero)], axis=0)
        s_ref[pp] = _dot_nt(q2, k_all)
    for pp in range(npp):
        for hh in range(2):
            for dr in range(NA_QROWS):
                rs = slice(hh * tq + dr * GRID_W, hh * tq + (dr + 1) * GRID_W)
                lat = [s_ref[pp, rs, j * 128:(j + 1) * 128]
                       + bias_ref[2 * pp + hh, tbl_ref[typ * (NA_QROWS * npair) + dr * npair + j]]
                       for j in range(npair)]
                sb = jnp.concatenate(lat + [s_ref[pp, rs, nlat:]], axis=1)
                e = jnp.exp(sb - jnp.max(sb, axis=1, keepdims=True))
                p_ref[pp, rs, :] = (e * (1.0 / jnp.sum(e, axis=1, keepdims=True))).astype(BF16)
    outs = []
    for pp in range(npp):
        v_all = jnp.concatenate([r[pp] for r in v_refs] + [vc_ref[pp]], axis=0)
        o2 = _dot(p_ref[pp], v_all)
        outs.append(jnp.where(lane < 64, o2[:tq], o2[tq:]))
    o_ref[...] = jnp.concatenate(outs, axis=1).astype(BF16)


def _na_out_kernel(x_ref, m_ref, a_ref, w_ref, *route_refs):
    route_in, o_ref, route_out = route_refs[:3], route_refs[3], route_refs[4:]
    m = m_ref[0]
    x_new = x_ref[...] + m[2:3] * _dot(a_ref[...], w_ref[...])
    o_ref[...] = x_new
    _route_tail(x_new, m, *route_in, *route_out)


def _na_tables(rows):
    npair = NA_WROWS // 2
    tbl = np.zeros((3, NA_QROWS, npair), np.int32)
    for typ, r_base in enumerate((0, NA_QROWS, rows - NA_QROWS)):
        w0 = int(np.clip(r_base - NA_KH // 2, 0, rows - NA_WROWS))
        for dr in range(NA_QROWS):
            r = r_base + dr
            r0 = int(np.clip(r - NA_KH // 2, 0, rows - NA_KH))
            for j in range(npair):
                kr = (w0 + 2 * j, w0 + 2 * j + 1)
                ok = [r0 <= x < r0 + NA_KH for x in kr]
                ri = [x - r + NA_KH - 1 for x in kr]
                if ok[0] and ok[1]:
                    e = ri[0]
                elif ok[0]:
                    e = 16 + ri[0]
                elif ok[1]:
                    e = 32 + ri[1]
                else:
                    e = 63
                tbl[typ, dr, j] = e
    return tbl.reshape(-1)


def _na_bias_kernel(rpb_ref, o_ref, *, nr, nw):
    h = pl.program_id(0)
    c = lax.broadcasted_iota(I32, (GRID_W, GRID_W), 0)
    k = lax.broadcasted_iota(I32, (GRID_W, GRID_W), 1)
    cstart = jnp.clip(c - NA_KW // 2, 0, GRID_W - NA_KW)
    ok = jnp.logical_and(k >= cstart, k < cstart + NA_KW)
    rel = k - c + NA_KW - 1
    neg = jnp.full((GRID_W, GRID_W), NEG_INF, F32)
    cbs = []
    for r in range(nr):
        acc = neg
        for w in range(nw):
            acc = jnp.where(rel == w, rpb_ref[(h * nr + r) * nw + w], acc)
        cbs.append(jnp.where(ok, acc, neg))
    cbs += [neg] * (17 - nr)
    for r in range(16):
        o_ref[0, r] = jnp.concatenate([cbs[r], cbs[r + 1]], axis=1)
        o_ref[0, 16 + r] = jnp.concatenate([cbs[r], neg], axis=1)
        o_ref[0, 32 + r] = jnp.concatenate([neg, cbs[r]], axis=1)
        o_ref[0, 48 + r] = jnp.concatenate([neg, neg], axis=1)


def _na_bias_table(rpb):
    nh, nr, nw = rpb.shape
    return pl.pallas_call(
        functools.partial(_na_bias_kernel, nr=nr, nw=nw),
        grid=(nh,),
        in_specs=[pl.BlockSpec(memory_space=pltpu.SMEM)],
        out_specs=pl.BlockSpec((1, 64, GRID_W, 2 * GRID_W), lambda h: (h, 0, 0, 0)),
        out_shape=jax.ShapeDtypeStruct((nh, 64, GRID_W, 2 * GRID_W), F32),
        compiler_params=_cparams(("parallel",)),
        name="na_bias_table",
    )(rpb.reshape(-1).astype(F32))


def _na_mixer(xs, mods, g, w_qkv, rpb, w_out, geo, dims, route_w):
    b, seq, lctx = dims
    nt, d = xs.shape
    n_tiles_tok = nt // TM
    grp = functools.partial(_tile_group, geo=geo)
    npairs = d // 128
    const2 = lambda t: (0, 0)
    q, k, v = pl.pallas_call(
        _na_proj_kernel,
        grid=(n_tiles_tok,),
        in_specs=[pl.BlockSpec((TM, d), lambda t: (t, 0)),
                  pl.BlockSpec((1, 6, d), lambda t: (grp(t), 0, 0)),
                  pl.BlockSpec((1, d), const2),
                  pl.BlockSpec((d, 3 * d), const2)],
        out_specs=[pl.BlockSpec((npairs, TM, 128), lambda t: (0, t, 0))] * 3,
        out_shape=[jax.ShapeDtypeStruct((npairs, nt, 128), BF16)] * 3,
        compiler_params=_cparams(("parallel",)),
        name="na_proj",
    )(xs, mods, g.reshape(1, d), w_qkv.astype(BF16))

    rows = seq // GRID_W
    tq = NA_QROWS * GRID_W
    n_tiles = rows // NA_QROWS
    wb = 256
    nwin = NA_WROWS * GRID_W // wb
    lat0 = b * lctx
    tbl = jnp.asarray(_na_tables(rows))
    bias = _na_bias_table(rpb)

    pp = NA_PAIRS_PER_STEP
    assert npairs % pp == 0 and (NA_QROWS * GRID_W) % wb == 0 and (NA_KH // 2 * GRID_W) % wb == 0

    def win(i):
        def f(p, bi, t, tbl_ref):
            w = jnp.clip(t * (NA_QROWS * GRID_W // wb) - NA_KH // 2 * GRID_W // wb, 0, seq // wb - nwin)
            return (p, (lat0 + bi * seq) // wb + w + i, 0)
        return f

    kv_specs = [pl.BlockSpec((pp, wb, 128), win(i)) for i in range(nwin)]
    attn = pl.pallas_call(
        functools.partial(_na_attn_kernel, n_tiles=n_tiles, nwin=nwin),
        grid_spec=pltpu.PrefetchScalarGridSpec(
            num_scalar_prefetch=1,
            grid=(npairs // pp, b, n_tiles),
            in_specs=[pl.BlockSpec((pp, tq, 128), lambda p, bi, t, tr: (p, (lat0 + bi * seq) // tq + t, 0))]
                     + kv_specs + kv_specs
                     + [pl.BlockSpec((pp, lctx, 128), lambda p, bi, t, tr: (p, bi, 0)),
                        pl.BlockSpec((pp, lctx, 128), lambda p, bi, t, tr: (p, bi, 0)),
                        pl.BlockSpec((2 * pp, 64, GRID_W, 2 * GRID_W), lambda p, bi, t, tr: (p, 0, 0, 0),
                                     pipeline_mode=pl.Buffered(1))],
            out_specs=pl.BlockSpec((tq, 128 * pp), lambda p, bi, t, tr: (bi * n_tiles + t, p)),
            scratch_shapes=[pltpu.VMEM((pp, 2 * tq, NA_WROWS * GRID_W + lctx), F32),
                            pltpu.VMEM((pp, 2 * tq, NA_WROWS * GRID_W + lctx), BF16)]),
        out_shape=jax.ShapeDtypeStruct((b * seq, d), BF16),
        compiler_params=_cparams(("parallel", "parallel", "arbitrary")),
        name="na_attn",
    )(tbl, q, *([k] * nwin), *([v] * nwin), k, v, bias)

    n_lat_tiles = b * seq // TM
    tile0 = lat0 // TM
    grp_l = functools.partial(_tile_group, geo=geo, tile0=tile0)
    r_in, r_out, r_shape, r_scratch = _route_specs(b * seq, d, route_w[1].shape[1])
    out = pl.pallas_call(
        _na_out_kernel,
        grid=(n_lat_tiles,),
        in_specs=[pl.BlockSpec((TM, d), lambda t: (t + tile0, 0)),
                  pl.BlockSpec((1, 6, d), lambda t: (grp_l(t), 0, 0)),
                  pl.BlockSpec((TM, d), lambda t: (t, 0)),
                  pl.BlockSpec((d, d), const2)] + r_in,
        out_specs=[pl.BlockSpec((TM, d), lambda t: (t, 0))] + r_out,
        out_shape=[jax.ShapeDtypeStruct((b * seq, d), F32)] + r_shape,
        scratch_shapes=r_scratch,
        compiler_params=_cparams(("arbitrary",)),
        name="na_out",
    )(xs, mods, attn, w_out.astype(BF16), *_route_args(route_w, d))
    return out[0], out[1:]


SC_CORES = 2
SC_SUBCORES = 16
SC_WORKERS = SC_CORES * SC_SUBCORES
SLOT_T_MAX = 2048


def _sc_mesh():
    return plsc.VectorSubcoreMesh(core_axis_name="c", subcore_axis_name="s")


def _sc_chunk(per_worker, max_chunk):
    return max(c for c in range(8, max_chunk + 1, 8) if per_worker % (2 * c) == 0)


def _sc_gather_rows(table, idx):
    dd = table.shape[1]
    bsz = idx.shape[0]
    per_w = bsz // SC_WORKERS
    assert per_w * SC_WORKERS == bsz
    chunk = _sc_chunk(per_w, 64)
    n_chunks = per_w // chunk

    def body(table_hbm, idx_hbm, out_hbm, idx_v, rows0, rows1, g0, g1, w0, w1):
        wid = lax.axis_index("s") * SC_CORES + lax.axis_index("c")
        pltpu.sync_copy(idx_hbm.at[wid], idx_v)
        base = wid * per_w

        def out_rows(j):
            return out_hbm.at[pl.ds(pl.multiple_of(base + j * chunk, 8), chunk)]

        def step(i, carry):
            j0, j1 = 2 * i, 2 * i + 1
            ga = pltpu.async_copy(table_hbm.at[idx_v.at[j0]], rows0, g0)
            gb = pltpu.async_copy(table_hbm.at[idx_v.at[j1]], rows1, g1)
            ga.wait()
            wa = pltpu.async_copy(rows0, out_rows(j0), w0)
            gb.wait()
            wb = pltpu.async_copy(rows1, out_rows(j1), w1)
            wa.wait()
            wb.wait()
            return carry

        lax.fori_loop(0, n_chunks // 2, step, 0)

    return pl.kernel(
        body, out_type=jax.ShapeDtypeStruct((bsz, dd), table.dtype), mesh=_sc_mesh(),
        scratch_types=[pltpu.VMEM((n_chunks, chunk), I32),
                       pltpu.VMEM((chunk, dd), table.dtype), pltpu.VMEM((chunk, dd), table.dtype),
                       pltpu.SemaphoreType.DMA, pltpu.SemaphoreType.DMA,
                       pltpu.SemaphoreType.DMA, pltpu.SemaphoreType.DMA],
        name="sc_gather_rows",
    )(table, idx.reshape(SC_WORKERS, n_chunks, chunk))


def _sc_scatter_rows(rows, dest, n_out):
    n, dd = rows.shape
    kk = dest.shape[0]
    per_w = n // SC_WORKERS
    assert per_w * SC_WORKERS == n
    chunk = _sc_chunk(per_w, 64)
    n_chunks = per_w // chunk
    dest_w = dest.reshape(kk, SC_WORKERS, n_chunks, chunk).transpose(1, 2, 0, 3)
    dest_w = dest_w.reshape(SC_WORKERS, n_chunks * kk, chunk)

    def body(rows_hbm, dest_hbm, out_hbm, idx_v, rows0, rows1, r0, r1, s0, s1):
        wid = lax.axis_index("s") * SC_CORES + lax.axis_index("c")
        pltpu.sync_copy(dest_hbm.at[wid], idx_v)
        base = wid * per_w

        def in_rows(j):
            return rows_hbm.at[pl.ds(pl.multiple_of(base + j * chunk, 8), chunk)]

        def step(i, carry):
            j0, j1 = 2 * i, 2 * i + 1
            ra = pltpu.async_copy(in_rows(j0), rows0, r0)
            rb = pltpu.async_copy(in_rows(j1), rows1, r1)
            ra.wait()
            sa = [pltpu.async_copy(rows0, out_hbm.at[idx_v.at[j0 * kk + k]], s0) for k in range(kk)]
            rb.wait()
            sb = [pltpu.async_copy(rows1, out_hbm.at[idx_v.at[j1 * kk + k]], s1) for k in range(kk)]
            for cp in sa + sb:
                cp.wait()
            return carry

        lax.fori_loop(0, n_chunks // 2, step, 0)

    return pl.kernel(
        body, out_type=jax.ShapeDtypeStruct((n_out, dd), rows.dtype), mesh=_sc_mesh(),
        scratch_types=[pltpu.VMEM((n_chunks * kk, chunk), I32),
                       pltpu.VMEM((chunk, dd), rows.dtype), pltpu.VMEM((chunk, dd), rows.dtype),
                       pltpu.SemaphoreType.DMA, pltpu.SemaphoreType.DMA,
                       pltpu.SemaphoreType.DMA, pltpu.SemaphoreType.DMA],
        name="sc_scatter_rows",
    )(rows, dest_w)


def _prefix_sum_rows(col):
    nr = col.shape[0]
    acc = jnp.broadcast_to(col, (nr, 128))
    row = lax.broadcasted_iota(I32, (nr, 128), 0)
    s = 1
    while s < nr:
        acc = acc + jnp.where(row >= s, pltpu.roll(acc, s, 0), 0.0)
        s *= 2
    return acc[:, 0:1]


def _slot_kernel(idx_ref, rank_ref, cnt_ref, dest_ref, btab_ref, *, bm):
    cnt = cnt_ref[...]
    ne = cnt.shape[0]
    padded = jnp.floor((cnt + (bm - 1.0)) * (1.0 / bm)) * bm
    pad_end = _prefix_sum_rows(padded)
    pad_start = pad_end - padded
    idx = idx_ref[...]
    e_iota = lax.broadcasted_iota(I32, (ne, idx.shape[1]), 0)
    starts = [jnp.sum(jnp.where(e_iota == idx[k:k + 1], pad_start, 0.0), axis=0, keepdims=True)
              for k in range(idx.shape[0])]
    dest_ref[...] = jnp.concatenate(starts, axis=0).astype(I32) + rank_ref[...]

    nbp = btab_ref.shape[1]
    bstart = lax.broadcasted_iota(I32, (ne, nbp), 1).astype(F32) * bm
    be = jnp.minimum(jnp.sum((pad_end <= bstart).astype(F32), axis=0, keepdims=True), ne - 1.0)
    mine = lax.broadcasted_iota(I32, (ne, nbp), 0).astype(F32) == be
    pick = lambda col: jnp.sum(jnp.where(mine, col, 0.0), axis=0, keepdims=True)
    live = jnp.clip(pick(pad_start + cnt) - bstart[0:1], 0.0, bm)
    r = lax.broadcasted_iota(I32, (ne, ne), 0)
    c = lax.broadcasted_iota(I32, (ne, ne), 1)
    ends_on_lanes = _dot_f32(jnp.ones((ne, ne), BF16), jnp.where(r == c, pad_end, 0.0))
    nxt_e = jnp.sum((ends_on_lanes <= pad_end).astype(F32), axis=1, keepdims=True)
    nxt_e = jnp.where(nxt_e < ne, nxt_e, -1.0)
    ordinal = _prefix_sum_rows((cnt > 0).astype(F32)) - 1.0
    slot_e = ordinal - 2.0 * jnp.floor(ordinal * 0.5)
    rowi = lax.broadcasted_iota(I32, btab_ref.shape, 0)
    tab = jnp.where(rowi == 0, be, jnp.where(rowi == 1, live, jnp.where(rowi == 2, pick(nxt_e), pick(slot_e))))
    btab_ref[...] = tab.astype(I32)


def _route_specs(n, d, ne):
    const2 = lambda t: (0, 0)
    in_specs = [pl.BlockSpec((1, d), const2), pl.BlockSpec((ne, d), const2), pl.BlockSpec((ne, 1), const2)]
    out_specs = [pl.BlockSpec((TM, d // 2), lambda t: (t, 0)),
                 pl.BlockSpec((TOP_K, TM), lambda t: (0, t)),
                 pl.BlockSpec((TOP_K, TM), lambda t: (0, t)),
                 pl.BlockSpec((TOP_K, TM), lambda t: (0, t)),
                 pl.BlockSpec((ne, 1), const2)]
    out_shape = [jax.ShapeDtypeStruct((n, d // 2), jnp.uint32),
                 jax.ShapeDtypeStruct((TOP_K, n), I32),
                 jax.ShapeDtypeStruct((TOP_K, n), F32),
                 jax.ShapeDtypeStruct((TOP_K, n), I32),
                 jax.ShapeDtypeStruct((ne, 1), F32)]
    return in_specs, out_specs, out_shape, [pltpu.VMEM((ne, 1), F32)]


def _route_args(route_w, d):
    g_ffn, w_router, b_router = route_w
    ne = w_router.shape[1]
    return g_ffn.reshape(1, d), w_router.T.astype(BF16), b_router.reshape(ne, 1)


def _route_tail(x, m, g_ref, wr_ref, br_ref, h_ref, idx_ref, gate_ref, rank_ref, cnt_ref, run_ref):
    t = pl.program_id(0)

    @pl.when(t == 0)
    def _():
        run_ref[...] = jnp.zeros_like(run_ref)

    h = _modulate(x, g_ref[...], m[3:4], m[4:5]).astype(BF16)
    h_ref[...] = _pack_bf16_pairs(h)
    logits = _dot_nt(wr_ref[...], h) + br_ref[...]
    ne, tm = logits.shape
    e_iota = lax.broadcasted_iota(I32, (ne, tm), 0)
    vals, idxs = [], []
    l = logits
    for _ in range(TOP_K):
        mk = jnp.max(l, axis=0, keepdims=True)
        ik = jnp.min(jnp.where(l == mk, e_iota, ne), axis=0, keepdims=True)
        vals.append(mk)
        idxs.append(ik)
        l = jnp.where(e_iota == ik, -jnp.inf, l)
    top_val = jnp.concatenate(vals, axis=0)
    ex = jnp.exp(top_val - vals[0])
    gate_ref[...] = ex / jnp.sum(ex, axis=0, keepdims=True)
    idx_ref[...] = jnp.concatenate(idxs, axis=0)

    hits = [e_iota == ik for ik in idxs]
    cnt = hits[0].astype(F32)
    for hk in hits[1:]:
        cnt = cnt + hk.astype(F32)
    si = lax.broadcasted_iota(I32, (tm, tm), 0)
    ti = lax.broadcasted_iota(I32, (tm, tm), 1)
    before = (si < ti).astype(BF16)
    total = _dot(cnt.astype(BF16), before) + run_ref[...]
    ranks = [jnp.sum(jnp.where(hk, total, 0.0), axis=0, keepdims=True) for hk in hits]
    rank_ref[...] = jnp.concatenate(ranks, axis=0).astype(I32)
    run_ref[...] = run_ref[...] + jnp.sum(cnt, axis=1, keepdims=True)
    cnt_ref[...] = run_ref[...]


def _pack_bf16_pairs(v):
    bits = lax.bitcast_convert_type(v.astype(BF16).astype(F32), jnp.uint32)
    half = bits.shape[1] // 2
    return (bits[:, half:] & jnp.uint32(0xFFFF0000)) | (bits[:, :half] >> 16)


def _unpack_bf16_pairs(w):
    return (lax.bitcast_convert_type(w << 16, F32),
            lax.bitcast_convert_type(w & jnp.uint32(0xFFFF0000), F32))


def _expert_kernel(tab_ref, x_ref, wgu_hbm, bgu_ref, wd_hbm, bd_ref, o_ref,
                   wgu_in, wd_in, wgu_bf, wd_bf, act_ref, sem_gu, sem_d, *, layer):
    blk = pl.program_id(0)
    ch = wgu_bf.shape[2]
    f = act_ref.shape[1]
    nc = f // ch
    expert, live, nxt, wslot = (tab_ref[i, blk] for i in range(4))
    used = live > 0
    fresh = jnp.logical_or(blk == 0, expert != tab_ref[0, jnp.maximum(blk - 1, 0)])

    def fetch(e, slot):
        return (pltpu.make_async_copy(wgu_hbm.at[layer, e], wgu_in.at[slot], sem_gu.at[slot]),
                pltpu.make_async_copy(wd_hbm.at[layer, e], wd_in.at[slot], sem_d.at[slot]))

    @pl.when(jnp.logical_and(used, fresh))
    def _():
        slot = wslot
        mine = fetch(expert, slot)

        @pl.when(blk == 0)
        def _():
            for cp in mine:
                cp.start()

        @pl.when(nxt >= 0)
        def _():
            for cp in fetch(nxt, 1 - slot):
                cp.start()

        for cp in mine:
            cp.wait()

    bm = x_ref.shape[0]
    fresh_full = jnp.logical_and(fresh, live > bm - MOE_SUB)

    def cast_gu(j):
        wgu_bf[j] = wgu_in[wslot, :, j * ch:(j + 1) * ch].astype(BF16)

    def cast_d(j):
        wd_bf[j] = wd_in[wslot, :, j * ch:(j + 1) * ch].astype(BF16)

    @pl.when(jnp.logical_and(jnp.logical_and(used, fresh), jnp.logical_not(fresh_full)))
    def _():
        for j in range(wgu_bf.shape[0]):
            cast_gu(j)
        for j in range(wd_bf.shape[0]):
            cast_d(j)

    def run(nr, cast):
        w = x_ref[:nr, :]
        row = lax.broadcasted_iota(I32, w.shape, 0)
        w = jnp.where(row < live, w, jnp.zeros_like(w))
        lo, hi = _unpack_bf16_pairs(w)
        x = jnp.concatenate([lo.astype(BF16), hi.astype(BF16)], axis=1)
        for c in range(nc):
            c0 = slice(c * ch, (c + 1) * ch)
            c1 = slice(f + c * ch, f + (c + 1) * ch)
            if cast:
                cast_gu(c)
                cast_gu(nc + c)
            glu = _dot(x, wgu_bf[c]) + bgu_ref[0, 0, :, c0]
            lin = _dot(x, wgu_bf[nc + c]) + bgu_ref[0, 0, :, c1]
            glu = jnp.minimum(glu, SWIGLU_LIMIT)
            lin = jnp.clip(lin, -SWIGLU_LIMIT, SWIGLU_LIMIT)
            act_ref[:nr, c0] = (glu * jax.nn.sigmoid(SWIGLU_ALPHA * glu) * (lin + 1.0)).astype(BF16)
        a = act_ref[:nr, :]
        ys = []
        for n in range(wd_bf.shape[0]):
            if cast:
                cast_d(n)
            ys.append(_dot(a, wd_bf[n]))
        o_ref[:nr, :] = _pack_bf16_pairs(jnp.concatenate(ys, axis=1) + bd_ref[0, 0])

    for nr in range(MOE_SUB, bm + 1, MOE_SUB):
        @pl.when(jnp.logical_and(jnp.logical_and(live > nr - MOE_SUB, live <= nr), jnp.logical_not(fresh_full)))
        def _(nr=nr):
            run(nr, False)
            if nr < bm:
                o_ref[nr:, :] = jnp.zeros((bm - nr, o_ref.shape[1]), o_ref.dtype)

    @pl.when(fresh_full)
    def _():
        run(bm, True)

    @pl.when(jnp.logical_not(used))
    def _():
        o_ref[...] = jnp.zeros_like(o_ref)


def _expert_call(tab, buf, w_gu, b_gu, w_down, b_down, layer, bm):
    n_slots, dw = buf.shape
    _, ne, d, f2 = w_gu.shape
    f = f2 // 2
    n_blocks = n_slots // bm
    lw = lambda i, tb: (layer, tb[0, i], 0, 0)
    row = lambda i, tb: (i, 0)
    cw = 256
    return pl.pallas_call(
        functools.partial(_expert_kernel, layer=layer),
        grid_spec=pltpu.PrefetchScalarGridSpec(
            num_scalar_prefetch=1,
            grid=(n_blocks,),
            in_specs=[pl.BlockSpec((bm, dw), row),
                      pl.BlockSpec(memory_space=pl.ANY),
                      pl.BlockSpec((1, 1, 1, f2), lw),
                      pl.BlockSpec(memory_space=pl.ANY),
                      pl.BlockSpec((1, 1, 1, d), lw)],
            out_specs=pl.BlockSpec((bm, dw), row),
            scratch_shapes=[pltpu.VMEM((2, d, f2), F32), pltpu.VMEM((2, f, d), F32),
                            pltpu.VMEM((f2 // cw, d, cw), BF16), pltpu.VMEM((d // cw, f, cw), BF16),
                            pltpu.VMEM((bm, f), BF16),
                            pltpu.SemaphoreType.DMA((2,)), pltpu.SemaphoreType.DMA((2,))]),
        out_shape=jax.ShapeDtypeStruct((n_slots, dw), jnp.uint32),
        compiler_params=_cparams(("arbitrary",)),
        name="moe_experts",
    )(tab, buf, w_gu, b_gu.reshape(b_gu.shape[0], ne, 1, f2), w_down, b_down.reshape(b_down.shape[0], ne, 1, d))


def _combine_kernel(x_ref, m_ref, y_ref, gate_ref, gout_ref, o_ref, *, final):
    m = m_ref[0]
    gate = gate_ref[...]
    half = y_ref.shape[2]
    f_lo, f_hi = None, None
    for k in range(TOP_K):
        lo, hi = _unpack_bf16_pairs(y_ref[k])
        gk = gate[:, k:k + 1]
        f_lo = gk * lo if f_lo is None else f_lo + gk * lo
        f_hi = gk * hi if f_hi is None else f_hi + gk * hi
    x_lo = x_ref[:, :half] + m[5:6, :half] * f_lo
    x_hi = x_ref[:, half:] + m[5:6, half:] * f_hi
    if final:
        ms = (jnp.sum(x_lo * x_lo, axis=-1, keepdims=True) + jnp.sum(x_hi * x_hi, axis=-1, keepdims=True))
        r = lax.rsqrt(ms * (0.5 / half) + NORM_EPS)
        x_lo = x_lo * r * gout_ref[:, :half]
        x_hi = x_hi * r * gout_ref[:, half:]
    o_ref[:, :half] = x_lo
    o_ref[:, half:] = x_hi


def _moe_layer(xs, routed, mods, layer, w_gu, b_gu, w_down, b_down, geo, g_out, final):
    n, d = xs.shape
    h, idx, gate, rank, cnt = routed
    assert h.shape[0] == n
    ne = cnt.shape[0]
    n_tiles = n // TM
    grp = functools.partial(_tile_group, geo=geo)
    const2 = lambda t: (0, 0)

    bm = MOE_BM
    n_rows = n * TOP_K
    n_blocks = -(-n_rows // bm) + ne
    n_slots = n_blocks * bm
    nbp = -(-n_blocks // 128) * 128
    slot_t = max(w for w in range(128, SLOT_T_MAX + 1, 128) if n % w == 0)
    dest, btab = pl.pallas_call(
        functools.partial(_slot_kernel, bm=bm),
        grid=(n // slot_t,),
        in_specs=[pl.BlockSpec((TOP_K, slot_t), lambda t: (0, t)),
                  pl.BlockSpec((TOP_K, slot_t), lambda t: (0, t)),
                  pl.BlockSpec((ne, 1), const2)],
        out_specs=[pl.BlockSpec((TOP_K, slot_t), lambda t: (0, t)),
                   pl.BlockSpec((8, nbp), const2)],
        out_shape=[jax.ShapeDtypeStruct((TOP_K, n), I32),
                   jax.ShapeDtypeStruct((8, nbp), I32)],
        compiler_params=_cparams(("arbitrary",)),
        name="moe_slots",
    )(idx, rank, cnt)

    buf = _sc_scatter_rows(h, dest, n_slots)
    y = _expert_call(btab, buf, w_gu, b_gu, w_down, b_down, layer, bm)
    dy = d // 2
    yk = _sc_gather_rows(y, dest.reshape(-1)).reshape(TOP_K, n, dy)
    return pl.pallas_call(
        functools.partial(_combine_kernel, final=final),
        grid=(n_tiles,),
        in_specs=[pl.BlockSpec((TM, d), lambda t: (t, 0)),
                  pl.BlockSpec((1, 6, d), lambda t: (grp(t), 0, 0)),
                  pl.BlockSpec((TOP_K, TM, dy), lambda t: (0, t, 0)),
                  pl.BlockSpec((TM, TOP_K), lambda t: (t, 0)),
                  pl.BlockSpec((1, d), const2)],
        out_specs=pl.BlockSpec((TM, d), lambda t: (t, 0)),
        out_shape=jax.ShapeDtypeStruct((n, d), F32),
        compiler_params=_cparams(("parallel",)),
        name="moe_combine",
    )(xs, mods, yk, gate.T, g_out.reshape(1, d))


def kernel(x, c, ctx, c_ctx, ada_w, ada_b, norm_mix, norm_ffn, norm_out, a_w_in, a_g_v, a_w_s, a_b_s, a_w_out, b_w_in, b_w_a2, b_b_a, b_g_o, b_w_out, c_w_qkv, c_rpb, c_w_out, moe_w_router, moe_b_router, moe_w_gu, moe_b_gu, moe_w_down, moe_b_down):
    b, seq, d = x.shape
    lctx = ctx.shape[1]
    depth = ada_w.shape[0]
    assert (b * lctx) % TM == 0 and seq % TM == 0 and lctx % GLA_TB == 0 and seq % GLA_TB == 0
    assert seq % (NA_QROWS * GRID_W) == 0 and (b * lctx) % (NA_QROWS * GRID_W) == 0 and lctx % 256 == 0
    assert seq // GRID_W >= NA_WROWS + NA_QROWS and b + 1 <= 8
    geo = (b * lctx // TM, seq // TM, b)
    dims = (b, seq, lctx)

    cond = jnp.zeros((8, d), F32).at[:b].set(c).at[b].set(c_ctx)
    mods_all = _adaln(cond, ada_w, ada_b)[:, :b + 1].reshape(depth, b + 1, 6, d)

    xs = jnp.concatenate([ctx.reshape(b * lctx, d), x.reshape(b * seq, d)], axis=0)
    ctx_tiles = geo[0]
    has_ctx = True
    for i in range(depth):
        kind, j = i % N_MIXERS, i // N_MIXERS
        ctx_later = any(kk % N_MIXERS != 0 for kk in range(i + 1, depth))
        mods = mods_all[i]
        route_w = (norm_ffn[i], moe_w_router[i], moe_b_router[i])
        if kind == 0:
            keep_ctx = has_ctx and ctx_later
            skip = 0 if keep_ctx or not has_ctx else ctx_tiles
            geo_i = geo if has_ctx else (0, geo[1], b)
            xs, routed = _gmlp_mixer(xs, mods, norm_mix[i], a_w_in[j], a_g_v[j], a_w_s[j], a_b_s[j], a_w_out[j],
                                     skip, xs.shape[0] // TM - skip, geo_i, route_w)
            has_ctx = keep_ctx
        elif kind == 1:
            assert has_ctx
            if not ctx_later:
                raise NotImplementedError("GLA mixer whose context rows are dropped before the FFN")
            xs, routed = _gla_mixer(xs, mods, norm_mix[i], b_w_in[j], b_w_a2[j], b_b_a[j], b_g_o[j], b_w_out[j],
                                    geo, dims, route_w)
        else:
            assert has_ctx
            if ctx_later:
                raise NotImplementedError("context output of the neighbourhood mixer")
            xs, routed = _na_mixer(xs, mods, norm_mix[i], c_w_qkv[j], c_rpb[j], c_w_out[j], geo, dims, route_w)
            has_ctx = False
        geo_i = geo if has_ctx else (0, geo[1], b)
        xs = _moe_layer(xs, routed, mods, i, moe_w_gu, moe_b_gu, moe_w_down, moe_b_down, geo_i, norm_out,
                        i == depth - 1)
    if has_ctx:
        xs = xs[b * lctx:]
    return xs.reshape(b, seq, d)
```

```python
import functools

import numpy as np
import jax
import jax.numpy as jnp
from jax import lax
from jax.experimental import pallas as pl
from jax.experimental.pallas import tpu as pltpu
from jax.experimental.pallas import tpu_sc as plsc

F32 = jnp.float32
BF16 = jnp.bfloat16
I32 = jnp.int32

LANES = 128
SUBLANES = 8
MXU_COLS = 256

NORM_EPS = 1e-6
GRID_W = 64
N_MIXERS = 3

CHUNK_A = 128
A_GROUPS = 8
GMLP_PART = 256
GLA_HEADS = 4
GLA_RANK = 16
GLA_TAU = 16.0
GLA_CHUNK = 128
ROPE_BASE = 10000.0
NA_HEADS = 16
NA_KH = 8
NA_KW = 16
NEG_INF = -1e30
NA_QROWS = 4
NA_WROWS = 12
NA_PAIRS_PER_STEP = 4
TOP_K = 4
SWIGLU_LIMIT = 7.0
SWIGLU_ALPHA = 1.702
MOE_BM = 1024
MOE_SUB = 256

TM = 512
GLA_TB = 256
GLA_PROJ_PART = 256
VMEM_LIMIT = 56 * 1024 * 1024


def _cparams(sem):
    return pltpu.CompilerParams(dimension_semantics=sem, vmem_limit_bytes=VMEM_LIMIT)


def _dot(a, b):
    return jnp.dot(a, b, preferred_element_type=F32)


def _dot_nt(a, b):
    return lax.dot_general(a, b, (((1,), (1,)), ((), ())), preferred_element_type=F32)


def _dot_tn(a, b):
    return lax.dot_general(a, b, (((0,), (0,)), ((), ())), preferred_element_type=F32)


def _rms(x, g):
    return x * lax.rsqrt(jnp.mean(x * x, axis=-1, keepdims=True) + NORM_EPS) * g


def _modulate(x, g, shift, scale):
    return _rms(x, g) * (1.0 + scale) + shift


ADA_COL_BLOCKS = 4


def _ada_kernel(st_ref, w_ref, b_ref, o_ref, *, nrows):
    st = st_ref[...]
    st = st * jax.nn.sigmoid(st)
    w = w_ref[0]
    rows = [jnp.sum(w * st[:, r:r + 1], axis=0, keepdims=True) for r in range(nrows)]
    rows += [jnp.zeros_like(rows[0])] * (o_ref.shape[1] - nrows)
    o_ref[0] = jnp.concatenate(rows, axis=0) + b_ref[0]


def _adaln(cond_t, nrows, ada_w, ada_b):
    depth, d, n6 = ada_w.shape
    bn = n6 // ADA_COL_BLOCKS
    return pl.pallas_call(
        functools.partial(_ada_kernel, nrows=nrows),
        grid=(depth, ADA_COL_BLOCKS),
        in_specs=[pl.BlockSpec((d, SUBLANES), lambda i, j: (0, 0)),
                  pl.BlockSpec((1, d, bn), lambda i, j: (i, 0, j)),
                  pl.BlockSpec((1, 1, bn), lambda i, j: (i, 0, j))],
        out_specs=pl.BlockSpec((1, SUBLANES, bn), lambda i, j: (i, 0, j)),
        out_shape=jax.ShapeDtypeStruct((depth, SUBLANES, n6), F32),
        compiler_params=_cparams(("parallel", "parallel")),
        name="adaln",
    )(cond_t, ada_w, ada_b.reshape(depth, 1, n6))


def _gelu(z):
    return 0.5 * z * (1.0 + lax.erf(z * np.float32(np.sqrt(0.5))))


def _gmlp_kernel(*refs, ctx_tiles):
    if ctx_tiles:
        xc_ref, xl_ref, *refs = refs
        *refs, x_ref = refs

        @pl.when(pl.program_id(0) < ctx_tiles)
        def _():
            x_ref[...] = xc_ref[...]

        @pl.when(pl.program_id(0) >= ctx_tiles)
        def _():
            x_ref[...] = xl_ref[...]
    else:
        x_ref, *refs = refs
    m_ref, g_ref, win_ref, gv_ref, ws_ref, bs_ref, wout_ref, *route_refs = refs
    route_in, o_ref, route_out = route_refs[:3], route_refs[3], route_refs[4:]
    m = m_ref[0]
    a = gv_ref.shape[1]
    gw = a // A_GROUPS
    npart = x_ref.shape[0] // GMLP_PART
    parts = [slice(p * GMLP_PART, (p + 1) * GMLP_PART) for p in range(npart)]
    z = [_dot(_modulate(x_ref[rs, :], g_ref[...], m[0:1], m[1:2]).astype(BF16), win_ref[...]) for rs in parts]
    for rs, zp in zip(parts, z):
        zp = _gelu(zp)
        u = zp[:, :a]
        v = _rms(zp[:, a:], gv_ref[...]).astype(BF16)
        rows = []
        for c in range(GMLP_PART // CHUNK_A):
            cols = [_dot(ws_ref[g], v[c * CHUNK_A:(c + 1) * CHUNK_A, g * gw:(g + 1) * gw])
                    for g in range(A_GROUPS)]
            rows.append(jnp.concatenate(cols, axis=1) + bs_ref[...])
        s = jnp.concatenate(rows, axis=0)
        y = _dot((u * s).astype(BF16), wout_ref[...])
        o_ref[rs, :] = x_ref[rs, :] + m[2:3] * y
    _route_tail(o_ref[...], m, *route_in, *route_out)


def _gmlp_mixer(xs, mods, g, w_in, g_v, w_s, b_s, w_out, tile0, n_tiles, geo, route_w):
    split = isinstance(xs, tuple)
    d = xs[0].shape[1] if split else xs.shape[1]
    a = g_v.shape[0]
    gw = a // A_GROUPS
    bias = jnp.repeat(b_s.T, gw, axis=1)
    grp = functools.partial(_tile_group, geo=geo, tile0=tile0)
    const2 = lambda t: (0, 0)
    r_in, r_out, r_shape, r_scratch = _route_specs(n_tiles * TM, d, route_w[1].shape[1])
    if split:
        assert tile0 == 0
        ctx_tiles = xs[0].shape[0] // TM
        x_specs = [pl.BlockSpec((TM, d), lambda t: (jnp.minimum(t, ctx_tiles - 1), 0)),
                   pl.BlockSpec((TM, d), lambda t: (jnp.maximum(t - ctx_tiles, 0), 0))]
        x_args, x_scratch = list(xs), [pltpu.VMEM((TM, d), F32)]
    else:
        ctx_tiles = 0
        x_specs, x_args, x_scratch = [pl.BlockSpec((TM, d), lambda t: (t + tile0, 0))], [xs], []
    out = pl.pallas_call(
        functools.partial(_gmlp_kernel, ctx_tiles=ctx_tiles),
        grid=(n_tiles,),
        in_specs=x_specs + [
                  pl.BlockSpec((1, 6, d), lambda t: (grp(t), 0, 0)),
                  pl.BlockSpec((1, d), const2),
                  pl.BlockSpec((d, 2 * a), const2),
                  pl.BlockSpec((1, a), const2),
                  pl.BlockSpec((A_GROUPS, CHUNK_A, CHUNK_A), lambda t: (0, 0, 0)),
                  pl.BlockSpec((CHUNK_A, a), const2),
                  pl.BlockSpec((a, d), const2)] + r_in,
        out_specs=[pl.BlockSpec((TM, d), lambda t: (t, 0))] + r_out,
        out_shape=[jax.ShapeDtypeStruct((n_tiles * TM, d), F32)] + r_shape,
        scratch_shapes=r_scratch + x_scratch,
        compiler_params=_cparams(("arbitrary",)),
        name="gmlp_mixer",
    )(*x_args, mods, g.reshape(1, d), w_in.astype(BF16), g_v.reshape(1, a), w_s.astype(BF16),
      bias, w_out.astype(BF16), *_route_args(route_w, d))
    return out[0], out[1:]


def _tile_group(t, geo, tile0=0):
    n_ctx_tiles, tiles_per_batch, nb = geo
    tt = t + tile0
    return jnp.where(tt < n_ctx_tiles, nb, (tt - n_ctx_tiles) // tiles_per_batch)


def _dot_f32(tri_bf16, x):
    hi = x.astype(BF16)
    r1 = x - hi.astype(F32)
    mid = r1.astype(BF16)
    lo = (r1 - mid.astype(F32)).astype(BF16)
    return _dot(tri_bf16, hi) + _dot(tri_bf16, mid) + _dot(tri_bf16, lo)


def _gla_proj_kernel(x_ref, m_ref, g_ref, wq_ref, wk_ref, wv_ref, wg_ref, wr_ref, wa_ref, ba_ref,
                     cos_ref, sin_ref, q_ref, k_ref, v_ref, go_ref, la_ref):
    m = m_ref[0]
    kdim = wq_ref.shape[1]
    hk = kdim // GLA_HEADS
    nf = hk // 4
    npart = x_ref.shape[0] // GLA_PROJ_PART
    parts = [slice(p * GLA_PROJ_PART, (p + 1) * GLA_PROJ_PART) for p in range(npart)]
    zs, qs, ks = [], [], []
    for rs in parts:
        h = _modulate(x_ref[rs, :], g_ref[...], m[0:1], m[1:2]).astype(BF16)
        zs.append(_dot(_dot(h, wr_ref[...]).astype(BF16), wa_ref[...]) + ba_ref[...])
        qs.append(_dot(h, wq_ref[...]) * np.float32(hk ** -0.5))
        ks.append(_dot(h, wk_ref[...]))
        v_ref[rs, :] = _dot(h, wv_ref[...]).astype(BF16)
        go_ref[rs, :] = _dot(h, wg_ref[...])
    for rs, z in zip(parts, zs):
        la_ref[rs, :] = jax.nn.log_sigmoid(z) * np.float32(1.0 / GLA_TAU)

    lane = lax.broadcasted_iota(I32, (GLA_PROJ_PART, kdim), 1)
    first = (lane % (2 * nf)) < nf
    for rs, q, k in zip(parts, qs, ks):
        cos = jnp.concatenate([cos_ref[rs, :]] * GLA_HEADS, axis=1)
        sin = jnp.concatenate([sin_ref[rs, :]] * GLA_HEADS, axis=1)

        def rope(t):
            up = pltpu.roll(t, kdim - nf, 1)
            dn = pltpu.roll(t, nf, 1)
            return t * cos + jnp.where(first, up, dn) * sin

        q_ref[rs, :] = rope(q)
        k_ref[rs, :] = rope(k)


def _gla_decays(q_ref, k_ref, la_ref, n, rev):
    c = GLA_CHUNK
    ri = lax.broadcasted_iota(I32, (c, c), 0)
    ci = lax.broadcasted_iota(I32, (c, c), 1)
    keep = (ci >= ri) if rev else (ci <= ri)
    rows = slice(n * c, (n + 1) * c)
    cum = _dot_f32(keep.astype(BF16), la_ref[rows, :])
    last = cum[0:1] if rev else cum[c - 1:c]
    q = q_ref[rows, :]
    k = k_ref[rows, :]
    mid = cum[c // 2:c // 2 + 1]
    return dict(rows=rows, keep=keep, dec=jnp.exp(last),
                q_dec=(q * jnp.exp(cum)).astype(BF16),
                q_att=(q * jnp.exp(cum - mid)).astype(BF16),
                k_att=(k * jnp.exp(mid - cum)).astype(BF16),
                k_end=(k * jnp.exp(last - cum)).astype(BF16))


def _gla_scan_kernel(qf_ref, kf_ref, vf_ref, laf_ref, qb_ref, kb_ref, vb_ref, lab_ref,
                     of_ref, ob_ref, stf_ref, stb_ref):
    s = pl.program_id(1)

    @pl.when(s == 0)
    def _():
        stf_ref[...] = jnp.zeros_like(stf_ref)
        stb_ref[...] = jnp.zeros_like(stb_ref)

    nch = qf_ref.shape[0] // GLA_CHUNK
    hk = qf_ref.shape[1] // GLA_HEADS
    hv = vf_ref.shape[1] // GLA_HEADS
    dirs = ((qf_ref, kf_ref, vf_ref, laf_ref, of_ref, stf_ref, False),
            (qb_ref, kb_ref, vb_ref, lab_ref, ob_ref, stb_ref, True))
    heads = [(slice(h * hk, (h + 1) * hk), slice(h * hv, (h + 1) * hv)) for h in range(GLA_HEADS)]
    for n in range(nch):
        w = [_gla_decays(q_ref, k_ref, la_ref, nch - 1 - n if rev else n, rev)
             for q_ref, k_ref, _, la_ref, _, _, rev in dirs]
        att = [[jnp.where(w[d]["keep"], _dot_nt(w[d]["q_att"][:, ks], w[d]["k_att"][:, ks]), 0.0).astype(BF16)
                for ks, _ in heads] for d in range(2)]
        for d, (_, _, v_ref, _, o_ref, st_ref, _) in enumerate(dirs):
            for h, (ks, vs) in enumerate(heads):
                o_ref[w[d]["rows"], vs] = (_dot(att[d][h], v_ref[w[d]["rows"], vs])
                                           + _dot_nt(w[d]["q_dec"][:, ks], st_ref[h].astype(BF16)))
        for d, (_, _, v_ref, _, _, st_ref, _) in enumerate(dirs):
            for h, (ks, vs) in enumerate(heads):
                st_ref[h] = (st_ref[h] * w[d]["dec"][:, ks]
                             + _dot_tn(v_ref[w[d]["rows"], vs], w[d]["k_end"][:, ks]))


def _gla_out_kernel(x_ref, m_ref, of_ref, ob_ref, go_ref, gn_ref, wout_ref, *route_refs):
    route_in, o_ref, route_out = route_refs[:3], route_refs[3], route_refs[4:]
    x = x_ref[...]
    m = m_ref[0]
    o = of_ref[...] + ob_ref[...]
    hv = gn_ref.shape[1]
    parts = [_rms(o[:, h * hv:(h + 1) * hv], gn_ref[...]) for h in range(GLA_HEADS)]
    o = jnp.concatenate(parts, axis=1)
    gate = go_ref[...]
    y = _dot((o * (gate * jax.nn.sigmoid(gate))).astype(BF16), wout_ref[...])
    x_new = x + m[2:3] * y
    o_ref[...] = x_new
    _route_tail(x_new, m, *route_in, *route_out)


def _gla_mixer(xs, mods, g, w_in, w_a2, b_a, g_o, w_out, geo, dims, route_w):
    b, seq, lctx = dims
    nt, d = xs.shape
    n_tiles = nt // TM
    kdim = w_a2.shape[2]
    vdim = g_o.shape[0] * GLA_HEADS
    hk = kdim // GLA_HEADS
    nf = hk // 4
    wq = w_in[:, :kdim].astype(BF16)
    wk = w_in[:, kdim:2 * kdim].astype(BF16)
    wv = w_in[:, 2 * kdim:2 * kdim + vdim].astype(BF16)
    wg = w_in[:, 2 * kdim + vdim:2 * kdim + 2 * vdim].astype(BF16)
    wr = jnp.pad(w_in[:, 2 * kdim + 2 * vdim:], ((0, 0), (0, 128 - 2 * GLA_RANK))).astype(BF16)
    wa = jnp.zeros((128, 2 * kdim), F32)
    wa = wa.at[:GLA_RANK, :kdim].set(w_a2[0]).at[GLA_RANK:2 * GLA_RANK, kdim:].set(w_a2[1]).astype(BF16)
    ba = b_a.reshape(1, 2 * kdim)

    tpos = np.arange(seq)
    freqs = np.power(np.float32(ROPE_BASE), -np.arange(nf, dtype=np.float32) / np.float32(nf)).astype(np.float32)
    ar = (tpos // GRID_W).astype(np.float32)[:, None] * freqs
    ac = (tpos % GRID_W).astype(np.float32)[:, None] * freqs
    cos = np.concatenate([np.cos(ar), np.cos(ar), np.cos(ac), np.cos(ac)], axis=1)
    sin = np.concatenate([-np.sin(ar), np.sin(ar), -np.sin(ac), np.sin(ac)], axis=1)
    cos = jnp.asarray(np.concatenate([np.ones((TM, hk), np.float32), cos], axis=0).astype(np.float32))
    sin = jnp.asarray(np.concatenate([np.zeros((TM, hk), np.float32), sin], axis=0).astype(np.float32))

    n_ctx_tiles, tiles_per_batch, _ = geo
    grp = functools.partial(_tile_group, geo=geo)

    def rope_blk(t):
        return (jnp.where(t < n_ctx_tiles, 0, 1 + (t - n_ctx_tiles) % tiles_per_batch), 0)

    const2 = lambda t: (0, 0)
    row = lambda t: (t, 0)
    q, k, v, go, la = pl.pallas_call(
        _gla_proj_kernel,
        grid=(n_tiles,),
        in_specs=[pl.BlockSpec((TM, d), row),
                  pl.BlockSpec((1, 6, d), lambda t: (grp(t), 0, 0)),
                  pl.BlockSpec((1, d), const2),
                  pl.BlockSpec((d, kdim), const2), pl.BlockSpec((d, kdim), const2),
                  pl.BlockSpec((d, vdim), const2), pl.BlockSpec((d, vdim), const2),
                  pl.BlockSpec((d, 128), const2), pl.BlockSpec((128, 2 * kdim), const2),
                  pl.BlockSpec((1, 2 * kdim), const2),
                  pl.BlockSpec((TM, hk), rope_blk), pl.BlockSpec((TM, hk), rope_blk)],
        out_specs=[pl.BlockSpec((TM, kdim), row), pl.BlockSpec((TM, kdim), row),
                   pl.BlockSpec((TM, vdim), row), pl.BlockSpec((TM, vdim), row),
                   pl.BlockSpec((TM, 2 * kdim), row)],
        out_shape=[jax.ShapeDtypeStruct((nt, kdim), F32), jax.ShapeDtypeStruct((nt, kdim), F32),
                   jax.ShapeDtypeStruct((nt, vdim), BF16), jax.ShapeDtypeStruct((nt, vdim), F32),
                   jax.ShapeDtypeStruct((nt, 2 * kdim), F32)],
        compiler_params=_cparams(("parallel",)),
        name="gla_proj",
    )(xs, mods, g.reshape(1, d), wq, wk, wv, wg, wr, wa, ba, cos, sin)

    tb = GLA_TB
    ctx_steps = lctx // tb
    lat_steps = seq // tb
    steps = ctx_steps + lat_steps

    def blk(bi, s, rev):
        if rev:
            cs, ls = ctx_steps - 1 - s, lat_steps - 1 - (s - ctx_steps)
        else:
            cs, ls = s, s - ctx_steps
        return jnp.where(s < ctx_steps, bi * ctx_steps + cs, b * ctx_steps + bi * lat_steps + ls)

    def dir_specs(rev):
        row_blk = lambda bi, s: (blk(bi, s, rev), 0)
        return [pl.BlockSpec((tb, kdim), row_blk), pl.BlockSpec((tb, kdim), row_blk),
                pl.BlockSpec((tb, vdim), row_blk),
                pl.BlockSpec((tb, kdim), lambda bi, s: (blk(bi, s, rev), 1 if rev else 0))]

    state = pltpu.VMEM((GLA_HEADS, vdim // GLA_HEADS, hk), F32)
    o_f, o_b = pl.pallas_call(
        _gla_scan_kernel,
        grid=(b, steps),
        in_specs=dir_specs(False) + dir_specs(True),
        out_specs=[pl.BlockSpec((tb, vdim), lambda bi, s: (blk(bi, s, False), 0)),
                   pl.BlockSpec((tb, vdim), lambda bi, s: (blk(bi, s, True), 0))],
        out_shape=[jax.ShapeDtypeStruct((nt, vdim), F32)] * 2,
        scratch_shapes=[state, state],
        compiler_params=_cparams(("parallel", "arbitrary")),
        name="gla_scan",
    )(q, k, v, la, q, k, v, la)

    r_in, r_out, r_shape, r_scratch = _route_specs(nt, d, route_w[1].shape[1])
    out = pl.pallas_call(
        _gla_out_kernel,
        grid=(n_tiles,),
        in_specs=[pl.BlockSpec((TM, d), row),
                  pl.BlockSpec((1, 6, d), lambda t: (grp(t), 0, 0)),
                  pl.BlockSpec((TM, vdim), row), pl.BlockSpec((TM, vdim), row),
                  pl.BlockSpec((TM, vdim), row),
                  pl.BlockSpec((1, vdim // GLA_HEADS), const2),
                  pl.BlockSpec((vdim, d), const2)] + r_in,
        out_specs=[pl.BlockSpec((TM, d), row)] + r_out,
        out_shape=[jax.ShapeDtypeStruct((nt, d), F32)] + r_shape,
        scratch_shapes=r_scratch,
        compiler_params=_cparams(("arbitrary",)),
        name="gla_out",
    )(xs, mods, o_f, o_b, go, g_o.reshape(1, -1), w_out.astype(BF16), *_route_args(route_w, d))
    return out[0], out[1:]


def _na_proj_kernel(x_ref, m_ref, g_ref, w_ref, q_ref, k_ref, v_ref):
    x = x_ref[...]
    m = m_ref[0]
    d = x.shape[1]
    h = _modulate(x, g_ref[...], m[0:1], m[1:2]).astype(BF16)
    qkv = _dot(h, w_ref[...])
    hd = d // NA_HEADS
    for p in range(q_ref.shape[0]):
        cs = slice(p * LANES, (p + 1) * LANES)
        q_ref[p] = (qkv[:, cs] * np.float32(hd ** -0.5)).astype(BF16)
        k_ref[p] = qkv[:, d + p * LANES:d + (p + 1) * LANES].astype(BF16)
        v_ref[p] = qkv[:, 2 * d + p * LANES:2 * d + (p + 1) * LANES].astype(BF16)


def _na_attn_kernel(tbl_ref, q_ref, *refs, n_tiles, nwin):
    k_refs, v_refs = refs[:nwin], refs[nwin:2 * nwin]
    kc_ref, vc_ref, bias_ref, o_ref, s_ref, p_ref = refs[2 * nwin:]
    t = pl.program_id(2)
    typ = jnp.where(t == 0, 0, jnp.where(t == n_tiles - 1, 2, 1))
    npair = NA_WROWS // 2
    nlat = NA_WROWS * GRID_W
    hd = LANES // 2
    lane = lax.broadcasted_iota(I32, q_ref.shape[1:], 1)
    npp, tq = q_ref.shape[0], q_ref.shape[1]
    for pp in range(npp):
        q = q_ref[pp]
        k_all = jnp.concatenate([r[pp] for r in k_refs] + [kc_ref[pp]], axis=0)
        zero = jnp.zeros_like(q)
        q2 = jnp.concatenate([jnp.where(lane < hd, q, zero), jnp.where(lane >= hd, q, zero)], axis=0)
        s_ref[pp] = _dot_nt(q2, k_all)
    for pp in range(npp):
        for hh in range(2):
            for dr in range(NA_QROWS):
                rs = slice(hh * tq + dr * GRID_W, hh * tq + (dr + 1) * GRID_W)
                lat = [s_ref[pp, rs, j * LANES:(j + 1) * LANES]
                       + bias_ref[2 * pp + hh, tbl_ref[typ * (NA_QROWS * npair) + dr * npair + j]]
                       for j in range(npair)]
                sb = jnp.concatenate(lat + [s_ref[pp, rs, nlat:]], axis=1)
                e = jnp.exp(sb - jnp.max(sb, axis=1, keepdims=True))
                p_ref[pp, rs, :] = (e * (1.0 / jnp.sum(e, axis=1, keepdims=True))).astype(BF16)
    outs = []
    for pp in range(npp):
        v_all = jnp.concatenate([r[pp] for r in v_refs] + [vc_ref[pp]], axis=0)
        o2 = _dot(p_ref[pp], v_all)
        outs.append(jnp.where(lane < hd, o2[:tq], o2[tq:]))
    o_ref[...] = jnp.concatenate(outs, axis=1).astype(BF16)


def _na_out_kernel(x_ref, m_ref, a_ref, w_ref, *route_refs):
    route_in, o_ref, route_out = route_refs[:3], route_refs[3], route_refs[4:]
    m = m_ref[0]
    x_new = x_ref[...] + m[2:3] * _dot(a_ref[...], w_ref[...])
    o_ref[...] = x_new
    _route_tail(x_new, m, *route_in, *route_out)


def _na_tables(rows):
    npair = NA_WROWS // 2
    tbl = np.zeros((3, NA_QROWS, npair), np.int32)
    for typ, r_base in enumerate((0, NA_QROWS, rows - NA_QROWS)):
        w0 = int(np.clip(r_base - NA_KH // 2, 0, rows - NA_WROWS))
        for dr in range(NA_QROWS):
            r = r_base + dr
            r0 = int(np.clip(r - NA_KH // 2, 0, rows - NA_KH))
            for j in range(npair):
                kr = (w0 + 2 * j, w0 + 2 * j + 1)
                ok = [r0 <= x < r0 + NA_KH for x in kr]
                ri = [x - r + NA_KH - 1 for x in kr]
                if ok[0] and ok[1]:
                    e = ri[0]
                elif ok[0]:
                    e = 16 + ri[0]
                elif ok[1]:
                    e = 32 + ri[1]
                else:
                    e = 63
                tbl[typ, dr, j] = e
    return tbl.reshape(-1)


def _na_bias_kernel(rpb_ref, o_ref, *, nr, nw):
    h = pl.program_id(0)
    c = lax.broadcasted_iota(I32, (GRID_W, GRID_W), 0)
    k = lax.broadcasted_iota(I32, (GRID_W, GRID_W), 1)
    cstart = jnp.clip(c - NA_KW // 2, 0, GRID_W - NA_KW)
    ok = jnp.logical_and(k >= cstart, k < cstart + NA_KW)
    rel = k - c + NA_KW - 1
    neg = jnp.full((GRID_W, GRID_W), NEG_INF, F32)
    cbs = []
    for r in range(nr):
        acc = neg
        for w in range(nw):
            acc = jnp.where(rel == w, rpb_ref[(h * nr + r) * nw + w], acc)
        cbs.append(jnp.where(ok, acc, neg))
    cbs += [neg] * (17 - nr)
    for r in range(16):
        o_ref[0, r] = jnp.concatenate([cbs[r], cbs[r + 1]], axis=1)
        o_ref[0, 16 + r] = jnp.concatenate([cbs[r], neg], axis=1)
        o_ref[0, 32 + r] = jnp.concatenate([neg, cbs[r]], axis=1)
        o_ref[0, 48 + r] = jnp.concatenate([neg, neg], axis=1)


def _na_bias_table(rpb):
    nh, nr, nw = rpb.shape
    return pl.pallas_call(
        functools.partial(_na_bias_kernel, nr=nr, nw=nw),
        grid=(nh,),
        in_specs=[pl.BlockSpec(memory_space=pltpu.SMEM)],
        out_specs=pl.BlockSpec((1, 64, GRID_W, 2 * GRID_W), lambda h: (h, 0, 0, 0)),
        out_shape=jax.ShapeDtypeStruct((nh, 64, GRID_W, 2 * GRID_W), F32),
        compiler_params=_cparams(("parallel",)),
        name="na_bias_table",
    )(rpb.reshape(-1).astype(F32))


def _na_mixer(xs, mods, g, w_qkv, rpb, w_out, geo, dims, route_w):
    b, seq, lctx = dims
    nt, d = xs.shape
    n_tiles_tok = nt // TM
    grp = functools.partial(_tile_group, geo=geo)
    npairs = d // 128
    const2 = lambda t: (0, 0)
    q, k, v = pl.pallas_call(
        _na_proj_kernel,
        grid=(n_tiles_tok,),
        in_specs=[pl.BlockSpec((TM, d), lambda t: (t, 0)),
                  pl.BlockSpec((1, 6, d), lambda t: (grp(t), 0, 0)),
                  pl.BlockSpec((1, d), const2),
                  pl.BlockSpec((d, 3 * d), const2)],
        out_specs=[pl.BlockSpec((npairs, TM, 128), lambda t: (0, t, 0))] * 3,
        out_shape=[jax.ShapeDtypeStruct((npairs, nt, 128), BF16)] * 3,
        compiler_params=_cparams(("parallel",)),
        name="na_proj",
    )(xs, mods, g.reshape(1, d), w_qkv.astype(BF16))

    rows = seq // GRID_W
    tq = NA_QROWS * GRID_W
    n_tiles = rows // NA_QROWS
    wb = 256
    nwin = NA_WROWS * GRID_W // wb
    lat0 = b * lctx
    tbl = jnp.asarray(_na_tables(rows))
    bias = _na_bias_table(rpb)

    pp = NA_PAIRS_PER_STEP
    assert npairs % pp == 0 and (NA_QROWS * GRID_W) % wb == 0 and (NA_KH // 2 * GRID_W) % wb == 0

    def win(i):
        def f(p, bi, t, tbl_ref):
            w = jnp.clip(t * (NA_QROWS * GRID_W // wb) - NA_KH // 2 * GRID_W // wb, 0, seq // wb - nwin)
            return (p, (lat0 + bi * seq) // wb + w + i, 0)
        return f

    kv_specs = [pl.BlockSpec((pp, wb, 128), win(i)) for i in range(nwin)]
    attn = pl.pallas_call(
        functools.partial(_na_attn_kernel, n_tiles=n_tiles, nwin=nwin),
        grid_spec=pltpu.PrefetchScalarGridSpec(
            num_scalar_prefetch=1,
            grid=(npairs // pp, b, n_tiles),
            in_specs=[pl.BlockSpec((pp, tq, 128), lambda p, bi, t, tr: (p, (lat0 + bi * seq) // tq + t, 0))]
                     + kv_specs + kv_specs
                     + [pl.BlockSpec((pp, lctx, 128), lambda p, bi, t, tr: (p, bi, 0)),
                        pl.BlockSpec((pp, lctx, 128), lambda p, bi, t, tr: (p, bi, 0)),
                        pl.BlockSpec((2 * pp, 64, GRID_W, 2 * GRID_W), lambda p, bi, t, tr: (p, 0, 0, 0),
                                     pipeline_mode=pl.Buffered(1))],
            out_specs=pl.BlockSpec((tq, 128 * pp), lambda p, bi, t, tr: (bi * n_tiles + t, p)),
            scratch_shapes=[pltpu.VMEM((pp, 2 * tq, NA_WROWS * GRID_W + lctx), F32),
                            pltpu.VMEM((pp, 2 * tq, NA_WROWS * GRID_W + lctx), BF16)]),
        out_shape=jax.ShapeDtypeStruct((b * seq, d), BF16),
        compiler_params=_cparams(("parallel", "parallel", "arbitrary")),
        name="na_attn",
    )(tbl, q, *([k] * nwin), *([v] * nwin), k, v, bias)

    n_lat_tiles = b * seq // TM
    tile0 = lat0 // TM
    grp_l = functools.partial(_tile_group, geo=geo, tile0=tile0)
    r_in, r_out, r_shape, r_scratch = _route_specs(b * seq, d, route_w[1].shape[1])
    out = pl.pallas_call(
        _na_out_kernel,
        grid=(n_lat_tiles,),
        in_specs=[pl.BlockSpec((TM, d), lambda t: (t + tile0, 0)),
                  pl.BlockSpec((1, 6, d), lambda t: (grp_l(t), 0, 0)),
                  pl.BlockSpec((TM, d), lambda t: (t, 0)),
                  pl.BlockSpec((d, d), const2)] + r_in,
        out_specs=[pl.BlockSpec((TM, d), lambda t: (t, 0))] + r_out,
        out_shape=[jax.ShapeDtypeStruct((b * seq, d), F32)] + r_shape,
        scratch_shapes=r_scratch,
        compiler_params=_cparams(("arbitrary",)),
        name="na_out",
    )(xs, mods, attn, w_out.astype(BF16), *_route_args(route_w, d))
    return out[0], out[1:]


SC_CORES = 2
SC_SUBCORES = 16
SC_WORKERS = SC_CORES * SC_SUBCORES
SLOT_T_MAX = 2048


def _sc_mesh():
    return plsc.VectorSubcoreMesh(core_axis_name="c", subcore_axis_name="s")


def _sc_chunk(per_worker, max_chunk):
    return max(c for c in range(8, max_chunk + 1, 8) if per_worker % (2 * c) == 0)


def _sc_gather_rows(table, idx):
    dd = table.shape[1]
    bsz = idx.shape[0]
    per_w = bsz // SC_WORKERS
    assert per_w * SC_WORKERS == bsz
    chunk = _sc_chunk(per_w, 64)
    n_chunks = per_w // chunk

    def body(table_hbm, idx_hbm, out_hbm, idx_v, rows0, rows1, g0, g1, w0, w1):
        wid = lax.axis_index("s") * SC_CORES + lax.axis_index("c")
        pltpu.sync_copy(idx_hbm.at[wid], idx_v)
        base = wid * per_w

        def out_rows(j):
            return out_hbm.at[pl.ds(pl.multiple_of(base + j * chunk, 8), chunk)]

        def step(i, carry):
            j0, j1 = 2 * i, 2 * i + 1
            ga = pltpu.async_copy(table_hbm.at[idx_v.at[j0]], rows0, g0)
            gb = pltpu.async_copy(table_hbm.at[idx_v.at[j1]], rows1, g1)
            ga.wait()
            wa = pltpu.async_copy(rows0, out_rows(j0), w0)
            gb.wait()
            wb = pltpu.async_copy(rows1, out_rows(j1), w1)
            wa.wait()
            wb.wait()
            return carry

        lax.fori_loop(0, n_chunks // 2, step, 0)

    return pl.kernel(
        body, out_type=jax.ShapeDtypeStruct((bsz, dd), table.dtype), mesh=_sc_mesh(),
        scratch_types=[pltpu.VMEM((n_chunks, chunk), I32),
                       pltpu.VMEM((chunk, dd), table.dtype), pltpu.VMEM((chunk, dd), table.dtype),
                       pltpu.SemaphoreType.DMA, pltpu.SemaphoreType.DMA,
                       pltpu.SemaphoreType.DMA, pltpu.SemaphoreType.DMA],
        name="sc_gather_rows",
    )(table, idx.reshape(SC_WORKERS, n_chunks, chunk))


def _sc_scatter_rows(rows, dest, n_out):
    n, dd = rows.shape
    kk = dest.shape[0]
    per_w = n // SC_WORKERS
    assert per_w * SC_WORKERS == n
    chunk = _sc_chunk(per_w, 64)
    n_chunks = per_w // chunk
    dest_w = dest.reshape(kk, SC_WORKERS, n_chunks, chunk).transpose(1, 2, 0, 3)
    dest_w = dest_w.reshape(SC_WORKERS, n_chunks * kk, chunk)

    def body(rows_hbm, dest_hbm, out_hbm, idx_v, rows0, rows1, r0, r1, s0, s1):
        wid = lax.axis_index("s") * SC_CORES + lax.axis_index("c")
        pltpu.sync_copy(dest_hbm.at[wid], idx_v)
        base = wid * per_w

        def in_rows(j):
            return rows_hbm.at[pl.ds(pl.multiple_of(base + j * chunk, 8), chunk)]

        def step(i, carry):
            j0, j1 = 2 * i, 2 * i + 1
            ra = pltpu.async_copy(in_rows(j0), rows0, r0)
            rb = pltpu.async_copy(in_rows(j1), rows1, r1)
            ra.wait()
            sa = [pltpu.async_copy(rows0, out_hbm.at[idx_v.at[j0 * kk + k]], s0) for k in range(kk)]
            rb.wait()
            sb = [pltpu.async_copy(rows1, out_hbm.at[idx_v.at[j1 * kk + k]], s1) for k in range(kk)]
            for cp in sa + sb:
                cp.wait()
            return carry

        lax.fori_loop(0, n_chunks // 2, step, 0)

    return pl.kernel(
        body, out_type=jax.ShapeDtypeStruct((n_out, dd), rows.dtype), mesh=_sc_mesh(),
        scratch_types=[pltpu.VMEM((n_chunks * kk, chunk), I32),
                       pltpu.VMEM((chunk, dd), rows.dtype), pltpu.VMEM((chunk, dd), rows.dtype),
                       pltpu.SemaphoreType.DMA, pltpu.SemaphoreType.DMA,
                       pltpu.SemaphoreType.DMA, pltpu.SemaphoreType.DMA],
        name="sc_scatter_rows",
    )(rows, dest_w)


def _prefix_sum_rows(col):
    nr = col.shape[0]
    acc = jnp.broadcast_to(col, (nr, LANES))
    row = lax.broadcasted_iota(I32, (nr, LANES), 0)
    s = 1
    while s < nr:
        acc = acc + jnp.where(row >= s, pltpu.roll(acc, s, 0), 0.0)
        s *= 2
    return acc[:, 0:1]


def _slot_kernel(idx_ref, rank_ref, cnt_ref, dest_ref, btab_ref, *, bm):
    cnt = cnt_ref[...]
    ne = cnt.shape[0]
    padded = jnp.floor((cnt + (bm - 1.0)) * (1.0 / bm)) * bm
    pad_end = _prefix_sum_rows(padded)
    pad_start = pad_end - padded
    idx = idx_ref[...]
    e_iota = lax.broadcasted_iota(I32, (ne, idx.shape[1]), 0)
    starts = [jnp.sum(jnp.where(e_iota == idx[k:k + 1], pad_start, 0.0), axis=0, keepdims=True)
              for k in range(idx.shape[0])]
    dest_ref[...] = jnp.concatenate(starts, axis=0).astype(I32) + rank_ref[...]

    nbp = btab_ref.shape[1]
    bstart = lax.broadcasted_iota(I32, (ne, nbp), 1).astype(F32) * bm
    be = jnp.minimum(jnp.sum((pad_end <= bstart).astype(F32), axis=0, keepdims=True), ne - 1.0)
    mine = lax.broadcasted_iota(I32, (ne, nbp), 0).astype(F32) == be
    pick = lambda col: jnp.sum(jnp.where(mine, col, 0.0), axis=0, keepdims=True)
    live = jnp.clip(pick(pad_start + cnt) - bstart[0:1], 0.0, bm)
    r = lax.broadcasted_iota(I32, (ne, ne), 0)
    c = lax.broadcasted_iota(I32, (ne, ne), 1)
    ends_on_lanes = _dot_f32(jnp.ones((ne, ne), BF16), jnp.where(r == c, pad_end, 0.0))
    nxt_e = jnp.sum((ends_on_lanes <= pad_end).astype(F32), axis=1, keepdims=True)
    nxt_e = jnp.where(nxt_e < ne, nxt_e, -1.0)
    ordinal = _prefix_sum_rows((cnt > 0).astype(F32)) - 1.0
    slot_e = ordinal - 2.0 * jnp.floor(ordinal * 0.5)
    rowi = lax.broadcasted_iota(I32, btab_ref.shape, 0)
    tab = jnp.where(rowi == 0, be, jnp.where(rowi == 1, live, jnp.where(rowi == 2, pick(nxt_e), pick(slot_e))))
    btab_ref[...] = tab.astype(I32)


def _route_specs(n, d, ne):
    const2 = lambda t: (0, 0)
    in_specs = [pl.BlockSpec((1, d), const2), pl.BlockSpec((ne, d), const2), pl.BlockSpec((ne, 1), const2)]
    out_specs = [pl.BlockSpec((TM, d // 2), lambda t: (t, 0)),
                 pl.BlockSpec((TOP_K, TM), lambda t: (0, t)),
                 pl.BlockSpec((TM, SUBLANES), lambda t: (t, 0)),
                 pl.BlockSpec((TOP_K, TM), lambda t: (0, t)),
                 pl.BlockSpec((ne, 1), const2)]
    out_shape = [jax.ShapeDtypeStruct((n, d // 2), jnp.uint32),
                 jax.ShapeDtypeStruct((TOP_K, n), I32),
                 jax.ShapeDtypeStruct((n, SUBLANES), F32),
                 jax.ShapeDtypeStruct((TOP_K, n), I32),
                 jax.ShapeDtypeStruct((ne, 1), F32)]
    return in_specs, out_specs, out_shape, [pltpu.VMEM((ne, 1), F32)]


def _route_args(route_w, d):
    g_ffn, w_router, b_router = route_w
    ne = w_router.shape[1]
    return g_ffn.reshape(1, d), w_router.T.astype(BF16), b_router.reshape(ne, 1)


def _route_tail(x, m, g_ref, wr_ref, br_ref, h_ref, idx_ref, gate_ref, rank_ref, cnt_ref, run_ref):
    t = pl.program_id(0)

    @pl.when(t == 0)
    def _():
        run_ref[...] = jnp.zeros_like(run_ref)

    h = _modulate(x, g_ref[...], m[3:4], m[4:5]).astype(BF16)
    h_ref[...] = _pack_bf16_pairs(h)
    logits = _dot_nt(wr_ref[...], h) + br_ref[...]
    ne, tm = logits.shape
    e_iota = lax.broadcasted_iota(I32, (ne, tm), 0)
    vals, idxs = [], []
    l = logits
    for _ in range(TOP_K):
        mk = jnp.max(l, axis=0, keepdims=True)
        ik = jnp.min(jnp.where(l == mk, e_iota, ne), axis=0, keepdims=True)
        vals.append(mk)
        idxs.append(ik)
        l = jnp.where(e_iota == ik, -jnp.inf, l)
    top_val = jnp.concatenate(vals, axis=0)
    ex = jnp.exp(top_val - vals[0])
    gates = ex / jnp.sum(ex, axis=0, keepdims=True)
    pad = jnp.zeros((gate_ref.shape[1] - TOP_K, tm), F32)
    gate_ref[...] = jnp.concatenate([gates, pad], axis=0).T
    idx_ref[...] = jnp.concatenate(idxs, axis=0)

    hits = [e_iota == ik for ik in idxs]
    cnt = hits[0].astype(F32)
    for hk in hits[1:]:
        cnt = cnt + hk.astype(F32)
    si = lax.broadcasted_iota(I32, (tm, tm), 0)
    ti = lax.broadcasted_iota(I32, (tm, tm), 1)
    before = (si < ti).astype(BF16)
    total = _dot(cnt.astype(BF16), before) + run_ref[...]
    ranks = [jnp.sum(jnp.where(hk, total, 0.0), axis=0, keepdims=True) for hk in hits]
    rank_ref[...] = jnp.concatenate(ranks, axis=0).astype(I32)
    run_ref[...] = run_ref[...] + jnp.sum(cnt, axis=1, keepdims=True)
    cnt_ref[...] = run_ref[...]


def _pack_bf16_pairs(v):
    bits = lax.bitcast_convert_type(v.astype(BF16).astype(F32), jnp.uint32)
    half = bits.shape[1] // 2
    return (bits[:, half:] & jnp.uint32(0xFFFF0000)) | (bits[:, :half] >> 16)


def _unpack_bf16_pairs(w):
    return (lax.bitcast_convert_type(w << 16, F32),
            lax.bitcast_convert_type(w & jnp.uint32(0xFFFF0000), F32))


def _expert_kernel(tab_ref, x_ref, wgu_hbm, bgu_ref, wd_hbm, bd_ref, o_ref,
                   wgu_in, wd_in, wgu_bf, wd_bf, act_ref, sem_gu, sem_d, *, layer):
    blk = pl.program_id(0)
    ch = wgu_bf.shape[2]
    f = act_ref.shape[1]
    nc = f // ch
    expert, live, nxt, wslot = (tab_ref[i, blk] for i in range(4))
    used = live > 0
    fresh = jnp.logical_or(blk == 0, expert != tab_ref[0, jnp.maximum(blk - 1, 0)])

    def fetch(e, slot):
        return (pltpu.make_async_copy(wgu_hbm.at[layer, e], wgu_in.at[slot], sem_gu.at[slot]),
                pltpu.make_async_copy(wd_hbm.at[layer, e], wd_in.at[slot], sem_d.at[slot]))

    @pl.when(jnp.logical_and(used, fresh))
    def _():
        slot = wslot
        mine = fetch(expert, slot)

        @pl.when(blk == 0)
        def _():
            for cp in mine:
                cp.start()

        @pl.when(nxt >= 0)
        def _():
            for cp in fetch(nxt, 1 - slot):
                cp.start()

        for cp in mine:
            cp.wait()

    bm = x_ref.shape[0]
    fresh_full = jnp.logical_and(fresh, live > bm - MOE_SUB)

    def cast_gu(j):
        wgu_bf[j] = wgu_in[wslot, :, j * ch:(j + 1) * ch].astype(BF16)

    def cast_d(j):
        wd_bf[j] = wd_in[wslot, :, j * ch:(j + 1) * ch].astype(BF16)

    @pl.when(jnp.logical_and(jnp.logical_and(used, fresh), jnp.logical_not(fresh_full)))
    def _():
        for j in range(wgu_bf.shape[0]):
            cast_gu(j)
        for j in range(wd_bf.shape[0]):
            cast_d(j)

    def run(nr, cast):
        w = x_ref[:nr, :]
        row = lax.broadcasted_iota(I32, w.shape, 0)
        w = jnp.where(row < live, w, jnp.zeros_like(w))
        lo, hi = _unpack_bf16_pairs(w)
        x = jnp.concatenate([lo.astype(BF16), hi.astype(BF16)], axis=1)
        for c in range(nc):
            c0 = slice(c * ch, (c + 1) * ch)
            c1 = slice(f + c * ch, f + (c + 1) * ch)
            if cast:
                cast_gu(c)
                cast_gu(nc + c)
            glu = _dot(x, wgu_bf[c]) + bgu_ref[0, 0, :, c0]
            lin = _dot(x, wgu_bf[nc + c]) + bgu_ref[0, 0, :, c1]
            glu = jnp.minimum(glu, SWIGLU_LIMIT)
            lin = jnp.clip(lin, -SWIGLU_LIMIT, SWIGLU_LIMIT)
            act_ref[:nr, c0] = (glu * jax.nn.sigmoid(SWIGLU_ALPHA * glu) * (lin + 1.0)).astype(BF16)
        a = act_ref[:nr, :]
        ys = []
        for n in range(wd_bf.shape[0]):
            if cast:
                cast_d(n)
            ys.append(_dot(a, wd_bf[n]))
        o_ref[:nr, :] = _pack_bf16_pairs(jnp.concatenate(ys, axis=1) + bd_ref[0, 0])

    for nr in range(MOE_SUB, bm + 1, MOE_SUB):
        @pl.when(jnp.logical_and(jnp.logical_and(live > nr - MOE_SUB, live <= nr), jnp.logical_not(fresh_full)))
        def _(nr=nr):
            run(nr, False)
            if nr < bm:
                o_ref[nr:, :] = jnp.zeros((bm - nr, o_ref.shape[1]), o_ref.dtype)

    @pl.when(fresh_full)
    def _():
        run(bm, True)

    @pl.when(jnp.logical_not(used))
    def _():
        o_ref[...] = jnp.zeros_like(o_ref)


def _expert_call(tab, buf, w_gu, b_gu, w_down, b_down, layer, bm):
    n_slots, dw = buf.shape
    _, ne, d, f2 = w_gu.shape
    f = f2 // 2
    n_blocks = n_slots // bm
    lw = lambda i, tb: (layer, tb[0, i], 0, 0)
    row = lambda i, tb: (i, 0)
    cw = MXU_COLS
    return pl.pallas_call(
        functools.partial(_expert_kernel, layer=layer),
        grid_spec=pltpu.PrefetchScalarGridSpec(
            num_scalar_prefetch=1,
            grid=(n_blocks,),
            in_specs=[pl.BlockSpec((bm, dw), row),
                      pl.BlockSpec(memory_space=pl.ANY),
                      pl.BlockSpec((1, 1, 1, f2), lw),
                      pl.BlockSpec(memory_space=pl.ANY),
                      pl.BlockSpec((1, 1, 1, d), lw)],
            out_specs=pl.BlockSpec((bm, dw), row),
            scratch_shapes=[pltpu.VMEM((2, d, f2), F32), pltpu.VMEM((2, f, d), F32),
                            pltpu.VMEM((f2 // cw, d, cw), BF16), pltpu.VMEM((d // cw, f, cw), BF16),
                            pltpu.VMEM((bm, f), BF16),
                            pltpu.SemaphoreType.DMA((2,)), pltpu.SemaphoreType.DMA((2,))]),
        out_shape=jax.ShapeDtypeStruct((n_slots, dw), jnp.uint32),
        compiler_params=_cparams(("arbitrary",)),
        name="moe_experts",
    )(tab, buf, w_gu, b_gu.reshape(b_gu.shape[0], ne, 1, f2), w_down, b_down.reshape(b_down.shape[0], ne, 1, d))


def _combine_kernel(x_ref, m_ref, y_ref, gate_ref, gout_ref, o_ref, *, final):
    m = m_ref[0]
    gate = gate_ref[...]
    half = y_ref.shape[2]
    f_lo, f_hi = None, None
    for k in range(TOP_K):
        lo, hi = _unpack_bf16_pairs(y_ref[k])
        gk = gate[:, k:k + 1]
        f_lo = gk * lo if f_lo is None else f_lo + gk * lo
        f_hi = gk * hi if f_hi is None else f_hi + gk * hi
    x_lo = x_ref[:, :half] + m[5:6, :half] * f_lo
    x_hi = x_ref[:, half:] + m[5:6, half:] * f_hi
    if final:
        ms = (jnp.sum(x_lo * x_lo, axis=-1, keepdims=True) + jnp.sum(x_hi * x_hi, axis=-1, keepdims=True))
        r = lax.rsqrt(ms * (0.5 / half) + NORM_EPS)
        x_lo = x_lo * r * gout_ref[:, :half]
        x_hi = x_hi * r * gout_ref[:, half:]
    o_ref[:, :half] = x_lo
    o_ref[:, half:] = x_hi


def _moe_layer(xs, routed, mods, layer, w_gu, b_gu, w_down, b_down, geo, g_out, final):
    n, d = xs.shape
    h, idx, gate, rank, cnt = routed
    assert h.shape[0] == n
    ne = cnt.shape[0]
    n_tiles = n // TM
    grp = functools.partial(_tile_group, geo=geo)
    const2 = lambda t: (0, 0)

    bm = MOE_BM
    n_rows = n * TOP_K
    n_blocks = -(-n_rows // bm) + ne
    n_slots = n_blocks * bm
    nbp = -(-n_blocks // LANES) * LANES
    slot_t = max(w for w in range(LANES, SLOT_T_MAX + 1, LANES) if n % w == 0)
    dest, btab = pl.pallas_call(
        functools.partial(_slot_kernel, bm=bm),
        grid=(n // slot_t,),
        in_specs=[pl.BlockSpec((TOP_K, slot_t), lambda t: (0, t)),
                  pl.BlockSpec((TOP_K, slot_t), lambda t: (0, t)),
                  pl.BlockSpec((ne, 1), const2)],
        out_specs=[pl.BlockSpec((TOP_K, slot_t), lambda t: (0, t)),
                   pl.BlockSpec((8, nbp), const2)],
        out_shape=[jax.ShapeDtypeStruct((TOP_K, n), I32),
                   jax.ShapeDtypeStruct((8, nbp), I32)],
        compiler_params=_cparams(("arbitrary",)),
        name="moe_slots",
    )(idx, rank, cnt)

    buf = _sc_scatter_rows(h, dest, n_slots)
    y = _expert_call(btab, buf, w_gu, b_gu, w_down, b_down, layer, bm)
    dy = d // 2
    yk = _sc_gather_rows(y, dest.reshape(-1)).reshape(TOP_K, n, dy)
    return pl.pallas_call(
        functools.partial(_combine_kernel, final=final),
        grid=(n_tiles,),
        in_specs=[pl.BlockSpec((TM, d), lambda t: (t, 0)),
                  pl.BlockSpec((1, 6, d), lambda t: (grp(t), 0, 0)),
                  pl.BlockSpec((TOP_K, TM, dy), lambda t: (0, t, 0)),
                  pl.BlockSpec((TM, SUBLANES), lambda t: (t, 0)),
                  pl.BlockSpec((1, d), const2)],
        out_specs=pl.BlockSpec((TM, d), lambda t: (t, 0)),
        out_shape=jax.ShapeDtypeStruct((n, d), F32),
        compiler_params=_cparams(("parallel",)),
        name="moe_combine",
    )(xs, mods, yk, gate, g_out.reshape(1, d))


def kernel(x, c, ctx, c_ctx, ada_w, ada_b, norm_mix, norm_ffn, norm_out, a_w_in, a_g_v, a_w_s, a_b_s, a_w_out, b_w_in, b_w_a2, b_b_a, b_g_o, b_w_out, c_w_qkv, c_rpb, c_w_out, moe_w_router, moe_b_router, moe_w_gu, moe_b_gu, moe_w_down, moe_b_down):
    b, seq, d = x.shape
    lctx = ctx.shape[1]
    depth = ada_w.shape[0]
    assert (b * lctx) % TM == 0 and seq % TM == 0 and lctx % GLA_TB == 0 and seq % GLA_TB == 0
    assert seq % (NA_QROWS * GRID_W) == 0 and (b * lctx) % (NA_QROWS * GRID_W) == 0 and lctx % 256 == 0
    assert seq // GRID_W >= NA_WROWS + NA_QROWS and b + 1 <= SUBLANES
    geo = (b * lctx // TM, seq // TM, b)
    dims = (b, seq, lctx)

    cond_t = jnp.zeros((d, SUBLANES), F32).at[:, :b].set(c.T).at[:, b].set(c_ctx)
    mods_all = _adaln(cond_t, b + 1, ada_w, ada_b)[:, :b + 1].reshape(depth, b + 1, 6, d)

    pair = (ctx.reshape(b * lctx, d), x.reshape(b * seq, d))
    n_tiles_all = (b * lctx + b * seq) // TM
    ctx_tiles = geo[0]
    has_ctx = True
    for i in range(depth):
        kind, j = i % N_MIXERS, i // N_MIXERS
        ctx_later = any(kk % N_MIXERS != 0 for kk in range(i + 1, depth))
        mods = mods_all[i]
        route_w = (norm_ffn[i], moe_w_router[i], moe_b_router[i])
        if i == 0 and not (kind == 0 and ctx_later):
            xs = jnp.concatenate(pair, axis=0)
        if kind == 0:
            keep_ctx = has_ctx and ctx_later
            skip = 0 if keep_ctx or not has_ctx else ctx_tiles
            geo_i = geo if has_ctx else (0, geo[1], b)
            src = pair if i == 0 and keep_ctx else xs
            n_src = n_tiles_all if i == 0 else xs.shape[0] // TM
            xs, routed = _gmlp_mixer(src, mods, norm_mix[i], a_w_in[j], a_g_v[j], a_w_s[j], a_b_s[j], a_w_out[j],
                                     skip, n_src - skip, geo_i, route_w)
            has_ctx = keep_ctx
        elif kind == 1:
            assert has_ctx
            if not ctx_later:
                raise NotImplementedError("GLA mixer whose context rows are dropped before the FFN")
            xs, routed = _gla_mixer(xs, mods, norm_mix[i], b_w_in[j], b_w_a2[j], b_b_a[j], b_g_o[j], b_w_out[j],
                                    geo, dims, route_w)
        else:
            assert has_ctx
            if ctx_later:
                raise NotImplementedError("context output of the neighbourhood mixer")
            xs, routed = _na_mixer(xs, mods, norm_mix[i], c_w_qkv[j], c_rpb[j], c_w_out[j], geo, dims, route_w)
            has_ctx = False
        geo_i = geo if has_ctx else (0, geo[1], b)
        xs = _moe_layer(xs, routed, mods, i, moe_w_gu, moe_b_gu, moe_w_down, moe_b_down, geo_i, norm_out,
                        i == depth - 1)
    if has_ctx:
        xs = xs[b * lctx:]
    return xs.reshape(b, seq, d)
```

```python
import functools

import numpy as np
import jax
import jax.numpy as jnp
from jax import lax
from jax.experimental import pallas as pl
from jax.experimental.pallas import tpu as pltpu
from jax.experimental.pallas import tpu_sc as plsc

F32 = jnp.float32
BF16 = jnp.bfloat16
I32 = jnp.int32

LANES = 128
SUBLANES = 8
MXU_COLS = 256

NORM_EPS = 1e-6
GRID_W = 64
N_MIXERS = 3

CHUNK_A = 128
A_GROUPS = 8
GMLP_PART = 256
GLA_HEADS = 4
GLA_RANK = 16
GLA_TAU = 16.0
GLA_CHUNK = 128
ROPE_BASE = 10000.0
NA_HEADS = 16
NA_KH = 8
NA_KW = 16
NEG_INF = -1e30
NA_QROWS = 4
NA_WROWS = 12
NA_PAIRS_PER_STEP = 4
TOP_K = 4
SWIGLU_LIMIT = 7.0
SWIGLU_ALPHA = 1.702
MOE_BM = 1024
MOE_SUB = 256

TM = 512
GLA_TB = 256
GLA_PROJ_PART = 256
VMEM_LIMIT = 56 * 1024 * 1024


def _cparams(sem):
    return pltpu.CompilerParams(dimension_semantics=sem, vmem_limit_bytes=VMEM_LIMIT)


def _dot(a, b):
    return jnp.dot(a, b, preferred_element_type=F32)


def _dot_nt(a, b):
    return lax.dot_general(a, b, (((1,), (1,)), ((), ())), preferred_element_type=F32)


def _dot_tn(a, b):
    return lax.dot_general(a, b, (((0,), (0,)), ((), ())), preferred_element_type=F32)


def _rms(x, g):
    return x * lax.rsqrt(jnp.mean(x * x, axis=-1, keepdims=True) + NORM_EPS) * g


def _modulate(x, g, shift, scale):
    return _rms(x, g) * (1.0 + scale) + shift


ADA_COL_BLOCKS = 4


def _ada_kernel(st_ref, w_ref, b_ref, o_ref, *, nrows):
    st = st_ref[...]
    st = st * jax.nn.sigmoid(st)
    w = w_ref[0]
    rows = [jnp.sum(w * st[:, r:r + 1], axis=0, keepdims=True) for r in range(nrows)]
    rows += [jnp.zeros_like(rows[0])] * (o_ref.shape[1] - nrows)
    o_ref[0] = jnp.concatenate(rows, axis=0) + b_ref[0]


def _adaln(cond_t, nrows, ada_w, ada_b):
    depth, d, n6 = ada_w.shape
    bn = n6 // ADA_COL_BLOCKS
    return pl.pallas_call(
        functools.partial(_ada_kernel, nrows=nrows),
        grid=(depth, ADA_COL_BLOCKS),
        in_specs=[pl.BlockSpec((d, SUBLANES), lambda i, j: (0, 0)),
                  pl.BlockSpec((1, d, bn), lambda i, j: (i, 0, j)),
                  pl.BlockSpec((1, 1, bn), lambda i, j: (i, 0, j))],
        out_specs=pl.BlockSpec((1, SUBLANES, bn), lambda i, j: (i, 0, j)),
        out_shape=jax.ShapeDtypeStruct((depth, SUBLANES, n6), F32),
        compiler_params=_cparams(("parallel", "parallel")),
        name="adaln",
    )(cond_t, ada_w, ada_b.reshape(depth, 1, n6))


def _gelu(z):
    return 0.5 * z * (1.0 + lax.erf(z * np.float32(np.sqrt(0.5))))


def _gmlp_kernel(*refs, ctx_tiles):
    if ctx_tiles:
        xc_ref, xl_ref, *refs = refs
        *refs, x_ref = refs

        @pl.when(pl.program_id(0) < ctx_tiles)
        def _():
            x_ref[...] = xc_ref[...]

        @pl.when(pl.program_id(0) >= ctx_tiles)
        def _():
            x_ref[...] = xl_ref[...]
    else:
        x_ref, *refs = refs
    m_ref, g_ref, win_ref, gv_ref, ws_ref, bs_ref, wout_ref, *route_refs = refs
    route_in, o_ref, route_out = route_refs[:3], route_refs[3], route_refs[4:]
    m = m_ref[0]
    a = gv_ref.shape[1]
    gw = a // A_GROUPS
    npart = x_ref.shape[0] // GMLP_PART
    parts = [slice(p * GMLP_PART, (p + 1) * GMLP_PART) for p in range(npart)]
    z = [_dot(_modulate(x_ref[rs, :], g_ref[...], m[0:1], m[1:2]).astype(BF16), win_ref[...]) for rs in parts]
    for rs, zp in zip(parts, z):
        zp = _gelu(zp)
        u = zp[:, :a]
        v = _rms(zp[:, a:], gv_ref[...]).astype(BF16)
        rows = []
        for c in range(GMLP_PART // CHUNK_A):
            cols = [_dot(ws_ref[g], v[c * CHUNK_A:(c + 1) * CHUNK_A, g * gw:(g + 1) * gw])
                    for g in range(A_GROUPS)]
            rows.append(jnp.concatenate(cols, axis=1) + bs_ref[...])
        s = jnp.concatenate(rows, axis=0)
        y = _dot((u * s).astype(BF16), wout_ref[...])
        o_ref[rs, :] = x_ref[rs, :] + m[2:3] * y
    _route_tail(o_ref[...], m, *route_in, *route_out)


def _gmlp_mixer(xs, mods, g, w_in, g_v, w_s, b_s, w_out, tile0, n_tiles, geo, route_w):
    split = isinstance(xs, tuple)
    d = xs[0].shape[1] if split else xs.shape[1]
    a = g_v.shape[0]
    gw = a // A_GROUPS
    bias = jnp.repeat(b_s.T, gw, axis=1)
    grp = functools.partial(_tile_group, geo=geo, tile0=tile0)
    const2 = lambda t: (0, 0)
    r_in, r_out, r_shape, r_scratch = _route_specs(n_tiles * TM, d, route_w[1].shape[1])
    if split:
        assert tile0 == 0
        ctx_tiles = xs[0].shape[0] // TM
        x_specs = [pl.BlockSpec((TM, d), lambda t: (jnp.minimum(t, ctx_tiles - 1), 0)),
                   pl.BlockSpec((TM, d), lambda t: (jnp.maximum(t - ctx_tiles, 0), 0))]
        x_args, x_scratch = list(xs), [pltpu.VMEM((TM, d), F32)]
    else:
        ctx_tiles = 0
        x_specs, x_args, x_scratch = [pl.BlockSpec((TM, d), lambda t: (t + tile0, 0))], [xs], []
    out = pl.pallas_call(
        functools.partial(_gmlp_kernel, ctx_tiles=ctx_tiles),
        grid=(n_tiles,),
        in_specs=x_specs + [
                  pl.BlockSpec((1, 6, d), lambda t: (grp(t), 0, 0)),
                  pl.BlockSpec((1, d), const2),
                  pl.BlockSpec((d, 2 * a), const2),
                  pl.BlockSpec((1, a), const2),
                  pl.BlockSpec((A_GROUPS, CHUNK_A, CHUNK_A), lambda t: (0, 0, 0)),
                  pl.BlockSpec((CHUNK_A, a), const2),
                  pl.BlockSpec((a, d), const2)] + r_in,
        out_specs=[pl.BlockSpec((TM, d), lambda t: (t, 0))] + r_out,
        out_shape=[jax.ShapeDtypeStruct((n_tiles * TM, d), F32)] + r_shape,
        scratch_shapes=r_scratch + x_scratch,
        compiler_params=_cparams(("arbitrary",)),
        name="gmlp_mixer",
    )(*x_args, mods, g.reshape(1, d), w_in.astype(BF16), g_v.reshape(1, a), w_s.astype(BF16),
      bias, w_out.astype(BF16), *_route_args(route_w, d))
    return out[0], out[1:]


def _tile_group(t, geo, tile0=0):
    n_ctx_tiles, tiles_per_batch, nb = geo
    tt = t + tile0
    return jnp.where(tt < n_ctx_tiles, nb, (tt - n_ctx_tiles) // tiles_per_batch)


def _dot_f32(tri_bf16, x):
    hi = x.astype(BF16)
    r1 = x - hi.astype(F32)
    mid = r1.astype(BF16)
    lo = (r1 - mid.astype(F32)).astype(BF16)
    return _dot(tri_bf16, hi) + _dot(tri_bf16, mid) + _dot(tri_bf16, lo)


def _gla_proj_kernel(x_ref, m_ref, g_ref, win_ref, wa_ref, ba_ref, cos_ref, sin_ref,
                     q_ref, k_ref, v_ref, go_ref, la_ref, wq_ref, wk_ref, wv_ref, wg_ref, wr_ref):
    kdim = wq_ref.shape[1]
    vdim = wv_ref.shape[1]

    @pl.when(pl.program_id(0) == 0)
    def _():
        c0 = 0
        for dst, width in ((wq_ref, kdim), (wk_ref, kdim), (wv_ref, vdim), (wg_ref, vdim)):
            dst[...] = win_ref[:, c0:c0 + width].astype(BF16)
            c0 += width
        wr_ref[...] = jnp.zeros_like(wr_ref)
        wr_ref[:, :2 * GLA_RANK] = win_ref[:, c0:c0 + 2 * GLA_RANK].astype(BF16)

    m = m_ref[0]
    hk = kdim // GLA_HEADS
    nf = hk // 4
    npart = x_ref.shape[0] // GLA_PROJ_PART
    parts = [slice(p * GLA_PROJ_PART, (p + 1) * GLA_PROJ_PART) for p in range(npart)]
    zs, qs, ks = [], [], []
    for rs in parts:
        h = _modulate(x_ref[rs, :], g_ref[...], m[0:1], m[1:2]).astype(BF16)
        zs.append(_dot(_dot(h, wr_ref[...]).astype(BF16), wa_ref[...]) + ba_ref[...])
        qs.append(_dot(h, wq_ref[...]) * np.float32(hk ** -0.5))
        ks.append(_dot(h, wk_ref[...]))
        v_ref[rs, :] = _dot(h, wv_ref[...]).astype(BF16)
        go_ref[rs, :] = _dot(h, wg_ref[...])
    for rs, z in zip(parts, zs):
        la_ref[rs, :] = jax.nn.log_sigmoid(z) * np.float32(1.0 / GLA_TAU)

    lane = lax.broadcasted_iota(I32, (GLA_PROJ_PART, kdim), 1)
    first = (lane % (2 * nf)) < nf
    for rs, q, k in zip(parts, qs, ks):
        cos = jnp.concatenate([cos_ref[rs, :]] * GLA_HEADS, axis=1)
        sin = jnp.concatenate([sin_ref[rs, :]] * GLA_HEADS, axis=1)

        def rope(t):
            up = pltpu.roll(t, kdim - nf, 1)
            dn = pltpu.roll(t, nf, 1)
            return t * cos + jnp.where(first, up, dn) * sin

        q_ref[rs, :] = rope(q)
        k_ref[rs, :] = rope(k)


def _gla_decays(q_ref, k_ref, la_ref, n, rev):
    c = GLA_CHUNK
    ri = lax.broadcasted_iota(I32, (c, c), 0)
    ci = lax.broadcasted_iota(I32, (c, c), 1)
    keep = (ci >= ri) if rev else (ci <= ri)
    rows = slice(n * c, (n + 1) * c)
    cum = _dot_f32(keep.astype(BF16), la_ref[rows, :])
    last = cum[0:1] if rev else cum[c - 1:c]
    q = q_ref[rows, :]
    k = k_ref[rows, :]
    mid = cum[c // 2:c // 2 + 1]
    return dict(rows=rows, keep=keep, dec=jnp.exp(last),
                q_dec=(q * jnp.exp(cum)).astype(BF16),
                q_att=(q * jnp.exp(cum - mid)).astype(BF16),
                k_att=(k * jnp.exp(mid - cum)).astype(BF16),
                k_end=(k * jnp.exp(last - cum)).astype(BF16))


def _gla_scan_kernel(qf_ref, kf_ref, vf_ref, laf_ref, qb_ref, kb_ref, vb_ref, lab_ref,
                     of_ref, ob_ref, stf_ref, stb_ref):
    s = pl.program_id(1)

    @pl.when(s == 0)
    def _():
        stf_ref[...] = jnp.zeros_like(stf_ref)
        stb_ref[...] = jnp.zeros_like(stb_ref)

    nch = qf_ref.shape[0] // GLA_CHUNK
    hk = qf_ref.shape[1] // GLA_HEADS
    hv = vf_ref.shape[1] // GLA_HEADS
    dirs = ((qf_ref, kf_ref, vf_ref, laf_ref, of_ref, stf_ref, False),
            (qb_ref, kb_ref, vb_ref, lab_ref, ob_ref, stb_ref, True))
    heads = [(slice(h * hk, (h + 1) * hk), slice(h * hv, (h + 1) * hv)) for h in range(GLA_HEADS)]
    for n in range(nch):
        w = [_gla_decays(q_ref, k_ref, la_ref, nch - 1 - n if rev else n, rev)
             for q_ref, k_ref, _, la_ref, _, _, rev in dirs]
        att = [[jnp.where(w[d]["keep"], _dot_nt(w[d]["q_att"][:, ks], w[d]["k_att"][:, ks]), 0.0).astype(BF16)
                for ks, _ in heads] for d in range(2)]
        for d, (_, _, v_ref, _, o_ref, st_ref, _) in enumerate(dirs):
            for h, (ks, vs) in enumerate(heads):
                o_ref[w[d]["rows"], vs] = (_dot(att[d][h], v_ref[w[d]["rows"], vs])
                                           + _dot_nt(w[d]["q_dec"][:, ks], st_ref[h].astype(BF16)))
        for d, (_, _, v_ref, _, _, st_ref, _) in enumerate(dirs):
            for h, (ks, vs) in enumerate(heads):
                st_ref[h] = (st_ref[h] * w[d]["dec"][:, ks]
                             + _dot_tn(v_ref[w[d]["rows"], vs], w[d]["k_end"][:, ks]))


def _gla_out_kernel(x_ref, m_ref, of_ref, ob_ref, go_ref, gn_ref, wout_ref, *route_refs):
    route_in, o_ref, route_out = route_refs[:3], route_refs[3], route_refs[4:]
    x = x_ref[...]
    m = m_ref[0]
    o = of_ref[...] + ob_ref[...]
    hv = gn_ref.shape[1]
    parts = [_rms(o[:, h * hv:(h + 1) * hv], gn_ref[...]) for h in range(GLA_HEADS)]
    o = jnp.concatenate(parts, axis=1)
    gate = go_ref[...]
    y = _dot((o * (gate * jax.nn.sigmoid(gate))).astype(BF16), wout_ref[...])
    x_new = x + m[2:3] * y
    o_ref[...] = x_new
    _route_tail(x_new, m, *route_in, *route_out)


def _gla_mixer(xs, mods, g, w_in, w_a2, b_a, g_o, w_out, geo, dims, route_w):
    b, seq, lctx = dims
    nt, d = xs.shape
    n_tiles = nt // TM
    kdim = w_a2.shape[2]
    vdim = g_o.shape[0] * GLA_HEADS
    hk = kdim // GLA_HEADS
    nf = hk // 4
    assert w_in.shape[1] == 2 * kdim + 2 * vdim + 2 * GLA_RANK and 2 * GLA_RANK <= LANES
    wa = jnp.zeros((LANES, 2 * kdim), F32)
    wa = wa.at[:GLA_RANK, :kdim].set(w_a2[0]).at[GLA_RANK:2 * GLA_RANK, kdim:].set(w_a2[1]).astype(BF16)
    ba = b_a.reshape(1, 2 * kdim)

    tpos = np.arange(seq)
    freqs = np.power(np.float32(ROPE_BASE), -np.arange(nf, dtype=np.float32) / np.float32(nf)).astype(np.float32)
    ar = (tpos // GRID_W).astype(np.float32)[:, None] * freqs
    ac = (tpos % GRID_W).astype(np.float32)[:, None] * freqs
    cos = np.concatenate([np.cos(ar), np.cos(ar), np.cos(ac), np.cos(ac)], axis=1)
    sin = np.concatenate([-np.sin(ar), np.sin(ar), -np.sin(ac), np.sin(ac)], axis=1)
    cos = jnp.asarray(np.concatenate([np.ones((TM, hk), np.float32), cos], axis=0).astype(np.float32))
    sin = jnp.asarray(np.concatenate([np.zeros((TM, hk), np.float32), sin], axis=0).astype(np.float32))

    n_ctx_tiles, tiles_per_batch, _ = geo
    grp = functools.partial(_tile_group, geo=geo)

    def rope_blk(t):
        return (jnp.where(t < n_ctx_tiles, 0, 1 + (t - n_ctx_tiles) % tiles_per_batch), 0)

    const2 = lambda t: (0, 0)
    row = lambda t: (t, 0)
    q, k, v, go, la = pl.pallas_call(
        _gla_proj_kernel,
        grid=(n_tiles,),
        in_specs=[pl.BlockSpec((TM, d), row),
                  pl.BlockSpec((1, 6, d), lambda t: (grp(t), 0, 0)),
                  pl.BlockSpec((1, d), const2),
                  _resident(w_in.shape), pl.BlockSpec((LANES, 2 * kdim), const2),
                  pl.BlockSpec((1, 2 * kdim), const2),
                  pl.BlockSpec((TM, hk), rope_blk), pl.BlockSpec((TM, hk), rope_blk)],
        out_specs=[pl.BlockSpec((TM, kdim), row), pl.BlockSpec((TM, kdim), row),
                   pl.BlockSpec((TM, vdim), row), pl.BlockSpec((TM, vdim), row),
                   pl.BlockSpec((TM, 2 * kdim), row)],
        out_shape=[jax.ShapeDtypeStruct((nt, kdim), F32), jax.ShapeDtypeStruct((nt, kdim), F32),
                   jax.ShapeDtypeStruct((nt, vdim), BF16), jax.ShapeDtypeStruct((nt, vdim), F32),
                   jax.ShapeDtypeStruct((nt, 2 * kdim), F32)],
        scratch_shapes=[pltpu.VMEM((d, kdim), BF16), pltpu.VMEM((d, kdim), BF16),
                        pltpu.VMEM((d, vdim), BF16), pltpu.VMEM((d, vdim), BF16), pltpu.VMEM((d, LANES), BF16)],
        compiler_params=_cparams(("arbitrary",)),
        name="gla_proj",
    )(xs, mods, g.reshape(1, d), w_in, wa, ba, cos, sin)

    tb = GLA_TB
    ctx_steps = lctx // tb
    lat_steps = seq // tb
    steps = ctx_steps + lat_steps

    def blk(bi, s, rev):
        if rev:
            cs, ls = ctx_steps - 1 - s, lat_steps - 1 - (s - ctx_steps)
        else:
            cs, ls = s, s - ctx_steps
        return jnp.where(s < ctx_steps, bi * ctx_steps + cs, b * ctx_steps + bi * lat_steps + ls)

    def dir_specs(rev):
        row_blk = lambda bi, s: (blk(bi, s, rev), 0)
        return [pl.BlockSpec((tb, kdim), row_blk), pl.BlockSpec((tb, kdim), row_blk),
                pl.BlockSpec((tb, vdim), row_blk),
                pl.BlockSpec((tb, kdim), lambda bi, s: (blk(bi, s, rev), 1 if rev else 0))]

    state = pltpu.VMEM((GLA_HEADS, vdim // GLA_HEADS, hk), F32)
    o_f, o_b = pl.pallas_call(
        _gla_scan_kernel,
        grid=(b, steps),
        in_specs=dir_specs(False) + dir_specs(True),
        out_specs=[pl.BlockSpec((tb, vdim), lambda bi, s: (blk(bi, s, False), 0)),
                   pl.BlockSpec((tb, vdim), lambda bi, s: (blk(bi, s, True), 0))],
        out_shape=[jax.ShapeDtypeStruct((nt, vdim), F32)] * 2,
        scratch_shapes=[state, state],
        compiler_params=_cparams(("parallel", "arbitrary")),
        name="gla_scan",
    )(q, k, v, la, q, k, v, la)

    r_in, r_out, r_shape, r_scratch = _route_specs(nt, d, route_w[1].shape[1])
    out = pl.pallas_call(
        _gla_out_kernel,
        grid=(n_tiles,),
        in_specs=[pl.BlockSpec((TM, d), row),
                  pl.BlockSpec((1, 6, d), lambda t: (grp(t), 0, 0)),
                  pl.BlockSpec((TM, vdim), row), pl.BlockSpec((TM, vdim), row),
                  pl.BlockSpec((TM, vdim), row),
                  pl.BlockSpec((1, vdim // GLA_HEADS), const2),
                  pl.BlockSpec((vdim, d), const2)] + r_in,
        out_specs=[pl.BlockSpec((TM, d), row)] + r_out,
        out_shape=[jax.ShapeDtypeStruct((nt, d), F32)] + r_shape,
        scratch_shapes=r_scratch,
        compiler_params=_cparams(("arbitrary",)),
        name="gla_out",
    )(xs, mods, o_f, o_b, go, g_o.reshape(1, -1), w_out.astype(BF16), *_route_args(route_w, d))
    return out[0], out[1:]


def _cast_on_first_step(pairs):
    @pl.when(pl.program_id(0) == 0)
    def _():
        for src, dst in pairs:
            dst[...] = src[...].astype(BF16)


def _resident(shape):
    return pl.BlockSpec(shape, lambda *_: (0,) * len(shape), pipeline_mode=pl.Buffered(1))


def _na_proj_kernel(x_ref, m_ref, g_ref, w_ref, q_ref, k_ref, v_ref, w_bf):
    _cast_on_first_step([(w_ref, w_bf)])
    x = x_ref[...]
    m = m_ref[0]
    d = x.shape[1]
    h = _modulate(x, g_ref[...], m[0:1], m[1:2]).astype(BF16)
    qkv = _dot(h, w_bf[...])
    hd = d // NA_HEADS
    for p in range(q_ref.shape[0]):
        cs = slice(p * LANES, (p + 1) * LANES)
        q_ref[p] = (qkv[:, cs] * np.float32(hd ** -0.5)).astype(BF16)
        k_ref[p] = qkv[:, d + p * LANES:d + (p + 1) * LANES].astype(BF16)
        v_ref[p] = qkv[:, 2 * d + p * LANES:2 * d + (p + 1) * LANES].astype(BF16)


def _na_attn_kernel(tbl_ref, q_ref, *refs, n_tiles, nwin):
    k_refs, v_refs = refs[:nwin], refs[nwin:2 * nwin]
    kc_ref, vc_ref, bias_ref, o_ref, s_ref, p_ref = refs[2 * nwin:]
    t = pl.program_id(2)
    typ = jnp.where(t == 0, 0, jnp.where(t == n_tiles - 1, 2, 1))
    npair = NA_WROWS // 2
    nlat = NA_WROWS * GRID_W
    hd = LANES // 2
    lane = lax.broadcasted_iota(I32, q_ref.shape[1:], 1)
    npp, tq = q_ref.shape[0], q_ref.shape[1]
    for pp in range(npp):
        q = q_ref[pp]
        k_all = jnp.concatenate([r[pp] for r in k_refs] + [kc_ref[pp]], axis=0)
        zero = jnp.zeros_like(q)
        q2 = jnp.concatenate([jnp.where(lane < hd, q, zero), jnp.where(lane >= hd, q, zero)], axis=0)
        s_ref[pp] = _dot_nt(q2, k_all)
    for pp in range(npp):
        for hh in range(2):
            for dr in range(NA_QROWS):
                rs = slice(hh * tq + dr * GRID_W, hh * tq + (dr + 1) * GRID_W)
                lat = [s_ref[pp, rs, j * LANES:(j + 1) * LANES]
                       + bias_ref[2 * pp + hh, tbl_ref[typ * (NA_QROWS * npair) + dr * npair + j]]
                       for j in range(npair)]
                sb = jnp.concatenate(lat + [s_ref[pp, rs, nlat:]], axis=1)
                e = jnp.exp(sb - jnp.max(sb, axis=1, keepdims=True))
                p_ref[pp, rs, :] = (e * (1.0 / jnp.sum(e, axis=1, keepdims=True))).astype(BF16)
    outs = []
    for pp in range(npp):
        v_all = jnp.concatenate([r[pp] for r in v_refs] + [vc_ref[pp]], axis=0)
        o2 = _dot(p_ref[pp], v_all)
        outs.append(jnp.where(lane < hd, o2[:tq], o2[tq:]))
    o_ref[...] = jnp.concatenate(outs, axis=1).astype(BF16)


def _na_out_kernel(x_ref, m_ref, a_ref, w_ref, *route_refs):
    route_in, o_ref, route_out = route_refs[:3], route_refs[3], route_refs[4:]
    m = m_ref[0]
    x_new = x_ref[...] + m[2:3] * _dot(a_ref[...], w_ref[...])
    o_ref[...] = x_new
    _route_tail(x_new, m, *route_in, *route_out)


def _na_tables(rows):
    npair = NA_WROWS // 2
    tbl = np.zeros((3, NA_QROWS, npair), np.int32)
    for typ, r_base in enumerate((0, NA_QROWS, rows - NA_QROWS)):
        w0 = int(np.clip(r_base - NA_KH // 2, 0, rows - NA_WROWS))
        for dr in range(NA_QROWS):
            r = r_base + dr
            r0 = int(np.clip(r - NA_KH // 2, 0, rows - NA_KH))
            for j in range(npair):
                kr = (w0 + 2 * j, w0 + 2 * j + 1)
                ok = [r0 <= x < r0 + NA_KH for x in kr]
                ri = [x - r + NA_KH - 1 for x in kr]
                if ok[0] and ok[1]:
                    e = ri[0]
                elif ok[0]:
                    e = 16 + ri[0]
                elif ok[1]:
                    e = 32 + ri[1]
                else:
                    e = 63
                tbl[typ, dr, j] = e
    return tbl.reshape(-1)


def _na_bias_kernel(rpb_ref, o_ref, *, nr, nw):
    h = pl.program_id(0)
    c = lax.broadcasted_iota(I32, (GRID_W, GRID_W), 0)
    k = lax.broadcasted_iota(I32, (GRID_W, GRID_W), 1)
    cstart = jnp.clip(c - NA_KW // 2, 0, GRID_W - NA_KW)
    ok = jnp.logical_and(k >= cstart, k < cstart + NA_KW)
    rel = k - c + NA_KW - 1
    neg = jnp.full((GRID_W, GRID_W), NEG_INF, F32)
    cbs = []
    for r in range(nr):
        acc = neg
        for w in range(nw):
            acc = jnp.where(rel == w, rpb_ref[(h * nr + r) * nw + w], acc)
        cbs.append(jnp.where(ok, acc, neg))
    cbs += [neg] * (17 - nr)
    for r in range(16):
        o_ref[0, r] = jnp.concatenate([cbs[r], cbs[r + 1]], axis=1)
        o_ref[0, 16 + r] = jnp.concatenate([cbs[r], neg], axis=1)
        o_ref[0, 32 + r] = jnp.concatenate([neg, cbs[r]], axis=1)
        o_ref[0, 48 + r] = jnp.concatenate([neg, neg], axis=1)


def _na_bias_table(rpb):
    nh, nr, nw = rpb.shape
    return pl.pallas_call(
        functools.partial(_na_bias_kernel, nr=nr, nw=nw),
        grid=(nh,),
        in_specs=[pl.BlockSpec(memory_space=pltpu.SMEM)],
        out_specs=pl.BlockSpec((1, 64, GRID_W, 2 * GRID_W), lambda h: (h, 0, 0, 0)),
        out_shape=jax.ShapeDtypeStruct((nh, 64, GRID_W, 2 * GRID_W), F32),
        compiler_params=_cparams(("parallel",)),
        name="na_bias_table",
    )(rpb.reshape(-1).astype(F32))


def _na_mixer(xs, mods, g, w_qkv, rpb, w_out, geo, dims, route_w):
    b, seq, lctx = dims
    nt, d = xs.shape
    n_tiles_tok = nt // TM
    grp = functools.partial(_tile_group, geo=geo)
    npairs = d // 128
    const2 = lambda t: (0, 0)
    q, k, v = pl.pallas_call(
        _na_proj_kernel,
        grid=(n_tiles_tok,),
        in_specs=[pl.BlockSpec((TM, d), lambda t: (t, 0)),
                  pl.BlockSpec((1, 6, d), lambda t: (grp(t), 0, 0)),
                  pl.BlockSpec((1, d), const2),
                  _resident((d, 3 * d))],
        out_specs=[pl.BlockSpec((npairs, TM, LANES), lambda t: (0, t, 0))] * 3,
        out_shape=[jax.ShapeDtypeStruct((npairs, nt, LANES), BF16)] * 3,
        scratch_shapes=[pltpu.VMEM((d, 3 * d), BF16)],
        compiler_params=_cparams(("arbitrary",)),
        name="na_proj",
    )(xs, mods, g.reshape(1, d), w_qkv)

    rows = seq // GRID_W
    tq = NA_QROWS * GRID_W
    n_tiles = rows // NA_QROWS
    wb = 256
    nwin = NA_WROWS * GRID_W // wb
    lat0 = b * lctx
    tbl = jnp.asarray(_na_tables(rows))
    bias = _na_bias_table(rpb)

    pp = NA_PAIRS_PER_STEP
    assert npairs % pp == 0 and (NA_QROWS * GRID_W) % wb == 0 and (NA_KH // 2 * GRID_W) % wb == 0

    def win(i):
        def f(p, bi, t, tbl_ref):
            w = jnp.clip(t * (NA_QROWS * GRID_W // wb) - NA_KH // 2 * GRID_W // wb, 0, seq // wb - nwin)
            return (p, (lat0 + bi * seq) // wb + w + i, 0)
        return f

    kv_specs = [pl.BlockSpec((pp, wb, 128), win(i)) for i in range(nwin)]
    attn = pl.pallas_call(
        functools.partial(_na_attn_kernel, n_tiles=n_tiles, nwin=nwin),
        grid_spec=pltpu.PrefetchScalarGridSpec(
            num_scalar_prefetch=1,
            grid=(npairs // pp, b, n_tiles),
            in_specs=[pl.BlockSpec((pp, tq, 128), lambda p, bi, t, tr: (p, (lat0 + bi * seq) // tq + t, 0))]
                     + kv_specs + kv_specs
                     + [pl.BlockSpec((pp, lctx, 128), lambda p, bi, t, tr: (p, bi, 0)),
                        pl.BlockSpec((pp, lctx, 128), lambda p, bi, t, tr: (p, bi, 0)),
                        pl.BlockSpec((2 * pp, 64, GRID_W, 2 * GRID_W), lambda p, bi, t, tr: (p, 0, 0, 0),
                                     pipeline_mode=pl.Buffered(1))],
            out_specs=pl.BlockSpec((tq, 128 * pp), lambda p, bi, t, tr: (bi * n_tiles + t, p)),
            scratch_shapes=[pltpu.VMEM((pp, 2 * tq, NA_WROWS * GRID_W + lctx), F32),
                            pltpu.VMEM((pp, 2 * tq, NA_WROWS * GRID_W + lctx), BF16)]),
        out_shape=jax.ShapeDtypeStruct((b * seq, d), BF16),
        compiler_params=_cparams(("parallel", "parallel", "arbitrary")),
        name="na_attn",
    )(tbl, q, *([k] * nwin), *([v] * nwin), k, v, bias)

    n_lat_tiles = b * seq // TM
    tile0 = lat0 // TM
    grp_l = functools.partial(_tile_group, geo=geo, tile0=tile0)
    r_in, r_out, r_shape, r_scratch = _route_specs(b * seq, d, route_w[1].shape[1])
    out = pl.pallas_call(
        _na_out_kernel,
        grid=(n_lat_tiles,),
        in_specs=[pl.BlockSpec((TM, d), lambda t: (t + tile0, 0)),
                  pl.BlockSpec((1, 6, d), lambda t: (grp_l(t), 0, 0)),
                  pl.BlockSpec((TM, d), lambda t: (t, 0)),
                  pl.BlockSpec((d, d), const2)] + r_in,
        out_specs=[pl.BlockSpec((TM, d), lambda t: (t, 0))] + r_out,
        out_shape=[jax.ShapeDtypeStruct((b * seq, d), F32)] + r_shape,
        scratch_shapes=r_scratch,
        compiler_params=_cparams(("arbitrary",)),
        name="na_out",
    )(xs, mods, attn, w_out.astype(BF16), *_route_args(route_w, d))
    return out[0], out[1:]


SC_CORES = 2
SC_SUBCORES = 16
SC_WORKERS = SC_CORES * SC_SUBCORES
SLOT_T_MAX = 2048


def _sc_mesh():
    return plsc.VectorSubcoreMesh(core_axis_name="c", subcore_axis_name="s")


def _sc_chunk(per_worker, max_chunk):
    return max(c for c in range(8, max_chunk + 1, 8) if per_worker % (2 * c) == 0)


def _sc_gather_rows(table, idx):
    dd = table.shape[1]
    bsz = idx.shape[0]
    per_w = bsz // SC_WORKERS
    assert per_w * SC_WORKERS == bsz
    chunk = _sc_chunk(per_w, 64)
    n_chunks = per_w // chunk

    def body(table_hbm, idx_hbm, out_hbm, idx_v, rows0, rows1, g0, g1, w0, w1):
        wid = lax.axis_index("s") * SC_CORES + lax.axis_index("c")
        pltpu.sync_copy(idx_hbm.at[wid], idx_v)
        base = wid * per_w

        def out_rows(j):
            return out_hbm.at[pl.ds(pl.multiple_of(base + j * chunk, 8), chunk)]

        def step(i, carry):
            j0, j1 = 2 * i, 2 * i + 1
            ga = pltpu.async_copy(table_hbm.at[idx_v.at[j0]], rows0, g0)
            gb = pltpu.async_copy(table_hbm.at[idx_v.at[j1]], rows1, g1)
            ga.wait()
            wa = pltpu.async_copy(rows0, out_rows(j0), w0)
            gb.wait()
            wb = pltpu.async_copy(rows1, out_rows(j1), w1)
            wa.wait()
            wb.wait()
            return carry

        lax.fori_loop(0, n_chunks // 2, step, 0)

    return pl.kernel(
        body, out_type=jax.ShapeDtypeStruct((bsz, dd), table.dtype), mesh=_sc_mesh(),
        scratch_types=[pltpu.VMEM((n_chunks, chunk), I32),
                       pltpu.VMEM((chunk, dd), table.dtype), pltpu.VMEM((chunk, dd), table.dtype),
                       pltpu.SemaphoreType.DMA, pltpu.SemaphoreType.DMA,
                       pltpu.SemaphoreType.DMA, pltpu.SemaphoreType.DMA],
        name="sc_gather_rows",
    )(table, idx.reshape(SC_WORKERS, n_chunks, chunk))


def _sc_scatter_rows(rows, dest, n_out):
    n, dd = rows.shape
    kk = dest.shape[0]
    per_w = n // SC_WORKERS
    assert per_w * SC_WORKERS == n
    chunk = _sc_chunk(per_w, 64)
    n_chunks = per_w // chunk
    dest_w = dest.reshape(kk, SC_WORKERS, n_chunks, chunk).transpose(1, 2, 0, 3)
    dest_w = dest_w.reshape(SC_WORKERS, n_chunks * kk, chunk)

    def body(rows_hbm, dest_hbm, out_hbm, idx_v, rows0, rows1, r0, r1, s0, s1):
        wid = lax.axis_index("s") * SC_CORES + lax.axis_index("c")
        pltpu.sync_copy(dest_hbm.at[wid], idx_v)
        base = wid * per_w

        def in_rows(j):
            return rows_hbm.at[pl.ds(pl.multiple_of(base + j * chunk, 8), chunk)]

        def step(i, carry):
            j0, j1 = 2 * i, 2 * i + 1
            ra = pltpu.async_copy(in_rows(j0), rows0, r0)
            rb = pltpu.async_copy(in_rows(j1), rows1, r1)
            ra.wait()
            sa = [pltpu.async_copy(rows0, out_hbm.at[idx_v.at[j0 * kk + k]], s0) for k in range(kk)]
            rb.wait()
            sb = [pltpu.async_copy(rows1, out_hbm.at[idx_v.at[j1 * kk + k]], s1) for k in range(kk)]
            for cp in sa + sb:
                cp.wait()
            return carry

        lax.fori_loop(0, n_chunks // 2, step, 0)

    return pl.kernel(
        body, out_type=jax.ShapeDtypeStruct((n_out, dd), rows.dtype), mesh=_sc_mesh(),
        scratch_types=[pltpu.VMEM((n_chunks * kk, chunk), I32),
                       pltpu.VMEM((chunk, dd), rows.dtype), pltpu.VMEM((chunk, dd), rows.dtype),
                       pltpu.SemaphoreType.DMA, pltpu.SemaphoreType.DMA,
                       pltpu.SemaphoreType.DMA, pltpu.SemaphoreType.DMA],
        name="sc_scatter_rows",
    )(rows, dest_w)


def _prefix_sum_rows(col):
    nr = col.shape[0]
    acc = jnp.broadcast_to(col, (nr, LANES))
    row = lax.broadcasted_iota(I32, (nr, LANES), 0)
    s = 1
    while s < nr:
        acc = acc + jnp.where(row >= s, pltpu.roll(acc, s, 0), 0.0)
        s *= 2
    return acc[:, 0:1]


def _slot_kernel(idx_ref, rank_ref, cnt_ref, dest_ref, btab_ref, *, bm):
    cnt = cnt_ref[...]
    ne = cnt.shape[0]
    padded = jnp.floor((cnt + (bm - 1.0)) * (1.0 / bm)) * bm
    pad_end = _prefix_sum_rows(padded)
    pad_start = pad_end - padded
    idx = idx_ref[...]
    e_iota = lax.broadcasted_iota(I32, (ne, idx.shape[1]), 0)
    starts = [jnp.sum(jnp.where(e_iota == idx[k:k + 1], pad_start, 0.0), axis=0, keepdims=True)
              for k in range(idx.shape[0])]
    dest_ref[...] = jnp.concatenate(starts, axis=0).astype(I32) + rank_ref[...]

    nbp = btab_ref.shape[1]
    bstart = lax.broadcasted_iota(I32, (ne, nbp), 1).astype(F32) * bm
    be = jnp.minimum(jnp.sum((pad_end <= bstart).astype(F32), axis=0, keepdims=True), ne - 1.0)
    mine = lax.broadcasted_iota(I32, (ne, nbp), 0).astype(F32) == be
    pick = lambda col: jnp.sum(jnp.where(mine, col, 0.0), axis=0, keepdims=True)
    live = jnp.clip(pick(pad_start + cnt) - bstart[0:1], 0.0, bm)
    r = lax.broadcasted_iota(I32, (ne, ne), 0)
    c = lax.broadcasted_iota(I32, (ne, ne), 1)
    ends_on_lanes = _dot_f32(jnp.ones((ne, ne), BF16), jnp.where(r == c, pad_end, 0.0))
    nxt_e = jnp.sum((ends_on_lanes <= pad_end).astype(F32), axis=1, keepdims=True)
    nxt_e = jnp.where(nxt_e < ne, nxt_e, -1.0)
    ordinal = _prefix_sum_rows((cnt > 0).astype(F32)) - 1.0
    slot_e = ordinal - 2.0 * jnp.floor(ordinal * 0.5)
    rowi = lax.broadcasted_iota(I32, btab_ref.shape, 0)
    tab = jnp.where(rowi == 0, be, jnp.where(rowi == 1, live, jnp.where(rowi == 2, pick(nxt_e), pick(slot_e))))
    btab_ref[...] = tab.astype(I32)


def _route_specs(n, d, ne):
    const2 = lambda t: (0, 0)
    in_specs = [pl.BlockSpec((1, d), const2), pl.BlockSpec((ne, d), const2), pl.BlockSpec((ne, 1), const2)]
    out_specs = [pl.BlockSpec((TM, d // 2), lambda t: (t, 0)),
                 pl.BlockSpec((TOP_K, TM), lambda t: (0, t)),
                 pl.BlockSpec((TM, SUBLANES), lambda t: (t, 0)),
                 pl.BlockSpec((TOP_K, TM), lambda t: (0, t)),
                 pl.BlockSpec((ne, 1), const2)]
    out_shape = [jax.ShapeDtypeStruct((n, d // 2), jnp.uint32),
                 jax.ShapeDtypeStruct((TOP_K, n), I32),
                 jax.ShapeDtypeStruct((n, SUBLANES), F32),
                 jax.ShapeDtypeStruct((TOP_K, n), I32),
                 jax.ShapeDtypeStruct((ne, 1), F32)]
    return in_specs, out_specs, out_shape, [pltpu.VMEM((ne, 1), F32)]


def _route_args(route_w, d):
    g_ffn, w_router, b_router = route_w
    ne = w_router.shape[1]
    return g_ffn.reshape(1, d), w_router.T.astype(BF16), b_router.reshape(ne, 1)


def _route_tail(x, m, g_ref, wr_ref, br_ref, h_ref, idx_ref, gate_ref, rank_ref, cnt_ref, run_ref):
    t = pl.program_id(0)

    @pl.when(t == 0)
    def _():
        run_ref[...] = jnp.zeros_like(run_ref)

    h = _modulate(x, g_ref[...], m[3:4], m[4:5]).astype(BF16)
    h_ref[...] = _pack_bf16_pairs(h)
    logits = _dot_nt(wr_ref[...], h) + br_ref[...]
    ne, tm = logits.shape
    e_iota = lax.broadcasted_iota(I32, (ne, tm), 0)
    vals, idxs = [], []
    l = logits
    for _ in range(TOP_K):
        mk = jnp.max(l, axis=0, keepdims=True)
        ik = jnp.min(jnp.where(l == mk, e_iota, ne), axis=0, keepdims=True)
        vals.append(mk)
        idxs.append(ik)
        l = jnp.where(e_iota == ik, -jnp.inf, l)
    top_val = jnp.concatenate(vals, axis=0)
    ex = jnp.exp(top_val - vals[0])
    gates = ex / jnp.sum(ex, axis=0, keepdims=True)
    pad = jnp.zeros((gate_ref.shape[1] - TOP_K, tm), F32)
    gate_ref[...] = jnp.concatenate([gates, pad], axis=0).T
    idx_ref[...] = jnp.concatenate(idxs, axis=0)

    hits = [e_iota == ik for ik in idxs]
    cnt = hits[0].astype(F32)
    for hk in hits[1:]:
        cnt = cnt + hk.astype(F32)
    si = lax.broadcasted_iota(I32, (tm, tm), 0)
    ti = lax.broadcasted_iota(I32, (tm, tm), 1)
    before = (si < ti).astype(BF16)
    total = _dot(cnt.astype(BF16), before) + run_ref[...]
    ranks = [jnp.sum(jnp.where(hk, total, 0.0), axis=0, keepdims=True) for hk in hits]
    rank_ref[...] = jnp.concatenate(ranks, axis=0).astype(I32)
    run_ref[...] = run_ref[...] + jnp.sum(cnt, axis=1, keepdims=True)
    cnt_ref[...] = run_ref[...]


def _pack_bf16_pairs(v):
    bits = lax.bitcast_convert_type(v.astype(BF16).astype(F32), jnp.uint32)
    half = bits.shape[1] // 2
    return (bits[:, half:] & jnp.uint32(0xFFFF0000)) | (bits[:, :half] >> 16)


def _unpack_bf16_pairs(w):
    return (lax.bitcast_convert_type(w << 16, F32),
            lax.bitcast_convert_type(w & jnp.uint32(0xFFFF0000), F32))


def _expert_kernel(tab_ref, x_ref, wgu_hbm, bgu_ref, wd_hbm, bd_ref, o_ref,
                   wgu_in, wd_in, wgu_bf, wd_bf, act_ref, sem_gu, sem_d, *, layer):
    blk = pl.program_id(0)
    ch = wgu_bf.shape[2]
    f = act_ref.shape[1]
    nc = f // ch
    expert, live, nxt, wslot = (tab_ref[i, blk] for i in range(4))
    used = live > 0
    fresh = jnp.logical_or(blk == 0, expert != tab_ref[0, jnp.maximum(blk - 1, 0)])

    def fetch(e, slot):
        return (pltpu.make_async_copy(wgu_hbm.at[layer, e], wgu_in.at[slot], sem_gu.at[slot]),
                pltpu.make_async_copy(wd_hbm.at[layer, e], wd_in.at[slot], sem_d.at[slot]))

    @pl.when(jnp.logical_and(used, fresh))
    def _():
        slot = wslot
        mine = fetch(expert, slot)

        @pl.when(blk == 0)
        def _():
            for cp in mine:
                cp.start()

        @pl.when(nxt >= 0)
        def _():
            for cp in fetch(nxt, 1 - slot):
                cp.start()

        for cp in mine:
            cp.wait()

    bm = x_ref.shape[0]
    fresh_full = jnp.logical_and(fresh, live > bm - MOE_SUB)

    def cast_gu(j):
        wgu_bf[j] = wgu_in[wslot, :, j * ch:(j + 1) * ch].astype(BF16)

    def cast_d(j):
        wd_bf[j] = wd_in[wslot, :, j * ch:(j + 1) * ch].astype(BF16)

    @pl.when(jnp.logical_and(jnp.logical_and(used, fresh), jnp.logical_not(fresh_full)))
    def _():
        for j in range(wgu_bf.shape[0]):
            cast_gu(j)
        for j in range(wd_bf.shape[0]):
            cast_d(j)

    def run(nr, cast):
        w = x_ref[:nr, :]
        row = lax.broadcasted_iota(I32, w.shape, 0)
        w = jnp.where(row < live, w, jnp.zeros_like(w))
        lo, hi = _unpack_bf16_pairs(w)
        x = jnp.concatenate([lo.astype(BF16), hi.astype(BF16)], axis=1)
        for c in range(nc):
            c0 = slice(c * ch, (c + 1) * ch)
            c1 = slice(f + c * ch, f + (c + 1) * ch)
            if cast:
                cast_gu(c)
                cast_gu(nc + c)
            glu = _dot(x, wgu_bf[c]) + bgu_ref[0, 0, :, c0]
            lin = _dot(x, wgu_bf[nc + c]) + bgu_ref[0, 0, :, c1]
            glu = jnp.minimum(glu, SWIGLU_LIMIT)
            lin = jnp.clip(lin, -SWIGLU_LIMIT, SWIGLU_LIMIT)
            act_ref[:nr, c0] = (glu * jax.nn.sigmoid(SWIGLU_ALPHA * glu) * (lin + 1.0)).astype(BF16)
        a = act_ref[:nr, :]
        ys = []
        for n in range(wd_bf.shape[0]):
            if cast:
                cast_d(n)
            ys.append(_dot(a, wd_bf[n]))
        o_ref[:nr, :] = _pack_bf16_pairs(jnp.concatenate(ys, axis=1) + bd_ref[0, 0])

    for nr in range(MOE_SUB, bm + 1, MOE_SUB):
        @pl.when(jnp.logical_and(jnp.logical_and(live > nr - MOE_SUB, live <= nr), jnp.logical_not(fresh_full)))
        def _(nr=nr):
            run(nr, False)
            if nr < bm:
                o_ref[nr:, :] = jnp.zeros((bm - nr, o_ref.shape[1]), o_ref.dtype)

    @pl.when(fresh_full)
    def _():
        run(bm, True)

    @pl.when(jnp.logical_not(used))
    def _():
        o_ref[...] = jnp.zeros_like(o_ref)


def _expert_call(tab, buf, w_gu, b_gu, w_down, b_down, layer, bm):
    n_slots, dw = buf.shape
    _, ne, d, f2 = w_gu.shape
    f = f2 // 2
    n_blocks = n_slots // bm
    lw = lambda i, tb: (layer, tb[0, i], 0, 0)
    row = lambda i, tb: (i, 0)
    cw = MXU_COLS
    return pl.pallas_call(
        functools.partial(_expert_kernel, layer=layer),
        grid_spec=pltpu.PrefetchScalarGridSpec(
            num_scalar_prefetch=1,
            grid=(n_blocks,),
            in_specs=[pl.BlockSpec((bm, dw), row),
                      pl.BlockSpec(memory_space=pl.ANY),
                      pl.BlockSpec((1, 1, 1, f2), lw),
                      pl.BlockSpec(memory_space=pl.ANY),
                      pl.BlockSpec((1, 1, 1, d), lw)],
            out_specs=pl.BlockSpec((bm, dw), row),
            scratch_shapes=[pltpu.VMEM((2, d, f2), F32), pltpu.VMEM((2, f, d), F32),
                            pltpu.VMEM((f2 // cw, d, cw), BF16), pltpu.VMEM((d // cw, f, cw), BF16),
                            pltpu.VMEM((bm, f), BF16),
                            pltpu.SemaphoreType.DMA((2,)), pltpu.SemaphoreType.DMA((2,))]),
        out_shape=jax.ShapeDtypeStruct((n_slots, dw), jnp.uint32),
        compiler_params=_cparams(("arbitrary",)),
        name="moe_experts",
    )(tab, buf, w_gu, b_gu.reshape(b_gu.shape[0], ne, 1, f2), w_down, b_down.reshape(b_down.shape[0], ne, 1, d))


def _combine_kernel(x_ref, m_ref, y_ref, gate_ref, gout_ref, o_ref, *, final):
    m = m_ref[0]
    gate = gate_ref[...]
    half = y_ref.shape[2]
    f_lo, f_hi = None, None
    for k in range(TOP_K):
        lo, hi = _unpack_bf16_pairs(y_ref[k])
        gk = gate[:, k:k + 1]
        f_lo = gk * lo if f_lo is None else f_lo + gk * lo
        f_hi = gk * hi if f_hi is None else f_hi + gk * hi
    x_lo = x_ref[:, :half] + m[5:6, :half] * f_lo
    x_hi = x_ref[:, half:] + m[5:6, half:] * f_hi
    if final:
        ms = (jnp.sum(x_lo * x_lo, axis=-1, keepdims=True) + jnp.sum(x_hi * x_hi, axis=-1, keepdims=True))
        r = lax.rsqrt(ms * (0.5 / half) + NORM_EPS)
        x_lo = x_lo * r * gout_ref[:, :half]
        x_hi = x_hi * r * gout_ref[:, half:]
    o_ref[:, :half] = x_lo
    o_ref[:, half:] = x_hi


def _moe_layer(xs, routed, mods, layer, w_gu, b_gu, w_down, b_down, geo, g_out, final):
    n, d = xs.shape
    h, idx, gate, rank, cnt = routed
    assert h.shape[0] == n
    ne = cnt.shape[0]
    n_tiles = n // TM
    grp = functools.partial(_tile_group, geo=geo)
    const2 = lambda t: (0, 0)

    bm = MOE_BM
    n_rows = n * TOP_K
    n_blocks = -(-n_rows // bm) + ne
    n_slots = n_blocks * bm
    nbp = -(-n_blocks // LANES) * LANES
    slot_t = max(w for w in range(LANES, SLOT_T_MAX + 1, LANES) if n % w == 0)
    dest, btab = pl.pallas_call(
        functools.partial(_slot_kernel, bm=bm),
        grid=(n // slot_t,),
        in_specs=[pl.BlockSpec((TOP_K, slot_t), lambda t: (0, t)),
                  pl.BlockSpec((TOP_K, slot_t), lambda t: (0, t)),
                  pl.BlockSpec((ne, 1), const2)],
        out_specs=[pl.BlockSpec((TOP_K, slot_t), lambda t: (0, t)),
                   pl.BlockSpec((8, nbp), const2)],
        out_shape=[jax.ShapeDtypeStruct((TOP_K, n), I32),
                   jax.ShapeDtypeStruct((8, nbp), I32)],
        compiler_params=_cparams(("arbitrary",)),
        name="moe_slots",
    )(idx, rank, cnt)

    buf = _sc_scatter_rows(h, dest, n_slots)
    y = _expert_call(btab, buf, w_gu, b_gu, w_down, b_down, layer, bm)
    dy = d // 2
    yk = _sc_gather_rows(y, dest.reshape(-1)).reshape(TOP_K, n, dy)
    return pl.pallas_call(
        functools.partial(_combine_kernel, final=final),
        grid=(n_tiles,),
        in_specs=[pl.BlockSpec((TM, d), lambda t: (t, 0)),
                  pl.BlockSpec((1, 6, d), lambda t: (grp(t), 0, 0)),
                  pl.BlockSpec((TOP_K, TM, dy), lambda t: (0, t, 0)),
                  pl.BlockSpec((TM, SUBLANES), lambda t: (t, 0)),
                  pl.BlockSpec((1, d), const2)],
        out_specs=pl.BlockSpec((TM, d), lambda t: (t, 0)),
        out_shape=jax.ShapeDtypeStruct((n, d), F32),
        compiler_params=_cparams(("parallel",)),
        name="moe_combine",
    )(xs, mods, yk, gate, g_out.reshape(1, d))


def kernel(x, c, ctx, c_ctx, ada_w, ada_b, norm_mix, norm_ffn, norm_out, a_w_in, a_g_v, a_w_s, a_b_s, a_w_out, b_w_in, b_w_a2, b_b_a, b_g_o, b_w_out, c_w_qkv, c_rpb, c_w_out, moe_w_router, moe_b_router, moe_w_gu, moe_b_gu, moe_w_down, moe_b_down):
    b, seq, d = x.shape
    lctx = ctx.shape[1]
    depth = ada_w.shape[0]
    assert (b * lctx) % TM == 0 and seq % TM == 0 and lctx % GLA_TB == 0 and seq % GLA_TB == 0
    assert seq % (NA_QROWS * GRID_W) == 0 and (b * lctx) % (NA_QROWS * GRID_W) == 0 and lctx % 256 == 0
    assert seq // GRID_W >= NA_WROWS + NA_QROWS and b + 1 <= SUBLANES
    geo = (b * lctx // TM, seq // TM, b)
    dims = (b, seq, lctx)

    cond_t = jnp.zeros((d, SUBLANES), F32).at[:, :b].set(c.T).at[:, b].set(c_ctx)
    mods_all = _adaln(cond_t, b + 1, ada_w, ada_b)[:, :b + 1].reshape(depth, b + 1, 6, d)

    pair = (ctx.reshape(b * lctx, d), x.reshape(b * seq, d))
    n_tiles_all = (b * lctx + b * seq) // TM
    ctx_tiles = geo[0]
    has_ctx = True
    for i in range(depth):
        kind, j = i % N_MIXERS, i // N_MIXERS
        ctx_later = any(kk % N_MIXERS != 0 for kk in range(i + 1, depth))
        mods = mods_all[i]
        route_w = (norm_ffn[i], moe_w_router[i], moe_b_router[i])
        if i == 0 and not (kind == 0 and ctx_later):
            xs = jnp.concatenate(pair, axis=0)
        if kind == 0:
            keep_ctx = has_ctx and ctx_later
            skip = 0 if keep_ctx or not has_ctx else ctx_tiles
            geo_i = geo if has_ctx else (0, geo[1], b)
            src = pair if i == 0 and keep_ctx else xs
            n_src = n_tiles_all if i == 0 else xs.shape[0] // TM
            xs, routed = _gmlp_mixer(src, mods, norm_mix[i], a_w_in[j], a_g_v[j], a_w_s[j], a_b_s[j], a_w_out[j],
                                     skip, n_src - skip, geo_i, route_w)
            has_ctx = keep_ctx
        elif kind == 1:
            assert has_ctx
            if not ctx_later:
                raise NotImplementedError("GLA mixer whose context rows are dropped before the FFN")
            xs, routed = _gla_mixer(xs, mods, norm_mix[i], b_w_in[j], b_w_a2[j], b_b_a[j], b_g_o[j], b_w_out[j],
                                    geo, dims, route_w)
        else:
            assert has_ctx
            if ctx_later:
                raise NotImplementedError("context output of the neighbourhood mixer")
            xs, routed = _na_mixer(xs, mods, norm_mix[i], c_w_qkv[j], c_rpb[j], c_w_out[j], geo, dims, route_w)
            has_ctx = False
        geo_i = geo if has_ctx else (0, geo[1], b)
        xs = _moe_layer(xs, routed, mods, i, moe_w_gu, moe_b_gu, moe_w_down, moe_b_down, geo_i, norm_out,
                        i == depth - 1)
    if has_ctx:
        xs = xs[b * lctx:]
    return xs.reshape(b, seq, d)
```

```python
import functools

import numpy as np
import jax
import jax.numpy as jnp
from jax import lax
from jax.experimental import pallas as pl
from jax.experimental.pallas import tpu as pltpu
from jax.experimental.pallas import tpu_sc as plsc

F32 = jnp.float32
BF16 = jnp.bfloat16
I32 = jnp.int32

LANES = 128
SUBLANES = 8
MXU_COLS = 256

NORM_EPS = 1e-6
GRID_W = 64
N_MIXERS = 3

CHUNK_A = 128
A_GROUPS = 8
GMLP_PART = 256
GLA_HEADS = 4
GLA_RANK = 16
GLA_TAU = 16.0
GLA_CHUNK = 128
ROPE_BASE = 10000.0
NA_HEADS = 16
NA_KH = 8
NA_KW = 16
NEG_INF = -1e30
NA_QROWS = 4
NA_WROWS = 12
NA_PAIRS_PER_STEP = 4
TOP_K = 4
SWIGLU_LIMIT = 7.0
SWIGLU_ALPHA = 1.702
MOE_BM = 1024
MOE_SUB = 128

TM = 512
GLA_TB = 256
GLA_PROJ_PART = 256
VMEM_LIMIT = 56 * 1024 * 1024


def _cparams(sem):
    return pltpu.CompilerParams(dimension_semantics=sem, vmem_limit_bytes=VMEM_LIMIT)


def _dot(a, b):
    return jnp.dot(a, b, preferred_element_type=F32)


def _dot_nt(a, b):
    return lax.dot_general(a, b, (((1,), (1,)), ((), ())), preferred_element_type=F32)


def _dot_tn(a, b):
    return lax.dot_general(a, b, (((0,), (0,)), ((), ())), preferred_element_type=F32)


def _rms(x, g):
    return x * lax.rsqrt(jnp.mean(x * x, axis=-1, keepdims=True) + NORM_EPS) * g


def _modulate(x, g, shift, scale):
    return _rms(x, g) * (1.0 + scale) + shift


ADA_COL_BLOCKS = 4


def _ada_kernel(st_ref, w_ref, b_ref, o_ref, *, nrows):
    st = st_ref[...]
    st = st * jax.nn.sigmoid(st)
    w = w_ref[0]
    rows = [jnp.sum(w * st[:, r:r + 1], axis=0, keepdims=True) for r in range(nrows)]
    rows += [jnp.zeros_like(rows[0])] * (o_ref.shape[1] - nrows)
    o_ref[0] = jnp.concatenate(rows, axis=0) + b_ref[0]


def _adaln(cond_t, nrows, ada_w, ada_b):
    depth, d, n6 = ada_w.shape
    bn = n6 // ADA_COL_BLOCKS
    return pl.pallas_call(
        functools.partial(_ada_kernel, nrows=nrows),
        grid=(depth, ADA_COL_BLOCKS),
        in_specs=[pl.BlockSpec((d, SUBLANES), lambda i, j: (0, 0)),
                  pl.BlockSpec((1, d, bn), lambda i, j: (i, 0, j)),
                  pl.BlockSpec((1, 1, bn), lambda i, j: (i, 0, j))],
        out_specs=pl.BlockSpec((1, SUBLANES, bn), lambda i, j: (i, 0, j)),
        out_shape=jax.ShapeDtypeStruct((depth, SUBLANES, n6), F32),
        compiler_params=_cparams(("parallel", "parallel")),
        name="adaln",
    )(cond_t, ada_w, ada_b.reshape(depth, 1, n6))


def _gelu(z):
    return 0.5 * z * (1.0 + lax.erf(z * np.float32(np.sqrt(0.5))))


def _gmlp_kernel(*refs, ctx_tiles):
    if ctx_tiles:
        xc_ref, xl_ref, *refs = refs
        *refs, x_ref = refs

        @pl.when(pl.program_id(0) < ctx_tiles)
        def _():
            x_ref[...] = xc_ref[...]

        @pl.when(pl.program_id(0) >= ctx_tiles)
        def _():
            x_ref[...] = xl_ref[...]
    else:
        x_ref, *refs = refs
    m_ref, g_ref, win_ref, gv_ref, ws_ref, bs_ref, wout_ref, *route_refs = refs
    route_in, o_ref, route_out = route_refs[:3], route_refs[3], route_refs[4:]
    m = m_ref[0]
    a = gv_ref.shape[1]
    gw = a // A_GROUPS
    npart = x_ref.shape[0] // GMLP_PART
    parts = [slice(p * GMLP_PART, (p + 1) * GMLP_PART) for p in range(npart)]
    z = [_dot(_modulate(x_ref[rs, :], g_ref[...], m[0:1], m[1:2]).astype(BF16), win_ref[...]) for rs in parts]
    for rs, zp in zip(parts, z):
        zp = _gelu(zp)
        u = zp[:, :a]
        v = _rms(zp[:, a:], gv_ref[...]).astype(BF16)
        rows = []
        for c in range(GMLP_PART // CHUNK_A):
            cols = [_dot(ws_ref[g], v[c * CHUNK_A:(c + 1) * CHUNK_A, g * gw:(g + 1) * gw])
                    for g in range(A_GROUPS)]
            rows.append(jnp.concatenate(cols, axis=1) + bs_ref[...])
        s = jnp.concatenate(rows, axis=0)
        y = _dot((u * s).astype(BF16), wout_ref[...])
        o_ref[rs, :] = x_ref[rs, :] + m[2:3] * y
    _route_tail(o_ref[...], m, *route_in, *route_out)


def _gmlp_mixer(xs, mods, g, w_in, g_v, w_s, b_s, w_out, tile0, n_tiles, geo, route_w):
    split = isinstance(xs, tuple)
    d = xs[0].shape[1] if split else xs.shape[1]
    a = g_v.shape[0]
    gw = a // A_GROUPS
    bias = jnp.repeat(b_s.T, gw, axis=1)
    grp = functools.partial(_tile_group, geo=geo, tile0=tile0)
    const2 = lambda t: (0, 0)
    r_in, r_out, r_shape, r_scratch = _route_specs(n_tiles * TM, d, route_w[1].shape[1])
    if split:
        assert tile0 == 0
        ctx_tiles = xs[0].shape[0] // TM
        x_specs = [pl.BlockSpec((TM, d), lambda t: (jnp.minimum(t, ctx_tiles - 1), 0)),
                   pl.BlockSpec((TM, d), lambda t: (jnp.maximum(t - ctx_tiles, 0), 0))]
        x_args, x_scratch = list(xs), [pltpu.VMEM((TM, d), F32)]
    else:
        ctx_tiles = 0
        x_specs, x_args, x_scratch = [pl.BlockSpec((TM, d), lambda t: (t + tile0, 0))], [xs], []
    out = pl.pallas_call(
        functools.partial(_gmlp_kernel, ctx_tiles=ctx_tiles),
        grid=(n_tiles,),
        in_specs=x_specs + [
                  pl.BlockSpec((1, 6, d), lambda t: (grp(t), 0, 0)),
                  pl.BlockSpec((1, d), const2),
                  pl.BlockSpec((d, 2 * a), const2),
                  pl.BlockSpec((1, a), const2),
                  pl.BlockSpec((A_GROUPS, CHUNK_A, CHUNK_A), lambda t: (0, 0, 0)),
                  pl.BlockSpec((CHUNK_A, a), const2),
                  pl.BlockSpec((a, d), const2)] + r_in,
        out_specs=[pl.BlockSpec((TM, d), lambda t: (t, 0))] + r_out,
        out_shape=[jax.ShapeDtypeStruct((n_tiles * TM, d), F32)] + r_shape,
        scratch_shapes=r_scratch + x_scratch,
        compiler_params=_cparams(("arbitrary",)),
        name="gmlp_mixer",
    )(*x_args, mods, g.reshape(1, d), w_in.astype(BF16), g_v.reshape(1, a), w_s.astype(BF16),
      bias, w_out.astype(BF16), *_route_args(route_w, d))
    return out[0], out[1:]


def _tile_group(t, geo, tile0=0):
    n_ctx_tiles, tiles_per_batch, nb = geo
    tt = t + tile0
    return jnp.where(tt < n_ctx_tiles, nb, (tt - n_ctx_tiles) // tiles_per_batch)


def _dot_f32(tri_bf16, x):
    hi = x.astype(BF16)
    r1 = x - hi.astype(F32)
    mid = r1.astype(BF16)
    lo = (r1 - mid.astype(F32)).astype(BF16)
    return _dot(tri_bf16, hi) + _dot(tri_bf16, mid) + _dot(tri_bf16, lo)


def _gla_proj_kernel(x_ref, m_ref, g_ref, wq_ref, wk_ref, wv_ref, wg_ref, wr_ref, wa_ref, ba_ref,
                     cos_ref, sin_ref, q_ref, k_ref, v_ref, go_ref, la_ref):
    m = m_ref[0]
    kdim = wq_ref.shape[1]
    hk = kdim // GLA_HEADS
    nf = hk // 4
    npart = x_ref.shape[0] // GLA_PROJ_PART
    parts = [slice(p * GLA_PROJ_PART, (p + 1) * GLA_PROJ_PART) for p in range(npart)]
    zs, qs, ks = [], [], []
    for rs in parts:
        h = _modulate(x_ref[rs, :], g_ref[...], m[0:1], m[1:2]).astype(BF16)
        zs.append(_dot(_dot(h, wr_ref[...]).astype(BF16), wa_ref[...]) + ba_ref[...])
        qs.append(_dot(h, wq_ref[...]) * np.float32(hk ** -0.5))
        ks.append(_dot(h, wk_ref[...]))
        v_ref[rs, :] = _dot(h, wv_ref[...]).astype(BF16)
        go_ref[rs, :] = _dot(h, wg_ref[...])
    for rs, z in zip(parts, zs):
        la_ref[rs, :] = jax.nn.log_sigmoid(z) * np.float32(1.0 / GLA_TAU)

    lane = lax.broadcasted_iota(I32, (GLA_PROJ_PART, kdim), 1)
    first = (lane % (2 * nf)) < nf
    for rs, q, k in zip(parts, qs, ks):
        cos = jnp.concatenate([cos_ref[rs, :]] * GLA_HEADS, axis=1)
        sin = jnp.concatenate([sin_ref[rs, :]] * GLA_HEADS, axis=1)

        def rope(t):
            up = pltpu.roll(t, kdim - nf, 1)
            dn = pltpu.roll(t, nf, 1)
            return t * cos + jnp.where(first, up, dn) * sin

        q_ref[rs, :] = rope(q)
        k_ref[rs, :] = rope(k)


def _gla_decays(q_ref, k_ref, la_ref, n, rev):
    c = GLA_CHUNK
    ri = lax.broadcasted_iota(I32, (c, c), 0)
    ci = lax.broadcasted_iota(I32, (c, c), 1)
    keep = (ci >= ri) if rev else (ci <= ri)
    rows = slice(n * c, (n + 1) * c)
    cum = _dot_f32(keep.astype(BF16), la_ref[rows, :])
    last = cum[0:1] if rev else cum[c - 1:c]
    q = q_ref[rows, :]
    k = k_ref[rows, :]
    mid = cum[c // 2:c // 2 + 1]
    return dict(rows=rows, keep=keep, dec=jnp.exp(last),
                q_dec=(q * jnp.exp(cum)).astype(BF16),
                q_att=(q * jnp.exp(cum - mid)).astype(BF16),
                k_att=(k * jnp.exp(mid - cum)).astype(BF16),
                k_end=(k * jnp.exp(last - cum)).astype(BF16))


def _gla_scan_kernel(qf_ref, kf_ref, vf_ref, laf_ref, qb_ref, kb_ref, vb_ref, lab_ref,
                     of_ref, ob_ref, stf_ref, stb_ref):
    s = pl.program_id(1)

    @pl.when(s == 0)
    def _():
        stf_ref[...] = jnp.zeros_like(stf_ref)
        stb_ref[...] = jnp.zeros_like(stb_ref)

    nch = qf_ref.shape[0] // GLA_CHUNK
    hk = qf_ref.shape[1] // GLA_HEADS
    hv = vf_ref.shape[1] // GLA_HEADS
    dirs = ((qf_ref, kf_ref, vf_ref, laf_ref, of_ref, stf_ref, False),
            (qb_ref, kb_ref, vb_ref, lab_ref, ob_ref, stb_ref, True))
    heads = [(slice(h * hk, (h + 1) * hk), slice(h * hv, (h + 1) * hv)) for h in range(GLA_HEADS)]
    for n in range(nch):
        w = [_gla_decays(q_ref, k_ref, la_ref, nch - 1 - n if rev else n, rev)
             for q_ref, k_ref, _, la_ref, _, _, rev in dirs]
        att = [[jnp.where(w[d]["keep"], _dot_nt(w[d]["q_att"][:, ks], w[d]["k_att"][:, ks]), 0.0).astype(BF16)
                for ks, _ in heads] for d in range(2)]
        for d, (_, _, v_ref, _, o_ref, st_ref, _) in enumerate(dirs):
            for h, (ks, vs) in enumerate(heads):
                o_ref[w[d]["rows"], vs] = (_dot(att[d][h], v_ref[w[d]["rows"], vs])
                                           + _dot_nt(w[d]["q_dec"][:, ks], st_ref[h].astype(BF16)))
        for d, (_, _, v_ref, _, _, st_ref, _) in enumerate(dirs):
            for h, (ks, vs) in enumerate(heads):
                st_ref[h] = (st_ref[h] * w[d]["dec"][:, ks]
                             + _dot_tn(v_ref[w[d]["rows"], vs], w[d]["k_end"][:, ks]))


def _gla_out_kernel(x_ref, m_ref, of_ref, ob_ref, go_ref, gn_ref, wout_ref, *route_refs):
    route_in, o_ref, route_out = route_refs[:3], route_refs[3], route_refs[4:]
    x = x_ref[...]
    m = m_ref[0]
    o = of_ref[...] + ob_ref[...]
    hv = gn_ref.shape[1]
    parts = [_rms(o[:, h * hv:(h + 1) * hv], gn_ref[...]) for h in range(GLA_HEADS)]
    o = jnp.concatenate(parts, axis=1)
    gate = go_ref[...]
    y = _dot((o * (gate * jax.nn.sigmoid(gate))).astype(BF16), wout_ref[...])
    x_new = x + m[2:3] * y
    o_ref[...] = x_new
    _route_tail(x_new, m, *route_in, *route_out)


def _gla_mixer(xs, mods, g, w_in, w_a2, b_a, g_o, w_out, geo, dims, route_w):
    b, seq, lctx = dims
    nt, d = xs.shape
    n_tiles = nt // TM
    kdim = w_a2.shape[2]
    vdim = g_o.shape[0] * GLA_HEADS
    hk = kdim // GLA_HEADS
    nf = hk // 4
    wq = w_in[:, :kdim].astype(BF16)
    wk = w_in[:, kdim:2 * kdim].astype(BF16)
    wv = w_in[:, 2 * kdim:2 * kdim + vdim].astype(BF16)
    wg = w_in[:, 2 * kdim + vdim:2 * kdim + 2 * vdim].astype(BF16)
    wr = jnp.pad(w_in[:, 2 * kdim + 2 * vdim:], ((0, 0), (0, 128 - 2 * GLA_RANK))).astype(BF16)
    wa = jnp.zeros((128, 2 * kdim), F32)
    wa = wa.at[:GLA_RANK, :kdim].set(w_a2[0]).at[GLA_RANK:2 * GLA_RANK, kdim:].set(w_a2[1]).astype(BF16)
    ba = b_a.reshape(1, 2 * kdim)

    tpos = np.arange(seq)
    freqs = np.power(np.float32(ROPE_BASE), -np.arange(nf, dtype=np.float32) / np.float32(nf)).astype(np.float32)
    ar = (tpos // GRID_W).astype(np.float32)[:, None] * freqs
    ac = (tpos % GRID_W).astype(np.float32)[:, None] * freqs
    cos = np.concatenate([np.cos(ar), np.cos(ar), np.cos(ac), np.cos(ac)], axis=1)
    sin = np.concatenate([-np.sin(ar), np.sin(ar), -np.sin(ac), np.sin(ac)], axis=1)
    cos = jnp.asarray(np.concatenate([np.ones((TM, hk), np.float32), cos], axis=0).astype(np.float32))
    sin = jnp.asarray(np.concatenate([np.zeros((TM, hk), np.float32), sin], axis=0).astype(np.float32))

    n_ctx_tiles, tiles_per_batch, _ = geo
    grp = functools.partial(_tile_group, geo=geo)

    def rope_blk(t):
        return (jnp.where(t < n_ctx_tiles, 0, 1 + (t - n_ctx_tiles) % tiles_per_batch), 0)

    const2 = lambda t: (0, 0)
    row = lambda t: (t, 0)
    q, k, v, go, la = pl.pallas_call(
        _gla_proj_kernel,
        grid=(n_tiles,),
        in_specs=[pl.BlockSpec((TM, d), row),
                  pl.BlockSpec((1, 6, d), lambda t: (grp(t), 0, 0)),
                  pl.BlockSpec((1, d), const2),
                  pl.BlockSpec((d, kdim), const2), pl.BlockSpec((d, kdim), const2),
                  pl.BlockSpec((d, vdim), const2), pl.BlockSpec((d, vdim), const2),
                  pl.BlockSpec((d, 128), const2), pl.BlockSpec((128, 2 * kdim), const2),
                  pl.BlockSpec((1, 2 * kdim), const2),
                  pl.BlockSpec((TM, hk), rope_blk), pl.BlockSpec((TM, hk), rope_blk)],
        out_specs=[pl.BlockSpec((TM, kdim), row), pl.BlockSpec((TM, kdim), row),
                   pl.BlockSpec((TM, vdim), row), pl.BlockSpec((TM, vdim), row),
                   pl.BlockSpec((TM, 2 * kdim), row)],
        out_shape=[jax.ShapeDtypeStruct((nt, kdim), F32), jax.ShapeDtypeStruct((nt, kdim), F32),
                   jax.ShapeDtypeStruct((nt, vdim), BF16), jax.ShapeDtypeStruct((nt, vdim), F32),
                   jax.ShapeDtypeStruct((nt, 2 * kdim), F32)],
        compiler_params=_cparams(("parallel",)),
        name="gla_proj",
    )(xs, mods, g.reshape(1, d), wq, wk, wv, wg, wr, wa, ba, cos, sin)

    tb = GLA_TB
    ctx_steps = lctx // tb
    lat_steps = seq // tb
    steps = ctx_steps + lat_steps

    def blk(bi, s, rev):
        if rev:
            cs, ls = ctx_steps - 1 - s, lat_steps - 1 - (s - ctx_steps)
        else:
            cs, ls = s, s - ctx_steps
        return jnp.where(s < ctx_steps, bi * ctx_steps + cs, b * ctx_steps + bi * lat_steps + ls)

    def dir_specs(rev):
        row_blk = lambda bi, s: (blk(bi, s, rev), 0)
        return [pl.BlockSpec((tb, kdim), row_blk), pl.BlockSpec((tb, kdim), row_blk),
                pl.BlockSpec((tb, vdim), row_blk),
                pl.BlockSpec((tb, kdim), lambda bi, s: (blk(bi, s, rev), 1 if rev else 0))]

    state = pltpu.VMEM((GLA_HEADS, vdim // GLA_HEADS, hk), F32)
    o_f, o_b = pl.pallas_call(
        _gla_scan_kernel,
        grid=(b, steps),
        in_specs=dir_specs(False) + dir_specs(True),
        out_specs=[pl.BlockSpec((tb, vdim), lambda bi, s: (blk(bi, s, False), 0)),
                   pl.BlockSpec((tb, vdim), lambda bi, s: (blk(bi, s, True), 0))],
        out_shape=[jax.ShapeDtypeStruct((nt, vdim), F32)] * 2,
        scratch_shapes=[state, state],
        compiler_params=_cparams(("parallel", "arbitrary")),
        name="gla_scan",
    )(q, k, v, la, q, k, v, la)

    r_in, r_out, r_shape, r_scratch = _route_specs(nt, d, route_w[1].shape[1])
    out = pl.pallas_call(
        _gla_out_kernel,
        grid=(n_tiles,),
        in_specs=[pl.BlockSpec((TM, d), row),
                  pl.BlockSpec((1, 6, d), lambda t: (grp(t), 0, 0)),
                  pl.BlockSpec((TM, vdim), row), pl.BlockSpec((TM, vdim), row),
                  pl.BlockSpec((TM, vdim), row),
                  pl.BlockSpec((1, vdim // GLA_HEADS), const2),
                  pl.BlockSpec((vdim, d), const2)] + r_in,
        out_specs=[pl.BlockSpec((TM, d), row)] + r_out,
        out_shape=[jax.ShapeDtypeStruct((nt, d), F32)] + r_shape,
        scratch_shapes=r_scratch,
        compiler_params=_cparams(("arbitrary",)),
        name="gla_out",
    )(xs, mods, o_f, o_b, go, g_o.reshape(1, -1), w_out.astype(BF16), *_route_args(route_w, d))
    return out[0], out[1:]


def _na_proj_kernel(x_ref, m_ref, g_ref, w_ref, q_ref, k_ref, v_ref):
    x = x_ref[...]
    m = m_ref[0]
    d = x.shape[1]
    h = _modulate(x, g_ref[...], m[0:1], m[1:2]).astype(BF16)
    qkv = _dot(h, w_ref[...])
    hd = d // NA_HEADS
    for p in range(q_ref.shape[0]):
        cs = slice(p * LANES, (p + 1) * LANES)
        q_ref[p] = (qkv[:, cs] * np.float32(hd ** -0.5)).astype(BF16)
        k_ref[p] = qkv[:, d + p * LANES:d + (p + 1) * LANES].astype(BF16)
        v_ref[p] = qkv[:, 2 * d + p * LANES:2 * d + (p + 1) * LANES].astype(BF16)


def _na_attn_kernel(tbl_ref, q_ref, *refs, n_tiles, nwin):
    k_refs, v_refs = refs[:nwin], refs[nwin:2 * nwin]
    kc_ref, vc_ref, bias_ref, o_ref, s_ref, p_ref = refs[2 * nwin:]
    t = pl.program_id(2)
    typ = jnp.where(t == 0, 0, jnp.where(t == n_tiles - 1, 2, 1))
    npair = NA_WROWS // 2
    nlat = NA_WROWS * GRID_W
    hd = LANES // 2
    lane = lax.broadcasted_iota(I32, q_ref.shape[1:], 1)
    npp, tq = q_ref.shape[0], q_ref.shape[1]
    for pp in range(npp):
        q = q_ref[pp]
        k_all = jnp.concatenate([r[pp] for r in k_refs] + [kc_ref[pp]], axis=0)
        zero = jnp.zeros_like(q)
        q2 = jnp.concatenate([jnp.where(lane < hd, q, zero), jnp.where(lane >= hd, q, zero)], axis=0)
        s_ref[pp] = _dot_nt(q2, k_all)
    for pp in range(npp):
        for hh in range(2):
            for dr in range(NA_QROWS):
                rs = slice(hh * tq + dr * GRID_W, hh * tq + (dr + 1) * GRID_W)
                lat = [s_ref[pp, rs, j * LANES:(j + 1) * LANES]
                       + bias_ref[2 * pp + hh, tbl_ref[typ * (NA_QROWS * npair) + dr * npair + j]]
                       for j in range(npair)]
                sb = jnp.concatenate(lat + [s_ref[pp, rs, nlat:]], axis=1)
                e = jnp.exp(sb - jnp.max(sb, axis=1, keepdims=True))
                p_ref[pp, rs, :] = (e * (1.0 / jnp.sum(e, axis=1, keepdims=True))).astype(BF16)
    outs = []
    for pp in range(npp):
        v_all = jnp.concatenate([r[pp] for r in v_refs] + [vc_ref[pp]], axis=0)
        o2 = _dot(p_ref[pp], v_all)
        outs.append(jnp.where(lane < hd, o2[:tq], o2[tq:]))
    o_ref[...] = jnp.concatenate(outs, axis=1).astype(BF16)


def _na_out_kernel(x_ref, m_ref, a_ref, w_ref, *route_refs):
    route_in, o_ref, route_out = route_refs[:3], route_refs[3], route_refs[4:]
    m = m_ref[0]
    x_new = x_ref[...] + m[2:3] * _dot(a_ref[...], w_ref[...])
    o_ref[...] = x_new
    _route_tail(x_new, m, *route_in, *route_out)


def _na_tables(rows):
    npair = NA_WROWS // 2
    tbl = np.zeros((3, NA_QROWS, npair), np.int32)
    for typ, r_base in enumerate((0, NA_QROWS, rows - NA_QROWS)):
        w0 = int(np.clip(r_base - NA_KH // 2, 0, rows - NA_WROWS))
        for dr in range(NA_QROWS):
            r = r_base + dr
            r0 = int(np.clip(r - NA_KH // 2, 0, rows - NA_KH))
            for j in range(npair):
                kr = (w0 + 2 * j, w0 + 2 * j + 1)
                ok = [r0 <= x < r0 + NA_KH for x in kr]
                ri = [x - r + NA_KH - 1 for x in kr]
                if ok[0] and ok[1]:
                    e = ri[0]
                elif ok[0]:
                    e = 16 + ri[0]
                elif ok[1]:
                    e = 32 + ri[1]
                else:
                    e = 63
                tbl[typ, dr, j] = e
    return tbl.reshape(-1)


def _na_bias_kernel(rpb_ref, o_ref, *, nr, nw):
    h = pl.program_id(0)
    c = lax.broadcasted_iota(I32, (GRID_W, GRID_W), 0)
    k = lax.broadcasted_iota(I32, (GRID_W, GRID_W), 1)
    cstart = jnp.clip(c - NA_KW // 2, 0, GRID_W - NA_KW)
    ok = jnp.logical_and(k >= cstart, k < cstart + NA_KW)
    rel = k - c + NA_KW - 1
    neg = jnp.full((GRID_W, GRID_W), NEG_INF, F32)
    cbs = []
    for r in range(nr):
        acc = neg
        for w in range(nw):
            acc = jnp.where(rel == w, rpb_ref[(h * nr + r) * nw + w], acc)
        cbs.append(jnp.where(ok, acc, neg))
    cbs += [neg] * (17 - nr)
    for r in range(16):
        o_ref[0, r] = jnp.concatenate([cbs[r], cbs[r + 1]], axis=1)
        o_ref[0, 16 + r] = jnp.concatenate([cbs[r], neg], axis=1)
        o_ref[0, 32 + r] = jnp.concatenate([neg, cbs[r]], axis=1)
        o_ref[0, 48 + r] = jnp.concatenate([neg, neg], axis=1)


def _na_bias_table(rpb):
    nh, nr, nw = rpb.shape
    return pl.pallas_call(
        functools.partial(_na_bias_kernel, nr=nr, nw=nw),
        grid=(nh,),
        in_specs=[pl.BlockSpec(memory_space=pltpu.SMEM)],
        out_specs=pl.BlockSpec((1, 64, GRID_W, 2 * GRID_W), lambda h: (h, 0, 0, 0)),
        out_shape=jax.ShapeDtypeStruct((nh, 64, GRID_W, 2 * GRID_W), F32),
        compiler_params=_cparams(("parallel",)),
        name="na_bias_table",
    )(rpb.reshape(-1).astype(F32))


def _na_mixer(xs, mods, g, w_qkv, rpb, w_out, geo, dims, route_w):
    b, seq, lctx = dims
    nt, d = xs.shape
    n_tiles_tok = nt // TM
    grp = functools.partial(_tile_group, geo=geo)
    npairs = d // 128
    const2 = lambda t: (0, 0)
    q, k, v = pl.pallas_call(
        _na_proj_kernel,
        grid=(n_tiles_tok,),
        in_specs=[pl.BlockSpec((TM, d), lambda t: (t, 0)),
                  pl.BlockSpec((1, 6, d), lambda t: (grp(t), 0, 0)),
                  pl.BlockSpec((1, d), const2),
                  pl.BlockSpec((d, 3 * d), const2)],
        out_specs=[pl.BlockSpec((npairs, TM, 128), lambda t: (0, t, 0))] * 3,
        out_shape=[jax.ShapeDtypeStruct((npairs, nt, 128), BF16)] * 3,
        compiler_params=_cparams(("parallel",)),
        name="na_proj",
    )(xs, mods, g.reshape(1, d), w_qkv.astype(BF16))

    rows = seq // GRID_W
    tq = NA_QROWS * GRID_W
    n_tiles = rows // NA_QROWS
    wb = 256
    nwin = NA_WROWS * GRID_W // wb
    lat0 = b * lctx
    tbl = jnp.asarray(_na_tables(rows))
    bias = _na_bias_table(rpb)

    pp = NA_PAIRS_PER_STEP
    assert npairs % pp == 0 and (NA_QROWS * GRID_W) % wb == 0 and (NA_KH // 2 * GRID_W) % wb == 0

    def win(i):
        def f(p, bi, t, tbl_ref):
            w = jnp.clip(t * (NA_QROWS * GRID_W // wb) - NA_KH // 2 * GRID_W // wb, 0, seq // wb - nwin)
            return (p, (lat0 + bi * seq) // wb + w + i, 0)
        return f

    kv_specs = [pl.BlockSpec((pp, wb, 128), win(i)) for i in range(nwin)]
    attn = pl.pallas_call(
        functools.partial(_na_attn_kernel, n_tiles=n_tiles, nwin=nwin),
        grid_spec=pltpu.PrefetchScalarGridSpec(
            num_scalar_prefetch=1,
            grid=(npairs // pp, b, n_tiles),
            in_specs=[pl.BlockSpec((pp, tq, 128), lambda p, bi, t, tr: (p, (lat0 + bi * seq) // tq + t, 0))]
                     + kv_specs + kv_specs
                     + [pl.BlockSpec((pp, lctx, 128), lambda p, bi, t, tr: (p, bi, 0)),
                        pl.BlockSpec((pp, lctx, 128), lambda p, bi, t, tr: (p, bi, 0)),
                        pl.BlockSpec((2 * pp, 64, GRID_W, 2 * GRID_W), lambda p, bi, t, tr: (p, 0, 0, 0),
                                     pipeline_mode=pl.Buffered(1))],
            out_specs=pl.BlockSpec((tq, 128 * pp), lambda p, bi, t, tr: (bi * n_tiles + t, p)),
            scratch_shapes=[pltpu.VMEM((pp, 2 * tq, NA_WROWS * GRID_W + lctx), F32),
                            pltpu.VMEM((pp, 2 * tq, NA_WROWS * GRID_W + lctx), BF16)]),
        out_shape=jax.ShapeDtypeStruct((b * seq, d), BF16),
        compiler_params=_cparams(("parallel", "parallel", "arbitrary")),
        name="na_attn",
    )(tbl, q, *([k] * nwin), *([v] * nwin), k, v, bias)

    n_lat_tiles = b * seq // TM
    tile0 = lat0 // TM
    grp_l = functools.partial(_tile_group, geo=geo, tile0=tile0)
    r_in, r_out, r_shape, r_scratch = _route_specs(b * seq, d, route_w[1].shape[1])
    out = pl.pallas_call(
        _na_out_kernel,
        grid=(n_lat_tiles,),
        in_specs=[pl.BlockSpec((TM, d), lambda t: (t + tile0, 0)),
                  pl.BlockSpec((1, 6, d), lambda t: (grp_l(t), 0, 0)),
                  pl.BlockSpec((TM, d), lambda t: (t, 0)),
                  pl.BlockSpec((d, d), const2)] + r_in,
        out_specs=[pl.BlockSpec((TM, d), lambda t: (t, 0))] + r_out,
        out_shape=[jax.ShapeDtypeStruct((b * seq, d), F32)] + r_shape,
        scratch_shapes=r_scratch,
        compiler_params=_cparams(("arbitrary",)),
        name="na_out",
    )(xs, mods, attn, w_out.astype(BF16), *_route_args(route_w, d))
    return out[0], out[1:]


SC_CORES = 2
SC_SUBCORES = 16
SC_WORKERS = SC_CORES * SC_SUBCORES
SLOT_T_MAX = 2048


def _sc_mesh():
    return plsc.VectorSubcoreMesh(core_axis_name="c", subcore_axis_name="s")


def _sc_chunk(per_worker, max_chunk):
    return max(c for c in range(8, max_chunk + 1, 8) if per_worker % (2 * c) == 0)


def _sc_gather_rows(table, idx):
    dd = table.shape[1]
    bsz = idx.shape[0]
    per_w = bsz // SC_WORKERS
    assert per_w * SC_WORKERS == bsz
    chunk = _sc_chunk(per_w, 64)
    n_chunks = per_w // chunk

    def body(table_hbm, idx_hbm, out_hbm, idx_v, rows0, rows1, g0, g1, w0, w1):
        wid = lax.axis_index("s") * SC_CORES + lax.axis_index("c")
        pltpu.sync_copy(idx_hbm.at[wid], idx_v)
        base = wid * per_w

        def out_rows(j):
            return out_hbm.at[pl.ds(pl.multiple_of(base + j * chunk, 8), chunk)]

        def step(i, carry):
            j0, j1 = 2 * i, 2 * i + 1
            ga = pltpu.async_copy(table_hbm.at[idx_v.at[j0]], rows0, g0)
            gb = pltpu.async_copy(table_hbm.at[idx_v.at[j1]], rows1, g1)
            ga.wait()
            wa = pltpu.async_copy(rows0, out_rows(j0), w0)
            gb.wait()
            wb = pltpu.async_copy(rows1, out_rows(j1), w1)
            wa.wait()
            wb.wait()
            return carry

        lax.fori_loop(0, n_chunks // 2, step, 0)

    return pl.kernel(
        body, out_type=jax.ShapeDtypeStruct((bsz, dd), table.dtype), mesh=_sc_mesh(),
        scratch_types=[pltpu.VMEM((n_chunks, chunk), I32),
                       pltpu.VMEM((chunk, dd), table.dtype), pltpu.VMEM((chunk, dd), table.dtype),
                       pltpu.SemaphoreType.DMA, pltpu.SemaphoreType.DMA,
                       pltpu.SemaphoreType.DMA, pltpu.SemaphoreType.DMA],
        name="sc_gather_rows",
    )(table, idx.reshape(SC_WORKERS, n_chunks, chunk))


def _sc_scatter_rows(rows, dest, n_out):
    n, dd = rows.shape
    kk = dest.shape[0]
    per_w = n // SC_WORKERS
    assert per_w * SC_WORKERS == n
    chunk = _sc_chunk(per_w, 64)
    n_chunks = per_w // chunk
    dest_w = dest.reshape(kk, SC_WORKERS, n_chunks, chunk).transpose(1, 2, 0, 3)
    dest_w = dest_w.reshape(SC_WORKERS, n_chunks * kk, chunk)

    def body(rows_hbm, dest_hbm, out_hbm, idx_v, rows0, rows1, r0, r1, s0, s1):
        wid = lax.axis_index("s") * SC_CORES + lax.axis_index("c")
        pltpu.sync_copy(dest_hbm.at[wid], idx_v)
        base = wid * per_w

        def in_rows(j):
            return rows_hbm.at[pl.ds(pl.multiple_of(base + j * chunk, 8), chunk)]

        def step(i, carry):
            j0, j1 = 2 * i, 2 * i + 1
            ra = pltpu.async_copy(in_rows(j0), rows0, r0)
            rb = pltpu.async_copy(in_rows(j1), rows1, r1)
            ra.wait()
            sa = [pltpu.async_copy(rows0, out_hbm.at[idx_v.at[j0 * kk + k]], s0) for k in range(kk)]
            rb.wait()
            sb = [pltpu.async_copy(rows1, out_hbm.at[idx_v.at[j1 * kk + k]], s1) for k in range(kk)]
            for cp in sa + sb:
                cp.wait()
            return carry

        lax.fori_loop(0, n_chunks // 2, step, 0)

    return pl.kernel(
        body, out_type=jax.ShapeDtypeStruct((n_out, dd), rows.dtype), mesh=_sc_mesh(),
        scratch_types=[pltpu.VMEM((n_chunks * kk, chunk), I32),
                       pltpu.VMEM((chunk, dd), rows.dtype), pltpu.VMEM((chunk, dd), rows.dtype),
                       pltpu.SemaphoreType.DMA, pltpu.SemaphoreType.DMA,
                       pltpu.SemaphoreType.DMA, pltpu.SemaphoreType.DMA],
        name="sc_scatter_rows",
    )(rows, dest_w)


def _prefix_sum_rows(col):
    nr = col.shape[0]
    acc = jnp.broadcast_to(col, (nr, LANES))
    row = lax.broadcasted_iota(I32, (nr, LANES), 0)
    s = 1
    while s < nr:
        acc = acc + jnp.where(row >= s, pltpu.roll(acc, s, 0), 0.0)
        s *= 2
    return acc[:, 0:1]


def _slot_kernel(idx_ref, rank_ref, cnt_ref, dest_ref, btab_ref, *, bm):
    cnt = cnt_ref[...]
    ne = cnt.shape[0]
    padded = jnp.floor((cnt + (bm - 1.0)) * (1.0 / bm)) * bm
    pad_end = _prefix_sum_rows(padded)
    pad_start = pad_end - padded
    idx = idx_ref[...]
    e_iota = lax.broadcasted_iota(I32, (ne, idx.shape[1]), 0)
    starts = [jnp.sum(jnp.where(e_iota == idx[k:k + 1], pad_start, 0.0), axis=0, keepdims=True)
              for k in range(idx.shape[0])]
    dest_ref[...] = jnp.concatenate(starts, axis=0).astype(I32) + rank_ref[...]

    nbp = btab_ref.shape[1]
    bstart = lax.broadcasted_iota(I32, (ne, nbp), 1).astype(F32) * bm
    be = jnp.minimum(jnp.sum((pad_end <= bstart).astype(F32), axis=0, keepdims=True), ne - 1.0)
    mine = lax.broadcasted_iota(I32, (ne, nbp), 0).astype(F32) == be
    pick = lambda col: jnp.sum(jnp.where(mine, col, 0.0), axis=0, keepdims=True)
    live = jnp.clip(pick(pad_start + cnt) - bstart[0:1], 0.0, bm)
    r = lax.broadcasted_iota(I32, (ne, ne), 0)
    c = lax.broadcasted_iota(I32, (ne, ne), 1)
    ends_on_lanes = _dot_f32(jnp.ones((ne, ne), BF16), jnp.where(r == c, pad_end, 0.0))
    nxt_e = jnp.sum((ends_on_lanes <= pad_end).astype(F32), axis=1, keepdims=True)
    nxt_e = jnp.where(nxt_e < ne, nxt_e, -1.0)
    ordinal = _prefix_sum_rows((cnt > 0).astype(F32)) - 1.0
    slot_e = ordinal - 2.0 * jnp.floor(ordinal * 0.5)
    rowi = lax.broadcasted_iota(I32, btab_ref.shape, 0)
    tab = jnp.where(rowi == 0, be, jnp.where(rowi == 1, live, jnp.where(rowi == 2, pick(nxt_e), pick(slot_e))))
    btab_ref[...] = tab.astype(I32)


def _route_specs(n, d, ne):
    const2 = lambda t: (0, 0)
    in_specs = [pl.BlockSpec((1, d), const2), pl.BlockSpec((ne, d), const2), pl.BlockSpec((ne, 1), const2)]
    out_specs = [pl.BlockSpec((TM, d // 2), lambda t: (t, 0)),
                 pl.BlockSpec((TOP_K, TM), lambda t: (0, t)),
                 pl.BlockSpec((TM, SUBLANES), lambda t: (t, 0)),
                 pl.BlockSpec((TOP_K, TM), lambda t: (0, t)),
                 pl.BlockSpec((ne, 1), const2)]
    out_shape = [jax.ShapeDtypeStruct((n, d // 2), jnp.uint32),
                 jax.ShapeDtypeStruct((TOP_K, n), I32),
                 jax.ShapeDtypeStruct((n, SUBLANES), F32),
                 jax.ShapeDtypeStruct((TOP_K, n), I32),
                 jax.ShapeDtypeStruct((ne, 1), F32)]
    return in_specs, out_specs, out_shape, [pltpu.VMEM((ne, 1), F32)]


def _route_args(route_w, d):
    g_ffn, w_router, b_router = route_w
    ne = w_router.shape[1]
    return g_ffn.reshape(1, d), w_router.T.astype(BF16), b_router.reshape(ne, 1)


def _route_tail(x, m, g_ref, wr_ref, br_ref, h_ref, idx_ref, gate_ref, rank_ref, cnt_ref, run_ref):
    t = pl.program_id(0)

    @pl.when(t == 0)
    def _():
        run_ref[...] = jnp.zeros_like(run_ref)

    h = _modulate(x, g_ref[...], m[3:4], m[4:5]).astype(BF16)
    h_ref[...] = _pack_bf16_pairs(h)
    logits = _dot_nt(wr_ref[...], h) + br_ref[...]
    ne, tm = logits.shape
    e_iota = lax.broadcasted_iota(I32, (ne, tm), 0)
    vals, idxs = [], []
    l = logits
    for _ in range(TOP_K):
        mk = jnp.max(l, axis=0, keepdims=True)
        ik = jnp.min(jnp.where(l == mk, e_iota, ne), axis=0, keepdims=True)
        vals.append(mk)
        idxs.append(ik)
        l = jnp.where(e_iota == ik, -jnp.inf, l)
    top_val = jnp.concatenate(vals, axis=0)
    ex = jnp.exp(top_val - vals[0])
    gates = ex / jnp.sum(ex, axis=0, keepdims=True)
    pad = jnp.zeros((gate_ref.shape[1] - TOP_K, tm), F32)
    gate_ref[...] = jnp.concatenate([gates, pad], axis=0).T
    idx_ref[...] = jnp.concatenate(idxs, axis=0)

    hits = [e_iota == ik for ik in idxs]
    cnt = hits[0].astype(F32)
    for hk in hits[1:]:
        cnt = cnt + hk.astype(F32)
    si = lax.broadcasted_iota(I32, (tm, tm), 0)
    ti = lax.broadcasted_iota(I32, (tm, tm), 1)
    before = (si < ti).astype(BF16)
    total = _dot(cnt.astype(BF16), before) + run_ref[...]
    ranks = [jnp.sum(jnp.where(hk, total, 0.0), axis=0, keepdims=True) for hk in hits]
    rank_ref[...] = jnp.concatenate(ranks, axis=0).astype(I32)
    run_ref[...] = run_ref[...] + jnp.sum(cnt, axis=1, keepdims=True)
    cnt_ref[...] = run_ref[...]


def _pack_bf16_pairs(v):
    bits = lax.bitcast_convert_type(v.astype(BF16).astype(F32), jnp.uint32)
    half = bits.shape[1] // 2
    return (bits[:, half:] & jnp.uint32(0xFFFF0000)) | (bits[:, :half] >> 16)


def _unpack_bf16_pairs(w):
    return (lax.bitcast_convert_type(w << 16, F32),
            lax.bitcast_convert_type(w & jnp.uint32(0xFFFF0000), F32))


def _expert_kernel(tab_ref, x_ref, wgu_hbm, bgu_ref, wd_hbm, bd_ref, o_ref,
                   wgu_in, wd_in, wgu_bf, wd_bf, act_ref, sem_gu, sem_d, *, layer):
    blk = pl.program_id(0)
    ch = wgu_bf.shape[2]
    f = act_ref.shape[1]
    nc = f // ch
    expert, live, nxt, wslot = (tab_ref[i, blk] for i in range(4))
    used = live > 0
    fresh = jnp.logical_or(blk == 0, expert != tab_ref[0, jnp.maximum(blk - 1, 0)])

    def fetch(e, slot):
        return (pltpu.make_async_copy(wgu_hbm.at[layer, e], wgu_in.at[slot], sem_gu.at[slot]),
                pltpu.make_async_copy(wd_hbm.at[layer, e], wd_in.at[slot], sem_d.at[slot]))

    @pl.when(jnp.logical_and(used, fresh))
    def _():
        slot = wslot
        mine = fetch(expert, slot)

        @pl.when(blk == 0)
        def _():
            for cp in mine:
                cp.start()

        @pl.when(nxt >= 0)
        def _():
            for cp in fetch(nxt, 1 - slot):
                cp.start()

        for cp in mine:
            cp.wait()

    bm = x_ref.shape[0]
    fresh_full = jnp.logical_and(fresh, live > bm - MOE_SUB)

    def cast_gu(j):
        wgu_bf[j] = wgu_in[wslot, :, j * ch:(j + 1) * ch].astype(BF16)

    def cast_d(j):
        wd_bf[j] = wd_in[wslot, :, j * ch:(j + 1) * ch].astype(BF16)

    @pl.when(jnp.logical_and(jnp.logical_and(used, fresh), jnp.logical_not(fresh_full)))
    def _():
        for j in range(wgu_bf.shape[0]):
            cast_gu(j)
        for j in range(wd_bf.shape[0]):
            cast_d(j)

    def run(nr, cast):
        w = x_ref[:nr, :]
        row = lax.broadcasted_iota(I32, w.shape, 0)
        w = jnp.where(row < live, w, jnp.zeros_like(w))
        lo, hi = _unpack_bf16_pairs(w)
        x = jnp.concatenate([lo.astype(BF16), hi.astype(BF16)], axis=1)
        for c in range(nc):
            c0 = slice(c * ch, (c + 1) * ch)
            c1 = slice(f + c * ch, f + (c + 1) * ch)
            if cast:
                cast_gu(c)
                cast_gu(nc + c)
            glu = _dot(x, wgu_bf[c]) + bgu_ref[0, 0, :, c0]
            lin = _dot(x, wgu_bf[nc + c]) + bgu_ref[0, 0, :, c1]
            glu = jnp.minimum(glu, SWIGLU_LIMIT)
            lin = jnp.clip(lin, -SWIGLU_LIMIT, SWIGLU_LIMIT)
            act_ref[:nr, c0] = (glu * jax.nn.sigmoid(SWIGLU_ALPHA * glu) * (lin + 1.0)).astype(BF16)
        a = act_ref[:nr, :]
        ys = []
        for n in range(wd_bf.shape[0]):
            if cast:
                cast_d(n)
            ys.append(_dot(a, wd_bf[n]))
        o_ref[:nr, :] = _pack_bf16_pairs(jnp.concatenate(ys, axis=1) + bd_ref[0, 0])

    for nr in range(MOE_SUB, bm + 1, MOE_SUB):
        @pl.when(jnp.logical_and(jnp.logical_and(live > nr - MOE_SUB, live <= nr), jnp.logical_not(fresh_full)))
        def _(nr=nr):
            run(nr, False)
            if nr < bm:
                o_ref[nr:, :] = jnp.zeros((bm - nr, o_ref.shape[1]), o_ref.dtype)

    @pl.when(fresh_full)
    def _():
        run(bm, True)

    @pl.when(jnp.logical_not(used))
    def _():
        o_ref[...] = jnp.zeros_like(o_ref)


def _expert_call(tab, buf, w_gu, b_gu, w_down, b_down, layer, bm):
    n_slots, dw = buf.shape
    _, ne, d, f2 = w_gu.shape
    f = f2 // 2
    n_blocks = n_slots // bm
    lw = lambda i, tb: (layer, tb[0, i], 0, 0)
    row = lambda i, tb: (i, 0)
    cw = MXU_COLS
    return pl.pallas_call(
        functools.partial(_expert_kernel, layer=layer),
        grid_spec=pltpu.PrefetchScalarGridSpec(
            num_scalar_prefetch=1,
            grid=(n_blocks,),
            in_specs=[pl.BlockSpec((bm, dw), row),
                      pl.BlockSpec(memory_space=pl.ANY),
                      pl.BlockSpec((1, 1, 1, f2), lw),
                      pl.BlockSpec(memory_space=pl.ANY),
                      pl.BlockSpec((1, 1, 1, d), lw)],
            out_specs=pl.BlockSpec((bm, dw), row),
            scratch_shapes=[pltpu.VMEM((2, d, f2), F32), pltpu.VMEM((2, f, d), F32),
                            pltpu.VMEM((f2 // cw, d, cw), BF16), pltpu.VMEM((d // cw, f, cw), BF16),
                            pltpu.VMEM((bm, f), BF16),
                            pltpu.SemaphoreType.DMA((2,)), pltpu.SemaphoreType.DMA((2,))]),
        out_shape=jax.ShapeDtypeStruct((n_slots, dw), jnp.uint32),
        compiler_params=_cparams(("arbitrary",)),
        name="moe_experts",
    )(tab, buf, w_gu, b_gu.reshape(b_gu.shape[0], ne, 1, f2), w_down, b_down.reshape(b_down.shape[0], ne, 1, d))


def _combine_kernel(x_ref, m_ref, y_ref, gate_ref, gout_ref, o_ref, *, final):
    m = m_ref[0]
    gate = gate_ref[...]
    half = y_ref.shape[2]
    f_lo, f_hi = None, None
    for k in range(TOP_K):
        lo, hi = _unpack_bf16_pairs(y_ref[k])
        gk = gate[:, k:k + 1]
        f_lo = gk * lo if f_lo is None else f_lo + gk * lo
        f_hi = gk * hi if f_hi is None else f_hi + gk * hi
    x_lo = x_ref[:, :half] + m[5:6, :half] * f_lo
    x_hi = x_ref[:, half:] + m[5:6, half:] * f_hi
    if final:
        ms = (jnp.sum(x_lo * x_lo, axis=-1, keepdims=True) + jnp.sum(x_hi * x_hi, axis=-1, keepdims=True))
        r = lax.rsqrt(ms * (0.5 / half) + NORM_EPS)
        x_lo = x_lo * r * gout_ref[:, :half]
        x_hi = x_hi * r * gout_ref[:, half:]
    o_ref[:, :half] = x_lo
    o_ref[:, half:] = x_hi


def _moe_layer(xs, routed, mods, layer, w_gu, b_gu, w_down, b_down, geo, g_out, final):
    n, d = xs.shape
    h, idx, gate, rank, cnt = routed
    assert h.shape[0] == n
    ne = cnt.shape[0]
    n_tiles = n // TM
    grp = functools.partial(_tile_group, geo=geo)
    const2 = lambda t: (0, 0)

    bm = MOE_BM
    n_rows = n * TOP_K
    n_blocks = -(-n_rows // bm) + ne
    n_slots = n_blocks * bm
    nbp = -(-n_blocks // LANES) * LANES
    slot_t = max(w for w in range(LANES, SLOT_T_MAX + 1, LANES) if n % w == 0)
    dest, btab = pl.pallas_call(
        functools.partial(_slot_kernel, bm=bm),
        grid=(n // slot_t,),
        in_specs=[pl.BlockSpec((TOP_K, slot_t), lambda t: (0, t)),
                  pl.BlockSpec((TOP_K, slot_t), lambda t: (0, t)),
                  pl.BlockSpec((ne, 1), const2)],
        out_specs=[pl.BlockSpec((TOP_K, slot_t), lambda t: (0, t)),
                   pl.BlockSpec((8, nbp), const2)],
        out_shape=[jax.ShapeDtypeStruct((TOP_K, n), I32),
                   jax.ShapeDtypeStruct((8, nbp), I32)],
        compiler_params=_cparams(("arbitrary",)),
        name="moe_slots",
    )(idx, rank, cnt)

    buf = _sc_scatter_rows(h, dest, n_slots)
    y = _expert_call(btab, buf, w_gu, b_gu, w_down, b_down, layer, bm)
    dy = d // 2
    yk = _sc_gather_rows(y, dest.reshape(-1)).reshape(TOP_K, n, dy)
    return pl.pallas_call(
        functools.partial(_combine_kernel, final=final),
        grid=(n_tiles,),
        in_specs=[pl.BlockSpec((TM, d), lambda t: (t, 0)),
                  pl.BlockSpec((1, 6, d), lambda t: (grp(t), 0, 0)),
                  pl.BlockSpec((TOP_K, TM, dy), lambda t: (0, t, 0)),
                  pl.BlockSpec((TM, SUBLANES), lambda t: (t, 0)),
                  pl.BlockSpec((1, d), const2)],
        out_specs=pl.BlockSpec((TM, d), lambda t: (t, 0)),
        out_shape=jax.ShapeDtypeStruct((n, d), F32),
        compiler_params=_cparams(("parallel",)),
        name="moe_combine",
    )(xs, mods, yk, gate, g_out.reshape(1, d))


def kernel(x, c, ctx, c_ctx, ada_w, ada_b, norm_mix, norm_ffn, norm_out, a_w_in, a_g_v, a_w_s, a_b_s, a_w_out, b_w_in, b_w_a2, b_b_a, b_g_o, b_w_out, c_w_qkv, c_rpb, c_w_out, moe_w_router, moe_b_router, moe_w_gu, moe_b_gu, moe_w_down, moe_b_down):
    b, seq, d = x.shape
    lctx = ctx.shape[1]
    depth = ada_w.shape[0]
    assert (b * lctx) % TM == 0 and seq % TM == 0 and lctx % GLA_TB == 0 and seq % GLA_TB == 0
    assert seq % (NA_QROWS * GRID_W) == 0 and (b * lctx) % (NA_QROWS * GRID_W) == 0 and lctx % 256 == 0
    assert seq // GRID_W >= NA_WROWS + NA_QROWS and b + 1 <= SUBLANES
    geo = (b * lctx // TM, seq // TM, b)
    dims = (b, seq, lctx)

    cond_t = jnp.zeros((d, SUBLANES), F32).at[:, :b].set(c.T).at[:, b].set(c_ctx)
    mods_all = _adaln(cond_t, b + 1, ada_w, ada_b)[:, :b + 1].reshape(depth, b + 1, 6, d)

    pair = (ctx.reshape(b * lctx, d), x.reshape(b * seq, d))
    n_tiles_all = (b * lctx + b * seq) // TM
    ctx_tiles = geo[0]
    has_ctx = True
    for i in range(depth):
        kind, j = i % N_MIXERS, i // N_MIXERS
        ctx_later = any(kk % N_MIXERS != 0 for kk in range(i + 1, depth))
        mods = mods_all[i]
        route_w = (norm_ffn[i], moe_w_router[i], moe_b_router[i])
        if i == 0 and not (kind == 0 and ctx_later):
            xs = jnp.concatenate(pair, axis=0)
        if kind == 0:
            keep_ctx = has_ctx and ctx_later
            skip = 0 if keep_ctx or not has_ctx else ctx_tiles
            geo_i = geo if has_ctx else (0, geo[1], b)
            src = pair if i == 0 and keep_ctx else xs
            n_src = n_tiles_all if i == 0 else xs.shape[0] // TM
            xs, routed = _gmlp_mixer(src, mods, norm_mix[i], a_w_in[j], a_g_v[j], a_w_s[j], a_b_s[j], a_w_out[j],
                                     skip, n_src - skip, geo_i, route_w)
            has_ctx = keep_ctx
        elif kind == 1:
            assert has_ctx
            if not ctx_later:
                raise NotImplementedError("GLA mixer whose context rows are dropped before the FFN")
            xs, routed = _gla_mixer(xs, mods, norm_mix[i], b_w_in[j], b_w_a2[j], b_b_a[j], b_g_o[j], b_w_out[j],
                                    geo, dims, route_w)
        else:
            assert has_ctx
            if ctx_later:
                raise NotImplementedError("context output of the neighbourhood mixer")
            xs, routed = _na_mixer(xs, mods, norm_mix[i], c_w_qkv[j], c_rpb[j], c_w_out[j], geo, dims, route_w)
            has_ctx = False
        geo_i = geo if has_ctx else (0, geo[1], b)
        xs = _moe_layer(xs, routed, mods, i, moe_w_gu, moe_b_gu, moe_w_down, moe_b_down, geo_i, norm_out,
                        i == depth - 1)
    if has_ctx:
        xs = xs[b * lctx:]
    return xs.reshape(b, seq, d)
```

```python
import functools

import numpy as np
import jax
import jax.numpy as jnp
from jax import lax
from jax.experimental import pallas as pl
from jax.experimental.pallas import tpu as pltpu
from jax.experimental.pallas import tpu_sc as plsc

F32 = jnp.float32
BF16 = jnp.bfloat16
I32 = jnp.int32

LANES = 128
SUBLANES = 8
MXU_COLS = 256

NORM_EPS = 1e-6
GRID_W = 64
N_MIXERS = 3

CHUNK_A = 128
A_GROUPS = 8
GMLP_PART = 256
GLA_HEADS = 4
GLA_RANK = 16
GLA_TAU = 16.0
GLA_CHUNK = 128
ROPE_BASE = 10000.0
NA_HEADS = 16
NA_KH = 8
NA_KW = 16
NEG_INF = -1e30
NA_QROWS = 4
NA_WROWS = 12
NA_PAIRS_PER_STEP = 4
TOP_K = 4
SWIGLU_LIMIT = 7.0
SWIGLU_ALPHA = 1.702
MOE_BM = 1024
MOE_SUB = 256

TM = 512
GLA_TB = 256
GLA_PROJ_PART = 256
VMEM_LIMIT = 56 * 1024 * 1024


def _cparams(sem):
    return pltpu.CompilerParams(dimension_semantics=sem, vmem_limit_bytes=VMEM_LIMIT)


def _dot(a, b):
    return jnp.dot(a, b, preferred_element_type=F32)


def _dot_nt(a, b):
    return lax.dot_general(a, b, (((1,), (1,)), ((), ())), preferred_element_type=F32)


def _dot_tn(a, b):
    return lax.dot_general(a, b, (((0,), (0,)), ((), ())), preferred_element_type=F32)


def _rms(x, g):
    return x * lax.rsqrt(jnp.mean(x * x, axis=-1, keepdims=True) + NORM_EPS) * g


def _modulate(x, g, shift, scale):
    return _rms(x, g) * (1.0 + scale) + shift


ADA_COL_BLOCKS = 4


def _ada_kernel(st_ref, w_ref, b_ref, o_ref, *, nrows):
    st = st_ref[...]
    st = st * jax.nn.sigmoid(st)
    w = w_ref[0]
    rows = [jnp.sum(w * st[:, r:r + 1], axis=0, keepdims=True) for r in range(nrows)]
    rows += [jnp.zeros_like(rows[0])] * (o_ref.shape[1] - nrows)
    o_ref[0] = jnp.concatenate(rows, axis=0) + b_ref[0]


def _adaln(cond_t, nrows, ada_w, ada_b):
    depth, d, n6 = ada_w.shape
    bn = n6 // ADA_COL_BLOCKS
    return pl.pallas_call(
        functools.partial(_ada_kernel, nrows=nrows),
        grid=(depth, ADA_COL_BLOCKS),
        in_specs=[pl.BlockSpec((d, SUBLANES), lambda i, j: (0, 0)),
                  pl.BlockSpec((1, d, bn), lambda i, j: (i, 0, j)),
                  pl.BlockSpec((1, 1, bn), lambda i, j: (i, 0, j))],
        out_specs=pl.BlockSpec((1, SUBLANES, bn), lambda i, j: (i, 0, j)),
        out_shape=jax.ShapeDtypeStruct((depth, SUBLANES, n6), F32),
        compiler_params=_cparams(("parallel", "parallel")),
        name="adaln",
    )(cond_t, ada_w, ada_b.reshape(depth, 1, n6))


def _gelu(z):
    return 0.5 * z * (1.0 + lax.erf(z * np.float32(np.sqrt(0.5))))


def _gmlp_kernel(*refs, ctx_tiles):
    if ctx_tiles:
        xc_ref, xl_ref, *refs = refs
        *refs, x_ref = refs

        @pl.when(pl.program_id(0) < ctx_tiles)
        def _():
            x_ref[...] = xc_ref[...]

        @pl.when(pl.program_id(0) >= ctx_tiles)
        def _():
            x_ref[...] = xl_ref[...]
    else:
        x_ref, *refs = refs
    m_ref, g_ref, win_ref, gv_ref, ws_ref, bs_ref, wout_ref, *route_refs = refs
    route_in, o_ref, route_out = route_refs[:3], route_refs[3], route_refs[4:]
    m = m_ref[0]
    a = gv_ref.shape[1]
    gw = a // A_GROUPS
    npart = x_ref.shape[0] // GMLP_PART
    parts = [slice(p * GMLP_PART, (p + 1) * GMLP_PART) for p in range(npart)]
    z = [_dot(_modulate(x_ref[rs, :], g_ref[...], m[0:1], m[1:2]).astype(BF16), win_ref[...]) for rs in parts]
    for rs, zp in zip(parts, z):
        zp = _gelu(zp)
        u = zp[:, :a]
        v = _rms(zp[:, a:], gv_ref[...]).astype(BF16)
        rows = []
        for c in range(GMLP_PART // CHUNK_A):
            cols = [_dot(ws_ref[g], v[c * CHUNK_A:(c + 1) * CHUNK_A, g * gw:(g + 1) * gw])
                    for g in range(A_GROUPS)]
            rows.append(jnp.concatenate(cols, axis=1) + bs_ref[...])
        s = jnp.concatenate(rows, axis=0)
        y = _dot((u * s).astype(BF16), wout_ref[...])
        o_ref[rs, :] = x_ref[rs, :] + m[2:3] * y
    _route_tail(o_ref[...], m, *route_in, *route_out)


def _gmlp_mixer(xs, mods, g, w_in, g_v, w_s, b_s, w_out, tile0, n_tiles, geo, route_w):
    split = isinstance(xs, tuple)
    d = xs[0].shape[1] if split else xs.shape[1]
    a = g_v.shape[0]
    gw = a // A_GROUPS
    bias = jnp.repeat(b_s.T, gw, axis=1)
    grp = functools.partial(_tile_group, geo=geo, tile0=tile0)
    const2 = lambda t: (0, 0)
    r_in, r_out, r_shape, r_scratch = _route_specs(n_tiles * TM, d, route_w[1].shape[1])
    if split:
        assert tile0 == 0
        ctx_tiles = xs[0].shape[0] // TM
        x_specs = [pl.BlockSpec((TM, d), lambda t: (jnp.minimum(t, ctx_tiles - 1), 0)),
                   pl.BlockSpec((TM, d), lambda t: (jnp.maximum(t - ctx_tiles, 0), 0))]
        x_args, x_scratch = list(xs), [pltpu.VMEM((TM, d), F32)]
    else:
        ctx_tiles = 0
        x_specs, x_args, x_scratch = [pl.BlockSpec((TM, d), lambda t: (t + tile0, 0))], [xs], []
    out = pl.pallas_call(
        functools.partial(_gmlp_kernel, ctx_tiles=ctx_tiles),
        grid=(n_tiles,),
        in_specs=x_specs + [
                  pl.BlockSpec((1, 6, d), lambda t: (grp(t), 0, 0)),
                  pl.BlockSpec((1, d), const2),
                  pl.BlockSpec((d, 2 * a), const2),
                  pl.BlockSpec((1, a), const2),
                  pl.BlockSpec((A_GROUPS, CHUNK_A, CHUNK_A), lambda t: (0, 0, 0)),
                  pl.BlockSpec((CHUNK_A, a), const2),
                  pl.BlockSpec((a, d), const2)] + r_in,
        out_specs=[pl.BlockSpec((TM, d), lambda t: (t, 0))] + r_out,
        out_shape=[jax.ShapeDtypeStruct((n_tiles * TM, d), F32)] + r_shape,
        scratch_shapes=r_scratch + x_scratch,
        compiler_params=_cparams(("arbitrary",)),
        name="gmlp_mixer",
    )(*x_args, mods, g.reshape(1, d), w_in.astype(BF16), g_v.reshape(1, a), w_s.astype(BF16),
      bias, w_out.astype(BF16), *_route_args(route_w, d))
    return out[0], out[1:]


def _tile_group(t, geo, tile0=0):
    n_ctx_tiles, tiles_per_batch, nb = geo
    tt = t + tile0
    return jnp.where(tt < n_ctx_tiles, nb, (tt - n_ctx_tiles) // tiles_per_batch)


def _dot_f32(tri_bf16, x):
    hi = x.astype(BF16)
    r1 = x - hi.astype(F32)
    mid = r1.astype(BF16)
    lo = (r1 - mid.astype(F32)).astype(BF16)
    return _dot(tri_bf16, hi) + _dot(tri_bf16, mid) + _dot(tri_bf16, lo)


def _gla_proj_kernel(x_ref, m_ref, g_ref, wq_ref, wk_ref, wv_ref, wg_ref, wr_ref, wa_ref, ba_ref,
                     cos_ref, sin_ref, q_ref, k_ref, v_ref, go_ref, la_ref):
    m = m_ref[0]
    kdim = wq_ref.shape[1]
    hk = kdim // GLA_HEADS
    nf = hk // 4
    npart = x_ref.shape[0] // GLA_PROJ_PART
    parts = [slice(p * GLA_PROJ_PART, (p + 1) * GLA_PROJ_PART) for p in range(npart)]
    zs, qs, ks = [], [], []
    for rs in parts:
        h = _modulate(x_ref[rs, :], g_ref[...], m[0:1], m[1:2]).astype(BF16)
        zs.append(_dot(_dot(h, wr_ref[...]).astype(BF16), wa_ref[...]) + ba_ref[...])
        qs.append(_dot(h, wq_ref[...]) * np.float32(hk ** -0.5))
        ks.append(_dot(h, wk_ref[...]))
        v_ref[rs, :] = _dot(h, wv_ref[...]).astype(BF16)
        go_ref[rs, :] = _dot(h, wg_ref[...])
    for rs, z in zip(parts, zs):
        la_ref[rs, :] = jax.nn.log_sigmoid(z) * np.float32(1.0 / GLA_TAU)

    lane = lax.broadcasted_iota(I32, (GLA_PROJ_PART, kdim), 1)
    first = (lane % (2 * nf)) < nf
    for rs, q, k in zip(parts, qs, ks):
        cos = jnp.concatenate([cos_ref[rs, :]] * GLA_HEADS, axis=1)
        sin = jnp.concatenate([sin_ref[rs, :]] * GLA_HEADS, axis=1)

        def rope(t):
            up = pltpu.roll(t, kdim - nf, 1)
            dn = pltpu.roll(t, nf, 1)
            return t * cos + jnp.where(first, up, dn) * sin

        q_ref[rs, :] = rope(q)
        k_ref[rs, :] = rope(k)


def _gla_decays(q_ref, k_ref, la_ref, n, rev):
    c = GLA_CHUNK
    ri = lax.broadcasted_iota(I32, (c, c), 0)
    ci = lax.broadcasted_iota(I32, (c, c), 1)
    keep = (ci >= ri) if rev else (ci <= ri)
    rows = slice(n * c, (n + 1) * c)
    cum = _dot_f32(keep.astype(BF16), la_ref[rows, :])
    last = cum[0:1] if rev else cum[c - 1:c]
    q = q_ref[rows, :]
    k = k_ref[rows, :]
    mid = cum[c // 2:c // 2 + 1]
    return dict(rows=rows, keep=keep, dec=jnp.exp(last),
                q_dec=(q * jnp.exp(cum)).astype(BF16),
                q_att=(q * jnp.exp(cum - mid)).astype(BF16),
                k_att=(k * jnp.exp(mid - cum)).astype(BF16),
                k_end=(k * jnp.exp(last - cum)).astype(BF16))


def _gla_scan_kernel(qf_ref, kf_ref, vf_ref, laf_ref, qb_ref, kb_ref, vb_ref, lab_ref,
                     of_ref, ob_ref, stf_ref, stb_ref):
    s = pl.program_id(1)

    @pl.when(s == 0)
    def _():
        stf_ref[...] = jnp.zeros_like(stf_ref)
        stb_ref[...] = jnp.zeros_like(stb_ref)

    nch = qf_ref.shape[0] // GLA_CHUNK
    hk = qf_ref.shape[1] // GLA_HEADS
    hv = vf_ref.shape[1] // GLA_HEADS
    dirs = ((qf_ref, kf_ref, vf_ref, laf_ref, of_ref, stf_ref, False),
            (qb_ref, kb_ref, vb_ref, lab_ref, ob_ref, stb_ref, True))
    heads = [(slice(h * hk, (h + 1) * hk), slice(h * hv, (h + 1) * hv)) for h in range(GLA_HEADS)]
    for n in range(nch):
        w = [_gla_decays(q_ref, k_ref, la_ref, nch - 1 - n if rev else n, rev)
             for q_ref, k_ref, _, la_ref, _, _, rev in dirs]
        att = [[jnp.where(w[d]["keep"], _dot_nt(w[d]["q_att"][:, ks], w[d]["k_att"][:, ks]), 0.0).astype(BF16)
                for ks, _ in heads] for d in range(2)]
        for d, (_, _, v_ref, _, o_ref, st_ref, _) in enumerate(dirs):
            for h, (ks, vs) in enumerate(heads):
                o_ref[w[d]["rows"], vs] = (_dot(att[d][h], v_ref[w[d]["rows"], vs])
                                           + _dot_nt(w[d]["q_dec"][:, ks], st_ref[h].astype(BF16)))
        for d, (_, _, v_ref, _, _, st_ref, _) in enumerate(dirs):
            for h, (ks, vs) in enumerate(heads):
                st_ref[h] = (st_ref[h] * w[d]["dec"][:, ks]
                             + _dot_tn(v_ref[w[d]["rows"], vs], w[d]["k_end"][:, ks]))


def _gla_out_kernel(x_ref, m_ref, of_ref, ob_ref, go_ref, gn_ref, wout_ref, *route_refs):
    route_in, o_ref, route_out = route_refs[:3], route_refs[3], route_refs[4:]
    x = x_ref[...]
    m = m_ref[0]
    o = of_ref[...] + ob_ref[...]
    hv = gn_ref.shape[1]
    parts = [_rms(o[:, h * hv:(h + 1) * hv], gn_ref[...]) for h in range(GLA_HEADS)]
    o = jnp.concatenate(parts, axis=1)
    gate = go_ref[...]
    y = _dot((o * (gate * jax.nn.sigmoid(gate))).astype(BF16), wout_ref[...])
    x_new = x + m[2:3] * y
    o_ref[...] = x_new
    _route_tail(x_new, m, *route_in, *route_out)


def _gla_mixer(xs, mods, g, w_in, w_a2, b_a, g_o, w_out, geo, dims, route_w):
    b, seq, lctx = dims
    nt, d = xs.shape
    n_tiles = nt // TM
    kdim = w_a2.shape[2]
    vdim = g_o.shape[0] * GLA_HEADS
    hk = kdim // GLA_HEADS
    nf = hk // 4
    wq = w_in[:, :kdim].astype(BF16)
    wk = w_in[:, kdim:2 * kdim].astype(BF16)
    wv = w_in[:, 2 * kdim:2 * kdim + vdim].astype(BF16)
    wg = w_in[:, 2 * kdim + vdim:2 * kdim + 2 * vdim].astype(BF16)
    wr = jnp.pad(w_in[:, 2 * kdim + 2 * vdim:], ((0, 0), (0, 128 - 2 * GLA_RANK))).astype(BF16)
    wa = jnp.zeros((128, 2 * kdim), F32)
    wa = wa.at[:GLA_RANK, :kdim].set(w_a2[0]).at[GLA_RANK:2 * GLA_RANK, kdim:].set(w_a2[1]).astype(BF16)
    ba = b_a.reshape(1, 2 * kdim)

    tpos = np.arange(seq)
    freqs = np.power(np.float32(ROPE_BASE), -np.arange(nf, dtype=np.float32) / np.float32(nf)).astype(np.float32)
    ar = (tpos // GRID_W).astype(np.float32)[:, None] * freqs
    ac = (tpos % GRID_W).astype(np.float32)[:, None] * freqs
    cos = np.concatenate([np.cos(ar), np.cos(ar), np.cos(ac), np.cos(ac)], axis=1)
    sin = np.concatenate([-np.sin(ar), np.sin(ar), -np.sin(ac), np.sin(ac)], axis=1)
    cos = jnp.asarray(np.concatenate([np.ones((TM, hk), np.float32), cos], axis=0).astype(np.float32))
    sin = jnp.asarray(np.concatenate([np.zeros((TM, hk), np.float32), sin], axis=0).astype(np.float32))

    n_ctx_tiles, tiles_per_batch, _ = geo
    grp = functools.partial(_tile_group, geo=geo)

    def rope_blk(t):
        return (jnp.where(t < n_ctx_tiles, 0, 1 + (t - n_ctx_tiles) % tiles_per_batch), 0)

    const2 = lambda t: (0, 0)
    row = lambda t: (t, 0)
    q, k, v, go, la = pl.pallas_call(
        _gla_proj_kernel,
        grid=(n_tiles,),
        in_specs=[pl.BlockSpec((TM, d), row),
                  pl.BlockSpec((1, 6, d), lambda t: (grp(t), 0, 0)),
                  pl.BlockSpec((1, d), const2),
                  pl.BlockSpec((d, kdim), const2), pl.BlockSpec((d, kdim), const2),
                  pl.BlockSpec((d, vdim), const2), pl.BlockSpec((d, vdim), const2),
                  pl.BlockSpec((d, 128), const2), pl.BlockSpec((128, 2 * kdim), const2),
                  pl.BlockSpec((1, 2 * kdim), const2),
                  pl.BlockSpec((TM, hk), rope_blk), pl.BlockSpec((TM, hk), rope_blk)],
        out_specs=[pl.BlockSpec((TM, kdim), row), pl.BlockSpec((TM, kdim), row),
                   pl.BlockSpec((TM, vdim), row), pl.BlockSpec((TM, vdim), row),
                   pl.BlockSpec((TM, 2 * kdim), row)],
        out_shape=[jax.ShapeDtypeStruct((nt, kdim), F32), jax.ShapeDtypeStruct((nt, kdim), F32),
                   jax.ShapeDtypeStruct((nt, vdim), BF16), jax.ShapeDtypeStruct((nt, vdim), F32),
                   jax.ShapeDtypeStruct((nt, 2 * kdim), F32)],
        compiler_params=_cparams(("parallel",)),
        name="gla_proj",
    )(xs, mods, g.reshape(1, d), wq, wk, wv, wg, wr, wa, ba, cos, sin)

    tb = GLA_TB
    ctx_steps = lctx // tb
    lat_steps = seq // tb
    steps = ctx_steps + lat_steps

    def blk(bi, s, rev):
        if rev:
            cs, ls = ctx_steps - 1 - s, lat_steps - 1 - (s - ctx_steps)
        else:
            cs, ls = s, s - ctx_steps
        return jnp.where(s < ctx_steps, bi * ctx_steps + cs, b * ctx_steps + bi * lat_steps + ls)

    def dir_specs(rev):
        row_blk = lambda bi, s: (blk(bi, s, rev), 0)
        return [pl.BlockSpec((tb, kdim), row_blk), pl.BlockSpec((tb, kdim), row_blk),
                pl.BlockSpec((tb, vdim), row_blk),
                pl.BlockSpec((tb, kdim), lambda bi, s: (blk(bi, s, rev), 1 if rev else 0))]

    state = pltpu.VMEM((GLA_HEADS, vdim // GLA_HEADS, hk), F32)
    o_f, o_b = pl.pallas_call(
        _gla_scan_kernel,
        grid=(b, steps),
        in_specs=dir_specs(False) + dir_specs(True),
        out_specs=[pl.BlockSpec((tb, vdim), lambda bi, s: (blk(bi, s, False), 0)),
                   pl.BlockSpec((tb, vdim), lambda bi, s: (blk(bi, s, True), 0))],
        out_shape=[jax.ShapeDtypeStruct((nt, vdim), F32)] * 2,
        scratch_shapes=[state, state],
        compiler_params=_cparams(("parallel", "arbitrary")),
        name="gla_scan",
    )(q, k, v, la, q, k, v, la)

    r_in, r_out, r_shape, r_scratch = _route_specs(nt, d, route_w[1].shape[1])
    out = pl.pallas_call(
        _gla_out_kernel,
        grid=(n_tiles,),
        in_specs=[pl.BlockSpec((TM, d), row),
                  pl.BlockSpec((1, 6, d), lambda t: (grp(t), 0, 0)),
                  pl.BlockSpec((TM, vdim), row), pl.BlockSpec((TM, vdim), row),
                  pl.BlockSpec((TM, vdim), row),
                  pl.BlockSpec((1, vdim // GLA_HEADS), const2),
                  pl.BlockSpec((vdim, d), const2)] + r_in,
        out_specs=[pl.BlockSpec((TM, d), row)] + r_out,
        out_shape=[jax.ShapeDtypeStruct((nt, d), F32)] + r_shape,
        scratch_shapes=r_scratch,
        compiler_params=_cparams(("arbitrary",)),
        name="gla_out",
    )(xs, mods, o_f, o_b, go, g_o.reshape(1, -1), w_out.astype(BF16), *_route_args(route_w, d))
    return out[0], out[1:]


def _na_proj_kernel(x_ref, m_ref, g_ref, w_ref, q_ref, k_ref, v_ref):
    x = x_ref[...]
    m = m_ref[0]
    d = x.shape[1]
    h = _modulate(x, g_ref[...], m[0:1], m[1:2]).astype(BF16)
    qkv = _dot(h, w_ref[...])
    hd = d // NA_HEADS
    for p in range(q_ref.shape[0]):
        cs = slice(p * LANES, (p + 1) * LANES)
        q_ref[p] = (qkv[:, cs] * np.float32(hd ** -0.5)).astype(BF16)
        k_ref[p] = qkv[:, d + p * LANES:d + (p + 1) * LANES].astype(BF16)
        v_ref[p] = qkv[:, 2 * d + p * LANES:2 * d + (p + 1) * LANES].astype(BF16)


def _na_attn_kernel(tbl_ref, q_ref, *refs, n_tiles, nwin):
    k_refs, v_refs = refs[:nwin], refs[nwin:2 * nwin]
    kc_ref, vc_ref, bias_ref, o_ref, s_ref, p_ref = refs[2 * nwin:]
    t = pl.program_id(2)
    typ = jnp.where(t == 0, 0, jnp.where(t == n_tiles - 1, 2, 1))
    npair = NA_WROWS // 2
    nlat = NA_WROWS * GRID_W
    hd = LANES // 2
    lane = lax.broadcasted_iota(I32, q_ref.shape[1:], 1)
    npp, tq = q_ref.shape[0], q_ref.shape[1]
    for pp in range(npp):
        q = q_ref[pp]
        k_all = jnp.concatenate([r[pp] for r in k_refs] + [kc_ref[pp]], axis=0)
        zero = jnp.zeros_like(q)
        q2 = jnp.concatenate([jnp.where(lane < hd, q, zero), jnp.where(lane >= hd, q, zero)], axis=0)
        s_ref[pp] = _dot_nt(q2, k_all)
    for pp in range(npp):
        for hh in range(2):
            for dr in range(NA_QROWS):
                rs = slice(hh * tq + dr * GRID_W, hh * tq + (dr + 1) * GRID_W)
                lat = [s_ref[pp, rs, j * LANES:(j + 1) * LANES]
                       + bias_ref[2 * pp + hh, tbl_ref[typ * (NA_QROWS * npair) + dr * npair + j]]
                       for j in range(npair)]
                sb = jnp.concatenate(lat + [s_ref[pp, rs, nlat:]], axis=1)
                e = jnp.exp(sb - jnp.max(sb, axis=1, keepdims=True))
                p_ref[pp, rs, :] = (e * (1.0 / jnp.sum(e, axis=1, keepdims=True))).astype(BF16)
    outs = []
    for pp in range(npp):
        v_all = jnp.concatenate([r[pp] for r in v_refs] + [vc_ref[pp]], axis=0)
        o2 = _dot(p_ref[pp], v_all)
        outs.append(jnp.where(lane < hd, o2[:tq], o2[tq:]))
    o_ref[...] = jnp.concatenate(outs, axis=1).astype(BF16)


def _na_out_kernel(x_ref, m_ref, a_ref, w_ref, *route_refs):
    route_in, o_ref, route_out = route_refs[:3], route_refs[3], route_refs[4:]
    m = m_ref[0]
    x_new = x_ref[...] + m[2:3] * _dot(a_ref[...], w_ref[...])
    o_ref[...] = x_new
    _route_tail(x_new, m, *route_in, *route_out)


def _na_tables(rows):
    npair = NA_WROWS // 2
    tbl = np.zeros((3, NA_QROWS, npair), np.int32)
    for typ, r_base in enumerate((0, NA_QROWS, rows - NA_QROWS)):
        w0 = int(np.clip(r_base - NA_KH // 2, 0, rows - NA_WROWS))
        for dr in range(NA_QROWS):
            r = r_base + dr
            r0 = int(np.clip(r - NA_KH // 2, 0, rows - NA_KH))
            for j in range(npair):
                kr = (w0 + 2 * j, w0 + 2 * j + 1)
                ok = [r0 <= x < r0 + NA_KH for x in kr]
                ri = [x - r + NA_KH - 1 for x in kr]
                if ok[0] and ok[1]:
                    e = ri[0]
                elif ok[0]:
                    e = 16 + ri[0]
                elif ok[1]:
                    e = 32 + ri[1]
                else:
                    e = 63
                tbl[typ, dr, j] = e
    return tbl.reshape(-1)


def _na_bias_kernel(rpb_ref, o_ref, *, nr, nw):
    h = pl.program_id(0)
    c = lax.broadcasted_iota(I32, (GRID_W, GRID_W), 0)
    k = lax.broadcasted_iota(I32, (GRID_W, GRID_W), 1)
    cstart = jnp.clip(c - NA_KW // 2, 0, GRID_W - NA_KW)
    ok = jnp.logical_and(k >= cstart, k < cstart + NA_KW)
    rel = k - c + NA_KW - 1
    neg = jnp.full((GRID_W, GRID_W), NEG_INF, F32)
    cbs = []
    for r in range(nr):
        acc = neg
        for w in range(nw):
            acc = jnp.where(rel == w, rpb_ref[(h * nr + r) * nw + w], acc)
        cbs.append(jnp.where(ok, acc, neg))
    cbs += [neg] * (17 - nr)
    for r in range(16):
        o_ref[0, r] = jnp.concatenate([cbs[r], cbs[r + 1]], axis=1)
        o_ref[0, 16 + r] = jnp.concatenate([cbs[r], neg], axis=1)
        o_ref[0, 32 + r] = jnp.concatenate([neg, cbs[r]], axis=1)
        o_ref[0, 48 + r] = jnp.concatenate([neg, neg], axis=1)


def _na_bias_table(rpb):
    nh, nr, nw = rpb.shape
    return pl.pallas_call(
        functools.partial(_na_bias_kernel, nr=nr, nw=nw),
        grid=(nh,),
        in_specs=[pl.BlockSpec(memory_space=pltpu.SMEM)],
        out_specs=pl.BlockSpec((1, 64, GRID_W, 2 * GRID_W), lambda h: (h, 0, 0, 0)),
        out_shape=jax.ShapeDtypeStruct((nh, 64, GRID_W, 2 * GRID_W), F32),
        compiler_params=_cparams(("parallel",)),
        name="na_bias_table",
    )(rpb.reshape(-1).astype(F32))


def _na_mixer(xs, mods, g, w_qkv, rpb, w_out, geo, dims, route_w):
    b, seq, lctx = dims
    nt, d = xs.shape
    n_tiles_tok = nt // TM
    grp = functools.partial(_tile_group, geo=geo)
    npairs = d // 128
    const2 = lambda t: (0, 0)
    q, k, v = pl.pallas_call(
        _na_proj_kernel,
        grid=(n_tiles_tok,),
        in_specs=[pl.BlockSpec((TM, d), lambda t: (t, 0)),
                  pl.BlockSpec((1, 6, d), lambda t: (grp(t), 0, 0)),
                  pl.BlockSpec((1, d), const2),
                  pl.BlockSpec((d, 3 * d), const2)],
        out_specs=[pl.BlockSpec((npairs, TM, 128), lambda t: (0, t, 0))] * 3,
        out_shape=[jax.ShapeDtypeStruct((npairs, nt, 128), BF16)] * 3,
        compiler_params=_cparams(("parallel",)),
        name="na_proj",
    )(xs, mods, g.reshape(1, d), w_qkv.astype(BF16))

    rows = seq // GRID_W
    tq = NA_QROWS * GRID_W
    n_tiles = rows // NA_QROWS
    wb = 256
    nwin = NA_WROWS * GRID_W // wb
    lat0 = b * lctx
    tbl = jnp.asarray(_na_tables(rows))
    bias = _na_bias_table(rpb)

    pp = NA_PAIRS_PER_STEP
    assert npairs % pp == 0 and (NA_QROWS * GRID_W) % wb == 0 and (NA_KH // 2 * GRID_W) % wb == 0

    def win(i):
        def f(p, bi, t, tbl_ref):
            w = jnp.clip(t * (NA_QROWS * GRID_W // wb) - NA_KH // 2 * GRID_W // wb, 0, seq // wb - nwin)
            return (p, (lat0 + bi * seq) // wb + w + i, 0)
        return f

    kv_specs = [pl.BlockSpec((pp, wb, 128), win(i)) for i in range(nwin)]
    attn = pl.pallas_call(
        functools.partial(_na_attn_kernel, n_tiles=n_tiles, nwin=nwin),
        grid_spec=pltpu.PrefetchScalarGridSpec(
            num_scalar_prefetch=1,
            grid=(npairs // pp, b, n_tiles),
            in_specs=[pl.BlockSpec((pp, tq, 128), lambda p, bi, t, tr: (p, (lat0 + bi * seq) // tq + t, 0))]
                     + kv_specs + kv_specs
                     + [pl.BlockSpec((pp, lctx, 128), lambda p, bi, t, tr: (p, bi, 0)),
                        pl.BlockSpec((pp, lctx, 128), lambda p, bi, t, tr: (p, bi, 0)),
                        pl.BlockSpec((2 * pp, 64, GRID_W, 2 * GRID_W), lambda p, bi, t, tr: (p, 0, 0, 0),
                                     pipeline_mode=pl.Buffered(1))],
            out_specs=pl.BlockSpec((tq, 128 * pp), lambda p, bi, t, tr: (bi * n_tiles + t, p)),
            scratch_shapes=[pltpu.VMEM((pp, 2 * tq, NA_WROWS * GRID_W + lctx), F32),
                            pltpu.VMEM((pp, 2 * tq, NA_WROWS * GRID_W + lctx), BF16)]),
        out_shape=jax.ShapeDtypeStruct((b * seq, d), BF16),
        compiler_params=_cparams(("parallel", "parallel", "arbitrary")),
        name="na_attn",
    )(tbl, q, *([k] * nwin), *([v] * nwin), k, v, bias)

    n_lat_tiles = b * seq // TM
    tile0 = lat0 // TM
    grp_l = functools.partial(_tile_group, geo=geo, tile0=tile0)
    r_in, r_out, r_shape, r_scratch = _route_specs(b * seq, d, route_w[1].shape[1])
    out = pl.pallas_call(
        _na_out_kernel,
        grid=(n_lat_tiles,),
        in_specs=[pl.BlockSpec((TM, d), lambda t: (t + tile0, 0)),
                  pl.BlockSpec((1, 6, d), lambda t: (grp_l(t), 0, 0)),
                  pl.BlockSpec((TM, d), lambda t: (t, 0)),
                  pl.BlockSpec((d, d), const2)] + r_in,
        out_specs=[pl.BlockSpec((TM, d), lambda t: (t, 0))] + r_out,
        out_shape=[jax.ShapeDtypeStruct((b * seq, d), F32)] + r_shape,
        scratch_shapes=r_scratch,
        compiler_params=_cparams(("arbitrary",)),
        name="na_out",
    )(xs, mods, attn, w_out.astype(BF16), *_route_args(route_w, d))
    return out[0], out[1:]


SC_CORES = 2
SC_SUBCORES = 16
SC_WORKERS = SC_CORES * SC_SUBCORES
SC_MAX_CHUNK = 96
SLOT_T_MAX = 2048


def _sc_mesh():
    return plsc.VectorSubcoreMesh(core_axis_name="c", subcore_axis_name="s")


def _sc_chunk(per_worker, max_chunk):
    return max(c for c in range(8, max_chunk + 1, 8) if per_worker % (2 * c) == 0)


def _sc_gather_rows(table, idx):
    dd = table.shape[1]
    bsz = idx.shape[0]
    per_w = bsz // SC_WORKERS
    assert per_w * SC_WORKERS == bsz
    chunk = _sc_chunk(per_w, SC_MAX_CHUNK)
    n_chunks = per_w // chunk

    def body(table_hbm, idx_hbm, out_hbm, idx_v, rows0, rows1, g0, g1, w0, w1):
        wid = lax.axis_index("s") * SC_CORES + lax.axis_index("c")
        pltpu.sync_copy(idx_hbm.at[wid], idx_v)
        base = wid * per_w

        def out_rows(j):
            return out_hbm.at[pl.ds(pl.multiple_of(base + j * chunk, 8), chunk)]

        def step(i, carry):
            j0, j1 = 2 * i, 2 * i + 1
            ga = pltpu.async_copy(table_hbm.at[idx_v.at[j0]], rows0, g0)
            gb = pltpu.async_copy(table_hbm.at[idx_v.at[j1]], rows1, g1)
            ga.wait()
            wa = pltpu.async_copy(rows0, out_rows(j0), w0)
            gb.wait()
            wb = pltpu.async_copy(rows1, out_rows(j1), w1)
            wa.wait()
            wb.wait()
            return carry

        lax.fori_loop(0, n_chunks // 2, step, 0)

    return pl.kernel(
        body, out_type=jax.ShapeDtypeStruct((bsz, dd), table.dtype), mesh=_sc_mesh(),
        scratch_types=[pltpu.VMEM((n_chunks, chunk), I32),
                       pltpu.VMEM((chunk, dd), table.dtype), pltpu.VMEM((chunk, dd), table.dtype),
                       pltpu.SemaphoreType.DMA, pltpu.SemaphoreType.DMA,
                       pltpu.SemaphoreType.DMA, pltpu.SemaphoreType.DMA],
        name="sc_gather_rows",
    )(table, idx.reshape(SC_WORKERS, n_chunks, chunk))


def _sc_scatter_rows(rows, dest, n_out):
    n, dd = rows.shape
    kk = dest.shape[0]
    per_w = n // SC_WORKERS
    assert per_w * SC_WORKERS == n
    chunk = _sc_chunk(per_w, SC_MAX_CHUNK)
    n_chunks = per_w // chunk
    dest_w = dest.reshape(kk, SC_WORKERS, n_chunks, chunk).transpose(1, 2, 0, 3)
    dest_w = dest_w.reshape(SC_WORKERS, n_chunks * kk, chunk)

    def body(rows_hbm, dest_hbm, out_hbm, idx_v, rows0, rows1, r0, r1, s0, s1):
        wid = lax.axis_index("s") * SC_CORES + lax.axis_index("c")
        pltpu.sync_copy(dest_hbm.at[wid], idx_v)
        base = wid * per_w

        def in_rows(j):
            return rows_hbm.at[pl.ds(pl.multiple_of(base + j * chunk, 8), chunk)]

        def step(i, carry):
            j0, j1 = 2 * i, 2 * i + 1
            ra = pltpu.async_copy(in_rows(j0), rows0, r0)
            rb = pltpu.async_copy(in_rows(j1), rows1, r1)
            ra.wait()
            sa = [pltpu.async_copy(rows0, out_hbm.at[idx_v.at[j0 * kk + k]], s0) for k in range(kk)]
            rb.wait()
            sb = [pltpu.async_copy(rows1, out_hbm.at[idx_v.at[j1 * kk + k]], s1) for k in range(kk)]
            for cp in sa + sb:
                cp.wait()
            return carry

        lax.fori_loop(0, n_chunks // 2, step, 0)

    return pl.kernel(
        body, out_type=jax.ShapeDtypeStruct((n_out, dd), rows.dtype), mesh=_sc_mesh(),
        scratch_types=[pltpu.VMEM((n_chunks * kk, chunk), I32),
                       pltpu.VMEM((chunk, dd), rows.dtype), pltpu.VMEM((chunk, dd), rows.dtype),
                       pltpu.SemaphoreType.DMA, pltpu.SemaphoreType.DMA,
                       pltpu.SemaphoreType.DMA, pltpu.SemaphoreType.DMA],
        name="sc_scatter_rows",
    )(rows, dest_w)


def _prefix_sum_rows(col):
    nr = col.shape[0]
    acc = jnp.broadcast_to(col, (nr, LANES))
    row = lax.broadcasted_iota(I32, (nr, LANES), 0)
    s = 1
    while s < nr:
        acc = acc + jnp.where(row >= s, pltpu.roll(acc, s, 0), 0.0)
        s *= 2
    return acc[:, 0:1]


def _slot_kernel(idx_ref, rank_ref, cnt_ref, dest_ref, btab_ref, *, bm):
    cnt = cnt_ref[...]
    ne = cnt.shape[0]
    padded = jnp.floor((cnt + (bm - 1.0)) * (1.0 / bm)) * bm
    pad_end = _prefix_sum_rows(padded)
    pad_start = pad_end - padded
    idx = idx_ref[...]
    e_iota = lax.broadcasted_iota(I32, (ne, idx.shape[1]), 0)
    starts = [jnp.sum(jnp.where(e_iota == idx[k:k + 1], pad_start, 0.0), axis=0, keepdims=True)
              for k in range(idx.shape[0])]
    dest_ref[...] = jnp.concatenate(starts, axis=0).astype(I32) + rank_ref[...]

    nbp = btab_ref.shape[1]
    bstart = lax.broadcasted_iota(I32, (ne, nbp), 1).astype(F32) * bm
    be = jnp.minimum(jnp.sum((pad_end <= bstart).astype(F32), axis=0, keepdims=True), ne - 1.0)
    mine = lax.broadcasted_iota(I32, (ne, nbp), 0).astype(F32) == be
    pick = lambda col: jnp.sum(jnp.where(mine, col, 0.0), axis=0, keepdims=True)
    live = jnp.clip(pick(pad_start + cnt) - bstart[0:1], 0.0, bm)
    r = lax.broadcasted_iota(I32, (ne, ne), 0)
    c = lax.broadcasted_iota(I32, (ne, ne), 1)
    ends_on_lanes = _dot_f32(jnp.ones((ne, ne), BF16), jnp.where(r == c, pad_end, 0.0))
    nxt_e = jnp.sum((ends_on_lanes <= pad_end).astype(F32), axis=1, keepdims=True)
    nxt_e = jnp.where(nxt_e < ne, nxt_e, -1.0)
    ordinal = _prefix_sum_rows((cnt > 0).astype(F32)) - 1.0
    slot_e = ordinal - 2.0 * jnp.floor(ordinal * 0.5)
    rowi = lax.broadcasted_iota(I32, btab_ref.shape, 0)
    tab = jnp.where(rowi == 0, be, jnp.where(rowi == 1, live, jnp.where(rowi == 2, pick(nxt_e), pick(slot_e))))
    btab_ref[...] = tab.astype(I32)


def _route_specs(n, d, ne):
    const2 = lambda t: (0, 0)
    in_specs = [pl.BlockSpec((1, d), const2), pl.BlockSpec((ne, d), const2), pl.BlockSpec((ne, 1), const2)]
    out_specs = [pl.BlockSpec((TM, d // 2), lambda t: (t, 0)),
                 pl.BlockSpec((TOP_K, TM), lambda t: (0, t)),
                 pl.BlockSpec((TM, SUBLANES), lambda t: (t, 0)),
                 pl.BlockSpec((TOP_K, TM), lambda t: (0, t)),
                 pl.BlockSpec((ne, 1), const2)]
    out_shape = [jax.ShapeDtypeStruct((n, d // 2), jnp.uint32),
                 jax.ShapeDtypeStruct((TOP_K, n), I32),
                 jax.ShapeDtypeStruct((n, SUBLANES), F32),
                 jax.ShapeDtypeStruct((TOP_K, n), I32),
                 jax.ShapeDtypeStruct((ne, 1), F32)]
    return in_specs, out_specs, out_shape, [pltpu.VMEM((ne, 1), F32)]


def _route_args(route_w, d):
    g_ffn, w_router, b_router = route_w
    ne = w_router.shape[1]
    return g_ffn.reshape(1, d), w_router.T.astype(BF16), b_router.reshape(ne, 1)


def _route_tail(x, m, g_ref, wr_ref, br_ref, h_ref, idx_ref, gate_ref, rank_ref, cnt_ref, run_ref):
    t = pl.program_id(0)

    @pl.when(t == 0)
    def _():
        run_ref[...] = jnp.zeros_like(run_ref)

    h = _modulate(x, g_ref[...], m[3:4], m[4:5]).astype(BF16)
    h_ref[...] = _pack_bf16_pairs(h)
    logits = _dot_nt(wr_ref[...], h) + br_ref[...]
    ne, tm = logits.shape
    e_iota = lax.broadcasted_iota(I32, (ne, tm), 0)
    vals, idxs = [], []
    l = logits
    for _ in range(TOP_K):
        mk = jnp.max(l, axis=0, keepdims=True)
        ik = jnp.min(jnp.where(l == mk, e_iota, ne), axis=0, keepdims=True)
        vals.append(mk)
        idxs.append(ik)
        l = jnp.where(e_iota == ik, -jnp.inf, l)
    top_val = jnp.concatenate(vals, axis=0)
    ex = jnp.exp(top_val - vals[0])
    gates = ex / jnp.sum(ex, axis=0, keepdims=True)
    pad = jnp.zeros((gate_ref.shape[1] - TOP_K, tm), F32)
    gate_ref[...] = jnp.concatenate([gates, pad], axis=0).T
    idx_ref[...] = jnp.concatenate(idxs, axis=0)

    hits = [e_iota == ik for ik in idxs]
    cnt = hits[0].astype(F32)
    for hk in hits[1:]:
        cnt = cnt + hk.astype(F32)
    si = lax.broadcasted_iota(I32, (tm, tm), 0)
    ti = lax.broadcasted_iota(I32, (tm, tm), 1)
    before = (si < ti).astype(BF16)
    total = _dot(cnt.astype(BF16), before) + run_ref[...]
    ranks = [jnp.sum(jnp.where(hk, total, 0.0), axis=0, keepdims=True) for hk in hits]
    rank_ref[...] = jnp.concatenate(ranks, axis=0).astype(I32)
    run_ref[...] = run_ref[...] + jnp.sum(cnt, axis=1, keepdims=True)
    cnt_ref[...] = run_ref[...]


def _pack_bf16_pairs(v):
    bits = lax.bitcast_convert_type(v.astype(BF16).astype(F32), jnp.uint32)
    half = bits.shape[1] // 2
    return (bits[:, half:] & jnp.uint32(0xFFFF0000)) | (bits[:, :half] >> 16)


def _unpack_bf16_pairs(w):
    return (lax.bitcast_convert_type(w << 16, F32),
            lax.bitcast_convert_type(w & jnp.uint32(0xFFFF0000), F32))


def _expert_kernel(tab_ref, x_ref, wgu_hbm, bgu_ref, wd_hbm, bd_ref, o_ref,
                   wgu_in, wd_in, wgu_bf, wd_bf, act_ref, sem_gu, sem_d, *, layer):
    blk = pl.program_id(0)
    ch = wgu_bf.shape[2]
    f = act_ref.shape[1]
    nc = f // ch
    expert, live, nxt, wslot = (tab_ref[i, blk] for i in range(4))
    used = live > 0
    fresh = jnp.logical_or(blk == 0, expert != tab_ref[0, jnp.maximum(blk - 1, 0)])

    def fetch(e, slot):
        return (pltpu.make_async_copy(wgu_hbm.at[layer, e], wgu_in.at[slot], sem_gu.at[slot]),
                pltpu.make_async_copy(wd_hbm.at[layer, e], wd_in.at[slot], sem_d.at[slot]))

    @pl.when(jnp.logical_and(used, fresh))
    def _():
        slot = wslot
        mine = fetch(expert, slot)

        @pl.when(blk == 0)
        def _():
            for cp in mine:
                cp.start()

        @pl.when(nxt >= 0)
        def _():
            for cp in fetch(nxt, 1 - slot):
                cp.start()

        for cp in mine:
            cp.wait()

    bm = x_ref.shape[0]
    fresh_full = jnp.logical_and(fresh, live > bm - MOE_SUB)

    def cast_gu(j):
        wgu_bf[j] = wgu_in[wslot, :, j * ch:(j + 1) * ch].astype(BF16)

    def cast_d(j):
        wd_bf[j] = wd_in[wslot, :, j * ch:(j + 1) * ch].astype(BF16)

    @pl.when(jnp.logical_and(jnp.logical_and(used, fresh), jnp.logical_not(fresh_full)))
    def _():
        for j in range(wgu_bf.shape[0]):
            cast_gu(j)
        for j in range(wd_bf.shape[0]):
            cast_d(j)

    def run(nr, cast):
        w = x_ref[:nr, :]
        row = lax.broadcasted_iota(I32, w.shape, 0)
        w = jnp.where(row < live, w, jnp.zeros_like(w))
        lo, hi = _unpack_bf16_pairs(w)
        x = jnp.concatenate([lo.astype(BF16), hi.astype(BF16)], axis=1)
        for c in range(nc):
            c0 = slice(c * ch, (c + 1) * ch)
            c1 = slice(f + c * ch, f + (c + 1) * ch)
            if cast:
                cast_gu(c)
                cast_gu(nc + c)
            glu = _dot(x, wgu_bf[c]) + bgu_ref[0, 0, :, c0]
            lin = _dot(x, wgu_bf[nc + c]) + bgu_ref[0, 0, :, c1]
            glu = jnp.minimum(glu, SWIGLU_LIMIT)
            lin = jnp.clip(lin, -SWIGLU_LIMIT, SWIGLU_LIMIT)
            act_ref[:nr, c0] = (glu * jax.nn.sigmoid(SWIGLU_ALPHA * glu) * (lin + 1.0)).astype(BF16)
        a = act_ref[:nr, :]
        ys = []
        for n in range(wd_bf.shape[0]):
            if cast:
                cast_d(n)
            ys.append(_dot(a, wd_bf[n]))
        o_ref[:nr, :] = _pack_bf16_pairs(jnp.concatenate(ys, axis=1) + bd_ref[0, 0])

    for nr in range(MOE_SUB, bm + 1, MOE_SUB):
        @pl.when(jnp.logical_and(jnp.logical_and(live > nr - MOE_SUB, live <= nr), jnp.logical_not(fresh_full)))
        def _(nr=nr):
            run(nr, False)
            if nr < bm:
                o_ref[nr:, :] = jnp.zeros((bm - nr, o_ref.shape[1]), o_ref.dtype)

    @pl.when(fresh_full)
    def _():
        run(bm, True)

    @pl.when(jnp.logical_not(used))
    def _():
        o_ref[...] = jnp.zeros_like(o_ref)


def _expert_call(tab, buf, w_gu, b_gu, w_down, b_down, layer, bm):
    n_slots, dw = buf.shape
    _, ne, d, f2 = w_gu.shape
    f = f2 // 2
    n_blocks = n_slots // bm
    lw = lambda i, tb: (layer, tb[0, i], 0, 0)
    row = lambda i, tb: (i, 0)
    cw = MXU_COLS
    return pl.pallas_call(
        functools.partial(_expert_kernel, layer=layer),
        grid_spec=pltpu.PrefetchScalarGridSpec(
            num_scalar_prefetch=1,
            grid=(n_blocks,),
            in_specs=[pl.BlockSpec((bm, dw), row),
                      pl.BlockSpec(memory_space=pl.ANY),
                      pl.BlockSpec((1, 1, 1, f2), lw),
                      pl.BlockSpec(memory_space=pl.ANY),
                      pl.BlockSpec((1, 1, 1, d), lw)],
            out_specs=pl.BlockSpec((bm, dw), row),
            scratch_shapes=[pltpu.VMEM((2, d, f2), F32), pltpu.VMEM((2, f, d), F32),
                            pltpu.VMEM((f2 // cw, d, cw), BF16), pltpu.VMEM((d // cw, f, cw), BF16),
                            pltpu.VMEM((bm, f), BF16),
                            pltpu.SemaphoreType.DMA((2,)), pltpu.SemaphoreType.DMA((2,))]),
        out_shape=jax.ShapeDtypeStruct((n_slots, dw), jnp.uint32),
        compiler_params=_cparams(("arbitrary",)),
        name="moe_experts",
    )(tab, buf, w_gu, b_gu.reshape(b_gu.shape[0], ne, 1, f2), w_down, b_down.reshape(b_down.shape[0], ne, 1, d))


def _combine_kernel(x_ref, m_ref, y_ref, gate_ref, gout_ref, o_ref, *, final):
    m = m_ref[0]
    gate = gate_ref[...]
    half = y_ref.shape[2]
    f_lo, f_hi = None, None
    for k in range(TOP_K):
        lo, hi = _unpack_bf16_pairs(y_ref[k])
        gk = gate[:, k:k + 1]
        f_lo = gk * lo if f_lo is None else f_lo + gk * lo
        f_hi = gk * hi if f_hi is None else f_hi + gk * hi
    x_lo = x_ref[:, :half] + m[5:6, :half] * f_lo
    x_hi = x_ref[:, half:] + m[5:6, half:] * f_hi
    if final:
        ms = (jnp.sum(x_lo * x_lo, axis=-1, keepdims=True) + jnp.sum(x_hi * x_hi, axis=-1, keepdims=True))
        r = lax.rsqrt(ms * (0.5 / half) + NORM_EPS)
        x_lo = x_lo * r * gout_ref[:, :half]
        x_hi = x_hi * r * gout_ref[:, half:]
    o_ref[:, :half] = x_lo
    o_ref[:, half:] = x_hi


def _moe_layer(xs, routed, mods, layer, w_gu, b_gu, w_down, b_down, geo, g_out, final):
    n, d = xs.shape
    h, idx, gate, rank, cnt = routed
    assert h.shape[0] == n
    ne = cnt.shape[0]
    n_tiles = n // TM
    grp = functools.partial(_tile_group, geo=geo)
    const2 = lambda t: (0, 0)

    bm = MOE_BM
    n_rows = n * TOP_K
    n_blocks = -(-n_rows // bm) + ne
    n_slots = n_blocks * bm
    nbp = -(-n_blocks // LANES) * LANES
    slot_t = max(w for w in range(LANES, SLOT_T_MAX + 1, LANES) if n % w == 0)
    dest, btab = pl.pallas_call(
        functools.partial(_slot_kernel, bm=bm),
        grid=(n // slot_t,),
        in_specs=[pl.BlockSpec((TOP_K, slot_t), lambda t: (0, t)),
                  pl.BlockSpec((TOP_K, slot_t), lambda t: (0, t)),
                  pl.BlockSpec((ne, 1), const2)],
        out_specs=[pl.BlockSpec((TOP_K, slot_t), lambda t: (0, t)),
                   pl.BlockSpec((8, nbp), const2)],
        out_shape=[jax.ShapeDtypeStruct((TOP_K, n), I32),
                   jax.ShapeDtypeStruct((8, nbp), I32)],
        compiler_params=_cparams(("arbitrary",)),
        name="moe_slots",
    )(idx, rank, cnt)

    buf = _sc_scatter_rows(h, dest, n_slots)
    y = _expert_call(btab, buf, w_gu, b_gu, w_down, b_down, layer, bm)
    dy = d // 2
    yk = _sc_gather_rows(y, dest.reshape(-1)).reshape(TOP_K, n, dy)
    return pl.pallas_call(
        functools.partial(_combine_kernel, final=final),
        grid=(n_tiles,),
        in_specs=[pl.BlockSpec((TM, d), lambda t: (t, 0)),
                  pl.BlockSpec((1, 6, d), lambda t: (grp(t), 0, 0)),
                  pl.BlockSpec((TOP_K, TM, dy), lambda t: (0, t, 0)),
                  pl.BlockSpec((TM, SUBLANES), lambda t: (t, 0)),
                  pl.BlockSpec((1, d), const2)],
        out_specs=pl.BlockSpec((TM, d), lambda t: (t, 0)),
        out_shape=jax.ShapeDtypeStruct((n, d), F32),
        compiler_params=_cparams(("parallel",)),
        name="moe_combine",
    )(xs, mods, yk, gate, g_out.reshape(1, d))


def kernel(x, c, ctx, c_ctx, ada_w, ada_b, norm_mix, norm_ffn, norm_out, a_w_in, a_g_v, a_w_s, a_b_s, a_w_out, b_w_in, b_w_a2, b_b_a, b_g_o, b_w_out, c_w_qkv, c_rpb, c_w_out, moe_w_router, moe_b_router, moe_w_gu, moe_b_gu, moe_w_down, moe_b_down):
    b, seq, d = x.shape
    lctx = ctx.shape[1]
    depth = ada_w.shape[0]
    assert (b * lctx) % TM == 0 and seq % TM == 0 and lctx % GLA_TB == 0 and seq % GLA_TB == 0
    assert seq % (NA_QROWS * GRID_W) == 0 and (b * lctx) % (NA_QROWS * GRID_W) == 0 and lctx % 256 == 0
    assert seq // GRID_W >= NA_WROWS + NA_QROWS and b + 1 <= SUBLANES
    geo = (b * lctx // TM, seq // TM, b)
    dims = (b, seq, lctx)

    cond_t = jnp.zeros((d, SUBLANES), F32).at[:, :b].set(c.T).at[:, b].set(c_ctx)
    mods_all = _adaln(cond_t, b + 1, ada_w, ada_b)[:, :b + 1].reshape(depth, b + 1, 6, d)

    pair = (ctx.reshape(b * lctx, d), x.reshape(b * seq, d))
    n_tiles_all = (b * lctx + b * seq) // TM
    ctx_tiles = geo[0]
    has_ctx = True
    for i in range(depth):
        kind, j = i % N_MIXERS, i // N_MIXERS
        ctx_later = any(kk % N_MIXERS != 0 for kk in range(i + 1, depth))
        mods = mods_all[i]
        route_w = (norm_ffn[i], moe_w_router[i], moe_b_router[i])
        if i == 0 and not (kind == 0 and ctx_later):
            xs = jnp.concatenate(pair, axis=0)
        if kind == 0:
            keep_ctx = has_ctx and ctx_later
            skip = 0 if keep_ctx or not has_ctx else ctx_tiles
            geo_i = geo if has_ctx else (0, geo[1], b)
            src = pair if i == 0 and keep_ctx else xs
            n_src = n_tiles_all if i == 0 else xs.shape[0] // TM
            xs, routed = _gmlp_mixer(src, mods, norm_mix[i], a_w_in[j], a_g_v[j], a_w_s[j], a_b_s[j], a_w_out[j],
                                     skip, n_src - skip, geo_i, route_w)
            has_ctx = keep_ctx
        elif kind == 1:
            assert has_ctx
            if not ctx_later:
                raise NotImplementedError("GLA mixer whose context rows are dropped before the FFN")
            xs, routed = _gla_mixer(xs, mods, norm_mix[i], b_w_in[j], b_w_a2[j], b_b_a[j], b_g_o[j], b_w_out[j],
                                    geo, dims, route_w)
        else:
            assert has_ctx
            if ctx_later:
                raise NotImplementedError("context output of the neighbourhood mixer")
            xs, routed = _na_mixer(xs, mods, norm_mix[i], c_w_qkv[j], c_rpb[j], c_w_out[j], geo, dims, route_w)
            has_ctx = False
        geo_i = geo if has_ctx else (0, geo[1], b)
        xs = _moe_layer(xs, routed, mods, i, moe_w_gu, moe_b_gu, moe_w_down, moe_b_down, geo_i, norm_out,
                        i == depth - 1)
    if has_ctx:
        xs = xs[b * lctx:]
    return xs.reshape(b, seq, d)
```

```python
import functools

import numpy as np
import jax
import jax.numpy as jnp
from jax import lax
from jax.experimental import pallas as pl
from jax.experimental.pallas import tpu as pltpu
from jax.experimental.pallas import tpu_sc as plsc

F32 = jnp.float32
BF16 = jnp.bfloat16
I32 = jnp.int32

LANES = 128
SUBLANES = 8
MXU_COLS = 256

NORM_EPS = 1e-6
GRID_W = 64
N_MIXERS = 3

CHUNK_A = 128
A_GROUPS = 8
GMLP_PART = 256
GLA_HEADS = 4
GLA_RANK = 16
GLA_TAU = 16.0
GLA_CHUNK = 128
ROPE_BASE = 10000.0
NA_HEADS = 16
NA_KH = 8
NA_KW = 16
NEG_INF = -1e30
NA_QROWS = 4
NA_WROWS = 12
NA_PAIRS_PER_STEP = 4
TOP_K = 4
SWIGLU_LIMIT = 7.0
SWIGLU_ALPHA = 1.702
MOE_BM = 1024
MOE_SUB = 256

TM = 512
GLA_TB = 256
GLA_PROJ_PART = 256
VMEM_LIMIT = 56 * 1024 * 1024


def _cparams(sem):
    return pltpu.CompilerParams(dimension_semantics=sem, vmem_limit_bytes=VMEM_LIMIT)


def _dot(a, b):
    return jnp.dot(a, b, preferred_element_type=F32)


def _dot_nt(a, b):
    return lax.dot_general(a, b, (((1,), (1,)), ((), ())), preferred_element_type=F32)


def _dot_tn(a, b):
    return lax.dot_general(a, b, (((0,), (0,)), ((), ())), preferred_element_type=F32)


def _rms(x, g):
    return x * lax.rsqrt(jnp.mean(x * x, axis=-1, keepdims=True) + NORM_EPS) * g


def _modulate(x, g, shift, scale):
    return _rms(x, g) * (1.0 + scale) + shift


ADA_COL_BLOCKS = 4


def _ada_kernel(st_ref, w_ref, b_ref, o_ref, *, nrows):
    st = st_ref[...]
    st = st * jax.nn.sigmoid(st)
    w = w_ref[0]
    rows = [jnp.sum(w * st[:, r:r + 1], axis=0, keepdims=True) for r in range(nrows)]
    rows += [jnp.zeros_like(rows[0])] * (o_ref.shape[1] - nrows)
    o_ref[0] = jnp.concatenate(rows, axis=0) + b_ref[0]


def _adaln(cond_t, nrows, ada_w, ada_b):
    depth, d, n6 = ada_w.shape
    bn = n6 // ADA_COL_BLOCKS
    return pl.pallas_call(
        functools.partial(_ada_kernel, nrows=nrows),
        grid=(depth, ADA_COL_BLOCKS),
        in_specs=[pl.BlockSpec((d, SUBLANES), lambda i, j: (0, 0)),
                  pl.BlockSpec((1, d, bn), lambda i, j: (i, 0, j)),
                  pl.BlockSpec((1, 1, bn), lambda i, j: (i, 0, j))],
        out_specs=pl.BlockSpec((1, SUBLANES, bn), lambda i, j: (i, 0, j)),
        out_shape=jax.ShapeDtypeStruct((depth, SUBLANES, n6), F32),
        compiler_params=_cparams(("parallel", "parallel")),
        name="adaln",
    )(cond_t, ada_w, ada_b.reshape(depth, 1, n6))


def _gelu(z):
    return 0.5 * z * (1.0 + lax.erf(z * np.float32(np.sqrt(0.5))))


def _gmlp_kernel(*refs, ctx_tiles, fused_combine=False):
    if fused_combine:
        xp_ref, yk_ref, gate_ref, mp_ref, *refs = refs
        *refs, x_ref = refs
        half = yk_ref.shape[2]
        x_ref[:, :half], x_ref[:, half:] = _combined_halves(xp_ref, mp_ref[0], yk_ref, gate_ref)
    elif ctx_tiles:
        xc_ref, xl_ref, *refs = refs
        *refs, x_ref = refs

        @pl.when(pl.program_id(0) < ctx_tiles)
        def _():
            x_ref[...] = xc_ref[...]

        @pl.when(pl.program_id(0) >= ctx_tiles)
        def _():
            x_ref[...] = xl_ref[...]
    else:
        x_ref, *refs = refs
    m_ref, g_ref, win_ref, gv_ref, ws_ref, bs_ref, wout_ref, *route_refs = refs
    route_in, o_ref, route_out = route_refs[:3], route_refs[3], route_refs[4:]
    m = m_ref[0]
    a = gv_ref.shape[1]
    gw = a // A_GROUPS
    npart = x_ref.shape[0] // GMLP_PART
    parts = [slice(p * GMLP_PART, (p + 1) * GMLP_PART) for p in range(npart)]
    z = [_dot(_modulate(x_ref[rs, :], g_ref[...], m[0:1], m[1:2]).astype(BF16), win_ref[...]) for rs in parts]
    for rs, zp in zip(parts, z):
        zp = _gelu(zp)
        u = zp[:, :a]
        v = _rms(zp[:, a:], gv_ref[...]).astype(BF16)
        rows = []
        for c in range(GMLP_PART // CHUNK_A):
            cols = [_dot(ws_ref[g], v[c * CHUNK_A:(c + 1) * CHUNK_A, g * gw:(g + 1) * gw])
                    for g in range(A_GROUPS)]
            rows.append(jnp.concatenate(cols, axis=1) + bs_ref[...])
        s = jnp.concatenate(rows, axis=0)
        y = _dot((u * s).astype(BF16), wout_ref[...])
        o_ref[rs, :] = x_ref[rs, :] + m[2:3] * y
    _route_tail(o_ref[...], m, *route_in, *route_out)


def _gmlp_mixer(xs, mods, g, w_in, g_v, w_s, b_s, w_out, tile0, n_tiles, geo, route_w):
    pending = isinstance(xs, dict)
    split = isinstance(xs, tuple)
    d = xs["x"].shape[1] if pending else xs[0].shape[1] if split else xs.shape[1]
    a = g_v.shape[0]
    gw = a // A_GROUPS
    bias = jnp.repeat(b_s.T, gw, axis=1)
    grp = functools.partial(_tile_group, geo=geo, tile0=tile0)
    const2 = lambda t: (0, 0)
    r_in, r_out, r_shape, r_scratch = _route_specs(n_tiles * TM, d, route_w[1].shape[1])
    if pending:
        assert tile0 == 0
        ctx_tiles = 0
        yk = xs["yk"]
        x_specs = [pl.BlockSpec((TM, d), lambda t: (t, 0)),
                   pl.BlockSpec((TOP_K, TM, yk.shape[2]), lambda t: (0, t, 0)),
                   pl.BlockSpec((TM, SUBLANES), lambda t: (t, 0)),
                   pl.BlockSpec((1, 6, d), lambda t: (grp(t), 0, 0))]
        x_args, x_scratch = [xs["x"], yk, xs["gate"], xs["mods"]], [pltpu.VMEM((TM, d), F32)]
    elif split:
        assert tile0 == 0
        ctx_tiles = xs[0].shape[0] // TM
        x_specs = [pl.BlockSpec((TM, d), lambda t: (jnp.minimum(t, ctx_tiles - 1), 0)),
                   pl.BlockSpec((TM, d), lambda t: (jnp.maximum(t - ctx_tiles, 0), 0))]
        x_args, x_scratch = list(xs), [pltpu.VMEM((TM, d), F32)]
    else:
        ctx_tiles = 0
        x_specs, x_args, x_scratch = [pl.BlockSpec((TM, d), lambda t: (t + tile0, 0))], [xs], []
    out = pl.pallas_call(
        functools.partial(_gmlp_kernel, ctx_tiles=ctx_tiles, fused_combine=pending),
        grid=(n_tiles,),
        in_specs=x_specs + [
                  pl.BlockSpec((1, 6, d), lambda t: (grp(t), 0, 0)),
                  pl.BlockSpec((1, d), const2),
                  pl.BlockSpec((d, 2 * a), const2),
                  pl.BlockSpec((1, a), const2),
                  pl.BlockSpec((A_GROUPS, CHUNK_A, CHUNK_A), lambda t: (0, 0, 0)),
                  pl.BlockSpec((CHUNK_A, a), const2),
                  pl.BlockSpec((a, d), const2)] + r_in,
        out_specs=[pl.BlockSpec((TM, d), lambda t: (t, 0))] + r_out,
        out_shape=[jax.ShapeDtypeStruct((n_tiles * TM, d), F32)] + r_shape,
        scratch_shapes=r_scratch + x_scratch,
        compiler_params=_cparams(("arbitrary",)),
        name="gmlp_mixer",
    )(*x_args, mods, g.reshape(1, d), w_in.astype(BF16), g_v.reshape(1, a), w_s.astype(BF16),
      bias, w_out.astype(BF16), *_route_args(route_w, d))
    return out[0], out[1:]


def _tile_group(t, geo, tile0=0):
    n_ctx_tiles, tiles_per_batch, nb = geo
    tt = t + tile0
    return jnp.where(tt < n_ctx_tiles, nb, (tt - n_ctx_tiles) // tiles_per_batch)


def _dot_f32(tri_bf16, x):
    hi = x.astype(BF16)
    r1 = x - hi.astype(F32)
    mid = r1.astype(BF16)
    lo = (r1 - mid.astype(F32)).astype(BF16)
    return _dot(tri_bf16, hi) + _dot(tri_bf16, mid) + _dot(tri_bf16, lo)


def _gla_proj_kernel(x_ref, m_ref, g_ref, wq_ref, wk_ref, wv_ref, wg_ref, wr_ref, wa_ref, ba_ref,
                     cos_ref, sin_ref, q_ref, k_ref, v_ref, go_ref, la_ref):
    m = m_ref[0]
    kdim = wq_ref.shape[1]
    hk = kdim // GLA_HEADS
    nf = hk // 4
    npart = x_ref.shape[0] // GLA_PROJ_PART
    parts = [slice(p * GLA_PROJ_PART, (p + 1) * GLA_PROJ_PART) for p in range(npart)]
    zs, qs, ks = [], [], []
    for rs in parts:
        h = _modulate(x_ref[rs, :], g_ref[...], m[0:1], m[1:2]).astype(BF16)
        zs.append(_dot(_dot(h, wr_ref[...]).astype(BF16), wa_ref[...]) + ba_ref[...])
        qs.append(_dot(h, wq_ref[...]) * np.float32(hk ** -0.5))
        ks.append(_dot(h, wk_ref[...]))
        v_ref[rs, :] = _dot(h, wv_ref[...]).astype(BF16)
        go_ref[rs, :] = _dot(h, wg_ref[...])
    for rs, z in zip(parts, zs):
        la_ref[rs, :] = jax.nn.log_sigmoid(z) * np.float32(1.0 / GLA_TAU)

    lane = lax.broadcasted_iota(I32, (GLA_PROJ_PART, kdim), 1)
    first = (lane % (2 * nf)) < nf
    for rs, q, k in zip(parts, qs, ks):
        cos = jnp.concatenate([cos_ref[rs, :]] * GLA_HEADS, axis=1)
        sin = jnp.concatenate([sin_ref[rs, :]] * GLA_HEADS, axis=1)

        def rope(t):
            up = pltpu.roll(t, kdim - nf, 1)
            dn = pltpu.roll(t, nf, 1)
            return t * cos + jnp.where(first, up, dn) * sin

        q_ref[rs, :] = rope(q)
        k_ref[rs, :] = rope(k)


def _gla_decays(q_ref, k_ref, la_ref, n, rev):
    c = GLA_CHUNK
    ri = lax.broadcasted_iota(I32, (c, c), 0)
    ci = lax.broadcasted_iota(I32, (c, c), 1)
    keep = (ci >= ri) if rev else (ci <= ri)
    rows = slice(n * c, (n + 1) * c)
    cum = _dot_f32(keep.astype(BF16), la_ref[rows, :])
    last = cum[0:1] if rev else cum[c - 1:c]
    q = q_ref[rows, :]
    k = k_ref[rows, :]
    mid = cum[c // 2:c // 2 + 1]
    return dict(rows=rows, keep=keep, dec=jnp.exp(last),
                q_dec=(q * jnp.exp(cum)).astype(BF16),
                q_att=(q * jnp.exp(cum - mid)).astype(BF16),
                k_att=(k * jnp.exp(mid - cum)).astype(BF16),
                k_end=(k * jnp.exp(last - cum)).astype(BF16))


def _gla_scan_kernel(qf_ref, kf_ref, vf_ref, laf_ref, qb_ref, kb_ref, vb_ref, lab_ref,
                     of_ref, ob_ref, stf_ref, stb_ref):
    s = pl.program_id(1)

    @pl.when(s == 0)
    def _():
        stf_ref[...] = jnp.zeros_like(stf_ref)
        stb_ref[...] = jnp.zeros_like(stb_ref)

    nch = qf_ref.shape[0] // GLA_CHUNK
    hk = qf_ref.shape[1] // GLA_HEADS
    hv = vf_ref.shape[1] // GLA_HEADS
    dirs = ((qf_ref, kf_ref, vf_ref, laf_ref, of_ref, stf_ref, False),
            (qb_ref, kb_ref, vb_ref, lab_ref, ob_ref, stb_ref, True))
    heads = [(slice(h * hk, (h + 1) * hk), slice(h * hv, (h + 1) * hv)) for h in range(GLA_HEADS)]
    for n in range(nch):
        w = [_gla_decays(q_ref, k_ref, la_ref, nch - 1 - n if rev else n, rev)
             for q_ref, k_ref, _, la_ref, _, _, rev in dirs]
        att = [[jnp.where(w[d]["keep"], _dot_nt(w[d]["q_att"][:, ks], w[d]["k_att"][:, ks]), 0.0).astype(BF16)
                for ks, _ in heads] for d in range(2)]
        for d, (_, _, v_ref, _, o_ref, st_ref, _) in enumerate(dirs):
            for h, (ks, vs) in enumerate(heads):
                o_ref[w[d]["rows"], vs] = (_dot(att[d][h], v_ref[w[d]["rows"], vs])
                                           + _dot_nt(w[d]["q_dec"][:, ks], st_ref[h].astype(BF16)))
        for d, (_, _, v_ref, _, _, st_ref, _) in enumerate(dirs):
            for h, (ks, vs) in enumerate(heads):
                st_ref[h] = (st_ref[h] * w[d]["dec"][:, ks]
                             + _dot_tn(v_ref[w[d]["rows"], vs], w[d]["k_end"][:, ks]))


def _gla_out_kernel(x_ref, m_ref, of_ref, ob_ref, go_ref, gn_ref, wout_ref, *route_refs):
    route_in, o_ref, route_out = route_refs[:3], route_refs[3], route_refs[4:]
    x = x_ref[...]
    m = m_ref[0]
    o = of_ref[...] + ob_ref[...]
    hv = gn_ref.shape[1]
    parts = [_rms(o[:, h * hv:(h + 1) * hv], gn_ref[...]) for h in range(GLA_HEADS)]
    o = jnp.concatenate(parts, axis=1)
    gate = go_ref[...]
    y = _dot((o * (gate * jax.nn.sigmoid(gate))).astype(BF16), wout_ref[...])
    x_new = x + m[2:3] * y
    o_ref[...] = x_new
    _route_tail(x_new, m, *route_in, *route_out)


def _gla_mixer(xs, mods, g, w_in, w_a2, b_a, g_o, w_out, geo, dims, route_w):
    b, seq, lctx = dims
    nt, d = xs.shape
    n_tiles = nt // TM
    kdim = w_a2.shape[2]
    vdim = g_o.shape[0] * GLA_HEADS
    hk = kdim // GLA_HEADS
    nf = hk // 4
    wq = w_in[:, :kdim].astype(BF16)
    wk = w_in[:, kdim:2 * kdim].astype(BF16)
    wv = w_in[:, 2 * kdim:2 * kdim + vdim].astype(BF16)
    wg = w_in[:, 2 * kdim + vdim:2 * kdim + 2 * vdim].astype(BF16)
    wr = jnp.pad(w_in[:, 2 * kdim + 2 * vdim:], ((0, 0), (0, 128 - 2 * GLA_RANK))).astype(BF16)
    wa = jnp.zeros((128, 2 * kdim), F32)
    wa = wa.at[:GLA_RANK, :kdim].set(w_a2[0]).at[GLA_RANK:2 * GLA_RANK, kdim:].set(w_a2[1]).astype(BF16)
    ba = b_a.reshape(1, 2 * kdim)

    tpos = np.arange(seq)
    freqs = np.power(np.float32(ROPE_BASE), -np.arange(nf, dtype=np.float32) / np.float32(nf)).astype(np.float32)
    ar = (tpos // GRID_W).astype(np.float32)[:, None] * freqs
    ac = (tpos % GRID_W).astype(np.float32)[:, None] * freqs
    cos = np.concatenate([np.cos(ar), np.cos(ar), np.cos(ac), np.cos(ac)], axis=1)
    sin = np.concatenate([-np.sin(ar), np.sin(ar), -np.sin(ac), np.sin(ac)], axis=1)
    cos = jnp.asarray(np.concatenate([np.ones((TM, hk), np.float32), cos], axis=0).astype(np.float32))
    sin = jnp.asarray(np.concatenate([np.zeros((TM, hk), np.float32), sin], axis=0).astype(np.float32))

    n_ctx_tiles, tiles_per_batch, _ = geo
    grp = functools.partial(_tile_group, geo=geo)

    def rope_blk(t):
        return (jnp.where(t < n_ctx_tiles, 0, 1 + (t - n_ctx_tiles) % tiles_per_batch), 0)

    const2 = lambda t: (0, 0)
    row = lambda t: (t, 0)
    q, k, v, go, la = pl.pallas_call(
        _gla_proj_kernel,
        grid=(n_tiles,),
        in_specs=[pl.BlockSpec((TM, d), row),
                  pl.BlockSpec((1, 6, d), lambda t: (grp(t), 0, 0)),
                  pl.BlockSpec((1, d), const2),
                  pl.BlockSpec((d, kdim), const2), pl.BlockSpec((d, kdim), const2),
                  pl.BlockSpec((d, vdim), const2), pl.BlockSpec((d, vdim), const2),
                  pl.BlockSpec((d, 128), const2), pl.BlockSpec((128, 2 * kdim), const2),
                  pl.BlockSpec((1, 2 * kdim), const2),
                  pl.BlockSpec((TM, hk), rope_blk), pl.BlockSpec((TM, hk), rope_blk)],
        out_specs=[pl.BlockSpec((TM, kdim), row), pl.BlockSpec((TM, kdim), row),
                   pl.BlockSpec((TM, vdim), row), pl.BlockSpec((TM, vdim), row),
                   pl.BlockSpec((TM, 2 * kdim), row)],
        out_shape=[jax.ShapeDtypeStruct((nt, kdim), F32), jax.ShapeDtypeStruct((nt, kdim), F32),
                   jax.ShapeDtypeStruct((nt, vdim), BF16), jax.ShapeDtypeStruct((nt, vdim), F32),
                   jax.ShapeDtypeStruct((nt, 2 * kdim), F32)],
        compiler_params=_cparams(("parallel",)),
        name="gla_proj",
    )(xs, mods, g.reshape(1, d), wq, wk, wv, wg, wr, wa, ba, cos, sin)

    tb = GLA_TB
    ctx_steps = lctx // tb
    lat_steps = seq // tb
    steps = ctx_steps + lat_steps

    def blk(bi, s, rev):
        if rev:
            cs, ls = ctx_steps - 1 - s, lat_steps - 1 - (s - ctx_steps)
        else:
            cs, ls = s, s - ctx_steps
        return jnp.where(s < ctx_steps, bi * ctx_steps + cs, b * ctx_steps + bi * lat_steps + ls)

    def dir_specs(rev):
        row_blk = lambda bi, s: (blk(bi, s, rev), 0)
        return [pl.BlockSpec((tb, kdim), row_blk), pl.BlockSpec((tb, kdim), row_blk),
                pl.BlockSpec((tb, vdim), row_blk),
                pl.BlockSpec((tb, kdim), lambda bi, s: (blk(bi, s, rev), 1 if rev else 0))]

    state = pltpu.VMEM((GLA_HEADS, vdim // GLA_HEADS, hk), F32)
    o_f, o_b = pl.pallas_call(
        _gla_scan_kernel,
        grid=(b, steps),
        in_specs=dir_specs(False) + dir_specs(True),
        out_specs=[pl.BlockSpec((tb, vdim), lambda bi, s: (blk(bi, s, False), 0)),
                   pl.BlockSpec((tb, vdim), lambda bi, s: (blk(bi, s, True), 0))],
        out_shape=[jax.ShapeDtypeStruct((nt, vdim), F32)] * 2,
        scratch_shapes=[state, state],
        compiler_params=_cparams(("parallel", "arbitrary")),
        name="gla_scan",
    )(q, k, v, la, q, k, v, la)

    r_in, r_out, r_shape, r_scratch = _route_specs(nt, d, route_w[1].shape[1])
    out = pl.pallas_call(
        _gla_out_kernel,
        grid=(n_tiles,),
        in_specs=[pl.BlockSpec((TM, d), row),
                  pl.BlockSpec((1, 6, d), lambda t: (grp(t), 0, 0)),
                  pl.BlockSpec((TM, vdim), row), pl.BlockSpec((TM, vdim), row),
                  pl.BlockSpec((TM, vdim), row),
                  pl.BlockSpec((1, vdim // GLA_HEADS), const2),
                  pl.BlockSpec((vdim, d), const2)] + r_in,
        out_specs=[pl.BlockSpec((TM, d), row)] + r_out,
        out_shape=[jax.ShapeDtypeStruct((nt, d), F32)] + r_shape,
        scratch_shapes=r_scratch,
        compiler_params=_cparams(("arbitrary",)),
        name="gla_out",
    )(xs, mods, o_f, o_b, go, g_o.reshape(1, -1), w_out.astype(BF16), *_route_args(route_w, d))
    return out[0], out[1:]


def _na_proj_kernel(*refs, fused_combine=False):
    if fused_combine:
        xp_ref, yk_ref, gate_ref, mp_ref, *refs = refs
        *refs, x_ref = refs
        half = yk_ref.shape[2]
        x_ref[:, :half], x_ref[:, half:] = _combined_halves(xp_ref, mp_ref[0], yk_ref, gate_ref)
        m_ref, g_ref, w_ref, q_ref, k_ref, v_ref = refs
    else:
        x_ref, m_ref, g_ref, w_ref, q_ref, k_ref, v_ref = refs
    x = x_ref[...]
    m = m_ref[0]
    d = x.shape[1]
    h = _modulate(x, g_ref[...], m[0:1], m[1:2]).astype(BF16)
    qkv = _dot(h, w_ref[...])
    hd = d // NA_HEADS
    for p in range(q_ref.shape[0]):
        cs = slice(p * LANES, (p + 1) * LANES)
        q_ref[p] = (qkv[:, cs] * np.float32(hd ** -0.5)).astype(BF16)
        k_ref[p] = qkv[:, d + p * LANES:d + (p + 1) * LANES].astype(BF16)
        v_ref[p] = qkv[:, 2 * d + p * LANES:2 * d + (p + 1) * LANES].astype(BF16)


def _na_attn_kernel(tbl_ref, q_ref, *refs, n_tiles, nwin):
    k_refs, v_refs = refs[:nwin], refs[nwin:2 * nwin]
    kc_ref, vc_ref, bias_ref, o_ref, s_ref, p_ref = refs[2 * nwin:]
    t = pl.program_id(2)
    typ = jnp.where(t == 0, 0, jnp.where(t == n_tiles - 1, 2, 1))
    npair = NA_WROWS // 2
    nlat = NA_WROWS * GRID_W
    hd = LANES // 2
    lane = lax.broadcasted_iota(I32, q_ref.shape[1:], 1)
    npp, tq = q_ref.shape[0], q_ref.shape[1]
    for pp in range(npp):
        q = q_ref[pp]
        k_all = jnp.concatenate([r[pp] for r in k_refs] + [kc_ref[pp]], axis=0)
        zero = jnp.zeros_like(q)
        q2 = jnp.concatenate([jnp.where(lane < hd, q, zero), jnp.where(lane >= hd, q, zero)], axis=0)
        s_ref[pp] = _dot_nt(q2, k_all)
    for pp in range(npp):
        for hh in range(2):
            for dr in range(NA_QROWS):
                rs = slice(hh * tq + dr * GRID_W, hh * tq + (dr + 1) * GRID_W)
                lat = [s_ref[pp, rs, j * LANES:(j + 1) * LANES]
                       + bias_ref[2 * pp + hh, tbl_ref[typ * (NA_QROWS * npair) + dr * npair + j]]
                       for j in range(npair)]
                sb = jnp.concatenate(lat + [s_ref[pp, rs, nlat:]], axis=1)
                e = jnp.exp(sb - jnp.max(sb, axis=1, keepdims=True))
                p_ref[pp, rs, :] = (e * (1.0 / jnp.sum(e, axis=1, keepdims=True))).astype(BF16)
    outs = []
    for pp in range(npp):
        v_all = jnp.concatenate([r[pp] for r in v_refs] + [vc_ref[pp]], axis=0)
        o2 = _dot(p_ref[pp], v_all)
        outs.append(jnp.where(lane < hd, o2[:tq], o2[tq:]))
    o_ref[...] = jnp.concatenate(outs, axis=1).astype(BF16)


def _na_out_kernel(x_ref, m_ref, a_ref, w_ref, *route_refs):
    route_in, o_ref, route_out = route_refs[:3], route_refs[3], route_refs[4:]
    m = m_ref[0]
    x_new = x_ref[...] + m[2:3] * _dot(a_ref[...], w_ref[...])
    o_ref[...] = x_new
    _route_tail(x_new, m, *route_in, *route_out)


def _na_tables(rows):
    npair = NA_WROWS // 2
    tbl = np.zeros((3, NA_QROWS, npair), np.int32)
    for typ, r_base in enumerate((0, NA_QROWS, rows - NA_QROWS)):
        w0 = int(np.clip(r_base - NA_KH // 2, 0, rows - NA_WROWS))
        for dr in range(NA_QROWS):
            r = r_base + dr
            r0 = int(np.clip(r - NA_KH // 2, 0, rows - NA_KH))
            for j in range(npair):
                kr = (w0 + 2 * j, w0 + 2 * j + 1)
                ok = [r0 <= x < r0 + NA_KH for x in kr]
                ri = [x - r + NA_KH - 1 for x in kr]
                if ok[0] and ok[1]:
                    e = ri[0]
                elif ok[0]:
                    e = 16 + ri[0]
                elif ok[1]:
                    e = 32 + ri[1]
                else:
                    e = 63
                tbl[typ, dr, j] = e
    return tbl.reshape(-1)


def _na_bias_kernel(rpb_ref, o_ref, *, nr, nw):
    h = pl.program_id(0)
    c = lax.broadcasted_iota(I32, (GRID_W, GRID_W), 0)
    k = lax.broadcasted_iota(I32, (GRID_W, GRID_W), 1)
    cstart = jnp.clip(c - NA_KW // 2, 0, GRID_W - NA_KW)
    ok = jnp.logical_and(k >= cstart, k < cstart + NA_KW)
    rel = k - c + NA_KW - 1
    neg = jnp.full((GRID_W, GRID_W), NEG_INF, F32)
    cbs = []
    for r in range(nr):
        acc = neg
        for w in range(nw):
            acc = jnp.where(rel == w, rpb_ref[(h * nr + r) * nw + w], acc)
        cbs.append(jnp.where(ok, acc, neg))
    cbs += [neg] * (17 - nr)
    for r in range(16):
        o_ref[0, r] = jnp.concatenate([cbs[r], cbs[r + 1]], axis=1)
        o_ref[0, 16 + r] = jnp.concatenate([cbs[r], neg], axis=1)
        o_ref[0, 32 + r] = jnp.concatenate([neg, cbs[r]], axis=1)
        o_ref[0, 48 + r] = jnp.concatenate([neg, neg], axis=1)


def _na_bias_table(rpb):
    nh, nr, nw = rpb.shape
    return pl.pallas_call(
        functools.partial(_na_bias_kernel, nr=nr, nw=nw),
        grid=(nh,),
        in_specs=[pl.BlockSpec(memory_space=pltpu.SMEM)],
        out_specs=pl.BlockSpec((1, 64, GRID_W, 2 * GRID_W), lambda h: (h, 0, 0, 0)),
        out_shape=jax.ShapeDtypeStruct((nh, 64, GRID_W, 2 * GRID_W), F32),
        compiler_params=_cparams(("parallel",)),
        name="na_bias_table",
    )(rpb.reshape(-1).astype(F32))


def _na_mixer(xs, mods, g, w_qkv, rpb, w_out, geo, dims, route_w):
    b, seq, lctx = dims
    pending = isinstance(xs, dict)
    nt, d = xs["x"].shape if pending else xs.shape
    n_tiles_tok = nt // TM
    grp = functools.partial(_tile_group, geo=geo)
    npairs = d // 128
    const2 = lambda t: (0, 0)
    mod_spec = pl.BlockSpec((1, 6, d), lambda t: (grp(t), 0, 0))
    x_specs, x_args, x_out, x_shape = [pl.BlockSpec((TM, d), lambda t: (t, 0))], [xs], [], []
    if pending:
        yk = xs["yk"]
        x_specs += [pl.BlockSpec((TOP_K, TM, yk.shape[2]), lambda t: (0, t, 0)),
                    pl.BlockSpec((TM, SUBLANES), lambda t: (t, 0)), mod_spec]
        x_args = [xs["x"], yk, xs["gate"], xs["mods"]]
        x_out, x_shape = [pl.BlockSpec((TM, d), lambda t: (t, 0))], [jax.ShapeDtypeStruct((nt, d), F32)]
    q, k, v, *x_new = pl.pallas_call(
        functools.partial(_na_proj_kernel, fused_combine=pending),
        grid=(n_tiles_tok,),
        in_specs=x_specs + [mod_spec,
                            pl.BlockSpec((1, d), const2),
                            pl.BlockSpec((d, 3 * d), const2)],
        out_specs=[pl.BlockSpec((npairs, TM, 128), lambda t: (0, t, 0))] * 3 + x_out,
        out_shape=[jax.ShapeDtypeStruct((npairs, nt, 128), BF16)] * 3 + x_shape,
        compiler_params=_cparams(("parallel",)),
        name="na_proj",
    )(*x_args, mods, g.reshape(1, d), w_qkv.astype(BF16))
    if pending:
        xs = x_new[0]

    rows = seq // GRID_W
    tq = NA_QROWS * GRID_W
    n_tiles = rows // NA_QROWS
    wb = 256
    nwin = NA_WROWS * GRID_W // wb
    lat0 = b * lctx
    tbl = jnp.asarray(_na_tables(rows))
    bias = _na_bias_table(rpb)

    pp = NA_PAIRS_PER_STEP
    assert npairs % pp == 0 and (NA_QROWS * GRID_W) % wb == 0 and (NA_KH // 2 * GRID_W) % wb == 0

    def win(i):
        def f(p, bi, t, tbl_ref):
            w = jnp.clip(t * (NA_QROWS * GRID_W // wb) - NA_KH // 2 * GRID_W // wb, 0, seq // wb - nwin)
            return (p, (lat0 + bi * seq) // wb + w + i, 0)
        return f

    kv_specs = [pl.BlockSpec((pp, wb, 128), win(i)) for i in range(nwin)]
    attn = pl.pallas_call(
        functools.partial(_na_attn_kernel, n_tiles=n_tiles, nwin=nwin),
        grid_spec=pltpu.PrefetchScalarGridSpec(
            num_scalar_prefetch=1,
            grid=(npairs // pp, b, n_tiles),
            in_specs=[pl.BlockSpec((pp, tq, 128), lambda p, bi, t, tr: (p, (lat0 + bi * seq) // tq + t, 0))]
                     + kv_specs + kv_specs
                     + [pl.BlockSpec((pp, lctx, 128), lambda p, bi, t, tr: (p, bi, 0)),
                        pl.BlockSpec((pp, lctx, 128), lambda p, bi, t, tr: (p, bi, 0)),
                        pl.BlockSpec((2 * pp, 64, GRID_W, 2 * GRID_W), lambda p, bi, t, tr: (p, 0, 0, 0),
                                     pipeline_mode=pl.Buffered(1))],
            out_specs=pl.BlockSpec((tq, 128 * pp), lambda p, bi, t, tr: (bi * n_tiles + t, p)),
            scratch_shapes=[pltpu.VMEM((pp, 2 * tq, NA_WROWS * GRID_W + lctx), F32),
                            pltpu.VMEM((pp, 2 * tq, NA_WROWS * GRID_W + lctx), BF16)]),
        out_shape=jax.ShapeDtypeStruct((b * seq, d), BF16),
        compiler_params=_cparams(("parallel", "parallel", "arbitrary")),
        name="na_attn",
    )(tbl, q, *([k] * nwin), *([v] * nwin), k, v, bias)

    n_lat_tiles = b * seq // TM
    tile0 = lat0 // TM
    grp_l = functools.partial(_tile_group, geo=geo, tile0=tile0)
    r_in, r_out, r_shape, r_scratch = _route_specs(b * seq, d, route_w[1].shape[1])
    out = pl.pallas_call(
        _na_out_kernel,
        grid=(n_lat_tiles,),
        in_specs=[pl.BlockSpec((TM, d), lambda t: (t + tile0, 0)),
                  pl.BlockSpec((1, 6, d), lambda t: (grp_l(t), 0, 0)),
                  pl.BlockSpec((TM, d), lambda t: (t, 0)),
                  pl.BlockSpec((d, d), const2)] + r_in,
        out_specs=[pl.BlockSpec((TM, d), lambda t: (t, 0))] + r_out,
        out_shape=[jax.ShapeDtypeStruct((b * seq, d), F32)] + r_shape,
        scratch_shapes=r_scratch,
        compiler_params=_cparams(("arbitrary",)),
        name="na_out",
    )(xs, mods, attn, w_out.astype(BF16), *_route_args(route_w, d))
    return out[0], out[1:]


SC_CORES = 2
SC_SUBCORES = 16
SC_WORKERS = SC_CORES * SC_SUBCORES
SC_MAX_CHUNK = 96
SLOT_T_MAX = 2048


def _sc_mesh():
    return plsc.VectorSubcoreMesh(core_axis_name="c", subcore_axis_name="s")


def _sc_chunk(per_worker, max_chunk):
    return max(c for c in range(8, max_chunk + 1, 8) if per_worker % (2 * c) == 0)


def _sc_gather_rows(table, idx):
    dd = table.shape[1]
    bsz = idx.shape[0]
    per_w = bsz // SC_WORKERS
    assert per_w * SC_WORKERS == bsz
    chunk = _sc_chunk(per_w, SC_MAX_CHUNK)
    n_chunks = per_w // chunk

    def body(table_hbm, idx_hbm, out_hbm, idx_v, rows0, rows1, g0, g1, w0, w1):
        wid = lax.axis_index("s") * SC_CORES + lax.axis_index("c")
        pltpu.sync_copy(idx_hbm.at[wid], idx_v)
        base = wid * per_w

        def out_rows(j):
            return out_hbm.at[pl.ds(pl.multiple_of(base + j * chunk, 8), chunk)]

        def step(i, carry):
            j0, j1 = 2 * i, 2 * i + 1
            ga = pltpu.async_copy(table_hbm.at[idx_v.at[j0]], rows0, g0)
            gb = pltpu.async_copy(table_hbm.at[idx_v.at[j1]], rows1, g1)
            ga.wait()
            wa = pltpu.async_copy(rows0, out_rows(j0), w0)
            gb.wait()
            wb = pltpu.async_copy(rows1, out_rows(j1), w1)
            wa.wait()
            wb.wait()
            return carry

        lax.fori_loop(0, n_chunks // 2, step, 0)

    return pl.kernel(
        body, out_type=jax.ShapeDtypeStruct((bsz, dd), table.dtype), mesh=_sc_mesh(),
        scratch_types=[pltpu.VMEM((n_chunks, chunk), I32),
                       pltpu.VMEM((chunk, dd), table.dtype), pltpu.VMEM((chunk, dd), table.dtype),
                       pltpu.SemaphoreType.DMA, pltpu.SemaphoreType.DMA,
                       pltpu.SemaphoreType.DMA, pltpu.SemaphoreType.DMA],
        name="sc_gather_rows",
    )(table, idx.reshape(SC_WORKERS, n_chunks, chunk))


def _sc_scatter_rows(rows, dest, n_out):
    n, dd = rows.shape
    kk = dest.shape[0]
    per_w = n // SC_WORKERS
    assert per_w * SC_WORKERS == n
    chunk = _sc_chunk(per_w, SC_MAX_CHUNK)
    n_chunks = per_w // chunk
    dest_w = dest.reshape(kk, SC_WORKERS, n_chunks, chunk).transpose(1, 2, 0, 3)
    dest_w = dest_w.reshape(SC_WORKERS, n_chunks * kk, chunk)

    def body(rows_hbm, dest_hbm, out_hbm, idx_v, rows0, rows1, r0, r1, s0, s1):
        wid = lax.axis_index("s") * SC_CORES + lax.axis_index("c")
        pltpu.sync_copy(dest_hbm.at[wid], idx_v)
        base = wid * per_w

        def in_rows(j):
            return rows_hbm.at[pl.ds(pl.multiple_of(base + j * chunk, 8), chunk)]

        def step(i, carry):
            j0, j1 = 2 * i, 2 * i + 1
            ra = pltpu.async_copy(in_rows(j0), rows0, r0)
            rb = pltpu.async_copy(in_rows(j1), rows1, r1)
            ra.wait()
            sa = [pltpu.async_copy(rows0, out_hbm.at[idx_v.at[j0 * kk + k]], s0) for k in range(kk)]
            rb.wait()
            sb = [pltpu.async_copy(rows1, out_hbm.at[idx_v.at[j1 * kk + k]], s1) for k in range(kk)]
            for cp in sa + sb:
                cp.wait()
            return carry

        lax.fori_loop(0, n_chunks // 2, step, 0)

    return pl.kernel(
        body, out_type=jax.ShapeDtypeStruct((n_out, dd), rows.dtype), mesh=_sc_mesh(),
        scratch_types=[pltpu.VMEM((n_chunks * kk, chunk), I32),
                       pltpu.VMEM((chunk, dd), rows.dtype), pltpu.VMEM((chunk, dd), rows.dtype),
                       pltpu.SemaphoreType.DMA, pltpu.SemaphoreType.DMA,
                       pltpu.SemaphoreType.DMA, pltpu.SemaphoreType.DMA],
        name="sc_scatter_rows",
    )(rows, dest_w)


def _prefix_sum_rows(col):
    nr = col.shape[0]
    acc = jnp.broadcast_to(col, (nr, LANES))
    row = lax.broadcasted_iota(I32, (nr, LANES), 0)
    s = 1
    while s < nr:
        acc = acc + jnp.where(row >= s, pltpu.roll(acc, s, 0), 0.0)
        s *= 2
    return acc[:, 0:1]


def _slot_kernel(idx_ref, rank_ref, cnt_ref, dest_ref, btab_ref, *, bm):
    cnt = cnt_ref[...]
    ne = cnt.shape[0]
    padded = jnp.floor((cnt + (bm - 1.0)) * (1.0 / bm)) * bm
    pad_end = _prefix_sum_rows(padded)
    pad_start = pad_end - padded
    idx = idx_ref[...]
    e_iota = lax.broadcasted_iota(I32, (ne, idx.shape[1]), 0)
    starts = [jnp.sum(jnp.where(e_iota == idx[k:k + 1], pad_start, 0.0), axis=0, keepdims=True)
              for k in range(idx.shape[0])]
    dest_ref[...] = jnp.concatenate(starts, axis=0).astype(I32) + rank_ref[...]

    nbp = btab_ref.shape[1]
    bstart = lax.broadcasted_iota(I32, (ne, nbp), 1).astype(F32) * bm
    be = jnp.minimum(jnp.sum((pad_end <= bstart).astype(F32), axis=0, keepdims=True), ne - 1.0)
    mine = lax.broadcasted_iota(I32, (ne, nbp), 0).astype(F32) == be
    pick = lambda col: jnp.sum(jnp.where(mine, col, 0.0), axis=0, keepdims=True)
    live = jnp.clip(pick(pad_start + cnt) - bstart[0:1], 0.0, bm)
    r = lax.broadcasted_iota(I32, (ne, ne), 0)
    c = lax.broadcasted_iota(I32, (ne, ne), 1)
    ends_on_lanes = _dot_f32(jnp.ones((ne, ne), BF16), jnp.where(r == c, pad_end, 0.0))
    nxt_e = jnp.sum((ends_on_lanes <= pad_end).astype(F32), axis=1, keepdims=True)
    nxt_e = jnp.where(nxt_e < ne, nxt_e, -1.0)
    ordinal = _prefix_sum_rows((cnt > 0).astype(F32)) - 1.0
    slot_e = ordinal - 2.0 * jnp.floor(ordinal * 0.5)
    rowi = lax.broadcasted_iota(I32, btab_ref.shape, 0)
    tab = jnp.where(rowi == 0, be, jnp.where(rowi == 1, live, jnp.where(rowi == 2, pick(nxt_e), pick(slot_e))))
    btab_ref[...] = tab.astype(I32)


def _route_specs(n, d, ne):
    const2 = lambda t: (0, 0)
    in_specs = [pl.BlockSpec((1, d), const2), pl.BlockSpec((ne, d), const2), pl.BlockSpec((ne, 1), const2)]
    out_specs = [pl.BlockSpec((TM, d // 2), lambda t: (t, 0)),
                 pl.BlockSpec((TOP_K, TM), lambda t: (0, t)),
                 pl.BlockSpec((TM, SUBLANES), lambda t: (t, 0)),
                 pl.BlockSpec((TOP_K, TM), lambda t: (0, t)),
                 pl.BlockSpec((ne, 1), const2)]
    out_shape = [jax.ShapeDtypeStruct((n, d // 2), jnp.uint32),
                 jax.ShapeDtypeStruct((TOP_K, n), I32),
                 jax.ShapeDtypeStruct((n, SUBLANES), F32),
                 jax.ShapeDtypeStruct((TOP_K, n), I32),
                 jax.ShapeDtypeStruct((ne, 1), F32)]
    return in_specs, out_specs, out_shape, [pltpu.VMEM((ne, 1), F32)]


def _route_args(route_w, d):
    g_ffn, w_router, b_router = route_w
    ne = w_router.shape[1]
    return g_ffn.reshape(1, d), w_router.T.astype(BF16), b_router.reshape(ne, 1)


def _route_tail(x, m, g_ref, wr_ref, br_ref, h_ref, idx_ref, gate_ref, rank_ref, cnt_ref, run_ref):
    t = pl.program_id(0)

    @pl.when(t == 0)
    def _():
        run_ref[...] = jnp.zeros_like(run_ref)

    h = _modulate(x, g_ref[...], m[3:4], m[4:5]).astype(BF16)
    h_ref[...] = _pack_bf16_pairs(h)
    logits = _dot_nt(wr_ref[...], h) + br_ref[...]
    ne, tm = logits.shape
    e_iota = lax.broadcasted_iota(I32, (ne, tm), 0)
    vals, idxs = [], []
    l = logits
    for _ in range(TOP_K):
        mk = jnp.max(l, axis=0, keepdims=True)
        ik = jnp.min(jnp.where(l == mk, e_iota, ne), axis=0, keepdims=True)
        vals.append(mk)
        idxs.append(ik)
        l = jnp.where(e_iota == ik, -jnp.inf, l)
    top_val = jnp.concatenate(vals, axis=0)
    ex = jnp.exp(top_val - vals[0])
    gates = ex / jnp.sum(ex, axis=0, keepdims=True)
    pad = jnp.zeros((gate_ref.shape[1] - TOP_K, tm), F32)
    gate_ref[...] = jnp.concatenate([gates, pad], axis=0).T
    idx_ref[...] = jnp.concatenate(idxs, axis=0)

    hits = [e_iota == ik for ik in idxs]
    cnt = hits[0].astype(F32)
    for hk in hits[1:]:
        cnt = cnt + hk.astype(F32)
    si = lax.broadcasted_iota(I32, (tm, tm), 0)
    ti = lax.broadcasted_iota(I32, (tm, tm), 1)
    before = (si < ti).astype(BF16)
    total = _dot(cnt.astype(BF16), before) + run_ref[...]
    ranks = [jnp.sum(jnp.where(hk, total, 0.0), axis=0, keepdims=True) for hk in hits]
    rank_ref[...] = jnp.concatenate(ranks, axis=0).astype(I32)
    run_ref[...] = run_ref[...] + jnp.sum(cnt, axis=1, keepdims=True)
    cnt_ref[...] = run_ref[...]


def _pack_bf16_pairs(v):
    bits = lax.bitcast_convert_type(v.astype(BF16).astype(F32), jnp.uint32)
    half = bits.shape[1] // 2
    return (bits[:, half:] & jnp.uint32(0xFFFF0000)) | (bits[:, :half] >> 16)


def _unpack_bf16_pairs(w):
    return (lax.bitcast_convert_type(w << 16, F32),
            lax.bitcast_convert_type(w & jnp.uint32(0xFFFF0000), F32))


def _expert_kernel(tab_ref, x_ref, wgu_hbm, bgu_ref, wd_hbm, bd_ref, o_ref,
                   wgu_in, wd_in, wgu_bf, wd_bf, act_ref, sem_gu, sem_d, *, layer):
    blk = pl.program_id(0)
    ch = wgu_bf.shape[2]
    f = act_ref.shape[1]
    nc = f // ch
    expert, live, nxt, wslot = (tab_ref[i, blk] for i in range(4))
    used = live > 0
    fresh = jnp.logical_or(blk == 0, expert != tab_ref[0, jnp.maximum(blk - 1, 0)])

    def fetch(e, slot):
        return (pltpu.make_async_copy(wgu_hbm.at[layer, e], wgu_in.at[slot], sem_gu.at[slot]),
                pltpu.make_async_copy(wd_hbm.at[layer, e], wd_in.at[slot], sem_d.at[slot]))

    @pl.when(jnp.logical_and(used, fresh))
    def _():
        slot = wslot
        mine = fetch(expert, slot)

        @pl.when(blk == 0)
        def _():
            for cp in mine:
                cp.start()

        @pl.when(nxt >= 0)
        def _():
            for cp in fetch(nxt, 1 - slot):
                cp.start()

        for cp in mine:
            cp.wait()

    bm = x_ref.shape[0]
    fresh_full = jnp.logical_and(fresh, live > bm - MOE_SUB)

    def cast_gu(j):
        wgu_bf[j] = wgu_in[wslot, :, j * ch:(j + 1) * ch].astype(BF16)

    def cast_d(j):
        wd_bf[j] = wd_in[wslot, :, j * ch:(j + 1) * ch].astype(BF16)

    @pl.when(jnp.logical_and(jnp.logical_and(used, fresh), jnp.logical_not(fresh_full)))
    def _():
        for j in range(wgu_bf.shape[0]):
            cast_gu(j)
        for j in range(wd_bf.shape[0]):
            cast_d(j)

    def run(nr, cast):
        w = x_ref[:nr, :]
        row = lax.broadcasted_iota(I32, w.shape, 0)
        w = jnp.where(row < live, w, jnp.zeros_like(w))
        lo, hi = _unpack_bf16_pairs(w)
        x = jnp.concatenate([lo.astype(BF16), hi.astype(BF16)], axis=1)
        for c in range(nc):
            c0 = slice(c * ch, (c + 1) * ch)
            c1 = slice(f + c * ch, f + (c + 1) * ch)
            if cast:
                cast_gu(c)
                cast_gu(nc + c)
            glu = _dot(x, wgu_bf[c]) + bgu_ref[0, 0, :, c0]
            lin = _dot(x, wgu_bf[nc + c]) + bgu_ref[0, 0, :, c1]
            glu = jnp.minimum(glu, SWIGLU_LIMIT)
            lin = jnp.clip(lin, -SWIGLU_LIMIT, SWIGLU_LIMIT)
            act_ref[:nr, c0] = (glu * jax.nn.sigmoid(SWIGLU_ALPHA * glu) * (lin + 1.0)).astype(BF16)
        a = act_ref[:nr, :]
        ys = []
        for n in range(wd_bf.shape[0]):
            if cast:
                cast_d(n)
            ys.append(_dot(a, wd_bf[n]))
        o_ref[:nr, :] = _pack_bf16_pairs(jnp.concatenate(ys, axis=1) + bd_ref[0, 0])

    for nr in range(MOE_SUB, bm + 1, MOE_SUB):
        @pl.when(jnp.logical_and(jnp.logical_and(live > nr - MOE_SUB, live <= nr), jnp.logical_not(fresh_full)))
        def _(nr=nr):
            run(nr, False)
            if nr < bm:
                o_ref[nr:, :] = jnp.zeros((bm - nr, o_ref.shape[1]), o_ref.dtype)

    @pl.when(fresh_full)
    def _():
        run(bm, True)

    @pl.when(jnp.logical_not(used))
    def _():
        o_ref[...] = jnp.zeros_like(o_ref)


def _expert_call(tab, buf, w_gu, b_gu, w_down, b_down, layer, bm):
    n_slots, dw = buf.shape
    _, ne, d, f2 = w_gu.shape
    f = f2 // 2
    n_blocks = n_slots // bm
    lw = lambda i, tb: (layer, tb[0, i], 0, 0)
    row = lambda i, tb: (i, 0)
    cw = MXU_COLS
    return pl.pallas_call(
        functools.partial(_expert_kernel, layer=layer),
        grid_spec=pltpu.PrefetchScalarGridSpec(
            num_scalar_prefetch=1,
            grid=(n_blocks,),
            in_specs=[pl.BlockSpec((bm, dw), row),
                      pl.BlockSpec(memory_space=pl.ANY),
                      pl.BlockSpec((1, 1, 1, f2), lw),
                      pl.BlockSpec(memory_space=pl.ANY),
                      pl.BlockSpec((1, 1, 1, d), lw)],
            out_specs=pl.BlockSpec((bm, dw), row),
            scratch_shapes=[pltpu.VMEM((2, d, f2), F32), pltpu.VMEM((2, f, d), F32),
                            pltpu.VMEM((f2 // cw, d, cw), BF16), pltpu.VMEM((d // cw, f, cw), BF16),
                            pltpu.VMEM((bm, f), BF16),
                            pltpu.SemaphoreType.DMA((2,)), pltpu.SemaphoreType.DMA((2,))]),
        out_shape=jax.ShapeDtypeStruct((n_slots, dw), jnp.uint32),
        compiler_params=_cparams(("arbitrary",)),
        name="moe_experts",
    )(tab, buf, w_gu, b_gu.reshape(b_gu.shape[0], ne, 1, f2), w_down, b_down.reshape(b_down.shape[0], ne, 1, d))


def _combined_halves(x_ref, m, y_ref, gate_ref):
    gate = gate_ref[...]
    half = y_ref.shape[2]
    f_lo, f_hi = None, None
    for k in range(TOP_K):
        lo, hi = _unpack_bf16_pairs(y_ref[k])
        gk = gate[:, k:k + 1]
        f_lo = gk * lo if f_lo is None else f_lo + gk * lo
        f_hi = gk * hi if f_hi is None else f_hi + gk * hi
    return x_ref[:, :half] + m[5:6, :half] * f_lo, x_ref[:, half:] + m[5:6, half:] * f_hi


def _combine_kernel(x_ref, m_ref, y_ref, gate_ref, gout_ref, o_ref, *, final):
    half = y_ref.shape[2]
    x_lo, x_hi = _combined_halves(x_ref, m_ref[0], y_ref, gate_ref)
    if final:
        ms = (jnp.sum(x_lo * x_lo, axis=-1, keepdims=True) + jnp.sum(x_hi * x_hi, axis=-1, keepdims=True))
        r = lax.rsqrt(ms * (0.5 / half) + NORM_EPS)
        x_lo = x_lo * r * gout_ref[:, :half]
        x_hi = x_hi * r * gout_ref[:, half:]
    o_ref[:, :half] = x_lo
    o_ref[:, half:] = x_hi


def _moe_layer(xs, routed, mods, layer, w_gu, b_gu, w_down, b_down, geo, g_out, final, defer=False):
    n, d = xs.shape
    h, idx, gate, rank, cnt = routed
    assert h.shape[0] == n
    ne = cnt.shape[0]
    n_tiles = n // TM
    grp = functools.partial(_tile_group, geo=geo)
    const2 = lambda t: (0, 0)

    bm = MOE_BM
    n_rows = n * TOP_K
    n_blocks = -(-n_rows // bm) + ne
    n_slots = n_blocks * bm
    nbp = -(-n_blocks // LANES) * LANES
    slot_t = max(w for w in range(LANES, SLOT_T_MAX + 1, LANES) if n % w == 0)
    dest, btab = pl.pallas_call(
        functools.partial(_slot_kernel, bm=bm),
        grid=(n // slot_t,),
        in_specs=[pl.BlockSpec((TOP_K, slot_t), lambda t: (0, t)),
                  pl.BlockSpec((TOP_K, slot_t), lambda t: (0, t)),
                  pl.BlockSpec((ne, 1), const2)],
        out_specs=[pl.BlockSpec((TOP_K, slot_t), lambda t: (0, t)),
                   pl.BlockSpec((8, nbp), const2)],
        out_shape=[jax.ShapeDtypeStruct((TOP_K, n), I32),
                   jax.ShapeDtypeStruct((8, nbp), I32)],
        compiler_params=_cparams(("arbitrary",)),
        name="moe_slots",
    )(idx, rank, cnt)

    buf = _sc_scatter_rows(h, dest, n_slots)
    y = _expert_call(btab, buf, w_gu, b_gu, w_down, b_down, layer, bm)
    dy = d // 2
    yk = _sc_gather_rows(y, dest.reshape(-1)).reshape(TOP_K, n, dy)
    if defer:
        return dict(x=xs, yk=yk, gate=gate, mods=mods)
    return pl.pallas_call(
        functools.partial(_combine_kernel, final=final),
        grid=(n_tiles,),
        in_specs=[pl.BlockSpec((TM, d), lambda t: (t, 0)),
                  pl.BlockSpec((1, 6, d), lambda t: (grp(t), 0, 0)),
                  pl.BlockSpec((TOP_K, TM, dy), lambda t: (0, t, 0)),
                  pl.BlockSpec((TM, SUBLANES), lambda t: (t, 0)),
                  pl.BlockSpec((1, d), const2)],
        out_specs=pl.BlockSpec((TM, d), lambda t: (t, 0)),
        out_shape=jax.ShapeDtypeStruct((n, d), F32),
        compiler_params=_cparams(("parallel",)),
        name="moe_combine",
    )(xs, mods, yk, gate, g_out.reshape(1, d))


def kernel(x, c, ctx, c_ctx, ada_w, ada_b, norm_mix, norm_ffn, norm_out, a_w_in, a_g_v, a_w_s, a_b_s, a_w_out, b_w_in, b_w_a2, b_b_a, b_g_o, b_w_out, c_w_qkv, c_rpb, c_w_out, moe_w_router, moe_b_router, moe_w_gu, moe_b_gu, moe_w_down, moe_b_down):
    b, seq, d = x.shape
    lctx = ctx.shape[1]
    depth = ada_w.shape[0]
    assert (b * lctx) % TM == 0 and seq % TM == 0 and lctx % GLA_TB == 0 and seq % GLA_TB == 0
    assert seq % (NA_QROWS * GRID_W) == 0 and (b * lctx) % (NA_QROWS * GRID_W) == 0 and lctx % 256 == 0
    assert seq // GRID_W >= NA_WROWS + NA_QROWS and b + 1 <= SUBLANES
    geo = (b * lctx // TM, seq // TM, b)
    dims = (b, seq, lctx)

    cond_t = jnp.zeros((d, SUBLANES), F32).at[:, :b].set(c.T).at[:, b].set(c_ctx)
    mods_all = _adaln(cond_t, b + 1, ada_w, ada_b)[:, :b + 1].reshape(depth, b + 1, 6, d)

    pair = (ctx.reshape(b * lctx, d), x.reshape(b * seq, d))
    n_tiles_all = (b * lctx + b * seq) // TM
    ctx_tiles = geo[0]
    has_ctx = True
    for i in range(depth):
        kind, j = i % N_MIXERS, i // N_MIXERS
        ctx_later = any(kk % N_MIXERS != 0 for kk in range(i + 1, depth))
        mods = mods_all[i]
        route_w = (norm_ffn[i], moe_w_router[i], moe_b_router[i])
        if i == 0 and not (kind == 0 and ctx_later):
            xs = jnp.concatenate(pair, axis=0)
        if kind == 0:
            keep_ctx = has_ctx and ctx_later
            skip = 0 if keep_ctx or not has_ctx else ctx_tiles
            geo_i = geo if has_ctx else (0, geo[1], b)
            src = pair if i == 0 and keep_ctx else xs
            n_src = n_tiles_all if i == 0 else (xs["x"] if isinstance(xs, dict) else xs).shape[0] // TM
            xs, routed = _gmlp_mixer(src, mods, norm_mix[i], a_w_in[j], a_g_v[j], a_w_s[j], a_b_s[j], a_w_out[j],
                                     skip, n_src - skip, geo_i, route_w)
            has_ctx = keep_ctx
        elif kind == 1:
            assert has_ctx
            if not ctx_later:
                raise NotImplementedError("GLA mixer whose context rows are dropped before the FFN")
            xs, routed = _gla_mixer(xs, mods, norm_mix[i], b_w_in[j], b_w_a2[j], b_b_a[j], b_g_o[j], b_w_out[j],
                                    geo, dims, route_w)
        else:
            assert has_ctx
            if ctx_later:
                raise NotImplementedError("context output of the neighbourhood mixer")
            xs, routed = _na_mixer(xs, mods, norm_mix[i], c_w_qkv[j], c_rpb[j], c_w_out[j], geo, dims, route_w)
            has_ctx = False
        geo_i = geo if has_ctx else (0, geo[1], b)
        nxt = (i + 1) % N_MIXERS
        defer = i + 1 < depth and (nxt == 2 or (nxt == 0 and not has_ctx))
        xs = _moe_layer(xs, routed, mods, i, moe_w_gu, moe_b_gu, moe_w_down, moe_b_down, geo_i, norm_out,
                        i == depth - 1, defer)
    if has_ctx:
        xs = xs[b * lctx:]
    return xs.reshape(b, seq, d)
```

```python
import functools

import numpy as np
import jax
import jax.numpy as jnp
from jax import lax
from jax.experimental import pallas as pl
from jax.experimental.pallas import tpu as pltpu
from jax.experimental.pallas import tpu_sc as plsc

F32 = jnp.float32
BF16 = jnp.bfloat16
I32 = jnp.int32

LANES = 128
SUBLANES = 8
MXU_COLS = 256

NORM_EPS = 1e-6
GRID_W = 64
N_MIXERS = 3

CHUNK_A = 128
A_GROUPS = 8
GMLP_PART = 256
GLA_HEADS = 4
GLA_RANK = 16
GLA_TAU = 16.0
GLA_CHUNK = 128
ROPE_BASE = 10000.0
NA_HEADS = 16
NA_KH = 8
NA_KW = 16
NEG_INF = -1e30
NA_QROWS = 4
NA_WROWS = 12
NA_PAIRS_PER_STEP = 4
TOP_K = 4
SWIGLU_LIMIT = 7.0
SWIGLU_ALPHA = 1.702
MOE_BM = 1024
MOE_SUB = 256

TM = 512
GLA_TB = 256
GLA_PROJ_PART = 256
VMEM_LIMIT = 56 * 1024 * 1024


def _cparams(sem):
    return pltpu.CompilerParams(dimension_semantics=sem, vmem_limit_bytes=VMEM_LIMIT)


def _dot(a, b):
    return jnp.dot(a, b, preferred_element_type=F32)


def _dot_nt(a, b):
    return lax.dot_general(a, b, (((1,), (1,)), ((), ())), preferred_element_type=F32)


def _dot_tn(a, b):
    return lax.dot_general(a, b, (((0,), (0,)), ((), ())), preferred_element_type=F32)


def _rms(x, g):
    return x * lax.rsqrt(jnp.mean(x * x, axis=-1, keepdims=True) + NORM_EPS) * g


def _modulate(x, g, shift, scale):
    return _rms(x, g) * (1.0 + scale) + shift


ADA_COL_BLOCKS = 4


def _ada_kernel(st_ref, w_ref, b_ref, o_ref, *, nrows):
    st = st_ref[...]
    st = st * jax.nn.sigmoid(st)
    w = w_ref[0]
    rows = [jnp.sum(w * st[:, r:r + 1], axis=0, keepdims=True) for r in range(nrows)]
    rows += [jnp.zeros_like(rows[0])] * (o_ref.shape[1] - nrows)
    o_ref[0] = jnp.concatenate(rows, axis=0) + b_ref[0]


def _adaln(cond_t, nrows, ada_w, ada_b):
    depth, d, n6 = ada_w.shape
    bn = n6 // ADA_COL_BLOCKS
    return pl.pallas_call(
        functools.partial(_ada_kernel, nrows=nrows),
        grid=(depth, ADA_COL_BLOCKS),
        in_specs=[pl.BlockSpec((d, SUBLANES), lambda i, j: (0, 0)),
                  pl.BlockSpec((1, d, bn), lambda i, j: (i, 0, j)),
                  pl.BlockSpec((1, 1, bn), lambda i, j: (i, 0, j))],
        out_specs=pl.BlockSpec((1, SUBLANES, bn), lambda i, j: (i, 0, j)),
        out_shape=jax.ShapeDtypeStruct((depth, SUBLANES, n6), F32),
        compiler_params=_cparams(("parallel", "parallel")),
        name="adaln",
    )(cond_t, ada_w, ada_b.reshape(depth, 1, n6))


def _gelu(z):
    return 0.5 * z * (1.0 + lax.erf(z * np.float32(np.sqrt(0.5))))


def _gmlp_kernel(*refs, ctx_tiles, fused_combine=False):
    if fused_combine:
        xp_ref, yk_ref, gate_ref, mp_ref, *refs = refs
        *refs, x_ref = refs
        half = yk_ref.shape[2]
        x_ref[:, :half], x_ref[:, half:] = _combined_halves(xp_ref, mp_ref[0], yk_ref, gate_ref)
    elif ctx_tiles:
        xc_ref, xl_ref, *refs = refs
        *refs, x_ref = refs

        @pl.when(pl.program_id(0) < ctx_tiles)
        def _():
            x_ref[...] = xc_ref[...]

        @pl.when(pl.program_id(0) >= ctx_tiles)
        def _():
            x_ref[...] = xl_ref[...]
    else:
        x_ref, *refs = refs
    m_ref, g_ref, win_ref, gv_ref, ws_ref, bs_ref, wout_ref, *route_refs = refs
    route_in, o_ref, route_out = route_refs[:3], route_refs[3], route_refs[4:]
    m = m_ref[0]
    a = gv_ref.shape[1]
    gw = a // A_GROUPS
    npart = x_ref.shape[0] // GMLP_PART
    parts = [slice(p * GMLP_PART, (p + 1) * GMLP_PART) for p in range(npart)]
    z = [_dot(_modulate(x_ref[rs, :], g_ref[...], m[0:1], m[1:2]).astype(BF16), win_ref[...]) for rs in parts]
    for rs, zp in zip(parts, z):
        zp = _gelu(zp)
        u = zp[:, :a]
        v = _rms(zp[:, a:], gv_ref[...]).astype(BF16)
        rows = []
        for c in range(GMLP_PART // CHUNK_A):
            cols = [_dot(ws_ref[g], v[c * CHUNK_A:(c + 1) * CHUNK_A, g * gw:(g + 1) * gw])
                    for g in range(A_GROUPS)]
            rows.append(jnp.concatenate(cols, axis=1) + bs_ref[...])
        s = jnp.concatenate(rows, axis=0)
        y = _dot((u * s).astype(BF16), wout_ref[...])
        o_ref[rs, :] = x_ref[rs, :] + m[2:3] * y
    _route_tail(o_ref[...], m, *route_in, *route_out)


def _gmlp_mixer(xs, mods, g, w_in, g_v, w_s, b_s, w_out, tile0, n_tiles, geo, route_w):
    pending = isinstance(xs, dict)
    split = isinstance(xs, tuple)
    d = xs["x"].shape[1] if pending else xs[0].shape[1] if split else xs.shape[1]
    a = g_v.shape[0]
    gw = a // A_GROUPS
    bias = jnp.repeat(b_s.T, gw, axis=1)
    grp = functools.partial(_tile_group, geo=geo, tile0=tile0)
    const2 = lambda t: (0, 0)
    r_in, r_out, r_shape, r_scratch = _route_specs(n_tiles * TM, d, route_w[1].shape[1])
    if pending:
        assert tile0 == 0
        ctx_tiles = 0
        yk = xs["yk"]
        x_specs = [pl.BlockSpec((TM, d), lambda t: (t, 0)),
                   pl.BlockSpec((TOP_K, TM, yk.shape[2]), lambda t: (0, t, 0)),
                   pl.BlockSpec((TM, SUBLANES), lambda t: (t, 0)),
                   pl.BlockSpec((1, 6, d), lambda t: (grp(t), 0, 0))]
        x_args, x_scratch = [xs["x"], yk, xs["gate"], xs["mods"]], [pltpu.VMEM((TM, d), F32)]
    elif split:
        assert tile0 == 0
        ctx_tiles = xs[0].shape[0] // TM
        x_specs = [pl.BlockSpec((TM, d), lambda t: (jnp.minimum(t, ctx_tiles - 1), 0)),
                   pl.BlockSpec((TM, d), lambda t: (jnp.maximum(t - ctx_tiles, 0), 0))]
        x_args, x_scratch = list(xs), [pltpu.VMEM((TM, d), F32)]
    else:
        ctx_tiles = 0
        x_specs, x_args, x_scratch = [pl.BlockSpec((TM, d), lambda t: (t + tile0, 0))], [xs], []
    out = pl.pallas_call(
        functools.partial(_gmlp_kernel, ctx_tiles=ctx_tiles, fused_combine=pending),
        grid=(n_tiles,),
        in_specs=x_specs + [
                  pl.BlockSpec((1, 6, d), lambda t: (grp(t), 0, 0)),
                  pl.BlockSpec((1, d), const2),
                  pl.BlockSpec((d, 2 * a), const2),
                  pl.BlockSpec((1, a), const2),
                  pl.BlockSpec((A_GROUPS, CHUNK_A, CHUNK_A), lambda t: (0, 0, 0)),
                  pl.BlockSpec((CHUNK_A, a), const2),
                  pl.BlockSpec((a, d), const2)] + r_in,
        out_specs=[pl.BlockSpec((TM, d), lambda t: (t, 0))] + r_out,
        out_shape=[jax.ShapeDtypeStruct((n_tiles * TM, d), F32)] + r_shape,
        scratch_shapes=r_scratch + x_scratch,
        compiler_params=_cparams(("arbitrary",)),
        name="gmlp_mixer",
    )(*x_args, mods, g.reshape(1, d), w_in.astype(BF16), g_v.reshape(1, a), w_s.astype(BF16),
      bias, w_out.astype(BF16), *_route_args(route_w, d))
    return out[0], out[1:]


def _tile_group(t, geo, tile0=0):
    n_ctx_tiles, tiles_per_batch, nb = geo
    tt = t + tile0
    return jnp.where(tt < n_ctx_tiles, nb, (tt - n_ctx_tiles) // tiles_per_batch)


def _dot_f32(tri_bf16, x):
    hi = x.astype(BF16)
    r1 = x - hi.astype(F32)
    mid = r1.astype(BF16)
    lo = (r1 - mid.astype(F32)).astype(BF16)
    return _dot(tri_bf16, hi) + _dot(tri_bf16, mid) + _dot(tri_bf16, lo)


def _gla_proj_kernel(*refs, fused_combine=False):
    if fused_combine:
        xp_ref, yk_ref, gate_ref, mp_ref, *refs = refs
        *refs, x_ref = refs
        half = yk_ref.shape[2]
        x_ref[:, :half], x_ref[:, half:] = _combined_halves(xp_ref, mp_ref[0], yk_ref, gate_ref)
    else:
        x_ref, *refs = refs
    (m_ref, g_ref, wq_ref, wk_ref, wv_ref, wg_ref, wr_ref, wa_ref, ba_ref,
     cos_ref, sin_ref, q_ref, k_ref, v_ref, go_ref, la_ref) = refs
    m = m_ref[0]
    kdim = wq_ref.shape[1]
    hk = kdim // GLA_HEADS
    nf = hk // 4
    npart = x_ref.shape[0] // GLA_PROJ_PART
    parts = [slice(p * GLA_PROJ_PART, (p + 1) * GLA_PROJ_PART) for p in range(npart)]
    zs, qs, ks = [], [], []
    for rs in parts:
        h = _modulate(x_ref[rs, :], g_ref[...], m[0:1], m[1:2]).astype(BF16)
        zs.append(_dot(_dot(h, wr_ref[...]).astype(BF16), wa_ref[...]) + ba_ref[...])
        qs.append(_dot(h, wq_ref[...]) * np.float32(hk ** -0.5))
        ks.append(_dot(h, wk_ref[...]))
        v_ref[rs, :] = _dot(h, wv_ref[...]).astype(BF16)
        go_ref[rs, :] = _dot(h, wg_ref[...])
    for rs, z in zip(parts, zs):
        la_ref[rs, :] = jax.nn.log_sigmoid(z) * np.float32(1.0 / GLA_TAU)

    lane = lax.broadcasted_iota(I32, (GLA_PROJ_PART, kdim), 1)
    first = (lane % (2 * nf)) < nf
    for rs, q, k in zip(parts, qs, ks):
        cos = jnp.concatenate([cos_ref[rs, :]] * GLA_HEADS, axis=1)
        sin = jnp.concatenate([sin_ref[rs, :]] * GLA_HEADS, axis=1)

        def rope(t):
            up = pltpu.roll(t, kdim - nf, 1)
            dn = pltpu.roll(t, nf, 1)
            return t * cos + jnp.where(first, up, dn) * sin

        q_ref[rs, :] = rope(q)
        k_ref[rs, :] = rope(k)


def _gla_decays(q_ref, k_ref, la_ref, n, rev):
    c = GLA_CHUNK
    ri = lax.broadcasted_iota(I32, (c, c), 0)
    ci = lax.broadcasted_iota(I32, (c, c), 1)
    keep = (ci >= ri) if rev else (ci <= ri)
    rows = slice(n * c, (n + 1) * c)
    cum = _dot_f32(keep.astype(BF16), la_ref[rows, :])
    last = cum[0:1] if rev else cum[c - 1:c]
    q = q_ref[rows, :]
    k = k_ref[rows, :]
    mid = cum[c // 2:c // 2 + 1]
    return dict(rows=rows, keep=keep, dec=jnp.exp(last),
                q_dec=(q * jnp.exp(cum)).astype(BF16),
                q_att=(q * jnp.exp(cum - mid)).astype(BF16),
                k_att=(k * jnp.exp(mid - cum)).astype(BF16),
                k_end=(k * jnp.exp(last - cum)).astype(BF16))


def _gla_scan_kernel(qf_ref, kf_ref, vf_ref, laf_ref, qb_ref, kb_ref, vb_ref, lab_ref,
                     of_ref, ob_ref, stf_ref, stb_ref):
    s = pl.program_id(1)

    @pl.when(s == 0)
    def _():
        stf_ref[...] = jnp.zeros_like(stf_ref)
        stb_ref[...] = jnp.zeros_like(stb_ref)

    nch = qf_ref.shape[0] // GLA_CHUNK
    hk = qf_ref.shape[1] // GLA_HEADS
    hv = vf_ref.shape[1] // GLA_HEADS
    dirs = ((qf_ref, kf_ref, vf_ref, laf_ref, of_ref, stf_ref, False),
            (qb_ref, kb_ref, vb_ref, lab_ref, ob_ref, stb_ref, True))
    heads = [(slice(h * hk, (h + 1) * hk), slice(h * hv, (h + 1) * hv)) for h in range(GLA_HEADS)]
    for n in range(nch):
        w = [_gla_decays(q_ref, k_ref, la_ref, nch - 1 - n if rev else n, rev)
             for q_ref, k_ref, _, la_ref, _, _, rev in dirs]
        att = [[jnp.where(w[d]["keep"], _dot_nt(w[d]["q_att"][:, ks], w[d]["k_att"][:, ks]), 0.0).astype(BF16)
                for ks, _ in heads] for d in range(2)]
        for d, (_, _, v_ref, _, o_ref, st_ref, _) in enumerate(dirs):
            for h, (ks, vs) in enumerate(heads):
                o_ref[w[d]["rows"], vs] = (_dot(att[d][h], v_ref[w[d]["rows"], vs])
                                           + _dot_nt(w[d]["q_dec"][:, ks], st_ref[h].astype(BF16)))
        for d, (_, _, v_ref, _, _, st_ref, _) in enumerate(dirs):
            for h, (ks, vs) in enumerate(heads):
                st_ref[h] = (st_ref[h] * w[d]["dec"][:, ks]
                             + _dot_tn(v_ref[w[d]["rows"], vs], w[d]["k_end"][:, ks]))


def _gla_out_kernel(x_ref, m_ref, of_ref, ob_ref, go_ref, gn_ref, wout_ref, *route_refs):
    route_in, o_ref, route_out = route_refs[:3], route_refs[3], route_refs[4:]
    x = x_ref[...]
    m = m_ref[0]
    o = of_ref[...] + ob_ref[...]
    hv = gn_ref.shape[1]
    parts = [_rms(o[:, h * hv:(h + 1) * hv], gn_ref[...]) for h in range(GLA_HEADS)]
    o = jnp.concatenate(parts, axis=1)
    gate = go_ref[...]
    y = _dot((o * (gate * jax.nn.sigmoid(gate))).astype(BF16), wout_ref[...])
    x_new = x + m[2:3] * y
    o_ref[...] = x_new
    _route_tail(x_new, m, *route_in, *route_out)


def _gla_mixer(xs, mods, g, w_in, w_a2, b_a, g_o, w_out, geo, dims, route_w):
    b, seq, lctx = dims
    pending = isinstance(xs, dict)
    nt, d = xs["x"].shape if pending else xs.shape
    n_tiles = nt // TM
    kdim = w_a2.shape[2]
    vdim = g_o.shape[0] * GLA_HEADS
    hk = kdim // GLA_HEADS
    nf = hk // 4
    wq = w_in[:, :kdim].astype(BF16)
    wk = w_in[:, kdim:2 * kdim].astype(BF16)
    wv = w_in[:, 2 * kdim:2 * kdim + vdim].astype(BF16)
    wg = w_in[:, 2 * kdim + vdim:2 * kdim + 2 * vdim].astype(BF16)
    wr = jnp.pad(w_in[:, 2 * kdim + 2 * vdim:], ((0, 0), (0, 128 - 2 * GLA_RANK))).astype(BF16)
    wa = jnp.zeros((128, 2 * kdim), F32)
    wa = wa.at[:GLA_RANK, :kdim].set(w_a2[0]).at[GLA_RANK:2 * GLA_RANK, kdim:].set(w_a2[1]).astype(BF16)
    ba = b_a.reshape(1, 2 * kdim)

    tpos = np.arange(seq)
    freqs = np.power(np.float32(ROPE_BASE), -np.arange(nf, dtype=np.float32) / np.float32(nf)).astype(np.float32)
    ar = (tpos // GRID_W).astype(np.float32)[:, None] * freqs
    ac = (tpos % GRID_W).astype(np.float32)[:, None] * freqs
    cos = np.concatenate([np.cos(ar), np.cos(ar), np.cos(ac), np.cos(ac)], axis=1)
    sin = np.concatenate([-np.sin(ar), np.sin(ar), -np.sin(ac), np.sin(ac)], axis=1)
    cos = jnp.asarray(np.concatenate([np.ones((TM, hk), np.float32), cos], axis=0).astype(np.float32))
    sin = jnp.asarray(np.concatenate([np.zeros((TM, hk), np.float32), sin], axis=0).astype(np.float32))

    n_ctx_tiles, tiles_per_batch, _ = geo
    grp = functools.partial(_tile_group, geo=geo)

    def rope_blk(t):
        return (jnp.where(t < n_ctx_tiles, 0, 1 + (t - n_ctx_tiles) % tiles_per_batch), 0)

    const2 = lambda t: (0, 0)
    row = lambda t: (t, 0)
    mod_spec = pl.BlockSpec((1, 6, d), lambda t: (grp(t), 0, 0))
    x_specs, x_args, x_out, x_shape = [pl.BlockSpec((TM, d), row)], [xs], [], []
    if pending:
        yk = xs["yk"]
        x_specs += [pl.BlockSpec((TOP_K, TM, yk.shape[2]), lambda t: (0, t, 0)),
                    pl.BlockSpec((TM, SUBLANES), row), mod_spec]
        x_args = [xs["x"], yk, xs["gate"], xs["mods"]]
        x_out, x_shape = [pl.BlockSpec((TM, d), row)], [jax.ShapeDtypeStruct((nt, d), F32)]
    q, k, v, go, la, *x_new = pl.pallas_call(
        functools.partial(_gla_proj_kernel, fused_combine=pending),
        grid=(n_tiles,),
        in_specs=x_specs + [
                  mod_spec,
                  pl.BlockSpec((1, d), const2),
                  pl.BlockSpec((d, kdim), const2), pl.BlockSpec((d, kdim), const2),
                  pl.BlockSpec((d, vdim), const2), pl.BlockSpec((d, vdim), const2),
                  pl.BlockSpec((d, 128), const2), pl.BlockSpec((128, 2 * kdim), const2),
                  pl.BlockSpec((1, 2 * kdim), const2),
                  pl.BlockSpec((TM, hk), rope_blk), pl.BlockSpec((TM, hk), rope_blk)],
        out_specs=[pl.BlockSpec((TM, kdim), row), pl.BlockSpec((TM, kdim), row),
                   pl.BlockSpec((TM, vdim), row), pl.BlockSpec((TM, vdim), row),
                   pl.BlockSpec((TM, 2 * kdim), row)] + x_out,
        out_shape=[jax.ShapeDtypeStruct((nt, kdim), F32), jax.ShapeDtypeStruct((nt, kdim), F32),
                   jax.ShapeDtypeStruct((nt, vdim), BF16), jax.ShapeDtypeStruct((nt, vdim), F32),
                   jax.ShapeDtypeStruct((nt, 2 * kdim), F32)] + x_shape,
        compiler_params=_cparams(("parallel",)),
        name="gla_proj",
    )(*x_args, mods, g.reshape(1, d), wq, wk, wv, wg, wr, wa, ba, cos, sin)
    if pending:
        xs = x_new[0]

    tb = GLA_TB
    ctx_steps = lctx // tb
    lat_steps = seq // tb
    steps = ctx_steps + lat_steps

    def blk(bi, s, rev):
        if rev:
            cs, ls = ctx_steps - 1 - s, lat_steps - 1 - (s - ctx_steps)
        else:
            cs, ls = s, s - ctx_steps
        return jnp.where(s < ctx_steps, bi * ctx_steps + cs, b * ctx_steps + bi * lat_steps + ls)

    def dir_specs(rev):
        row_blk = lambda bi, s: (blk(bi, s, rev), 0)
        return [pl.BlockSpec((tb, kdim), row_blk), pl.BlockSpec((tb, kdim), row_blk),
                pl.BlockSpec((tb, vdim), row_blk),
                pl.BlockSpec((tb, kdim), lambda bi, s: (blk(bi, s, rev), 1 if rev else 0))]

    state = pltpu.VMEM((GLA_HEADS, vdim // GLA_HEADS, hk), F32)
    o_f, o_b = pl.pallas_call(
        _gla_scan_kernel,
        grid=(b, steps),
        in_specs=dir_specs(False) + dir_specs(True),
        out_specs=[pl.BlockSpec((tb, vdim), lambda bi, s: (blk(bi, s, False), 0)),
                   pl.BlockSpec((tb, vdim), lambda bi, s: (blk(bi, s, True), 0))],
        out_shape=[jax.ShapeDtypeStruct((nt, vdim), F32)] * 2,
        scratch_shapes=[state, state],
        compiler_params=_cparams(("parallel", "arbitrary")),
        name="gla_scan",
    )(q, k, v, la, q, k, v, la)

    r_in, r_out, r_shape, r_scratch = _route_specs(nt, d, route_w[1].shape[1])
    out = pl.pallas_call(
        _gla_out_kernel,
        grid=(n_tiles,),
        in_specs=[pl.BlockSpec((TM, d), row),
                  pl.BlockSpec((1, 6, d), lambda t: (grp(t), 0, 0)),
                  pl.BlockSpec((TM, vdim), row), pl.BlockSpec((TM, vdim), row),
                  pl.BlockSpec((TM, vdim), row),
                  pl.BlockSpec((1, vdim // GLA_HEADS), const2),
                  pl.BlockSpec((vdim, d), const2)] + r_in,
        out_specs=[pl.BlockSpec((TM, d), row)] + r_out,
        out_shape=[jax.ShapeDtypeStruct((nt, d), F32)] + r_shape,
        scratch_shapes=r_scratch,
        compiler_params=_cparams(("arbitrary",)),
        name="gla_out",
    )(xs, mods, o_f, o_b, go, g_o.reshape(1, -1), w_out.astype(BF16), *_route_args(route_w, d))
    return out[0], out[1:]


def _na_proj_kernel(*refs, fused_combine=False):
    if fused_combine:
        xp_ref, yk_ref, gate_ref, mp_ref, *refs = refs
        *refs, x_ref = refs
        half = yk_ref.shape[2]
        x_ref[:, :half], x_ref[:, half:] = _combined_halves(xp_ref, mp_ref[0], yk_ref, gate_ref)
        m_ref, g_ref, w_ref, q_ref, k_ref, v_ref = refs
    else:
        x_ref, m_ref, g_ref, w_ref, q_ref, k_ref, v_ref = refs
    x = x_ref[...]
    m = m_ref[0]
    d = x.shape[1]
    h = _modulate(x, g_ref[...], m[0:1], m[1:2]).astype(BF16)
    qkv = _dot(h, w_ref[...])
    hd = d // NA_HEADS
    for p in range(q_ref.shape[0]):
        cs = slice(p * LANES, (p + 1) * LANES)
        q_ref[p] = (qkv[:, cs] * np.float32(hd ** -0.5)).astype(BF16)
        k_ref[p] = qkv[:, d + p * LANES:d + (p + 1) * LANES].astype(BF16)
        v_ref[p] = qkv[:, 2 * d + p * LANES:2 * d + (p + 1) * LANES].astype(BF16)


def _na_attn_kernel(tbl_ref, q_ref, *refs, n_tiles, nwin):
    k_refs, v_refs = refs[:nwin], refs[nwin:2 * nwin]
    kc_ref, vc_ref, bias_ref, o_ref, s_ref, p_ref = refs[2 * nwin:]
    t = pl.program_id(2)
    typ = jnp.where(t == 0, 0, jnp.where(t == n_tiles - 1, 2, 1))
    npair = NA_WROWS // 2
    nlat = NA_WROWS * GRID_W
    hd = LANES // 2
    lane = lax.broadcasted_iota(I32, q_ref.shape[1:], 1)
    npp, tq = q_ref.shape[0], q_ref.shape[1]
    for pp in range(npp):
        q = q_ref[pp]
        k_all = jnp.concatenate([r[pp] for r in k_refs] + [kc_ref[pp]], axis=0)
        zero = jnp.zeros_like(q)
        q2 = jnp.concatenate([jnp.where(lane < hd, q, zero), jnp.where(lane >= hd, q, zero)], axis=0)
        s_ref[pp] = _dot_nt(q2, k_all)
    for pp in range(npp):
        for hh in range(2):
            for dr in range(NA_QROWS):
                rs = slice(hh * tq + dr * GRID_W, hh * tq + (dr + 1) * GRID_W)
                lat = [s_ref[pp, rs, j * LANES:(j + 1) * LANES]
                       + bias_ref[2 * pp + hh, tbl_ref[typ * (NA_QROWS * npair) + dr * npair + j]]
                       for j in range(npair)]
                sb = jnp.concatenate(lat + [s_ref[pp, rs, nlat:]], axis=1)
                e = jnp.exp(sb - jnp.max(sb, axis=1, keepdims=True))
                p_ref[pp, rs, :] = (e * (1.0 / jnp.sum(e, axis=1, keepdims=True))).astype(BF16)
    outs = []
    for pp in range(npp):
        v_all = jnp.concatenate([r[pp] for r in v_refs] + [vc_ref[pp]], axis=0)
        o2 = _dot(p_ref[pp], v_all)
        outs.append(jnp.where(lane < hd, o2[:tq], o2[tq:]))
    o_ref[...] = jnp.concatenate(outs, axis=1).astype(BF16)


def _na_out_kernel(x_ref, m_ref, a_ref, w_ref, *route_refs):
    route_in, o_ref, route_out = route_refs[:3], route_refs[3], route_refs[4:]
    m = m_ref[0]
    x_new = x_ref[...] + m[2:3] * _dot(a_ref[...], w_ref[...])
    o_ref[...] = x_new
    _route_tail(x_new, m, *route_in, *route_out)


def _na_tables(rows):
    npair = NA_WROWS // 2
    tbl = np.zeros((3, NA_QROWS, npair), np.int32)
    for typ, r_base in enumerate((0, NA_QROWS, rows - NA_QROWS)):
        w0 = int(np.clip(r_base - NA_KH // 2, 0, rows - NA_WROWS))
        for dr in range(NA_QROWS):
            r = r_base + dr
            r0 = int(np.clip(r - NA_KH // 2, 0, rows - NA_KH))
            for j in range(npair):
                kr = (w0 + 2 * j, w0 + 2 * j + 1)
                ok = [r0 <= x < r0 + NA_KH for x in kr]
                ri = [x - r + NA_KH - 1 for x in kr]
                if ok[0] and ok[1]:
                    e = ri[0]
                elif ok[0]:
                    e = 16 + ri[0]
                elif ok[1]:
                    e = 32 + ri[1]
                else:
                    e = 63
                tbl[typ, dr, j] = e
    return tbl.reshape(-1)


def _na_bias_kernel(rpb_ref, o_ref, *, nr, nw):
    h = pl.program_id(0)
    c = lax.broadcasted_iota(I32, (GRID_W, GRID_W), 0)
    k = lax.broadcasted_iota(I32, (GRID_W, GRID_W), 1)
    cstart = jnp.clip(c - NA_KW // 2, 0, GRID_W - NA_KW)
    ok = jnp.logical_and(k >= cstart, k < cstart + NA_KW)
    rel = k - c + NA_KW - 1
    neg = jnp.full((GRID_W, GRID_W), NEG_INF, F32)
    cbs = []
    for r in range(nr):
        acc = neg
        for w in range(nw):
            acc = jnp.where(rel == w, rpb_ref[(h * nr + r) * nw + w], acc)
        cbs.append(jnp.where(ok, acc, neg))
    cbs += [neg] * (17 - nr)
    for r in range(16):
        o_ref[0, r] = jnp.concatenate([cbs[r], cbs[r + 1]], axis=1)
        o_ref[0, 16 + r] = jnp.concatenate([cbs[r], neg], axis=1)
        o_ref[0, 32 + r] = jnp.concatenate([neg, cbs[r]], axis=1)
        o_ref[0, 48 + r] = jnp.concatenate([neg, neg], axis=1)


def _na_bias_table(rpb):
    nh, nr, nw = rpb.shape
    return pl.pallas_call(
        functools.partial(_na_bias_kernel, nr=nr, nw=nw),
        grid=(nh,),
        in_specs=[pl.BlockSpec(memory_space=pltpu.SMEM)],
        out_specs=pl.BlockSpec((1, 64, GRID_W, 2 * GRID_W), lambda h: (h, 0, 0, 0)),
        out_shape=jax.ShapeDtypeStruct((nh, 64, GRID_W, 2 * GRID_W), F32),
        compiler_params=_cparams(("parallel",)),
        name="na_bias_table",
    )(rpb.reshape(-1).astype(F32))


def _na_mixer(xs, mods, g, w_qkv, rpb, w_out, geo, dims, route_w):
    b, seq, lctx = dims
    pending = isinstance(xs, dict)
    nt, d = xs["x"].shape if pending else xs.shape
    n_tiles_tok = nt // TM
    grp = functools.partial(_tile_group, geo=geo)
    npairs = d // 128
    const2 = lambda t: (0, 0)
    mod_spec = pl.BlockSpec((1, 6, d), lambda t: (grp(t), 0, 0))
    x_specs, x_args, x_out, x_shape = [pl.BlockSpec((TM, d), lambda t: (t, 0))], [xs], [], []
    if pending:
        yk = xs["yk"]
        x_specs += [pl.BlockSpec((TOP_K, TM, yk.shape[2]), lambda t: (0, t, 0)),
                    pl.BlockSpec((TM, SUBLANES), lambda t: (t, 0)), mod_spec]
        x_args = [xs["x"], yk, xs["gate"], xs["mods"]]
        x_out, x_shape = [pl.BlockSpec((TM, d), lambda t: (t, 0))], [jax.ShapeDtypeStruct((nt, d), F32)]
    q, k, v, *x_new = pl.pallas_call(
        functools.partial(_na_proj_kernel, fused_combine=pending),
        grid=(n_tiles_tok,),
        in_specs=x_specs + [mod_spec,
                            pl.BlockSpec((1, d), const2),
                            pl.BlockSpec((d, 3 * d), const2)],
        out_specs=[pl.BlockSpec((npairs, TM, 128), lambda t: (0, t, 0))] * 3 + x_out,
        out_shape=[jax.ShapeDtypeStruct((npairs, nt, 128), BF16)] * 3 + x_shape,
        compiler_params=_cparams(("parallel",)),
        name="na_proj",
    )(*x_args, mods, g.reshape(1, d), w_qkv.astype(BF16))
    if pending:
        xs = x_new[0]

    rows = seq // GRID_W
    tq = NA_QROWS * GRID_W
    n_tiles = rows // NA_QROWS
    wb = 256
    nwin = NA_WROWS * GRID_W // wb
    lat0 = b * lctx
    tbl = jnp.asarray(_na_tables(rows))
    bias = _na_bias_table(rpb)

    pp = NA_PAIRS_PER_STEP
    assert npairs % pp == 0 and (NA_QROWS * GRID_W) % wb == 0 and (NA_KH // 2 * GRID_W) % wb == 0

    def win(i):
        def f(p, bi, t, tbl_ref):
            w = jnp.clip(t * (NA_QROWS * GRID_W // wb) - NA_KH // 2 * GRID_W // wb, 0, seq // wb - nwin)
            return (p, (lat0 + bi * seq) // wb + w + i, 0)
        return f

    kv_specs = [pl.BlockSpec((pp, wb, 128), win(i)) for i in range(nwin)]
    attn = pl.pallas_call(
        functools.partial(_na_attn_kernel, n_tiles=n_tiles, nwin=nwin),
        grid_spec=pltpu.PrefetchScalarGridSpec(
            num_scalar_prefetch=1,
            grid=(npairs // pp, b, n_tiles),
            in_specs=[pl.BlockSpec((pp, tq, 128), lambda p, bi, t, tr: (p, (lat0 + bi * seq) // tq + t, 0))]
                     + kv_specs + kv_specs
                     + [pl.BlockSpec((pp, lctx, 128), lambda p, bi, t, tr: (p, bi, 0)),
                        pl.BlockSpec((pp, lctx, 128), lambda p, bi, t, tr: (p, bi, 0)),
                        pl.BlockSpec((2 * pp, 64, GRID_W, 2 * GRID_W), lambda p, bi, t, tr: (p, 0, 0, 0),
                                     pipeline_mode=pl.Buffered(1))],
            out_specs=pl.BlockSpec((tq, 128 * pp), lambda p, bi, t, tr: (bi * n_tiles + t, p)),
            scratch_shapes=[pltpu.VMEM((pp, 2 * tq, NA_WROWS * GRID_W + lctx), F32),
                            pltpu.VMEM((pp, 2 * tq, NA_WROWS * GRID_W + lctx), BF16)]),
        out_shape=jax.ShapeDtypeStruct((b * seq, d), BF16),
        compiler_params=_cparams(("parallel", "parallel", "arbitrary")),
        name="na_attn",
    )(tbl, q, *([k] * nwin), *([v] * nwin), k, v, bias)

    n_lat_tiles = b * seq // TM
    tile0 = lat0 // TM
    grp_l = functools.partial(_tile_group, geo=geo, tile0=tile0)
    r_in, r_out, r_shape, r_scratch = _route_specs(b * seq, d, route_w[1].shape[1])
    out = pl.pallas_call(
        _na_out_kernel,
        grid=(n_lat_tiles,),
        in_specs=[pl.BlockSpec((TM, d), lambda t: (t + tile0, 0)),
                  pl.BlockSpec((1, 6, d), lambda t: (grp_l(t), 0, 0)),
                  pl.BlockSpec((TM, d), lambda t: (t, 0)),
                  pl.BlockSpec((d, d), const2)] + r_in,
        out_specs=[pl.BlockSpec((TM, d), lambda t: (t, 0))] + r_out,
        out_shape=[jax.ShapeDtypeStruct((b * seq, d), F32)] + r_shape,
        scratch_shapes=r_scratch,
        compiler_params=_cparams(("arbitrary",)),
        name="na_out",
    )(xs, mods, attn, w_out.astype(BF16), *_route_args(route_w, d))
    return out[0], out[1:]


SC_CORES = 2
SC_SUBCORES = 16
SC_WORKERS = SC_CORES * SC_SUBCORES
SC_MAX_CHUNK = 96
SLOT_T_MAX = 2048


def _sc_mesh():
    return plsc.VectorSubcoreMesh(core_axis_name="c", subcore_axis_name="s")


def _sc_chunk(per_worker, max_chunk):
    return max(c for c in range(8, max_chunk + 1, 8) if per_worker % (2 * c) == 0)


def _sc_gather_rows(table, idx):
    dd = table.shape[1]
    bsz = idx.shape[0]
    per_w = bsz // SC_WORKERS
    assert per_w * SC_WORKERS == bsz
    chunk = _sc_chunk(per_w, SC_MAX_CHUNK)
    n_chunks = per_w // chunk

    def body(table_hbm, idx_hbm, out_hbm, idx_v, rows0, rows1, g0, g1, w0, w1):
        wid = lax.axis_index("s") * SC_CORES + lax.axis_index("c")
        pltpu.sync_copy(idx_hbm.at[wid], idx_v)
        base = wid * per_w

        def out_rows(j):
            return out_hbm.at[pl.ds(pl.multiple_of(base + j * chunk, 8), chunk)]

        def step(i, carry):
            j0, j1 = 2 * i, 2 * i + 1
            ga = pltpu.async_copy(table_hbm.at[idx_v.at[j0]], rows0, g0)
            gb = pltpu.async_copy(table_hbm.at[idx_v.at[j1]], rows1, g1)
            ga.wait()
            wa = pltpu.async_copy(rows0, out_rows(j0), w0)
            gb.wait()
            wb = pltpu.async_copy(rows1, out_rows(j1), w1)
            wa.wait()
            wb.wait()
            return carry

        lax.fori_loop(0, n_chunks // 2, step, 0)

    return pl.kernel(
        body, out_type=jax.ShapeDtypeStruct((bsz, dd), table.dtype), mesh=_sc_mesh(),
        scratch_types=[pltpu.VMEM((n_chunks, chunk), I32),
                       pltpu.VMEM((chunk, dd), table.dtype), pltpu.VMEM((chunk, dd), table.dtype),
                       pltpu.SemaphoreType.DMA, pltpu.SemaphoreType.DMA,
                       pltpu.SemaphoreType.DMA, pltpu.SemaphoreType.DMA],
        name="sc_gather_rows",
    )(table, idx.reshape(SC_WORKERS, n_chunks, chunk))


def _sc_scatter_rows(rows, dest, n_out):
    n, dd = rows.shape
    kk = dest.shape[0]
    per_w = n // SC_WORKERS
    assert per_w * SC_WORKERS == n
    chunk = _sc_chunk(per_w, SC_MAX_CHUNK)
    n_chunks = per_w // chunk
    dest_w = dest.reshape(kk, SC_WORKERS, n_chunks, chunk).transpose(1, 2, 0, 3)
    dest_w = dest_w.reshape(SC_WORKERS, n_chunks * kk, chunk)

    def body(rows_hbm, dest_hbm, out_hbm, idx_v, rows0, rows1, r0, r1, s0, s1):
        wid = lax.axis_index("s") * SC_CORES + lax.axis_index("c")
        pltpu.sync_copy(dest_hbm.at[wid], idx_v)
        base = wid * per_w

        def in_rows(j):
            return rows_hbm.at[pl.ds(pl.multiple_of(base + j * chunk, 8), chunk)]

        def step(i, carry):
            j0, j1 = 2 * i, 2 * i + 1
            ra = pltpu.async_copy(in_rows(j0), rows0, r0)
            rb = pltpu.async_copy(in_rows(j1), rows1, r1)
            ra.wait()
            sa = [pltpu.async_copy(rows0, out_hbm.at[idx_v.at[j0 * kk + k]], s0) for k in range(kk)]
            rb.wait()
            sb = [pltpu.async_copy(rows1, out_hbm.at[idx_v.at[j1 * kk + k]], s1) for k in range(kk)]
            for cp in sa + sb:
                cp.wait()
            return carry

        lax.fori_loop(0, n_chunks // 2, step, 0)

    return pl.kernel(
        body, out_type=jax.ShapeDtypeStruct((n_out, dd), rows.dtype), mesh=_sc_mesh(),
        scratch_types=[pltpu.VMEM((n_chunks * kk, chunk), I32),
                       pltpu.VMEM((chunk, dd), rows.dtype), pltpu.VMEM((chunk, dd), rows.dtype),
                       pltpu.SemaphoreType.DMA, pltpu.SemaphoreType.DMA,
                       pltpu.SemaphoreType.DMA, pltpu.SemaphoreType.DMA],
        name="sc_scatter_rows",
    )(rows, dest_w)


def _prefix_sum_rows(col):
    nr = col.shape[0]
    acc = jnp.broadcast_to(col, (nr, LANES))
    row = lax.broadcasted_iota(I32, (nr, LANES), 0)
    s = 1
    while s < nr:
        acc = acc + jnp.where(row >= s, pltpu.roll(acc, s, 0), 0.0)
        s *= 2
    return acc[:, 0:1]


def _slot_kernel(idx_ref, rank_ref, cnt_ref, dest_ref, btab_ref, *, bm):
    cnt = cnt_ref[...]
    ne = cnt.shape[0]
    padded = jnp.floor((cnt + (bm - 1.0)) * (1.0 / bm)) * bm
    pad_end = _prefix_sum_rows(padded)
    pad_start = pad_end - padded
    idx = idx_ref[...]
    e_iota = lax.broadcasted_iota(I32, (ne, idx.shape[1]), 0)
    starts = [jnp.sum(jnp.where(e_iota == idx[k:k + 1], pad_start, 0.0), axis=0, keepdims=True)
              for k in range(idx.shape[0])]
    dest_ref[...] = jnp.concatenate(starts, axis=0).astype(I32) + rank_ref[...]

    nbp = btab_ref.shape[1]
    bstart = lax.broadcasted_iota(I32, (ne, nbp), 1).astype(F32) * bm
    be = jnp.minimum(jnp.sum((pad_end <= bstart).astype(F32), axis=0, keepdims=True), ne - 1.0)
    mine = lax.broadcasted_iota(I32, (ne, nbp), 0).astype(F32) == be
    pick = lambda col: jnp.sum(jnp.where(mine, col, 0.0), axis=0, keepdims=True)
    live = jnp.clip(pick(pad_start + cnt) - bstart[0:1], 0.0, bm)
    r = lax.broadcasted_iota(I32, (ne, ne), 0)
    c = lax.broadcasted_iota(I32, (ne, ne), 1)
    ends_on_lanes = _dot_f32(jnp.ones((ne, ne), BF16), jnp.where(r == c, pad_end, 0.0))
    nxt_e = jnp.sum((ends_on_lanes <= pad_end).astype(F32), axis=1, keepdims=True)
    nxt_e = jnp.where(nxt_e < ne, nxt_e, -1.0)
    ordinal = _prefix_sum_rows((cnt > 0).astype(F32)) - 1.0
    slot_e = ordinal - 2.0 * jnp.floor(ordinal * 0.5)
    rowi = lax.broadcasted_iota(I32, btab_ref.shape, 0)
    tab = jnp.where(rowi == 0, be, jnp.where(rowi == 1, live, jnp.where(rowi == 2, pick(nxt_e), pick(slot_e))))
    btab_ref[...] = tab.astype(I32)


def _route_specs(n, d, ne):
    const2 = lambda t: (0, 0)
    in_specs = [pl.BlockSpec((1, d), const2), pl.BlockSpec((ne, d), const2), pl.BlockSpec((ne, 1), const2)]
    out_specs = [pl.BlockSpec((TM, d // 2), lambda t: (t, 0)),
                 pl.BlockSpec((TOP_K, TM), lambda t: (0, t)),
                 pl.BlockSpec((TM, SUBLANES), lambda t: (t, 0)),
                 pl.BlockSpec((TOP_K, TM), lambda t: (0, t)),
                 pl.BlockSpec((ne, 1), const2)]
    out_shape = [jax.ShapeDtypeStruct((n, d // 2), jnp.uint32),
                 jax.ShapeDtypeStruct((TOP_K, n), I32),
                 jax.ShapeDtypeStruct((n, SUBLANES), F32),
                 jax.ShapeDtypeStruct((TOP_K, n), I32),
                 jax.ShapeDtypeStruct((ne, 1), F32)]
    return in_specs, out_specs, out_shape, [pltpu.VMEM((ne, 1), F32)]


def _route_args(route_w, d):
    g_ffn, w_router, b_router = route_w
    ne = w_router.shape[1]
    return g_ffn.reshape(1, d), w_router.T.astype(BF16), b_router.reshape(ne, 1)


def _route_tail(x, m, g_ref, wr_ref, br_ref, h_ref, idx_ref, gate_ref, rank_ref, cnt_ref, run_ref):
    t = pl.program_id(0)

    @pl.when(t == 0)
    def _():
        run_ref[...] = jnp.zeros_like(run_ref)

    h = _modulate(x, g_ref[...], m[3:4], m[4:5]).astype(BF16)
    h_ref[...] = _pack_bf16_pairs(h)
    logits = _dot_nt(wr_ref[...], h) + br_ref[...]
    ne, tm = logits.shape
    e_iota = lax.broadcasted_iota(I32, (ne, tm), 0)
    vals, idxs = [], []
    l = logits
    for _ in range(TOP_K):
        mk = jnp.max(l, axis=0, keepdims=True)
        ik = jnp.min(jnp.where(l == mk, e_iota, ne), axis=0, keepdims=True)
        vals.append(mk)
        idxs.append(ik)
        l = jnp.where(e_iota == ik, -jnp.inf, l)
    top_val = jnp.concatenate(vals, axis=0)
    ex = jnp.exp(top_val - vals[0])
    gates = ex / jnp.sum(ex, axis=0, keepdims=True)
    pad = jnp.zeros((gate_ref.shape[1] - TOP_K, tm), F32)
    gate_ref[...] = jnp.concatenate([gates, pad], axis=0).T
    idx_ref[...] = jnp.concatenate(idxs, axis=0)

    hits = [e_iota == ik for ik in idxs]
    cnt = hits[0].astype(F32)
    for hk in hits[1:]:
        cnt = cnt + hk.astype(F32)
    si = lax.broadcasted_iota(I32, (tm, tm), 0)
    ti = lax.broadcasted_iota(I32, (tm, tm), 1)
    before = (si < ti).astype(BF16)
    total = _dot(cnt.astype(BF16), before) + run_ref[...]
    ranks = [jnp.sum(jnp.where(hk, total, 0.0), axis=0, keepdims=True) for hk in hits]
    rank_ref[...] = jnp.concatenate(ranks, axis=0).astype(I32)
    run_ref[...] = run_ref[...] + jnp.sum(cnt, axis=1, keepdims=True)
    cnt_ref[...] = run_ref[...]


def _pack_bf16_pairs(v):
    bits = lax.bitcast_convert_type(v.astype(BF16).astype(F32), jnp.uint32)
    half = bits.shape[1] // 2
    return (bits[:, half:] & jnp.uint32(0xFFFF0000)) | (bits[:, :half] >> 16)


def _unpack_bf16_pairs(w):
    return (lax.bitcast_convert_type(w << 16, F32),
            lax.bitcast_convert_type(w & jnp.uint32(0xFFFF0000), F32))


def _expert_kernel(tab_ref, x_ref, wgu_hbm, bgu_ref, wd_hbm, bd_ref, o_ref,
                   wgu_in, wd_in, wgu_bf, wd_bf, act_ref, sem_gu, sem_d, *, layer):
    blk = pl.program_id(0)
    ch = wgu_bf.shape[2]
    f = act_ref.shape[1]
    nc = f // ch
    expert, live, nxt, wslot = (tab_ref[i, blk] for i in range(4))
    used = live > 0
    fresh = jnp.logical_or(blk == 0, expert != tab_ref[0, jnp.maximum(blk - 1, 0)])

    def fetch(e, slot):
        return (pltpu.make_async_copy(wgu_hbm.at[layer, e], wgu_in.at[slot], sem_gu.at[slot]),
                pltpu.make_async_copy(wd_hbm.at[layer, e], wd_in.at[slot], sem_d.at[slot]))

    @pl.when(jnp.logical_and(used, fresh))
    def _():
        slot = wslot
        mine = fetch(expert, slot)

        @pl.when(blk == 0)
        def _():
            for cp in mine:
                cp.start()

        @pl.when(nxt >= 0)
        def _():
            for cp in fetch(nxt, 1 - slot):
                cp.start()

        for cp in mine:
            cp.wait()

    bm = x_ref.shape[0]
    fresh_full = jnp.logical_and(fresh, live > bm - MOE_SUB)

    def cast_gu(j):
        wgu_bf[j] = wgu_in[wslot, :, j * ch:(j + 1) * ch].astype(BF16)

    def cast_d(j):
        wd_bf[j] = wd_in[wslot, :, j * ch:(j + 1) * ch].astype(BF16)

    @pl.when(jnp.logical_and(jnp.logical_and(used, fresh), jnp.logical_not(fresh_full)))
    def _():
        for j in range(wgu_bf.shape[0]):
            cast_gu(j)
        for j in range(wd_bf.shape[0]):
            cast_d(j)

    def run(nr, cast):
        w = x_ref[:nr, :]
        row = lax.broadcasted_iota(I32, w.shape, 0)
        w = jnp.where(row < live, w, jnp.zeros_like(w))
        lo, hi = _unpack_bf16_pairs(w)
        x = jnp.concatenate([lo.astype(BF16), hi.astype(BF16)], axis=1)
        for c in range(nc):
            c0 = slice(c * ch, (c + 1) * ch)
            c1 = slice(f + c * ch, f + (c + 1) * ch)
            if cast:
                cast_gu(c)
                cast_gu(nc + c)
            glu = _dot(x, wgu_bf[c]) + bgu_ref[0, 0, :, c0]
            lin = _dot(x, wgu_bf[nc + c]) + bgu_ref[0, 0, :, c1]
            glu = jnp.minimum(glu, SWIGLU_LIMIT)
            lin = jnp.clip(lin, -SWIGLU_LIMIT, SWIGLU_LIMIT)
            act_ref[:nr, c0] = (glu * jax.nn.sigmoid(SWIGLU_ALPHA * glu) * (lin + 1.0)).astype(BF16)
        a = act_ref[:nr, :]
        ys = []
        for n in range(wd_bf.shape[0]):
            if cast:
                cast_d(n)
            ys.append(_dot(a, wd_bf[n]))
        o_ref[:nr, :] = _pack_bf16_pairs(jnp.concatenate(ys, axis=1) + bd_ref[0, 0])

    for nr in range(MOE_SUB, bm + 1, MOE_SUB):
        @pl.when(jnp.logical_and(jnp.logical_and(live > nr - MOE_SUB, live <= nr), jnp.logical_not(fresh_full)))
        def _(nr=nr):
            run(nr, False)
            if nr < bm:
                o_ref[nr:, :] = jnp.zeros((bm - nr, o_ref.shape[1]), o_ref.dtype)

    @pl.when(fresh_full)
    def _():
        run(bm, True)

    @pl.when(jnp.logical_not(used))
    def _():
        o_ref[...] = jnp.zeros_like(o_ref)


def _expert_call(tab, buf, w_gu, b_gu, w_down, b_down, layer, bm):
    n_slots, dw = buf.shape
    _, ne, d, f2 = w_gu.shape
    f = f2 // 2
    n_blocks = n_slots // bm
    lw = lambda i, tb: (layer, tb[0, i], 0, 0)
    row = lambda i, tb: (i, 0)
    cw = MXU_COLS
    return pl.pallas_call(
        functools.partial(_expert_kernel, layer=layer),
        grid_spec=pltpu.PrefetchScalarGridSpec(
            num_scalar_prefetch=1,
            grid=(n_blocks,),
            in_specs=[pl.BlockSpec((bm, dw), row),
                      pl.BlockSpec(memory_space=pl.ANY),
                      pl.BlockSpec((1, 1, 1, f2), lw),
                      pl.BlockSpec(memory_space=pl.ANY),
                      pl.BlockSpec((1, 1, 1, d), lw)],
            out_specs=pl.BlockSpec((bm, dw), row),
            scratch_shapes=[pltpu.VMEM((2, d, f2), F32), pltpu.VMEM((2, f, d), F32),
                            pltpu.VMEM((f2 // cw, d, cw), BF16), pltpu.VMEM((d // cw, f, cw), BF16),
                            pltpu.VMEM((bm, f), BF16),
                            pltpu.SemaphoreType.DMA((2,)), pltpu.SemaphoreType.DMA((2,))]),
        out_shape=jax.ShapeDtypeStruct((n_slots, dw), jnp.uint32),
        compiler_params=_cparams(("arbitrary",)),
        name="moe_experts",
    )(tab, buf, w_gu, b_gu.reshape(b_gu.shape[0], ne, 1, f2), w_down, b_down.reshape(b_down.shape[0], ne, 1, d))


def _combined_halves(x_ref, m, y_ref, gate_ref):
    gate = gate_ref[...]
    half = y_ref.shape[2]
    f_lo, f_hi = None, None
    for k in range(TOP_K):
        lo, hi = _unpack_bf16_pairs(y_ref[k])
        gk = gate[:, k:k + 1]
        f_lo = gk * lo if f_lo is None else f_lo + gk * lo
        f_hi = gk * hi if f_hi is None else f_hi + gk * hi
    return x_ref[:, :half] + m[5:6, :half] * f_lo, x_ref[:, half:] + m[5:6, half:] * f_hi


def _combine_kernel(x_ref, m_ref, y_ref, gate_ref, gout_ref, o_ref, *, final):
    half = y_ref.shape[2]
    x_lo, x_hi = _combined_halves(x_ref, m_ref[0], y_ref, gate_ref)
    if final:
        ms = (jnp.sum(x_lo * x_lo, axis=-1, keepdims=True) + jnp.sum(x_hi * x_hi, axis=-1, keepdims=True))
        r = lax.rsqrt(ms * (0.5 / half) + NORM_EPS)
        x_lo = x_lo * r * gout_ref[:, :half]
        x_hi = x_hi * r * gout_ref[:, half:]
    o_ref[:, :half] = x_lo
    o_ref[:, half:] = x_hi


def _moe_layer(xs, routed, mods, layer, w_gu, b_gu, w_down, b_down, geo, g_out, final, defer=False):
    n, d = xs.shape
    h, idx, gate, rank, cnt = routed
    assert h.shape[0] == n
    ne = cnt.shape[0]
    n_tiles = n // TM
    grp = functools.partial(_tile_group, geo=geo)
    const2 = lambda t: (0, 0)

    bm = MOE_BM
    n_rows = n * TOP_K
    n_blocks = -(-n_rows // bm) + ne
    n_slots = n_blocks * bm
    nbp = -(-n_blocks // LANES) * LANES
    slot_t = max(w for w in range(LANES, SLOT_T_MAX + 1, LANES) if n % w == 0)
    dest, btab = pl.pallas_call(
        functools.partial(_slot_kernel, bm=bm),
        grid=(n // slot_t,),
        in_specs=[pl.BlockSpec((TOP_K, slot_t), lambda t: (0, t)),
                  pl.BlockSpec((TOP_K, slot_t), lambda t: (0, t)),
                  pl.BlockSpec((ne, 1), const2)],
        out_specs=[pl.BlockSpec((TOP_K, slot_t), lambda t: (0, t)),
                   pl.BlockSpec((8, nbp), const2)],
        out_shape=[jax.ShapeDtypeStruct((TOP_K, n), I32),
                   jax.ShapeDtypeStruct((8, nbp), I32)],
        compiler_params=_cparams(("arbitrary",)),
        name="moe_slots",
    )(idx, rank, cnt)

    buf = _sc_scatter_rows(h, dest, n_slots)
    y = _expert_call(btab, buf, w_gu, b_gu, w_down, b_down, layer, bm)
    dy = d // 2
    yk = _sc_gather_rows(y, dest.reshape(-1)).reshape(TOP_K, n, dy)
    if defer:
        return dict(x=xs, yk=yk, gate=gate, mods=mods)
    return pl.pallas_call(
        functools.partial(_combine_kernel, final=final),
        grid=(n_tiles,),
        in_specs=[pl.BlockSpec((TM, d), lambda t: (t, 0)),
                  pl.BlockSpec((1, 6, d), lambda t: (grp(t), 0, 0)),
                  pl.BlockSpec((TOP_K, TM, dy), lambda t: (0, t, 0)),
                  pl.BlockSpec((TM, SUBLANES), lambda t: (t, 0)),
                  pl.BlockSpec((1, d), const2)],
        out_specs=pl.BlockSpec((TM, d), lambda t: (t, 0)),
        out_shape=jax.ShapeDtypeStruct((n, d), F32),
        compiler_params=_cparams(("parallel",)),
        name="moe_combine",
    )(xs, mods, yk, gate, g_out.reshape(1, d))


def kernel(x, c, ctx, c_ctx, ada_w, ada_b, norm_mix, norm_ffn, norm_out, a_w_in, a_g_v, a_w_s, a_b_s, a_w_out, b_w_in, b_w_a2, b_b_a, b_g_o, b_w_out, c_w_qkv, c_rpb, c_w_out, moe_w_router, moe_b_router, moe_w_gu, moe_b_gu, moe_w_down, moe_b_down):
    b, seq, d = x.shape
    lctx = ctx.shape[1]
    depth = ada_w.shape[0]
    assert (b * lctx) % TM == 0 and seq % TM == 0 and lctx % GLA_TB == 0 and seq % GLA_TB == 0
    assert seq % (NA_QROWS * GRID_W) == 0 and (b * lctx) % (NA_QROWS * GRID_W) == 0 and lctx % 256 == 0
    assert seq // GRID_W >= NA_WROWS + NA_QROWS and b + 1 <= SUBLANES
    geo = (b * lctx // TM, seq // TM, b)
    dims = (b, seq, lctx)

    cond_t = jnp.zeros((d, SUBLANES), F32).at[:, :b].set(c.T).at[:, b].set(c_ctx)
    mods_all = _adaln(cond_t, b + 1, ada_w, ada_b)[:, :b + 1].reshape(depth, b + 1, 6, d)

    pair = (ctx.reshape(b * lctx, d), x.reshape(b * seq, d))
    n_tiles_all = (b * lctx + b * seq) // TM
    ctx_tiles = geo[0]
    has_ctx = True
    for i in range(depth):
        kind, j = i % N_MIXERS, i // N_MIXERS
        ctx_later = any(kk % N_MIXERS != 0 for kk in range(i + 1, depth))
        mods = mods_all[i]
        route_w = (norm_ffn[i], moe_w_router[i], moe_b_router[i])
        if i == 0 and not (kind == 0 and ctx_later):
            xs = jnp.concatenate(pair, axis=0)
        if kind == 0:
            keep_ctx = has_ctx and ctx_later
            skip = 0 if keep_ctx or not has_ctx else ctx_tiles
            geo_i = geo if has_ctx else (0, geo[1], b)
            src = pair if i == 0 and keep_ctx else xs
            n_src = n_tiles_all if i == 0 else (xs["x"] if isinstance(xs, dict) else xs).shape[0] // TM
            xs, routed = _gmlp_mixer(src, mods, norm_mix[i], a_w_in[j], a_g_v[j], a_w_s[j], a_b_s[j], a_w_out[j],
                                     skip, n_src - skip, geo_i, route_w)
            has_ctx = keep_ctx
        elif kind == 1:
            assert has_ctx
            if not ctx_later:
                raise NotImplementedError("GLA mixer whose context rows are dropped before the FFN")
            xs, routed = _gla_mixer(xs, mods, norm_mix[i], b_w_in[j], b_w_a2[j], b_b_a[j], b_g_o[j], b_w_out[j],
                                    geo, dims, route_w)
        else:
            assert has_ctx
            if ctx_later:
                raise NotImplementedError("context output of the neighbourhood mixer")
            xs, routed = _na_mixer(xs, mods, norm_mix[i], c_w_qkv[j], c_rpb[j], c_w_out[j], geo, dims, route_w)
            has_ctx = False
        geo_i = geo if has_ctx else (0, geo[1], b)
        nxt = (i + 1) % N_MIXERS
        defer = i + 1 < depth and (nxt != 0 or not has_ctx)
        xs = _moe_layer(xs, routed, mods, i, moe_w_gu, moe_b_gu, moe_w_down, moe_b_down, geo_i, norm_out,
                        i == depth - 1, defer)
    if has_ctx:
        xs = xs[b * lctx:]
    return xs.reshape(b, seq, d)
```
